```python
import jax, jax.numpy as jnp
from jax import lax
import numpy as np

D_MODEL = 1024
BATCH = 32
SEQ = 2048
DEPTH = 4

GRID_W = 64
CTX_LEN = 256
HEAD_DIM = 64
N_Q_HEADS = 8
N_KV_HEADS = 2
Q_GROUP = N_Q_HEADS // N_KV_HEADS
ATTN_WIDTH = N_Q_HEADS * HEAD_DIM
KV_WIDTH = N_KV_HEADS * HEAD_DIM
AXIS_DIM = HEAD_DIM // 2
ROPE_THETA = 10000.0
Q_BLOCK = 128
ATTN_SCALE = HEAD_DIM ** -0.5
CONF_WIDTH = D_MODEL // 2
CONF_KERNEL = 31
SC_WIDTH = D_MODEL // 2
SC_KERNEL = 3
N_BRANCHES = 3
N_MOD = 6
FFN_HIDDEN = -(-8 * D_MODEL // (3 * 256)) * 256
EPS = 1e-6

OFF_K = ATTN_WIDTH
OFF_V = OFF_K + KV_WIDTH
OFF_CONF = OFF_V + KV_WIDTH
OFF_SC = OFF_CONF + 2 * CONF_WIDTH
OFF_GATE = OFF_SC + 3 * SC_WIDTH
IN_WIDTH = OFF_GATE + N_BRANCHES * D_MODEL

kernel_name = 'hybrid_gqa_conformer_shortconv_dit_block'


def rms_norm(x):
    xf = x.astype(jnp.float32)
    return (xf * lax.rsqrt(jnp.mean(xf * xf, axis=-1, keepdims=True) + EPS)).astype(x.dtype)


def head_rms_norm(x, g):
    return rms_norm(x) * g


def layer_norm(x, g, b):
    xf = x.astype(jnp.float32)
    mu = jnp.mean(xf, axis=-1, keepdims=True)
    var = jnp.mean(jnp.square(xf - mu), axis=-1, keepdims=True)
    return ((xf - mu) * lax.rsqrt(var + EPS)).astype(x.dtype) * g + b


def rope_tables(seq_len):
    rows = seq_len // GRID_W
    r_ids, c_ids = jnp.meshgrid(jnp.arange(rows), jnp.arange(GRID_W), indexing='ij')
    r_ids = r_ids.reshape(-1).astype(jnp.float32)
    c_ids = c_ids.reshape(-1).astype(jnp.float32)
    freqs = ROPE_THETA ** (-jnp.arange(0, AXIS_DIM, 2, dtype=jnp.float32) / AXIS_DIM)
    ang_r = r_ids[:, None] * freqs
    ang_c = c_ids[:, None] * freqs
    return (jnp.cos(ang_r)[:, None, :], jnp.sin(ang_r)[:, None, :],
            jnp.cos(ang_c)[:, None, :], jnp.sin(ang_c)[:, None, :])


def _rotate_half(xp, cos, sin):
    x1, x2 = jnp.split(xp, 2, axis=-1)
    return jnp.concatenate([x1 * cos - x2 * sin, x1 * sin + x2 * cos], axis=-1)


def apply_rope_2d(x, tabs):
    cos_r, sin_r, cos_c, sin_c = tabs
    xf = x.astype(jnp.float32)
    out = jnp.concatenate([_rotate_half(xf[..., :AXIS_DIM], cos_r, sin_r),
                           _rotate_half(xf[..., AXIS_DIM:], cos_c, sin_c)], axis=-1)
    return out.astype(x.dtype)


def depthwise_conv(x, w):
    k = w.shape[0]
    return lax.conv_general_dilated(x, w[:, None, :].astype(x.dtype), window_strides=(1,),
                                    padding=[(k // 2, k // 2)],
                                    dimension_numbers=('NWC', 'WIO', 'NWC'),
                                    feature_group_count=x.shape[-1])


def latent_attention(q, k_all, v_all):
    b, s = q.shape[0], q.shape[1]
    n_blk = s // Q_BLOCK
    qb = q.reshape(b, n_blk, Q_BLOCK, N_KV_HEADS, Q_GROUP, HEAD_DIM).transpose(1, 0, 2, 3, 4, 5)

    def one_block(qi):
        sc = jnp.einsum('bqhgd,bkhd->bhgqk', qi, k_all).astype(jnp.float32) * ATTN_SCALE
        p = jax.nn.softmax(sc, axis=-1).astype(v_all.dtype)
        return jnp.einsum('bhgqk,bkhd->bqhgd', p, v_all)

    o = lax.map(one_block, qb)
    return o.transpose(1, 0, 2, 3, 4, 5).reshape(b, s, ATTN_WIDTH)


def context_attention(qc, kc, vc):
    b, l = qc.shape[0], qc.shape[1]
    qg = qc.reshape(b, l, N_KV_HEADS, Q_GROUP, HEAD_DIM)
    sc = jnp.einsum('bqhgd,bkhd->bhgqk', qg, kc).astype(jnp.float32) * ATTN_SCALE
    p = jax.nn.softmax(sc, axis=-1).astype(vc.dtype)
    return jnp.einsum('bhgqk,bkhd->bqhgd', p, vc).reshape(b, l, ATTN_WIDTH)


def conformer_branch(u, dw_w, dw_b, ln_g, ln_b, w_out):
    a, g = jnp.split(u, 2, axis=-1)
    h = a * jax.nn.sigmoid(g)
    h = depthwise_conv(h, dw_w) + dw_b
    h = jax.nn.silu(layer_norm(h, ln_g, ln_b))
    return h @ w_out


def shortconv_branch(u, dw_w, w_out):
    bg, cg, xs = jnp.split(u, 3, axis=-1)
    return (bg * depthwise_conv(cg * xs, dw_w)) @ w_out


def merge_branches(attn_heads, conf_u, sc_u, gate_logits, w_attn_o, conf_dw_w, conf_dw_b,
                   conf_ln_g, conf_ln_b, w_conf_out, sc_dw_w, w_sc_out, w_mix_out):
    y_attn = attn_heads @ w_attn_o
    y_conf = conformer_branch(conf_u, conf_dw_w, conf_dw_b, conf_ln_g, conf_ln_b, w_conf_out)
    y_sc = shortconv_branch(sc_u, sc_dw_w, w_sc_out)
    g = jax.nn.sigmoid(gate_logits.reshape(gate_logits.shape[:-1] + (N_BRANCHES, D_MODEL)))
    merged = g[..., 0, :] * y_attn + g[..., 1, :] * y_conf + g[..., 2, :] * y_sc
    return merged @ w_mix_out


def swiglu(h, w_in, w_out):
    a, b = jnp.split(h @ w_in, 2, axis=-1)
    return (jax.nn.silu(a) * b) @ w_out


def _fwd_setup_inputs(seed: int = 0) -> dict:
    key = jax.random.key(seed)
    ks = jax.random.split(key, 24)
    f32 = jnp.float32
    L, D = DEPTH, D_MODEL

    def nrm(k, shape, scale):
        return jax.random.normal(k, shape, f32) * scale

    return {
        'x': nrm(ks[0], (BATCH, SEQ, D), 1.0),
        'c': nrm(ks[1], (BATCH, D), 1.0),
        'ctx': nrm(ks[2], (BATCH, CTX_LEN, D), 1.0),
        'c_ctx': nrm(ks[3], (D,), 1.0),
        'w_ada': nrm(ks[4], (L, D, N_MOD * D), 0.5 * D ** -0.5),
        'b_ada': nrm(ks[5], (L, N_MOD * D), 0.02),
        'w_in': nrm(ks[6], (L, D, IN_WIDTH), D ** -0.5),
        'q_norm': 1.0 + nrm(ks[7], (L, HEAD_DIM), 0.02),
        'k_norm': 1.0 + nrm(ks[8], (L, HEAD_DIM), 0.02),
        'w_attn_o': nrm(ks[9], (L, ATTN_WIDTH, D), ATTN_WIDTH ** -0.5),
        'conf_dw_w': nrm(ks[10], (L, CONF_KERNEL, CONF_WIDTH), CONF_KERNEL ** -0.5),
        'conf_dw_b': nrm(ks[11], (L, CONF_WIDTH), 0.02),
        'conf_ln_g': 1.0 + nrm(ks[12], (L, CONF_WIDTH), 0.02),
        'conf_ln_b': nrm(ks[13], (L, CONF_WIDTH), 0.02),
        'w_conf_out': nrm(ks[14], (L, CONF_WIDTH, D), CONF_WIDTH ** -0.5),
        'sc_dw_w': nrm(ks[15], (L, SC_KERNEL, SC_WIDTH), SC_KERNEL ** -0.5),
        'w_sc_out': nrm(ks[16], (L, SC_WIDTH, D), SC_WIDTH ** -0.5),
        'w_mix_out': nrm(ks[17], (L, D, D), D ** -0.5),
        'w_ffn_in': nrm(ks[18], (L, D, 2 * FFN_HIDDEN), D ** -0.5),
        'w_ffn_out': nrm(ks[19], (L, FFN_HIDDEN, D), FFN_HIDDEN ** -0.5),
    }


def _fwd_reference(x, c, ctx, c_ctx, w_ada, b_ada, w_in, q_norm, k_norm, w_attn_o, conf_dw_w, conf_dw_b,
              conf_ln_g, conf_ln_b, w_conf_out, sc_dw_w, w_sc_out, w_mix_out, w_ffn_in, w_ffn_out):
    b, s = x.shape[0], x.shape[1]
    lc = ctx.shape[1]
    rope = rope_tables(s)
    for i in range(DEPTH):
        last = i == DEPTH - 1
        wi = w_in[i]
        branch_w = (w_attn_o[i], conf_dw_w[i], conf_dw_b[i], conf_ln_g[i], conf_ln_b[i],
                    w_conf_out[i], sc_dw_w[i], w_sc_out[i], w_mix_out[i])
        mod = (jax.nn.silu(c) @ w_ada[i] + b_ada[i]).reshape(b, N_MOD, 1, D_MODEL)
        modc = (jax.nn.silu(c_ctx) @ w_ada[i] + b_ada[i]).reshape(N_MOD, D_MODEL)

        hc = rms_norm(ctx) * (1 + modc[1]) + modc[0]
        kvc = hc @ wi[:, OFF_K:OFF_CONF]
        kc = head_rms_norm(kvc[..., :KV_WIDTH].reshape(b, lc, N_KV_HEADS, HEAD_DIM), k_norm[i])
        vc = kvc[..., KV_WIDTH:].reshape(b, lc, N_KV_HEADS, HEAD_DIM)

        h = rms_norm(x) * (1 + mod[:, 1]) + mod[:, 0]
        p = h @ wi
        q = apply_rope_2d(head_rms_norm(p[..., :OFF_K].reshape(b, s, N_Q_HEADS, HEAD_DIM), q_norm[i]), rope)
        k = apply_rope_2d(head_rms_norm(p[..., OFF_K:OFF_V].reshape(b, s, N_KV_HEADS, HEAD_DIM), k_norm[i]), rope)
        v = p[..., OFF_V:OFF_CONF].reshape(b, s, N_KV_HEADS, HEAD_DIM)
        attn = latent_attention(q, jnp.concatenate([kc, k], axis=1), jnp.concatenate([vc, v], axis=1))
        mixed = merge_branches(attn, p[..., OFF_CONF:OFF_SC], p[..., OFF_SC:OFF_GATE], p[..., OFF_GATE:], *branch_w)
        x_new = x + mod[:, 2] * mixed
        h2 = rms_norm(x_new) * (1 + mod[:, 4]) + mod[:, 3]
        x_new = x_new + mod[:, 5] * swiglu(h2, w_ffn_in[i], w_ffn_out[i])

        if not last:
            qc = head_rms_norm((hc @ wi[:, :OFF_K]).reshape(b, lc, N_Q_HEADS, HEAD_DIM), q_norm[i])
            pc = hc @ wi[:, OFF_CONF:]
            attn_c = context_attention(qc, kc, vc)
            mixed_c = merge_branches(attn_c, pc[..., :OFF_SC - OFF_CONF], pc[..., OFF_SC - OFF_CONF:OFF_GATE - OFF_CONF],
                                     pc[..., OFF_GATE - OFF_CONF:], *branch_w)
            ctx = ctx + modc[2] * mixed_c
            h2c = rms_norm(ctx) * (1 + modc[4]) + modc[3]
            ctx = ctx + modc[5] * swiglu(h2c, w_ffn_in[i], w_ffn_out[i])
        x = x_new
    return x


import jax as _jax
import jax.numpy as _jnp

TWIN_FORMAT = 'train_step'
FWD_PARAMS = ['x', 'c', 'ctx', 'c_ctx', 'w_ada', 'b_ada', 'w_in', 'q_norm', 'k_norm', 'w_attn_o', 'conf_dw_w', 'conf_dw_b', 'conf_ln_g', 'conf_ln_b', 'w_conf_out', 'sc_dw_w', 'w_sc_out', 'w_mix_out', 'w_ffn_in', 'w_ffn_out']
TWIN_WEIGHTS = ['c_ctx', 'w_ada', 'b_ada', 'w_in', 'q_norm', 'k_norm', 'w_attn_o', 'conf_dw_w', 'conf_dw_b', 'conf_ln_g', 'conf_ln_b', 'w_conf_out', 'sc_dw_w', 'w_sc_out', 'w_mix_out', 'w_ffn_in', 'w_ffn_out']
TWIN_DIFF_INPUT = 'x'
TWIN_INPUTS = ['x', 'c', 'ctx', 'c_ctx', 'w_ada', 'b_ada', 'w_in', 'q_norm', 'k_norm', 'w_attn_o', 'conf_dw_w', 'conf_dw_b', 'conf_ln_g', 'conf_ln_b', 'w_conf_out', 'sc_dw_w', 'w_sc_out', 'w_mix_out', 'w_ffn_in', 'w_ffn_out', 'loss_target', 'm_c_ctx', 'm_w_ada', 'm_b_ada', 'm_w_in', 'm_q_norm', 'm_k_norm', 'm_w_attn_o', 'm_conf_dw_w', 'm_conf_dw_b', 'm_conf_ln_g', 'm_conf_ln_b', 'm_w_conf_out', 'm_sc_dw_w', 'm_w_sc_out', 'm_w_mix_out', 'm_w_ffn_in', 'm_w_ffn_out', 'v_c_ctx', 'v_w_ada', 'v_b_ada', 'v_w_in', 'v_q_norm', 'v_k_norm', 'v_w_attn_o', 'v_conf_dw_w', 'v_conf_dw_b', 'v_conf_ln_g', 'v_conf_ln_b', 'v_w_conf_out', 'v_sc_dw_w', 'v_w_sc_out', 'v_w_mix_out', 'v_w_ffn_in', 'v_w_ffn_out']
TWIN_OUTPUTS = ['loss', 'grad_x', 'grad_c_ctx', 'grad_w_ada', 'grad_b_ada', 'grad_w_in', 'grad_q_norm', 'grad_k_norm', 'grad_w_attn_o', 'grad_conf_dw_w', 'grad_conf_dw_b', 'grad_conf_ln_g', 'grad_conf_ln_b', 'grad_w_conf_out', 'grad_sc_dw_w', 'grad_w_sc_out', 'grad_w_mix_out', 'grad_w_ffn_in', 'grad_w_ffn_out', 'delta_c_ctx', 'delta_w_ada', 'delta_b_ada', 'delta_w_in', 'delta_q_norm', 'delta_k_norm', 'delta_w_attn_o', 'delta_conf_dw_w', 'delta_conf_dw_b', 'delta_conf_ln_g', 'delta_conf_ln_b', 'delta_w_conf_out', 'delta_sc_dw_w', 'delta_w_sc_out', 'delta_w_mix_out', 'delta_w_ffn_in', 'delta_w_ffn_out', 'new_m_c_ctx', 'new_m_w_ada', 'new_m_b_ada', 'new_m_w_in', 'new_m_q_norm', 'new_m_k_norm', 'new_m_w_attn_o', 'new_m_conf_dw_w', 'new_m_conf_dw_b', 'new_m_conf_ln_g', 'new_m_conf_ln_b', 'new_m_w_conf_out', 'new_m_sc_dw_w', 'new_m_w_sc_out', 'new_m_w_mix_out', 'new_m_w_ffn_in', 'new_m_w_ffn_out', 'new_v_c_ctx', 'new_v_w_ada', 'new_v_b_ada', 'new_v_w_in', 'new_v_q_norm', 'new_v_k_norm', 'new_v_w_attn_o', 'new_v_conf_dw_w', 'new_v_conf_dw_b', 'new_v_conf_ln_g', 'new_v_conf_ln_b', 'new_v_w_conf_out', 'new_v_sc_dw_w', 'new_v_w_sc_out', 'new_v_w_mix_out', 'new_v_w_ffn_in', 'new_v_w_ffn_out']
TWIN_LEAF_KINDS = {'loss': 'loss', 'grad_x': 'grad_x', 'grad_c_ctx': 'grad_w', 'grad_w_ada': 'grad_w', 'grad_b_ada': 'grad_w', 'grad_w_in': 'grad_w', 'grad_q_norm': 'grad_w', 'grad_k_norm': 'grad_w', 'grad_w_attn_o': 'grad_w', 'grad_conf_dw_w': 'grad_w', 'grad_conf_dw_b': 'grad_w', 'grad_conf_ln_g': 'grad_w', 'grad_conf_ln_b': 'grad_w', 'grad_w_conf_out': 'grad_w', 'grad_sc_dw_w': 'grad_w', 'grad_w_sc_out': 'grad_w', 'grad_w_mix_out': 'grad_w', 'grad_w_ffn_in': 'grad_w', 'grad_w_ffn_out': 'grad_w', 'delta_c_ctx': 'delta_w', 'delta_w_ada': 'delta_w', 'delta_b_ada': 'delta_w', 'delta_w_in': 'delta_w', 'delta_q_norm': 'delta_w', 'delta_k_norm': 'delta_w', 'delta_w_attn_o': 'delta_w', 'delta_conf_dw_w': 'delta_w', 'delta_conf_dw_b': 'delta_w', 'delta_conf_ln_g': 'delta_w', 'delta_conf_ln_b': 'delta_w', 'delta_w_conf_out': 'delta_w', 'delta_sc_dw_w': 'delta_w', 'delta_w_sc_out': 'delta_w', 'delta_w_mix_out': 'delta_w', 'delta_w_ffn_in': 'delta_w', 'delta_w_ffn_out': 'delta_w', 'new_m_c_ctx': 'new_m', 'new_m_w_ada': 'new_m', 'new_m_b_ada': 'new_m', 'new_m_w_in': 'new_m', 'new_m_q_norm': 'new_m', 'new_m_k_norm': 'new_m', 'new_m_w_attn_o': 'new_m', 'new_m_conf_dw_w': 'new_m', 'new_m_conf_dw_b': 'new_m', 'new_m_conf_ln_g': 'new_m', 'new_m_conf_ln_b': 'new_m', 'new_m_w_conf_out': 'new_m', 'new_m_sc_dw_w': 'new_m', 'new_m_w_sc_out': 'new_m', 'new_m_w_mix_out': 'new_m', 'new_m_w_ffn_in': 'new_m', 'new_m_w_ffn_out': 'new_m', 'new_v_c_ctx': 'new_v', 'new_v_w_ada': 'new_v', 'new_v_b_ada': 'new_v', 'new_v_w_in': 'new_v', 'new_v_q_norm': 'new_v', 'new_v_k_norm': 'new_v', 'new_v_w_attn_o': 'new_v', 'new_v_conf_dw_w': 'new_v', 'new_v_conf_dw_b': 'new_v', 'new_v_conf_ln_g': 'new_v', 'new_v_conf_ln_b': 'new_v', 'new_v_w_conf_out': 'new_v', 'new_v_sc_dw_w': 'new_v', 'new_v_w_sc_out': 'new_v', 'new_v_w_mix_out': 'new_v', 'new_v_w_ffn_in': 'new_v', 'new_v_w_ffn_out': 'new_v'}


def _forward(args):
    return _fwd_reference(*[args[k] for k in FWD_PARAMS])


def _output_shape():
    out = _jax.eval_shape(lambda: _forward(_fwd_setup_inputs(0)))
    return out.shape, out.dtype

N_MICROBATCH = 1
ADAM_LR = 0.001
ADAM_B1 = 0.9
ADAM_B2 = 0.999
ADAM_EPS = 1e-08
ADAM_WD = 0.01
ADAM_STEP = 10
PER_EXAMPLE_BATCH_AXIS = {'x': 0, 'c': 0, 'ctx': 0, 'loss_target': 0}
SHARED_INPUTS = []
_WEIGHT_DTYPES = {'c_ctx': _jnp.float32, 'w_ada': _jnp.float32, 'b_ada': _jnp.float32, 'w_in': _jnp.float32, 'q_norm': _jnp.float32, 'k_norm': _jnp.float32, 'w_attn_o': _jnp.float32, 'conf_dw_w': _jnp.float32, 'conf_dw_b': _jnp.float32, 'conf_ln_g': _jnp.float32, 'conf_ln_b': _jnp.float32, 'w_conf_out': _jnp.float32, 'sc_dw_w': _jnp.float32, 'w_sc_out': _jnp.float32, 'w_mix_out': _jnp.float32, 'w_ffn_in': _jnp.float32, 'w_ffn_out': _jnp.float32}
MOMENT_SCALE = {'c_ctx': 8.674151e-02, 'w_ada': 2.469135e+00, 'b_ada': 5.575368e+00, 'w_in': 1.611897e-01, 'q_norm': 3.898954e-02, 'k_norm': 3.903040e-02, 'w_attn_o': 2.417811e-01, 'conf_dw_w': 1.384317e-01, 'conf_dw_b': 1.010164e+00, 'conf_ln_g': 1.608024e+00, 'conf_ln_b': 1.126621e+00, 'w_conf_out': 1.626406e-01, 'sc_dw_w': 3.339186e+00, 'w_sc_out': 1.621205e-01, 'w_mix_out': 2.439330e-01, 'w_ffn_in': 1.041633e-01, 'w_ffn_out': 1.462103e-01}


def _to_microbatches(a, axis):
    t = _jnp.moveaxis(a, axis, 0)
    t = t.reshape((N_MICROBATCH, t.shape[0] // N_MICROBATCH) + t.shape[1:])
    return _jnp.moveaxis(t, 1, axis + 1)


def setup_inputs(seed: int = 0) -> dict:
    inp = _fwd_setup_inputs(seed)
    key = _jax.random.fold_in(_jax.random.key(seed), 7919)
    shape, _ = _output_shape()
    out = dict(inp)
    out["loss_target"] = _jax.random.normal(_jax.random.fold_in(key, 0), shape, _jnp.float32)
    for i, name in enumerate(TWIN_WEIGHTS):
        w = inp[name].astype(_jnp.float32)
        if MOMENT_SCALE is None:
            s = _jnp.sqrt(_jnp.mean(_jnp.square(w)) + 1e-30)
        else:
            s = MOMENT_SCALE[name]
        km, kv = _jax.random.split(_jax.random.fold_in(key, i + 1))
        out[name] = w
        out["m_" + name] = s * _jax.random.normal(km, w.shape, _jnp.float32)
        out["v_" + name] = (s * s) * _jax.random.uniform(kv, w.shape, _jnp.float32, 0.5, 1.5)
    if N_MICROBATCH > 1:
        for name, axis in PER_EXAMPLE_BATCH_AXIS.items():
            out[name] = _to_microbatches(out[name], axis)
    return {'x': out['x'], 'c': out['c'], 'ctx': out['ctx'], 'c_ctx': out['c_ctx'], 'w_ada': out['w_ada'], 'b_ada': out['b_ada'], 'w_in': out['w_in'], 'q_norm': out['q_norm'], 'k_norm': out['k_norm'], 'w_attn_o': out['w_attn_o'], 'conf_dw_w': out['conf_dw_w'], 'conf_dw_b': out['conf_dw_b'], 'conf_ln_g': out['conf_ln_g'], 'conf_ln_b': out['conf_ln_b'], 'w_conf_out': out['w_conf_out'], 'sc_dw_w': out['sc_dw_w'], 'w_sc_out': out['w_sc_out'], 'w_mix_out': out['w_mix_out'], 'w_ffn_in': out['w_ffn_in'], 'w_ffn_out': out['w_ffn_out'], 'loss_target': out['loss_target'], 'm_c_ctx': out['m_c_ctx'], 'm_w_ada': out['m_w_ada'], 'm_b_ada': out['m_b_ada'], 'm_w_in': out['m_w_in'], 'm_q_norm': out['m_q_norm'], 'm_k_norm': out['m_k_norm'], 'm_w_attn_o': out['m_w_attn_o'], 'm_conf_dw_w': out['m_conf_dw_w'], 'm_conf_dw_b': out['m_conf_dw_b'], 'm_conf_ln_g': out['m_conf_ln_g'], 'm_conf_ln_b': out['m_conf_ln_b'], 'm_w_conf_out': out['m_w_conf_out'], 'm_sc_dw_w': out['m_sc_dw_w'], 'm_w_sc_out': out['m_w_sc_out'], 'm_w_mix_out': out['m_w_mix_out'], 'm_w_ffn_in': out['m_w_ffn_in'], 'm_w_ffn_out': out['m_w_ffn_out'], 'v_c_ctx': out['v_c_ctx'], 'v_w_ada': out['v_w_ada'], 'v_b_ada': out['v_b_ada'], 'v_w_in': out['v_w_in'], 'v_q_norm': out['v_q_norm'], 'v_k_norm': out['v_k_norm'], 'v_w_attn_o': out['v_w_attn_o'], 'v_conf_dw_w': out['v_conf_dw_w'], 'v_conf_dw_b': out['v_conf_dw_b'], 'v_conf_ln_g': out['v_conf_ln_g'], 'v_conf_ln_b': out['v_conf_ln_b'], 'v_w_conf_out': out['v_w_conf_out'], 'v_sc_dw_w': out['v_sc_dw_w'], 'v_w_sc_out': out['v_w_sc_out'], 'v_w_mix_out': out['v_w_mix_out'], 'v_w_ffn_in': out['v_w_ffn_in'], 'v_w_ffn_out': out['v_w_ffn_out']}


def _loss(weights, diff, rest, loss_target):
    with _jax.named_scope("forward"):
        args = {**rest, TWIN_DIFF_INPUT: diff, **{k: w.astype(_WEIGHT_DTYPES[k]) for k, w in weights.items()}}
        y = _forward(args)
    with _jax.named_scope("loss_head"):
        err = _jnp.square(y.astype(_jnp.float32) - loss_target)
        return 0.5 * _jnp.sum(_jnp.mean(err, axis=-1)) if err.ndim else 0.5 * err


def _adamw(w, g, m, v):
    m = ADAM_B1 * m + (1.0 - ADAM_B1) * g
    v = ADAM_B2 * v + (1.0 - ADAM_B2) * _jnp.square(g)
    m_hat = m / (1.0 - ADAM_B1 ** ADAM_STEP)
    v_hat = v / (1.0 - ADAM_B2 ** ADAM_STEP)
    delta = -ADAM_LR * (m_hat / (_jnp.sqrt(v_hat) + ADAM_EPS) + ADAM_WD * w)
    return delta, m, v


def reference(x, c, ctx, c_ctx, w_ada, b_ada, w_in, q_norm, k_norm, w_attn_o, conf_dw_w, conf_dw_b, conf_ln_g, conf_ln_b, w_conf_out, sc_dw_w, w_sc_out, w_mix_out, w_ffn_in, w_ffn_out, loss_target, m_c_ctx, m_w_ada, m_b_ada, m_w_in, m_q_norm, m_k_norm, m_w_attn_o, m_conf_dw_w, m_conf_dw_b, m_conf_ln_g, m_conf_ln_b, m_w_conf_out, m_sc_dw_w, m_w_sc_out, m_w_mix_out, m_w_ffn_in, m_w_ffn_out, v_c_ctx, v_w_ada, v_b_ada, v_w_in, v_q_norm, v_k_norm, v_w_attn_o, v_conf_dw_w, v_conf_dw_b, v_conf_ln_g, v_conf_ln_b, v_w_conf_out, v_sc_dw_w, v_w_sc_out, v_w_mix_out, v_w_ffn_in, v_w_ffn_out):
    given = dict(x=x, c=c, ctx=ctx, c_ctx=c_ctx, w_ada=w_ada, b_ada=b_ada, w_in=w_in, q_norm=q_norm, k_norm=k_norm, w_attn_o=w_attn_o, conf_dw_w=conf_dw_w, conf_dw_b=conf_dw_b, conf_ln_g=conf_ln_g, conf_ln_b=conf_ln_b, w_conf_out=w_conf_out, sc_dw_w=sc_dw_w, w_sc_out=w_sc_out, w_mix_out=w_mix_out, w_ffn_in=w_ffn_in, w_ffn_out=w_ffn_out, loss_target=loss_target, m_c_ctx=m_c_ctx, m_w_ada=m_w_ada, m_b_ada=m_b_ada, m_w_in=m_w_in, m_q_norm=m_q_norm, m_k_norm=m_k_norm, m_w_attn_o=m_w_attn_o, m_conf_dw_w=m_conf_dw_w, m_conf_dw_b=m_conf_dw_b, m_conf_ln_g=m_conf_ln_g, m_conf_ln_b=m_conf_ln_b, m_w_conf_out=m_w_conf_out, m_sc_dw_w=m_sc_dw_w, m_w_sc_out=m_w_sc_out, m_w_mix_out=m_w_mix_out, m_w_ffn_in=m_w_ffn_in, m_w_ffn_out=m_w_ffn_out, v_c_ctx=v_c_ctx, v_w_ada=v_w_ada, v_b_ada=v_b_ada, v_w_in=v_w_in, v_q_norm=v_q_norm, v_k_norm=v_k_norm, v_w_attn_o=v_w_attn_o, v_conf_dw_w=v_conf_dw_w, v_conf_dw_b=v_conf_dw_b, v_conf_ln_g=v_conf_ln_g, v_conf_ln_b=v_conf_ln_b, v_w_conf_out=v_w_conf_out, v_sc_dw_w=v_sc_dw_w, v_w_sc_out=v_w_sc_out, v_w_mix_out=v_w_mix_out, v_w_ffn_in=v_w_ffn_in, v_w_ffn_out=v_w_ffn_out)
    weights = {n: given[n] for n in TWIN_WEIGHTS}
    shared = {n: given[n] for n in SHARED_INPUTS}
    per_example = {n: given[n] for n in ['x', 'c', 'ctx']}
    grad_fn = _jax.value_and_grad(_loss, argnums=(0, 1))

    def one_microbatch(ex, loss_target):
        ex = dict(ex)
        diff = ex.pop(TWIN_DIFF_INPUT)
        return grad_fn(weights, diff, {**shared, **ex}, loss_target)

    if N_MICROBATCH == 1:
        loss, (grad_w, grad_x) = one_microbatch(per_example, given["loss_target"])
    else:
        def body(carry, xs):
            loss_sum, grad_sum = carry
            l_k, (gw_k, gx_k) = one_microbatch(xs[0], xs[1])
            with _jax.named_scope("update"):
                return (loss_sum + l_k, _jax.tree.map(_jnp.add, grad_sum, gw_k)), gx_k

        init = (_jnp.zeros((), _jnp.float32), _jax.tree.map(_jnp.zeros_like, weights))
        (loss, grad_w), grad_x = _jax.lax.scan(body, init, (per_example, given["loss_target"]))
    with _jax.named_scope("update"):
        delta_w, new_m, new_v = {}, {}, {}
        for n in TWIN_WEIGHTS:
            delta_w[n], new_m[n], new_v[n] = _adamw(weights[n], grad_w[n], given["m_" + n], given["v_" + n])
    return (loss, grad_x, *[grad_w[n] for n in TWIN_WEIGHTS], *[delta_w[n] for n in TWIN_WEIGHTS],
            *[new_m[n] for n in TWIN_WEIGHTS], *[new_v[n] for n in TWIN_WEIGHTS])
```

```python
import functools

import jax
import jax.numpy as jnp
from jax import lax
from jax.experimental import pallas as pl
from jax.experimental.pallas import tpu as pltpu

F32, BF16 = jnp.float32, jnp.bfloat16
HIGHEST = lax.Precision.HIGHEST

D = 1024
SEQ = 2048
CTX = 256
DEPTH = 4
BL = 4
GRID_W = 64
HD = 64
NQ = 8
NKV = 2
AW = NQ * HD
KVW = NKV * HD
CW = D // 2
CONF_K = 31
SC_K = 3
NMOD = 6
FH = -(-8 * D // (3 * 256)) * 256
EPS = 1e-6
ROPE_THETA = 10000.0
ATTN_SCALE = HD ** -0.5
OFF_K = AW
OFF_V = OFF_K + KVW
OFF_CONF = OFF_V + KVW
OFF_SC = OFF_CONF + 2 * CW
OFF_GATE = OFF_SC + 3 * CW
IN_W = OFF_GATE + 3 * D
QKVW = OFF_CONF
NCHIP = 4

ADAM_LR, ADAM_B1, ADAM_B2, ADAM_EPS, ADAM_WD, ADAM_STEP = 0.001, 0.9, 0.999, 1e-08, 0.01, 10

TM = CTX
RE = CTX + SEQ
TPE = RE // TM
NROW = BL * RE
NT = NROW // TM
LANE = 128
CB = CW // LANE
CONV_CH = 128
PADR = 16
VMEM_LIMIT = 52 * 1024 * 1024
HEAD_PERM = (0, 4, 1, 5, 2, 6, 3, 7)
HEAD_INV = (0, 2, 4, 6, 1, 3, 5, 7)

MESH = pl.DeviceIdType.MESH
ANY = pl.BlockSpec(memory_space=pl.ANY)


def _pcall(body, *, name, out_shape, grid=(), in_specs=None, out_specs=None, scratch=(), sem=None):
    if not grid:
        return pl.pallas_call(body, name=name, out_shape=out_shape)
    params = pltpu.CompilerParams(dimension_semantics=sem, vmem_limit_bytes=VMEM_LIMIT)
    return pl.pallas_call(body, name=name, out_shape=out_shape, grid=grid, in_specs=in_specs, out_specs=out_specs,
                          scratch_shapes=list(scratch), compiler_params=params)


def _pick(n, cands):
    for t in cands:
        if n % t == 0:
            return t
    return n


def _seg(t):
    return jnp.where(t % TPE == 0, BL, t // TPE)


def _slot(t):
    return 2 * (t // TPE) + jnp.where(t % TPE == 0, 0, 1)


def _sigmoid(x):
    return 1.0 / (1.0 + jnp.exp(-x))


def mm(a, b, *, ta=False, tb=False, bias=None, acc=None, out_dtype=F32, name):
    if ta:
        rows, ka = a.shape
        nb = b.shape[1]
        tr = _pick(rows, (512, 256, 128, 8))
        tk = ka if ka <= 1024 else _pick(ka, (1024, 512, 256, 128))
        tn = _pick(nb, (512, 640, 384, 256, 128))

        def body(a_ref, b_ref, o_ref):
            r = pl.program_id(2)

            @pl.when(r == 0)
            def _():
                o_ref[...] = jnp.zeros_like(o_ref)

            o_ref[...] += lax.dot_general(a_ref[...].astype(BF16), b_ref[...].astype(BF16),
                                          (((0,), (0,)), ((), ())), preferred_element_type=F32)

        return _pcall(body, name=name, out_shape=jax.ShapeDtypeStruct((ka, nb), F32),
                      grid=(ka // tk, nb // tn, rows // tr),
                      in_specs=[pl.BlockSpec((tr, tk), lambda i, j, r: (r, i)),
                                pl.BlockSpec((tr, tn), lambda i, j, r: (r, j))],
                      out_specs=pl.BlockSpec((tk, tn), lambda i, j, r: (i, j)),
                      sem=("parallel", "parallel", "arbitrary"))(a, b)

    m, k = a.shape
    nb = b.shape[0] if tb else b.shape[1]
    tm = _pick(m, (512, 256, 128, 8))
    tn = _pick(nb, (512, 640, 384, 256, 128))
    dims = (((1,), (1,)), ((), ())) if tb else (((1,), (0,)), ((), ()))
    has_bias, has_acc = bias is not None, acc is not None

    def body(*refs):
        a_ref, b_ref = refs[0], refs[1]
        o_ref = refs[-1]
        out = lax.dot_general(a_ref[...].astype(BF16), b_ref[...].astype(BF16), dims, preferred_element_type=F32)
        pos = 2
        if has_bias:
            out = out + refs[pos][...]
            pos += 1
        if has_acc:
            out = out + refs[pos][...]
        o_ref[...] = out.astype(out_dtype)

    in_specs = [pl.BlockSpec((tm, k), lambda i, j: (i, 0)),
                pl.BlockSpec((tn, k), lambda i, j: (j, 0)) if tb else pl.BlockSpec((k, tn), lambda i, j: (0, j))]
    args = [a, b]
    if has_bias:
        in_specs.append(pl.BlockSpec((1, tn), lambda i, j: (0, j)))
        args.append(bias)
    if has_acc:
        in_specs.append(pl.BlockSpec((tm, tn), lambda i, j: (i, j)))
        args.append(acc)
    return _pcall(body, name=name, out_shape=jax.ShapeDtypeStruct((m, nb), out_dtype),
                  grid=(m // tm, nb // tn), in_specs=in_specs,
                  out_specs=pl.BlockSpec((tm, tn), lambda i, j: (i, j)),
                  sem=("parallel", "parallel"))(*args)


def _mods_spec():
    return pl.BlockSpec((1, 1, NMOD * D), lambda t: (_seg(t), 0, 0))


def _rows(width):
    return pl.BlockSpec((TM, width), lambda t: (t, 0))


def norm_mod_fwd(x, mods, k_sh, k_sc, name):
    def body(x_ref, m_ref, h_ref):
        x = x_ref[...]
        r = lax.rsqrt(jnp.mean(x * x, axis=-1, keepdims=True) + EPS)
        sh = m_ref[0, :, k_sh * D:(k_sh + 1) * D]
        sc = m_ref[0, :, k_sc * D:(k_sc + 1) * D]
        h_ref[...] = (x * r * (1.0 + sc) + sh).astype(BF16)

    return _pcall(body, name=name, out_shape=jax.ShapeDtypeStruct((NROW, D), BF16), grid=(NT,),
                  in_specs=[_rows(D), _mods_spec()], out_specs=_rows(D), sem=("parallel",))(x, mods)


def _accumulate_slot(t, ref, part):
    first = (t % TPE) <= 1

    @pl.when(first)
    def _():
        ref[0] = part

    @pl.when(jnp.logical_not(first))
    def _():
        ref[0] += part


def norm_mod_bwd(x, mods, dh, dres, k_sc, name):
    def body(x_ref, m_ref, dh_ref, dres_ref, dx_ref, dp_ref):
        t = pl.program_id(0)
        x = x_ref[...]
        r = lax.rsqrt(jnp.mean(x * x, axis=-1, keepdims=True) + EPS)
        xn = x * r
        sc = m_ref[0, :, k_sc * D:(k_sc + 1) * D]
        dh = dh_ref[...]
        dxn = dh * (1.0 + sc)
        dx_ref[...] = r * (dxn - xn * jnp.mean(dxn * xn, axis=-1, keepdims=True)) + dres_ref[...]
        part = jnp.concatenate([jnp.sum(dh, axis=0, keepdims=True), jnp.sum(dh * xn, axis=0, keepdims=True)], axis=1)
        _accumulate_slot(t, dp_ref, part)

    return _pcall(body, name=name,
                  out_shape=(jax.ShapeDtypeStruct((NROW, D), F32), jax.ShapeDtypeStruct((2 * BL, 1, 2 * D), F32)),
                  grid=(NT,), in_specs=[_rows(D), _mods_spec(), _rows(D), _rows(D)],
                  out_specs=(_rows(D), pl.BlockSpec((1, 1, 2 * D), lambda t: (_slot(t), 0, 0))),
                  sem=("arbitrary",))(x, mods, dh, dres)


def gate_resid_fwd(x, y, mods, k_g, name):
    def body(x_ref, y_ref, m_ref, o_ref):
        o_ref[...] = x_ref[...] + m_ref[0, :, k_g * D:(k_g + 1) * D] * y_ref[...]

    return _pcall(body, name=name, out_shape=jax.ShapeDtypeStruct((NROW, D), F32), grid=(NT,),
                  in_specs=[_rows(D), _rows(D), _mods_spec()], out_specs=_rows(D), sem=("parallel",))(x, y, mods)


def gate_resid_bwd(dx, y, mods, k_g, name):
    def body(dx_ref, y_ref, m_ref, dy_ref, dp_ref):
        t = pl.program_id(0)
        dx = dx_ref[...]
        dy_ref[...] = (dx * m_ref[0, :, k_g * D:(k_g + 1) * D]).astype(BF16)
        _accumulate_slot(t, dp_ref, jnp.sum(dx * y_ref[...], axis=0, keepdims=True))

    return _pcall(body, name=name,
                  out_shape=(jax.ShapeDtypeStruct((NROW, D), BF16), jax.ShapeDtypeStruct((2 * BL, 1, D), F32)),
                  grid=(NT,), in_specs=[_rows(D), _rows(D), _mods_spec()],
                  out_specs=(_rows(D), pl.BlockSpec((1, 1, D), lambda t: (_slot(t), 0, 0))),
                  sem=("arbitrary",))(dx, y, mods)


def _swap16(y, lo16):
    return jnp.where(lo16, pltpu.roll(y, LANE - 16, 1), pltpu.roll(y, 16, 1))


def _group_mean(v, g_mat):
    return jnp.dot(v, g_mat, precision=HIGHEST, preferred_element_type=F32)


def qkv_fwd(p_main, cos_t, sin_t, g_mat, gq, gk, name):
    def body(p_ref, cos_ref, sin_ref, g_ref, gq_ref, gk_ref, q_ref, k_ref, v_ref):
        cos, sin, g_mat_v = cos_ref[...], sin_ref[...], g_ref[...]
        lo16 = (lax.broadcasted_iota(jnp.int32, (TM, LANE), 1) % 32) < 16

        def block(xb, g):
            r = lax.rsqrt(_group_mean(xb * xb, g_mat_v) + EPS)
            y = xb * r * g
            return y * cos + _swap16(y, lo16) * sin

        for j in range(AW // LANE):
            q_ref[:, j * LANE:(j + 1) * LANE] = (block(p_ref[:, j * LANE:(j + 1) * LANE], gq_ref[...])
                                                 * ATTN_SCALE).astype(BF16)
        k_ref[...] = block(p_ref[:, OFF_K:OFF_K + LANE], gk_ref[...]).astype(BF16)
        v_ref[...] = p_ref[:, OFF_V:OFF_V + LANE].astype(BF16)

    tab = pl.BlockSpec((TM, LANE), lambda t: (t % TPE, 0))
    small = pl.BlockSpec((1, LANE), lambda t: (0, 0))
    return _pcall(body, name=name,
                  out_shape=(jax.ShapeDtypeStruct((NROW, AW), BF16), jax.ShapeDtypeStruct((NROW, KVW), BF16),
                             jax.ShapeDtypeStruct((NROW, KVW), BF16)),
                  grid=(NT,),
                  in_specs=[_rows(QKVW), tab, tab, pl.BlockSpec((LANE, LANE), lambda t: (0, 0)), small, small],
                  out_specs=(_rows(AW), _rows(KVW), _rows(KVW)), sem=("parallel",))(p_main, cos_t, sin_t, g_mat, gq, gk)


def qkv_bwd(p_main, cos_t, sin_t, g_mat, gq, gk, dq, dk, dv, name):
    def body(p_ref, cos_ref, sin_ref, g_ref, gq_ref, gk_ref, dq_ref, dk_ref, dv_ref, dp_ref, dg_ref):
        t = pl.program_id(0)
        cos, sin, g_mat_v = cos_ref[...], sin_ref[...], g_ref[...]
        lo16 = (lax.broadcasted_iota(jnp.int32, (TM, LANE), 1) % 32) < 16

        def block(xb, g, dyr):
            r = lax.rsqrt(_group_mean(xb * xb, g_mat_v) + EPS)
            xn = xb * r
            dy = dyr * cos + _swap16(dyr * sin, lo16)
            dgl = jnp.sum(dy * xn, axis=0, keepdims=True)
            dxn = dy * g
            return r * (dxn - xn * _group_mean(dxn * xn, g_mat_v)), dgl

        parts = []
        for j in range(AW // LANE):
            sl = slice(j * LANE, (j + 1) * LANE)
            dxb, dgl = block(p_ref[:, sl], gq_ref[...], dq_ref[:, sl] * ATTN_SCALE)
            dp_ref[:, sl] = dxb.astype(BF16)
            parts.append(dgl)
        dxb, dgl = block(p_ref[:, OFF_K:OFF_K + LANE], gk_ref[...], dk_ref[...])
        dp_ref[:, OFF_K:OFF_K + LANE] = dxb.astype(BF16)
        parts.append(dgl)
        parts.append(jnp.zeros((1, LANE), F32))
        dp_ref[:, OFF_V:OFF_V + LANE] = dv_ref[...].astype(BF16)
        part = jnp.concatenate(parts, axis=1)

        @pl.when(t == 0)
        def _():
            dg_ref[...] = part

        @pl.when(t != 0)
        def _():
            dg_ref[...] += part

    tab = pl.BlockSpec((TM, LANE), lambda t: (t % TPE, 0))
    small = pl.BlockSpec((1, LANE), lambda t: (0, 0))
    return _pcall(body, name=name,
                  out_shape=(jax.ShapeDtypeStruct((NROW, QKVW), BF16), jax.ShapeDtypeStruct((1, QKVW), F32)),
                  grid=(NT,),
                  in_specs=[_rows(QKVW), tab, tab, pl.BlockSpec((LANE, LANE), lambda t: (0, 0)), small, small,
                            _rows(AW), _rows(KVW), _rows(KVW)],
                  out_specs=(_rows(QKVW), pl.BlockSpec((1, QKVW), lambda t: (0, 0))),
                  sem=("arbitrary",))(p_main, cos_t, sin_t, g_mat, gq, gk, dq, dk, dv)


def _layer_norm_parts(yc):
    mu = jnp.mean(yc, axis=-1, keepdims=True)
    xc = yc - mu
    rs = lax.rsqrt(jnp.mean(xc * xc, axis=-1, keepdims=True) + EPS)
    return xc * rs, rs


def ln_silu_fwd(yc, g, b, name):
    def body(y_ref, g_ref, b_ref, o_ref):
        nrm, _ = _layer_norm_parts(y_ref[...])
        ln = nrm * g_ref[...] + b_ref[...]
        o_ref[...] = (ln * _sigmoid(ln)).astype(BF16)

    vec = pl.BlockSpec((1, CW), lambda t: (0, 0))
    return _pcall(body, name=name, out_shape=jax.ShapeDtypeStruct((NROW, CW), BF16), grid=(NT,),
                  in_specs=[_rows(CW), vec, vec], out_specs=_rows(CW), sem=("parallel",))(yc, g, b)


def ln_silu_bwd(yc, g, b, dhs, name):
    def body(y_ref, g_ref, b_ref, dh_ref, dy_ref, dg_ref, db_ref):
        t = pl.program_id(0)
        nrm, rs = _layer_norm_parts(y_ref[...])
        ln = nrm * g_ref[...] + b_ref[...]
        sg = _sigmoid(ln)
        dln = dh_ref[...] * (sg * (1.0 + ln * (1.0 - sg)))
        dn = dln * g_ref[...]
        dy_ref[...] = rs * (dn - jnp.mean(dn, axis=-1, keepdims=True)
                            - nrm * jnp.mean(dn * nrm, axis=-1, keepdims=True))
        pg = jnp.sum(dln * nrm, axis=0, keepdims=True)
        pb = jnp.sum(dln, axis=0, keepdims=True)

        @pl.when(t == 0)
        def _():
            dg_ref[...] = pg
            db_ref[...] = pb

        @pl.when(t != 0)
        def _():
            dg_ref[...] += pg
            db_ref[...] += pb

    vec = pl.BlockSpec((1, CW), lambda t: (0, 0))
    return _pcall(body, name=name,
                  out_shape=(jax.ShapeDtypeStruct((NROW, CW), F32), jax.ShapeDtypeStruct((1, CW), F32),
                             jax.ShapeDtypeStruct((1, CW), F32)),
                  grid=(NT,), in_specs=[_rows(CW), vec, vec, _rows(CW)], out_specs=(_rows(CW), vec, vec),
                  sem=("arbitrary",))(yc, g, b, dhs)


def gate_merge_fwd(p_gate, ya, yb, ys, name):
    def body(p_ref, a_ref, b_ref, s_ref, o_ref):
        out = _sigmoid(p_ref[:, 0:D]) * a_ref[...]
        out += _sigmoid(p_ref[:, D:2 * D]) * b_ref[...]
        out += _sigmoid(p_ref[:, 2 * D:3 * D]) * s_ref[...]
        o_ref[...] = out.astype(BF16)

    return _pcall(body, name=name, out_shape=jax.ShapeDtypeStruct((NROW, D), BF16), grid=(NT,),
                  in_specs=[_rows(3 * D), _rows(D), _rows(D), _rows(D)], out_specs=_rows(D),
                  sem=("parallel",))(p_gate, ya, yb, ys)


def gate_merge_bwd(p_gate, ya, yb, ys, dmerged, name):
    def body(p_ref, a_ref, b_ref, s_ref, dm_ref, da_ref, db_ref, ds_ref, dp_ref):
        dm = dm_ref[...]
        for i, (y_ref, dy_ref) in enumerate(((a_ref, da_ref), (b_ref, db_ref), (s_ref, ds_ref))):
            g = _sigmoid(p_ref[:, i * D:(i + 1) * D])
            dy_ref[...] = (dm * g).astype(BF16)
            dp_ref[:, i * D:(i + 1) * D] = (dm * y_ref[...] * g * (1.0 - g)).astype(BF16)

    return _pcall(body, name=name,
                  out_shape=(jax.ShapeDtypeStruct((NROW, D), BF16),) * 3 + (jax.ShapeDtypeStruct((NROW, 3 * D), BF16),),
                  grid=(NT,), in_specs=[_rows(3 * D), _rows(D), _rows(D), _rows(D), _rows(D)],
                  out_specs=(_rows(D), _rows(D), _rows(D), _rows(3 * D)), sem=("parallel",))(p_gate, ya, yb, ys, dmerged)


def swiglu_fwd(u, name):
    def body(u_ref, f_ref):
        a = u_ref[:, 0:FH]
        f_ref[...] = (a * _sigmoid(a) * u_ref[:, FH:2 * FH]).astype(BF16)

    return _pcall(body, name=name, out_shape=jax.ShapeDtypeStruct((NROW, FH), BF16), grid=(NT,),
                  in_specs=[_rows(2 * FH)], out_specs=_rows(FH), sem=("parallel",))(u)


def swiglu_bwd(u, df, name):
    def body(u_ref, df_ref, du_ref):
        a, b, df_v = u_ref[:, 0:FH], u_ref[:, FH:2 * FH], df_ref[...]
        sg = _sigmoid(a)
        du_ref[:, 0:FH] = (df_v * b * (sg * (1.0 + a * (1.0 - sg)))).astype(BF16)
        du_ref[:, FH:2 * FH] = (df_v * a * sg).astype(BF16)

    return _pcall(body, name=name, out_shape=jax.ShapeDtypeStruct((NROW, 2 * FH), BF16), grid=(NT,),
                  in_specs=[_rows(2 * FH), _rows(FH)], out_specs=_rows(2 * FH), sem=("parallel",))(u, df)


def loss_fwd_bwd(y, target, name):
    def body(y_ref, t_ref, dy_ref, l_ref):
        t = pl.program_id(0)
        latent = (t % TPE) != 0
        err = jnp.where(latent, y_ref[...] - t_ref[...], 0.0)
        dy_ref[...] = err * (1.0 / D)
        part = jnp.sum(err * err, axis=0, keepdims=True)

        @pl.when(t == 0)
        def _():
            l_ref[...] = part

        @pl.when(t != 0)
        def _():
            l_ref[...] += part

    tgt = pl.BlockSpec((TM, D), lambda t: ((t // TPE) * (TPE - 1) + jnp.maximum(t % TPE - 1, 0), 0))
    return _pcall(body, name=name,
                  out_shape=(jax.ShapeDtypeStruct((NROW, D), F32), jax.ShapeDtypeStruct((1, D), F32)),
                  grid=(NT,), in_specs=[_rows(D), tgt], out_specs=(_rows(D), pl.BlockSpec((1, D), lambda t: (0, 0))),
                  sem=("arbitrary",))(y, target)


def _attn_probs(qm, k, cmask):
    s = lax.dot_general(qm, k, (((1,), (1,)), ((), ())), preferred_element_type=F32)
    s = jnp.where(cmask, s, -1e30)
    e = jnp.exp(s - jnp.max(s, axis=-1, keepdims=True))
    return e * (1.0 / jnp.sum(e, axis=-1, keepdims=True))


def _attn_masks(t):
    nvalid = jnp.where(t == 0, CTX, RE)
    cmask = lax.broadcasted_iota(jnp.int32, (TM, RE), 1) < nvalid
    lo = lax.broadcasted_iota(jnp.int32, (TM, LANE), 1) < HD
    return cmask, lo


def attn_fwd(q, k, v, name):
    def body(q_ref, k_ref, v_ref, o_ref):
        cmask, lo = _attn_masks(pl.program_id(1))
        qv, kv, vv = q_ref[...], k_ref[...], v_ref[...]
        outs = []
        for sel in (lo, jnp.logical_not(lo)):
            p = _attn_probs(jnp.where(sel, qv, jnp.zeros_like(qv)), kv, cmask)
            outs.append(jnp.dot(p.astype(BF16), vv, preferred_element_type=F32))
        o_ref[...] = jnp.where(lo, outs[0], outs[1]).astype(BF16)

    qs = pl.BlockSpec((TM, LANE), lambda b, t, j: (b * TPE + t, j))
    ks = pl.BlockSpec((RE, LANE), lambda b, t, j: (b, 0))
    return _pcall(body, name=name, out_shape=jax.ShapeDtypeStruct((NROW, AW), BF16), grid=(BL, TPE, AW // LANE),
                  in_specs=[qs, ks, ks], out_specs=qs, sem=("parallel", "parallel", "parallel"))(q, k, v)


def attn_bwd(q, k, v, do, name):
    def body(q_ref, k_ref, v_ref, do_ref, dq_ref, dk_ref, dv_ref):
        t, j = pl.program_id(1), pl.program_id(2)
        cmask, lo = _attn_masks(t)
        qv, kv, vv, dov = q_ref[...], k_ref[...], v_ref[...], do_ref[...]

        @pl.when(jnp.logical_and(t == 0, j == 0))
        def _():
            dk_ref[...] = jnp.zeros_like(dk_ref)
            dv_ref[...] = jnp.zeros_like(dv_ref)

        dqs = []
        for sel in (lo, jnp.logical_not(lo)):
            qm = jnp.where(sel, qv, jnp.zeros_like(qv))
            dom = jnp.where(sel, dov, jnp.zeros_like(dov))
            p = _attn_probs(qm, kv, cmask)
            dv_ref[...] += lax.dot_general(p.astype(BF16), dom, (((0,), (0,)), ((), ())), preferred_element_type=F32)
            dp = lax.dot_general(dom, vv, (((1,), (1,)), ((), ())), preferred_element_type=F32)
            ds = (p * (dp - jnp.sum(dp * p, axis=-1, keepdims=True))).astype(BF16)
            dqs.append(jnp.dot(ds, kv, preferred_element_type=F32))
            dk_ref[...] += lax.dot_general(ds, qm, (((0,), (0,)), ((), ())), preferred_element_type=F32)
        dq_ref[...] = jnp.where(lo, dqs[0], dqs[1])

    qs = pl.BlockSpec((TM, LANE), lambda b, t, j: (b * TPE + t, j))
    ks = pl.BlockSpec((RE, LANE), lambda b, t, j: (b, 0))
    return _pcall(body, name=name,
                  out_shape=(jax.ShapeDtypeStruct((NROW, AW), F32), jax.ShapeDtypeStruct((NROW, KVW), F32),
                             jax.ShapeDtypeStruct((NROW, KVW), F32)),
                  grid=(BL, TPE, AW // LANE), in_specs=[qs, ks, ks, qs], out_specs=(qs, ks, ks),
                  sem=("parallel", "arbitrary", "arbitrary"))(q, k, v, do)


CONV_SEGS = ((0, CTX), (CTX, SEQ))


def _p_block(col0):
    return pl.BlockSpec((RE, LANE), lambda cb, b: (b, col0 // LANE + cb))


def _conv_io(width):
    return pl.BlockSpec((RE, LANE), lambda cb, b: (b, cb))


def _taps(n):
    return pl.BlockSpec((n, LANE), lambda cb, b: (0, cb))


def _fill_pad(pad_ref, length, values):
    pad_ref[0:PADR, :] = jnp.zeros((PADR, LANE), F32)
    pad_ref[PADR + length:2 * PADR + length, :] = jnp.zeros((PADR, LANE), F32)
    pad_ref[PADR:PADR + length, :] = values


def _conv_chunk(pad_ref, w_ref, ntap, c0, first_row):
    acc = jnp.zeros((CONV_CH, LANE), F32)
    for kk in range(ntap):
        r0 = c0 + first_row(kk)
        acc += w_ref[kk:kk + 1, :] * pad_ref[r0:r0 + CONV_CH, :]
    return acc


def conv_fwd(p_main, wdw, bdw, w3, name):
    def body(a_ref, g_ref, bg_ref, cg_ref, xs_ref, w_ref, b_ref, w3_ref, yc_ref, z_ref, pad_ref):
        for off, length in CONV_SEGS:
            rows = slice(off, off + length)
            _fill_pad(pad_ref, length, a_ref[rows, :] * _sigmoid(g_ref[rows, :]))
            for c0 in range(0, length, CONV_CH):
                acc = _conv_chunk(pad_ref, w_ref, CONF_K, c0, lambda kk: PADR + kk - CONF_K // 2)
                yc_ref[off + c0:off + c0 + CONV_CH, :] = acc + b_ref[...]
            pad_ref[PADR:PADR + length, :] = cg_ref[rows, :] * xs_ref[rows, :]
            for c0 in range(0, length, CONV_CH):
                acc = _conv_chunk(pad_ref, w3_ref, SC_K, c0, lambda kk: PADR + kk - SC_K // 2)
                z_ref[off + c0:off + c0 + CONV_CH, :] = (bg_ref[off + c0:off + c0 + CONV_CH, :] * acc).astype(BF16)

    return _pcall(body, name=name,
                  out_shape=(jax.ShapeDtypeStruct((NROW, CW), F32), jax.ShapeDtypeStruct((NROW, CW), BF16)),
                  grid=(CB, BL),
                  in_specs=[_p_block(OFF_CONF), _p_block(OFF_CONF + CW), _p_block(OFF_SC), _p_block(OFF_SC + CW),
                            _p_block(OFF_SC + 2 * CW), _taps(CONF_K), _taps(1), _taps(SC_K)],
                  out_specs=(_conv_io(CW), _conv_io(CW)),
                  scratch=[pltpu.VMEM((SEQ + 2 * PADR, LANE), F32)],
                  sem=("parallel", "parallel"))(p_main, p_main, p_main, p_main, p_main, wdw, bdw, w3)


def _tap_grad(pad_ref, d_ref, off, length, first_row):
    acc = jnp.zeros((8, LANE), F32)
    for c0 in range(0, length, CONV_CH):
        prod = d_ref[off + c0:off + c0 + CONV_CH, :] * pad_ref[c0 + first_row:c0 + first_row + CONV_CH, :]
        acc += jnp.sum(prod.reshape(CONV_CH // 8, 8, LANE), axis=0)
    return jnp.sum(acc, axis=0, keepdims=True)


def conv_bwd(p_main, wdw, w3, dyc, dz, name):
    def body(a_ref, g_ref, bg_ref, cg_ref, xs_ref, w_ref, w3_ref, dyc_ref, dz_ref,
             da_ref, dg_ref, dbg_ref, dcg_ref, dxs_ref, dw_ref, db_ref, dw3_ref, pad_x, pad_d, dconv_ref):
        b = pl.program_id(1)

        @pl.when(b == 0)
        def _():
            dw_ref[...] = jnp.zeros_like(dw_ref)
            db_ref[...] = jnp.zeros_like(db_ref)
            dw3_ref[...] = jnp.zeros_like(dw3_ref)

        db_ref[...] += jnp.sum(dyc_ref[...], axis=0, keepdims=True)
        for off, length in CONV_SEGS:
            rows = slice(off, off + length)
            _fill_pad(pad_x, length, a_ref[rows, :] * _sigmoid(g_ref[rows, :]))
            _fill_pad(pad_d, length, dyc_ref[rows, :])
            for kk in range(CONF_K):
                dw_ref[kk:kk + 1, :] += _tap_grad(pad_x, dyc_ref, off, length, PADR + kk - CONF_K // 2)
            for c0 in range(0, length, CONV_CH):
                dh = _conv_chunk(pad_d, w_ref, CONF_K, c0, lambda kk: PADR + CONF_K // 2 - kk)
                ch = slice(off + c0, off + c0 + CONV_CH)
                sg = _sigmoid(g_ref[ch, :])
                da_ref[ch, :] = (dh * sg).astype(BF16)
                dg_ref[ch, :] = (dh * a_ref[ch, :] * sg * (1.0 - sg)).astype(BF16)
            pad_x[PADR:PADR + length, :] = cg_ref[rows, :] * xs_ref[rows, :]
            dconv_ref[rows, :] = dz_ref[rows, :] * bg_ref[rows, :]
            pad_d[PADR:PADR + length, :] = dconv_ref[rows, :]
            for kk in range(SC_K):
                dw3_ref[kk:kk + 1, :] += _tap_grad(pad_x, dconv_ref, off, length, PADR + kk - SC_K // 2)
            for c0 in range(0, length, CONV_CH):
                ch = slice(off + c0, off + c0 + CONV_CH)
                c3 = _conv_chunk(pad_x, w3_ref, SC_K, c0, lambda kk: PADR + kk - SC_K // 2)
                dbg_ref[ch, :] = (dz_ref[ch, :] * c3).astype(BF16)
                dcx = _conv_chunk(pad_d, w3_ref, SC_K, c0, lambda kk: PADR + SC_K // 2 - kk)
                dcg_ref[ch, :] = (dcx * xs_ref[ch, :]).astype(BF16)
                dxs_ref[ch, :] = (dcx * cg_ref[ch, :]).astype(BF16)

    slab = jax.ShapeDtypeStruct((NROW, CW), BF16)
    return _pcall(body, name=name,
                  out_shape=(slab,) * 5 + (jax.ShapeDtypeStruct((CONF_K, CW), F32), jax.ShapeDtypeStruct((1, CW), F32),
                                           jax.ShapeDtypeStruct((SC_K, CW), F32)),
                  grid=(CB, BL),
                  in_specs=[_p_block(OFF_CONF), _p_block(OFF_CONF + CW), _p_block(OFF_SC), _p_block(OFF_SC + CW),
                            _p_block(OFF_SC + 2 * CW), _taps(CONF_K), _taps(SC_K), _conv_io(CW), _conv_io(CW)],
                  out_specs=(_conv_io(CW),) * 5 + (_taps(CONF_K), _taps(1), _taps(SC_K)),
                  scratch=[pltpu.VMEM((SEQ + 2 * PADR, LANE), F32), pltpu.VMEM((SEQ + 2 * PADR, LANE), F32),
                           pltpu.VMEM((RE, LANE), F32)],
                  sem=("parallel", "arbitrary"))(p_main, p_main, p_main, p_main, p_main, wdw, w3, dyc, dz)


def silu_rows(x, name):
    def body(x_ref, o_ref):
        o_ref[...] = x_ref[...] * _sigmoid(x_ref[...])

    return _pcall(body, name=name, out_shape=jax.ShapeDtypeStruct(x.shape, F32))(x)


def silu_rows_bwd(x, dcs, name):
    def body(x_ref, d_ref, o_ref):
        x = x_ref[...]
        sg = _sigmoid(x)
        tot = d_ref[0]
        for i in range(1, DEPTH):
            tot += d_ref[i]
        o_ref[...] = tot * (sg * (1.0 + x * (1.0 - sg)))

    return _pcall(body, name=name, out_shape=jax.ShapeDtypeStruct(x.shape, F32))(x, dcs)


def dmod_assemble(parts, name):
    def body(p_ref, dm_ref, db_ref):
        row = lax.broadcasted_iota(jnp.int32, (8, NMOD * D), 0)
        dm = jnp.zeros((8, NMOD * D), F32)
        db = jnp.zeros((1, NMOD * D), F32)
        for s in range(2 * BL):
            target = BL if s % 2 == 0 else s // 2
            part = p_ref[s:s + 1, :]
            dm += jnp.where(row == target, part, 0.0)
            db += part
        dm_ref[...] = dm
        db_ref[...] = db

    return _pcall(body, name=name, out_shape=(jax.ShapeDtypeStruct((8, NMOD * D), F32),
                                              jax.ShapeDtypeStruct((1, NMOD * D), F32)))(parts)


def sum_leading(x, name):
    n = x.shape[0]
    tr = _pick(x.shape[1], (256, 32, 8))

    def body(x_ref, o_ref):
        tot = x_ref[0].astype(F32)
        for i in range(1, n):
            tot += x_ref[i].astype(F32)
        o_ref[...] = tot

    return _pcall(body, name=name, out_shape=jax.ShapeDtypeStruct(x.shape[1:], F32), grid=(x.shape[1] // tr,),
                  in_specs=[pl.BlockSpec((n, tr, x.shape[2]), lambda i: (0, i, 0))],
                  out_specs=pl.BlockSpec((tr, x.shape[2]), lambda i: (i, 0)), sem=("parallel",))(x)


def add_cast(g, other, half, name):
    _, rows, cols = g.shape
    rh = rows // 2
    tr = _pick(rh, (176, 16))
    nblk = rh // tr

    def body(half_ref, g_ref, o_ref, out_ref):
        out_ref[...] = (g_ref[...] + o_ref[...]).astype(BF16)

    spec = pltpu.PrefetchScalarGridSpec(
        num_scalar_prefetch=1, grid=(NCHIP, nblk),
        in_specs=[pl.BlockSpec((1, tr, cols), lambda s, i, h: (s, h[0] * nblk + i, 0)),
                  pl.BlockSpec((1, tr, cols), lambda s, i, h: (s, i, 0))],
        out_specs=pl.BlockSpec((1, tr, cols), lambda s, i, h: (s, i, 0)))
    return pl.pallas_call(body, name=name, out_shape=jax.ShapeDtypeStruct((NCHIP, rh, cols), BF16), grid_spec=spec,
                          compiler_params=pltpu.CompilerParams(dimension_semantics=("parallel", "parallel"),
                                                               vmem_limit_bytes=VMEM_LIMIT))(half.reshape(1), g, other)


def adamw(w, g, m, v, name):
    rows, cols = w.shape
    tr = _pick(rows, (256, 248, 128, 8))
    c1 = 1.0 / (1.0 - ADAM_B1 ** ADAM_STEP)
    c2 = 1.0 / (1.0 - ADAM_B2 ** ADAM_STEP)

    def body(w_ref, g_ref, m_ref, v_ref, d_ref, mo_ref, vo_ref):
        gv = g_ref[...]
        mn = ADAM_B1 * m_ref[...] + (1.0 - ADAM_B1) * gv
        vn = ADAM_B2 * v_ref[...] + (1.0 - ADAM_B2) * (gv * gv)
        d_ref[...] = -ADAM_LR * ((mn * c1) / (jnp.sqrt(vn * c2) + ADAM_EPS) + ADAM_WD * w_ref[...])
        mo_ref[...] = mn
        vo_ref[...] = vn

    spec = pl.BlockSpec((tr, cols), lambda i: (i, 0))
    sds = jax.ShapeDtypeStruct((rows, cols), F32)
    return _pcall(body, name=name, out_shape=(sds, sds, sds), grid=(rows // tr,), in_specs=[spec] * 4,
                  out_specs=(spec, spec, spec), sem=("parallel",))(w, g, m, v)


def _place():
    return lax.axis_index("x"), lax.axis_index("y"), lax.axis_index("c")


def _other_chips(x, y):
    return [(1 - x, y), (x, 1 - y), (1 - x, 1 - y)]


def _comm_call(body, name, out_shape, n_in, nsem):
    return pl.pallas_call(body, name=name, out_shape=out_shape, in_specs=[ANY] * n_in,
                          out_specs=jax.tree.map(lambda _: ANY, out_shape),
                          scratch_shapes=[pltpu.SemaphoreType.DMA((nsem,)), pltpu.SemaphoreType.DMA((nsem,)),
                                          pltpu.SemaphoreType.DMA])


def all_gather8(block, name):
    def body(x_ref, out_ref, send_sems, recv_sems, local_sem):
        x, y, c = _place()
        me, sibling = (x, y, c), (x, y, 1 - c)
        chips = _other_chips(x, y)

        def slot(px, py, pc):
            return out_ref.at[4 * px + 2 * py + pc]

        def copy(k, blk, to, src=None):
            return pltpu.make_async_remote_copy(src_ref=slot(*blk) if src is None else src, dst_ref=slot(*blk),
                                                send_sem=send_sems.at[k], recv_sem=recv_sems.at[k],
                                                device_id=to, device_id_type=MESH)

        mine = pltpu.make_async_copy(x_ref, slot(*me), local_sem)
        mine.start()
        first = [copy(0, me, sibling, src=x_ref)]
        first += [copy(1 + j, me, (*chip, c), src=x_ref) for j, chip in enumerate(chips)]
        for cp in first:
            cp.start()
        passed = [copy(4 + j, (*chip, c), sibling) for j, chip in enumerate(chips)]
        for j, chip in enumerate(chips):
            copy(1 + j, (*chip, c), me).wait_recv()
            passed[j].start()
        copy(0, sibling, me).wait_recv()
        for j, chip in enumerate(chips):
            copy(4 + j, (*chip, 1 - c), me).wait_recv()
        for cp in first + passed:
            cp.wait_send()
        mine.wait()

    return _comm_call(body, name, jax.ShapeDtypeStruct((8,) + block.shape, block.dtype), 1, 7)(block)


def all_gather_shards(w, name):
    rows, _ = w.shape
    rh = rows // 2

    def body(w_ref, out_ref, send_sems, recv_sems, local_sem):
        x, y, c = _place()
        sibling = (x, y, 1 - c)
        chips = _other_chips(x, y)

        def half(px, py, pc):
            return out_ref.at[2 * px + py, pl.ds(pc * rh, rh)]

        def copy(k, blk, to, src=None):
            return pltpu.make_async_remote_copy(src_ref=half(*blk) if src is None else src, dst_ref=half(*blk),
                                                send_sem=send_sems.at[k], recv_sem=recv_sems.at[k],
                                                device_id=to, device_id_type=MESH)

        mine = pltpu.make_async_copy(w_ref, out_ref.at[2 * x + y], local_sem)
        mine.start()
        first = [copy(j, (x, y, c), (*chip, c), src=w_ref.at[pl.ds(c * rh, rh)]) for j, chip in enumerate(chips)]
        for cp in first:
            cp.start()
        passed = [copy(3 + j, (*chip, c), sibling) for j, chip in enumerate(chips)]
        for j, chip in enumerate(chips):
            copy(j, (*chip, c), (x, y, c)).wait_recv()
            passed[j].start()
        for j, chip in enumerate(chips):
            copy(3 + j, (*chip, 1 - c), (x, y, c)).wait_recv()
        for cp in first + passed:
            cp.wait_send()
        mine.wait()

    return _comm_call(body, name, jax.ShapeDtypeStruct((NCHIP,) + w.shape, w.dtype), 1, 6)(w)


def sibling_swap_half(g, name):
    _, rows, cols = g.shape
    rh = rows // 2

    def body(g_ref, out_ref, send_sems, recv_sems, local_sem):
        x, y, c = _place()
        cp = pltpu.make_async_remote_copy(src_ref=g_ref.at[:, pl.ds((1 - c) * rh, rh)], dst_ref=out_ref,
                                          send_sem=send_sems.at[0], recv_sem=recv_sems.at[0],
                                          device_id=(x, y, 1 - c), device_id_type=MESH)
        cp.start()
        cp.wait()

    return _comm_call(body, name, jax.ShapeDtypeStruct((NCHIP, rh, cols), g.dtype), 1, 1)(g)


def chip_exchange(send, name):
    def body(s_ref, out_ref, send_sems, recv_sems, local_sem):
        x, y, c = _place()
        chips = _other_chips(x, y)
        own = 2 * x + y

        def copy(j, chip):
            return pltpu.make_async_remote_copy(src_ref=s_ref.at[2 * chip[0] + chip[1]], dst_ref=out_ref.at[own],
                                                send_sem=send_sems.at[j], recv_sem=recv_sems.at[j],
                                                device_id=(*chip, c), device_id_type=MESH)

        def arrival(j, chip):
            return pltpu.make_async_remote_copy(src_ref=s_ref.at[own], dst_ref=out_ref.at[2 * chip[0] + chip[1]],
                                                send_sem=send_sems.at[j], recv_sem=recv_sems.at[j],
                                                device_id=(*chip, c), device_id_type=MESH)

        mine = pltpu.make_async_copy(s_ref.at[own], out_ref.at[own], local_sem)
        mine.start()
        sends = [copy(j, chip) for j, chip in enumerate(chips)]
        for cp in sends:
            cp.start()
        for j, chip in enumerate(chips):
            arrival(j, chip).wait_recv()
        for cp in sends:
            cp.wait_send()
        mine.wait()

    return _comm_call(body, name, jax.ShapeDtypeStruct(send.shape, send.dtype), 1, 3)(send)


def sibling_join_halves(gh, name):
    rh, cols = gh.shape

    def body(g_ref, out_ref, send_sems, recv_sems, local_sem):
        x, y, c = _place()
        mine = pltpu.make_async_copy(g_ref, out_ref.at[pl.ds(c * rh, rh)], local_sem)
        mine.start()
        cp = pltpu.make_async_remote_copy(src_ref=g_ref, dst_ref=out_ref.at[pl.ds(c * rh, rh)],
                                          send_sem=send_sems.at[0], recv_sem=recv_sems.at[0],
                                          device_id=(x, y, 1 - c), device_id_type=MESH)
        cp.start()
        pltpu.make_async_remote_copy(src_ref=g_ref, dst_ref=out_ref.at[pl.ds((1 - c) * rh, rh)],
                                     send_sem=send_sems.at[0], recv_sem=recv_sems.at[0],
                                     device_id=(x, y, 1 - c), device_id_type=MESH).wait_recv()
        cp.wait_send()
        mine.wait()

    return _comm_call(body, name, jax.ShapeDtypeStruct((2 * rh, cols), gh.dtype), 1, 1)(gh)


PACK_COLS = 1024
SHARDED = (("w_ada", 1), ("w_in", 1), ("w_attn_o", 1), ("w_conf_out", 1), ("w_sc_out", 1), ("w_mix_out", 0),
           ("w_ffn_in", 1), ("w_ffn_out", 0), ("conf_dw_w", 1), ("sc_dw_w", 1))
MATMUL_W = ("w_ada", "w_in", "w_attn_o", "w_conf_out", "w_sc_out", "w_mix_out", "w_ffn_in", "w_ffn_out")
CONV_W = ("conf_dw_w", "sc_dw_w")
SMALL = ("c_ctx", "b_ada", "q_norm", "k_norm", "conf_dw_b", "conf_ln_g", "conf_ln_b")


def _pack_rows(arrays, row_multiple):
    flat = jnp.concatenate([a.reshape(-1) for a in arrays])
    rows = -(-flat.shape[0] // PACK_COLS)
    rows = -(-rows // row_multiple) * row_multiple
    flat = jnp.pad(flat, (0, rows * PACK_COLS - flat.shape[0]))
    return flat.reshape(rows, PACK_COLS)


def _unpack(flat2d, shapes):
    flat = flat2d.reshape(-1)
    out, pos = [], 0
    for shp in shapes:
        n = 1
        for s in shp:
            n *= s
        out.append(flat[pos:pos + n].reshape(shp))
        pos += n
    return out


def _join_shards(stacked, shapes, axes):
    per_chip = [_unpack(stacked[s], shapes) for s in range(NCHIP)]
    return [jnp.concatenate([per_chip[s][i] for s in range(NCHIP)], axis=1 + axes[i]) for i in range(len(shapes))]


def _split_shards(fulls, axes, row_multiple):
    slots = []
    for s in range(NCHIP):
        parts = []
        for full, ax in zip(fulls, axes):
            n = full.shape[1 + ax] // NCHIP
            parts.append(lax.slice_in_dim(full, s * n, (s + 1) * n, axis=1 + ax))
        slots.append(_pack_rows(parts, row_multiple))
    return jnp.stack(slots)


def _perm_heads(w, axis, perm):
    heads =[lax.slice_in_dim(w, p * HD, (p + 1) * HD, axis=axis) for p in perm]
    return jnp.concatenate(heads, axis=axis)


def _rope_tables():
    rows = SEQ // GRID_W
    r_ids = jnp.repeat(jnp.arange(rows, dtype=F32), GRID_W)
    c_ids = jnp.tile(jnp.arange(GRID_W, dtype=F32), rows)
    freqs = ROPE_THETA ** (-jnp.arange(0, HD // 2, 2, dtype=F32) / (HD // 2))
    ang_r, ang_c = r_ids[:, None] * freqs, c_ids[:, None] * freqs
    cos_h = jnp.concatenate([jnp.cos(ang_r), jnp.cos(ang_r), jnp.cos(ang_c), jnp.cos(ang_c)], axis=1)
    sin_h = jnp.concatenate([-jnp.sin(ang_r), jnp.sin(ang_r), -jnp.sin(ang_c), jnp.sin(ang_c)], axis=1)
    cos_t = jnp.concatenate([jnp.ones((CTX, HD), F32), cos_h], axis=0)
    sin_t = jnp.concatenate([jnp.zeros((CTX, HD), F32), sin_h], axis=0)
    return jnp.tile(cos_t, (1, LANE // HD)), jnp.tile(sin_t, (1, LANE // HD))


def _group_matrix():
    gid = jnp.arange(LANE) // HD
    return jnp.where(gid[:, None] == gid[None, :], 1.0 / HD, 0.0).astype(F32)


def _layer_fwd(i, xs, mods, w, tabs):
    cos_t, sin_t, g_mat = tabs
    n = f"l{i}_"
    sv = {"x_in": xs, "mods": mods}
    sv["h"] = norm_mod_fwd(xs, mods, 0, 1, n + "norm1")
    sv["p_main"] = mm(sv["h"], w["wi_main"], name=n + "p_main")
    sv["p_gate"] = mm(sv["h"], w["wi_gate"], name=n + "p_gate")
    sv["q"], sv["k"], sv["v"] = qkv_fwd(sv["p_main"], cos_t, sin_t, g_mat, w["gq"], w["gk"], n + "qkv")
    sv["o"] = attn_fwd(sv["q"], sv["k"], sv["v"], n + "attn")
    sv["yc"], sv["z"] = conv_fwd(sv["p_main"], w["conf_dw_w"], w["conf_dw_b"], w["sc_dw_w"], n + "conv")
    sv["hs"] = ln_silu_fwd(sv["yc"], w["conf_ln_g"], w["conf_ln_b"], n + "ln_silu")
    sv["ya"] = mm(sv["o"], w["w_attn_o"], name=n + "y_attn")
    sv["yb"] = mm(sv["hs"], w["w_conf_out"], name=n + "y_conf")
    sv["ys"] = mm(sv["z"], w["w_sc_out"], name=n + "y_sc")
    sv["merged"] = gate_merge_fwd(sv["p_gate"], sv["ya"], sv["yb"], sv["ys"], n + "merge")
    sv["mixed"] = mm(sv["merged"], w["w_mix_out"], name=n + "mix")
    sv["x1"] = gate_resid_fwd(xs, sv["mixed"], mods, 2, n + "resid1")
    sv["h2"] = norm_mod_fwd(sv["x1"], mods, 3, 4, n + "norm2")
    sv["u"] = mm(sv["h2"], w["w_ffn_in"], name=n + "ffn_in")
    sv["f"] = swiglu_fwd(sv["u"], n + "swiglu")
    sv["of"] = mm(sv["f"], w["w_ffn_out"], name=n + "ffn_out")
    x2 = gate_resid_fwd(sv["x1"], sv["of"], mods, 5, n + "resid2")
    return x2, sv


def _layer_bwd(i, dx2, sv, w, tabs, cs):
    cos_t, sin_t, g_mat = tabs
    n = f"l{i}b_"
    mods = sv["mods"]
    g = {}
    dof, dm5 = gate_resid_bwd(dx2, sv["of"], mods, 5, n + "resid2")
    df = mm(dof, w["w_ffn_out"], tb=True, name=n + "d_f")
    g["w_ffn_out"] = mm(sv["f"], dof, ta=True, name=n + "dw_ffn_out")
    du = swiglu_bwd(sv["u"], df, n + "swiglu")
    dh2 = mm(du, w["w_ffn_in"], tb=True, name=n + "d_h2")
    g["w_ffn_in"] = mm(sv["h2"], du, ta=True, name=n + "dw_ffn_in")
    dx1, dm34 = norm_mod_bwd(sv["x1"], mods, dh2, dx2, 4, n + "norm2")
    dmixed, dm2 = gate_resid_bwd(dx1, sv["mixed"], mods, 2, n + "resid1")
    dmerged = mm(dmixed, w["w_mix_out"], tb=True, name=n + "d_merged")
    g["w_mix_out"] = mm(sv["merged"], dmixed, ta=True, name=n + "dw_mix")
    dya, dyb, dys, dp_gate = gate_merge_bwd(sv["p_gate"], sv["ya"], sv["yb"], sv["ys"], dmerged, n + "merge")
    do = mm(dya, w["w_attn_o"], tb=True, out_dtype=BF16, name=n + "d_o")
    g["w_attn_o"] = mm(sv["o"], dya, ta=True, name=n + "dw_attn_o")
    dhs = mm(dyb, w["w_conf_out"], tb=True, name=n + "d_hs")
    g["w_conf_out"] = mm(sv["hs"], dyb, ta=True, name=n + "dw_conf_out")
    dz = mm(dys, w["w_sc_out"], tb=True, name=n + "d_z")
    g["w_sc_out"] = mm(sv["z"], dys, ta=True, name=n + "dw_sc_out")
    dyc, g["conf_ln_g"], g["conf_ln_b"] = ln_silu_bwd(sv["yc"], w["conf_ln_g"], w["conf_ln_b"], dhs, n + "ln_silu")
    da, dg, dbg, dcg, dxs, g["conf_dw_w"], g["conf_dw_b"], g["sc_dw_w"] = conv_bwd(
        sv["p_main"], w["conf_dw_w"], w["sc_dw_w"], dyc, dz, n + "conv")
    dq, dk, dv = attn_bwd(sv["q"], sv["k"], sv["v"], do, n + "attn")
    dp_qkv, dgqk = qkv_bwd(sv["p_main"], cos_t, sin_t, g_mat, w["gq"], w["gk"], dq, dk, dv, n + "qkv")
    dp_main = jnp.concatenate([dp_qkv, da, dg, dbg, dcg, dxs], axis=1)
    dh = mm(dp_main, w["wi_main"], tb=True, name=n + "d_h_main")
    dh = mm(dp_gate, w["wi_gate"], tb=True, acc=dh, name=n + "d_h_gate")
    g["wi_main"] = mm(sv["h"], dp_main, ta=True, name=n + "dw_in_main")
    g["wi_gate"] = mm(sv["h"], dp_gate, ta=True, name=n + "dw_in_gate")
    dx_in, dm01 = norm_mod_bwd(sv["x_in"], mods, dh, dx1, 1, n + "norm1")
    parts = jnp.concatenate([dm01, dm2, dm34, dm5], axis=2).reshape(2 * BL, NMOD * D)
    dmod, g["b_ada"] = dmod_assemble(parts, n + "dmod")
    g["w_ada"] = mm(cs, dmod, ta=True, name=n + "dw_ada")
    g["dcs"] = mm(dmod, w["w_ada"], tb=True, name=n + "d_cs")
    g["q_norm"] = dgqk[0, :AW].reshape(NQ, HD).sum(axis=0)
    g["k_norm"] = dgqk[0, OFF_K:OFF_K + KVW].reshape(NKV, HD).sum(axis=0)
    return dx_in, g


def kernel(x, c, ctx, c_ctx, w_ada, b_ada, w_in, q_norm, k_norm, w_attn_o, conf_dw_w, conf_dw_b, conf_ln_g, conf_ln_b, w_conf_out, sc_dw_w, w_sc_out, w_mix_out, w_ffn_in, w_ffn_out, loss_target, m_c_ctx, m_w_ada, m_b_ada, m_w_in, m_q_norm, m_k_norm, m_w_attn_o, m_conf_dw_w, m_conf_dw_b, m_conf_ln_g, m_conf_ln_b, m_w_conf_out, m_sc_dw_w, m_w_sc_out, m_w_mix_out, m_w_ffn_in, m_w_ffn_out, v_c_ctx, v_w_ada, v_b_ada, v_w_in, v_q_norm, v_k_norm, v_w_attn_o, v_conf_dw_w, v_conf_dw_b, v_conf_ln_g, v_conf_ln_b, v_w_conf_out, v_sc_dw_w, v_w_sc_out, v_w_mix_out, v_w_ffn_in, v_w_ffn_out):
    local = dict(c_ctx=c_ctx, w_ada=w_ada, b_ada=b_ada, w_in=w_in, q_norm=q_norm, k_norm=k_norm, w_attn_o=w_attn_o,
                 conf_dw_w=conf_dw_w, conf_dw_b=conf_dw_b, conf_ln_g=conf_ln_g, conf_ln_b=conf_ln_b,
                 w_conf_out=w_conf_out, sc_dw_w=sc_dw_w, w_sc_out=w_sc_out, w_mix_out=w_mix_out, w_ffn_in=w_ffn_in,
                 w_ffn_out=w_ffn_out)
    mom_m = dict(c_ctx=m_c_ctx, w_ada=m_w_ada, b_ada=m_b_ada, w_in=m_w_in, q_norm=m_q_norm, k_norm=m_k_norm,
                 w_attn_o=m_w_attn_o, conf_dw_w=m_conf_dw_w, conf_dw_b=m_conf_dw_b, conf_ln_g=m_conf_ln_g,
                 conf_ln_b=m_conf_ln_b, w_conf_out=m_w_conf_out, sc_dw_w=m_sc_dw_w, w_sc_out=m_w_sc_out,
                 w_mix_out=m_w_mix_out, w_ffn_in=m_w_ffn_in, w_ffn_out=m_w_ffn_out)
    mom_v = dict(c_ctx=v_c_ctx, w_ada=v_w_ada, b_ada=v_b_ada, w_in=v_w_in, q_norm=v_q_norm, k_norm=v_k_norm,
                 w_attn_o=v_w_attn_o, conf_dw_w=v_conf_dw_w, conf_dw_b=v_conf_dw_b, conf_ln_g=v_conf_ln_g,
                 conf_ln_b=v_conf_ln_b, w_conf_out=v_w_conf_out, sc_dw_w=v_sc_dw_w, w_sc_out=v_w_sc_out,
                 w_mix_out=v_w_mix_out, w_ffn_in=v_w_ffn_in, w_ffn_out=v_w_ffn_out)
    order = ("c_ctx", "w_ada", "b_ada", "w_in", "q_norm", "k_norm", "w_attn_o", "conf_dw_w", "conf_dw_b", "conf_ln_g",
             "conf_ln_b", "w_conf_out", "sc_dw_w", "w_sc_out", "w_mix_out", "w_ffn_in", "w_ffn_out")
    axis_of = dict(SHARDED)

    mm_shapes = [local[k].shape for k in MATMUL_W]
    mm_axes = [axis_of[k] for k in MATMUL_W]
    packed = _pack_rows([local[k] for k in MATMUL_W], 32).astype(BF16)
    full_mm = dict(zip(MATMUL_W, _join_shards(all_gather_shards(packed, "gather_weights"), mm_shapes, mm_axes)))
    conv_shapes = [local[k].shape for k in CONV_W]
    conv_axes = [axis_of[k] for k in CONV_W]
    conv_all = all_gather8(_pack_rows([local[k] for k in CONV_W], 8), "gather_conv_taps")
    full_conv = dict(zip(CONV_W, _join_shards(conv_all[0::2], conv_shapes, conv_axes)))

    loss_local, grad_x, full_g, small_g = local_step(x, c, ctx, c_ctx, full_mm, full_conv, b_ada, q_norm, k_norm,
                                                     conf_dw_b, conf_ln_g, conf_ln_b, loss_target)
    loss = lax.psum(loss_local, ("x", "y", "c"))

    sh_names = [k for k, _ in SHARDED]
    sh_axes = [ax for _, ax in SHARDED]
    gpack = _split_shards([full_g[k] for k in sh_names], sh_axes, 32)
    from_sibling = sibling_swap_half(gpack, "rs_sibling_swap")
    chip_sum = add_cast(gpack, from_sibling, lax.axis_index("c").astype(jnp.int32), "rs_chip_sum")
    arrived = chip_exchange(chip_sum, "rs_chip_exchange")
    g_half = sum_leading(arrived, "rs_sum")
    g_shard = sibling_join_halves(g_half, "rs_sibling_join")
    shard_g = dict(zip(sh_names, _unpack(g_shard, [local[k].shape for k in sh_names])))

    small_sum = sum_leading(all_gather8(_pack_rows([small_g[k] for k in SMALL], 8), "gather_small_grads"), "small_sum")
    small_g = dict(zip(SMALL, _unpack(small_sum, [local[k].shape for k in SMALL])))

    grad, delta, new_m, new_v = {}, {}, {}, {}
    for k in order:
        gk_ = shard_g[k] if k in shard_g else small_g[k]
        shp = local[k].shape
        view = (1, shp[0]) if len(shp) == 1 else (-1, shp[-1])
        d_, m_, v_ = adamw(local[k].reshape(view), gk_.reshape(view), mom_m[k].reshape(view), mom_v[k].reshape(view),
                           "adamw_" + k)
        grad[k], delta[k], new_m[k], new_v[k] = gk_, d_.reshape(shp), m_.reshape(shp), v_.reshape(shp)
    return (loss, grad_x, *[grad[k] for k in order], *[delta[k] for k in order], *[new_m[k] for k in order],
            *[new_v[k] for k in order])


def local_step(x, c, ctx, c_ctx, full_mm, full_conv, b_ada, q_norm, k_norm, conf_dw_b, conf_ln_g, conf_ln_b,
               loss_target):
    tabs = _rope_tables() + (_group_matrix(),)
    layer_w = []
    for i in range(DEPTH):
        wi = full_mm["w_in"][i]
        layer_w.append(dict(
            w_ada=full_mm["w_ada"][i],
            wi_main=jnp.concatenate([_perm_heads(wi[:, :AW], 1, HEAD_PERM), wi[:, AW:OFF_GATE]], axis=1),
            wi_gate=wi[:, OFF_GATE:],
            w_attn_o=_perm_heads(full_mm["w_attn_o"][i], 0, HEAD_PERM),
            w_conf_out=full_mm["w_conf_out"][i], w_sc_out=full_mm["w_sc_out"][i], w_mix_out=full_mm["w_mix_out"][i],
            w_ffn_in=full_mm["w_ffn_in"][i], w_ffn_out=full_mm["w_ffn_out"][i],
            conf_dw_w=full_conv["conf_dw_w"][i], sc_dw_w=full_conv["sc_dw_w"][i],
            conf_dw_b=conf_dw_b[i][None], conf_ln_g=conf_ln_g[i][None], conf_ln_b=conf_ln_b[i][None],
            gq=jnp.tile(q_norm[i], LANE // HD)[None], gk=jnp.tile(k_norm[i], LANE // HD)[None]))

    cin = jnp.concatenate([c, c_ctx[None], jnp.zeros((8 - BL - 1, D), F32)], axis=0)
    cs = silu_rows(cin, "silu_c")
    xs = jnp.concatenate([ctx, x], axis=1).reshape(NROW, D)
    saved = []
    for i in range(DEPTH):
        mods = mm(cs, layer_w[i]["w_ada"], bias=b_ada[i][None], name=f"l{i}_mod").reshape(8, 1, NMOD * D)
        xs, sv = _layer_fwd(i, xs, mods, layer_w[i], tabs)
        saved.append(sv)
    dxs, loss_lanes = loss_fwd_bwd(xs, loss_target.reshape(BL * SEQ, D), "loss")
    loss_local = 0.5 * jnp.sum(loss_lanes) / D

    grads = [None] * DEPTH
    for i in reversed(range(DEPTH)):
        dxs, grads[i] = _layer_bwd(i, dxs, saved[i], layer_w[i], tabs, cs)
    grad_x = dxs.reshape(BL, RE, D)[:, CTX:, :]
    dcin = silu_rows_bwd(cin, jnp.stack([grads[i]["dcs"] for i in range(DEPTH)]), "silu_c_bwd")

    def stack(key):
        return jnp.stack([grads[i][key] for i in range(DEPTH)])

    full_g = {k: stack(k) for k in ("w_ada", "w_attn_o", "w_conf_out", "w_sc_out", "w_mix_out", "w_ffn_in", "w_ffn_out",
                                    "conf_dw_w", "sc_dw_w")}
    full_g["w_attn_o"] = _perm_heads(full_g["w_attn_o"], 1, HEAD_INV)
    wi_main_g = stack("wi_main")
    full_g["w_in"] = jnp.concatenate([_perm_heads(wi_main_g[:, :, :AW], 2, HEAD_INV), wi_main_g[:, :, AW:],
                                      stack("wi_gate")], axis=2)

    small_g = dict(c_ctx=dcin[BL], b_ada=stack("b_ada").reshape(DEPTH, NMOD * D), q_norm=stack("q_norm"),
                   k_norm=stack("k_norm"), conf_dw_b=stack("conf_dw_b").reshape(DEPTH, CW),
                   conf_ln_g=stack("conf_ln_g").reshape(DEPTH, CW), conf_ln_b=stack("conf_ln_b").reshape(DEPTH, CW))
    return loss_local, grad_x, full_g, small_g
```

```python
import functools

import jax
import jax.numpy as jnp
from jax import lax
from jax.experimental import pallas as pl
from jax.experimental.pallas import tpu as pltpu

F32, BF16 = jnp.float32, jnp.bfloat16
HIGHEST = lax.Precision.HIGHEST

D = 1024
SEQ = 2048
CTX = 256
DEPTH = 4
BL = 4
GRID_W = 64
HD = 64
NQ = 8
NKV = 2
AW = NQ * HD
KVW = NKV * HD
CW = D // 2
CONF_K = 31
SC_K = 3
NMOD = 6
FH = -(-8 * D // (3 * 256)) * 256
EPS = 1e-6
ROPE_THETA = 10000.0
ATTN_SCALE = HD ** -0.5
OFF_K = AW
OFF_V = OFF_K + KVW
OFF_CONF = OFF_V + KVW
OFF_SC = OFF_CONF + 2 * CW
OFF_GATE = OFF_SC + 3 * CW
IN_W = OFF_GATE + 3 * D
QKVW = OFF_CONF
NCHIP = 4

ADAM_LR, ADAM_B1, ADAM_B2, ADAM_EPS, ADAM_WD, ADAM_STEP = 0.001, 0.9, 0.999, 1e-08, 0.01, 10

TM = CTX
RE = CTX + SEQ
TPE = RE // TM
NROW = BL * RE
NT = NROW // TM
LANE = 128
CB = CW // LANE
CONV_CH = 128
PADR = 16
VMEM_LIMIT = 52 * 1024 * 1024

MESH = pl.DeviceIdType.MESH
ANY = pl.BlockSpec(memory_space=pl.ANY)


def _pcall(body, *, name, out_shape, grid=(), in_specs=None, out_specs=None, scratch=(), sem=None):
    if not grid:
        return pl.pallas_call(body, name=name, out_shape=out_shape)
    params = pltpu.CompilerParams(dimension_semantics=sem, vmem_limit_bytes=VMEM_LIMIT)
    return pl.pallas_call(body, name=name, out_shape=out_shape, grid=grid, in_specs=in_specs, out_specs=out_specs,
                          scratch_shapes=list(scratch), compiler_params=params)


def _pick(n, cands):
    for t in cands:
        if n % t == 0:
            return t
    return n


def _seg(t):
    return jnp.where(t % TPE == 0, BL, t // TPE)


def _slot(t):
    return 2 * (t // TPE) + jnp.where(t % TPE == 0, 0, 1)


def _sigmoid(x):
    return 1.0 / (1.0 + jnp.exp(-x))


MM_BUDGET = 40 * 1024 * 1024
N_TILE_CAP = 1664


def _tile(n, cap=N_TILE_CAP):
    if n <= cap:
        return n
    for t in range(cap - cap % LANE, 0, -LANE):
        if n % t == 0:
            return t
    return n


def _row_tile(m, bytes_of):
    for tm in (1024, 512, 256, 128):
        if m % tm == 0 and bytes_of(tm) <= MM_BUDGET:
            return tm
    return m


def _w_dims(w):
    arr, _, kind = w
    if kind == "cols":
        return arr.shape[2], NCHIP * arr.shape[3]
    return arr.shape[-2], arr.shape[-1]


def _sz(dtype):
    return jnp.dtype(dtype).itemsize


def mm_nn(a, w, *, bias=None, out_dtype=F32, name):
    arr, layer, kind = w
    m, k = a.shape
    _, n = _w_dims(w)
    tn = _tile(arr.shape[3]) if kind == "cols" else _tile(n)
    tm = _row_tile(m, lambda t: 2 * (t * k * _sz(a.dtype) + k * tn * 2 + t * tn * _sz(out_dtype)))
    if kind == "mat":
        b_spec = pl.BlockSpec((k, tn), lambda j, i: (0, j))
    elif kind == "plain":
        b_spec = pl.BlockSpec((None, k, tn), lambda j, i: (layer, 0, j))
    else:
        per = arr.shape[3] // tn
        b_spec = pl.BlockSpec((None, None, k, tn), lambda j, i: (layer, j // per, 0, j % per))
    has_bias = bias is not None

    def body(*refs):
        out = jnp.dot(refs[0][...].astype(BF16), refs[1][...].astype(BF16), preferred_element_type=F32)
        if has_bias:
            out = out + refs[2][...]
        refs[-1][...] = out.astype(out_dtype)

    in_specs = [pl.BlockSpec((tm, k), lambda j, i: (i, 0)), b_spec]
    args = [a, arr]
    if has_bias:
        in_specs.append(pl.BlockSpec((1, tn), lambda j, i: (0, j)))
        args.append(bias)
    return _pcall(body, name=name, out_shape=jax.ShapeDtypeStruct((m, n), out_dtype), grid=(n // tn, m // tm),
                  in_specs=in_specs, out_specs=pl.BlockSpec((tm, tn), lambda j, i: (i, j)),
                  sem=("parallel", "parallel"))(*args)


def mm_nt(a, w, *, acc=None, out_dtype=F32, name):
    arr, layer, kind = w
    m, n = a.shape
    kdim, _ = _w_dims(w)
    has_acc = acc is not None
    tk = _tile(kdim, 1408)
    if kind == "cols":
        c = arr.shape[3]
        tm = _row_tile(m, lambda t: 2 * (t * c * _sz(a.dtype) + tk * c * 2 + t * tk * _sz(out_dtype)) + t * tk * 4)

        def body(a_ref, b_ref, o_ref, acc_ref):
            s = pl.program_id(2)

            @pl.when(s == 0)
            def _():
                acc_ref[...] = jnp.zeros_like(acc_ref)

            acc_ref[...] += lax.dot_general(a_ref[...].astype(BF16), b_ref[...], (((1,), (1,)), ((), ())),
                                            preferred_element_type=F32)

            @pl.when(s == NCHIP - 1)
            def _():
                o_ref[...] = acc_ref[...].astype(out_dtype)

        return _pcall(body, name=name, out_shape=jax.ShapeDtypeStruct((m, kdim), out_dtype),
                      grid=(kdim // tk, m // tm, NCHIP),
                      in_specs=[pl.BlockSpec((tm, c), lambda j, i, s: (i, s)),
                                pl.BlockSpec((None, None, tk, c), lambda j, i, s: (layer, s, j, 0))],
                      out_specs=pl.BlockSpec((tm, tk), lambda j, i, s: (i, j)),
                      scratch=[pltpu.VMEM((tm, tk), F32)],
                      sem=("parallel", "parallel", "arbitrary"))(a, arr)

    tm = _row_tile(m, lambda t: 2 * (t * n * _sz(a.dtype) + tk * n * 2 + t * tk * (_sz(out_dtype) + 4 * has_acc)))
    if kind == "mat":
        b_spec = pl.BlockSpec((tk, n), lambda j, i: (j, 0))
    else:
        b_spec = pl.BlockSpec((None, tk, n), lambda j, i: (layer, j, 0))

    def body(*refs):
        out = lax.dot_general(refs[0][...].astype(BF16), refs[1][...].astype(BF16), (((1,), (1,)), ((), ())),
                              preferred_element_type=F32)
        if has_acc:
            out = out + refs[2][...]
        refs[-1][...] = out.astype(out_dtype)

    in_specs = [pl.BlockSpec((tm, n), lambda j, i: (i, 0)), b_spec]
    args = [a, arr]
    if has_acc:
        in_specs.append(pl.BlockSpec((tm, tk), lambda j, i: (i, j)))
        args.append(acc)
    return _pcall(body, name=name, out_shape=jax.ShapeDtypeStruct((m, kdim), out_dtype), grid=(kdim // tk, m // tm),
                  in_specs=in_specs, out_specs=pl.BlockSpec((tm, tk), lambda j, i: (i, j)),
                  sem=("parallel", "parallel"))(*args)


def mm_tn(a, b, *, into=None, out_dtype=F32, name):
    rows, k = a.shape
    n = b.shape[1]
    kind = "mat" if into is None else into[2]
    if kind == "cols":
        buf, layer, _ = into
        c = buf.shape[3]
        tn, tk = _tile(c), k
        per = c // tn
        out_spec = pl.BlockSpec((None, None, tk, tn), lambda i, j, r: (layer, j // per, 0, j % per))
        odt = buf.dtype
    elif kind == "plain":
        buf, layer, _ = into
        tn, tk = _tile(n), _tile(k, 1408)
        out_spec = pl.BlockSpec((None, tk, tn), lambda i, j, r: (layer, i, j))
        odt = buf.dtype
    else:
        tn, tk = _tile(n), _tile(k, 1408)
        out_spec = pl.BlockSpec((tk, tn), lambda i, j, r: (i, j))
        odt = out_dtype
    tr = _row_tile(rows, lambda t: 2 * (t * tk * _sz(a.dtype) + t * tn * _sz(b.dtype) + tk * tn * _sz(odt)) + tk * tn * 4)
    nsteps = rows // tr

    def body(*refs):
        a_ref, b_ref = refs[0], refs[1]
        o_ref, acc_ref = refs[-2], refs[-1]
        r = pl.program_id(2)

        @pl.when(r == 0)
        def _():
            acc_ref[...] = jnp.zeros_like(acc_ref)

        acc_ref[...] += lax.dot_general(a_ref[...].astype(BF16), b_ref[...].astype(BF16), (((0,), (0,)), ((), ())),
                                        preferred_element_type=F32)

        @pl.when(r == nsteps - 1)
        def _():
            o_ref[...] = acc_ref[...].astype(odt)

    in_specs = [pl.BlockSpec((tr, tk), lambda i, j, r: (r, i)), pl.BlockSpec((tr, tn), lambda i, j, r: (r, j))]
    args = [a, b]
    if kind == "mat":
        out_shape = jax.ShapeDtypeStruct((k, n), odt)
        aliases = {}
    else:
        in_specs.append(ANY)
        args.append(buf)
        out_shape = jax.ShapeDtypeStruct(buf.shape, odt)
        aliases = {2: 0}
    return pl.pallas_call(body, name=name, out_shape=out_shape, grid=(k // tk, n // tn, nsteps), in_specs=in_specs,
                          out_specs=out_spec, scratch_shapes=[pltpu.VMEM((tk, tn), F32)], input_output_aliases=aliases,
                          compiler_params=pltpu.CompilerParams(
                              dimension_semantics=("parallel", "parallel", "arbitrary"),
                              vmem_limit_bytes=VMEM_LIMIT))(*args)


def _mods_spec():
    return pl.BlockSpec((1, 1, NMOD * D), lambda t: (_seg(t), 0, 0))


def _rows(width):
    return pl.BlockSpec((TM, width), lambda t: (t, 0))


def norm_mod_fwd(x, mods, k_sh, k_sc, name):
    def body(x_ref, m_ref, h_ref):
        x = x_ref[...]
        r = lax.rsqrt(jnp.mean(x * x, axis=-1, keepdims=True) + EPS)
        sh = m_ref[0, :, k_sh * D:(k_sh + 1) * D]
        sc = m_ref[0, :, k_sc * D:(k_sc + 1) * D]
        h_ref[...] = (x * r * (1.0 + sc) + sh).astype(BF16)

    return _pcall(body, name=name, out_shape=jax.ShapeDtypeStruct((NROW, D), BF16), grid=(NT,),
                  in_specs=[_rows(D), _mods_spec()], out_specs=_rows(D), sem=("parallel",))(x, mods)


def _accumulate_slot(t, ref, part):
    first = (t % TPE) <= 1

    @pl.when(first)
    def _():
        ref[0] = part

    @pl.when(jnp.logical_not(first))
    def _():
        ref[0] += part


def norm_mod_bwd(x, mods, dh, dres, k_sc, name):
    def body(x_ref, m_ref, dh_ref, dres_ref, dx_ref, dp_ref):
        t = pl.program_id(0)
        x = x_ref[...]
        r = lax.rsqrt(jnp.mean(x * x, axis=-1, keepdims=True) + EPS)
        xn = x * r
        sc = m_ref[0, :, k_sc * D:(k_sc + 1) * D]
        dh = dh_ref[...]
        dxn = dh * (1.0 + sc)
        dx_ref[...] = r * (dxn - xn * jnp.mean(dxn * xn, axis=-1, keepdims=True)) + dres_ref[...]
        part = jnp.concatenate([jnp.sum(dh, axis=0, keepdims=True), jnp.sum(dh * xn, axis=0, keepdims=True)], axis=1)
        _accumulate_slot(t, dp_ref, part)

    return _pcall(body, name=name,
                  out_shape=(jax.ShapeDtypeStruct((NROW, D), F32), jax.ShapeDtypeStruct((2 * BL, 1, 2 * D), F32)),
                  grid=(NT,), in_specs=[_rows(D), _mods_spec(), _rows(D), _rows(D)],
                  out_specs=(_rows(D), pl.BlockSpec((1, 1, 2 * D), lambda t: (_slot(t), 0, 0))),
                  sem=("arbitrary",))(x, mods, dh, dres)


def gate_resid_fwd(x, y, mods, k_g, name):
    def body(x_ref, y_ref, m_ref, o_ref):
        o_ref[...] = x_ref[...] + m_ref[0, :, k_g * D:(k_g + 1) * D] * y_ref[...]

    return _pcall(body, name=name, out_shape=jax.ShapeDtypeStruct((NROW, D), F32), grid=(NT,),
                  in_specs=[_rows(D), _rows(D), _mods_spec()], out_specs=_rows(D), sem=("parallel",))(x, y, mods)


def gate_resid_bwd(dx, y, mods, k_g, name):
    def body(dx_ref, y_ref, m_ref, dy_ref, dp_ref):
        t = pl.program_id(0)
        dx = dx_ref[...]
        dy_ref[...] = (dx * m_ref[0, :, k_g * D:(k_g + 1) * D]).astype(BF16)
        _accumulate_slot(t, dp_ref, jnp.sum(dx * y_ref[...], axis=0, keepdims=True))

    return _pcall(body, name=name,
                  out_shape=(jax.ShapeDtypeStruct((NROW, D), BF16), jax.ShapeDtypeStruct((2 * BL, 1, D), F32)),
                  grid=(NT,), in_specs=[_rows(D), _rows(D), _mods_spec()],
                  out_specs=(_rows(D), pl.BlockSpec((1, 1, D), lambda t: (_slot(t), 0, 0))),
                  sem=("arbitrary",))(dx, y, mods)


def _swap16(y, lo16):
    return jnp.where(lo16, pltpu.roll(y, LANE - 16, 1), pltpu.roll(y, 16, 1))


def _group_mean(v, g_mat):
    return jnp.dot(v, g_mat, precision=HIGHEST, preferred_element_type=F32)


def qkv_fwd(p_main, cos_t, sin_t, g_mat, gq, gk, name):
    def body(p_ref, cos_ref, sin_ref, g_ref, gq_ref, gk_ref, q_ref, k_ref, v_ref):
        cos, sin, g_mat_v = cos_ref[...], sin_ref[...], g_ref[...]
        lo16 = (lax.broadcasted_iota(jnp.int32, (TM, LANE), 1) % 32) < 16

        def block(xb, g):
            r = lax.rsqrt(_group_mean(xb * xb, g_mat_v) + EPS)
            y = xb * r * g
            return y * cos + _swap16(y, lo16) * sin

        for j in range(AW // LANE):
            q_ref[:, j * LANE:(j + 1) * LANE] = (block(p_ref[:, j * LANE:(j + 1) * LANE], gq_ref[...])
                                                 * ATTN_SCALE).astype(BF16)
        lo = lax.broadcasted_iota(jnp.int32, (TM, LANE), 1) < HD
        for src, dst_ref in ((block(p_ref[:, OFF_K:OFF_K + LANE], gk_ref[...]), k_ref), (p_ref[:, OFF_V:OFF_V + LANE], v_ref)):
            swapped = pltpu.roll(src, HD, 1)
            dst_ref[:, 0:LANE] = jnp.where(lo, src, swapped).astype(BF16)
            dst_ref[:, LANE:2 * LANE] = jnp.where(lo, swapped, src).astype(BF16)

    tab = pl.BlockSpec((TM, LANE), lambda t: (t % TPE, 0))
    small = pl.BlockSpec((1, LANE), lambda t: (0, 0))
    return _pcall(body, name=name,
                  out_shape=(jax.ShapeDtypeStruct((NROW, AW), BF16), jax.ShapeDtypeStruct((NROW, 2 * KVW), BF16),
                             jax.ShapeDtypeStruct((NROW, 2 * KVW), BF16)),
                  grid=(NT,),
                  in_specs=[_rows(QKVW), tab, tab, pl.BlockSpec((LANE, LANE), lambda t: (0, 0)), small, small],
                  out_specs=(_rows(AW), _rows(2 * KVW), _rows(2 * KVW)),
                  sem=("parallel",))(p_main, cos_t, sin_t, g_mat, gq, gk)


def qkv_bwd(p_main, cos_t, sin_t, g_mat, gq, gk, dq, dk, dv, name):
    def body(p_ref, cos_ref, sin_ref, g_ref, gq_ref, gk_ref, dq_ref, dk_ref, dv_ref, dp_ref, dg_ref):
        t = pl.program_id(0)
        cos, sin, g_mat_v = cos_ref[...], sin_ref[...], g_ref[...]
        lo16 = (lax.broadcasted_iota(jnp.int32, (TM, LANE), 1) % 32) < 16

        def block(xb, g, dyr):
            r = lax.rsqrt(_group_mean(xb * xb, g_mat_v) + EPS)
            xn = xb * r
            dy = dyr * cos + _swap16(dyr * sin, lo16)
            dgl = jnp.sum(dy * xn, axis=0, keepdims=True)
            dxn = dy * g
            return r * (dxn - xn * _group_mean(dxn * xn, g_mat_v)), dgl

        parts = []
        for j in range(AW // LANE):
            sl = slice(j * LANE, (j + 1) * LANE)
            dxb, dgl = block(p_ref[:, sl], gq_ref[...], dq_ref[:, sl] * ATTN_SCALE)
            dp_ref[:, sl] = dxb.astype(BF16)
            parts.append(dgl)
        lo = lax.broadcasted_iota(jnp.int32, (TM, LANE), 1) < HD

        def fold(d_ref):
            d0, d1 = d_ref[:, 0:LANE], d_ref[:, LANE:2 * LANE]
            return jnp.where(lo, d0 + pltpu.roll(d0, HD, 1), d1 + pltpu.roll(d1, HD, 1))

        dxb, dgl = block(p_ref[:, OFF_K:OFF_K + LANE], gk_ref[...], fold(dk_ref))
        dp_ref[:, OFF_K:OFF_K + LANE] = dxb.astype(BF16)
        parts.append(dgl)
        parts.append(jnp.zeros((1, LANE), F32))
        dp_ref[:, OFF_V:OFF_V + LANE] = fold(dv_ref).astype(BF16)
        part = jnp.concatenate(parts, axis=1)

        @pl.when(t == 0)
        def _():
            dg_ref[...] = part

        @pl.when(t != 0)
        def _():
            dg_ref[...] += part

    tab = pl.BlockSpec((TM, LANE), lambda t: (t % TPE, 0))
    small = pl.BlockSpec((1, LANE), lambda t: (0, 0))
    return _pcall(body, name=name,
                  out_shape=(jax.ShapeDtypeStruct((NROW, QKVW), BF16), jax.ShapeDtypeStruct((1, QKVW), F32)),
                  grid=(NT,),
                  in_specs=[_rows(QKVW), tab, tab, pl.BlockSpec((LANE, LANE), lambda t: (0, 0)), small, small,
                            _rows(AW), _rows(2 * KVW), _rows(2 * KVW)],
                  out_specs=(_rows(QKVW), pl.BlockSpec((1, QKVW), lambda t: (0, 0))),
                  sem=("arbitrary",))(p_main, cos_t, sin_t, g_mat, gq, gk, dq, dk, dv)


def _layer_norm_parts(yc):
    mu = jnp.mean(yc, axis=-1, keepdims=True)
    xc = yc - mu
    rs = lax.rsqrt(jnp.mean(xc * xc, axis=-1, keepdims=True) + EPS)
    return xc * rs, rs


def ln_silu_fwd(yc, g, b, name):
    def body(y_ref, g_ref, b_ref, o_ref):
        nrm, _ = _layer_norm_parts(y_ref[...])
        ln = nrm * g_ref[...] + b_ref[...]
        o_ref[...] = (ln * _sigmoid(ln)).astype(BF16)

    vec = pl.BlockSpec((1, CW), lambda t: (0, 0))
    return _pcall(body, name=name, out_shape=jax.ShapeDtypeStruct((NROW, CW), BF16), grid=(NT,),
                  in_specs=[_rows(CW), vec, vec], out_specs=_rows(CW), sem=("parallel",))(yc, g, b)


def ln_silu_bwd(yc, g, b, dhs, name):
    def body(y_ref, g_ref, b_ref, dh_ref, dy_ref, dg_ref, db_ref):
        t = pl.program_id(0)
        nrm, rs = _layer_norm_parts(y_ref[...])
        ln = nrm * g_ref[...] + b_ref[...]
        sg = _sigmoid(ln)
        dln = dh_ref[...] * (sg * (1.0 + ln * (1.0 - sg)))
        dn = dln * g_ref[...]
        dy_ref[...] = rs * (dn - jnp.mean(dn, axis=-1, keepdims=True)
                            - nrm * jnp.mean(dn * nrm, axis=-1, keepdims=True))
        pg = jnp.sum(dln * nrm, axis=0, keepdims=True)
        pb = jnp.sum(dln, axis=0, keepdims=True)

        @pl.when(t == 0)
        def _():
            dg_ref[...] = pg
            db_ref[...] = pb

        @pl.when(t != 0)
        def _():
            dg_ref[...] += pg
            db_ref[...] += pb

    vec = pl.BlockSpec((1, CW), lambda t: (0, 0))
    return _pcall(body, name=name,
                  out_shape=(jax.ShapeDtypeStruct((NROW, CW), F32), jax.ShapeDtypeStruct((1, CW), F32),
                             jax.ShapeDtypeStruct((1, CW), F32)),
                  grid=(NT,), in_specs=[_rows(CW), vec, vec, _rows(CW)], out_specs=(_rows(CW), vec, vec),
                  sem=("arbitrary",))(yc, g, b, dhs)


def gate_merge_fwd(p_gate, ya, yb, ys, name):
    def body(p_ref, a_ref, b_ref, s_ref, o_ref):
        out = _sigmoid(p_ref[:, 0:D]) * a_ref[...]
        out += _sigmoid(p_ref[:, D:2 * D]) * b_ref[...]
        out += _sigmoid(p_ref[:, 2 * D:3 * D]) * s_ref[...]
        o_ref[...] = out.astype(BF16)

    return _pcall(body, name=name, out_shape=jax.ShapeDtypeStruct((NROW, D), BF16), grid=(NT,),
                  in_specs=[_rows(3 * D), _rows(D), _rows(D), _rows(D)], out_specs=_rows(D),
                  sem=("parallel",))(p_gate, ya, yb, ys)


def gate_merge_bwd(p_gate, ya, yb, ys, dmerged, name):
    def body(p_ref, a_ref, b_ref, s_ref, dm_ref, da_ref, db_ref, ds_ref, dp_ref):
        dm = dm_ref[...]
        for i, (y_ref, dy_ref) in enumerate(((a_ref, da_ref), (b_ref, db_ref), (s_ref, ds_ref))):
            g = _sigmoid(p_ref[:, i * D:(i + 1) * D])
            dy_ref[...] = (dm * g).astype(BF16)
            dp_ref[:, i * D:(i + 1) * D] = (dm * y_ref[...] * g * (1.0 - g)).astype(BF16)

    return _pcall(body, name=name,
                  out_shape=(jax.ShapeDtypeStruct((NROW, D), BF16),) * 3 + (jax.ShapeDtypeStruct((NROW, 3 * D), BF16),),
                  grid=(NT,), in_specs=[_rows(3 * D), _rows(D), _rows(D), _rows(D), _rows(D)],
                  out_specs=(_rows(D), _rows(D), _rows(D), _rows(3 * D)), sem=("parallel",))(p_gate, ya, yb, ys, dmerged)


def swiglu_fwd(u, name):
    def body(u_ref, f_ref):
        a = u_ref[:, 0:FH]
        f_ref[...] = (a * _sigmoid(a) * u_ref[:, FH:2 * FH]).astype(BF16)

    return _pcall(body, name=name, out_shape=jax.ShapeDtypeStruct((NROW, FH), BF16), grid=(NT,),
                  in_specs=[_rows(2 * FH)], out_specs=_rows(FH), sem=("parallel",))(u)


def swiglu_bwd(u, df, name):
    def body(u_ref, df_ref, du_ref):
        a, b, df_v = u_ref[:, 0:FH], u_ref[:, FH:2 * FH], df_ref[...]
        sg = _sigmoid(a)
        du_ref[:, 0:FH] = (df_v * b * (sg * (1.0 + a * (1.0 - sg)))).astype(BF16)
        du_ref[:, FH:2 * FH] = (df_v * a * sg).astype(BF16)

    return _pcall(body, name=name, out_shape=jax.ShapeDtypeStruct((NROW, 2 * FH), BF16), grid=(NT,),
                  in_specs=[_rows(2 * FH), _rows(FH)], out_specs=_rows(2 * FH), sem=("parallel",))(u, df)


def loss_fwd_bwd(y, target, name):
    def body(y_ref, t_ref, dy_ref, l_ref):
        t = pl.program_id(0)
        latent = (t % TPE) != 0
        err = jnp.where(latent, y_ref[...] - t_ref[...], 0.0)
        dy_ref[...] = err * (1.0 / D)
        part = jnp.sum(err * err, axis=0, keepdims=True)

        @pl.when(t == 0)
        def _():
            l_ref[...] = part

        @pl.when(t != 0)
        def _():
            l_ref[...] += part

    tgt = pl.BlockSpec((TM, D), lambda t: ((t // TPE) * (TPE - 1) + jnp.maximum(t % TPE - 1, 0), 0))
    return _pcall(body, name=name,
                  out_shape=(jax.ShapeDtypeStruct((NROW, D), F32), jax.ShapeDtypeStruct((1, D), F32)),
                  grid=(NT,), in_specs=[_rows(D), tgt], out_specs=(_rows(D), pl.BlockSpec((1, D), lambda t: (0, 0))),
                  sem=("arbitrary",))(y, target)


QB_PER_KV = AW // LANE // NKV


def _softmax_parts(qm, k):
    s = lax.dot_general(qm, k, (((1,), (1,)), ((), ())), preferred_element_type=F32)
    e = jnp.exp(s - jnp.max(s, axis=-1, keepdims=True))
    return e, 1.0 / jnp.sum(e, axis=-1, keepdims=True)


def _attn_specs():
    qs = pl.BlockSpec((TM, LANE), lambda b, h, t, j: (b * TPE + t, h * QB_PER_KV + j))
    ks = pl.BlockSpec((RE, LANE), lambda b, h, t, j: (b, h))
    return qs, ks


def _lane_halves():
    lo = lax.broadcasted_iota(jnp.int32, (TM, LANE), 1) < HD
    return lo, jnp.logical_not(lo)


def attn_fwd(q, k, v, name):
    def body(q_ref, k_ref, v_ref, o_ref):
        t = pl.program_id(2)
        halves = _lane_halves()
        qv = q_ref[...]

        def run(nk):
            kv, vv = k_ref[0:nk, :], v_ref[0:nk, :]
            outs = []
            for sel in halves:
                e, rinv = _softmax_parts(jnp.where(sel, qv, jnp.zeros_like(qv)), kv)
                outs.append(jnp.dot(e.astype(BF16), vv, preferred_element_type=F32) * rinv)
            o_ref[...] = jnp.where(halves[0], outs[0], outs[1]).astype(BF16)

        @pl.when(t == 0)
        def _():
            run(CTX)

        @pl.when(t != 0)
        def _():
            run(RE)

    qs, ks = _attn_specs()
    return _pcall(body, name=name, out_shape=jax.ShapeDtypeStruct((NROW, AW), BF16), grid=(BL, NKV, TPE, QB_PER_KV),
                  in_specs=[qs, ks, ks], out_specs=qs, sem=("parallel",) * 4)(q, k, v)


def attn_bwd(q, k, v, do, name):
    def body(q_ref, k_ref, v_ref, do_ref, dq_ref, dk_ref, dv_ref):
        t, j = pl.program_id(2), pl.program_id(3)
        halves = _lane_halves()
        qv, dov = q_ref[...], do_ref[...]

        @pl.when(jnp.logical_and(t == 0, j == 0))
        def _():
            dk_ref[...] = jnp.zeros_like(dk_ref)
            dv_ref[...] = jnp.zeros_like(dv_ref)

        def run(nk):
            kv, vv = k_ref[0:nk, :], v_ref[0:nk, :]
            dqs = []
            for sel in halves:
                qm = jnp.where(sel, qv, jnp.zeros_like(qv))
                dom = jnp.where(sel, dov, jnp.zeros_like(dov))
                e, rinv = _softmax_parts(qm, kv)
                p = e * rinv
                dv_ref[0:nk, :] += lax.dot_general(p.astype(BF16), dom, (((0,), (0,)), ((), ())),
                                                   preferred_element_type=F32)
                dp = lax.dot_general(dom, vv, (((1,), (1,)), ((), ())), preferred_element_type=F32)
                ds = (p * (dp - jnp.sum(dp * p, axis=-1, keepdims=True))).astype(BF16)
                dqs.append(jnp.dot(ds, kv, preferred_element_type=F32))
                dk_ref[0:nk, :] += lax.dot_general(ds, qm, (((0,), (0,)), ((), ())), preferred_element_type=F32)
            dq_ref[...] = jnp.where(halves[0], dqs[0], dqs[1])

        @pl.when(t == 0)
        def _():
            run(CTX)

        @pl.when(t != 0)
        def _():
            run(RE)

    qs, ks = _attn_specs()
    return _pcall(body, name=name,
                  out_shape=(jax.ShapeDtypeStruct((NROW, AW), F32), jax.ShapeDtypeStruct((NROW, 2 * KVW), F32),
                             jax.ShapeDtypeStruct((NROW, 2 * KVW), F32)),
                  grid=(BL, NKV, TPE, QB_PER_KV), in_specs=[qs, ks, ks, qs], out_specs=(qs, ks, ks),
                  sem=("parallel", "parallel", "arbitrary", "arbitrary"))(q, k, v, do)


CONV_SEGS = ((0, CTX), (CTX, SEQ))


def _p_block(col0):
    return pl.BlockSpec((RE, LANE), lambda cb, b: (b, col0 // LANE + cb))


def _conv_io(width):
    return pl.BlockSpec((RE, LANE), lambda cb, b: (b, cb))


def _taps(n):
    return pl.BlockSpec((n, LANE), lambda cb, b: (0, cb))


def _fill_pad(pad_ref, length, values):
    pad_ref[0:PADR, :] = jnp.zeros((PADR, LANE), F32)
    pad_ref[PADR + length:2 * PADR + length, :] = jnp.zeros((PADR, LANE), F32)
    pad_ref[PADR:PADR + length, :] = values


def _conv_chunk(pad_ref, w_ref, ntap, c0, first_row):
    acc = jnp.zeros((CONV_CH, LANE), F32)
    for kk in range(ntap):
        r0 = c0 + first_row(kk)
        acc += w_ref[kk:kk + 1, :] * pad_ref[r0:r0 + CONV_CH, :]
    return acc


def conv_fwd(p_main, wdw, bdw, w3, name):
    def body(a_ref, g_ref, bg_ref, cg_ref, xs_ref, w_ref, b_ref, w3_ref, yc_ref, z_ref, pad_ref):
        for off, length in CONV_SEGS:
            rows = slice(off, off + length)
            _fill_pad(pad_ref, length, a_ref[rows, :] * _sigmoid(g_ref[rows, :]))
            for c0 in range(0, length, CONV_CH):
                acc = _conv_chunk(pad_ref, w_ref, CONF_K, c0, lambda kk: PADR + kk - CONF_K // 2)
                yc_ref[off + c0:off + c0 + CONV_CH, :] = acc + b_ref[...]
            pad_ref[PADR:PADR + length, :] = cg_ref[rows, :] * xs_ref[rows, :]
            for c0 in range(0, length, CONV_CH):
                acc = _conv_chunk(pad_ref, w3_ref, SC_K, c0, lambda kk: PADR + kk - SC_K // 2)
                z_ref[off + c0:off + c0 + CONV_CH, :] = (bg_ref[off + c0:off + c0 + CONV_CH, :] * acc).astype(BF16)

    return _pcall(body, name=name,
                  out_shape=(jax.ShapeDtypeStruct((NROW, CW), F32), jax.ShapeDtypeStruct((NROW, CW), BF16)),
                  grid=(CB, BL),
                  in_specs=[_p_block(OFF_CONF), _p_block(OFF_CONF + CW), _p_block(OFF_SC), _p_block(OFF_SC + CW),
                            _p_block(OFF_SC + 2 * CW), _taps(CONF_K), _taps(1), _taps(SC_K)],
                  out_specs=(_conv_io(CW), _conv_io(CW)),
                  scratch=[pltpu.VMEM((SEQ + 2 * PADR, LANE), F32)],
                  sem=("parallel", "parallel"))(p_main, p_main, p_main, p_main, p_main, wdw, bdw, w3)


def _tap_grad(pad_ref, d_ref, off, length, first_row):
    acc = jnp.zeros((8, LANE), F32)
    for c0 in range(0, length, CONV_CH):
        prod = d_ref[off + c0:off + c0 + CONV_CH, :] * pad_ref[c0 + first_row:c0 + first_row + CONV_CH, :]
        acc += jnp.sum(prod.reshape(CONV_CH // 8, 8, LANE), axis=0)
    return jnp.sum(acc, axis=0, keepdims=True)


def conv_bwd(p_main, wdw, w3, dyc, dz, name):
    def body(a_ref, g_ref, bg_ref, cg_ref, xs_ref, w_ref, w3_ref, dyc_ref, dz_ref,
             da_ref, dg_ref, dbg_ref, dcg_ref, dxs_ref, dw_ref, db_ref, dw3_ref, pad_x, pad_d, dconv_ref):
        b = pl.program_id(1)

        @pl.when(b == 0)
        def _():
            dw_ref[...] = jnp.zeros_like(dw_ref)
            db_ref[...] = jnp.zeros_like(db_ref)
            dw3_ref[...] = jnp.zeros_like(dw3_ref)

        db_ref[...] += jnp.sum(dyc_ref[...], axis=0, keepdims=True)
        for off, length in CONV_SEGS:
            rows = slice(off, off + length)
            _fill_pad(pad_x, length, a_ref[rows, :] * _sigmoid(g_ref[rows, :]))
            _fill_pad(pad_d, length, dyc_ref[rows, :])
            for kk in range(CONF_K):
                dw_ref[kk:kk + 1, :] += _tap_grad(pad_x, dyc_ref, off, length, PADR + kk - CONF_K // 2)
            for c0 in range(0, length, CONV_CH):
                dh = _conv_chunk(pad_d, w_ref, CONF_K, c0, lambda kk: PADR + CONF_K // 2 - kk)
                ch = slice(off + c0, off + c0 + CONV_CH)
                sg = _sigmoid(g_ref[ch, :])
                da_ref[ch, :] = (dh * sg).astype(BF16)
                dg_ref[ch, :] = (dh * a_ref[ch, :] * sg * (1.0 - sg)).astype(BF16)
            pad_x[PADR:PADR + length, :] = cg_ref[rows, :] * xs_ref[rows, :]
            dconv_ref[rows, :] = dz_ref[rows, :] * bg_ref[rows, :]
            pad_d[PADR:PADR + length, :] = dconv_ref[rows, :]
            for kk in range(SC_K):
                dw3_ref[kk:kk + 1, :] += _tap_grad(pad_x, dconv_ref, off, length, PADR + kk - SC_K // 2)
            for c0 in range(0, length, CONV_CH):
                ch = slice(off + c0, off + c0 + CONV_CH)
                c3 = _conv_chunk(pad_x, w3_ref, SC_K, c0, lambda kk: PADR + kk - SC_K // 2)
                dbg_ref[ch, :] = (dz_ref[ch, :] * c3).astype(BF16)
                dcx = _conv_chunk(pad_d, w3_ref, SC_K, c0, lambda kk: PADR + SC_K // 2 - kk)
                dcg_ref[ch, :] = (dcx * xs_ref[ch, :]).astype(BF16)
                dxs_ref[ch, :] = (dcx * cg_ref[ch, :]).astype(BF16)

    slab = jax.ShapeDtypeStruct((NROW, CW), BF16)
    return _pcall(body, name=name,
                  out_shape=(slab,) * 5 + (jax.ShapeDtypeStruct((CONF_K, CW), F32), jax.ShapeDtypeStruct((1, CW), F32),
                                           jax.ShapeDtypeStruct((SC_K, CW), F32)),
                  grid=(CB, BL),
                  in_specs=[_p_block(OFF_CONF), _p_block(OFF_CONF + CW), _p_block(OFF_SC), _p_block(OFF_SC + CW),
                            _p_block(OFF_SC + 2 * CW), _taps(CONF_K), _taps(SC_K), _conv_io(CW), _conv_io(CW)],
                  out_specs=(_conv_io(CW),) * 5 + (_taps(CONF_K), _taps(1), _taps(SC_K)),
                  scratch=[pltpu.VMEM((SEQ + 2 * PADR, LANE), F32), pltpu.VMEM((SEQ + 2 * PADR, LANE), F32),
                           pltpu.VMEM((RE, LANE), F32)],
                  sem=("parallel", "arbitrary"))(p_main, p_main, p_main, p_main, p_main, wdw, w3, dyc, dz)


def silu_rows(x, name):
    def body(x_ref, o_ref):
        o_ref[...] = x_ref[...] * _sigmoid(x_ref[...])

    return _pcall(body, name=name, out_shape=jax.ShapeDtypeStruct(x.shape, F32))(x)


def silu_rows_bwd(x, dcs, name):
    def body(x_ref, d_ref, o_ref):
        x = x_ref[...]
        sg = _sigmoid(x)
        tot = d_ref[0]
        for i in range(1, DEPTH):
            tot += d_ref[i]
        o_ref[...] = tot * (sg * (1.0 + x * (1.0 - sg)))

    return _pcall(body, name=name, out_shape=jax.ShapeDtypeStruct(x.shape, F32))(x, dcs)


def dmod_assemble(parts, name):
    def body(p_ref, dm_ref, db_ref):
        row = lax.broadcasted_iota(jnp.int32, (8, NMOD * D), 0)
        dm = jnp.zeros((8, NMOD * D), F32)
        db = jnp.zeros((1, NMOD * D), F32)
        for s in range(2 * BL):
            target = BL if s % 2 == 0 else s // 2
            part = p_ref[s:s + 1, :]
            dm += jnp.where(row == target, part, 0.0)
            db += part
        dm_ref[...] = dm
        db_ref[...] = db

    return _pcall(body, name=name, out_shape=(jax.ShapeDtypeStruct((8, NMOD * D), F32),
                                              jax.ShapeDtypeStruct((1, NMOD * D), F32)))(parts)


def sum_leading(x, name):
    n = x.shape[0]
    tr = _pick(x.shape[1], (256, 32, 8))

    def body(x_ref, o_ref):
        tot = x_ref[0].astype(F32)
        for i in range(1, n):
            tot += x_ref[i].astype(F32)
        o_ref[...] = tot

    return _pcall(body, name=name, out_shape=jax.ShapeDtypeStruct(x.shape[1:], F32), grid=(x.shape[1] // tr,),
                  in_specs=[pl.BlockSpec((n, tr, x.shape[2]), lambda i: (0, i, 0))],
                  out_specs=pl.BlockSpec((tr, x.shape[2]), lambda i: (i, 0)), sem=("parallel",))(x)


LH = DEPTH // 2
SLAB_ROWS = (256, 176, 128, 64, 8)


def _prefetch_call(body, name, out_shape, grid, in_specs, out_specs, sem, scalars, *args):
    spec = pltpu.PrefetchScalarGridSpec(num_scalar_prefetch=len(scalars), grid=grid, in_specs=in_specs,
                                        out_specs=out_specs)
    return pl.pallas_call(body, name=name, out_shape=out_shape, grid_spec=spec,
                          compiler_params=pltpu.CompilerParams(dimension_semantics=sem,
                                                               vmem_limit_bytes=VMEM_LIMIT))(*scalars, *args)


def cast_into_slot(w, chip, name):
    depth, r, c = w.shape
    tr = _pick(r, SLAB_ROWS)

    def body(s_ref, w_ref, o_ref):
        o_ref[...] = w_ref[...].astype(BF16)

    return _prefetch_call(body, name, jax.ShapeDtypeStruct((depth, NCHIP, r, c), BF16), (depth, r // tr),
                          [pl.BlockSpec((None, tr, c), lambda l, i, s: (l, i, 0))],
                          pl.BlockSpec((None, None, tr, c), lambda l, i, s: (l, s[0], i, 0)),
                          ("parallel", "parallel"), (chip,), w)


def rs_add(g, other, core, chip, name):
    _, _, r, c = g.shape
    tr = _pick(r, SLAB_ROWS)

    def body(core_ref, chip_ref, g_ref, o_ref, send_ref, arr_ref):
        k = pl.program_id(2)
        tot = (g_ref[...].astype(F32) + o_ref[...].astype(F32)).astype(BF16)
        send_ref[...] = tot

        @pl.when(k == chip_ref[0])
        def _():
            arr_ref[...] = tot

    blk = (None, None, tr, c)
    return _prefetch_call(
        body, name, (jax.ShapeDtypeStruct(other.shape, BF16), jax.ShapeDtypeStruct(g.shape, BF16)),
        (LH, r // tr, NCHIP),
        [pl.BlockSpec(blk, lambda l, i, k, cr, ch: (cr[0] * LH + l, k, i, 0)),
         pl.BlockSpec(blk, lambda l, i, k, cr, ch: (l, k, i, 0))],
        (pl.BlockSpec(blk, lambda l, i, k, cr, ch: (l, k, i, 0)),
         pl.BlockSpec(blk, lambda l, i, k, cr, ch: (cr[0] * LH + l, ch[0], i, 0))),
        ("parallel", "parallel", "arbitrary"), (core, chip), g, other)


def adamw_sum(w, arr, m, v, name):
    depth, r, c = w.shape
    tr = _pick(r, SLAB_ROWS)
    c1 = 1.0 / (1.0 - ADAM_B1 ** ADAM_STEP)
    c2 = 1.0 / (1.0 - ADAM_B2 ** ADAM_STEP)

    def body(w_ref, a_ref, m_ref, v_ref, g_ref, d_ref, mo_ref, vo_ref):
        gv = a_ref[0].astype(F32)
        for k in range(1, NCHIP):
            gv += a_ref[k].astype(F32)
        mn = ADAM_B1 * m_ref[...] + (1.0 - ADAM_B1) * gv
        vn = ADAM_B2 * v_ref[...] + (1.0 - ADAM_B2) * (gv * gv)
        g_ref[...] = gv
        d_ref[...] = -ADAM_LR * ((mn * c1) / (jnp.sqrt(vn * c2) + ADAM_EPS) + ADAM_WD * w_ref[...])
        mo_ref[...] = mn
        vo_ref[...] = vn

    spec = pl.BlockSpec((None, tr, c), lambda l, i: (l, i, 0))
    sds = jax.ShapeDtypeStruct(w.shape, F32)
    return _pcall(body, name=name, out_shape=(sds,) * 4, grid=(depth, r // tr),
                  in_specs=[spec, pl.BlockSpec((None, NCHIP, tr, c), lambda l, i: (l, 0, i, 0)), spec, spec],
                  out_specs=(spec,) * 4, sem=("parallel", "parallel"))(w, arr, m, v)


def adamw(w, g, m, v, name):
    rows, cols = w.shape
    tr = _pick(rows, (256, 248, 128, 8))
    c1 = 1.0 / (1.0 - ADAM_B1 ** ADAM_STEP)
    c2 = 1.0 / (1.0 - ADAM_B2 ** ADAM_STEP)

    def body(w_ref, g_ref, m_ref, v_ref, d_ref, mo_ref, vo_ref):
        gv = g_ref[...]
        mn = ADAM_B1 * m_ref[...] + (1.0 - ADAM_B1) * gv
        vn = ADAM_B2 * v_ref[...] + (1.0 - ADAM_B2) * (gv * gv)
        d_ref[...] = -ADAM_LR * ((mn * c1) / (jnp.sqrt(vn * c2) + ADAM_EPS) + ADAM_WD * w_ref[...])
        mo_ref[...] = mn
        vo_ref[...] = vn

    spec = pl.BlockSpec((tr, cols), lambda i: (i, 0))
    sds = jax.ShapeDtypeStruct((rows, cols), F32)
    return _pcall(body, name=name, out_shape=(sds, sds, sds), grid=(rows // tr,), in_specs=[spec] * 4,
                  out_specs=(spec, spec, spec), sem=("parallel",))(w, g, m, v)


def _place():
    return lax.axis_index("x"), lax.axis_index("y"), lax.axis_index("c")


def _other_chips(x, y):
    return [(1 - x, y), (x, 1 - y), (1 - x, 1 - y)]


def _comm_call(body, name, out_shape, n_in, nsem):
    return pl.pallas_call(body, name=name, out_shape=out_shape, in_specs=[ANY] * n_in,
                          out_specs=jax.tree.map(lambda _: ANY, out_shape),
                          scratch_shapes=[pltpu.SemaphoreType.DMA((nsem,)), pltpu.SemaphoreType.DMA((nsem,)),
                                          pltpu.SemaphoreType.DMA])


def all_gather8(block, name):
    def body(x_ref, out_ref, send_sems, recv_sems, local_sem):
        x, y, c = _place()
        me, sibling = (x, y, c), (x, y, 1 - c)
        chips = _other_chips(x, y)

        def slot(px, py, pc):
            return out_ref.at[4 * px + 2 * py + pc]

        def copy(k, blk, to, src=None):
            return pltpu.make_async_remote_copy(src_ref=slot(*blk) if src is None else src, dst_ref=slot(*blk),
                                                send_sem=send_sems.at[k], recv_sem=recv_sems.at[k],
                                                device_id=to, device_id_type=MESH)

        mine = pltpu.make_async_copy(x_ref, slot(*me), local_sem)
        mine.start()
        first = [copy(0, me, sibling, src=x_ref)]
        first += [copy(1 + j, me, (*chip, c), src=x_ref) for j, chip in enumerate(chips)]
        for cp in first:
            cp.start()
        passed = [copy(4 + j, (*chip, c), sibling) for j, chip in enumerate(chips)]
        for j, chip in enumerate(chips):
            copy(1 + j, (*chip, c), me).wait_recv()
            passed[j].start()
        copy(0, sibling, me).wait_recv()
        for j, chip in enumerate(chips):
            copy(4 + j, (*chip, 1 - c), me).wait_recv()
        for cp in first + passed:
            cp.wait_send()
        mine.wait()

    return _comm_call(body, name, jax.ShapeDtypeStruct((8,) + block.shape, block.dtype), 1, 7)(block)


def _inplace_comm_call(body, name, n_pass, extra, nsem):
    def call(*args):
        out_shape = tuple(jax.ShapeDtypeStruct(a.shape, a.dtype) for a in args[extra:])
        return pl.pallas_call(body, name=name, out_shape=out_shape, in_specs=[ANY] * len(args),
                              out_specs=tuple(ANY for _ in out_shape),
                              input_output_aliases={extra + i: i for i in range(n_pass)},
                              scratch_shapes=[pltpu.SemaphoreType.DMA((nsem,)), pltpu.SemaphoreType.DMA((nsem,))])(*args)
    return call


def _remote(src, dst, send_sems, recv_sems, k, to):
    return pltpu.make_async_remote_copy(src_ref=src, dst_ref=dst, send_sem=send_sems.at[k], recv_sem=recv_sems.at[k],
                                        device_id=to, device_id_type=MESH)


def gather_weights(bufs, name):
    n = len(bufs)

    def body(*refs):
        outs = refs[n:2 * n]
        send_sems, recv_sems = refs[2 * n:]
        x, y, c = _place()
        sibling = (x, y, 1 - c)
        chips = _other_chips(x, y)
        own = 2 * x + y

        def slab(w, core, chip_slot):
            return outs[w].at[pl.ds(core * LH, LH), chip_slot]

        started = []
        for w in range(n):
            for j, chip in enumerate(chips):
                cp = _remote(slab(w, c, own), slab(w, c, own), send_sems, recv_sems, 6 * w + j, (*chip, c))
                cp.start()
                started.append(cp)
        for j, chip in enumerate(chips):
            theirs = 2 * chip[0] + chip[1]
            for w in range(n):
                _remote(slab(w, c, own), slab(w, c, theirs), send_sems, recv_sems, 6 * w + j, (*chip, c)).wait_recv()
                cp = _remote(slab(w, c, theirs), slab(w, c, theirs), send_sems, recv_sems, 6 * w + 3 + j, sibling)
                cp.start()
                started.append(cp)
        for j, chip in enumerate(chips):
            theirs = 2 * chip[0] + chip[1]
            for w in range(n):
                _remote(slab(w, c, own), slab(w, 1 - c, theirs), send_sems, recv_sems, 6 * w + 3 + j, sibling).wait_recv()
        for cp in started:
            cp.wait_send()

    return _inplace_comm_call(body, name, n, 0, 6 * n)(*bufs)


def rs_swap(grads, name):
    n = len(grads)

    def body(*refs):
        ins, outs = refs[:n], refs[n:2 * n]
        send_sems, recv_sems = refs[2 * n:]
        x, y, c = _place()
        copies = [_remote(ins[w].at[pl.ds((1 - c) * LH, LH)], outs[w], send_sems, recv_sems, w, (x, y, 1 - c))
                  for w in range(n)]
        for cp in copies:
            cp.start()
        for cp in copies:
            cp.wait()

    out_shape = tuple(jax.ShapeDtypeStruct((LH,) + g.shape[1:], g.dtype) for g in grads)
    return pl.pallas_call(body, name=name, out_shape=out_shape, in_specs=[ANY] * n, out_specs=tuple(ANY for _ in grads),
                          scratch_shapes=[pltpu.SemaphoreType.DMA((n,)), pltpu.SemaphoreType.DMA((n,))])(*grads)


def rs_exchange(sends, arrs, name):
    n = len(sends)

    def body(*refs):
        snd = refs[:n]
        outs = refs[2 * n:3 * n]
        send_sems, recv_sems = refs[3 * n:]
        x, y, c = _place()
        sibling = (x, y, 1 - c)
        chips = _other_chips(x, y)
        own = 2 * x + y

        def slab(w, core, chip_slot):
            return outs[w].at[pl.ds(core * LH, LH), chip_slot]

        started = []
        for w in range(n):
            for j, chip in enumerate(chips):
                cp = _remote(snd[w].at[:, 2 * chip[0] + chip[1]], slab(w, c, own), send_sems, recv_sems, 7 * w + j,
                             (*chip, c))
                cp.start()
                started.append(cp)
            cp = _remote(slab(w, c, own), slab(w, c, own), send_sems, recv_sems, 7 * w + 3, sibling)
            cp.start()
            started.append(cp)
        for j, chip in enumerate(chips):
            theirs = 2 * chip[0] + chip[1]
            for w in range(n):
                _remote(snd[w].at[:, own], slab(w, c, theirs), send_sems, recv_sems, 7 * w + j, (*chip, c)).wait_recv()
                cp = _remote(slab(w, c, theirs), slab(w, c, theirs), send_sems, recv_sems, 7 * w + 4 + j, sibling)
                cp.start()
                started.append(cp)
        for w in range(n):
            _remote(snd[w].at[:, own], slab(w, 1 - c, own), send_sems, recv_sems, 7 * w + 3, sibling).wait_recv()
        for j, chip in enumerate(chips):
            theirs = 2 * chip[0] + chip[1]
            for w in range(n):
                _remote(snd[w].at[:, own], slab(w, 1 - c, theirs), send_sems, recv_sems, 7 * w + 4 + j,
                        sibling).wait_recv()
        for cp in started:
            cp.wait_send()

    return _inplace_comm_call(body, name, n, n, 7 * n)(*sends, *arrs)


PACK_COLS = 1024
MATMUL_W = ("w_ada", "w_in", "w_attn_o", "w_conf_out", "w_sc_out", "w_mix_out", "w_ffn_in", "w_ffn_out")
ROW_SPLIT = ("w_mix_out", "w_ffn_out")
CONV_W = ("conf_dw_w", "sc_dw_w")
SMALL = ("c_ctx", "b_ada", "q_norm", "k_norm", "conf_dw_b", "conf_ln_g", "conf_ln_b", "conf_dw_w", "sc_dw_w")


def _pack_rows(arrays, row_multiple):
    flat = jnp.concatenate([a.reshape(-1) for a in arrays])
    rows = -(-flat.shape[0] // PACK_COLS)
    rows = -(-rows // row_multiple) * row_multiple
    flat = jnp.pad(flat, (0, rows * PACK_COLS - flat.shape[0]))
    return flat.reshape(rows, PACK_COLS)


def _unpack(flat2d, shapes):
    flat = flat2d.reshape(-1)
    out, pos = [], 0
    for shp in shapes:
        n = 1
        for s in shp:
            n *= s
        out.append(flat[pos:pos + n].reshape(shp))
        pos += n
    return out


def _rows_joined(stacked):
    depth, nchip, r, c = stacked.shape
    return stacked.reshape(depth, nchip * r, c)


def _cols_joined(stacked_layer):
    nchip, r, c = stacked_layer.shape
    return jnp.transpose(stacked_layer, (1, 0, 2)).reshape(r, nchip * c)


def _cols_split(full):
    r, cols = full.shape
    return jnp.transpose(full.reshape(r, NCHIP, cols // NCHIP), (1, 0, 2))


def _rope_tables():
    rows = SEQ // GRID_W
    r_ids = jnp.repeat(jnp.arange(rows, dtype=F32), GRID_W)
    c_ids = jnp.tile(jnp.arange(GRID_W, dtype=F32), rows)
    freqs = ROPE_THETA ** (-jnp.arange(0, HD // 2, 2, dtype=F32) / (HD // 2))
    ang_r, ang_c = r_ids[:, None] * freqs, c_ids[:, None] * freqs
    cos_h = jnp.concatenate([jnp.cos(ang_r), jnp.cos(ang_r), jnp.cos(ang_c), jnp.cos(ang_c)], axis=1)
    sin_h = jnp.concatenate([-jnp.sin(ang_r), jnp.sin(ang_r), -jnp.sin(ang_c), jnp.sin(ang_c)], axis=1)
    cos_t = jnp.concatenate([jnp.ones((CTX, HD), F32), cos_h], axis=0)
    sin_t = jnp.concatenate([jnp.zeros((CTX, HD), F32), sin_h], axis=0)
    return jnp.tile(cos_t, (1, LANE // HD)), jnp.tile(sin_t, (1, LANE // HD))


def _group_matrix():
    gid = jnp.arange(LANE) // HD
    return jnp.where(gid[:, None] == gid[None, :], 1.0 / HD, 0.0).astype(F32)


def _layer_fwd(i, xs, mods, w, tabs):
    cos_t, sin_t, g_mat = tabs
    n = f"l{i}_"
    sv = {"x_in": xs, "mods": mods}
    sv["h"] = norm_mod_fwd(xs, mods, 0, 1, n + "norm1")
    sv["p_main"] = mm_nn(sv["h"], w["wi_main"], name=n + "p_main")
    sv["p_gate"] = mm_nn(sv["h"], w["wi_gate"], name=n + "p_gate")
    sv["q"], sv["k"], sv["v"] = qkv_fwd(sv["p_main"], cos_t, sin_t, g_mat, w["gq"], w["gk"], n + "qkv")
    sv["o"] = attn_fwd(sv["q"], sv["k"], sv["v"], n + "attn")
    sv["yc"], sv["z"] = conv_fwd(sv["p_main"], w["conf_dw_w"], w["conf_dw_b"], w["sc_dw_w"], n + "conv")
    sv["hs"] = ln_silu_fwd(sv["yc"], w["conf_ln_g"], w["conf_ln_b"], n + "ln_silu")
    sv["ya"] = mm_nn(sv["o"], w["w_attn_o"], name=n + "y_attn")
    sv["yb"] = mm_nn(sv["hs"], w["w_conf_out"], name=n + "y_conf")
    sv["ys"] = mm_nn(sv["z"], w["w_sc_out"], name=n + "y_sc")
    sv["merged"] = gate_merge_fwd(sv["p_gate"], sv["ya"], sv["yb"], sv["ys"], n + "merge")
    sv["mixed"] = mm_nn(sv["merged"], w["w_mix_out"], name=n + "mix")
    sv["x1"] = gate_resid_fwd(xs, sv["mixed"], mods, 2, n + "resid1")
    sv["h2"] = norm_mod_fwd(sv["x1"], mods, 3, 4, n + "norm2")
    sv["u"] = mm_nn(sv["h2"], w["w_ffn_in"], name=n + "ffn_in")
    sv["f"] = swiglu_fwd(sv["u"], n + "swiglu")
    sv["of"] = mm_nn(sv["f"], w["w_ffn_out"], name=n + "ffn_out")
    x2 = gate_resid_fwd(sv["x1"], sv["of"], mods, 5, n + "resid2")
    return x2, sv


def _layer_bwd(i, dx2, sv, w, tabs, cs, gbuf):
    cos_t, sin_t, g_mat = tabs
    n = f"l{i}b_"
    mods = sv["mods"]
    g = {}

    def wgrad(a, b, key, kind, name):
        gbuf[key] = mm_tn(a, b, into=(gbuf[key], i, kind), name=n + name)

    dof, dm5 = gate_resid_bwd(dx2, sv["of"], mods, 5, n + "resid2")
    df = mm_nt(dof, w["w_ffn_out"], name=n + "d_f")
    wgrad(sv["f"], dof, "w_ffn_out", "plain", "dw_ffn_out")
    du = swiglu_bwd(sv["u"], df, n + "swiglu")
    dh2 = mm_nt(du, w["w_ffn_in"], name=n + "d_h2")
    wgrad(sv["h2"], du, "w_ffn_in", "cols", "dw_ffn_in")
    dx1, dm34 = norm_mod_bwd(sv["x1"], mods, dh2, dx2, 4, n + "norm2")
    dmixed, dm2 = gate_resid_bwd(dx1, sv["mixed"], mods, 2, n + "resid1")
    dmerged = mm_nt(dmixed, w["w_mix_out"], name=n + "d_merged")
    wgrad(sv["merged"], dmixed, "w_mix_out", "plain", "dw_mix")
    dya, dyb, dys, dp_gate = gate_merge_bwd(sv["p_gate"], sv["ya"], sv["yb"], sv["ys"], dmerged, n + "merge")
    do = mm_nt(dya, w["w_attn_o"], out_dtype=BF16, name=n + "d_o")
    wgrad(sv["o"], dya, "w_attn_o", "cols", "dw_attn_o")
    dhs = mm_nt(dyb, w["w_conf_out"], name=n + "d_hs")
    wgrad(sv["hs"], dyb, "w_conf_out", "cols", "dw_conf_out")
    dz = mm_nt(dys, w["w_sc_out"], name=n + "d_z")
    wgrad(sv["z"], dys, "w_sc_out", "cols", "dw_sc_out")
    dyc, g["conf_ln_g"], g["conf_ln_b"] = ln_silu_bwd(sv["yc"], w["conf_ln_g"], w["conf_ln_b"], dhs, n + "ln_silu")
    da, dg, dbg, dcg, dxs, g["conf_dw_w"], g["conf_dw_b"], g["sc_dw_w"] = conv_bwd(
        sv["p_main"], w["conf_dw_w"], w["sc_dw_w"], dyc, dz, n + "conv")
    dq, dk, dv = attn_bwd(sv["q"], sv["k"], sv["v"], do, n + "attn")
    dp_qkv, dgqk = qkv_bwd(sv["p_main"], cos_t, sin_t, g_mat, w["gq"], w["gk"], dq, dk, dv, n + "qkv")
    dp_main = jnp.concatenate([dp_qkv, da, dg, dbg, dcg, dxs], axis=1)
    dh = mm_nt(dp_main, w["wi_main"], name=n + "d_h_main")
    dh = mm_nt(dp_gate, w["wi_gate"], acc=dh, name=n + "d_h_gate")
    g["w_in"] = _cols_split(jnp.concatenate([mm_tn(sv["h"], dp_main, out_dtype=BF16, name=n + "dw_in_main"),
                                             mm_tn(sv["h"], dp_gate, out_dtype=BF16, name=n + "dw_in_gate")], axis=1))
    dx_in, dm01 = norm_mod_bwd(sv["x_in"], mods, dh, dx1, 1, n + "norm1")
    parts = jnp.concatenate([dm01, dm2, dm34, dm5], axis=2).reshape(2 * BL, NMOD * D)
    dmod, g["b_ada"] = dmod_assemble(parts, n + "dmod")
    wgrad(cs, dmod, "w_ada", "cols", "dw_ada")
    g["dcs"] = mm_nt(dmod, w["w_ada"], name=n + "d_cs")
    g["q_norm"] = dgqk[0, :AW].reshape(NQ, HD).sum(axis=0)
    g["k_norm"] = dgqk[0, OFF_K:OFF_K + KVW].reshape(NKV, HD).sum(axis=0)
    return dx_in, g


def kernel(x, c, ctx, c_ctx, w_ada, b_ada, w_in, q_norm, k_norm, w_attn_o, conf_dw_w, conf_dw_b, conf_ln_g, conf_ln_b, w_conf_out, sc_dw_w, w_sc_out, w_mix_out, w_ffn_in, w_ffn_out, loss_target, m_c_ctx, m_w_ada, m_b_ada, m_w_in, m_q_norm, m_k_norm, m_w_attn_o, m_conf_dw_w, m_conf_dw_b, m_conf_ln_g, m_conf_ln_b, m_w_conf_out, m_sc_dw_w, m_w_sc_out, m_w_mix_out, m_w_ffn_in, m_w_ffn_out, v_c_ctx, v_w_ada, v_b_ada, v_w_in, v_q_norm, v_k_norm, v_w_attn_o, v_conf_dw_w, v_conf_dw_b, v_conf_ln_g, v_conf_ln_b, v_w_conf_out, v_sc_dw_w, v_w_sc_out, v_w_mix_out, v_w_ffn_in, v_w_ffn_out):
    local = dict(c_ctx=c_ctx, w_ada=w_ada, b_ada=b_ada, w_in=w_in, q_norm=q_norm, k_norm=k_norm, w_attn_o=w_attn_o,
                 conf_dw_w=conf_dw_w, conf_dw_b=conf_dw_b, conf_ln_g=conf_ln_g, conf_ln_b=conf_ln_b,
                 w_conf_out=w_conf_out, sc_dw_w=sc_dw_w, w_sc_out=w_sc_out, w_mix_out=w_mix_out, w_ffn_in=w_ffn_in,
                 w_ffn_out=w_ffn_out)
    mom_m = dict(c_ctx=m_c_ctx, w_ada=m_w_ada, b_ada=m_b_ada, w_in=m_w_in, q_norm=m_q_norm, k_norm=m_k_norm,
                 w_attn_o=m_w_attn_o, conf_dw_w=m_conf_dw_w, conf_dw_b=m_conf_dw_b, conf_ln_g=m_conf_ln_g,
                 conf_ln_b=m_conf_ln_b, w_conf_out=m_w_conf_out, sc_dw_w=m_sc_dw_w, w_sc_out=m_w_sc_out,
                 w_mix_out=m_w_mix_out, w_ffn_in=m_w_ffn_in, w_ffn_out=m_w_ffn_out)
    mom_v = dict(c_ctx=v_c_ctx, w_ada=v_w_ada, b_ada=v_b_ada, w_in=v_w_in, q_norm=v_q_norm, k_norm=v_k_norm,
                 w_attn_o=v_w_attn_o, conf_dw_w=v_conf_dw_w, conf_dw_b=v_conf_dw_b, conf_ln_g=v_conf_ln_g,
                 conf_ln_b=v_conf_ln_b, w_conf_out=v_w_conf_out, sc_dw_w=v_sc_dw_w, w_sc_out=v_w_sc_out,
                 w_mix_out=v_w_mix_out, w_ffn_in=v_w_ffn_in, w_ffn_out=v_w_ffn_out)
    order = ("c_ctx", "w_ada", "b_ada", "w_in", "q_norm", "k_norm", "w_attn_o", "conf_dw_w", "conf_dw_b", "conf_ln_g",
             "conf_ln_b", "w_conf_out", "sc_dw_w", "w_sc_out", "w_mix_out", "w_ffn_in", "w_ffn_out")
    core = lax.axis_index("c").astype(jnp.int32)
    chip = (2 * lax.axis_index("x") + lax.axis_index("y")).astype(jnp.int32)

    own = [cast_into_slot(local[k], chip.reshape(1), "cast_" + k) for k in MATMUL_W]
    wg = dict(zip(MATMUL_W, gather_weights(own, "gather_weights")))
    conv_shapes = [local[k].shape for k in CONV_W]
    conv_all = all_gather8(_pack_rows([local[k] for k in CONV_W], 8), "gather_conv_taps")
    per_chip = [_unpack(conv_all[2 * s], conv_shapes) for s in range(NCHIP)]
    full_conv = {k: jnp.concatenate([per_chip[s][i] for s in range(NCHIP)], axis=2) for i, k in enumerate(CONV_W)}

    loss_local, grad_x, gbuf, small_g = local_step(x, c, ctx, c_ctx, wg, full_conv, b_ada, q_norm, k_norm,
                                                   conf_dw_b, conf_ln_g, conf_ln_b, loss_target)
    loss = lax.psum(loss_local, ("x", "y", "c"))

    partial = [gbuf[k] for k in MATMUL_W]
    from_sibling = rs_swap(partial, "rs_swap")
    sends, arrs = zip(*[rs_add(g_, o_, core.reshape(1), chip.reshape(1), "rs_add_" + k)
                        for k, g_, o_ in zip(MATMUL_W, partial, from_sibling)])
    arrs = dict(zip(MATMUL_W, rs_exchange(sends, arrs, "rs_exchange")))

    small_shapes = [small_g[k].shape for k in SMALL]
    small_sum = sum_leading(all_gather8(_pack_rows([small_g[k] for k in SMALL], 8), "gather_small_grads"), "small_sum")
    small_g = dict(zip(SMALL, _unpack(small_sum, small_shapes)))
    for k in CONV_W:
        width = local[k].shape[2]
        small_g[k] = lax.dynamic_slice_in_dim(small_g[k], chip * width, width, axis=2)

    grad, delta, new_m, new_v = {}, {}, {}, {}
    for k in order:
        if k in arrs:
            grad[k], delta[k], new_m[k], new_v[k] = adamw_sum(local[k], arrs[k], mom_m[k], mom_v[k], "adamw_" + k)
            continue
        shp = local[k].shape
        view = (1, shp[0]) if len(shp) == 1 else (-1, shp[-1])
        d_, m_, v_ = adamw(local[k].reshape(view), small_g[k].reshape(view), mom_m[k].reshape(view),
                           mom_v[k].reshape(view), "adamw_" + k)
        grad[k], delta[k], new_m[k], new_v[k] = small_g[k], d_.reshape(shp), m_.reshape(shp), v_.reshape(shp)
    return (loss, grad_x, *[grad[k] for k in order], *[delta[k] for k in order], *[new_m[k] for k in order],
            *[new_v[k] for k in order])


def local_step(x, c, ctx, c_ctx, wg, full_conv, b_ada, q_norm, k_norm, conf_dw_b, conf_ln_g, conf_ln_b, loss_target):
    tabs = _rope_tables() + (_group_matrix(),)
    plain = {k: _rows_joined(wg[k]) for k in ROW_SPLIT}
    layer_w = []
    for i in range(DEPTH):
        wi = _cols_joined(wg["w_in"][i])
        layer_w.append(dict(
            w_ada=(wg["w_ada"], i, "cols"),
            wi_main=(wi[:, :OFF_GATE], 0, "mat"), wi_gate=(wi[:, OFF_GATE:], 0, "mat"),
            w_attn_o=(wg["w_attn_o"], i, "cols"), w_conf_out=(wg["w_conf_out"], i, "cols"),
            w_sc_out=(wg["w_sc_out"], i, "cols"), w_ffn_in=(wg["w_ffn_in"], i, "cols"),
            w_mix_out=(plain["w_mix_out"], i, "plain"), w_ffn_out=(plain["w_ffn_out"], i, "plain"),
            conf_dw_w=full_conv["conf_dw_w"][i], sc_dw_w=full_conv["sc_dw_w"][i],
            conf_dw_b=conf_dw_b[i][None], conf_ln_g=conf_ln_g[i][None], conf_ln_b=conf_ln_b[i][None],
            gq=jnp.tile(q_norm[i], LANE // HD)[None], gk=jnp.tile(k_norm[i], LANE // HD)[None]))

    cin = jnp.concatenate([c, c_ctx[None], jnp.zeros((8 - BL - 1, D), F32)], axis=0)
    cs = silu_rows(cin, "silu_c")
    xs = jnp.concatenate([ctx, x], axis=1).reshape(NROW, D)
    saved = []
    for i in range(DEPTH):
        mods = mm_nn(cs, layer_w[i]["w_ada"], bias=b_ada[i][None], name=f"l{i}_mod").reshape(8, 1, NMOD * D)
        xs, sv = _layer_fwd(i, xs, mods, layer_w[i], tabs)
        saved.append(sv)
    dxs, loss_lanes = loss_fwd_bwd(xs, loss_target.reshape(BL * SEQ, D), "loss")
    loss_local = 0.5 * jnp.sum(loss_lanes) / D

    gbuf = {k: lax.empty(plain[k].shape if k in ROW_SPLIT else wg[k].shape, BF16) for k in MATMUL_W if k != "w_in"}
    grads = [None] * DEPTH
    for i in reversed(range(DEPTH)):
        dxs, grads[i] = _layer_bwd(i, dxs, saved[i], layer_w[i], tabs, cs, gbuf)
    grad_x = dxs.reshape(BL, RE, D)[:, CTX:, :]
    dcin = silu_rows_bwd(cin, jnp.stack([grads[i]["dcs"] for i in range(DEPTH)]), "silu_c_bwd")

    def stack(key):
        return jnp.stack([grads[i][key] for i in range(DEPTH)])

    gbuf["w_in"] = stack("w_in")
    for k in ROW_SPLIT:
        gbuf[k] = gbuf[k].reshape(wg[k].shape)
    small_g = dict(c_ctx=dcin[BL], b_ada=stack("b_ada").reshape(DEPTH, NMOD * D), q_norm=stack("q_norm"),
                   k_norm=stack("k_norm"), conf_dw_b=stack("conf_dw_b").reshape(DEPTH, CW),
                   conf_ln_g=stack("conf_ln_g").reshape(DEPTH, CW), conf_ln_b=stack("conf_ln_b").reshape(DEPTH, CW),
                   conf_dw_w=stack("conf_dw_w"), sc_dw_w=stack("sc_dw_w"))
    return loss_local, grad_x, gbuf, small_g
```

```python
import functools

import jax
import jax.numpy as jnp
from jax import lax
from jax.experimental import pallas as pl
from jax.experimental.pallas import tpu as pltpu

F32, BF16 = jnp.float32, jnp.bfloat16
HIGHEST = lax.Precision.HIGHEST

D = 1024
SEQ = 2048
CTX = 256
DEPTH = 4
BL = 4
GRID_W = 64
HD = 64
NQ = 8
NKV = 2
AW = NQ * HD
KVW = NKV * HD
CW = D // 2
CONF_K = 31
SC_K = 3
NMOD = 6
FH = -(-8 * D // (3 * 256)) * 256
EPS = 1e-6
ROPE_THETA = 10000.0
ATTN_SCALE = HD ** -0.5
OFF_K = AW
OFF_V = OFF_K + KVW
OFF_CONF = OFF_V + KVW
OFF_SC = OFF_CONF + 2 * CW
OFF_GATE = OFF_SC + 3 * CW
IN_W = OFF_GATE + 3 * D
QKVW = OFF_CONF
NCHIP = 4

ADAM_LR, ADAM_B1, ADAM_B2, ADAM_EPS, ADAM_WD, ADAM_STEP = 0.001, 0.9, 0.999, 1e-08, 0.01, 10

TM = CTX
RE = CTX + SEQ
TPE = RE // TM
NROW = BL * RE
NT = NROW // TM
LANE = 128
CB = CW // LANE
CONV_CH = 128
PADR = 16
VMEM_LIMIT = 52 * 1024 * 1024

MESH = pl.DeviceIdType.MESH
ANY = pl.BlockSpec(memory_space=pl.ANY)


def _pcall(body, *, name, out_shape, grid=(), in_specs=None, out_specs=None, scratch=(), sem=None):
    if not grid:
        return pl.pallas_call(body, name=name, out_shape=out_shape)
    params = pltpu.CompilerParams(dimension_semantics=sem, vmem_limit_bytes=VMEM_LIMIT)
    return pl.pallas_call(body, name=name, out_shape=out_shape, grid=grid, in_specs=in_specs, out_specs=out_specs,
                          scratch_shapes=list(scratch), compiler_params=params)


def _pick(n, cands):
    for t in cands:
        if n % t == 0:
            return t
    return n


def _seg(t):
    return jnp.where(t % TPE == 0, BL, t // TPE)


def _slot(t):
    return 2 * (t // TPE) + jnp.where(t % TPE == 0, 0, 1)


def _sigmoid(x):
    return 1.0 / (1.0 + jnp.exp(-x))


MM_BUDGET = 40 * 1024 * 1024
N_TILE_CAP = 1664


def _tile(n, cap=N_TILE_CAP):
    if n <= cap:
        return n
    for t in range(cap - cap % LANE, 0, -LANE):
        if n % t == 0:
            return t
    return n


def _row_tile(m, bytes_of):
    for tm in (1024, 512, 256, 128):
        if m % tm == 0 and bytes_of(tm) <= MM_BUDGET:
            return tm
    return m


def _w_dims(w):
    arr, _, kind = w
    if kind == "cols":
        return arr.shape[2], NCHIP * arr.shape[3]
    return arr.shape[-2], arr.shape[-1]


def _sz(dtype):
    return jnp.dtype(dtype).itemsize


def mm_nn(a, w, *, bias=None, out_dtype=F32, name):
    arr, layer, kind = w
    m, k = a.shape
    _, n = _w_dims(w)
    tn = _tile(arr.shape[3]) if kind == "cols" else _tile(n)
    tm = _row_tile(m, lambda t: 2 * (t * k * _sz(a.dtype) + k * tn * 2 + t * tn * _sz(out_dtype)))
    if kind == "mat":
        b_spec = pl.BlockSpec((k, tn), lambda j, i: (0, j))
    elif kind == "plain":
        b_spec = pl.BlockSpec((None, k, tn), lambda j, i: (layer, 0, j))
    else:
        per = arr.shape[3] // tn
        b_spec = pl.BlockSpec((None, None, k, tn), lambda j, i: (layer, j // per, 0, j % per))
    has_bias = bias is not None

    def body(*refs):
        out = jnp.dot(refs[0][...].astype(BF16), refs[1][...].astype(BF16), preferred_element_type=F32)
        if has_bias:
            out = out + refs[2][...]
        refs[-1][...] = out.astype(out_dtype)

    in_specs = [pl.BlockSpec((tm, k), lambda j, i: (i, 0)), b_spec]
    args = [a, arr]
    if has_bias:
        in_specs.append(pl.BlockSpec((1, tn), lambda j, i: (0, j)))
        args.append(bias)
    return _pcall(body, name=name, out_shape=jax.ShapeDtypeStruct((m, n), out_dtype), grid=(n // tn, m // tm),
                  in_specs=in_specs, out_specs=pl.BlockSpec((tm, tn), lambda j, i: (i, j)),
                  sem=("parallel", "parallel"))(*args)


def mm_nt(a, w, *, acc=None, out_dtype=F32, name):
    arr, layer, kind = w
    kdim, _ = _w_dims(w)
    has_acc = acc is not None
    tk = _tile(kdim, 1408)
    if kind == "cols":
        c = arr.shape[3]
        m = a.shape[-2]
        if a.ndim == 3:
            a_spec = lambda t: pl.BlockSpec((None, t, c), lambda j, i, s: (s // 2, i, s % 2))
        else:
            a_spec = lambda t: pl.BlockSpec((t, c), lambda j, i, s: (i, s))
        tm = _row_tile(m, lambda t: 2 * (t * c * _sz(a.dtype) + tk * c * 2 + t * tk * _sz(out_dtype)) + t * tk * 4)

        def body(a_ref, b_ref, o_ref, acc_ref):
            s = pl.program_id(2)

            @pl.when(s == 0)
            def _():
                acc_ref[...] = jnp.zeros_like(acc_ref)

            acc_ref[...] += lax.dot_general(a_ref[...].astype(BF16), b_ref[...], (((1,), (1,)), ((), ())),
                                            preferred_element_type=F32)

            @pl.when(s == NCHIP - 1)
            def _():
                o_ref[...] = acc_ref[...].astype(out_dtype)

        return _pcall(body, name=name, out_shape=jax.ShapeDtypeStruct((m, kdim), out_dtype),
                      grid=(kdim // tk, m // tm, NCHIP),
                      in_specs=[a_spec(tm), pl.BlockSpec((None, None, tk, c), lambda j, i, s: (layer, s, j, 0))],
                      out_specs=pl.BlockSpec((tm, tk), lambda j, i, s: (i, j)),
                      scratch=[pltpu.VMEM((tm, tk), F32)],
                      sem=("parallel", "parallel", "arbitrary"))(a, arr)

    m, n = a.shape
    tm = _row_tile(m, lambda t: 2 * (t * n * _sz(a.dtype) + tk * n * 2 + t * tk * (_sz(out_dtype) + 4 * has_acc)))
    if kind == "mat":
        b_spec = pl.BlockSpec((tk, n), lambda j, i: (j, 0))
    else:
        b_spec = pl.BlockSpec((None, tk, n), lambda j, i: (layer, j, 0))

    def body(*refs):
        out = lax.dot_general(refs[0][...].astype(BF16), refs[1][...].astype(BF16), (((1,), (1,)), ((), ())),
                              preferred_element_type=F32)
        if has_acc:
            out = out + refs[2][...]
        refs[-1][...] = out.astype(out_dtype)

    in_specs = [pl.BlockSpec((tm, n), lambda j, i: (i, 0)), b_spec]
    args = [a, arr]
    if has_acc:
        in_specs.append(pl.BlockSpec((tm, tk), lambda j, i: (i, j)))
        args.append(acc)
    return _pcall(body, name=name, out_shape=jax.ShapeDtypeStruct((m, kdim), out_dtype), grid=(kdim // tk, m // tm),
                  in_specs=in_specs, out_specs=pl.BlockSpec((tm, tk), lambda j, i: (i, j)),
                  sem=("parallel", "parallel"))(*args)


def mm_tn(a, b, *, into=None, out_dtype=F32, name):
    rows, k = a.shape
    halves = b.ndim == 3
    n = 2 * b.shape[2] if halves else b.shape[1]
    kind = "mat" if into is None else into[2]
    if kind == "cols":
        buf, layer, _ = into
        c = buf.shape[3]
        tn, tk = _tile(c), k
        per = c // tn
        out_spec = pl.BlockSpec((None, None, tk, tn), lambda i, j, r: (layer, j // per, 0, j % per))
        odt = buf.dtype
    elif kind == "plain":
        buf, layer, _ = into
        tn, tk = _tile(n), _tile(k, 1408)
        out_spec = pl.BlockSpec((None, tk, tn), lambda i, j, r: (layer, i, j))
        odt = buf.dtype
    else:
        tn, tk = _tile(n), _tile(k, 1408)
        out_spec = pl.BlockSpec((tk, tn), lambda i, j, r: (i, j))
        odt = out_dtype
    tr = _row_tile(rows, lambda t: 2 * (t * tk * _sz(a.dtype) + t * tn * _sz(b.dtype) + tk * tn * _sz(odt)) + tk * tn * 4)
    nsteps = rows // tr

    def body(*refs):
        a_ref, b_ref = refs[0], refs[1]
        o_ref, acc_ref = refs[-2], refs[-1]
        r = pl.program_id(2)

        @pl.when(r == 0)
        def _():
            acc_ref[...] = jnp.zeros_like(acc_ref)

        acc_ref[...] += lax.dot_general(a_ref[...].astype(BF16), b_ref[...].astype(BF16), (((0,), (0,)), ((), ())),
                                        preferred_element_type=F32)

        @pl.when(r == nsteps - 1)
        def _():
            o_ref[...] = acc_ref[...].astype(odt)

    if halves:
        per_half = (n // 2) // tn
        b_spec = pl.BlockSpec((None, tr, tn), lambda i, j, r: (j // per_half, r, j % per_half))
    else:
        b_spec = pl.BlockSpec((tr, tn), lambda i, j, r: (r, j))
    in_specs = [pl.BlockSpec((tr, tk), lambda i, j, r: (r, i)), b_spec]
    args = [a, b]
    if kind == "mat":
        out_shape = jax.ShapeDtypeStruct((k, n), odt)
        aliases = {}
    else:
        in_specs.append(ANY)
        args.append(buf)
        out_shape = jax.ShapeDtypeStruct(buf.shape, odt)
        aliases = {2: 0}
    return pl.pallas_call(body, name=name, out_shape=out_shape, grid=(k // tk, n // tn, nsteps), in_specs=in_specs,
                          out_specs=out_spec, scratch_shapes=[pltpu.VMEM((tk, tn), F32)], input_output_aliases=aliases,
                          compiler_params=pltpu.CompilerParams(
                              dimension_semantics=("parallel", "parallel", "arbitrary"),
                              vmem_limit_bytes=VMEM_LIMIT))(*args)


def ffn_in_swiglu(h, w_in, layer, name):
    m, k = h.shape
    c = w_in.shape[3]
    tm = 512

    def body(h_ref, wa_ref, wb_ref, f_ref, u_ref):
        hv = h_ref[...]
        a = jnp.dot(hv, wa_ref[...], preferred_element_type=F32)
        b = jnp.dot(hv, wb_ref[...], preferred_element_type=F32)
        f_ref[...] = (a * _sigmoid(a) * b).astype(BF16)
        u_ref[0] = a.astype(BF16)
        u_ref[1] = b.astype(BF16)

    return _pcall(body, name=name,
                  out_shape=(jax.ShapeDtypeStruct((m, FH), BF16), jax.ShapeDtypeStruct((2, m, FH), BF16)),
                  grid=(2, m // tm),
                  in_specs=[pl.BlockSpec((tm, k), lambda j, i: (i, 0)),
                            pl.BlockSpec((None, None, k, c), lambda j, i: (layer, j, 0, 0)),
                            pl.BlockSpec((None, None, k, c), lambda j, i: (layer, 2 + j, 0, 0))],
                  out_specs=(pl.BlockSpec((tm, c), lambda j, i: (i, j)), pl.BlockSpec((2, tm, c), lambda j, i: (0, i, j))),
                  sem=("parallel", "parallel"))(h, w_in, w_in)


def d_f_swiglu(dof, w_out, layer, u2, name):
    m, k = dof.shape
    c = FH // 2
    tm = 512

    def body(d_ref, w_ref, u_ref, du_ref):
        df = lax.dot_general(d_ref[...], w_ref[...], (((1,), (1,)), ((), ())), preferred_element_type=F32)
        a, b = u_ref[0].astype(F32), u_ref[1].astype(F32)
        sg = _sigmoid(a)
        du_ref[0] = (df * b * (sg * (1.0 + a * (1.0 - sg)))).astype(BF16)
        du_ref[1] = (df * a * sg).astype(BF16)

    ublk = pl.BlockSpec((2, tm, c), lambda j, i: (0, i, j))
    return _pcall(body, name=name, out_shape=jax.ShapeDtypeStruct((2, m, FH), BF16), grid=(2, m // tm),
                  in_specs=[pl.BlockSpec((tm, k), lambda j, i: (i, 0)),
                            pl.BlockSpec((None, c, k), lambda j, i: (layer, j, 0)), ublk],
                  out_specs=ublk, sem=("parallel", "parallel"))(dof, w_out, u2)


GATE_TN = 512


def gate_mm_fwd(h, wi_gate, o, hs, z, wo, wc, ws, name):
    m, k = h.shape
    tm, tn = 512, min(GATE_TN, D)
    nj = D // tn

    def body(h_ref, g0_ref, g1_ref, g2_ref, o_ref, hs_ref, z_ref, wo_ref, wc_ref, ws_ref, m_ref, g_ref, y_ref):
        hv = h_ref[...]
        acc = jnp.zeros((tm, tn), F32)
        for g, (gw_ref, x_ref, w_ref) in enumerate(((g0_ref, o_ref, wo_ref), (g1_ref, hs_ref, wc_ref),
                                                    (g2_ref, z_ref, ws_ref))):
            gate = _sigmoid(jnp.dot(hv, gw_ref[...], preferred_element_type=F32))
            y = jnp.dot(x_ref[...], w_ref[...], preferred_element_type=F32)
            acc += gate * y
            g_ref[g] = gate.astype(BF16)
            y_ref[g] = y.astype(BF16)
        m_ref[...] = acc.astype(BF16)

    def gate_w(g):
        return pl.BlockSpec((k, tn), lambda j, i: (0, g * nj + j))

    def branch(width):
        return pl.BlockSpec((tm, width), lambda j, i: (i, 0))

    def branch_w(width):
        return pl.BlockSpec((width, tn), lambda j, i: (0, j))

    stacked = pl.BlockSpec((3, tm, tn), lambda j, i: (0, i, j))
    sds3 = jax.ShapeDtypeStruct((3, m, D), BF16)
    return _pcall(body, name=name, out_shape=(jax.ShapeDtypeStruct((m, D), BF16), sds3, sds3), grid=(nj, m // tm),
                  in_specs=[pl.BlockSpec((tm, k), lambda j, i: (i, 0)), gate_w(0), gate_w(1), gate_w(2),
                            branch(o.shape[1]), branch(hs.shape[1]), branch(z.shape[1]),
                            branch_w(wo.shape[0]), branch_w(wc.shape[0]), branch_w(ws.shape[0])],
                  out_specs=(pl.BlockSpec((tm, tn), lambda j, i: (i, j)), stacked, stacked),
                  sem=("parallel", "parallel"))(h, wi_gate, wi_gate, wi_gate, o, hs, z, wo, wc, ws)


def d_merged_gate(dmixed, w_mix, layer, gates, ys, name):
    m, k = dmixed.shape
    tm = 256

    def body(d_ref, w_ref, g_ref, y_ref, da_ref, db_ref, ds_ref, dp_ref):
        dm = lax.dot_general(d_ref[...], w_ref[...], (((1,), (1,)), ((), ())), preferred_element_type=F32)
        for g, dy_ref in enumerate((da_ref, db_ref, ds_ref)):
            gate = g_ref[g].astype(F32)
            dy_ref[...] = (dm * gate).astype(BF16)
            dp_ref[:, g * D:(g + 1) * D] = (dm * y_ref[g].astype(F32) * gate * (1.0 - gate)).astype(BF16)

    stacked = pl.BlockSpec((3, tm, D), lambda i: (0, i, 0))
    row = pl.BlockSpec((tm, D), lambda i: (i, 0))
    sds = jax.ShapeDtypeStruct((m, D), BF16)
    return _pcall(body, name=name, out_shape=(sds, sds, sds, jax.ShapeDtypeStruct((m, 3 * D), BF16)), grid=(m // tm,),
                  in_specs=[pl.BlockSpec((tm, k), lambda i: (i, 0)), pl.BlockSpec((None, D, k), lambda i: (layer, 0, 0)),
                            stacked, stacked],
                  out_specs=(row, row, row, pl.BlockSpec((tm, 3 * D), lambda i: (i, 0))),
                  sem=("parallel",))(dmixed, w_mix, gates, ys)


def _mods_spec():
    return pl.BlockSpec((1, 1, NMOD * D), lambda t: (_seg(t), 0, 0))


def _rows(width):
    return pl.BlockSpec((TM, width), lambda t: (t, 0))


def norm_mod_fwd(x, mods, k_sh, k_sc, name):
    def body(x_ref, m_ref, h_ref):
        x = x_ref[...]
        r = lax.rsqrt(jnp.mean(x * x, axis=-1, keepdims=True) + EPS)
        sh = m_ref[0, :, k_sh * D:(k_sh + 1) * D]
        sc = m_ref[0, :, k_sc * D:(k_sc + 1) * D]
        h_ref[...] = (x * r * (1.0 + sc) + sh).astype(BF16)

    return _pcall(body, name=name, out_shape=jax.ShapeDtypeStruct((NROW, D), BF16), grid=(NT,),
                  in_specs=[_rows(D), _mods_spec()], out_specs=_rows(D), sem=("parallel",))(x, mods)


def _accumulate_slot(t, ref, part):
    first = (t % TPE) <= 1

    @pl.when(first)
    def _():
        ref[0] = part

    @pl.when(jnp.logical_not(first))
    def _():
        ref[0] += part


def norm_mod_bwd(x, mods, dh, dres, k_sc, name):
    def body(x_ref, m_ref, dh_ref, dres_ref, dx_ref, dp_ref):
        t = pl.program_id(0)
        x = x_ref[...]
        r = lax.rsqrt(jnp.mean(x * x, axis=-1, keepdims=True) + EPS)
        xn = x * r
        sc = m_ref[0, :, k_sc * D:(k_sc + 1) * D]
        dh = dh_ref[...]
        dxn = dh * (1.0 + sc)
        dx_ref[...] = r * (dxn - xn * jnp.mean(dxn * xn, axis=-1, keepdims=True)) + dres_ref[...]
        part = jnp.concatenate([jnp.sum(dh, axis=0, keepdims=True), jnp.sum(dh * xn, axis=0, keepdims=True)], axis=1)
        _accumulate_slot(t, dp_ref, part)

    return _pcall(body, name=name,
                  out_shape=(jax.ShapeDtypeStruct((NROW, D), F32), jax.ShapeDtypeStruct((2 * BL, 1, 2 * D), F32)),
                  grid=(NT,), in_specs=[_rows(D), _mods_spec(), _rows(D), _rows(D)],
                  out_specs=(_rows(D), pl.BlockSpec((1, 1, 2 * D), lambda t: (_slot(t), 0, 0))),
                  sem=("arbitrary",))(x, mods, dh, dres)


def gate_resid_fwd(x, y, mods, k_g, name):
    def body(x_ref, y_ref, m_ref, o_ref):
        o_ref[...] = x_ref[...] + m_ref[0, :, k_g * D:(k_g + 1) * D] * y_ref[...]

    return _pcall(body, name=name, out_shape=jax.ShapeDtypeStruct((NROW, D), F32), grid=(NT,),
                  in_specs=[_rows(D), _rows(D), _mods_spec()], out_specs=_rows(D), sem=("parallel",))(x, y, mods)


def gate_resid_bwd(dx, y, mods, k_g, name):
    def body(dx_ref, y_ref, m_ref, dy_ref, dp_ref):
        t = pl.program_id(0)
        dx = dx_ref[...]
        dy_ref[...] = (dx * m_ref[0, :, k_g * D:(k_g + 1) * D]).astype(BF16)
        _accumulate_slot(t, dp_ref, jnp.sum(dx * y_ref[...], axis=0, keepdims=True))

    return _pcall(body, name=name,
                  out_shape=(jax.ShapeDtypeStruct((NROW, D), BF16), jax.ShapeDtypeStruct((2 * BL, 1, D), F32)),
                  grid=(NT,), in_specs=[_rows(D), _rows(D), _mods_spec()],
                  out_specs=(_rows(D), pl.BlockSpec((1, 1, D), lambda t: (_slot(t), 0, 0))),
                  sem=("arbitrary",))(dx, y, mods)


def _swap16(y, lo16):
    return jnp.where(lo16, pltpu.roll(y, LANE - 16, 1), pltpu.roll(y, 16, 1))


def _group_mean(v, g_mat):
    return jnp.dot(v, g_mat, precision=HIGHEST, preferred_element_type=F32)


def qkv_fwd(p_main, cos_t, sin_t, g_mat, gq, gk, name):
    def body(p_ref, cos_ref, sin_ref, g_ref, gq_ref, gk_ref, q_ref, k_ref, v_ref):
        cos, sin, g_mat_v = cos_ref[...], sin_ref[...], g_ref[...]
        lo16 = (lax.broadcasted_iota(jnp.int32, (TM, LANE), 1) % 32) < 16

        def block(xb, g):
            r = lax.rsqrt(_group_mean(xb * xb, g_mat_v) + EPS)
            y = xb * r * g
            return y * cos + _swap16(y, lo16) * sin

        for j in range(AW // LANE):
            q_ref[:, j * LANE:(j + 1) * LANE] = (block(p_ref[:, j * LANE:(j + 1) * LANE], gq_ref[...])
                                                 * ATTN_SCALE).astype(BF16)
        lo = lax.broadcasted_iota(jnp.int32, (TM, LANE), 1) < HD
        for src, dst_ref in ((block(p_ref[:, OFF_K:OFF_K + LANE], gk_ref[...]), k_ref), (p_ref[:, OFF_V:OFF_V + LANE], v_ref)):
            swapped = pltpu.roll(src, HD, 1)
            dst_ref[:, 0:LANE] = jnp.where(lo, src, swapped).astype(BF16)
            dst_ref[:, LANE:2 * LANE] = jnp.where(lo, swapped, src).astype(BF16)

    tab = pl.BlockSpec((TM, LANE), lambda t: (t % TPE, 0))
    small = pl.BlockSpec((1, LANE), lambda t: (0, 0))
    return _pcall(body, name=name,
                  out_shape=(jax.ShapeDtypeStruct((NROW, AW), BF16), jax.ShapeDtypeStruct((NROW, 2 * KVW), BF16),
                             jax.ShapeDtypeStruct((NROW, 2 * KVW), BF16)),
                  grid=(NT,),
                  in_specs=[_rows(QKVW), tab, tab, pl.BlockSpec((LANE, LANE), lambda t: (0, 0)), small, small],
                  out_specs=(_rows(AW), _rows(2 * KVW), _rows(2 * KVW)),
                  sem=("parallel",))(p_main, cos_t, sin_t, g_mat, gq, gk)


def qkv_bwd(p_main, cos_t, sin_t, g_mat, gq, gk, dq, dk, dv, name):
    def body(p_ref, cos_ref, sin_ref, g_ref, gq_ref, gk_ref, dq_ref, dk_ref, dv_ref, dp_ref, dg_ref):
        t = pl.program_id(0)
        cos, sin, g_mat_v = cos_ref[...], sin_ref[...], g_ref[...]
        lo16 = (lax.broadcasted_iota(jnp.int32, (TM, LANE), 1) % 32) < 16

        def block(xb, g, dyr):
            r = lax.rsqrt(_group_mean(xb * xb, g_mat_v) + EPS)
            xn = xb * r
            dy = dyr * cos + _swap16(dyr * sin, lo16)
            dgl = jnp.sum(dy * xn, axis=0, keepdims=True)
            dxn = dy * g
            return r * (dxn - xn * _group_mean(dxn * xn, g_mat_v)), dgl

        parts = []
        for j in range(AW // LANE):
            sl = slice(j * LANE, (j + 1) * LANE)
            dxb, dgl = block(p_ref[:, sl], gq_ref[...], dq_ref[:, sl] * ATTN_SCALE)
            dp_ref[:, sl] = dxb.astype(BF16)
            parts.append(dgl)
        lo = lax.broadcasted_iota(jnp.int32, (TM, LANE), 1) < HD

        def fold(d_ref):
            d0, d1 = d_ref[:, 0:LANE], d_ref[:, LANE:2 * LANE]
            return jnp.where(lo, d0 + pltpu.roll(d0, HD, 1), d1 + pltpu.roll(d1, HD, 1))

        dxb, dgl = block(p_ref[:, OFF_K:OFF_K + LANE], gk_ref[...], fold(dk_ref))
        dp_ref[:, OFF_K:OFF_K + LANE] = dxb.astype(BF16)
        parts.append(dgl)
        parts.append(jnp.zeros((1, LANE), F32))
        dp_ref[:, OFF_V:OFF_V + LANE] = fold(dv_ref).astype(BF16)
        part = jnp.concatenate(parts, axis=1)

        @pl.when(t == 0)
        def _():
            dg_ref[...] = part

        @pl.when(t != 0)
        def _():
            dg_ref[...] += part

    tab = pl.BlockSpec((TM, LANE), lambda t: (t % TPE, 0))
    small = pl.BlockSpec((1, LANE), lambda t: (0, 0))
    return _pcall(body, name=name,
                  out_shape=(jax.ShapeDtypeStruct((NROW, QKVW), BF16), jax.ShapeDtypeStruct((1, QKVW), F32)),
                  grid=(NT,),
                  in_specs=[_rows(QKVW), tab, tab, pl.BlockSpec((LANE, LANE), lambda t: (0, 0)), small, small,
                            _rows(AW), _rows(2 * KVW), _rows(2 * KVW)],
                  out_specs=(_rows(QKVW), pl.BlockSpec((1, QKVW), lambda t: (0, 0))),
                  sem=("arbitrary",))(p_main, cos_t, sin_t, g_mat, gq, gk, dq, dk, dv)


def _layer_norm_parts(yc):
    mu = jnp.mean(yc, axis=-1, keepdims=True)
    xc = yc - mu
    rs = lax.rsqrt(jnp.mean(xc * xc, axis=-1, keepdims=True) + EPS)
    return xc * rs, rs


def ln_silu_fwd(yc, g, b, name):
    def body(y_ref, g_ref, b_ref, o_ref):
        nrm, _ = _layer_norm_parts(y_ref[...])
        ln = nrm * g_ref[...] + b_ref[...]
        o_ref[...] = (ln * _sigmoid(ln)).astype(BF16)

    vec = pl.BlockSpec((1, CW), lambda t: (0, 0))
    return _pcall(body, name=name, out_shape=jax.ShapeDtypeStruct((NROW, CW), BF16), grid=(NT,),
                  in_specs=[_rows(CW), vec, vec], out_specs=_rows(CW), sem=("parallel",))(yc, g, b)


def ln_silu_bwd(yc, g, b, dhs, name):
    def body(y_ref, g_ref, b_ref, dh_ref, dy_ref, dg_ref, db_ref):
        t = pl.program_id(0)
        nrm, rs = _layer_norm_parts(y_ref[...])
        ln = nrm * g_ref[...] + b_ref[...]
        sg = _sigmoid(ln)
        dln = dh_ref[...] * (sg * (1.0 + ln * (1.0 - sg)))
        dn = dln * g_ref[...]
        dy_ref[...] = rs * (dn - jnp.mean(dn, axis=-1, keepdims=True)
                            - nrm * jnp.mean(dn * nrm, axis=-1, keepdims=True))
        pg = jnp.sum(dln * nrm, axis=0, keepdims=True)
        pb = jnp.sum(dln, axis=0, keepdims=True)

        @pl.when(t == 0)
        def _():
            dg_ref[...] = pg
            db_ref[...] = pb

        @pl.when(t != 0)
        def _():
            dg_ref[...] += pg
            db_ref[...] += pb

    vec = pl.BlockSpec((1, CW), lambda t: (0, 0))
    return _pcall(body, name=name,
                  out_shape=(jax.ShapeDtypeStruct((NROW, CW), F32), jax.ShapeDtypeStruct((1, CW), F32),
                             jax.ShapeDtypeStruct((1, CW), F32)),
                  grid=(NT,), in_specs=[_rows(CW), vec, vec, _rows(CW)], out_specs=(_rows(CW), vec, vec),
                  sem=("arbitrary",))(yc, g, b, dhs)


def loss_fwd_bwd(y, target, name):
    def body(y_ref, t_ref, dy_ref, l_ref):
        t = pl.program_id(0)
        latent = (t % TPE) != 0
        err = jnp.where(latent, y_ref[...] - t_ref[...], 0.0)
        dy_ref[...] = err * (1.0 / D)
        part = jnp.sum(err * err, axis=0, keepdims=True)

        @pl.when(t == 0)
        def _():
            l_ref[...] = part

        @pl.when(t != 0)
        def _():
            l_ref[...] += part

    tgt = pl.BlockSpec((TM, D), lambda t: ((t // TPE) * (TPE - 1) + jnp.maximum(t % TPE - 1, 0), 0))
    return _pcall(body, name=name,
                  out_shape=(jax.ShapeDtypeStruct((NROW, D), F32), jax.ShapeDtypeStruct((1, D), F32)),
                  grid=(NT,), in_specs=[_rows(D), tgt], out_specs=(_rows(D), pl.BlockSpec((1, D), lambda t: (0, 0))),
                  sem=("arbitrary",))(y, target)


QB_PER_KV = AW // LANE // NKV


def _softmax_parts(qm, k):
    s = lax.dot_general(qm, k, (((1,), (1,)), ((), ())), preferred_element_type=F32)
    e = jnp.exp(s - jnp.max(s, axis=-1, keepdims=True))
    return e, 1.0 / jnp.sum(e, axis=-1, keepdims=True)


def _attn_specs():
    qs = pl.BlockSpec((TM, LANE), lambda b, h, t, j: (b * TPE + t, h * QB_PER_KV + j))
    ks = pl.BlockSpec((RE, LANE), lambda b, h, t, j: (b, h))
    return qs, ks


def _lane_halves():
    lo = lax.broadcasted_iota(jnp.int32, (TM, LANE), 1) < HD
    return lo, jnp.logical_not(lo)


def attn_fwd(q, k, v, name):
    def body(q_ref, k_ref, v_ref, o_ref):
        t = pl.program_id(2)
        halves = _lane_halves()
        qv = q_ref[...]

        def run(nk):
            kv, vv = k_ref[0:nk, :], v_ref[0:nk, :]
            outs = []
            for sel in halves:
                e, rinv = _softmax_parts(jnp.where(sel, qv, jnp.zeros_like(qv)), kv)
                outs.append(jnp.dot(e.astype(BF16), vv, preferred_element_type=F32) * rinv)
            o_ref[...] = jnp.where(halves[0], outs[0], outs[1]).astype(BF16)

        @pl.when(t == 0)
        def _():
            run(CTX)

        @pl.when(t != 0)
        def _():
            run(RE)

    qs, ks = _attn_specs()
    return _pcall(body, name=name, out_shape=jax.ShapeDtypeStruct((NROW, AW), BF16), grid=(BL, NKV, TPE, QB_PER_KV),
                  in_specs=[qs, ks, ks], out_specs=qs, sem=("parallel",) * 4)(q, k, v)


def attn_bwd(q, k, v, do, name):
    def body(q_ref, k_ref, v_ref, do_ref, dq_ref, dk_ref, dv_ref):
        t, j = pl.program_id(2), pl.program_id(3)
        halves = _lane_halves()
        qv, dov = q_ref[...], do_ref[...]

        @pl.when(jnp.logical_and(t == 0, j == 0))
        def _():
            dk_ref[...] = jnp.zeros_like(dk_ref)
            dv_ref[...] = jnp.zeros_like(dv_ref)

        def run(nk):
            kv, vv = k_ref[0:nk, :], v_ref[0:nk, :]
            dqs = []
            for sel in halves:
                qm = jnp.where(sel, qv, jnp.zeros_like(qv))
                dom = jnp.where(sel, dov, jnp.zeros_like(dov))
                e, rinv = _softmax_parts(qm, kv)
                p = e * rinv
                dv_ref[0:nk, :] += lax.dot_general(p.astype(BF16), dom, (((0,), (0,)), ((), ())),
                                                   preferred_element_type=F32)
                dp = lax.dot_general(dom, vv, (((1,), (1,)), ((), ())), preferred_element_type=F32)
                ds = (p * (dp - jnp.sum(dp * p, axis=-1, keepdims=True))).astype(BF16)
                dqs.append(jnp.dot(ds, kv, preferred_element_type=F32))
                dk_ref[0:nk, :] += lax.dot_general(ds, qm, (((0,), (0,)), ((), ())), preferred_element_type=F32)
            dq_ref[...] = jnp.where(halves[0], dqs[0], dqs[1])

        @pl.when(t == 0)
        def _():
            run(CTX)

        @pl.when(t != 0)
        def _():
            run(RE)

    qs, ks = _attn_specs()
    return _pcall(body, name=name,
                  out_shape=(jax.ShapeDtypeStruct((NROW, AW), F32), jax.ShapeDtypeStruct((NROW, 2 * KVW), F32),
                             jax.ShapeDtypeStruct((NROW, 2 * KVW), F32)),
                  grid=(BL, NKV, TPE, QB_PER_KV), in_specs=[qs, ks, ks, qs], out_specs=(qs, ks, ks),
                  sem=("parallel", "parallel", "arbitrary", "arbitrary"))(q, k, v, do)


CONV_SEGS = ((0, CTX), (CTX, SEQ))


def _p_block(col0):
    return pl.BlockSpec((RE, LANE), lambda cb, b: (b, col0 // LANE + cb))


def _conv_io(width):
    return pl.BlockSpec((RE, LANE), lambda cb, b: (b, cb))


def _taps(n):
    return pl.BlockSpec((n, LANE), lambda cb, b: (0, cb))


def _fill_pad(pad_ref, length, values):
    pad_ref[0:PADR, :] = jnp.zeros((PADR, LANE), F32)
    pad_ref[PADR + length:2 * PADR + length, :] = jnp.zeros((PADR, LANE), F32)
    pad_ref[PADR:PADR + length, :] = values


def _conv_chunk(pad_ref, w_ref, ntap, c0, first_row):
    acc = jnp.zeros((CONV_CH, LANE), F32)
    for kk in range(ntap):
        r0 = c0 + first_row(kk)
        acc += w_ref[kk:kk + 1, :] * pad_ref[r0:r0 + CONV_CH, :]
    return acc


def conv_fwd(p_main, wdw, bdw, w3, name):
    def body(a_ref, g_ref, bg_ref, cg_ref, xs_ref, w_ref, b_ref, w3_ref, yc_ref, z_ref, pad_ref):
        for off, length in CONV_SEGS:
            rows = slice(off, off + length)
            _fill_pad(pad_ref, length, a_ref[rows, :] * _sigmoid(g_ref[rows, :]))
            for c0 in range(0, length, CONV_CH):
                acc = _conv_chunk(pad_ref, w_ref, CONF_K, c0, lambda kk: PADR + kk - CONF_K // 2)
                yc_ref[off + c0:off + c0 + CONV_CH, :] = acc + b_ref[...]
            pad_ref[PADR:PADR + length, :] = cg_ref[rows, :] * xs_ref[rows, :]
            for c0 in range(0, length, CONV_CH):
                acc = _conv_chunk(pad_ref, w3_ref, SC_K, c0, lambda kk: PADR + kk - SC_K // 2)
                z_ref[off + c0:off + c0 + CONV_CH, :] = (bg_ref[off + c0:off + c0 + CONV_CH, :] * acc).astype(BF16)

    return _pcall(body, name=name,
                  out_shape=(jax.ShapeDtypeStruct((NROW, CW), F32), jax.ShapeDtypeStruct((NROW, CW), BF16)),
                  grid=(CB, BL),
                  in_specs=[_p_block(OFF_CONF), _p_block(OFF_CONF + CW), _p_block(OFF_SC), _p_block(OFF_SC + CW),
                            _p_block(OFF_SC + 2 * CW), _taps(CONF_K), _taps(1), _taps(SC_K)],
                  out_specs=(_conv_io(CW), _conv_io(CW)),
                  scratch=[pltpu.VMEM((SEQ + 2 * PADR, LANE), F32)],
                  sem=("parallel", "parallel"))(p_main, p_main, p_main, p_main, p_main, wdw, bdw, w3)


def _tap_grad(pad_ref, d_ref, off, length, first_row):
    acc = jnp.zeros((8, LANE), F32)
    for c0 in range(0, length, CONV_CH):
        prod = d_ref[off + c0:off + c0 + CONV_CH, :] * pad_ref[c0 + first_row:c0 + first_row + CONV_CH, :]
        acc += jnp.sum(prod.reshape(CONV_CH // 8, 8, LANE), axis=0)
    return jnp.sum(acc, axis=0, keepdims=True)


def conv_bwd(p_main, wdw, w3, dyc, dz, name):
    def body(a_ref, g_ref, bg_ref, cg_ref, xs_ref, w_ref, w3_ref, dyc_ref, dz_ref,
             da_ref, dg_ref, dbg_ref, dcg_ref, dxs_ref, dw_ref, db_ref, dw3_ref, pad_x, pad_d, dconv_ref):
        b = pl.program_id(1)

        @pl.when(b == 0)
        def _():
            dw_ref[...] = jnp.zeros_like(dw_ref)
            db_ref[...] = jnp.zeros_like(db_ref)
            dw3_ref[...] = jnp.zeros_like(dw3_ref)

        db_ref[...] += jnp.sum(dyc_ref[...], axis=0, keepdims=True)
        for off, length in CONV_SEGS:
            rows = slice(off, off + length)
            _fill_pad(pad_x, length, a_ref[rows, :] * _sigmoid(g_ref[rows, :]))
            _fill_pad(pad_d, length, dyc_ref[rows, :])
            for kk in range(CONF_K):
                dw_ref[kk:kk + 1, :] += _tap_grad(pad_x, dyc_ref, off, length, PADR + kk - CONF_K // 2)
            for c0 in range(0, length, CONV_CH):
                dh = _conv_chunk(pad_d, w_ref, CONF_K, c0, lambda kk: PADR + CONF_K // 2 - kk)
                ch = slice(off + c0, off + c0 + CONV_CH)
                sg = _sigmoid(g_ref[ch, :])
                da_ref[ch, :] = (dh * sg).astype(BF16)
                dg_ref[ch, :] = (dh * a_ref[ch, :] * sg * (1.0 - sg)).astype(BF16)
            pad_x[PADR:PADR + length, :] = cg_ref[rows, :] * xs_ref[rows, :]
            dconv_ref[rows, :] = dz_ref[rows, :] * bg_ref[rows, :]
            pad_d[PADR:PADR + length, :] = dconv_ref[rows, :]
            for kk in range(SC_K):
                dw3_ref[kk:kk + 1, :] += _tap_grad(pad_x, dconv_ref, off, length, PADR + kk - SC_K // 2)
            for c0 in range(0, length, CONV_CH):
                ch = slice(off + c0, off + c0 + CONV_CH)
                c3 = _conv_chunk(pad_x, w3_ref, SC_K, c0, lambda kk: PADR + kk - SC_K // 2)
                dbg_ref[ch, :] = (dz_ref[ch, :] * c3).astype(BF16)
                dcx = _conv_chunk(pad_d, w3_ref, SC_K, c0, lambda kk: PADR + SC_K // 2 - kk)
                dcg_ref[ch, :] = (dcx * xs_ref[ch, :]).astype(BF16)
                dxs_ref[ch, :] = (dcx * cg_ref[ch, :]).astype(BF16)

    slab = jax.ShapeDtypeStruct((NROW, CW), BF16)
    return _pcall(body, name=name,
                  out_shape=(slab,) * 5 + (jax.ShapeDtypeStruct((CONF_K, CW), F32), jax.ShapeDtypeStruct((1, CW), F32),
                                           jax.ShapeDtypeStruct((SC_K, CW), F32)),
                  grid=(CB, BL),
                  in_specs=[_p_block(OFF_CONF), _p_block(OFF_CONF + CW), _p_block(OFF_SC), _p_block(OFF_SC + CW),
                            _p_block(OFF_SC + 2 * CW), _taps(CONF_K), _taps(SC_K), _conv_io(CW), _conv_io(CW)],
                  out_specs=(_conv_io(CW),) * 5 + (_taps(CONF_K), _taps(1), _taps(SC_K)),
                  scratch=[pltpu.VMEM((SEQ + 2 * PADR, LANE), F32), pltpu.VMEM((SEQ + 2 * PADR, LANE), F32),
                           pltpu.VMEM((RE, LANE), F32)],
                  sem=("parallel", "arbitrary"))(p_main, p_main, p_main, p_main, p_main, wdw, w3, dyc, dz)


def silu_rows(x, name):
    def body(x_ref, o_ref):
        o_ref[...] = x_ref[...] * _sigmoid(x_ref[...])

    return _pcall(body, name=name, out_shape=jax.ShapeDtypeStruct(x.shape, F32))(x)


def silu_rows_bwd(x, dcs, name):
    def body(x_ref, d_ref, o_ref):
        x = x_ref[...]
        sg = _sigmoid(x)
        tot = d_ref[0]
        for i in range(1, DEPTH):
            tot += d_ref[i]
        o_ref[...] = tot * (sg * (1.0 + x * (1.0 - sg)))

    return _pcall(body, name=name, out_shape=jax.ShapeDtypeStruct(x.shape, F32))(x, dcs)


def dmod_assemble(parts, name):
    def body(p_ref, dm_ref, db_ref):
        row = lax.broadcasted_iota(jnp.int32, (8, NMOD * D), 0)
        dm = jnp.zeros((8, NMOD * D), F32)
        db = jnp.zeros((1, NMOD * D), F32)
        for s in range(2 * BL):
            target = BL if s % 2 == 0 else s // 2
            part = p_ref[s:s + 1, :]
            dm += jnp.where(row == target, part, 0.0)
            db += part
        dm_ref[...] = dm
        db_ref[...] = db

    return _pcall(body, name=name, out_shape=(jax.ShapeDtypeStruct((8, NMOD * D), F32),
                                              jax.ShapeDtypeStruct((1, NMOD * D), F32)))(parts)


def sum_leading(x, name):
    n = x.shape[0]
    tr = _pick(x.shape[1], (256, 32, 8))

    def body(x_ref, o_ref):
        tot = x_ref[0].astype(F32)
        for i in range(1, n):
            tot += x_ref[i].astype(F32)
        o_ref[...] = tot

    return _pcall(body, name=name, out_shape=jax.ShapeDtypeStruct(x.shape[1:], F32), grid=(x.shape[1] // tr,),
                  in_specs=[pl.BlockSpec((n, tr, x.shape[2]), lambda i: (0, i, 0))],
                  out_specs=pl.BlockSpec((tr, x.shape[2]), lambda i: (i, 0)), sem=("parallel",))(x)


LH = DEPTH // 2
SLAB_ROWS = (256, 176, 128, 64, 8)


def _prefetch_call(body, name, out_shape, grid, in_specs, out_specs, sem, scalars, *args):
    spec = pltpu.PrefetchScalarGridSpec(num_scalar_prefetch=len(scalars), grid=grid, in_specs=in_specs,
                                        out_specs=out_specs)
    return pl.pallas_call(body, name=name, out_shape=out_shape, grid_spec=spec,
                          compiler_params=pltpu.CompilerParams(dimension_semantics=sem,
                                                               vmem_limit_bytes=VMEM_LIMIT))(*scalars, *args)


def cast_into_slot(w, chip, name):
    depth, r, c = w.shape
    tr = _pick(r, SLAB_ROWS)

    def body(s_ref, w_ref, o_ref):
        o_ref[...] = w_ref[...].astype(BF16)

    return _prefetch_call(body, name, jax.ShapeDtypeStruct((depth, NCHIP, r, c), BF16), (depth, r // tr),
                          [pl.BlockSpec((None, tr, c), lambda l, i, s: (l, i, 0))],
                          pl.BlockSpec((None, None, tr, c), lambda l, i, s: (l, s[0], i, 0)),
                          ("parallel", "parallel"), (chip,), w)


def rs_add(g, other, core, chip, name):
    _, _, r, c = g.shape
    tr = _pick(r, SLAB_ROWS)

    def body(core_ref, chip_ref, g_ref, o_ref, send_ref, arr_ref):
        k = pl.program_id(2)
        tot = (g_ref[...].astype(F32) + o_ref[...].astype(F32)).astype(BF16)
        send_ref[...] = tot

        @pl.when(k == chip_ref[0])
        def _():
            arr_ref[...] = tot

    blk = (None, None, tr, c)
    return _prefetch_call(
        body, name, (jax.ShapeDtypeStruct(other.shape, BF16), jax.ShapeDtypeStruct(g.shape, BF16)),
        (LH, r // tr, NCHIP),
        [pl.BlockSpec(blk, lambda l, i, k, cr, ch: (cr[0] * LH + l, k, i, 0)),
         pl.BlockSpec(blk, lambda l, i, k, cr, ch: (l, k, i, 0))],
        (pl.BlockSpec(blk, lambda l, i, k, cr, ch: (l, k, i, 0)),
         pl.BlockSpec(blk, lambda l, i, k, cr, ch: (cr[0] * LH + l, ch[0], i, 0))),
        ("parallel", "parallel", "arbitrary"), (core, chip), g, other)


def adamw_sum(w, arr, m, v, name):
    depth, r, c = w.shape
    tr = _pick(r, SLAB_ROWS)
    c1 = 1.0 / (1.0 - ADAM_B1 ** ADAM_STEP)
    c2 = 1.0 / (1.0 - ADAM_B2 ** ADAM_STEP)

    def body(w_ref, a_ref, m_ref, v_ref, g_ref, d_ref, mo_ref, vo_ref):
        gv = a_ref[0].astype(F32)
        for k in range(1, NCHIP):
            gv += a_ref[k].astype(F32)
        mn = ADAM_B1 * m_ref[...] + (1.0 - ADAM_B1) * gv
        vn = ADAM_B2 * v_ref[...] + (1.0 - ADAM_B2) * (gv * gv)
        g_ref[...] = gv
        d_ref[...] = -ADAM_LR * ((mn * c1) / (jnp.sqrt(vn * c2) + ADAM_EPS) + ADAM_WD * w_ref[...])
        mo_ref[...] = mn
        vo_ref[...] = vn

    spec = pl.BlockSpec((None, tr, c), lambda l, i: (l, i, 0))
    sds = jax.ShapeDtypeStruct(w.shape, F32)
    return _pcall(body, name=name, out_shape=(sds,) * 4, grid=(depth, r // tr),
                  in_specs=[spec, pl.BlockSpec((None, NCHIP, tr, c), lambda l, i: (l, 0, i, 0)), spec, spec],
                  out_specs=(spec,) * 4, sem=("parallel", "parallel"))(w, arr, m, v)


def adamw(w, g, m, v, name):
    rows, cols = w.shape
    tr = _pick(rows, (256, 248, 128, 8))
    c1 = 1.0 / (1.0 - ADAM_B1 ** ADAM_STEP)
    c2 = 1.0 / (1.0 - ADAM_B2 ** ADAM_STEP)

    def body(w_ref, g_ref, m_ref, v_ref, d_ref, mo_ref, vo_ref):
        gv = g_ref[...]
        mn = ADAM_B1 * m_ref[...] + (1.0 - ADAM_B1) * gv
        vn = ADAM_B2 * v_ref[...] + (1.0 - ADAM_B2) * (gv * gv)
        d_ref[...] = -ADAM_LR * ((mn * c1) / (jnp.sqrt(vn * c2) + ADAM_EPS) + ADAM_WD * w_ref[...])
        mo_ref[...] = mn
        vo_ref[...] = vn

    spec = pl.BlockSpec((tr, cols), lambda i: (i, 0))
    sds = jax.ShapeDtypeStruct((rows, cols), F32)
    return _pcall(body, name=name, out_shape=(sds, sds, sds), grid=(rows // tr,), in_specs=[spec] * 4,
                  out_specs=(spec, spec, spec), sem=("parallel",))(w, g, m, v)


def _place():
    return lax.axis_index("x"), lax.axis_index("y"), lax.axis_index("c")


def _other_chips(x, y):
    return [(1 - x, y), (x, 1 - y), (1 - x, 1 - y)]


def _comm_call(body, name, out_shape, n_in, nsem):
    return pl.pallas_call(body, name=name, out_shape=out_shape, in_specs=[ANY] * n_in,
                          out_specs=jax.tree.map(lambda _: ANY, out_shape),
                          scratch_shapes=[pltpu.SemaphoreType.DMA((nsem,)), pltpu.SemaphoreType.DMA((nsem,)),
                                          pltpu.SemaphoreType.DMA])


def all_gather8(block, name):
    def body(x_ref, out_ref, send_sems, recv_sems, local_sem):
        x, y, c = _place()
        me, sibling = (x, y, c), (x, y, 1 - c)
        chips = _other_chips(x, y)

        def slot(px, py, pc):
            return out_ref.at[4 * px + 2 * py + pc]

        def copy(k, blk, to, src=None):
            return pltpu.make_async_remote_copy(src_ref=slot(*blk) if src is None else src, dst_ref=slot(*blk),
                                                send_sem=send_sems.at[k], recv_sem=recv_sems.at[k],
                                                device_id=to, device_id_type=MESH)

        mine = pltpu.make_async_copy(x_ref, slot(*me), local_sem)
        mine.start()
        first = [copy(0, me, sibling, src=x_ref)]
        first += [copy(1 + j, me, (*chip, c), src=x_ref) for j, chip in enumerate(chips)]
        for cp in first:
            cp.start()
        passed = [copy(4 + j, (*chip, c), sibling) for j, chip in enumerate(chips)]
        for j, chip in enumerate(chips):
            copy(1 + j, (*chip, c), me).wait_recv()
            passed[j].start()
        copy(0, sibling, me).wait_recv()
        for j, chip in enumerate(chips):
            copy(4 + j, (*chip, 1 - c), me).wait_recv()
        for cp in first + passed:
            cp.wait_send()
        mine.wait()

    return _comm_call(body, name, jax.ShapeDtypeStruct((8,) + block.shape, block.dtype), 1, 7)(block)


def _inplace_comm_call(body, name, n_pass, extra, nsem):
    def call(*args):
        out_shape = tuple(jax.ShapeDtypeStruct(a.shape, a.dtype) for a in args[extra:])
        return pl.pallas_call(body, name=name, out_shape=out_shape, in_specs=[ANY] * len(args),
                              out_specs=tuple(ANY for _ in out_shape),
                              input_output_aliases={extra + i: i for i in range(n_pass)},
                              scratch_shapes=[pltpu.SemaphoreType.DMA((nsem,)), pltpu.SemaphoreType.DMA((nsem,))])(*args)
    return call


def _remote(src, dst, send_sems, recv_sems, k, to):
    return pltpu.make_async_remote_copy(src_ref=src, dst_ref=dst, send_sem=send_sems.at[k], recv_sem=recv_sems.at[k],
                                        device_id=to, device_id_type=MESH)


def gather_weights(bufs, name):
    n = len(bufs)

    def body(*refs):
        outs = refs[n:2 * n]
        send_sems, recv_sems = refs[2 * n:]
        x, y, c = _place()
        sibling = (x, y, 1 - c)
        chips = _other_chips(x, y)
        own = 2 * x + y

        def slab(w, core, chip_slot):
            return outs[w].at[pl.ds(core * LH, LH), chip_slot]

        started = []
        for w in range(n):
            for j, chip in enumerate(chips):
                cp = _remote(slab(w, c, own), slab(w, c, own), send_sems, recv_sems, 6 * w + j, (*chip, c))
                cp.start()
                started.append(cp)
        for j, chip in enumerate(chips):
            theirs = 2 * chip[0] + chip[1]
            for w in range(n):
                _remote(slab(w, c, own), slab(w, c, theirs), send_sems, recv_sems, 6 * w + j, (*chip, c)).wait_recv()
                cp = _remote(slab(w, c, theirs), slab(w, c, theirs), send_sems, recv_sems, 6 * w + 3 + j, sibling)
                cp.start()
                started.append(cp)
        for j, chip in enumerate(chips):
            theirs = 2 * chip[0] + chip[1]
            for w in range(n):
                _remote(slab(w, c, own), slab(w, 1 - c, theirs), send_sems, recv_sems, 6 * w + 3 + j, sibling).wait_recv()
        for cp in started:
            cp.wait_send()

    return _inplace_comm_call(body, name, n, 0, 6 * n)(*bufs)


def rs_swap(grads, name):
    n = len(grads)

    def body(*refs):
        ins, outs = refs[:n], refs[n:2 * n]
        send_sems, recv_sems = refs[2 * n:]
        x, y, c = _place()
        copies = [_remote(ins[w].at[pl.ds((1 - c) * LH, LH)], outs[w], send_sems, recv_sems, w, (x, y, 1 - c))
                  for w in range(n)]
        for cp in copies:
            cp.start()
        for cp in copies:
            cp.wait()

    out_shape = tuple(jax.ShapeDtypeStruct((LH,) + g.shape[1:], g.dtype) for g in grads)
    return pl.pallas_call(body, name=name, out_shape=out_shape, in_specs=[ANY] * n, out_specs=tuple(ANY for _ in grads),
                          scratch_shapes=[pltpu.SemaphoreType.DMA((n,)), pltpu.SemaphoreType.DMA((n,))])(*grads)


def rs_exchange(sends, arrs, name):
    n = len(sends)

    def body(*refs):
        snd = refs[:n]
        outs = refs[2 * n:3 * n]
        send_sems, recv_sems = refs[3 * n:]
        x, y, c = _place()
        sibling = (x, y, 1 - c)
        chips = _other_chips(x, y)
        own = 2 * x + y

        def slab(w, core, chip_slot):
            return outs[w].at[pl.ds(core * LH, LH), chip_slot]

        started = []
        for w in range(n):
            for j, chip in enumerate(chips):
                cp = _remote(snd[w].at[:, 2 * chip[0] + chip[1]], slab(w, c, own), send_sems, recv_sems, 7 * w + j,
                             (*chip, c))
                cp.start()
                started.append(cp)
            cp = _remote(slab(w, c, own), slab(w, c, own), send_sems, recv_sems, 7 * w + 3, sibling)
            cp.start()
            started.append(cp)
        for j, chip in enumerate(chips):
            theirs = 2 * chip[0] + chip[1]
            for w in range(n):
                _remote(snd[w].at[:, own], slab(w, c, theirs), send_sems, recv_sems, 7 * w + j, (*chip, c)).wait_recv()
                cp = _remote(slab(w, c, theirs), slab(w, c, theirs), send_sems, recv_sems, 7 * w + 4 + j, sibling)
                cp.start()
                started.append(cp)
        for w in range(n):
            _remote(snd[w].at[:, own], slab(w, 1 - c, own), send_sems, recv_sems, 7 * w + 3, sibling).wait_recv()
        for j, chip in enumerate(chips):
            theirs = 2 * chip[0] + chip[1]
            for w in range(n):
                _remote(snd[w].at[:, own], slab(w, 1 - c, theirs), send_sems, recv_sems, 7 * w + 4 + j,
                        sibling).wait_recv()
        for cp in started:
            cp.wait_send()

    return _inplace_comm_call(body, name, n, n, 7 * n)(*sends, *arrs)


PACK_COLS = 1024
MATMUL_W = ("w_ada", "w_in", "w_attn_o", "w_conf_out", "w_sc_out", "w_mix_out", "w_ffn_in", "w_ffn_out")
ROW_SPLIT = ("w_mix_out", "w_ffn_out")
CONV_W = ("conf_dw_w", "sc_dw_w")
SMALL = ("c_ctx", "b_ada", "q_norm", "k_norm", "conf_dw_b", "conf_ln_g", "conf_ln_b", "conf_dw_w", "sc_dw_w")


def _pack_rows(arrays, row_multiple):
    flat = jnp.concatenate([a.reshape(-1) for a in arrays])
    rows = -(-flat.shape[0] // PACK_COLS)
    rows = -(-rows // row_multiple) * row_multiple
    flat = jnp.pad(flat, (0, rows * PACK_COLS - flat.shape[0]))
    return flat.reshape(rows, PACK_COLS)


def _unpack(flat2d, shapes):
    flat = flat2d.reshape(-1)
    out, pos = [], 0
    for shp in shapes:
        n = 1
        for s in shp:
            n *= s
        out.append(flat[pos:pos + n].reshape(shp))
        pos += n
    return out


def _rows_joined(stacked):
    depth, nchip, r, c = stacked.shape
    return stacked.reshape(depth, nchip * r, c)


def _cols_joined(stacked_layer):
    nchip, r, c = stacked_layer.shape
    return jnp.transpose(stacked_layer, (1, 0, 2)).reshape(r, nchip * c)


def _cols_split(full):
    r, cols = full.shape
    return jnp.transpose(full.reshape(r, NCHIP, cols // NCHIP), (1, 0, 2))


def _rope_tables():
    rows = SEQ // GRID_W
    r_ids = jnp.repeat(jnp.arange(rows, dtype=F32), GRID_W)
    c_ids = jnp.tile(jnp.arange(GRID_W, dtype=F32), rows)
    freqs = ROPE_THETA ** (-jnp.arange(0, HD // 2, 2, dtype=F32) / (HD // 2))
    ang_r, ang_c = r_ids[:, None] * freqs, c_ids[:, None] * freqs
    cos_h = jnp.concatenate([jnp.cos(ang_r), jnp.cos(ang_r), jnp.cos(ang_c), jnp.cos(ang_c)], axis=1)
    sin_h = jnp.concatenate([-jnp.sin(ang_r), jnp.sin(ang_r), -jnp.sin(ang_c), jnp.sin(ang_c)], axis=1)
    cos_t = jnp.concatenate([jnp.ones((CTX, HD), F32), cos_h], axis=0)
    sin_t = jnp.concatenate([jnp.zeros((CTX, HD), F32), sin_h], axis=0)
    return jnp.tile(cos_t, (1, LANE // HD)), jnp.tile(sin_t, (1, LANE // HD))


def _group_matrix():
    gid = jnp.arange(LANE) // HD
    return jnp.where(gid[:, None] == gid[None, :], 1.0 / HD, 0.0).astype(F32)


def _layer_fwd(i, xs, mods, w, tabs):
    cos_t, sin_t, g_mat = tabs
    n = f"l{i}_"
    sv = {"x_in": xs, "mods": mods}
    sv["h"] = norm_mod_fwd(xs, mods, 0, 1, n + "norm1")
    sv["p_main"] = mm_nn(sv["h"], w["wi_main"], name=n + "p_main")
    sv["q"], sv["k"], sv["v"] = qkv_fwd(sv["p_main"], cos_t, sin_t, g_mat, w["gq"], w["gk"], n + "qkv")
    sv["o"] = attn_fwd(sv["q"], sv["k"], sv["v"], n + "attn")
    sv["yc"], sv["z"] = conv_fwd(sv["p_main"], w["conf_dw_w"], w["conf_dw_b"], w["sc_dw_w"], n + "conv")
    sv["hs"] = ln_silu_fwd(sv["yc"], w["conf_ln_g"], w["conf_ln_b"], n + "ln_silu")
    sv["merged"], sv["gates"], sv["ys"] = gate_mm_fwd(sv["h"], w["wi_gate"][0], sv["o"], sv["hs"], sv["z"],
                                                      w["w_attn_o"][0], w["w_conf_out"][0], w["w_sc_out"][0],
                                                      n + "gate_merge")
    sv["mixed"] = mm_nn(sv["merged"], w["w_mix_out"], name=n + "mix")
    sv["x1"] = gate_resid_fwd(xs, sv["mixed"], mods, 2, n + "resid1")
    sv["h2"] = norm_mod_fwd(sv["x1"], mods, 3, 4, n + "norm2")
    sv["f"], sv["u2"] = ffn_in_swiglu(sv["h2"], w["w_ffn_in"][0], i, n + "ffn_in")
    sv["of"] = mm_nn(sv["f"], w["w_ffn_out"], name=n + "ffn_out")
    x2 = gate_resid_fwd(sv["x1"], sv["of"], mods, 5, n + "resid2")
    return x2, sv


def _layer_bwd(i, dx2, sv, w, tabs, cs, gbuf):
    cos_t, sin_t, g_mat = tabs
    n = f"l{i}b_"
    mods = sv["mods"]
    g = {}

    def wgrad(a, b, key, kind, name):
        gbuf[key] = mm_tn(a, b, into=(gbuf[key], i, kind), name=n + name)

    dof, dm5 = gate_resid_bwd(dx2, sv["of"], mods, 5, n + "resid2")
    du = d_f_swiglu(dof, w["w_ffn_out"][0], i, sv["u2"], n + "d_f")
    wgrad(sv["f"], dof, "w_ffn_out", "plain", "dw_ffn_out")
    dh2 = mm_nt(du, w["w_ffn_in"], name=n + "d_h2")
    wgrad(sv["h2"], du, "w_ffn_in", "cols", "dw_ffn_in")
    dx1, dm34 = norm_mod_bwd(sv["x1"], mods, dh2, dx2, 4, n + "norm2")
    dmixed, dm2 = gate_resid_bwd(dx1, sv["mixed"], mods, 2, n + "resid1")
    dya, dyb, dys, dp_gate = d_merged_gate(dmixed, w["w_mix_out"][0], i, sv["gates"], sv["ys"], n + "d_merged")
    wgrad(sv["merged"], dmixed, "w_mix_out", "plain", "dw_mix")
    do = mm_nt(dya, w["w_attn_o"], out_dtype=BF16, name=n + "d_o")
    g["w_attn_o"] = _cols_split(mm_tn(sv["o"], dya, out_dtype=BF16, name=n + "dw_attn_o"))
    dhs = mm_nt(dyb, w["w_conf_out"], name=n + "d_hs")
    g["w_conf_out"] = _cols_split(mm_tn(sv["hs"], dyb, out_dtype=BF16, name=n + "dw_conf_out"))
    dz = mm_nt(dys, w["w_sc_out"], name=n + "d_z")
    g["w_sc_out"] = _cols_split(mm_tn(sv["z"], dys, out_dtype=BF16, name=n + "dw_sc_out"))
    dyc, g["conf_ln_g"], g["conf_ln_b"] = ln_silu_bwd(sv["yc"], w["conf_ln_g"], w["conf_ln_b"], dhs, n + "ln_silu")
    da, dg, dbg, dcg, dxs, g["conf_dw_w"], g["conf_dw_b"], g["sc_dw_w"] = conv_bwd(
        sv["p_main"], w["conf_dw_w"], w["sc_dw_w"], dyc, dz, n + "conv")
    dq, dk, dv = attn_bwd(sv["q"], sv["k"], sv["v"], do, n + "attn")
    dp_qkv, dgqk = qkv_bwd(sv["p_main"], cos_t, sin_t, g_mat, w["gq"], w["gk"], dq, dk, dv, n + "qkv")
    dp_main = jnp.concatenate([dp_qkv, da, dg, dbg, dcg, dxs], axis=1)
    dh = mm_nt(dp_main, w["wi_main"], name=n + "d_h_main")
    dh = mm_nt(dp_gate, w["wi_gate"], acc=dh, name=n + "d_h_gate")
    g["w_in"] = _cols_split(jnp.concatenate([mm_tn(sv["h"], dp_main, out_dtype=BF16, name=n + "dw_in_main"),
                                             mm_tn(sv["h"], dp_gate, out_dtype=BF16, name=n + "dw_in_gate")], axis=1))
    dx_in, dm01 = norm_mod_bwd(sv["x_in"], mods, dh, dx1, 1, n + "norm1")
    parts = jnp.concatenate([dm01, dm2, dm34, dm5], axis=2).reshape(2 * BL, NMOD * D)
    dmod, g["b_ada"] = dmod_assemble(parts, n + "dmod")
    wgrad(cs, dmod, "w_ada", "cols", "dw_ada")
    g["dcs"] = mm_nt(dmod, w["w_ada"], name=n + "d_cs")
    g["q_norm"] = dgqk[0, :AW].reshape(NQ, HD).sum(axis=0)
    g["k_norm"] = dgqk[0, OFF_K:OFF_K + KVW].reshape(NKV, HD).sum(axis=0)
    return dx_in, g


def kernel(x, c, ctx, c_ctx, w_ada, b_ada, w_in, q_norm, k_norm, w_attn_o, conf_dw_w, conf_dw_b, conf_ln_g, conf_ln_b, w_conf_out, sc_dw_w, w_sc_out, w_mix_out, w_ffn_in, w_ffn_out, loss_target, m_c_ctx, m_w_ada, m_b_ada, m_w_in, m_q_norm, m_k_norm, m_w_attn_o, m_conf_dw_w, m_conf_dw_b, m_conf_ln_g, m_conf_ln_b, m_w_conf_out, m_sc_dw_w, m_w_sc_out, m_w_mix_out, m_w_ffn_in, m_w_ffn_out, v_c_ctx, v_w_ada, v_b_ada, v_w_in, v_q_norm, v_k_norm, v_w_attn_o, v_conf_dw_w, v_conf_dw_b, v_conf_ln_g, v_conf_ln_b, v_w_conf_out, v_sc_dw_w, v_w_sc_out, v_w_mix_out, v_w_ffn_in, v_w_ffn_out):
    local = dict(c_ctx=c_ctx, w_ada=w_ada, b_ada=b_ada, w_in=w_in, q_norm=q_norm, k_norm=k_norm, w_attn_o=w_attn_o,
                 conf_dw_w=conf_dw_w, conf_dw_b=conf_dw_b, conf_ln_g=conf_ln_g, conf_ln_b=conf_ln_b,
                 w_conf_out=w_conf_out, sc_dw_w=sc_dw_w, w_sc_out=w_sc_out, w_mix_out=w_mix_out, w_ffn_in=w_ffn_in,
                 w_ffn_out=w_ffn_out)
    mom_m = dict(c_ctx=m_c_ctx, w_ada=m_w_ada, b_ada=m_b_ada, w_in=m_w_in, q_norm=m_q_norm, k_norm=m_k_norm,
                 w_attn_o=m_w_attn_o, conf_dw_w=m_conf_dw_w, conf_dw_b=m_conf_dw_b, conf_ln_g=m_conf_ln_g,
                 conf_ln_b=m_conf_ln_b, w_conf_out=m_w_conf_out, sc_dw_w=m_sc_dw_w, w_sc_out=m_w_sc_out,
                 w_mix_out=m_w_mix_out, w_ffn_in=m_w_ffn_in, w_ffn_out=m_w_ffn_out)
    mom_v = dict(c_ctx=v_c_ctx, w_ada=v_w_ada, b_ada=v_b_ada, w_in=v_w_in, q_norm=v_q_norm, k_norm=v_k_norm,
                 w_attn_o=v_w_attn_o, conf_dw_w=v_conf_dw_w, conf_dw_b=v_conf_dw_b, conf_ln_g=v_conf_ln_g,
                 conf_ln_b=v_conf_ln_b, w_conf_out=v_w_conf_out, sc_dw_w=v_sc_dw_w, w_sc_out=v_w_sc_out,
                 w_mix_out=v_w_mix_out, w_ffn_in=v_w_ffn_in, w_ffn_out=v_w_ffn_out)
    order = ("c_ctx", "w_ada", "b_ada", "w_in", "q_norm", "k_norm", "w_attn_o", "conf_dw_w", "conf_dw_b", "conf_ln_g",
             "conf_ln_b", "w_conf_out", "sc_dw_w", "w_sc_out", "w_mix_out", "w_ffn_in", "w_ffn_out")
    core = lax.axis_index("c").astype(jnp.int32)
    chip = (2 * lax.axis_index("x") + lax.axis_index("y")).astype(jnp.int32)

    own = [cast_into_slot(local[k], chip.reshape(1), "cast_" + k) for k in MATMUL_W]
    wg = dict(zip(MATMUL_W, gather_weights(own, "gather_weights")))
    conv_shapes = [local[k].shape for k in CONV_W]
    conv_all = all_gather8(_pack_rows([local[k] for k in CONV_W], 8), "gather_conv_taps")
    per_chip = [_unpack(conv_all[2 * s], conv_shapes) for s in range(NCHIP)]
    full_conv = {k: jnp.concatenate([per_chip[s][i] for s in range(NCHIP)], axis=2) for i, k in enumerate(CONV_W)}

    loss_local, grad_x, gbuf, small_g = local_step(x, c, ctx, c_ctx, wg, full_conv, b_ada, q_norm, k_norm,
                                                   conf_dw_b, conf_ln_g, conf_ln_b, loss_target)
    loss = lax.psum(loss_local, ("x", "y", "c"))

    partial = [gbuf[k] for k in MATMUL_W]
    from_sibling = rs_swap(partial, "rs_swap")
    sends, arrs = zip(*[rs_add(g_, o_, core.reshape(1), chip.reshape(1), "rs_add_" + k)
                        for k, g_, o_ in zip(MATMUL_W, partial, from_sibling)])
    arrs = dict(zip(MATMUL_W, rs_exchange(sends, arrs, "rs_exchange")))

    small_shapes = [small_g[k].shape for k in SMALL]
    small_sum = sum_leading(all_gather8(_pack_rows([small_g[k] for k in SMALL], 8), "gather_small_grads"), "small_sum")
    small_g = dict(zip(SMALL, _unpack(small_sum, small_shapes)))
    for k in CONV_W:
        width = local[k].shape[2]
        small_g[k] = lax.dynamic_slice_in_dim(small_g[k], chip * width, width, axis=2)

    grad, delta, new_m, new_v = {}, {}, {}, {}
    for k in order:
        if k in arrs:
            grad[k], delta[k], new_m[k], new_v[k] = adamw_sum(local[k], arrs[k], mom_m[k], mom_v[k], "adamw_" + k)
            continue
        shp = local[k].shape
        view = (1, shp[0]) if len(shp) == 1 else (-1, shp[-1])
        d_, m_, v_ = adamw(local[k].reshape(view), small_g[k].reshape(view), mom_m[k].reshape(view),
                           mom_v[k].reshape(view), "adamw_" + k)
        grad[k], delta[k], new_m[k], new_v[k] = small_g[k], d_.reshape(shp), m_.reshape(shp), v_.reshape(shp)
    return (loss, grad_x, *[grad[k] for k in order], *[delta[k] for k in order], *[new_m[k] for k in order],
            *[new_v[k] for k in order])


def local_step(x, c, ctx, c_ctx, wg, full_conv, b_ada, q_norm, k_norm, conf_dw_b, conf_ln_g, conf_ln_b, loss_target):
    tabs = _rope_tables() + (_group_matrix(),)
    plain = {k: _rows_joined(wg[k]) for k in ROW_SPLIT}
    layer_w = []
    for i in range(DEPTH):
        wi = _cols_joined(wg["w_in"][i])
        layer_w.append(dict(
            w_ada=(wg["w_ada"], i, "cols"),
            wi_main=(wi[:, :OFF_GATE], 0, "mat"), wi_gate=(wi[:, OFF_GATE:], 0, "mat"),
            w_attn_o=(_cols_joined(wg["w_attn_o"][i]), 0, "mat"), w_conf_out=(_cols_joined(wg["w_conf_out"][i]), 0, "mat"),
            w_sc_out=(_cols_joined(wg["w_sc_out"][i]), 0, "mat"), w_ffn_in=(wg["w_ffn_in"], i, "cols"),
            w_mix_out=(plain["w_mix_out"], i, "plain"), w_ffn_out=(plain["w_ffn_out"], i, "plain"),
            conf_dw_w=full_conv["conf_dw_w"][i], sc_dw_w=full_conv["sc_dw_w"][i],
            conf_dw_b=conf_dw_b[i][None], conf_ln_g=conf_ln_g[i][None], conf_ln_b=conf_ln_b[i][None],
            gq=jnp.tile(q_norm[i], LANE // HD)[None], gk=jnp.tile(k_norm[i], LANE // HD)[None]))

    cin = jnp.concatenate([c, c_ctx[None], jnp.zeros((8 - BL - 1, D), F32)], axis=0)
    cs = silu_rows(cin, "silu_c")
    xs = jnp.concatenate([ctx, x], axis=1).reshape(NROW, D)
    saved = []
    for i in range(DEPTH):
        mods = mm_nn(cs, layer_w[i]["w_ada"], bias=b_ada[i][None], name=f"l{i}_mod").reshape(8, 1, NMOD * D)
        xs, sv = _layer_fwd(i, xs, mods, layer_w[i], tabs)
        saved.append(sv)
    dxs, loss_lanes = loss_fwd_bwd(xs, loss_target.reshape(BL * SEQ, D), "loss")
    loss_local = 0.5 * jnp.sum(loss_lanes) / D

    glue_grads = ("w_in", "w_attn_o", "w_conf_out", "w_sc_out")
    gbuf = {k: lax.empty(plain[k].shape if k in ROW_SPLIT else wg[k].shape, BF16)
            for k in MATMUL_W if k not in glue_grads}
    grads = [None] * DEPTH
    for i in reversed(range(DEPTH)):
        dxs, grads[i] = _layer_bwd(i, dxs, saved[i], layer_w[i], tabs, cs, gbuf)
    grad_x = dxs.reshape(BL, RE, D)[:, CTX:, :]
    dcin = silu_rows_bwd(cin, jnp.stack([grads[i]["dcs"] for i in range(DEPTH)]), "silu_c_bwd")

    def stack(key):
        return jnp.stack([grads[i][key] for i in range(DEPTH)])

    for k in glue_grads:
        gbuf[k] = stack(k)
    for k in ROW_SPLIT:
        gbuf[k] = gbuf[k].reshape(wg[k].shape)
    small_g = dict(c_ctx=dcin[BL], b_ada=stack("b_ada").reshape(DEPTH, NMOD * D), q_norm=stack("q_norm"),
                   k_norm=stack("k_norm"), conf_dw_b=stack("conf_dw_b").reshape(DEPTH, CW),
                   conf_ln_g=stack("conf_ln_g").reshape(DEPTH, CW), conf_ln_b=stack("conf_ln_b").reshape(DEPTH, CW),
                   conf_dw_w=stack("conf_dw_w"), sc_dw_w=stack("sc_dw_w"))
    return loss_local, grad_x, gbuf, small_g
```

```python
import functools
from typing import Any, Callable, NamedTuple, Sequence

import jax
import jax.numpy as jnp
from jax import lax
from jax.experimental import pallas as pl
from jax.experimental.pallas import tpu as pltpu

F32, BF16 = jnp.float32, jnp.bfloat16
HIGHEST = lax.Precision.HIGHEST

D = 1024
SEQ = 2048
CTX = 256
DEPTH = 4
BL = 4
GRID_W = 64
HD = 64
NQ = 8
NKV = 2
AW = NQ * HD
KVW = NKV * HD
CW = D // 2
CONF_K = 31
SC_K = 3
NMOD = 6
FH = -(-8 * D // (3 * 256)) * 256
EPS = 1e-6
ROPE_THETA = 10000.0
ATTN_SCALE = HD ** -0.5
OFF_K = AW
OFF_V = OFF_K + KVW
OFF_CONF = OFF_V + KVW
OFF_SC = OFF_CONF + 2 * CW
OFF_GATE = OFF_SC + 3 * CW
IN_W = OFF_GATE + 3 * D
QKVW = OFF_CONF
NCHIP = 4

ADAM_LR, ADAM_B1, ADAM_B2, ADAM_EPS, ADAM_WD, ADAM_STEP = 0.001, 0.9, 0.999, 1e-08, 0.01, 10

TM = CTX
RE = CTX + SEQ
TPE = RE // TM
NROW = BL * RE
NT = NROW // TM
LANE = 128
CB = CW // LANE
CONV_CH = 128
PADR = 16
VMEM_LIMIT = 52 * 1024 * 1024

MESH = pl.DeviceIdType.MESH
ANY = pl.BlockSpec(memory_space=pl.ANY)


class CommSpec(NamedTuple):
    ro: Sequence[Any]
    rw: Sequence[Any]
    new: Sequence[Any]
    nsem: int
    program: Callable


def _pcall(body, *, name, out_shape, grid=(), in_specs=None, out_specs=None, scratch=(), sem=None, comm=None):
    if not grid:
        return pl.pallas_call(body, name=name, out_shape=out_shape)
    if comm is None:
        params = pltpu.CompilerParams(dimension_semantics=sem, vmem_limit_bytes=VMEM_LIMIT)
        return pl.pallas_call(body, name=name, out_shape=out_shape, grid=grid, in_specs=in_specs, out_specs=out_specs,
                              scratch_shapes=list(scratch), compiler_params=params)

    single = not isinstance(out_shape, (tuple, list))
    out_shapes = (out_shape,) if single else tuple(out_shape)
    out_specs_t = (out_specs,) if single else tuple(out_specs)
    n_in, n_out, n_scr = len(in_specs), len(out_shapes), len(scratch)
    n_ro, n_rw, n_new = len(comm.ro), len(comm.rw), len(comm.new)

    def carrier(*refs):
        ins = refs[:n_in]
        ro_refs = refs[n_in:n_in + n_ro]
        o0 = n_in + n_ro + n_rw
        outs = refs[o0:o0 + n_out]
        rw_refs = refs[o0 + n_out:o0 + n_out + n_rw]
        new_refs = refs[o0 + n_out + n_rw:o0 + n_out + n_rw + n_new]
        s0 = o0 + n_out + n_rw + n_new
        scr = refs[s0:s0 + n_scr]
        send_sems, recv_sems = refs[s0 + n_scr:]
        first = functools.reduce(jnp.logical_and, [pl.program_id(a) == 0 for a in range(len(grid))])
        last = functools.reduce(jnp.logical_and, [pl.program_id(a) == grid[a] - 1 for a in range(len(grid))])
        starts, arrivals = comm.program(ro_refs, rw_refs, new_refs, send_sems, recv_sems)

        @pl.when(first)
        def _():
            for cp in starts:
                cp.start()

        body(*ins, *outs, *scr)

        @pl.when(last)
        def _():
            for cp in arrivals:
                cp.wait_recv()
            for cp in starts:
                cp.wait_send()

    def call(*args):
        rw_shapes = tuple(jax.ShapeDtypeStruct(a.shape, a.dtype) for a in comm.rw)
        res = pl.pallas_call(
            carrier, name=name, out_shape=out_shapes + rw_shapes + tuple(comm.new), grid=grid,
            in_specs=list(in_specs) + [ANY] * (n_ro + n_rw),
            out_specs=out_specs_t + (ANY,) * (n_rw + n_new),
            scratch_shapes=list(scratch) + [pltpu.SemaphoreType.DMA((comm.nsem,)), pltpu.SemaphoreType.DMA((comm.nsem,))],
            input_output_aliases={n_in + n_ro + i: n_out + i for i in range(n_rw)},
            compiler_params=pltpu.CompilerParams(dimension_semantics=("arbitrary",) * len(grid),
                                                 vmem_limit_bytes=VMEM_LIMIT))(*args, *comm.ro, *comm.rw)
        compute = res[0] if single else tuple(res[:n_out])
        return compute, list(res[n_out:n_out + n_rw]), list(res[n_out + n_rw:])

    return call


def comm_only(name, comm):
    n_ro, n_rw, n_new = len(comm.ro), len(comm.rw), len(comm.new)

    def body(*refs):
        ro_refs = refs[:n_ro]
        rw_refs = refs[n_ro + n_rw:n_ro + 2 * n_rw]
        new_refs = refs[n_ro + 2 * n_rw:n_ro + 2 * n_rw + n_new]
        send_sems, recv_sems = refs[n_ro + 2 * n_rw + n_new:]
        starts, arrivals = comm.program(ro_refs, rw_refs, new_refs, send_sems, recv_sems)
        for cp in starts:
            cp.start()
        for cp in arrivals:
            cp.wait_recv()
        for cp in starts:
            cp.wait_send()

    rw_shapes = tuple(jax.ShapeDtypeStruct(a.shape, a.dtype) for a in comm.rw)
    res = pl.pallas_call(body, name=name, out_shape=rw_shapes + tuple(comm.new), in_specs=[ANY] * (n_ro + n_rw),
                         out_specs=(ANY,) * (n_rw + n_new), input_output_aliases={n_ro + i: i for i in range(n_rw)},
                         scratch_shapes=[pltpu.SemaphoreType.DMA((comm.nsem,)), pltpu.SemaphoreType.DMA((comm.nsem,))])(
                             *comm.ro, *comm.rw)
    return list(res[:n_rw]), list(res[n_rw:])


def _pick(n, cands):
    for t in cands:
        if n % t == 0:
            return t
    return n


def _seg(t):
    return jnp.where(t % TPE == 0, BL, t // TPE)


def _slot(t):
    return 2 * (t // TPE) + jnp.where(t % TPE == 0, 0, 1)


def _sigmoid(x):
    return 1.0 / (1.0 + jnp.exp(-x))


MM_BUDGET = 40 * 1024 * 1024
N_TILE_CAP = 1664


def _tile(n, cap=N_TILE_CAP):
    if n <= cap:
        return n
    for t in range(cap - cap % LANE, 0, -LANE):
        if n % t == 0:
            return t
    return n


def _row_tile(m, bytes_of):
    for tm in (1024, 512, 256, 128):
        if m % tm == 0 and bytes_of(tm) <= MM_BUDGET:
            return tm
    return m


def _w_dims(w):
    arr, kind = w
    if kind == "cols":
        return arr.shape[1], NCHIP * arr.shape[2]
    return arr.shape


def _sz(dtype):
    return jnp.dtype(dtype).itemsize


def mm_nn(a, w, *, bias=None, out_dtype=F32, name):
    arr, kind = w
    m, k = a.shape
    _, n = _w_dims(w)
    tn = _tile(arr.shape[2]) if kind == "cols" else _tile(n)
    tm = _row_tile(m, lambda t: 2 * (t * k * _sz(a.dtype) + k * tn * 2 + t * tn * _sz(out_dtype)))
    if kind == "mat":
        b_spec = pl.BlockSpec((k, tn), lambda j, i: (0, j))
    else:
        per = arr.shape[2] // tn
        b_spec = pl.BlockSpec((None, k, tn), lambda j, i: (j // per, 0, j % per))
    has_bias = bias is not None

    def body(*refs):
        out = jnp.dot(refs[0][...].astype(BF16), refs[1][...].astype(BF16), preferred_element_type=F32)
        if has_bias:
            out = out + refs[2][...]
        refs[-1][...] = out.astype(out_dtype)

    in_specs = [pl.BlockSpec((tm, k), lambda j, i: (i, 0)), b_spec]
    args = [a, arr]
    if has_bias:
        in_specs.append(pl.BlockSpec((1, tn), lambda j, i: (0, j)))
        args.append(bias)
    return _pcall(body, name=name, out_shape=jax.ShapeDtypeStruct((m, n), out_dtype), grid=(n // tn, m // tm),
                  in_specs=in_specs, out_specs=pl.BlockSpec((tm, tn), lambda j, i: (i, j)),
                  sem=("parallel", "parallel"))(*args)


def mm_nt(a, w, *, acc=None, out_dtype=F32, name):
    arr, kind = w
    kdim, _ = _w_dims(w)
    has_acc = acc is not None
    tk = _tile(kdim, 1408)
    if kind == "cols":
        c = arr.shape[2]
        m = a.shape[-2]
        if a.ndim == 3:
            a_spec = lambda t: pl.BlockSpec((None, t, c), lambda j, i, s: (s // 2, i, s % 2))
        else:
            a_spec = lambda t: pl.BlockSpec((t, c), lambda j, i, s: (i, s))
        tm = _row_tile(m, lambda t: 2 * (t * c * _sz(a.dtype) + tk * c * 2 + t * tk * _sz(out_dtype)) + t * tk * 4)

        def body(a_ref, b_ref, o_ref, acc_ref):
            s = pl.program_id(2)

            @pl.when(s == 0)
            def _():
                acc_ref[...] = jnp.zeros_like(acc_ref)

            acc_ref[...] += lax.dot_general(a_ref[...].astype(BF16), b_ref[...], (((1,), (1,)), ((), ())),
                                            preferred_element_type=F32)

            @pl.when(s == NCHIP - 1)
            def _():
                o_ref[...] = acc_ref[...].astype(out_dtype)

        return _pcall(body, name=name, out_shape=jax.ShapeDtypeStruct((m, kdim), out_dtype),
                      grid=(kdim // tk, m // tm, NCHIP),
                      in_specs=[a_spec(tm), pl.BlockSpec((None, tk, c), lambda j, i, s: (s, j, 0))],
                      out_specs=pl.BlockSpec((tm, tk), lambda j, i, s: (i, j)),
                      scratch=[pltpu.VMEM((tm, tk), F32)],
                      sem=("parallel", "parallel", "arbitrary"))(a, arr)

    m, n = a.shape
    tm = _row_tile(m, lambda t: 2 * (t * n * _sz(a.dtype) + tk * n * 2 + t * tk * (_sz(out_dtype) + 4 * has_acc)))
    b_spec = pl.BlockSpec((tk, n), lambda j, i: (j, 0))

    def body(*refs):
        out = lax.dot_general(refs[0][...].astype(BF16), refs[1][...].astype(BF16), (((1,), (1,)), ((), ())),
                              preferred_element_type=F32)
        if has_acc:
            out = out + refs[2][...]
        refs[-1][...] = out.astype(out_dtype)

    in_specs = [pl.BlockSpec((tm, n), lambda j, i: (i, 0)), b_spec]
    args = [a, arr]
    if has_acc:
        in_specs.append(pl.BlockSpec((tm, tk), lambda j, i: (i, j)))
        args.append(acc)
    return _pcall(body, name=name, out_shape=jax.ShapeDtypeStruct((m, kdim), out_dtype), grid=(kdim // tk, m // tm),
                  in_specs=in_specs, out_specs=pl.BlockSpec((tm, tk), lambda j, i: (i, j)),
                  sem=("parallel", "parallel"))(*args)


def mm_tn(a, b, *, cols=False, out_dtype=F32, name):
    rows, k = a.shape
    halves = b.ndim == 3
    n = 2 * b.shape[2] if halves else b.shape[1]
    odt = out_dtype
    if cols:
        c = n // NCHIP
        tn, tk = _tile(c), k
        per = c // tn
        out_spec = pl.BlockSpec((None, tk, tn), lambda i, j, r: (j // per, 0, j % per))
        out_shape = jax.ShapeDtypeStruct((NCHIP, k, c), odt)
    else:
        tn, tk = _tile(n), _tile(k, 1408)
        out_spec = pl.BlockSpec((tk, tn), lambda i, j, r: (i, j))
        out_shape = jax.ShapeDtypeStruct((k, n), odt)
    tr = _row_tile(rows, lambda t: 2 * (t * tk * _sz(a.dtype) + t * tn * _sz(b.dtype) + tk * tn * _sz(odt)) + tk * tn * 4)
    nsteps = rows // tr

    def body(*refs):
        a_ref, b_ref = refs[0], refs[1]
        o_ref, acc_ref = refs[-2], refs[-1]
        r = pl.program_id(2)

        @pl.when(r == 0)
        def _():
            acc_ref[...] = jnp.zeros_like(acc_ref)

        acc_ref[...] += lax.dot_general(a_ref[...].astype(BF16), b_ref[...].astype(BF16), (((0,), (0,)), ((), ())),
                                        preferred_element_type=F32)

        @pl.when(r == nsteps - 1)
        def _():
            o_ref[...] = acc_ref[...].astype(odt)

    if halves:
        per_half = (n // 2) // tn
        b_spec = pl.BlockSpec((None, tr, tn), lambda i, j, r: (j // per_half, r, j % per_half))
    else:
        b_spec = pl.BlockSpec((tr, tn), lambda i, j, r: (r, j))
    return _pcall(body, name=name, out_shape=out_shape, grid=(k // tk, n // tn, nsteps),
                  in_specs=[pl.BlockSpec((tr, tk), lambda i, j, r: (r, i)), b_spec], out_specs=out_spec,
                  scratch=[pltpu.VMEM((tk, tn), F32)], sem=("parallel", "parallel", "arbitrary"))(a, b)


def ffn_in_swiglu(h, w_in, name, comm=None):
    m, k = h.shape
    c = w_in.shape[2]
    tm = 512

    def body(h_ref, wa_ref, wb_ref, f_ref, u_ref):
        hv = h_ref[...]
        a = jnp.dot(hv, wa_ref[...], preferred_element_type=F32)
        b = jnp.dot(hv, wb_ref[...], preferred_element_type=F32)
        f_ref[...] = (a * _sigmoid(a) * b).astype(BF16)
        u_ref[0] = a.astype(BF16)
        u_ref[1] = b.astype(BF16)

    return _pcall(body, name=name,
                  out_shape=(jax.ShapeDtypeStruct((m, FH), BF16), jax.ShapeDtypeStruct((2, m, FH), BF16)),
                  grid=(2, m // tm),
                  in_specs=[pl.BlockSpec((tm, k), lambda j, i: (i, 0)),
                            pl.BlockSpec((None, k, c), lambda j, i: (j, 0, 0)),
                            pl.BlockSpec((None, k, c), lambda j, i: (2 + j, 0, 0))],
                  out_specs=(pl.BlockSpec((tm, c), lambda j, i: (i, j)), pl.BlockSpec((2, tm, c), lambda j, i: (0, i, j))),
                  sem=("parallel", "parallel"), comm=comm)(h, w_in, w_in)


def d_f_swiglu(dof, w_out, u2, name, comm=None):
    m, k = dof.shape
    c = FH // 2
    tm = 512

    def body(d_ref, w_ref, u_ref, du_ref):
        df = lax.dot_general(d_ref[...], w_ref[...], (((1,), (1,)), ((), ())), preferred_element_type=F32)
        a, b = u_ref[0].astype(F32), u_ref[1].astype(F32)
        sg = _sigmoid(a)
        du_ref[0] = (df * b * (sg * (1.0 + a * (1.0 - sg)))).astype(BF16)
        du_ref[1] = (df * a * sg).astype(BF16)

    ublk = pl.BlockSpec((2, tm, c), lambda j, i: (0, i, j))
    return _pcall(body, name=name, out_shape=jax.ShapeDtypeStruct((2, m, FH), BF16), grid=(2, m // tm),
                  in_specs=[pl.BlockSpec((tm, k), lambda j, i: (i, 0)), pl.BlockSpec((c, k), lambda j, i: (j, 0)), ublk],
                  out_specs=ublk, sem=("parallel", "parallel"), comm=comm)(dof, w_out, u2)


GATE_TN = 512


def gate_mm_fwd(h, wi_gate, o, hs, z, wo, wc, ws, name):
    m, k = h.shape
    tm, tn = 512, min(GATE_TN, D)
    nj = D // tn

    def body(h_ref, g0_ref, g1_ref, g2_ref, o_ref, hs_ref, z_ref, wo_ref, wc_ref, ws_ref, m_ref, g_ref, y_ref):
        hv = h_ref[...]
        acc = jnp.zeros((tm, tn), F32)
        for g, (gw_ref, x_ref, w_ref) in enumerate(((g0_ref, o_ref, wo_ref), (g1_ref, hs_ref, wc_ref),
                                                    (g2_ref, z_ref, ws_ref))):
            gate = _sigmoid(jnp.dot(hv, gw_ref[...], preferred_element_type=F32))
            y = jnp.dot(x_ref[...], w_ref[...], preferred_element_type=F32)
            acc += gate * y
            g_ref[g] = gate.astype(BF16)
            y_ref[g] = y.astype(BF16)
        m_ref[...] = acc.astype(BF16)

    def gate_w(g):
        return pl.BlockSpec((k, tn), lambda j, i: (0, g * nj + j))

    def branch(width):
        return pl.BlockSpec((tm, width), lambda j, i: (i, 0))

    def branch_w(width):
        return pl.BlockSpec((width, tn), lambda j, i: (0, j))

    stacked = pl.BlockSpec((3, tm, tn), lambda j, i: (0, i, j))
    sds3 = jax.ShapeDtypeStruct((3, m, D), BF16)
    return _pcall(body, name=name, out_shape=(jax.ShapeDtypeStruct((m, D), BF16), sds3, sds3), grid=(nj, m // tm),
                  in_specs=[pl.BlockSpec((tm, k), lambda j, i: (i, 0)), gate_w(0), gate_w(1), gate_w(2),
                            branch(o.shape[1]), branch(hs.shape[1]), branch(z.shape[1]),
                            branch_w(wo.shape[0]), branch_w(wc.shape[0]), branch_w(ws.shape[0])],
                  out_specs=(pl.BlockSpec((tm, tn), lambda j, i: (i, j)), stacked, stacked),
                  sem=("parallel", "parallel"))(h, wi_gate, wi_gate, wi_gate, o, hs, z, wo, wc, ws)


def d_merged_gate(dmixed, w_mix, gates, ys, name):
    m, k = dmixed.shape
    tm = 256

    def body(d_ref, w_ref, g_ref, y_ref, da_ref, db_ref, ds_ref, dp_ref):
        dm = lax.dot_general(d_ref[...], w_ref[...], (((1,), (1,)), ((), ())), preferred_element_type=F32)
        for g, dy_ref in enumerate((da_ref, db_ref, ds_ref)):
            gate = g_ref[g].astype(F32)
            dy_ref[...] = (dm * gate).astype(BF16)
            dp_ref[:, g * D:(g + 1) * D] = (dm * y_ref[g].astype(F32) * gate * (1.0 - gate)).astype(BF16)

    stacked = pl.BlockSpec((3, tm, D), lambda i: (0, i, 0))
    row = pl.BlockSpec((tm, D), lambda i: (i, 0))
    sds = jax.ShapeDtypeStruct((m, D), BF16)
    return _pcall(body, name=name, out_shape=(sds, sds, sds, jax.ShapeDtypeStruct((m, 3 * D), BF16)), grid=(m // tm,),
                  in_specs=[pl.BlockSpec((tm, k), lambda i: (i, 0)), pl.BlockSpec((D, k), lambda i: (0, 0)),
                            stacked, stacked],
                  out_specs=(row, row, row, pl.BlockSpec((tm, 3 * D), lambda i: (i, 0))),
                  sem=("parallel",))(dmixed, w_mix, gates, ys)


def _mods_spec():
    return pl.BlockSpec((1, 1, NMOD * D), lambda t: (_seg(t), 0, 0))


def _rows(width):
    return pl.BlockSpec((TM, width), lambda t: (t, 0))


def norm_mod_fwd(x, mods, k_sh, k_sc, name):
    def body(x_ref, m_ref, h_ref):
        x = x_ref[...]
        r = lax.rsqrt(jnp.mean(x * x, axis=-1, keepdims=True) + EPS)
        sh = m_ref[0, :, k_sh * D:(k_sh + 1) * D]
        sc = m_ref[0, :, k_sc * D:(k_sc + 1) * D]
        h_ref[...] = (x * r * (1.0 + sc) + sh).astype(BF16)

    return _pcall(body, name=name, out_shape=jax.ShapeDtypeStruct((NROW, D), BF16), grid=(NT,),
                  in_specs=[_rows(D), _mods_spec()], out_specs=_rows(D), sem=("parallel",))(x, mods)


def _accumulate_slot(t, ref, part):
    first = (t % TPE) <= 1

    @pl.when(first)
    def _():
        ref[0] = part

    @pl.when(jnp.logical_not(first))
    def _():
        ref[0] += part


def norm_mod_bwd(x, mods, dh, dres, k_sc, name):
    def body(x_ref, m_ref, dh_ref, dres_ref, dx_ref, dp_ref):
        t = pl.program_id(0)
        x = x_ref[...]
        r = lax.rsqrt(jnp.mean(x * x, axis=-1, keepdims=True) + EPS)
        xn = x * r
        sc = m_ref[0, :, k_sc * D:(k_sc + 1) * D]
        dh = dh_ref[...]
        dxn = dh * (1.0 + sc)
        dx_ref[...] = r * (dxn - xn * jnp.mean(dxn * xn, axis=-1, keepdims=True)) + dres_ref[...]
        part = jnp.concatenate([jnp.sum(dh, axis=0, keepdims=True), jnp.sum(dh * xn, axis=0, keepdims=True)], axis=1)
        _accumulate_slot(t, dp_ref, part)

    return _pcall(body, name=name,
                  out_shape=(jax.ShapeDtypeStruct((NROW, D), F32), jax.ShapeDtypeStruct((2 * BL, 1, 2 * D), F32)),
                  grid=(NT,), in_specs=[_rows(D), _mods_spec(), _rows(D), _rows(D)],
                  out_specs=(_rows(D), pl.BlockSpec((1, 1, 2 * D), lambda t: (_slot(t), 0, 0))),
                  sem=("arbitrary",))(x, mods, dh, dres)


def gate_resid_fwd(x, y, mods, k_g, name):
    def body(x_ref, y_ref, m_ref, o_ref):
        o_ref[...] = x_ref[...] + m_ref[0, :, k_g * D:(k_g + 1) * D] * y_ref[...]

    return _pcall(body, name=name, out_shape=jax.ShapeDtypeStruct((NROW, D), F32), grid=(NT,),
                  in_specs=[_rows(D), _rows(D), _mods_spec()], out_specs=_rows(D), sem=("parallel",))(x, y, mods)


def gate_resid_bwd(dx, y, mods, k_g, name):
    def body(dx_ref, y_ref, m_ref, dy_ref, dp_ref):
        t = pl.program_id(0)
        dx = dx_ref[...]
        dy_ref[...] = (dx * m_ref[0, :, k_g * D:(k_g + 1) * D]).astype(BF16)
        _accumulate_slot(t, dp_ref, jnp.sum(dx * y_ref[...], axis=0, keepdims=True))

    return _pcall(body, name=name,
                  out_shape=(jax.ShapeDtypeStruct((NROW, D), BF16), jax.ShapeDtypeStruct((2 * BL, 1, D), F32)),
                  grid=(NT,), in_specs=[_rows(D), _rows(D), _mods_spec()],
                  out_specs=(_rows(D), pl.BlockSpec((1, 1, D), lambda t: (_slot(t), 0, 0))),
                  sem=("arbitrary",))(dx, y, mods)


def _swap16(y, lo16):
    return jnp.where(lo16, pltpu.roll(y, LANE - 16, 1), pltpu.roll(y, 16, 1))


def _group_mean(v, g_mat):
    return jnp.dot(v, g_mat, precision=HIGHEST, preferred_element_type=F32)


def qkv_fwd(p_main, cos_t, sin_t, g_mat, gq, gk, name):
    def body(p_ref, cos_ref, sin_ref, g_ref, gq_ref, gk_ref, q_ref, k_ref, v_ref):
        cos, sin, g_mat_v = cos_ref[...], sin_ref[...], g_ref[...]
        lo16 = (lax.broadcasted_iota(jnp.int32, (TM, LANE), 1) % 32) < 16

        def block(xb, g):
            r = lax.rsqrt(_group_mean(xb * xb, g_mat_v) + EPS)
            y = xb * r * g
            return y * cos + _swap16(y, lo16) * sin

        for j in range(AW // LANE):
            q_ref[:, j * LANE:(j + 1) * LANE] = (block(p_ref[:, j * LANE:(j + 1) * LANE], gq_ref[...])
                                                 * ATTN_SCALE).astype(BF16)
        lo = lax.broadcasted_iota(jnp.int32, (TM, LANE), 1) < HD
        for src, dst_ref in ((block(p_ref[:, OFF_K:OFF_K + LANE], gk_ref[...]), k_ref), (p_ref[:, OFF_V:OFF_V + LANE], v_ref)):
            swapped = pltpu.roll(src, HD, 1)
            dst_ref[:, 0:LANE] = jnp.where(lo, src, swapped).astype(BF16)
            dst_ref[:, LANE:2 * LANE] = jnp.where(lo, swapped, src).astype(BF16)

    tab = pl.BlockSpec((TM, LANE), lambda t: (t % TPE, 0))
    small = pl.BlockSpec((1, LANE), lambda t: (0, 0))
    return _pcall(body, name=name,
                  out_shape=(jax.ShapeDtypeStruct((NROW, AW), BF16), jax.ShapeDtypeStruct((NROW, 2 * KVW), BF16),
                             jax.ShapeDtypeStruct((NROW, 2 * KVW), BF16)),
                  grid=(NT,),
                  in_specs=[_rows(QKVW), tab, tab, pl.BlockSpec((LANE, LANE), lambda t: (0, 0)), small, small],
                  out_specs=(_rows(AW), _rows(2 * KVW), _rows(2 * KVW)),
                  sem=("parallel",))(p_main, cos_t, sin_t, g_mat, gq, gk)


def qkv_bwd(p_main, cos_t, sin_t, g_mat, gq, gk, dq, dk, dv, name, comm=None):
    def body(p_ref, cos_ref, sin_ref, g_ref, gq_ref, gk_ref, dq_ref, dk_ref, dv_ref, dp_ref, dg_ref):
        t = pl.program_id(0)
        cos, sin, g_mat_v = cos_ref[...], sin_ref[...], g_ref[...]
        lo16 = (lax.broadcasted_iota(jnp.int32, (TM, LANE), 1) % 32) < 16

        def block(xb, g, dyr):
            r = lax.rsqrt(_group_mean(xb * xb, g_mat_v) + EPS)
            xn = xb * r
            dy = dyr * cos + _swap16(dyr * sin, lo16)
            dgl = jnp.sum(dy * xn, axis=0, keepdims=True)
            dxn = dy * g
            return r * (dxn - xn * _group_mean(dxn * xn, g_mat_v)), dgl

        parts = []
        for j in range(AW // LANE):
            sl = slice(j * LANE, (j + 1) * LANE)
            dxb, dgl = block(p_ref[:, sl], gq_ref[...], dq_ref[:, sl] * ATTN_SCALE)
            dp_ref[:, sl] = dxb.astype(BF16)
            parts.append(dgl)
        lo = lax.broadcasted_iota(jnp.int32, (TM, LANE), 1) < HD

        def fold(d_ref):
            d0, d1 = d_ref[:, 0:LANE], d_ref[:, LANE:2 * LANE]
            return jnp.where(lo, d0 + pltpu.roll(d0, HD, 1), d1 + pltpu.roll(d1, HD, 1))

        dxb, dgl = block(p_ref[:, OFF_K:OFF_K + LANE], gk_ref[...], fold(dk_ref))
        dp_ref[:, OFF_K:OFF_K + LANE] = dxb.astype(BF16)
        parts.append(dgl)
        parts.append(jnp.zeros((1, LANE), F32))
        dp_ref[:, OFF_V:OFF_V + LANE] = fold(dv_ref).astype(BF16)
        part = jnp.concatenate(parts, axis=1)

        @pl.when(t == 0)
        def _():
            dg_ref[...] = part

        @pl.when(t != 0)
        def _():
            dg_ref[...] += part

    tab = pl.BlockSpec((TM, LANE), lambda t: (t % TPE, 0))
    small = pl.BlockSpec((1, LANE), lambda t: (0, 0))
    return _pcall(body, name=name,
                  out_shape=(jax.ShapeDtypeStruct((NROW, QKVW), BF16), jax.ShapeDtypeStruct((1, QKVW), F32)),
                  grid=(NT,),
                  in_specs=[_rows(QKVW), tab, tab, pl.BlockSpec((LANE, LANE), lambda t: (0, 0)), small, small,
                            _rows(AW), _rows(2 * KVW), _rows(2 * KVW)],
                  out_specs=(_rows(QKVW), pl.BlockSpec((1, QKVW), lambda t: (0, 0))),
                  sem=("arbitrary",), comm=comm)(p_main, cos_t, sin_t, g_mat, gq, gk, dq, dk, dv)


def _layer_norm_parts(yc):
    mu = jnp.mean(yc, axis=-1, keepdims=True)
    xc = yc - mu
    rs = lax.rsqrt(jnp.mean(xc * xc, axis=-1, keepdims=True) + EPS)
    return xc * rs, rs


def ln_silu_fwd(yc, g, b, name):
    def body(y_ref, g_ref, b_ref, o_ref):
        nrm, _ = _layer_norm_parts(y_ref[...])
        ln = nrm * g_ref[...] + b_ref[...]
        o_ref[...] = (ln * _sigmoid(ln)).astype(BF16)

    vec = pl.BlockSpec((1, CW), lambda t: (0, 0))
    return _pcall(body, name=name, out_shape=jax.ShapeDtypeStruct((NROW, CW), BF16), grid=(NT,),
                  in_specs=[_rows(CW), vec, vec], out_specs=_rows(CW), sem=("parallel",))(yc, g, b)


def ln_silu_bwd(yc, g, b, dhs, name):
    def body(y_ref, g_ref, b_ref, dh_ref, dy_ref, dg_ref, db_ref):
        t = pl.program_id(0)
        nrm, rs = _layer_norm_parts(y_ref[...])
        ln = nrm * g_ref[...] + b_ref[...]
        sg = _sigmoid(ln)
        dln = dh_ref[...] * (sg * (1.0 + ln * (1.0 - sg)))
        dn = dln * g_ref[...]
        dy_ref[...] = rs * (dn - jnp.mean(dn, axis=-1, keepdims=True)
                            - nrm * jnp.mean(dn * nrm, axis=-1, keepdims=True))
        pg = jnp.sum(dln * nrm, axis=0, keepdims=True)
        pb = jnp.sum(dln, axis=0, keepdims=True)

        @pl.when(t == 0)
        def _():
            dg_ref[...] = pg
            db_ref[...] = pb

        @pl.when(t != 0)
        def _():
            dg_ref[...] += pg
            db_ref[...] += pb

    vec = pl.BlockSpec((1, CW), lambda t: (0, 0))
    return _pcall(body, name=name,
                  out_shape=(jax.ShapeDtypeStruct((NROW, CW), F32), jax.ShapeDtypeStruct((1, CW), F32),
                             jax.ShapeDtypeStruct((1, CW), F32)),
                  grid=(NT,), in_specs=[_rows(CW), vec, vec, _rows(CW)], out_specs=(_rows(CW), vec, vec),
                  sem=("arbitrary",))(yc, g, b, dhs)


def loss_fwd_bwd(y, target, name):
    def body(y_ref, t_ref, dy_ref, l_ref):
        t = pl.program_id(0)
        latent = (t % TPE) != 0
        err = jnp.where(latent, y_ref[...] - t_ref[...], 0.0)
        dy_ref[...] = err * (1.0 / D)
        part = jnp.sum(err * err, axis=0, keepdims=True)

        @pl.when(t == 0)
        def _():
            l_ref[...] = part

        @pl.when(t != 0)
        def _():
            l_ref[...] += part

    tgt = pl.BlockSpec((TM, D), lambda t: ((t // TPE) * (TPE - 1) + jnp.maximum(t % TPE - 1, 0), 0))
    return _pcall(body, name=name,
                  out_shape=(jax.ShapeDtypeStruct((NROW, D), F32), jax.ShapeDtypeStruct((1, D), F32)),
                  grid=(NT,), in_specs=[_rows(D), tgt], out_specs=(_rows(D), pl.BlockSpec((1, D), lambda t: (0, 0))),
                  sem=("arbitrary",))(y, target)


QB_PER_KV = AW // LANE // NKV


def _softmax_parts(qm, k):
    s = lax.dot_general(qm, k, (((1,), (1,)), ((), ())), preferred_element_type=F32)
    e = jnp.exp(s - jnp.max(s, axis=-1, keepdims=True))
    return e, 1.0 / jnp.sum(e, axis=-1, keepdims=True)


def _attn_specs():
    qs = pl.BlockSpec((TM, LANE), lambda b, h, t, j: (b * TPE + t, h * QB_PER_KV + j))
    ks = pl.BlockSpec((RE, LANE), lambda b, h, t, j: (b, h))
    return qs, ks


def _lane_halves():
    lo = lax.broadcasted_iota(jnp.int32, (TM, LANE), 1) < HD
    return lo, jnp.logical_not(lo)


def attn_fwd(q, k, v, name, comm=None):
    def body(q_ref, k_ref, v_ref, o_ref):
        t = pl.program_id(2)
        halves = _lane_halves()
        qv = q_ref[...]

        def run(nk):
            kv, vv = k_ref[0:nk, :], v_ref[0:nk, :]
            outs = []
            for sel in halves:
                e, rinv = _softmax_parts(jnp.where(sel, qv, jnp.zeros_like(qv)), kv)
                outs.append(jnp.dot(e.astype(BF16), vv, preferred_element_type=F32) * rinv)
            o_ref[...] = jnp.where(halves[0], outs[0], outs[1]).astype(BF16)

        @pl.when(t == 0)
        def _():
            run(CTX)

        @pl.when(t != 0)
        def _():
            run(RE)

    qs, ks = _attn_specs()
    return _pcall(body, name=name, out_shape=jax.ShapeDtypeStruct((NROW, AW), BF16), grid=(BL, NKV, TPE, QB_PER_KV),
                  in_specs=[qs, ks, ks], out_specs=qs, sem=("parallel",) * 4, comm=comm)(q, k, v)


def attn_bwd(q, k, v, do, name, comm=None):
    def body(q_ref, k_ref, v_ref, do_ref, dq_ref, dk_ref, dv_ref):
        t, j = pl.program_id(2), pl.program_id(3)
        halves = _lane_halves()
        qv, dov = q_ref[...], do_ref[...]

        @pl.when(jnp.logical_and(t == 0, j == 0))
        def _():
            dk_ref[...] = jnp.zeros_like(dk_ref)
            dv_ref[...] = jnp.zeros_like(dv_ref)

        def run(nk):
            kv, vv = k_ref[0:nk, :], v_ref[0:nk, :]
            dqs = []
            for sel in halves:
                qm = jnp.where(sel, qv, jnp.zeros_like(qv))
                dom = jnp.where(sel, dov, jnp.zeros_like(dov))
                e, rinv = _softmax_parts(qm, kv)
                p = e * rinv
                dv_ref[0:nk, :] += lax.dot_general(p.astype(BF16), dom, (((0,), (0,)), ((), ())),
                                                   preferred_element_type=F32)
                dp = lax.dot_general(dom, vv, (((1,), (1,)), ((), ())), preferred_element_type=F32)
                ds = (p * (dp - jnp.sum(dp * p, axis=-1, keepdims=True))).astype(BF16)
                dqs.append(jnp.dot(ds, kv, preferred_element_type=F32))
                dk_ref[0:nk, :] += lax.dot_general(ds, qm, (((0,), (0,)), ((), ())), preferred_element_type=F32)
            dq_ref[...] = jnp.where(halves[0], dqs[0], dqs[1])

        @pl.when(t == 0)
        def _():
            run(CTX)

        @pl.when(t != 0)
        def _():
            run(RE)

    qs, ks = _attn_specs()
    return _pcall(body, name=name,
                  out_shape=(jax.ShapeDtypeStruct((NROW, AW), F32), jax.ShapeDtypeStruct((NROW, 2 * KVW), F32),
                             jax.ShapeDtypeStruct((NROW, 2 * KVW), F32)),
                  grid=(BL, NKV, TPE, QB_PER_KV), in_specs=[qs, ks, ks, qs], out_specs=(qs, ks, ks),
                  sem=("parallel", "parallel", "arbitrary", "arbitrary"), comm=comm)(q, k, v, do)


CONV_SEGS = ((0, CTX), (CTX, SEQ))


def _p_block(col0):
    return pl.BlockSpec((RE, LANE), lambda cb, b: (b, col0 // LANE + cb))


def _conv_io(width):
    return pl.BlockSpec((RE, LANE), lambda cb, b: (b, cb))


def _taps(n):
    return pl.BlockSpec((n, LANE), lambda cb, b: (0, cb))


def _fill_pad(pad_ref, length, values):
    pad_ref[0:PADR, :] = jnp.zeros((PADR, LANE), F32)
    pad_ref[PADR + length:2 * PADR + length, :] = jnp.zeros((PADR, LANE), F32)
    pad_ref[PADR:PADR + length, :] = values


def _conv_chunk(pad_ref, w_ref, ntap, c0, first_row):
    acc = jnp.zeros((CONV_CH, LANE), F32)
    for kk in range(ntap):
        r0 = c0 + first_row(kk)
        acc += w_ref[kk:kk + 1, :] * pad_ref[r0:r0 + CONV_CH, :]
    return acc


def conv_fwd(p_main, wdw, bdw, w3, name):
    def body(a_ref, g_ref, bg_ref, cg_ref, xs_ref, w_ref, b_ref, w3_ref, yc_ref, z_ref, pad_ref):
        for off, length in CONV_SEGS:
            rows = slice(off, off + length)
            _fill_pad(pad_ref, length, a_ref[rows, :] * _sigmoid(g_ref[rows, :]))
            for c0 in range(0, length, CONV_CH):
                acc = _conv_chunk(pad_ref, w_ref, CONF_K, c0, lambda kk: PADR + kk - CONF_K // 2)
                yc_ref[off + c0:off + c0 + CONV_CH, :] = acc + b_ref[...]
            pad_ref[PADR:PADR + length, :] = cg_ref[rows, :] * xs_ref[rows, :]
            for c0 in range(0, length, CONV_CH):
                acc = _conv_chunk(pad_ref, w3_ref, SC_K, c0, lambda kk: PADR + kk - SC_K // 2)
                z_ref[off + c0:off + c0 + CONV_CH, :] = (bg_ref[off + c0:off + c0 + CONV_CH, :] * acc).astype(BF16)

    return _pcall(body, name=name,
                  out_shape=(jax.ShapeDtypeStruct((NROW, CW), F32), jax.ShapeDtypeStruct((NROW, CW), BF16)),
                  grid=(CB, BL),
                  in_specs=[_p_block(OFF_CONF), _p_block(OFF_CONF + CW), _p_block(OFF_SC), _p_block(OFF_SC + CW),
                            _p_block(OFF_SC + 2 * CW), _taps(CONF_K), _taps(1), _taps(SC_K)],
                  out_specs=(_conv_io(CW), _conv_io(CW)),
                  scratch=[pltpu.VMEM((SEQ + 2 * PADR, LANE), F32)],
                  sem=("parallel", "parallel"))(p_main, p_main, p_main, p_main, p_main, wdw, bdw, w3)


def _tap_grad(pad_ref, d_ref, off, length, first_row):
    acc = jnp.zeros((8, LANE), F32)
    for c0 in range(0, length, CONV_CH):
        prod = d_ref[off + c0:off + c0 + CONV_CH, :] * pad_ref[c0 + first_row:c0 + first_row + CONV_CH, :]
        acc += jnp.sum(prod.reshape(CONV_CH // 8, 8, LANE), axis=0)
    return jnp.sum(acc, axis=0, keepdims=True)


def conv_bwd(p_main, wdw, w3, dyc, dz, name):
    def body(a_ref, g_ref, bg_ref, cg_ref, xs_ref, w_ref, w3_ref, dyc_ref, dz_ref,
             da_ref, dg_ref, dbg_ref, dcg_ref, dxs_ref, dw_ref, db_ref, dw3_ref, pad_x, pad_d, dconv_ref):
        b = pl.program_id(1)

        @pl.when(b == 0)
        def _():
            dw_ref[...] = jnp.zeros_like(dw_ref)
            db_ref[...] = jnp.zeros_like(db_ref)
            dw3_ref[...] = jnp.zeros_like(dw3_ref)

        db_ref[...] += jnp.sum(dyc_ref[...], axis=0, keepdims=True)
        for off, length in CONV_SEGS:
            rows = slice(off, off + length)
            _fill_pad(pad_x, length, a_ref[rows, :] * _sigmoid(g_ref[rows, :]))
            _fill_pad(pad_d, length, dyc_ref[rows, :])
            for kk in range(CONF_K):
                dw_ref[kk:kk + 1, :] += _tap_grad(pad_x, dyc_ref, off, length, PADR + kk - CONF_K // 2)
            for c0 in range(0, length, CONV_CH):
                dh = _conv_chunk(pad_d, w_ref, CONF_K, c0, lambda kk: PADR + CONF_K // 2 - kk)
                ch = slice(off + c0, off + c0 + CONV_CH)
                sg = _sigmoid(g_ref[ch, :])
                da_ref[ch, :] = (dh * sg).astype(BF16)
                dg_ref[ch, :] = (dh * a_ref[ch, :] * sg * (1.0 - sg)).astype(BF16)
            pad_x[PADR:PADR + length, :] = cg_ref[rows, :] * xs_ref[rows, :]
            dconv_ref[rows, :] = dz_ref[rows, :] * bg_ref[rows, :]
            pad_d[PADR:PADR + length, :] = dconv_ref[rows, :]
            for kk in range(SC_K):
                dw3_ref[kk:kk + 1, :] += _tap_grad(pad_x, dconv_ref, off, length, PADR + kk - SC_K // 2)
            for c0 in range(0, length, CONV_CH):
                ch = slice(off + c0, off + c0 + CONV_CH)
                c3 = _conv_chunk(pad_x, w3_ref, SC_K, c0, lambda kk: PADR + kk - SC_K // 2)
                dbg_ref[ch, :] = (dz_ref[ch, :] * c3).astype(BF16)
                dcx = _conv_chunk(pad_d, w3_ref, SC_K, c0, lambda kk: PADR + SC_K // 2 - kk)
                dcg_ref[ch, :] = (dcx * xs_ref[ch, :]).astype(BF16)
                dxs_ref[ch, :] = (dcx * cg_ref[ch, :]).astype(BF16)

    slab = jax.ShapeDtypeStruct((NROW, CW), BF16)
    return _pcall(body, name=name,
                  out_shape=(slab,) * 5 + (jax.ShapeDtypeStruct((CONF_K, CW), F32), jax.ShapeDtypeStruct((1, CW), F32),
                                           jax.ShapeDtypeStruct((SC_K, CW), F32)),
                  grid=(CB, BL),
                  in_specs=[_p_block(OFF_CONF), _p_block(OFF_CONF + CW), _p_block(OFF_SC), _p_block(OFF_SC + CW),
                            _p_block(OFF_SC + 2 * CW), _taps(CONF_K), _taps(SC_K), _conv_io(CW), _conv_io(CW)],
                  out_specs=(_conv_io(CW),) * 5 + (_taps(CONF_K), _taps(1), _taps(SC_K)),
                  scratch=[pltpu.VMEM((SEQ + 2 * PADR, LANE), F32), pltpu.VMEM((SEQ + 2 * PADR, LANE), F32),
                           pltpu.VMEM((RE, LANE), F32)],
                  sem=("parallel", "arbitrary"))(p_main, p_main, p_main, p_main, p_main, wdw, w3, dyc, dz)


def silu_rows(x, name):
    def body(x_ref, o_ref):
        o_ref[...] = x_ref[...] * _sigmoid(x_ref[...])

    return _pcall(body, name=name, out_shape=jax.ShapeDtypeStruct(x.shape, F32))(x)


def silu_rows_bwd(x, dcs, name):
    def body(x_ref, d_ref, o_ref):
        x = x_ref[...]
        sg = _sigmoid(x)
        tot = d_ref[0]
        for i in range(1, DEPTH):
            tot += d_ref[i]
        o_ref[...] = tot * (sg * (1.0 + x * (1.0 - sg)))

    return _pcall(body, name=name, out_shape=jax.ShapeDtypeStruct(x.shape, F32))(x, dcs)


def dmod_assemble(parts, name):
    def body(p_ref, dm_ref, db_ref):
        row = lax.broadcasted_iota(jnp.int32, (8, NMOD * D), 0)
        dm = jnp.zeros((8, NMOD * D), F32)
        db = jnp.zeros((1, NMOD * D), F32)
        for s in range(2 * BL):
            target = BL if s % 2 == 0 else s // 2
            part = p_ref[s:s + 1, :]
            dm += jnp.where(row == target, part, 0.0)
            db += part
        dm_ref[...] = dm
        db_ref[...] = db

    return _pcall(body, name=name, out_shape=(jax.ShapeDtypeStruct((8, NMOD * D), F32),
                                              jax.ShapeDtypeStruct((1, NMOD * D), F32)))(parts)


def sum_leading(x, name):
    n = x.shape[0]
    tr = _pick(x.shape[1], (256, 32, 8))

    def body(x_ref, o_ref):
        tot = x_ref[0].astype(F32)
        for i in range(1, n):
            tot += x_ref[i].astype(F32)
        o_ref[...] = tot

    return _pcall(body, name=name, out_shape=jax.ShapeDtypeStruct(x.shape[1:], F32), grid=(x.shape[1] // tr,),
                  in_specs=[pl.BlockSpec((n, tr, x.shape[2]), lambda i: (0, i, 0))],
                  out_specs=pl.BlockSpec((tr, x.shape[2]), lambda i: (i, 0)), sem=("parallel",))(x)


SLAB_ROWS = (256, 176, 128, 64, 8)


def _prefetch_call(body, name, out_shape, grid, in_specs, out_specs, sem, scalars, *args):
    spec = pltpu.PrefetchScalarGridSpec(num_scalar_prefetch=len(scalars), grid=grid, in_specs=in_specs,
                                        out_specs=out_specs)
    return pl.pallas_call(body, name=name, out_shape=out_shape, grid_spec=spec,
                          compiler_params=pltpu.CompilerParams(dimension_semantics=sem,
                                                               vmem_limit_bytes=VMEM_LIMIT))(*scalars, *args)


def cast_layers(w, chip, name):
    depth, r, c = w.shape
    tr = _pick(r, SLAB_ROWS)

    def body(s_ref, w_ref, *o_refs):
        for l in range(depth):
            o_refs[l][...] = w_ref[l].astype(BF16)

    slab = pl.BlockSpec((None, tr, c), lambda i, s: (s[0], i, 0))
    return _prefetch_call(body, name, (jax.ShapeDtypeStruct((NCHIP, r, c), BF16),) * depth, (r // tr,),
                          [pl.BlockSpec((depth, tr, c), lambda i, s: (0, i, 0))], (slab,) * depth,
                          ("parallel",), (chip,), w)


def rs_add(g, other, core, chip, name):
    _, r, c = g.shape
    rh = r // 2
    tr = _pick(rh, SLAB_ROWS)
    nblk = rh // tr

    def body(core_ref, chip_ref, g_ref, o_ref, send_ref, arr_ref):
        k = pl.program_id(1)
        tot = (g_ref[...].astype(F32) + o_ref[...].astype(F32)).astype(BF16)
        send_ref[...] = tot

        @pl.when(k == chip_ref[0])
        def _():
            arr_ref[...] = tot

    blk = (None, tr, c)
    return _prefetch_call(
        body, name, (jax.ShapeDtypeStruct(other.shape, BF16), jax.ShapeDtypeStruct(g.shape, BF16)), (nblk, NCHIP),
        [pl.BlockSpec(blk, lambda i, k, cr, ch: (k, cr[0] * nblk + i, 0)), pl.BlockSpec(blk, lambda i, k, cr, ch: (k, i, 0))],
        (pl.BlockSpec(blk, lambda i, k, cr, ch: (k, i, 0)),
         pl.BlockSpec(blk, lambda i, k, cr, ch: (ch[0], cr[0] * nblk + i, 0))),
        ("parallel", "arbitrary"), (core, chip), g, other)


def adamw_layer(w, arr, m, v, layer, prev, name):
    depth, r, c = w.shape
    tr = _pick(r, SLAB_ROWS)
    c1 = 1.0 / (1.0 - ADAM_B1 ** ADAM_STEP)
    c2 = 1.0 / (1.0 - ADAM_B2 ** ADAM_STEP)

    def body(w_ref, a_ref, m_ref, v_ref, p0, p1, p2, p3, g_ref, d_ref, mo_ref, vo_ref):
        gv = a_ref[0].astype(F32)
        for k in range(1, NCHIP):
            gv += a_ref[k].astype(F32)
        mn = ADAM_B1 * m_ref[...] + (1.0 - ADAM_B1) * gv
        vn = ADAM_B2 * v_ref[...] + (1.0 - ADAM_B2) * (gv * gv)
        g_ref[...] = gv
        d_ref[...] = -ADAM_LR * ((mn * c1) / (jnp.sqrt(vn * c2) + ADAM_EPS) + ADAM_WD * w_ref[...])
        mo_ref[...] = mn
        vo_ref[...] = vn

    spec = pl.BlockSpec((None, tr, c), lambda i: (layer, i, 0))
    sds = jax.ShapeDtypeStruct(w.shape, F32)
    return pl.pallas_call(body, name=name, out_shape=(sds,) * 4, grid=(r // tr,),
                          in_specs=[spec, pl.BlockSpec((NCHIP, tr, c), lambda i: (0, i, 0)), spec, spec] + [ANY] * 4,
                          out_specs=(spec,) * 4, input_output_aliases={4: 0, 5: 1, 6: 2, 7: 3},
                          compiler_params=pltpu.CompilerParams(dimension_semantics=("parallel",),
                                                               vmem_limit_bytes=VMEM_LIMIT))(w, arr, m, v, *prev)


def adamw(w, g, m, v, name):
    rows, cols = w.shape
    tr = _pick(rows, (256, 248, 128, 8))
    c1 = 1.0 / (1.0 - ADAM_B1 ** ADAM_STEP)
    c2 = 1.0 / (1.0 - ADAM_B2 ** ADAM_STEP)

    def body(w_ref, g_ref, m_ref, v_ref, d_ref, mo_ref, vo_ref):
        gv = g_ref[...]
        mn = ADAM_B1 * m_ref[...] + (1.0 - ADAM_B1) * gv
        vn = ADAM_B2 * v_ref[...] + (1.0 - ADAM_B2) * (gv * gv)
        d_ref[...] = -ADAM_LR * ((mn * c1) / (jnp.sqrt(vn * c2) + ADAM_EPS) + ADAM_WD * w_ref[...])
        mo_ref[...] = mn
        vo_ref[...] = vn

    spec = pl.BlockSpec((tr, cols), lambda i: (i, 0))
    sds = jax.ShapeDtypeStruct((rows, cols), F32)
    return _pcall(body, name=name, out_shape=(sds, sds, sds), grid=(rows // tr,), in_specs=[spec] * 4,
                  out_specs=(spec, spec, spec), sem=("parallel",))(w, g, m, v)


def _place():
    return lax.axis_index("x"), lax.axis_index("y"), lax.axis_index("c")


def _other_chips(x, y):
    return [(1 - x, y), (x, 1 - y), (1 - x, 1 - y)]


def _comm_call(body, name, out_shape, n_in, nsem):
    return pl.pallas_call(body, name=name, out_shape=out_shape, in_specs=[ANY] * n_in,
                          out_specs=jax.tree.map(lambda _: ANY, out_shape),
                          scratch_shapes=[pltpu.SemaphoreType.DMA((nsem,)), pltpu.SemaphoreType.DMA((nsem,)),
                                          pltpu.SemaphoreType.DMA])


def all_gather8(block, name):
    def body(x_ref, out_ref, send_sems, recv_sems, local_sem):
        x, y, c = _place()
        me, sibling = (x, y, c), (x, y, 1 - c)
        chips = _other_chips(x, y)

        def slot(px, py, pc):
            return out_ref.at[4 * px + 2 * py + pc]

        def copy(k, blk, to, src=None):
            return pltpu.make_async_remote_copy(src_ref=slot(*blk) if src is None else src, dst_ref=slot(*blk),
                                                send_sem=send_sems.at[k], recv_sem=recv_sems.at[k],
                                                device_id=to, device_id_type=MESH)

        mine = pltpu.make_async_copy(x_ref, slot(*me), local_sem)
        mine.start()
        first = [copy(0, me, sibling, src=x_ref)]
        first += [copy(1 + j, me, (*chip, c), src=x_ref) for j, chip in enumerate(chips)]
        for cp in first:
            cp.start()
        passed = [copy(4 + j, (*chip, c), sibling) for j, chip in enumerate(chips)]
        for j, chip in enumerate(chips):
            copy(1 + j, (*chip, c), me).wait_recv()
            passed[j].start()
        copy(0, sibling, me).wait_recv()
        for j, chip in enumerate(chips):
            copy(4 + j, (*chip, 1 - c), me).wait_recv()
        for cp in first + passed:
            cp.wait_send()
        mine.wait()

    return _comm_call(body, name, jax.ShapeDtypeStruct((8,) + block.shape, block.dtype), 1, 7)(block)


def _remote(src, dst, send_sems, recv_sems, k, to):
    return pltpu.make_async_remote_copy(src_ref=src, dst_ref=dst, send_sem=send_sems.at[k], recv_sem=recv_sems.at[k],
                                        device_id=to, device_id_type=MESH)


def _half(ref, slot, core):
    rh = ref.shape[1] // 2
    return ref.at[slot, pl.ds(core * rh, rh)]


def _all_slots_half(ref, core):
    rh = ref.shape[1] // 2
    return ref.at[:, pl.ds(core * rh, rh)]


def gather_ici(bufs):
    def program(ro, rw, new, ss, rs):
        x, y, c = _place()
        own = 2 * x + y
        starts, arrivals = [], []
        for w, ref in enumerate(rw):
            for j, chip in enumerate(_other_chips(x, y)):
                starts.append(_remote(_half(ref, own, c), _half(ref, own, c), ss, rs, 3 * w + j, (*chip, c)))
                arrivals.append(_remote(_half(ref, own, c), _half(ref, 2 * chip[0] + chip[1], c), ss, rs, 3 * w + j,
                                        (*chip, c)))
        return starts, arrivals

    return CommSpec((), tuple(bufs), (), 3 * len(bufs), program)


def gather_d2d(bufs):
    def program(ro, rw, new, ss, rs):
        x, y, c = _place()
        starts, arrivals = [], []
        for w, ref in enumerate(rw):
            for j, chip in enumerate(_other_chips(x, y)):
                slot = 2 * chip[0] + chip[1]
                starts.append(_remote(_half(ref, slot, c), _half(ref, slot, c), ss, rs, 3 * w + j, (x, y, 1 - c)))
                arrivals.append(_remote(_half(ref, slot, c), _half(ref, slot, 1 - c), ss, rs, 3 * w + j, (x, y, 1 - c)))
        return starts, arrivals

    return CommSpec((), tuple(bufs), (), 3 * len(bufs), program)


def rs_swap(grads):
    def program(ro, rw, new, ss, rs):
        x, y, c = _place()
        copies = [_remote(_all_slots_half(g, 1 - c), new[w], ss, rs, w, (x, y, 1 - c)) for w, g in enumerate(ro)]
        return copies, copies

    shapes = tuple(jax.ShapeDtypeStruct((NCHIP, g.shape[1] // 2, g.shape[2]), g.dtype) for g in grads)
    return CommSpec(tuple(grads), (), shapes, len(grads), program)


def rs_ici(sends, arrs):
    def program(ro, rw, new, ss, rs):
        x, y, c = _place()
        own = 2 * x + y
        starts, arrivals = [], []
        for w, (snd, arr) in enumerate(zip(ro, rw)):
            for j, chip in enumerate(_other_chips(x, y)):
                slot = 2 * chip[0] + chip[1]
                starts.append(_remote(snd.at[slot], _half(arr, own, c), ss, rs, 3 * w + j, (*chip, c)))
                arrivals.append(_remote(snd.at[slot], _half(arr, slot, c), ss, rs, 3 * w + j, (*chip, c)))
        return starts, arrivals

    return CommSpec(tuple(sends), tuple(arrs), (), 3 * len(sends), program)


def rs_d2d(arrs):
    def program(ro, rw, new, ss, rs):
        x, y, c = _place()
        starts = [_remote(_all_slots_half(a, c), _all_slots_half(a, c), ss, rs, w, (x, y, 1 - c)) for w, a in enumerate(rw)]
        arrivals = [_remote(_all_slots_half(a, c), _all_slots_half(a, 1 - c), ss, rs, w, (x, y, 1 - c))
                    for w, a in enumerate(rw)]
        return starts, arrivals

    return CommSpec((), tuple(arrs), (), len(arrs), program)


PACK_COLS = 1024
MATMUL_W = ("w_ada", "w_in", "w_attn_o", "w_conf_out", "w_sc_out", "w_mix_out", "w_ffn_in", "w_ffn_out")
ROW_SPLIT = ("w_mix_out", "w_ffn_out")
CONV_W = ("conf_dw_w", "sc_dw_w")
SMALL = ("c_ctx", "b_ada", "q_norm", "k_norm", "conf_dw_b", "conf_ln_g", "conf_ln_b", "conf_dw_w", "sc_dw_w")


def _pack_rows(arrays, row_multiple):
    flat = jnp.concatenate([a.reshape(-1) for a in arrays])
    rows = -(-flat.shape[0] // PACK_COLS)
    rows = -(-rows // row_multiple) * row_multiple
    flat = jnp.pad(flat, (0, rows * PACK_COLS - flat.shape[0]))
    return flat.reshape(rows, PACK_COLS)


def _unpack(flat2d, shapes):
    flat = flat2d.reshape(-1)
    out, pos = [], 0
    for shp in shapes:
        n = 1
        for s in shp:
            n *= s
        out.append(flat[pos:pos + n].reshape(shp))
        pos += n
    return out


def _cols_joined(stacked_layer):
    nchip, r, c = stacked_layer.shape
    return jnp.transpose(stacked_layer, (1, 0, 2)).reshape(r, nchip * c)


def _cols_split(full):
    r, cols = full.shape
    return jnp.transpose(full.reshape(r, NCHIP, cols // NCHIP), (1, 0, 2))


def _rope_tables():
    rows = SEQ // GRID_W
    r_ids = jnp.repeat(jnp.arange(rows, dtype=F32), GRID_W)
    c_ids = jnp.tile(jnp.arange(GRID_W, dtype=F32), rows)
    freqs = ROPE_THETA ** (-jnp.arange(0, HD // 2, 2, dtype=F32) / (HD // 2))
    ang_r, ang_c = r_ids[:, None] * freqs, c_ids[:, None] * freqs
    cos_h = jnp.concatenate([jnp.cos(ang_r), jnp.cos(ang_r), jnp.cos(ang_c), jnp.cos(ang_c)], axis=1)
    sin_h = jnp.concatenate([-jnp.sin(ang_r), jnp.sin(ang_r), -jnp.sin(ang_c), jnp.sin(ang_c)], axis=1)
    cos_t = jnp.concatenate([jnp.ones((CTX, HD), F32), cos_h], axis=0)
    sin_t = jnp.concatenate([jnp.zeros((CTX, HD), F32), sin_h], axis=0)
    return jnp.tile(cos_t, (1, LANE // HD)), jnp.tile(sin_t, (1, LANE // HD))


def _group_matrix():
    gid = jnp.arange(LANE) // HD
    return jnp.where(gid[:, None] == gid[None, :], 1.0 / HD, 0.0).astype(F32)


def _layer_weights(bufs, small, i):
    b = dict(zip(MATMUL_W, bufs))
    wi = _cols_joined(b["w_in"])

    def rows_joined(a):
        return a.reshape(a.shape[0] * a.shape[1], a.shape[2])

    return dict(
        w_ada=(b["w_ada"], "cols"), wi_main=(wi[:, :OFF_GATE], "mat"), wi_gate=(wi[:, OFF_GATE:], "mat"),
        w_attn_o=(_cols_joined(b["w_attn_o"]), "mat"), w_conf_out=(_cols_joined(b["w_conf_out"]), "mat"),
        w_sc_out=(_cols_joined(b["w_sc_out"]), "mat"), w_ffn_in=(b["w_ffn_in"], "cols"),
        w_mix_out=(rows_joined(b["w_mix_out"]), "mat"), w_ffn_out=(rows_joined(b["w_ffn_out"]), "mat"),
        conf_dw_w=small["conf_dw_w"][i], sc_dw_w=small["sc_dw_w"][i], conf_dw_b=small["conf_dw_b"][i][None],
        conf_ln_g=small["conf_ln_g"][i][None], conf_ln_b=small["conf_ln_b"][i][None],
        gq=jnp.tile(small["q_norm"][i], LANE // HD)[None], gk=jnp.tile(small["k_norm"][i], LANE // HD)[None])


def _layer_fwd(i, xs, mods, w, tabs, next_bufs):
    cos_t, sin_t, g_mat = tabs
    n = f"l{i}_"
    sv = {"x_in": xs, "mods": mods}
    sv["h"] = norm_mod_fwd(xs, mods, 0, 1, n + "norm1")
    sv["p_main"] = mm_nn(sv["h"], w["wi_main"], name=n + "p_main")
    sv["q"], sv["k"], sv["v"] = qkv_fwd(sv["p_main"], cos_t, sin_t, g_mat, w["gq"], w["gk"], n + "qkv")
    if next_bufs is None:
        sv["o"] = attn_fwd(sv["q"], sv["k"], sv["v"], n + "attn")
    else:
        sv["o"], next_bufs, _ = attn_fwd(sv["q"], sv["k"], sv["v"], n + "attn", comm=gather_ici(next_bufs))
    sv["yc"], sv["z"] = conv_fwd(sv["p_main"], w["conf_dw_w"], w["conf_dw_b"], w["sc_dw_w"], n + "conv")
    sv["hs"] = ln_silu_fwd(sv["yc"], w["conf_ln_g"], w["conf_ln_b"], n + "ln_silu")
    sv["merged"], sv["gates"], sv["ys"] = gate_mm_fwd(sv["h"], w["wi_gate"][0], sv["o"], sv["hs"], sv["z"],
                                                      w["w_attn_o"][0], w["w_conf_out"][0], w["w_sc_out"][0],
                                                      n + "gate_merge")
    sv["mixed"] = mm_nn(sv["merged"], w["w_mix_out"], name=n + "mix")
    sv["x1"] = gate_resid_fwd(xs, sv["mixed"], mods, 2, n + "resid1")
    sv["h2"] = norm_mod_fwd(sv["x1"], mods, 3, 4, n + "norm2")
    if next_bufs is None:
        sv["f"], sv["u2"] = ffn_in_swiglu(sv["h2"], w["w_ffn_in"][0], n + "ffn_in")
    else:
        (sv["f"], sv["u2"]), next_bufs, _ = ffn_in_swiglu(sv["h2"], w["w_ffn_in"][0], n + "ffn_in",
                                                          comm=gather_d2d(next_bufs))
    sv["of"] = mm_nn(sv["f"], w["w_ffn_out"], name=n + "ffn_out")
    x2 = gate_resid_fwd(sv["x1"], sv["of"], mods, 5, n + "resid2")
    return x2, sv, next_bufs


def _layer_bwd(i, dx2, sv, w, tabs, cs, pending, ids):
    cos_t, sin_t, g_mat = tabs
    n = f"l{i}b_"
    mods = sv["mods"]
    g = {}
    dof, dm5 = gate_resid_bwd(dx2, sv["of"], mods, 5, n + "resid2")
    if pending is None:
        du = d_f_swiglu(dof, w["w_ffn_out"][0], sv["u2"], n + "d_f")
    else:
        du, _, swapped = d_f_swiglu(dof, w["w_ffn_out"][0], sv["u2"], n + "d_f", comm=rs_swap(pending))
        sends, arrs = zip(*[rs_add(g_, s_, ids[0], ids[1], f"{n}rs_add_{k}")
                            for k, g_, s_ in zip(MATMUL_W, pending, swapped)])
    g["w_ffn_out"] = mm_tn(sv["f"], dof, out_dtype=BF16, name=n + "dw_ffn_out").reshape(NCHIP, FH // NCHIP, D)
    dh2 = mm_nt(du, w["w_ffn_in"], name=n + "d_h2")
    g["w_ffn_in"] = mm_tn(sv["h2"], du, cols=True, out_dtype=BF16, name=n + "dw_ffn_in")
    dx1, dm34 = norm_mod_bwd(sv["x1"], mods, dh2, dx2, 4, n + "norm2")
    dmixed, dm2 = gate_resid_bwd(dx1, sv["mixed"], mods, 2, n + "resid1")
    dya, dyb, dys, dp_gate = d_merged_gate(dmixed, w["w_mix_out"][0], sv["gates"], sv["ys"], n + "d_merged")
    g["w_mix_out"] = mm_tn(sv["merged"], dmixed, out_dtype=BF16, name=n + "dw_mix").reshape(NCHIP, D // NCHIP, D)
    do = mm_nt(dya, w["w_attn_o"], out_dtype=BF16, name=n + "d_o")
    g["w_attn_o"] = _cols_split(mm_tn(sv["o"], dya, out_dtype=BF16, name=n + "dw_attn_o"))
    dhs = mm_nt(dyb, w["w_conf_out"], name=n + "d_hs")
    g["w_conf_out"] = _cols_split(mm_tn(sv["hs"], dyb, out_dtype=BF16, name=n + "dw_conf_out"))
    dz = mm_nt(dys, w["w_sc_out"], name=n + "d_z")
    g["w_sc_out"] = _cols_split(mm_tn(sv["z"], dys, out_dtype=BF16, name=n + "dw_sc_out"))
    dyc, g["conf_ln_g"], g["conf_ln_b"] = ln_silu_bwd(sv["yc"], w["conf_ln_g"], w["conf_ln_b"], dhs, n + "ln_silu")
    da, dg, dbg, dcg, dxs, g["conf_dw_w"], g["conf_dw_b"], g["sc_dw_w"] = conv_bwd(
        sv["p_main"], w["conf_dw_w"], w["sc_dw_w"], dyc, dz, n + "conv")
    done = None
    if pending is None:
        dq, dk, dv = attn_bwd(sv["q"], sv["k"], sv["v"], do, n + "attn")
        dp_qkv, dgqk = qkv_bwd(sv["p_main"], cos_t, sin_t, g_mat, w["gq"], w["gk"], dq, dk, dv, n + "qkv")
    else:
        (dq, dk, dv), arrs, _ = attn_bwd(sv["q"], sv["k"], sv["v"], do, n + "attn", comm=rs_ici(sends, arrs))
        (dp_qkv, dgqk), done, _ = qkv_bwd(sv["p_main"], cos_t, sin_t, g_mat, w["gq"], w["gk"], dq, dk, dv, n + "qkv",
                                          comm=rs_d2d(arrs))
    dp_main = jnp.concatenate([dp_qkv, da, dg, dbg, dcg, dxs], axis=1)
    dh = mm_nt(dp_main, w["wi_main"], name=n + "d_h_main")
    dh = mm_nt(dp_gate, w["wi_gate"], acc=dh, name=n + "d_h_gate")
    g["w_in"] = _cols_split(jnp.concatenate([mm_tn(sv["h"], dp_main, out_dtype=BF16, name=n + "dw_in_main"),
                                             mm_tn(sv["h"], dp_gate, out_dtype=BF16, name=n + "dw_in_gate")], axis=1))
    dx_in, dm01 = norm_mod_bwd(sv["x_in"], mods, dh, dx1, 1, n + "norm1")
    parts = jnp.concatenate([dm01, dm2, dm34, dm5], axis=2).reshape(2 * BL, NMOD * D)
    dmod, g["b_ada"] = dmod_assemble(parts, n + "dmod")
    g["w_ada"] = mm_tn(cs, dmod, cols=True, out_dtype=BF16, name=n + "dw_ada")
    g["dcs"] = mm_nt(dmod, w["w_ada"], name=n + "d_cs")
    g["q_norm"] = dgqk[0, :AW].reshape(NQ, HD).sum(axis=0)
    g["k_norm"] = dgqk[0, OFF_K:OFF_K + KVW].reshape(NKV, HD).sum(axis=0)
    return dx_in, g, done


def kernel(x, c, ctx, c_ctx, w_ada, b_ada, w_in, q_norm, k_norm, w_attn_o, conf_dw_w, conf_dw_b, conf_ln_g, conf_ln_b, w_conf_out, sc_dw_w, w_sc_out, w_mix_out, w_ffn_in, w_ffn_out, loss_target, m_c_ctx, m_w_ada, m_b_ada, m_w_in, m_q_norm, m_k_norm, m_w_attn_o, m_conf_dw_w, m_conf_dw_b, m_conf_ln_g, m_conf_ln_b, m_w_conf_out, m_sc_dw_w, m_w_sc_out, m_w_mix_out, m_w_ffn_in, m_w_ffn_out, v_c_ctx, v_w_ada, v_b_ada, v_w_in, v_q_norm, v_k_norm, v_w_attn_o, v_conf_dw_w, v_conf_dw_b, v_conf_ln_g, v_conf_ln_b, v_w_conf_out, v_sc_dw_w, v_w_sc_out, v_w_mix_out, v_w_ffn_in, v_w_ffn_out):
    local = dict(c_ctx=c_ctx, w_ada=w_ada, b_ada=b_ada, w_in=w_in, q_norm=q_norm, k_norm=k_norm, w_attn_o=w_attn_o,
                 conf_dw_w=conf_dw_w, conf_dw_b=conf_dw_b, conf_ln_g=conf_ln_g, conf_ln_b=conf_ln_b,
                 w_conf_out=w_conf_out, sc_dw_w=sc_dw_w, w_sc_out=w_sc_out, w_mix_out=w_mix_out, w_ffn_in=w_ffn_in,
                 w_ffn_out=w_ffn_out)
    mom_m = dict(c_ctx=m_c_ctx, w_ada=m_w_ada, b_ada=m_b_ada, w_in=m_w_in, q_norm=m_q_norm, k_norm=m_k_norm,
                 w_attn_o=m_w_attn_o, conf_dw_w=m_conf_dw_w, conf_dw_b=m_conf_dw_b, conf_ln_g=m_conf_ln_g,
                 conf_ln_b=m_conf_ln_b, w_conf_out=m_w_conf_out, sc_dw_w=m_sc_dw_w, w_sc_out=m_w_sc_out,
                 w_mix_out=m_w_mix_out, w_ffn_in=m_w_ffn_in, w_ffn_out=m_w_ffn_out)
    mom_v = dict(c_ctx=v_c_ctx, w_ada=v_w_ada, b_ada=v_b_ada, w_in=v_w_in, q_norm=v_q_norm, k_norm=v_k_norm,
                 w_attn_o=v_w_attn_o, conf_dw_w=v_conf_dw_w, conf_dw_b=v_conf_dw_b, conf_ln_g=v_conf_ln_g,
                 conf_ln_b=v_conf_ln_b, w_conf_out=v_w_conf_out, sc_dw_w=v_sc_dw_w, w_sc_out=v_w_sc_out,
                 w_mix_out=v_w_mix_out, w_ffn_in=v_w_ffn_in, w_ffn_out=v_w_ffn_out)
    order = ("c_ctx", "w_ada", "b_ada", "w_in", "q_norm", "k_norm", "w_attn_o", "conf_dw_w", "conf_dw_b", "conf_ln_g",
             "conf_ln_b", "w_conf_out", "sc_dw_w", "w_sc_out", "w_mix_out", "w_ffn_in", "w_ffn_out")
    core = lax.axis_index("c").astype(jnp.int32)
    chip = (2 * lax.axis_index("x") + lax.axis_index("y")).astype(jnp.int32)

    own = [cast_layers(local[k], chip.reshape(1), "cast_" + k) for k in MATMUL_W]
    layer_bufs = [[own[w][l] for w in range(len(MATMUL_W))] for l in range(DEPTH)]
    conv_shapes = [local[k].shape for k in CONV_W]
    conv_all = all_gather8(_pack_rows([local[k] for k in CONV_W], 8), "gather_conv_taps")
    per_chip = [_unpack(conv_all[2 * s], conv_shapes) for s in range(NCHIP)]
    small = dict(b_ada=b_ada, q_norm=q_norm, k_norm=k_norm, conf_dw_b=conf_dw_b, conf_ln_g=conf_ln_g, conf_ln_b=conf_ln_b)
    for i, k in enumerate(CONV_W):
        small[k] = jnp.concatenate([per_chip[s][i] for s in range(NCHIP)], axis=2)

    loss_local, grad_x, sums, small_g = local_step(x, c, ctx, c_ctx, layer_bufs, small, loss_target,
                                                   ids=(core.reshape(1), chip.reshape(1)))
    loss = lax.psum(loss_local, ("x", "y", "c"))

    small_shapes = [small_g[k].shape for k in SMALL]
    small_sum = sum_leading(all_gather8(_pack_rows([small_g[k] for k in SMALL], 8), "gather_small_grads"), "small_sum")
    small_g = dict(zip(SMALL, _unpack(small_sum, small_shapes)))
    for k in CONV_W:
        width = local[k].shape[2]
        small_g[k] = lax.dynamic_slice_in_dim(small_g[k], chip * width, width, axis=2)

    grad, delta, new_m, new_v = {}, {}, {}, {}
    for k in order:
        if k in MATMUL_W:
            outs = [lax.empty(local[k].shape, F32) for _ in range(4)]
            for l in range(DEPTH):
                outs = adamw_layer(local[k], sums[l][MATMUL_W.index(k)], mom_m[k], mom_v[k], l, outs, f"adamw_{k}_{l}")
            grad[k], delta[k], new_m[k], new_v[k] = outs
            continue
        shp = local[k].shape
        view = (1, shp[0]) if len(shp) == 1 else (-1, shp[-1])
        d_, m_, v_ = adamw(local[k].reshape(view), small_g[k].reshape(view), mom_m[k].reshape(view),
                           mom_v[k].reshape(view), "adamw_" + k)
        grad[k], delta[k], new_m[k], new_v[k] = small_g[k], d_.reshape(shp), m_.reshape(shp), v_.reshape(shp)
    return (loss, grad_x, *[grad[k] for k in order], *[delta[k] for k in order], *[new_m[k] for k in order],
            *[new_v[k] for k in order])


def local_step(x, c, ctx, c_ctx, layer_bufs, small, loss_target, ids=None):
    tabs = _rope_tables() + (_group_matrix(),)
    distributed = ids is not None
    layer_bufs = list(layer_bufs)
    if distributed:
        layer_bufs[0], _ = comm_only("gather0_ici", gather_ici(layer_bufs[0]))
        layer_bufs[0], _ = comm_only("gather0_d2d", gather_d2d(layer_bufs[0]))

    cin = jnp.concatenate([c, c_ctx[None], jnp.zeros((8 - BL - 1, D), F32)], axis=0)
    cs = silu_rows(cin, "silu_c")
    xs = jnp.concatenate([ctx, x], axis=1).reshape(NROW, D)
    saved, layer_w = [], []
    for i in range(DEPTH):
        w = _layer_weights(layer_bufs[i], small, i)
        mods = mm_nn(cs, w["w_ada"], bias=small["b_ada"][i][None], name=f"l{i}_mod").reshape(8, 1, NMOD * D)
        ahead = layer_bufs[i + 1] if distributed and i + 1 < DEPTH else None
        xs, sv, ahead = _layer_fwd(i, xs, mods, w, tabs, ahead)
        if ahead is not None:
            layer_bufs[i + 1] = ahead
        saved.append(sv)
        layer_w.append(w)
    dxs, loss_lanes = loss_fwd_bwd(xs, loss_target.reshape(BL * SEQ, D), "loss")
    loss_local = 0.5 * jnp.sum(loss_lanes) / D

    grads, sums = [None] * DEPTH, [None] * DEPTH
    pending = None
    for i in reversed(range(DEPTH)):
        dxs, grads[i], done = _layer_bwd(i, dxs, saved[i], layer_w[i], tabs, cs, pending, ids)
        partial = [grads[i][k] for k in MATMUL_W]
        if distributed:
            if pending is not None:
                sums[i + 1] = done
            pending = partial
        else:
            sums[i] = partial
    if distributed:
        _, swapped = comm_only("rs0_swap", rs_swap(pending))
        sends, arrs = zip(*[rs_add(g_, s_, ids[0], ids[1], "rs0_add_" + k) for k, g_, s_ in zip(MATMUL_W, pending, swapped)])
        arrs, _ = comm_only("rs0_ici", rs_ici(sends, arrs))
        sums[0], _ = comm_only("rs0_d2d", rs_d2d(arrs))
    grad_x = dxs.reshape(BL, RE, D)[:, CTX:, :]
    dcin = silu_rows_bwd(cin, jnp.stack([grads[i]["dcs"] for i in range(DEPTH)]), "silu_c_bwd")

    def stack(key):
        return jnp.stack([grads[i][key] for i in range(DEPTH)])

    small_g = dict(c_ctx=dcin[BL], b_ada=stack("b_ada").reshape(DEPTH, NMOD * D), q_norm=stack("q_norm"),
                   k_norm=stack("k_norm"), conf_dw_b=stack("conf_dw_b").reshape(DEPTH, CW),
                   conf_ln_g=stack("conf_ln_g").reshape(DEPTH, CW), conf_ln_b=stack("conf_ln_b").reshape(DEPTH, CW),
                   conf_dw_w=stack("conf_dw_w"), sc_dw_w=stack("sc_dw_w"))
    return loss_local, grad_x, sums, small_g
```

```python
import functools
from typing import Any, Callable, NamedTuple, Sequence

import jax
import jax.numpy as jnp
from jax import lax
from jax.experimental import pallas as pl
from jax.experimental.pallas import tpu as pltpu

F32, BF16 = jnp.float32, jnp.bfloat16
HIGHEST = lax.Precision.HIGHEST

D = 1024
SEQ = 2048
CTX = 256
DEPTH = 4
BL = 4
GRID_W = 64
HD = 64
NQ = 8
NKV = 2
AW = NQ * HD
KVW = NKV * HD
CW = D // 2
CONF_K = 31
SC_K = 3
NMOD = 6
FH = -(-8 * D // (3 * 256)) * 256
EPS = 1e-6
ROPE_THETA = 10000.0
ATTN_SCALE = HD ** -0.5
OFF_K = AW
OFF_V = OFF_K + KVW
OFF_CONF = OFF_V + KVW
OFF_SC = OFF_CONF + 2 * CW
OFF_GATE = OFF_SC + 3 * CW
IN_W = OFF_GATE + 3 * D
QKVW = OFF_CONF
NCHIP = 4

ADAM_LR, ADAM_B1, ADAM_B2, ADAM_EPS, ADAM_WD, ADAM_STEP = 0.001, 0.9, 0.999, 1e-08, 0.01, 10

TM = CTX
RE = CTX + SEQ
TPE = RE // TM
NROW = BL * RE
NT = NROW // TM
LANE = 128
CB = CW // LANE
CONV_CH = 128
PADR = 16
VMEM_LIMIT = 52 * 1024 * 1024

MESH = pl.DeviceIdType.MESH
ANY = pl.BlockSpec(memory_space=pl.ANY)


class CommSpec(NamedTuple):
    ro: Sequence[Any]
    rw: Sequence[Any]
    new: Sequence[Any]
    nsem: int
    program: Callable


def _pcall(body, *, name, out_shape, grid=(), in_specs=None, out_specs=None, scratch=(), sem=None, comm=None):
    if not grid:
        return pl.pallas_call(body, name=name, out_shape=out_shape)
    if comm is None:
        params = pltpu.CompilerParams(dimension_semantics=sem, vmem_limit_bytes=VMEM_LIMIT)
        return pl.pallas_call(body, name=name, out_shape=out_shape, grid=grid, in_specs=in_specs, out_specs=out_specs,
                              scratch_shapes=list(scratch), compiler_params=params)

    single = not isinstance(out_shape, (tuple, list))
    out_shapes = (out_shape,) if single else tuple(out_shape)
    out_specs_t = (out_specs,) if single else tuple(out_specs)
    n_in, n_out, n_scr = len(in_specs), len(out_shapes), len(scratch)
    n_ro, n_rw, n_new = len(comm.ro), len(comm.rw), len(comm.new)

    def carrier(*refs):
        ins = refs[:n_in]
        ro_refs = refs[n_in:n_in + n_ro]
        o0 = n_in + n_ro + n_rw
        outs = refs[o0:o0 + n_out]
        rw_refs = refs[o0 + n_out:o0 + n_out + n_rw]
        new_refs = refs[o0 + n_out + n_rw:o0 + n_out + n_rw + n_new]
        s0 = o0 + n_out + n_rw + n_new
        scr = refs[s0:s0 + n_scr]
        send_sems, recv_sems = refs[s0 + n_scr:]
        first = functools.reduce(jnp.logical_and, [pl.program_id(a) == 0 for a in range(len(grid))])
        last = functools.reduce(jnp.logical_and, [pl.program_id(a) == grid[a] - 1 for a in range(len(grid))])
        starts, arrivals = comm.program(ro_refs, rw_refs, new_refs, send_sems, recv_sems)

        @pl.when(first)
        def _():
            for cp in starts:
                cp.start()

        body(*ins, *outs, *scr)

        @pl.when(last)
        def _():
            for cp in arrivals:
                cp.wait_recv()
            for cp in starts:
                cp.wait_send()

    def call(*args):
        rw_shapes = tuple(jax.ShapeDtypeStruct(a.shape, a.dtype) for a in comm.rw)
        res = pl.pallas_call(
            carrier, name=name, out_shape=out_shapes + rw_shapes + tuple(comm.new), grid=grid,
            in_specs=list(in_specs) + [ANY] * (n_ro + n_rw),
            out_specs=out_specs_t + (ANY,) * (n_rw + n_new),
            scratch_shapes=list(scratch) + [pltpu.SemaphoreType.DMA((comm.nsem,)), pltpu.SemaphoreType.DMA((comm.nsem,))],
            input_output_aliases={n_in + n_ro + i: n_out + i for i in range(n_rw)},
            compiler_params=pltpu.CompilerParams(dimension_semantics=("arbitrary",) * len(grid),
                                                 vmem_limit_bytes=VMEM_LIMIT))(*args, *comm.ro, *comm.rw)
        compute = res[0] if single else tuple(res[:n_out])
        return compute, list(res[n_out:n_out + n_rw]), list(res[n_out + n_rw:])

    return call


def comm_only(name, comm):
    n_ro, n_rw, n_new = len(comm.ro), len(comm.rw), len(comm.new)

    def body(*refs):
        ro_refs = refs[:n_ro]
        rw_refs = refs[n_ro + n_rw:n_ro + 2 * n_rw]
        new_refs = refs[n_ro + 2 * n_rw:n_ro + 2 * n_rw + n_new]
        send_sems, recv_sems = refs[n_ro + 2 * n_rw + n_new:]
        starts, arrivals = comm.program(ro_refs, rw_refs, new_refs, send_sems, recv_sems)
        for cp in starts:
            cp.start()
        for cp in arrivals:
            cp.wait_recv()
        for cp in starts:
            cp.wait_send()

    rw_shapes = tuple(jax.ShapeDtypeStruct(a.shape, a.dtype) for a in comm.rw)
    res = pl.pallas_call(body, name=name, out_shape=rw_shapes + tuple(comm.new), in_specs=[ANY] * (n_ro + n_rw),
                         out_specs=(ANY,) * (n_rw + n_new), input_output_aliases={n_ro + i: i for i in range(n_rw)},
                         scratch_shapes=[pltpu.SemaphoreType.DMA((comm.nsem,)), pltpu.SemaphoreType.DMA((comm.nsem,))])(
                             *comm.ro, *comm.rw)
    return list(res[:n_rw]), list(res[n_rw:])


def _pick(n, cands):
    for t in cands:
        if n % t == 0:
            return t
    return n


def _seg(t):
    return jnp.where(t % TPE == 0, BL, t // TPE)


def _slot(t):
    return 2 * (t // TPE) + jnp.where(t % TPE == 0, 0, 1)


def _sigmoid(x):
    return 1.0 / (1.0 + jnp.exp(-x))


MM_BUDGET = 40 * 1024 * 1024
N_TILE_CAP = 1664


def _tile(n, cap=N_TILE_CAP):
    if n <= cap:
        return n
    for t in range(cap - cap % LANE, 0, -LANE):
        if n % t == 0:
            return t
    return n


def _row_tile(m, bytes_of):
    for tm in (1024, 512, 256, 128):
        if m % tm == 0 and bytes_of(tm) <= MM_BUDGET:
            return tm
    return m


def _w_dims(w):
    arr, kind = w
    if kind == "cols":
        return arr.shape[1], NCHIP * arr.shape[2]
    return arr.shape


def _sz(dtype):
    return jnp.dtype(dtype).itemsize


def mm_nn(a, w, *, bias=None, out_dtype=F32, name):
    arr, kind = w
    m, k = a.shape
    _, n = _w_dims(w)
    tn = _tile(arr.shape[2]) if kind == "cols" else _tile(n)
    tm = _row_tile(m, lambda t: 2 * (t * k * _sz(a.dtype) + k * tn * 2 + t * tn * _sz(out_dtype)))
    if kind == "mat":
        b_spec = pl.BlockSpec((k, tn), lambda j, i: (0, j))
    else:
        per = arr.shape[2] // tn
        b_spec = pl.BlockSpec((None, k, tn), lambda j, i: (j // per, 0, j % per))
    has_bias = bias is not None

    def body(*refs):
        out = jnp.dot(refs[0][...].astype(BF16), refs[1][...].astype(BF16), preferred_element_type=F32)
        if has_bias:
            out = out + refs[2][...]
        refs[-1][...] = out.astype(out_dtype)

    in_specs = [pl.BlockSpec((tm, k), lambda j, i: (i, 0)), b_spec]
    args = [a, arr]
    if has_bias:
        in_specs.append(pl.BlockSpec((1, tn), lambda j, i: (0, j)))
        args.append(bias)
    return _pcall(body, name=name, out_shape=jax.ShapeDtypeStruct((m, n), out_dtype), grid=(n // tn, m // tm),
                  in_specs=in_specs, out_specs=pl.BlockSpec((tm, tn), lambda j, i: (i, j)),
                  sem=("parallel", "parallel"))(*args)


def mm_nt(a, w, *, acc=None, out_dtype=F32, name):
    arr, kind = w
    kdim, _ = _w_dims(w)
    has_acc = acc is not None
    tk = _tile(kdim, 1408)
    if kind == "cols":
        c = arr.shape[2]
        m = a.shape[-2]
        if a.ndim == 3:
            a_spec = lambda t: pl.BlockSpec((None, t, c), lambda j, i, s: (s // 2, i, s % 2))
        else:
            a_spec = lambda t: pl.BlockSpec((t, c), lambda j, i, s: (i, s))
        tm = _row_tile(m, lambda t: 2 * (t * c * _sz(a.dtype) + tk * c * 2 + t * tk * _sz(out_dtype)) + t * tk * 4)

        def body(a_ref, b_ref, o_ref, acc_ref):
            s = pl.program_id(2)

            @pl.when(s == 0)
            def _():
                acc_ref[...] = jnp.zeros_like(acc_ref)

            acc_ref[...] += lax.dot_general(a_ref[...].astype(BF16), b_ref[...], (((1,), (1,)), ((), ())),
                                            preferred_element_type=F32)

            @pl.when(s == NCHIP - 1)
            def _():
                o_ref[...] = acc_ref[...].astype(out_dtype)

        return _pcall(body, name=name, out_shape=jax.ShapeDtypeStruct((m, kdim), out_dtype),
                      grid=(kdim // tk, m // tm, NCHIP),
                      in_specs=[a_spec(tm), pl.BlockSpec((None, tk, c), lambda j, i, s: (s, j, 0))],
                      out_specs=pl.BlockSpec((tm, tk), lambda j, i, s: (i, j)),
                      scratch=[pltpu.VMEM((tm, tk), F32)],
                      sem=("parallel", "parallel", "arbitrary"))(a, arr)

    m, n = a.shape
    tm = _row_tile(m, lambda t: 2 * (t * n * _sz(a.dtype) + tk * n * 2 + t * tk * (_sz(out_dtype) + 4 * has_acc)))
    b_spec = pl.BlockSpec((tk, n), lambda j, i: (j, 0))

    def body(*refs):
        out = lax.dot_general(refs[0][...].astype(BF16), refs[1][...].astype(BF16), (((1,), (1,)), ((), ())),
                              preferred_element_type=F32)
        if has_acc:
            out = out + refs[2][...]
        refs[-1][...] = out.astype(out_dtype)

    in_specs = [pl.BlockSpec((tm, n), lambda j, i: (i, 0)), b_spec]
    args = [a, arr]
    if has_acc:
        in_specs.append(pl.BlockSpec((tm, tk), lambda j, i: (i, j)))
        args.append(acc)
    return _pcall(body, name=name, out_shape=jax.ShapeDtypeStruct((m, kdim), out_dtype), grid=(kdim // tk, m // tm),
                  in_specs=in_specs, out_specs=pl.BlockSpec((tm, tk), lambda j, i: (i, j)),
                  sem=("parallel", "parallel"))(*args)


def mm_tn(a, b, *, cols=False, out_dtype=F32, name):
    rows, k = a.shape
    halves = b.ndim == 3
    n = 2 * b.shape[2] if halves else b.shape[1]
    odt = out_dtype
    if cols:
        c = n // NCHIP
        tn, tk = _tile(c), k
        per = c // tn
        out_spec = pl.BlockSpec((None, tk, tn), lambda i, j, r: (j // per, 0, j % per))
        out_shape = jax.ShapeDtypeStruct((NCHIP, k, c), odt)
    else:
        tn, tk = _tile(n), _tile(k, 1408)
        out_spec = pl.BlockSpec((tk, tn), lambda i, j, r: (i, j))
        out_shape = jax.ShapeDtypeStruct((k, n), odt)
    tr = _row_tile(rows, lambda t: 2 * (t * tk * _sz(a.dtype) + t * tn * _sz(b.dtype) + tk * tn * _sz(odt)) + tk * tn * 4)
    nsteps = rows // tr

    def body(*refs):
        a_ref, b_ref = refs[0], refs[1]
        o_ref, acc_ref = refs[-2], refs[-1]
        r = pl.program_id(2)

        @pl.when(r == 0)
        def _():
            acc_ref[...] = jnp.zeros_like(acc_ref)

        acc_ref[...] += lax.dot_general(a_ref[...].astype(BF16), b_ref[...].astype(BF16), (((0,), (0,)), ((), ())),
                                        preferred_element_type=F32)

        @pl.when(r == nsteps - 1)
        def _():
            o_ref[...] = acc_ref[...].astype(odt)

    if halves:
        per_half = (n // 2) // tn
        b_spec = pl.BlockSpec((None, tr, tn), lambda i, j, r: (j // per_half, r, j % per_half))
    else:
        b_spec = pl.BlockSpec((tr, tn), lambda i, j, r: (r, j))
    return _pcall(body, name=name, out_shape=out_shape, grid=(k // tk, n // tn, nsteps),
                  in_specs=[pl.BlockSpec((tr, tk), lambda i, j, r: (r, i)), b_spec], out_specs=out_spec,
                  scratch=[pltpu.VMEM((tk, tn), F32)], sem=("parallel", "parallel", "arbitrary"))(a, b)


def ffn_in_swiglu(h, w_in, name, comm=None):
    m, k = h.shape
    c = w_in.shape[2]
    tm = 512

    def body(h_ref, wa_ref, wb_ref, f_ref, u_ref):
        hv = h_ref[...]
        a = jnp.dot(hv, wa_ref[...], preferred_element_type=F32)
        b = jnp.dot(hv, wb_ref[...], preferred_element_type=F32)
        f_ref[...] = (a * _sigmoid(a) * b).astype(BF16)
        u_ref[0] = a.astype(BF16)
        u_ref[1] = b.astype(BF16)

    return _pcall(body, name=name,
                  out_shape=(jax.ShapeDtypeStruct((m, FH), BF16), jax.ShapeDtypeStruct((2, m, FH), BF16)),
                  grid=(2, m // tm),
                  in_specs=[pl.BlockSpec((tm, k), lambda j, i: (i, 0)),
                            pl.BlockSpec((None, k, c), lambda j, i: (j, 0, 0)),
                            pl.BlockSpec((None, k, c), lambda j, i: (2 + j, 0, 0))],
                  out_specs=(pl.BlockSpec((tm, c), lambda j, i: (i, j)), pl.BlockSpec((2, tm, c), lambda j, i: (0, i, j))),
                  sem=("parallel", "parallel"), comm=comm)(h, w_in, w_in)


def d_f_swiglu(dof, w_out, u2, name, comm=None):
    m, k = dof.shape
    c = FH // 2
    tm = 512

    def body(d_ref, w_ref, u_ref, du_ref):
        df = lax.dot_general(d_ref[...], w_ref[...], (((1,), (1,)), ((), ())), preferred_element_type=F32)
        a, b = u_ref[0].astype(F32), u_ref[1].astype(F32)
        sg = _sigmoid(a)
        du_ref[0] = (df * b * (sg * (1.0 + a * (1.0 - sg)))).astype(BF16)
        du_ref[1] = (df * a * sg).astype(BF16)

    ublk = pl.BlockSpec((2, tm, c), lambda j, i: (0, i, j))
    return _pcall(body, name=name, out_shape=jax.ShapeDtypeStruct((2, m, FH), BF16), grid=(2, m // tm),
                  in_specs=[pl.BlockSpec((tm, k), lambda j, i: (i, 0)), pl.BlockSpec((c, k), lambda j, i: (j, 0)), ublk],
                  out_specs=ublk, sem=("parallel", "parallel"), comm=comm)(dof, w_out, u2)


GATE_TN = 512


def gate_mm_fwd(h, wi_gate, o, hs, z, wo, wc, ws, name):
    m, k = h.shape
    tm, tn = 512, min(GATE_TN, D)
    nj = D // tn

    def body(h_ref, g0_ref, g1_ref, g2_ref, o_ref, hs_ref, z_ref, wo_ref, wc_ref, ws_ref, m_ref, g_ref, y_ref):
        hv = h_ref[...]
        acc = jnp.zeros((tm, tn), F32)
        for g, (gw_ref, x_ref, w_ref) in enumerate(((g0_ref, o_ref, wo_ref), (g1_ref, hs_ref, wc_ref),
                                                    (g2_ref, z_ref, ws_ref))):
            gate = _sigmoid(jnp.dot(hv, gw_ref[...], preferred_element_type=F32))
            y = jnp.dot(x_ref[...], w_ref[...], preferred_element_type=F32)
            acc += gate * y
            g_ref[g] = gate.astype(BF16)
            y_ref[g] = y.astype(BF16)
        m_ref[...] = acc.astype(BF16)

    def gate_w(g):
        return pl.BlockSpec((k, tn), lambda j, i: (0, g * nj + j))

    def branch(width):
        return pl.BlockSpec((tm, width), lambda j, i: (i, 0))

    def branch_w(width):
        return pl.BlockSpec((width, tn), lambda j, i: (0, j))

    stacked = pl.BlockSpec((3, tm, tn), lambda j, i: (0, i, j))
    sds3 = jax.ShapeDtypeStruct((3, m, D), BF16)
    return _pcall(body, name=name, out_shape=(jax.ShapeDtypeStruct((m, D), BF16), sds3, sds3), grid=(nj, m // tm),
                  in_specs=[pl.BlockSpec((tm, k), lambda j, i: (i, 0)), gate_w(0), gate_w(1), gate_w(2),
                            branch(o.shape[1]), branch(hs.shape[1]), branch(z.shape[1]),
                            branch_w(wo.shape[0]), branch_w(wc.shape[0]), branch_w(ws.shape[0])],
                  out_specs=(pl.BlockSpec((tm, tn), lambda j, i: (i, j)), stacked, stacked),
                  sem=("parallel", "parallel"))(h, wi_gate, wi_gate, wi_gate, o, hs, z, wo, wc, ws)


def d_merged_gate(dmixed, w_mix, gates, ys, name):
    m, k = dmixed.shape
    tm = 256

    def body(d_ref, w_ref, g_ref, y_ref, da_ref, db_ref, ds_ref, dp_ref):
        dm = lax.dot_general(d_ref[...], w_ref[...], (((1,), (1,)), ((), ())), preferred_element_type=F32)
        for g, dy_ref in enumerate((da_ref, db_ref, ds_ref)):
            gate = g_ref[g].astype(F32)
            dy_ref[...] = (dm * gate).astype(BF16)
            dp_ref[:, g * D:(g + 1) * D] = (dm * y_ref[g].astype(F32) * gate * (1.0 - gate)).astype(BF16)

    stacked = pl.BlockSpec((3, tm, D), lambda i: (0, i, 0))
    row = pl.BlockSpec((tm, D), lambda i: (i, 0))
    sds = jax.ShapeDtypeStruct((m, D), BF16)
    return _pcall(body, name=name, out_shape=(sds, sds, sds, jax.ShapeDtypeStruct((m, 3 * D), BF16)), grid=(m // tm,),
                  in_specs=[pl.BlockSpec((tm, k), lambda i: (i, 0)), pl.BlockSpec((D, k), lambda i: (0, 0)),
                            stacked, stacked],
                  out_specs=(row, row, row, pl.BlockSpec((tm, 3 * D), lambda i: (i, 0))),
                  sem=("parallel",))(dmixed, w_mix, gates, ys)


def _mods_spec():
    return pl.BlockSpec((1, 1, NMOD * D), lambda t: (_seg(t), 0, 0))


def _rows(width):
    return pl.BlockSpec((TM, width), lambda t: (t, 0))


def norm_mod_fwd(x, mods, k_sh, k_sc, name):
    def body(x_ref, m_ref, h_ref):
        x = x_ref[...]
        r = lax.rsqrt(jnp.mean(x * x, axis=-1, keepdims=True) + EPS)
        sh = m_ref[0, :, k_sh * D:(k_sh + 1) * D]
        sc = m_ref[0, :, k_sc * D:(k_sc + 1) * D]
        h_ref[...] = (x * r * (1.0 + sc) + sh).astype(BF16)

    return _pcall(body, name=name, out_shape=jax.ShapeDtypeStruct((NROW, D), BF16), grid=(NT,),
                  in_specs=[_rows(D), _mods_spec()], out_specs=_rows(D), sem=("parallel",))(x, mods)


def _accumulate_slot(t, ref, part):
    first = (t % TPE) <= 1

    @pl.when(first)
    def _():
        ref[0] = part

    @pl.when(jnp.logical_not(first))
    def _():
        ref[0] += part


def norm_mod_bwd(x, mods, dh, dres, k_sc, name):
    def body(x_ref, m_ref, dh_ref, dres_ref, dx_ref, dp_ref):
        t = pl.program_id(0)
        x = x_ref[...]
        r = lax.rsqrt(jnp.mean(x * x, axis=-1, keepdims=True) + EPS)
        xn = x * r
        sc = m_ref[0, :, k_sc * D:(k_sc + 1) * D]
        dh = dh_ref[...]
        dxn = dh * (1.0 + sc)
        dx_ref[...] = r * (dxn - xn * jnp.mean(dxn * xn, axis=-1, keepdims=True)) + dres_ref[...]
        part = jnp.concatenate([jnp.sum(dh, axis=0, keepdims=True), jnp.sum(dh * xn, axis=0, keepdims=True)], axis=1)
        _accumulate_slot(t, dp_ref, part)

    return _pcall(body, name=name,
                  out_shape=(jax.ShapeDtypeStruct((NROW, D), F32), jax.ShapeDtypeStruct((2 * BL, 1, 2 * D), F32)),
                  grid=(NT,), in_specs=[_rows(D), _mods_spec(), _rows(D), _rows(D)],
                  out_specs=(_rows(D), pl.BlockSpec((1, 1, 2 * D), lambda t: (_slot(t), 0, 0))),
                  sem=("arbitrary",))(x, mods, dh, dres)


def resid_norm_fwd(x, y, mods_g, k_g, mods_n, k_sh, k_sc, name):
    def body(x_ref, y_ref, mg_ref, mn_ref, x1_ref, h_ref):
        x1 = x_ref[...] + mg_ref[0, :, k_g * D:(k_g + 1) * D] * y_ref[...]
        x1_ref[...] = x1
        r = lax.rsqrt(jnp.mean(x1 * x1, axis=-1, keepdims=True) + EPS)
        sh = mn_ref[0, :, k_sh * D:(k_sh + 1) * D]
        sc = mn_ref[0, :, k_sc * D:(k_sc + 1) * D]
        h_ref[...] = (x1 * r * (1.0 + sc) + sh).astype(BF16)

    return _pcall(body, name=name,
                  out_shape=(jax.ShapeDtypeStruct((NROW, D), F32), jax.ShapeDtypeStruct((NROW, D), BF16)), grid=(NT,),
                  in_specs=[_rows(D), _rows(D), _mods_spec(), _mods_spec()], out_specs=(_rows(D), _rows(D)),
                  sem=("parallel",))(x, y, mods_g, mods_n)


def norm_resid_bwd(x, mods_n, dh, dres, k_sc, y, mods_g, k_g, name):
    def body(x_ref, mn_ref, dh_ref, dres_ref, y_ref, mg_ref, dx_ref, dpn_ref, dy_ref, dpg_ref):
        t = pl.program_id(0)
        x = x_ref[...]
        r = lax.rsqrt(jnp.mean(x * x, axis=-1, keepdims=True) + EPS)
        xn = x * r
        sc = mn_ref[0, :, k_sc * D:(k_sc + 1) * D]
        dh = dh_ref[...]
        dxn = dh * (1.0 + sc)
        dx = r * (dxn - xn * jnp.mean(dxn * xn, axis=-1, keepdims=True)) + dres_ref[...]
        dx_ref[...] = dx
        dy_ref[...] = (dx * mg_ref[0, :, k_g * D:(k_g + 1) * D]).astype(BF16)
        part = jnp.concatenate([jnp.sum(dh, axis=0, keepdims=True), jnp.sum(dh * xn, axis=0, keepdims=True)], axis=1)
        _accumulate_slot(t, dpn_ref, part)
        _accumulate_slot(t, dpg_ref, jnp.sum(dx * y_ref[...], axis=0, keepdims=True))

    def slot(width):
        return pl.BlockSpec((1, 1, width), lambda t: (_slot(t), 0, 0))

    return _pcall(body, name=name,
                  out_shape=(jax.ShapeDtypeStruct((NROW, D), F32), jax.ShapeDtypeStruct((2 * BL, 1, 2 * D), F32),
                             jax.ShapeDtypeStruct((NROW, D), BF16), jax.ShapeDtypeStruct((2 * BL, 1, D), F32)),
                  grid=(NT,), in_specs=[_rows(D), _mods_spec(), _rows(D), _rows(D), _rows(D), _mods_spec()],
                  out_specs=(_rows(D), slot(2 * D), _rows(D), slot(D)), sem=("arbitrary",))(x, mods_n, dh, dres, y, mods_g)


def gate_resid_fwd(x, y, mods, k_g, name):
    def body(x_ref, y_ref, m_ref, o_ref):
        o_ref[...] = x_ref[...] + m_ref[0, :, k_g * D:(k_g + 1) * D] * y_ref[...]

    return _pcall(body, name=name, out_shape=jax.ShapeDtypeStruct((NROW, D), F32), grid=(NT,),
                  in_specs=[_rows(D), _rows(D), _mods_spec()], out_specs=_rows(D), sem=("parallel",))(x, y, mods)


def gate_resid_bwd(dx, y, mods, k_g, name):
    def body(dx_ref, y_ref, m_ref, dy_ref, dp_ref):
        t = pl.program_id(0)
        dx = dx_ref[...]
        dy_ref[...] = (dx * m_ref[0, :, k_g * D:(k_g + 1) * D]).astype(BF16)
        _accumulate_slot(t, dp_ref, jnp.sum(dx * y_ref[...], axis=0, keepdims=True))

    return _pcall(body, name=name,
                  out_shape=(jax.ShapeDtypeStruct((NROW, D), BF16), jax.ShapeDtypeStruct((2 * BL, 1, D), F32)),
                  grid=(NT,), in_specs=[_rows(D), _rows(D), _mods_spec()],
                  out_specs=(_rows(D), pl.BlockSpec((1, 1, D), lambda t: (_slot(t), 0, 0))),
                  sem=("arbitrary",))(dx, y, mods)


def _swap16(y, lo16):
    return jnp.where(lo16, pltpu.roll(y, LANE - 16, 1), pltpu.roll(y, 16, 1))


def _group_mean(v, g_mat):
    return jnp.dot(v, g_mat, precision=HIGHEST, preferred_element_type=F32)


def qkv_fwd(p_main, cos_t, sin_t, g_mat, gq, gk, name):
    def body(p_ref, cos_ref, sin_ref, g_ref, gq_ref, gk_ref, q_ref, k_ref, v_ref):
        cos, sin, g_mat_v = cos_ref[...], sin_ref[...], g_ref[...]
        lo16 = (lax.broadcasted_iota(jnp.int32, (TM, LANE), 1) % 32) < 16

        def block(xb, g):
            r = lax.rsqrt(_group_mean(xb * xb, g_mat_v) + EPS)
            y = xb * r * g
            return y * cos + _swap16(y, lo16) * sin

        for j in range(AW // LANE):
            q_ref[:, j * LANE:(j + 1) * LANE] = (block(p_ref[:, j * LANE:(j + 1) * LANE], gq_ref[...])
                                                 * ATTN_SCALE).astype(BF16)
        lo = lax.broadcasted_iota(jnp.int32, (TM, LANE), 1) < HD
        for src, dst_ref in ((block(p_ref[:, OFF_K:OFF_K + LANE], gk_ref[...]), k_ref), (p_ref[:, OFF_V:OFF_V + LANE], v_ref)):
            swapped = pltpu.roll(src, HD, 1)
            dst_ref[:, 0:LANE] = jnp.where(lo, src, swapped).astype(BF16)
            dst_ref[:, LANE:2 * LANE] = jnp.where(lo, swapped, src).astype(BF16)

    tab = pl.BlockSpec((TM, LANE), lambda t: (t % TPE, 0))
    small = pl.BlockSpec((1, LANE), lambda t: (0, 0))
    return _pcall(body, name=name,
                  out_shape=(jax.ShapeDtypeStruct((NROW, AW), BF16), jax.ShapeDtypeStruct((NROW, 2 * KVW), BF16),
                             jax.ShapeDtypeStruct((NROW, 2 * KVW), BF16)),
                  grid=(NT,),
                  in_specs=[_rows(QKVW), tab, tab, pl.BlockSpec((LANE, LANE), lambda t: (0, 0)), small, small],
                  out_specs=(_rows(AW), _rows(2 * KVW), _rows(2 * KVW)),
                  sem=("parallel",))(p_main, cos_t, sin_t, g_mat, gq, gk)


def qkv_bwd(p_main, cos_t, sin_t, g_mat, gq, gk, dq, dk, dv, name, comm=None):
    def body(p_ref, cos_ref, sin_ref, g_ref, gq_ref, gk_ref, dq_ref, dk_ref, dv_ref, dp_ref, dg_ref):
        t = pl.program_id(0)
        cos, sin, g_mat_v = cos_ref[...], sin_ref[...], g_ref[...]
        lo16 = (lax.broadcasted_iota(jnp.int32, (TM, LANE), 1) % 32) < 16

        def block(xb, g, dyr):
            r = lax.rsqrt(_group_mean(xb * xb, g_mat_v) + EPS)
            xn = xb * r
            dy = dyr * cos + _swap16(dyr * sin, lo16)
            dgl = jnp.sum(dy * xn, axis=0, keepdims=True)
            dxn = dy * g
            return r * (dxn - xn * _group_mean(dxn * xn, g_mat_v)), dgl

        parts = []
        for j in range(AW // LANE):
            sl = slice(j * LANE, (j + 1) * LANE)
            dxb, dgl = block(p_ref[:, sl], gq_ref[...], dq_ref[:, sl] * ATTN_SCALE)
            dp_ref[:, sl] = dxb.astype(BF16)
            parts.append(dgl)
        lo = lax.broadcasted_iota(jnp.int32, (TM, LANE), 1) < HD

        def fold(d_ref):
            d0, d1 = d_ref[:, 0:LANE], d_ref[:, LANE:2 * LANE]
            return jnp.where(lo, d0 + pltpu.roll(d0, HD, 1), d1 + pltpu.roll(d1, HD, 1))

        dxb, dgl = block(p_ref[:, OFF_K:OFF_K + LANE], gk_ref[...], fold(dk_ref))
        dp_ref[:, OFF_K:OFF_K + LANE] = dxb.astype(BF16)
        parts.append(dgl)
        parts.append(jnp.zeros((1, LANE), F32))
        dp_ref[:, OFF_V:OFF_V + LANE] = fold(dv_ref).astype(BF16)
        part = jnp.concatenate(parts, axis=1)

        @pl.when(t == 0)
        def _():
            dg_ref[...] = part

        @pl.when(t != 0)
        def _():
            dg_ref[...] += part

    tab = pl.BlockSpec((TM, LANE), lambda t: (t % TPE, 0))
    small = pl.BlockSpec((1, LANE), lambda t: (0, 0))
    return _pcall(body, name=name,
                  out_shape=(jax.ShapeDtypeStruct((NROW, QKVW), BF16), jax.ShapeDtypeStruct((1, QKVW), F32)),
                  grid=(NT,),
                  in_specs=[_rows(QKVW), tab, tab, pl.BlockSpec((LANE, LANE), lambda t: (0, 0)), small, small,
                            _rows(AW), _rows(2 * KVW), _rows(2 * KVW)],
                  out_specs=(_rows(QKVW), pl.BlockSpec((1, QKVW), lambda t: (0, 0))),
                  sem=("arbitrary",), comm=comm)(p_main, cos_t, sin_t, g_mat, gq, gk, dq, dk, dv)


def _layer_norm_parts(yc):
    mu = jnp.mean(yc, axis=-1, keepdims=True)
    xc = yc - mu
    rs = lax.rsqrt(jnp.mean(xc * xc, axis=-1, keepdims=True) + EPS)
    return xc * rs, rs


def ln_silu_fwd(yc, g, b, name):
    def body(y_ref, g_ref, b_ref, o_ref):
        nrm, _ = _layer_norm_parts(y_ref[...])
        ln = nrm * g_ref[...] + b_ref[...]
        o_ref[...] = (ln * _sigmoid(ln)).astype(BF16)

    vec = pl.BlockSpec((1, CW), lambda t: (0, 0))
    return _pcall(body, name=name, out_shape=jax.ShapeDtypeStruct((NROW, CW), BF16), grid=(NT,),
                  in_specs=[_rows(CW), vec, vec], out_specs=_rows(CW), sem=("parallel",))(yc, g, b)


def ln_silu_bwd(yc, g, b, dhs, name):
    def body(y_ref, g_ref, b_ref, dh_ref, dy_ref, dg_ref, db_ref):
        t = pl.program_id(0)
        nrm, rs = _layer_norm_parts(y_ref[...])
        ln = nrm * g_ref[...] + b_ref[...]
        sg = _sigmoid(ln)
        dln = dh_ref[...] * (sg * (1.0 + ln * (1.0 - sg)))
        dn = dln * g_ref[...]
        dy_ref[...] = rs * (dn - jnp.mean(dn, axis=-1, keepdims=True)
                            - nrm * jnp.mean(dn * nrm, axis=-1, keepdims=True))
        pg = jnp.sum(dln * nrm, axis=0, keepdims=True)
        pb = jnp.sum(dln, axis=0, keepdims=True)

        @pl.when(t == 0)
        def _():
            dg_ref[...] = pg
            db_ref[...] = pb

        @pl.when(t != 0)
        def _():
            dg_ref[...] += pg
            db_ref[...] += pb

    vec = pl.BlockSpec((1, CW), lambda t: (0, 0))
    return _pcall(body, name=name,
                  out_shape=(jax.ShapeDtypeStruct((NROW, CW), F32), jax.ShapeDtypeStruct((1, CW), F32),
                             jax.ShapeDtypeStruct((1, CW), F32)),
                  grid=(NT,), in_specs=[_rows(CW), vec, vec, _rows(CW)], out_specs=(_rows(CW), vec, vec),
                  sem=("arbitrary",))(yc, g, b, dhs)


def loss_fwd_bwd(y, target, name):
    def body(y_ref, t_ref, dy_ref, l_ref):
        t = pl.program_id(0)
        latent = (t % TPE) != 0
        err = jnp.where(latent, y_ref[...] - t_ref[...], 0.0)
        dy_ref[...] = err * (1.0 / D)
        part = jnp.sum(err * err, axis=0, keepdims=True)

        @pl.when(t == 0)
        def _():
            l_ref[...] = part

        @pl.when(t != 0)
        def _():
            l_ref[...] += part

    tgt = pl.BlockSpec((TM, D), lambda t: ((t // TPE) * (TPE - 1) + jnp.maximum(t % TPE - 1, 0), 0))
    return _pcall(body, name=name,
                  out_shape=(jax.ShapeDtypeStruct((NROW, D), F32), jax.ShapeDtypeStruct((1, D), F32)),
                  grid=(NT,), in_specs=[_rows(D), tgt], out_specs=(_rows(D), pl.BlockSpec((1, D), lambda t: (0, 0))),
                  sem=("arbitrary",))(y, target)


QB_PER_KV = AW // LANE // NKV


def _softmax_parts(qm, k):
    s = lax.dot_general(qm, k, (((1,), (1,)), ((), ())), preferred_element_type=F32)
    e = jnp.exp(s - jnp.max(s, axis=-1, keepdims=True))
    return e, 1.0 / jnp.sum(e, axis=-1, keepdims=True)


def _attn_specs():
    qs = pl.BlockSpec((TM, LANE), lambda b, h, t, j: (b * TPE + t, h * QB_PER_KV + j))
    ks = pl.BlockSpec((RE, LANE), lambda b, h, t, j: (b, h))
    return qs, ks


def _lane_halves():
    lo = lax.broadcasted_iota(jnp.int32, (TM, LANE), 1) < HD
    return lo, jnp.logical_not(lo)


def _stack_heads(x, halves):
    zero = jnp.zeros_like(x)
    return jnp.concatenate([jnp.where(halves[0], x, zero), jnp.where(halves[1], x, zero)], axis=0)


def attn_fwd(q, k, v, name, comm=None):
    def body(q_ref, k_ref, v_ref, o_ref):
        t = pl.program_id(2)
        halves = _lane_halves()
        qv = q_ref[...]

        def run(nk):
            e, rinv = _softmax_parts(_stack_heads(qv, halves), k_ref[0:nk, :])
            out = jnp.dot(e.astype(BF16), v_ref[0:nk, :], preferred_element_type=F32) * rinv
            o_ref[...] = jnp.where(halves[0], out[0:TM], out[TM:2 * TM]).astype(BF16)

        @pl.when(t == 0)
        def _():
            run(CTX)

        @pl.when(t != 0)
        def _():
            run(RE)

    qs, ks = _attn_specs()
    return _pcall(body, name=name, out_shape=jax.ShapeDtypeStruct((NROW, AW), BF16), grid=(BL, NKV, TPE, QB_PER_KV),
                  in_specs=[qs, ks, ks], out_specs=qs, sem=("parallel",) * 4, comm=comm)(q, k, v)


def attn_bwd(q, k, v, o, do, name, comm=None):
    def body(q_ref, k_ref, v_ref, o_ref, do_ref, dq_ref, dk_ref, dv_ref):
        t, j = pl.program_id(2), pl.program_id(3)
        halves = _lane_halves()
        qv, dov = q_ref[...], do_ref[...]
        q2, do2 = _stack_heads(qv, halves), _stack_heads(dov, halves)
        delta = jnp.sum(do2.astype(F32) * jnp.concatenate([o_ref[...], o_ref[...]], axis=0).astype(F32), axis=-1,
                        keepdims=True)

        @pl.when(jnp.logical_and(t == 0, j == 0))
        def _():
            dk_ref[...] = jnp.zeros_like(dk_ref)
            dv_ref[...] = jnp.zeros_like(dv_ref)

        def run(nk):
            kv, vv = k_ref[0:nk, :], v_ref[0:nk, :]
            e, rinv = _softmax_parts(q2, kv)
            p = e * rinv
            dv_ref[0:nk, :] += lax.dot_general(p.astype(BF16), do2, (((0,), (0,)), ((), ())), preferred_element_type=F32)
            dp = lax.dot_general(do2, vv, (((1,), (1,)), ((), ())), preferred_element_type=F32)
            ds = (p * (dp - delta)).astype(BF16)
            dq = jnp.dot(ds, kv, preferred_element_type=F32)
            dk_ref[0:nk, :] += lax.dot_general(ds, q2, (((0,), (0,)), ((), ())), preferred_element_type=F32)
            dq_ref[...] = jnp.where(halves[0], dq[0:TM], dq[TM:2 * TM])

        @pl.when(t == 0)
        def _():
            run(CTX)

        @pl.when(t != 0)
        def _():
            run(RE)

    qs, ks = _attn_specs()
    return _pcall(body, name=name,
                  out_shape=(jax.ShapeDtypeStruct((NROW, AW), F32), jax.ShapeDtypeStruct((NROW, 2 * KVW), F32),
                             jax.ShapeDtypeStruct((NROW, 2 * KVW), F32)),
                  grid=(BL, NKV, TPE, QB_PER_KV), in_specs=[qs, ks, ks, qs, qs], out_specs=(qs, ks, ks),
                  sem=("parallel", "parallel", "arbitrary", "arbitrary"), comm=comm)(q, k, v, o, do)


CONV_SEGS = ((0, CTX), (CTX, SEQ))


def _p_block(col0):
    return pl.BlockSpec((RE, LANE), lambda cb, b: (b, col0 // LANE + cb))


def _conv_io(width):
    return pl.BlockSpec((RE, LANE), lambda cb, b: (b, cb))


def _taps(n):
    return pl.BlockSpec((n, LANE), lambda cb, b: (0, cb))


def _fill_pad(pad_ref, length, values):
    pad_ref[0:PADR, :] = jnp.zeros((PADR, LANE), F32)
    pad_ref[PADR + length:2 * PADR + length, :] = jnp.zeros((PADR, LANE), F32)
    pad_ref[PADR:PADR + length, :] = values


def _conv_chunk(pad_ref, w_ref, ntap, c0, first_row):
    acc = jnp.zeros((CONV_CH, LANE), F32)
    for kk in range(ntap):
        r0 = c0 + first_row(kk)
        acc += w_ref[kk:kk + 1, :] * pad_ref[r0:r0 + CONV_CH, :]
    return acc


def conv_fwd(p_main, wdw, bdw, w3, name):
    def body(a_ref, g_ref, bg_ref, cg_ref, xs_ref, w_ref, b_ref, w3_ref, yc_ref, z_ref, pad_ref):
        for off, length in CONV_SEGS:
            rows = slice(off, off + length)
            _fill_pad(pad_ref, length, a_ref[rows, :] * _sigmoid(g_ref[rows, :]))
            for c0 in range(0, length, CONV_CH):
                acc = _conv_chunk(pad_ref, w_ref, CONF_K, c0, lambda kk: PADR + kk - CONF_K // 2)
                yc_ref[off + c0:off + c0 + CONV_CH, :] = acc + b_ref[...]
            pad_ref[PADR:PADR + length, :] = cg_ref[rows, :] * xs_ref[rows, :]
            for c0 in range(0, length, CONV_CH):
                acc = _conv_chunk(pad_ref, w3_ref, SC_K, c0, lambda kk: PADR + kk - SC_K // 2)
                z_ref[off + c0:off + c0 + CONV_CH, :] = (bg_ref[off + c0:off + c0 + CONV_CH, :] * acc).astype(BF16)

    return _pcall(body, name=name,
                  out_shape=(jax.ShapeDtypeStruct((NROW, CW), F32), jax.ShapeDtypeStruct((NROW, CW), BF16)),
                  grid=(CB, BL),
                  in_specs=[_p_block(OFF_CONF), _p_block(OFF_CONF + CW), _p_block(OFF_SC), _p_block(OFF_SC + CW),
                            _p_block(OFF_SC + 2 * CW), _taps(CONF_K), _taps(1), _taps(SC_K)],
                  out_specs=(_conv_io(CW), _conv_io(CW)),
                  scratch=[pltpu.VMEM((SEQ + 2 * PADR, LANE), F32)],
                  sem=("parallel", "parallel"))(p_main, p_main, p_main, p_main, p_main, wdw, bdw, w3)


def _tap_grad(pad_ref, d_ref, off, length, first_row):
    acc = jnp.zeros((8, LANE), F32)
    for c0 in range(0, length, CONV_CH):
        prod = d_ref[off + c0:off + c0 + CONV_CH, :] * pad_ref[c0 + first_row:c0 + first_row + CONV_CH, :]
        acc += jnp.sum(prod.reshape(CONV_CH // 8, 8, LANE), axis=0)
    return jnp.sum(acc, axis=0, keepdims=True)


def conv_bwd(p_main, wdw, w3, dyc, dz, name):
    def body(a_ref, g_ref, bg_ref, cg_ref, xs_ref, w_ref, w3_ref, dyc_ref, dz_ref,
             da_ref, dg_ref, dbg_ref, dcg_ref, dxs_ref, dw_ref, db_ref, dw3_ref, pad_x, pad_d, dconv_ref):
        b = pl.program_id(1)

        @pl.when(b == 0)
        def _():
            dw_ref[...] = jnp.zeros_like(dw_ref)
            db_ref[...] = jnp.zeros_like(db_ref)
            dw3_ref[...] = jnp.zeros_like(dw3_ref)

        db_ref[...] += jnp.sum(dyc_ref[...], axis=0, keepdims=True)
        for off, length in CONV_SEGS:
            rows = slice(off, off + length)
            _fill_pad(pad_x, length, a_ref[rows, :] * _sigmoid(g_ref[rows, :]))
            _fill_pad(pad_d, length, dyc_ref[rows, :])
            for kk in range(CONF_K):
                dw_ref[kk:kk + 1, :] += _tap_grad(pad_x, dyc_ref, off, length, PADR + kk - CONF_K // 2)
            for c0 in range(0, length, CONV_CH):
                dh = _conv_chunk(pad_d, w_ref, CONF_K, c0, lambda kk: PADR + CONF_K // 2 - kk)
                ch = slice(off + c0, off + c0 + CONV_CH)
                sg = _sigmoid(g_ref[ch, :])
                da_ref[ch, :] = (dh * sg).astype(BF16)
                dg_ref[ch, :] = (dh * a_ref[ch, :] * sg * (1.0 - sg)).astype(BF16)
            pad_x[PADR:PADR + length, :] = cg_ref[rows, :] * xs_ref[rows, :]
            dconv_ref[rows, :] = dz_ref[rows, :] * bg_ref[rows, :]
            pad_d[PADR:PADR + length, :] = dconv_ref[rows, :]
            for kk in range(SC_K):
                dw3_ref[kk:kk + 1, :] += _tap_grad(pad_x, dconv_ref, off, length, PADR + kk - SC_K // 2)
            for c0 in range(0, length, CONV_CH):
                ch = slice(off + c0, off + c0 + CONV_CH)
                c3 = _conv_chunk(pad_x, w3_ref, SC_K, c0, lambda kk: PADR + kk - SC_K // 2)
                dbg_ref[ch, :] = (dz_ref[ch, :] * c3).astype(BF16)
                dcx = _conv_chunk(pad_d, w3_ref, SC_K, c0, lambda kk: PADR + SC_K // 2 - kk)
                dcg_ref[ch, :] = (dcx * xs_ref[ch, :]).astype(BF16)
                dxs_ref[ch, :] = (dcx * cg_ref[ch, :]).astype(BF16)

    slab = jax.ShapeDtypeStruct((NROW, CW), BF16)
    return _pcall(body, name=name,
                  out_shape=(slab,) * 5 + (jax.ShapeDtypeStruct((CONF_K, CW), F32), jax.ShapeDtypeStruct((1, CW), F32),
                                           jax.ShapeDtypeStruct((SC_K, CW), F32)),
                  grid=(CB, BL),
                  in_specs=[_p_block(OFF_CONF), _p_block(OFF_CONF + CW), _p_block(OFF_SC), _p_block(OFF_SC + CW),
                            _p_block(OFF_SC + 2 * CW), _taps(CONF_K), _taps(SC_K), _conv_io(CW), _conv_io(CW)],
                  out_specs=(_conv_io(CW),) * 5 + (_taps(CONF_K), _taps(1), _taps(SC_K)),
                  scratch=[pltpu.VMEM((SEQ + 2 * PADR, LANE), F32), pltpu.VMEM((SEQ + 2 * PADR, LANE), F32),
                           pltpu.VMEM((RE, LANE), F32)],
                  sem=("parallel", "arbitrary"))(p_main, p_main, p_main, p_main, p_main, wdw, w3, dyc, dz)


def silu_rows(x, name):
    def body(x_ref, o_ref):
        o_ref[...] = x_ref[...] * _sigmoid(x_ref[...])

    return _pcall(body, name=name, out_shape=jax.ShapeDtypeStruct(x.shape, F32))(x)


def silu_rows_bwd(x, dcs, name):
    def body(x_ref, d_ref, o_ref):
        x = x_ref[...]
        sg = _sigmoid(x)
        tot = d_ref[0]
        for i in range(1, DEPTH):
            tot += d_ref[i]
        o_ref[...] = tot * (sg * (1.0 + x * (1.0 - sg)))

    return _pcall(body, name=name, out_shape=jax.ShapeDtypeStruct(x.shape, F32))(x, dcs)


def dmod_assemble(parts, name):
    def body(p_ref, dm_ref, db_ref):
        row = lax.broadcasted_iota(jnp.int32, (8, NMOD * D), 0)
        dm = jnp.zeros((8, NMOD * D), F32)
        db = jnp.zeros((1, NMOD * D), F32)
        for s in range(2 * BL):
            target = BL if s % 2 == 0 else s // 2
            part = p_ref[s:s + 1, :]
            dm += jnp.where(row == target, part, 0.0)
            db += part
        dm_ref[...] = dm
        db_ref[...] = db

    return _pcall(body, name=name, out_shape=(jax.ShapeDtypeStruct((8, NMOD * D), F32),
                                              jax.ShapeDtypeStruct((1, NMOD * D), F32)))(parts)


def sum_leading(x, name):
    n = x.shape[0]
    tr = _pick(x.shape[1], (256, 32, 8))

    def body(x_ref, o_ref):
        tot = x_ref[0].astype(F32)
        for i in range(1, n):
            tot += x_ref[i].astype(F32)
        o_ref[...] = tot

    return _pcall(body, name=name, out_shape=jax.ShapeDtypeStruct(x.shape[1:], F32), grid=(x.shape[1] // tr,),
                  in_specs=[pl.BlockSpec((n, tr, x.shape[2]), lambda i: (0, i, 0))],
                  out_specs=pl.BlockSpec((tr, x.shape[2]), lambda i: (i, 0)), sem=("parallel",))(x)


SLAB_ROWS = (256, 176, 128, 64, 8)


def _prefetch_call(body, name, out_shape, grid, in_specs, out_specs, sem, scalars, *args):
    spec = pltpu.PrefetchScalarGridSpec(num_scalar_prefetch=len(scalars), grid=grid, in_specs=in_specs,
                                        out_specs=out_specs)
    return pl.pallas_call(body, name=name, out_shape=out_shape, grid_spec=spec,
                          compiler_params=pltpu.CompilerParams(dimension_semantics=sem,
                                                               vmem_limit_bytes=VMEM_LIMIT))(*scalars, *args)


def cast_layers(w, chip, name):
    depth, r, c = w.shape
    tr = _pick(r, SLAB_ROWS)

    def body(s_ref, w_ref, *o_refs):
        for l in range(depth):
            o_refs[l][...] = w_ref[l].astype(BF16)

    slab = pl.BlockSpec((None, tr, c), lambda i, s: (s[0], i, 0))
    return _prefetch_call(body, name, (jax.ShapeDtypeStruct((NCHIP, r, c), BF16),) * depth, (r // tr,),
                          [pl.BlockSpec((depth, tr, c), lambda i, s: (0, i, 0))], (slab,) * depth,
                          ("parallel",), (chip,), w)


def rs_add(g, other, core, chip, name):
    _, r, c = g.shape
    rh = r // 2
    tr = _pick(rh, SLAB_ROWS)
    nblk = rh // tr

    def body(core_ref, chip_ref, g_ref, o_ref, send_ref, arr_ref):
        k = pl.program_id(1)
        tot = (g_ref[...].astype(F32) + o_ref[...].astype(F32)).astype(BF16)
        send_ref[...] = tot

        @pl.when(k == chip_ref[0])
        def _():
            arr_ref[...] = tot

    blk = (None, tr, c)
    return _prefetch_call(
        body, name, (jax.ShapeDtypeStruct(other.shape, BF16), jax.ShapeDtypeStruct(g.shape, BF16)), (nblk, NCHIP),
        [pl.BlockSpec(blk, lambda i, k, cr, ch: (k, cr[0] * nblk + i, 0)), pl.BlockSpec(blk, lambda i, k, cr, ch: (k, i, 0))],
        (pl.BlockSpec(blk, lambda i, k, cr, ch: (k, i, 0)),
         pl.BlockSpec(blk, lambda i, k, cr, ch: (ch[0], cr[0] * nblk + i, 0))),
        ("parallel", "arbitrary"), (core, chip), g, other)


def adamw_layer(w, arr, m, v, layer, prev, name):
    depth, r, c = w.shape
    tr = _pick(r, SLAB_ROWS)
    c1 = 1.0 / (1.0 - ADAM_B1 ** ADAM_STEP)
    c2 = 1.0 / (1.0 - ADAM_B2 ** ADAM_STEP)

    def body(w_ref, a_ref, m_ref, v_ref, p0, p1, p2, p3, g_ref, d_ref, mo_ref, vo_ref):
        gv = a_ref[0].astype(F32)
        for k in range(1, NCHIP):
            gv += a_ref[k].astype(F32)
        mn = ADAM_B1 * m_ref[...] + (1.0 - ADAM_B1) * gv
        vn = ADAM_B2 * v_ref[...] + (1.0 - ADAM_B2) * (gv * gv)
        g_ref[...] = gv
        d_ref[...] = -ADAM_LR * ((mn * c1) / (jnp.sqrt(vn * c2) + ADAM_EPS) + ADAM_WD * w_ref[...])
        mo_ref[...] = mn
        vo_ref[...] = vn

    spec = pl.BlockSpec((None, tr, c), lambda i: (layer, i, 0))
    sds = jax.ShapeDtypeStruct(w.shape, F32)
    return pl.pallas_call(body, name=name, out_shape=(sds,) * 4, grid=(r // tr,),
                          in_specs=[spec, pl.BlockSpec((NCHIP, tr, c), lambda i: (0, i, 0)), spec, spec] + [ANY] * 4,
                          out_specs=(spec,) * 4, input_output_aliases={4: 0, 5: 1, 6: 2, 7: 3},
                          compiler_params=pltpu.CompilerParams(dimension_semantics=("parallel",),
                                                               vmem_limit_bytes=VMEM_LIMIT))(w, arr, m, v, *prev)


def adamw(w, g, m, v, name):
    rows, cols = w.shape
    tr = _pick(rows, (256, 248, 128, 8))
    c1 = 1.0 / (1.0 - ADAM_B1 ** ADAM_STEP)
    c2 = 1.0 / (1.0 - ADAM_B2 ** ADAM_STEP)

    def body(w_ref, g_ref, m_ref, v_ref, d_ref, mo_ref, vo_ref):
        gv = g_ref[...]
        mn = ADAM_B1 * m_ref[...] + (1.0 - ADAM_B1) * gv
        vn = ADAM_B2 * v_ref[...] + (1.0 - ADAM_B2) * (gv * gv)
        d_ref[...] = -ADAM_LR * ((mn * c1) / (jnp.sqrt(vn * c2) + ADAM_EPS) + ADAM_WD * w_ref[...])
        mo_ref[...] = mn
        vo_ref[...] = vn

    spec = pl.BlockSpec((tr, cols), lambda i: (i, 0))
    sds = jax.ShapeDtypeStruct((rows, cols), F32)
    return _pcall(body, name=name, out_shape=(sds, sds, sds), grid=(rows // tr,), in_specs=[spec] * 4,
                  out_specs=(spec, spec, spec), sem=("parallel",))(w, g, m, v)


def _place():
    return lax.axis_index("x"), lax.axis_index("y"), lax.axis_index("c")


def _other_chips(x, y):
    return [(1 - x, y), (x, 1 - y), (1 - x, 1 - y)]


def _comm_call(body, name, out_shape, n_in, nsem):
    return pl.pallas_call(body, name=name, out_shape=out_shape, in_specs=[ANY] * n_in,
                          out_specs=jax.tree.map(lambda _: ANY, out_shape),
                          scratch_shapes=[pltpu.SemaphoreType.DMA((nsem,)), pltpu.SemaphoreType.DMA((nsem,)),
                                          pltpu.SemaphoreType.DMA])


def all_gather8(block, name):
    def body(x_ref, out_ref, send_sems, recv_sems, local_sem):
        x, y, c = _place()
        me, sibling = (x, y, c), (x, y, 1 - c)
        chips = _other_chips(x, y)

        def slot(px, py, pc):
            return out_ref.at[4 * px + 2 * py + pc]

        def copy(k, blk, to, src=None):
            return pltpu.make_async_remote_copy(src_ref=slot(*blk) if src is None else src, dst_ref=slot(*blk),
                                                send_sem=send_sems.at[k], recv_sem=recv_sems.at[k],
                                                device_id=to, device_id_type=MESH)

        mine = pltpu.make_async_copy(x_ref, slot(*me), local_sem)
        mine.start()
        first = [copy(0, me, sibling, src=x_ref)]
        first += [copy(1 + j, me, (*chip, c), src=x_ref) for j, chip in enumerate(chips)]
        for cp in first:
            cp.start()
        passed = [copy(4 + j, (*chip, c), sibling) for j, chip in enumerate(chips)]
        for j, chip in enumerate(chips):
            copy(1 + j, (*chip, c), me).wait_recv()
            passed[j].start()
        copy(0, sibling, me).wait_recv()
        for j, chip in enumerate(chips):
            copy(4 + j, (*chip, 1 - c), me).wait_recv()
        for cp in first + passed:
            cp.wait_send()
        mine.wait()

    return _comm_call(body, name, jax.ShapeDtypeStruct((8,) + block.shape, block.dtype), 1, 7)(block)


def _remote(src, dst, send_sems, recv_sems, k, to):
    return pltpu.make_async_remote_copy(src_ref=src, dst_ref=dst, send_sem=send_sems.at[k], recv_sem=recv_sems.at[k],
                                        device_id=to, device_id_type=MESH)


def _half(ref, slot, core):
    rh = ref.shape[1] // 2
    return ref.at[slot, pl.ds(core * rh, rh)]


def _all_slots_half(ref, core):
    rh = ref.shape[1] // 2
    return ref.at[:, pl.ds(core * rh, rh)]


def gather_ici(bufs):
    def program(ro, rw, new, ss, rs):
        x, y, c = _place()
        own = 2 * x + y
        starts, arrivals = [], []
        for w, ref in enumerate(rw):
            for j, chip in enumerate(_other_chips(x, y)):
                starts.append(_remote(_half(ref, own, c), _half(ref, own, c), ss, rs, 3 * w + j, (*chip, c)))
                arrivals.append(_remote(_half(ref, own, c), _half(ref, 2 * chip[0] + chip[1], c), ss, rs, 3 * w + j,
                                        (*chip, c)))
        return starts, arrivals

    return CommSpec((), tuple(bufs), (), 3 * len(bufs), program)


def gather_d2d(bufs):
    def program(ro, rw, new, ss, rs):
        x, y, c = _place()
        starts, arrivals = [], []
        for w, ref in enumerate(rw):
            for j, chip in enumerate(_other_chips(x, y)):
                slot = 2 * chip[0] + chip[1]
                starts.append(_remote(_half(ref, slot, c), _half(ref, slot, c), ss, rs, 3 * w + j, (x, y, 1 - c)))
                arrivals.append(_remote(_half(ref, slot, c), _half(ref, slot, 1 - c), ss, rs, 3 * w + j, (x, y, 1 - c)))
        return starts, arrivals

    return CommSpec((), tuple(bufs), (), 3 * len(bufs), program)


def rs_swap(grads):
    def program(ro, rw, new, ss, rs):
        x, y, c = _place()
        copies = [_remote(_all_slots_half(g, 1 - c), new[w], ss, rs, w, (x, y, 1 - c)) for w, g in enumerate(ro)]
        return copies, copies

    shapes = tuple(jax.ShapeDtypeStruct((NCHIP, g.shape[1] // 2, g.shape[2]), g.dtype) for g in grads)
    return CommSpec(tuple(grads), (), shapes, len(grads), program)


def rs_ici(sends, arrs):
    def program(ro, rw, new, ss, rs):
        x, y, c = _place()
        own = 2 * x + y
        starts, arrivals = [], []
        for w, (snd, arr) in enumerate(zip(ro, rw)):
            for j, chip in enumerate(_other_chips(x, y)):
                slot = 2 * chip[0] + chip[1]
                starts.append(_remote(snd.at[slot], _half(arr, own, c), ss, rs, 3 * w + j, (*chip, c)))
                arrivals.append(_remote(snd.at[slot], _half(arr, slot, c), ss, rs, 3 * w + j, (*chip, c)))
        return starts, arrivals

    return CommSpec(tuple(sends), tuple(arrs), (), 3 * len(sends), program)


def rs_d2d(arrs):
    def program(ro, rw, new, ss, rs):
        x, y, c = _place()
        starts = [_remote(_all_slots_half(a, c), _all_slots_half(a, c), ss, rs, w, (x, y, 1 - c)) for w, a in enumerate(rw)]
        arrivals = [_remote(_all_slots_half(a, c), _all_slots_half(a, 1 - c), ss, rs, w, (x, y, 1 - c))
                    for w, a in enumerate(rw)]
        return starts, arrivals

    return CommSpec((), tuple(arrs), (), len(arrs), program)


PACK_COLS = 1024
MATMUL_W = ("w_ada", "w_in", "w_attn_o", "w_conf_out", "w_sc_out", "w_mix_out", "w_ffn_in", "w_ffn_out")
ROW_SPLIT = ("w_mix_out", "w_ffn_out")
CONV_W = ("conf_dw_w", "sc_dw_w")
SMALL = ("c_ctx", "b_ada", "q_norm", "k_norm", "conf_dw_b", "conf_ln_g", "conf_ln_b", "conf_dw_w", "sc_dw_w")


def _pack_rows(arrays, row_multiple):
    flat = jnp.concatenate([a.reshape(-1) for a in arrays])
    rows = -(-flat.shape[0] // PACK_COLS)
    rows = -(-rows // row_multiple) * row_multiple
    flat = jnp.pad(flat, (0, rows * PACK_COLS - flat.shape[0]))
    return flat.reshape(rows, PACK_COLS)


def _unpack(flat2d, shapes):
    flat = flat2d.reshape(-1)
    out, pos = [], 0
    for shp in shapes:
        n = 1
        for s in shp:
            n *= s
        out.append(flat[pos:pos + n].reshape(shp))
        pos += n
    return out


def _cols_joined(stacked_layer):
    nchip, r, c = stacked_layer.shape
    return jnp.transpose(stacked_layer, (1, 0, 2)).reshape(r, nchip * c)


def _cols_split(full):
    r, cols = full.shape
    return jnp.transpose(full.reshape(r, NCHIP, cols // NCHIP), (1, 0, 2))


def _rope_tables():
    rows = SEQ // GRID_W
    r_ids = jnp.repeat(jnp.arange(rows, dtype=F32), GRID_W)
    c_ids = jnp.tile(jnp.arange(GRID_W, dtype=F32), rows)
    freqs = ROPE_THETA ** (-jnp.arange(0, HD // 2, 2, dtype=F32) / (HD // 2))
    ang_r, ang_c = r_ids[:, None] * freqs, c_ids[:, None] * freqs
    cos_h = jnp.concatenate([jnp.cos(ang_r), jnp.cos(ang_r), jnp.cos(ang_c), jnp.cos(ang_c)], axis=1)
    sin_h = jnp.concatenate([-jnp.sin(ang_r), jnp.sin(ang_r), -jnp.sin(ang_c), jnp.sin(ang_c)], axis=1)
    cos_t = jnp.concatenate([jnp.ones((CTX, HD), F32), cos_h], axis=0)
    sin_t = jnp.concatenate([jnp.zeros((CTX, HD), F32), sin_h], axis=0)
    return jnp.tile(cos_t, (1, LANE // HD)), jnp.tile(sin_t, (1, LANE // HD))


def _group_matrix():
    gid = jnp.arange(LANE) // HD
    return jnp.where(gid[:, None] == gid[None, :], 1.0 / HD, 0.0).astype(F32)


def _layer_weights(bufs, small, i):
    b = dict(zip(MATMUL_W, bufs))
    wi = _cols_joined(b["w_in"])

    def rows_joined(a):
        return a.reshape(a.shape[0] * a.shape[1], a.shape[2])

    return dict(
        w_ada=(b["w_ada"], "cols"), wi_main=(wi[:, :OFF_GATE], "mat"), wi_gate=(wi[:, OFF_GATE:], "mat"),
        w_attn_o=(_cols_joined(b["w_attn_o"]), "mat"), w_conf_out=(_cols_joined(b["w_conf_out"]), "mat"),
        w_sc_out=(_cols_joined(b["w_sc_out"]), "mat"), w_ffn_in=(b["w_ffn_in"], "cols"),
        w_mix_out=(rows_joined(b["w_mix_out"]), "mat"), w_ffn_out=(rows_joined(b["w_ffn_out"]), "mat"),
        conf_dw_w=small["conf_dw_w"][i], sc_dw_w=small["sc_dw_w"][i], conf_dw_b=small["conf_dw_b"][i][None],
        conf_ln_g=small["conf_ln_g"][i][None], conf_ln_b=small["conf_ln_b"][i][None],
        gq=jnp.tile(small["q_norm"][i], LANE // HD)[None], gk=jnp.tile(small["k_norm"][i], LANE // HD)[None])


def _layer_fwd(i, xs, h, mods, w, tabs, next_bufs, next_layer):
    cos_t, sin_t, g_mat = tabs
    n = f"l{i}_"
    sv = {"x_in": xs, "mods": mods, "h": h}
    sv["p_main"] = mm_nn(sv["h"], w["wi_main"], name=n + "p_main")
    sv["q"], sv["k"], sv["v"] = qkv_fwd(sv["p_main"], cos_t, sin_t, g_mat, w["gq"], w["gk"], n + "qkv")
    if next_bufs is None:
        sv["o"] = attn_fwd(sv["q"], sv["k"], sv["v"], n + "attn")
    else:
        sv["o"], next_bufs, _ = attn_fwd(sv["q"], sv["k"], sv["v"], n + "attn", comm=gather_ici(next_bufs))
    sv["yc"], sv["z"] = conv_fwd(sv["p_main"], w["conf_dw_w"], w["conf_dw_b"], w["sc_dw_w"], n + "conv")
    sv["hs"] = ln_silu_fwd(sv["yc"], w["conf_ln_g"], w["conf_ln_b"], n + "ln_silu")
    sv["merged"], sv["gates"], sv["ys"] = gate_mm_fwd(sv["h"], w["wi_gate"][0], sv["o"], sv["hs"], sv["z"],
                                                      w["w_attn_o"][0], w["w_conf_out"][0], w["w_sc_out"][0],
                                                      n + "gate_merge")
    sv["mixed"] = mm_nn(sv["merged"], w["w_mix_out"], name=n + "mix")
    sv["x1"], sv["h2"] = resid_norm_fwd(xs, sv["mixed"], mods, 2, mods, 3, 4, n + "resid1_norm2")
    if next_bufs is None:
        sv["f"], sv["u2"] = ffn_in_swiglu(sv["h2"], w["w_ffn_in"][0], n + "ffn_in")
    else:
        (sv["f"], sv["u2"]), next_bufs, _ = ffn_in_swiglu(sv["h2"], w["w_ffn_in"][0], n + "ffn_in",
                                                          comm=gather_d2d(next_bufs))
    sv["of"] = mm_nn(sv["f"], w["w_ffn_out"], name=n + "ffn_out")
    if next_layer is None:
        return gate_resid_fwd(sv["x1"], sv["of"], mods, 5, n + "resid2"), None, sv
    w_next, mods_next = next_layer(next_bufs)
    x2, h_next = resid_norm_fwd(sv["x1"], sv["of"], mods, 5, mods_next, 0, 1, n + "resid2_norm1")
    return x2, (h_next, w_next, mods_next), sv


def _layer_bwd(i, dx2, dof, dm5, sv, w, tabs, cs, pending, ids, below):
    cos_t, sin_t, g_mat = tabs
    n = f"l{i}b_"
    mods = sv["mods"]
    g = {}
    if pending is None:
        du = d_f_swiglu(dof, w["w_ffn_out"][0], sv["u2"], n + "d_f")
    else:
        du, _, swapped = d_f_swiglu(dof, w["w_ffn_out"][0], sv["u2"], n + "d_f", comm=rs_swap(pending))
        sends, arrs = zip(*[rs_add(g_, s_, ids[0], ids[1], f"{n}rs_add_{k}")
                            for k, g_, s_ in zip(MATMUL_W, pending, swapped)])
    g["w_ffn_out"] = mm_tn(sv["f"], dof, out_dtype=BF16, name=n + "dw_ffn_out").reshape(NCHIP, FH // NCHIP, D)
    dh2 = mm_nt(du, w["w_ffn_in"], name=n + "d_h2")
    g["w_ffn_in"] = mm_tn(sv["h2"], du, cols=True, out_dtype=BF16, name=n + "dw_ffn_in")
    dx1, dm34, dmixed, dm2 = norm_resid_bwd(sv["x1"], mods, dh2, dx2, 4, sv["mixed"], mods, 2, n + "norm2_resid1")
    dya, dyb, dys, dp_gate = d_merged_gate(dmixed, w["w_mix_out"][0], sv["gates"], sv["ys"], n + "d_merged")
    g["w_mix_out"] = mm_tn(sv["merged"], dmixed, out_dtype=BF16, name=n + "dw_mix").reshape(NCHIP, D // NCHIP, D)
    do = mm_nt(dya, w["w_attn_o"], out_dtype=BF16, name=n + "d_o")
    g["w_attn_o"] = _cols_split(mm_tn(sv["o"], dya, out_dtype=BF16, name=n + "dw_attn_o"))
    dhs = mm_nt(dyb, w["w_conf_out"], name=n + "d_hs")
    g["w_conf_out"] = _cols_split(mm_tn(sv["hs"], dyb, out_dtype=BF16, name=n + "dw_conf_out"))
    dz = mm_nt(dys, w["w_sc_out"], name=n + "d_z")
    g["w_sc_out"] = _cols_split(mm_tn(sv["z"], dys, out_dtype=BF16, name=n + "dw_sc_out"))
    dyc, g["conf_ln_g"], g["conf_ln_b"] = ln_silu_bwd(sv["yc"], w["conf_ln_g"], w["conf_ln_b"], dhs, n + "ln_silu")
    da, dg, dbg, dcg, dxs, g["conf_dw_w"], g["conf_dw_b"], g["sc_dw_w"] = conv_bwd(
        sv["p_main"], w["conf_dw_w"], w["sc_dw_w"], dyc, dz, n + "conv")
    done = None
    if pending is None:
        dq, dk, dv = attn_bwd(sv["q"], sv["k"], sv["v"], sv["o"], do, n + "attn")
        dp_qkv, dgqk = qkv_bwd(sv["p_main"], cos_t, sin_t, g_mat, w["gq"], w["gk"], dq, dk, dv, n + "qkv")
    else:
        (dq, dk, dv), arrs, _ = attn_bwd(sv["q"], sv["k"], sv["v"], sv["o"], do, n + "attn", comm=rs_ici(sends, arrs))
        (dp_qkv, dgqk), done, _ = qkv_bwd(sv["p_main"], cos_t, sin_t, g_mat, w["gq"], w["gk"], dq, dk, dv, n + "qkv",
                                          comm=rs_d2d(arrs))
    dp_main = jnp.concatenate([dp_qkv, da, dg, dbg, dcg, dxs], axis=1)
    dh = mm_nt(dp_main, w["wi_main"], name=n + "d_h_main")
    dh = mm_nt(dp_gate, w["wi_gate"], acc=dh, name=n + "d_h_gate")
    g["w_in"] = _cols_split(jnp.concatenate([mm_tn(sv["h"], dp_main, out_dtype=BF16, name=n + "dw_in_main"),
                                             mm_tn(sv["h"], dp_gate, out_dtype=BF16, name=n + "dw_in_gate")], axis=1))
    if below is None:
        dx_in, dm01 = norm_mod_bwd(sv["x_in"], mods, dh, dx1, 1, n + "norm1")
        dof_below = dm5_below = None
    else:
        dx_in, dm01, dof_below, dm5_below = norm_resid_bwd(sv["x_in"], mods, dh, dx1, 1, below["of"], below["mods"], 5,
                                                           n + "norm1_resid2")
    parts = jnp.concatenate([dm01, dm2, dm34, dm5], axis=2).reshape(2 * BL, NMOD * D)
    dmod, g["b_ada"] = dmod_assemble(parts, n + "dmod")
    g["w_ada"] = mm_tn(cs, dmod, cols=True, out_dtype=BF16, name=n + "dw_ada")
    g["dcs"] = mm_nt(dmod, w["w_ada"], name=n + "d_cs")
    g["q_norm"] = dgqk[0, :AW].reshape(NQ, HD).sum(axis=0)
    g["k_norm"] = dgqk[0, OFF_K:OFF_K + KVW].reshape(NKV, HD).sum(axis=0)
    return dx_in, dof_below, dm5_below, g, done


def kernel(x, c, ctx, c_ctx, w_ada, b_ada, w_in, q_norm, k_norm, w_attn_o, conf_dw_w, conf_dw_b, conf_ln_g, conf_ln_b, w_conf_out, sc_dw_w, w_sc_out, w_mix_out, w_ffn_in, w_ffn_out, loss_target, m_c_ctx, m_w_ada, m_b_ada, m_w_in, m_q_norm, m_k_norm, m_w_attn_o, m_conf_dw_w, m_conf_dw_b, m_conf_ln_g, m_conf_ln_b, m_w_conf_out, m_sc_dw_w, m_w_sc_out, m_w_mix_out, m_w_ffn_in, m_w_ffn_out, v_c_ctx, v_w_ada, v_b_ada, v_w_in, v_q_norm, v_k_norm, v_w_attn_o, v_conf_dw_w, v_conf_dw_b, v_conf_ln_g, v_conf_ln_b, v_w_conf_out, v_sc_dw_w, v_w_sc_out, v_w_mix_out, v_w_ffn_in, v_w_ffn_out):
    local = dict(c_ctx=c_ctx, w_ada=w_ada, b_ada=b_ada, w_in=w_in, q_norm=q_norm, k_norm=k_norm, w_attn_o=w_attn_o,
                 conf_dw_w=conf_dw_w, conf_dw_b=conf_dw_b, conf_ln_g=conf_ln_g, conf_ln_b=conf_ln_b,
                 w_conf_out=w_conf_out, sc_dw_w=sc_dw_w, w_sc_out=w_sc_out, w_mix_out=w_mix_out, w_ffn_in=w_ffn_in,
                 w_ffn_out=w_ffn_out)
    mom_m = dict(c_ctx=m_c_ctx, w_ada=m_w_ada, b_ada=m_b_ada, w_in=m_w_in, q_norm=m_q_norm, k_norm=m_k_norm,
                 w_attn_o=m_w_attn_o, conf_dw_w=m_conf_dw_w, conf_dw_b=m_conf_dw_b, conf_ln_g=m_conf_ln_g,
                 conf_ln_b=m_conf_ln_b, w_conf_out=m_w_conf_out, sc_dw_w=m_sc_dw_w, w_sc_out=m_w_sc_out,
                 w_mix_out=m_w_mix_out, w_ffn_in=m_w_ffn_in, w_ffn_out=m_w_ffn_out)
    mom_v = dict(c_ctx=v_c_ctx, w_ada=v_w_ada, b_ada=v_b_ada, w_in=v_w_in, q_norm=v_q_norm, k_norm=v_k_norm,
                 w_attn_o=v_w_attn_o, conf_dw_w=v_conf_dw_w, conf_dw_b=v_conf_dw_b, conf_ln_g=v_conf_ln_g,
                 conf_ln_b=v_conf_ln_b, w_conf_out=v_w_conf_out, sc_dw_w=v_sc_dw_w, w_sc_out=v_w_sc_out,
                 w_mix_out=v_w_mix_out, w_ffn_in=v_w_ffn_in, w_ffn_out=v_w_ffn_out)
    order = ("c_ctx", "w_ada", "b_ada", "w_in", "q_norm", "k_norm", "w_attn_o", "conf_dw_w", "conf_dw_b", "conf_ln_g",
             "conf_ln_b", "w_conf_out", "sc_dw_w", "w_sc_out", "w_mix_out", "w_ffn_in", "w_ffn_out")
    core = lax.axis_index("c").astype(jnp.int32)
    chip = (2 * lax.axis_index("x") + lax.axis_index("y")).astype(jnp.int32)

    own = [cast_layers(local[k], chip.reshape(1), "cast_" + k) for k in MATMUL_W]
    layer_bufs = [[own[w][l] for w in range(len(MATMUL_W))] for l in range(DEPTH)]
    conv_shapes = [local[k].shape for k in CONV_W]
    conv_all = all_gather8(_pack_rows([local[k] for k in CONV_W], 8), "gather_conv_taps")
    per_chip = [_unpack(conv_all[2 * s], conv_shapes) for s in range(NCHIP)]
    small = dict(b_ada=b_ada, q_norm=q_norm, k_norm=k_norm, conf_dw_b=conf_dw_b, conf_ln_g=conf_ln_g, conf_ln_b=conf_ln_b)
    for i, k in enumerate(CONV_W):
        small[k] = jnp.concatenate([per_chip[s][i] for s in range(NCHIP)], axis=2)

    loss_local, grad_x, sums, small_g = local_step(x, c, ctx, c_ctx, layer_bufs, small, loss_target,
                                                   ids=(core.reshape(1), chip.reshape(1)))
    loss = lax.psum(loss_local, ("x", "y", "c"))

    small_shapes = [small_g[k].shape for k in SMALL]
    small_sum = sum_leading(all_gather8(_pack_rows([small_g[k] for k in SMALL], 8), "gather_small_grads"), "small_sum")
    small_g = dict(zip(SMALL, _unpack(small_sum, small_shapes)))
    for k in CONV_W:
        width = local[k].shape[2]
        small_g[k] = lax.dynamic_slice_in_dim(small_g[k], chip * width, width, axis=2)

    grad, delta, new_m, new_v = {}, {}, {}, {}
    for k in order:
        if k in MATMUL_W:
            outs = [lax.empty(local[k].shape, F32) for _ in range(4)]
            for l in range(DEPTH):
                outs = adamw_layer(local[k], sums[l][MATMUL_W.index(k)], mom_m[k], mom_v[k], l, outs, f"adamw_{k}_{l}")
            grad[k], delta[k], new_m[k], new_v[k] = outs
            continue
        shp = local[k].shape
        view = (1, shp[0]) if len(shp) == 1 else (-1, shp[-1])
        d_, m_, v_ = adamw(local[k].reshape(view), small_g[k].reshape(view), mom_m[k].reshape(view),
                           mom_v[k].reshape(view), "adamw_" + k)
        grad[k], delta[k], new_m[k], new_v[k] = small_g[k], d_.reshape(shp), m_.reshape(shp), v_.reshape(shp)
    return (loss, grad_x, *[grad[k] for k in order], *[delta[k] for k in order], *[new_m[k] for k in order],
            *[new_v[k] for k in order])


def local_step(x, c, ctx, c_ctx, layer_bufs, small, loss_target, ids=None):
    tabs = _rope_tables() + (_group_matrix(),)
    distributed = ids is not None
    layer_bufs = list(layer_bufs)
    if distributed:
        layer_bufs[0], _ = comm_only("gather0_ici", gather_ici(layer_bufs[0]))
        layer_bufs[0], _ = comm_only("gather0_d2d", gather_d2d(layer_bufs[0]))

    cin = jnp.concatenate([c, c_ctx[None], jnp.zeros((8 - BL - 1, D), F32)], axis=0)
    cs = silu_rows(cin, "silu_c")
    xs = jnp.concatenate([ctx, x], axis=1).reshape(NROW, D)
    saved, layer_w = [], []

    def make_layer(i, bufs):
        w = _layer_weights(bufs, small, i)
        return w, mm_nn(cs, w["w_ada"], bias=small["b_ada"][i][None], name=f"l{i}_mod").reshape(8, 1, NMOD * D)

    w, mods = make_layer(0, layer_bufs[0])
    h = norm_mod_fwd(xs, mods, 0, 1, "l0_norm1")
    for i in range(DEPTH):
        last = i == DEPTH - 1

        def next_layer(bufs, i=i):
            if bufs is not None:
                layer_bufs[i + 1] = bufs
            return make_layer(i + 1, layer_bufs[i + 1])

        ahead = layer_bufs[i + 1] if distributed and not last else None
        xs, following, sv = _layer_fwd(i, xs, h, mods, w, tabs, ahead, None if last else next_layer)
        saved.append(sv)
        layer_w.append(w)
        if following is not None:
            h, w, mods = following
    dxs, loss_lanes = loss_fwd_bwd(xs, loss_target.reshape(BL * SEQ, D), "loss")
    loss_local = 0.5 * jnp.sum(loss_lanes) / D

    grads, sums = [None] * DEPTH, [None] * DEPTH
    pending = None
    dof, dm5 = gate_resid_bwd(dxs, saved[-1]["of"], saved[-1]["mods"], 5, "top_resid2")
    for i in reversed(range(DEPTH)):
        dxs, dof, dm5, grads[i], done = _layer_bwd(i, dxs, dof, dm5, saved[i], layer_w[i], tabs, cs, pending, ids,
                                                   saved[i - 1] if i > 0 else None)
        partial = [grads[i][k] for k in MATMUL_W]
        if distributed:
            if pending is not None:
                sums[i + 1] = done
            pending = partial
        else:
            sums[i] = partial
    if distributed:
        _, swapped = comm_only("rs0_swap", rs_swap(pending))
        sends, arrs = zip(*[rs_add(g_, s_, ids[0], ids[1], "rs0_add_" + k) for k, g_, s_ in zip(MATMUL_W, pending, swapped)])
        arrs, _ = comm_only("rs0_ici", rs_ici(sends, arrs))
        sums[0], _ = comm_only("rs0_d2d", rs_d2d(arrs))
    grad_x = dxs.reshape(BL, RE, D)[:, CTX:, :]
    dcin = silu_rows_bwd(cin, jnp.stack([grads[i]["dcs"] for i in range(DEPTH)]), "silu_c_bwd")

    def stack(key):
        return jnp.stack([grads[i][key] for i in range(DEPTH)])

    small_g = dict(c_ctx=dcin[BL], b_ada=stack("b_ada").reshape(DEPTH, NMOD * D), q_norm=stack("q_norm"),
                   k_norm=stack("k_norm"), conf_dw_b=stack("conf_dw_b").reshape(DEPTH, CW),
                   conf_ln_g=stack("conf_ln_g").reshape(DEPTH, CW), conf_ln_b=stack("conf_ln_b").reshape(DEPTH, CW),
                   conf_dw_w=stack("conf_dw_w"), sc_dw_w=stack("sc_dw_w"))
    return loss_local, grad_x, sums, small_g
```

```python
import functools
from typing import Any, Callable, NamedTuple, Sequence

import jax
import jax.numpy as jnp
from jax import lax
from jax.experimental import pallas as pl
from jax.experimental.pallas import tpu as pltpu

F32, BF16 = jnp.float32, jnp.bfloat16
HIGHEST = lax.Precision.HIGHEST

D = 1024
SEQ = 2048
CTX = 256
DEPTH = 4
BL = 4
GRID_W = 64
HD = 64
NQ = 8
NKV = 2
AW = NQ * HD
KVW = NKV * HD
CW = D // 2
CONF_K = 31
SC_K = 3
NMOD = 6
FH = -(-8 * D // (3 * 256)) * 256
EPS = 1e-6
ROPE_THETA = 10000.0
ATTN_SCALE = HD ** -0.5
OFF_K = AW
OFF_V = OFF_K + KVW
OFF_CONF = OFF_V + KVW
OFF_SC = OFF_CONF + 2 * CW
OFF_GATE = OFF_SC + 3 * CW
IN_W = OFF_GATE + 3 * D
QKVW = OFF_CONF
NCHIP = 4

ADAM_LR, ADAM_B1, ADAM_B2, ADAM_EPS, ADAM_WD, ADAM_STEP = 0.001, 0.9, 0.999, 1e-08, 0.01, 10

TM = CTX
RE = CTX + SEQ
TPE = RE // TM
NROW = BL * RE
NT = NROW // TM
LANE = 128
CB = CW // LANE
CONV_CH = 128
PADR = 16
VMEM_LIMIT = 52 * 1024 * 1024

MESH = pl.DeviceIdType.MESH
ANY = pl.BlockSpec(memory_space=pl.ANY)


class CommSpec(NamedTuple):
    ro: Sequence[Any]
    rw: Sequence[Any]
    new: Sequence[Any]
    nsem: int
    program: Callable


def _pcall(body, *, name, out_shape, grid=(), in_specs=None, out_specs=None, scratch=(), sem=None, comm=None):
    if not grid:
        return pl.pallas_call(body, name=name, out_shape=out_shape)
    if comm is None:
        params = pltpu.CompilerParams(dimension_semantics=sem, vmem_limit_bytes=VMEM_LIMIT)
        return pl.pallas_call(body, name=name, out_shape=out_shape, grid=grid, in_specs=in_specs, out_specs=out_specs,
                              scratch_shapes=list(scratch), compiler_params=params)

    single = not isinstance(out_shape, (tuple, list))
    out_shapes = (out_shape,) if single else tuple(out_shape)
    out_specs_t = (out_specs,) if single else tuple(out_specs)
    n_in, n_out, n_scr = len(in_specs), len(out_shapes), len(scratch)
    n_ro, n_rw, n_new = len(comm.ro), len(comm.rw), len(comm.new)

    def carrier(*refs):
        ins = refs[:n_in]
        ro_refs = refs[n_in:n_in + n_ro]
        o0 = n_in + n_ro + n_rw
        outs = refs[o0:o0 + n_out]
        rw_refs = refs[o0 + n_out:o0 + n_out + n_rw]
        new_refs = refs[o0 + n_out + n_rw:o0 + n_out + n_rw + n_new]
        s0 = o0 + n_out + n_rw + n_new
        scr = refs[s0:s0 + n_scr]
        send_sems, recv_sems = refs[s0 + n_scr:]
        first = functools.reduce(jnp.logical_and, [pl.program_id(a) == 0 for a in range(len(grid))])
        last = functools.reduce(jnp.logical_and, [pl.program_id(a) == grid[a] - 1 for a in range(len(grid))])
        starts, arrivals = comm.program(ro_refs, rw_refs, new_refs, send_sems, recv_sems)

        @pl.when(first)
        def _():
            for cp in starts:
                cp.start()

        body(*ins, *outs, *scr)

        @pl.when(last)
        def _():
            for cp in arrivals:
                cp.wait_recv()
            for cp in starts:
                cp.wait_send()

    def call(*args):
        rw_shapes = tuple(jax.ShapeDtypeStruct(a.shape, a.dtype) for a in comm.rw)
        res = pl.pallas_call(
            carrier, name=name, out_shape=out_shapes + rw_shapes + tuple(comm.new), grid=grid,
            in_specs=list(in_specs) + [ANY] * (n_ro + n_rw),
            out_specs=out_specs_t + (ANY,) * (n_rw + n_new),
            scratch_shapes=list(scratch) + [pltpu.SemaphoreType.DMA((comm.nsem,)), pltpu.SemaphoreType.DMA((comm.nsem,))],
            input_output_aliases={n_in + n_ro + i: n_out + i for i in range(n_rw)},
            compiler_params=pltpu.CompilerParams(dimension_semantics=("arbitrary",) * len(grid),
                                                 vmem_limit_bytes=VMEM_LIMIT))(*args, *comm.ro, *comm.rw)
        compute = res[0] if single else tuple(res[:n_out])
        return compute, list(res[n_out:n_out + n_rw]), list(res[n_out + n_rw:])

    return call


def comm_only(name, comm):
    n_ro, n_rw, n_new = len(comm.ro), len(comm.rw), len(comm.new)

    def body(*refs):
        ro_refs = refs[:n_ro]
        rw_refs = refs[n_ro + n_rw:n_ro + 2 * n_rw]
        new_refs = refs[n_ro + 2 * n_rw:n_ro + 2 * n_rw + n_new]
        send_sems, recv_sems = refs[n_ro + 2 * n_rw + n_new:]
        starts, arrivals = comm.program(ro_refs, rw_refs, new_refs, send_sems, recv_sems)
        for cp in starts:
            cp.start()
        for cp in arrivals:
            cp.wait_recv()
        for cp in starts:
            cp.wait_send()

    rw_shapes = tuple(jax.ShapeDtypeStruct(a.shape, a.dtype) for a in comm.rw)
    res = pl.pallas_call(body, name=name, out_shape=rw_shapes + tuple(comm.new), in_specs=[ANY] * (n_ro + n_rw),
                         out_specs=(ANY,) * (n_rw + n_new), input_output_aliases={n_ro + i: i for i in range(n_rw)},
                         scratch_shapes=[pltpu.SemaphoreType.DMA((comm.nsem,)), pltpu.SemaphoreType.DMA((comm.nsem,))])(
                             *comm.ro, *comm.rw)
    return list(res[:n_rw]), list(res[n_rw:])


def _pick(n, cands):
    for t in cands:
        if n % t == 0:
            return t
    return n


def _seg(t):
    return jnp.where(t % TPE == 0, BL, t // TPE)


def _slot(t):
    return 2 * (t // TPE) + jnp.where(t % TPE == 0, 0, 1)


def _sigmoid(x):
    return 1.0 / (1.0 + jnp.exp(-x))


MM_BUDGET = 40 * 1024 * 1024
N_TILE_CAP = 1664


def _tile(n, cap=N_TILE_CAP):
    if n <= cap:
        return n
    for t in range(cap - cap % LANE, 0, -LANE):
        if n % t == 0:
            return t
    return n


def _row_tile(m, bytes_of):
    for tm in (1024, 512, 256, 128):
        if m % tm == 0 and bytes_of(tm) <= MM_BUDGET:
            return tm
    return m


def _w_dims(w):
    arr, kind = w
    if kind == "cols":
        return arr.shape[1], NCHIP * arr.shape[2]
    return arr.shape


def _sz(dtype):
    return jnp.dtype(dtype).itemsize


def mm_nn(a, w, *, bias=None, out_dtype=F32, name):
    arr, kind = w
    m, k = a.shape
    _, n = _w_dims(w)
    tn = _tile(arr.shape[2]) if kind == "cols" else _tile(n)
    tm = _row_tile(m, lambda t: 2 * (t * k * _sz(a.dtype) + k * tn * 2 + t * tn * _sz(out_dtype)))
    if kind == "mat":
        b_spec = pl.BlockSpec((k, tn), lambda j, i: (0, j))
    else:
        per = arr.shape[2] // tn
        b_spec = pl.BlockSpec((None, k, tn), lambda j, i: (j // per, 0, j % per))
    has_bias = bias is not None

    def body(*refs):
        out = jnp.dot(refs[0][...].astype(BF16), refs[1][...].astype(BF16), preferred_element_type=F32)
        if has_bias:
            out = out + refs[2][...]
        refs[-1][...] = out.astype(out_dtype)

    in_specs = [pl.BlockSpec((tm, k), lambda j, i: (i, 0)), b_spec]
    args = [a, arr]
    if has_bias:
        in_specs.append(pl.BlockSpec((1, tn), lambda j, i: (0, j)))
        args.append(bias)
    return _pcall(body, name=name, out_shape=jax.ShapeDtypeStruct((m, n), out_dtype), grid=(n // tn, m // tm),
                  in_specs=in_specs, out_specs=pl.BlockSpec((tm, tn), lambda j, i: (i, j)),
                  sem=("parallel", "parallel"))(*args)


def mm_nt(a, w, *, acc=None, out_dtype=F32, name, comm=None):
    arr, kind = w
    kdim, _ = _w_dims(w)
    has_acc = acc is not None
    tk = _tile(kdim, 1408)
    if kind == "cols":
        c = arr.shape[2]
        m = a.shape[-2]
        if a.ndim == 3:
            a_spec = lambda t: pl.BlockSpec((None, t, c), lambda j, i, s: (s // 2, i, s % 2))
        else:
            a_spec = lambda t: pl.BlockSpec((t, c), lambda j, i, s: (i, s))
        tm = _row_tile(m, lambda t: 2 * (t * c * _sz(a.dtype) + tk * c * 2 + t * tk * _sz(out_dtype)) + t * tk * 4)

        def body(a_ref, b_ref, o_ref, acc_ref):
            s = pl.program_id(2)

            @pl.when(s == 0)
            def _():
                acc_ref[...] = jnp.zeros_like(acc_ref)

            acc_ref[...] += lax.dot_general(a_ref[...].astype(BF16), b_ref[...], (((1,), (1,)), ((), ())),
                                            preferred_element_type=F32)

            @pl.when(s == NCHIP - 1)
            def _():
                o_ref[...] = acc_ref[...].astype(out_dtype)

        return _pcall(body, name=name, out_shape=jax.ShapeDtypeStruct((m, kdim), out_dtype),
                      grid=(kdim // tk, m // tm, NCHIP),
                      in_specs=[a_spec(tm), pl.BlockSpec((None, tk, c), lambda j, i, s: (s, j, 0))],
                      out_specs=pl.BlockSpec((tm, tk), lambda j, i, s: (i, j)),
                      scratch=[pltpu.VMEM((tm, tk), F32)],
                      sem=("parallel", "parallel", "arbitrary"), comm=comm)(a, arr)

    m, n = a.shape
    tm = _row_tile(m, lambda t: 2 * (t * n * _sz(a.dtype) + tk * n * 2 + t * tk * (_sz(out_dtype) + 4 * has_acc)))
    b_spec = pl.BlockSpec((tk, n), lambda j, i: (j, 0))

    def body(*refs):
        out = lax.dot_general(refs[0][...].astype(BF16), refs[1][...].astype(BF16), (((1,), (1,)), ((), ())),
                              preferred_element_type=F32)
        if has_acc:
            out = out + refs[2][...]
        refs[-1][...] = out.astype(out_dtype)

    in_specs = [pl.BlockSpec((tm, n), lambda j, i: (i, 0)), b_spec]
    args = [a, arr]
    if has_acc:
        in_specs.append(pl.BlockSpec((tm, tk), lambda j, i: (i, j)))
        args.append(acc)
    return _pcall(body, name=name, out_shape=jax.ShapeDtypeStruct((m, kdim), out_dtype), grid=(kdim // tk, m // tm),
                  in_specs=in_specs, out_specs=pl.BlockSpec((tm, tk), lambda j, i: (i, j)),
                  sem=("parallel", "parallel"))(*args)


def mm_tn(a, b, *, cols=False, out_dtype=F32, name):
    rows, k = a.shape
    halves = b.ndim == 3
    n = 2 * b.shape[2] if halves else b.shape[1]
    odt = out_dtype
    if cols:
        c = n // NCHIP
        tn, tk = _tile(c), k
        per = c // tn
        out_spec = pl.BlockSpec((None, tk, tn), lambda i, j, r: (j // per, 0, j % per))
        out_shape = jax.ShapeDtypeStruct((NCHIP, k, c), odt)
    else:
        tn, tk = _tile(n), _tile(k, 1408)
        out_spec = pl.BlockSpec((tk, tn), lambda i, j, r: (i, j))
        out_shape = jax.ShapeDtypeStruct((k, n), odt)
    tr = _row_tile(rows, lambda t: 2 * (t * tk * _sz(a.dtype) + t * tn * _sz(b.dtype) + tk * tn * _sz(odt)) + tk * tn * 4)
    nsteps = rows // tr

    def body(*refs):
        a_ref, b_ref = refs[0], refs[1]
        o_ref, acc_ref = refs[-2], refs[-1]
        r = pl.program_id(2)

        @pl.when(r == 0)
        def _():
            acc_ref[...] = jnp.zeros_like(acc_ref)

        acc_ref[...] += lax.dot_general(a_ref[...].astype(BF16), b_ref[...].astype(BF16), (((0,), (0,)), ((), ())),
                                        preferred_element_type=F32)

        @pl.when(r == nsteps - 1)
        def _():
            o_ref[...] = acc_ref[...].astype(odt)

    if halves:
        per_half = (n // 2) // tn
        b_spec = pl.BlockSpec((None, tr, tn), lambda i, j, r: (j // per_half, r, j % per_half))
    else:
        b_spec = pl.BlockSpec((tr, tn), lambda i, j, r: (r, j))
    return _pcall(body, name=name, out_shape=out_shape, grid=(k // tk, n // tn, nsteps),
                  in_specs=[pl.BlockSpec((tr, tk), lambda i, j, r: (r, i)), b_spec], out_specs=out_spec,
                  scratch=[pltpu.VMEM((tk, tn), F32)], sem=("parallel", "parallel", "arbitrary"))(a, b)


def ffn_in_swiglu(h, w_in, name, comm=None):
    m, k = h.shape
    c = w_in.shape[2]
    tm = 512

    def body(h_ref, wa_ref, wb_ref, f_ref, u_ref):
        hv = h_ref[...]
        a = jnp.dot(hv, wa_ref[...], preferred_element_type=F32)
        b = jnp.dot(hv, wb_ref[...], preferred_element_type=F32)
        f_ref[...] = (a * _sigmoid(a) * b).astype(BF16)
        u_ref[0] = a.astype(BF16)
        u_ref[1] = b.astype(BF16)

    return _pcall(body, name=name,
                  out_shape=(jax.ShapeDtypeStruct((m, FH), BF16), jax.ShapeDtypeStruct((2, m, FH), BF16)),
                  grid=(2, m // tm),
                  in_specs=[pl.BlockSpec((tm, k), lambda j, i: (i, 0)),
                            pl.BlockSpec((None, k, c), lambda j, i: (j, 0, 0)),
                            pl.BlockSpec((None, k, c), lambda j, i: (2 + j, 0, 0))],
                  out_specs=(pl.BlockSpec((tm, c), lambda j, i: (i, j)), pl.BlockSpec((2, tm, c), lambda j, i: (0, i, j))),
                  sem=("parallel", "parallel"), comm=comm)(h, w_in, w_in)


def d_f_swiglu(dof, w_out, u2, name, comm=None):
    m, k = dof.shape
    c = FH // 2
    tm = 512

    def body(d_ref, w_ref, u_ref, du_ref):
        df = lax.dot_general(d_ref[...], w_ref[...], (((1,), (1,)), ((), ())), preferred_element_type=F32)
        a, b = u_ref[0].astype(F32), u_ref[1].astype(F32)
        sg = _sigmoid(a)
        du_ref[0] = (df * b * (sg * (1.0 + a * (1.0 - sg)))).astype(BF16)
        du_ref[1] = (df * a * sg).astype(BF16)

    ublk = pl.BlockSpec((2, tm, c), lambda j, i: (0, i, j))
    return _pcall(body, name=name, out_shape=jax.ShapeDtypeStruct((2, m, FH), BF16), grid=(2, m // tm),
                  in_specs=[pl.BlockSpec((tm, k), lambda j, i: (i, 0)), pl.BlockSpec((c, k), lambda j, i: (j, 0)), ublk],
                  out_specs=ublk, sem=("parallel", "parallel"), comm=comm)(dof, w_out, u2)


GATE_TN = 512


def gate_mm_fwd(h, wi_gate, o, hs, z, wo, wc, ws, name):
    m, k = h.shape
    tm, tn = 512, min(GATE_TN, D)
    nj = D // tn

    def body(h_ref, g0_ref, g1_ref, g2_ref, o_ref, hs_ref, z_ref, wo_ref, wc_ref, ws_ref, m_ref, g_ref, y_ref):
        hv = h_ref[...]
        acc = jnp.zeros((tm, tn), F32)
        for g, (gw_ref, x_ref, w_ref) in enumerate(((g0_ref, o_ref, wo_ref), (g1_ref, hs_ref, wc_ref),
                                                    (g2_ref, z_ref, ws_ref))):
            gate = _sigmoid(jnp.dot(hv, gw_ref[...], preferred_element_type=F32))
            y = jnp.dot(x_ref[...], w_ref[...], preferred_element_type=F32)
            acc += gate * y
            g_ref[g] = gate.astype(BF16)
            y_ref[g] = y.astype(BF16)
        m_ref[...] = acc.astype(BF16)

    def gate_w(g):
        return pl.BlockSpec((k, tn), lambda j, i: (0, g * nj + j))

    def branch(width):
        return pl.BlockSpec((tm, width), lambda j, i: (i, 0))

    def branch_w(width):
        return pl.BlockSpec((width, tn), lambda j, i: (0, j))

    stacked = pl.BlockSpec((3, tm, tn), lambda j, i: (0, i, j))
    sds3 = jax.ShapeDtypeStruct((3, m, D), BF16)
    return _pcall(body, name=name, out_shape=(jax.ShapeDtypeStruct((m, D), BF16), sds3, sds3), grid=(nj, m // tm),
                  in_specs=[pl.BlockSpec((tm, k), lambda j, i: (i, 0)), gate_w(0), gate_w(1), gate_w(2),
                            branch(o.shape[1]), branch(hs.shape[1]), branch(z.shape[1]),
                            branch_w(wo.shape[0]), branch_w(wc.shape[0]), branch_w(ws.shape[0])],
                  out_specs=(pl.BlockSpec((tm, tn), lambda j, i: (i, j)), stacked, stacked),
                  sem=("parallel", "parallel"))(h, wi_gate, wi_gate, wi_gate, o, hs, z, wo, wc, ws)


def d_merged_gate(dmixed, w_mix, gates, ys, name):
    m, k = dmixed.shape
    tm = 256

    def body(d_ref, w_ref, g_ref, y_ref, da_ref, db_ref, ds_ref, dp_ref):
        dm = lax.dot_general(d_ref[...], w_ref[...], (((1,), (1,)), ((), ())), preferred_element_type=F32)
        for g, dy_ref in enumerate((da_ref, db_ref, ds_ref)):
            gate = g_ref[g].astype(F32)
            dy_ref[...] = (dm * gate).astype(BF16)
            dp_ref[:, g * D:(g + 1) * D] = (dm * y_ref[g].astype(F32) * gate * (1.0 - gate)).astype(BF16)

    stacked = pl.BlockSpec((3, tm, D), lambda i: (0, i, 0))
    row = pl.BlockSpec((tm, D), lambda i: (i, 0))
    sds = jax.ShapeDtypeStruct((m, D), BF16)
    return _pcall(body, name=name, out_shape=(sds, sds, sds, jax.ShapeDtypeStruct((m, 3 * D), BF16)), grid=(m // tm,),
                  in_specs=[pl.BlockSpec((tm, k), lambda i: (i, 0)), pl.BlockSpec((D, k), lambda i: (0, 0)),
                            stacked, stacked],
                  out_specs=(row, row, row, pl.BlockSpec((tm, 3 * D), lambda i: (i, 0))),
                  sem=("parallel",))(dmixed, w_mix, gates, ys)


def _mods_spec():
    return pl.BlockSpec((1, 1, NMOD * D), lambda t: (_seg(t), 0, 0))


def _rows(width):
    return pl.BlockSpec((TM, width), lambda t: (t, 0))


def norm_mod_fwd(x, mods, k_sh, k_sc, name):
    def body(x_ref, m_ref, h_ref):
        x = x_ref[...]
        r = lax.rsqrt(jnp.mean(x * x, axis=-1, keepdims=True) + EPS)
        sh = m_ref[0, :, k_sh * D:(k_sh + 1) * D]
        sc = m_ref[0, :, k_sc * D:(k_sc + 1) * D]
        h_ref[...] = (x * r * (1.0 + sc) + sh).astype(BF16)

    return _pcall(body, name=name, out_shape=jax.ShapeDtypeStruct((NROW, D), BF16), grid=(NT,),
                  in_specs=[_rows(D), _mods_spec()], out_specs=_rows(D), sem=("parallel",))(x, mods)


def _accumulate_slot(t, ref, part):
    first = (t % TPE) <= 1

    @pl.when(first)
    def _():
        ref[0] = part

    @pl.when(jnp.logical_not(first))
    def _():
        ref[0] += part


def norm_mod_bwd(x, mods, dh, dres, k_sc, name):
    def body(x_ref, m_ref, dh_ref, dres_ref, dx_ref, dp_ref):
        t = pl.program_id(0)
        x = x_ref[...]
        r = lax.rsqrt(jnp.mean(x * x, axis=-1, keepdims=True) + EPS)
        xn = x * r
        sc = m_ref[0, :, k_sc * D:(k_sc + 1) * D]
        dh = dh_ref[...]
        dxn = dh * (1.0 + sc)
        dx_ref[...] = r * (dxn - xn * jnp.mean(dxn * xn, axis=-1, keepdims=True)) + dres_ref[...]
        part = jnp.concatenate([jnp.sum(dh, axis=0, keepdims=True), jnp.sum(dh * xn, axis=0, keepdims=True)], axis=1)
        _accumulate_slot(t, dp_ref, part)

    return _pcall(body, name=name,
                  out_shape=(jax.ShapeDtypeStruct((NROW, D), F32), jax.ShapeDtypeStruct((2 * BL, 1, 2 * D), F32)),
                  grid=(NT,), in_specs=[_rows(D), _mods_spec(), _rows(D), _rows(D)],
                  out_specs=(_rows(D), pl.BlockSpec((1, 1, 2 * D), lambda t: (_slot(t), 0, 0))),
                  sem=("arbitrary",))(x, mods, dh, dres)


def resid_norm_fwd(x, y, mods_g, k_g, mods_n, k_sh, k_sc, name):
    def body(x_ref, y_ref, mg_ref, mn_ref, x1_ref, h_ref):
        x1 = x_ref[...] + mg_ref[0, :, k_g * D:(k_g + 1) * D] * y_ref[...]
        x1_ref[...] = x1
        r = lax.rsqrt(jnp.mean(x1 * x1, axis=-1, keepdims=True) + EPS)
        sh = mn_ref[0, :, k_sh * D:(k_sh + 1) * D]
        sc = mn_ref[0, :, k_sc * D:(k_sc + 1) * D]
        h_ref[...] = (x1 * r * (1.0 + sc) + sh).astype(BF16)

    return _pcall(body, name=name,
                  out_shape=(jax.ShapeDtypeStruct((NROW, D), F32), jax.ShapeDtypeStruct((NROW, D), BF16)), grid=(NT,),
                  in_specs=[_rows(D), _rows(D), _mods_spec(), _mods_spec()], out_specs=(_rows(D), _rows(D)),
                  sem=("parallel",))(x, y, mods_g, mods_n)


def norm_resid_bwd(x, mods_n, dh, dres, k_sc, y, mods_g, k_g, name):
    def body(x_ref, mn_ref, dh_ref, dres_ref, y_ref, mg_ref, dx_ref, dpn_ref, dy_ref, dpg_ref):
        t = pl.program_id(0)
        x = x_ref[...]
        r = lax.rsqrt(jnp.mean(x * x, axis=-1, keepdims=True) + EPS)
        xn = x * r
        sc = mn_ref[0, :, k_sc * D:(k_sc + 1) * D]
        dh = dh_ref[...]
        dxn = dh * (1.0 + sc)
        dx = r * (dxn - xn * jnp.mean(dxn * xn, axis=-1, keepdims=True)) + dres_ref[...]
        dx_ref[...] = dx
        dy_ref[...] = (dx * mg_ref[0, :, k_g * D:(k_g + 1) * D]).astype(BF16)
        part = jnp.concatenate([jnp.sum(dh, axis=0, keepdims=True), jnp.sum(dh * xn, axis=0, keepdims=True)], axis=1)
        _accumulate_slot(t, dpn_ref, part)
        _accumulate_slot(t, dpg_ref, jnp.sum(dx * y_ref[...], axis=0, keepdims=True))

    def slot(width):
        return pl.BlockSpec((1, 1, width), lambda t: (_slot(t), 0, 0))

    return _pcall(body, name=name,
                  out_shape=(jax.ShapeDtypeStruct((NROW, D), F32), jax.ShapeDtypeStruct((2 * BL, 1, 2 * D), F32),
                             jax.ShapeDtypeStruct((NROW, D), BF16), jax.ShapeDtypeStruct((2 * BL, 1, D), F32)),
                  grid=(NT,), in_specs=[_rows(D), _mods_spec(), _rows(D), _rows(D), _rows(D), _mods_spec()],
                  out_specs=(_rows(D), slot(2 * D), _rows(D), slot(D)), sem=("arbitrary",))(x, mods_n, dh, dres, y, mods_g)


def gate_resid_fwd(x, y, mods, k_g, name):
    def body(x_ref, y_ref, m_ref, o_ref):
        o_ref[...] = x_ref[...] + m_ref[0, :, k_g * D:(k_g + 1) * D] * y_ref[...]

    return _pcall(body, name=name, out_shape=jax.ShapeDtypeStruct((NROW, D), F32), grid=(NT,),
                  in_specs=[_rows(D), _rows(D), _mods_spec()], out_specs=_rows(D), sem=("parallel",))(x, y, mods)


def gate_resid_bwd(dx, y, mods, k_g, name):
    def body(dx_ref, y_ref, m_ref, dy_ref, dp_ref):
        t = pl.program_id(0)
        dx = dx_ref[...]
        dy_ref[...] = (dx * m_ref[0, :, k_g * D:(k_g + 1) * D]).astype(BF16)
        _accumulate_slot(t, dp_ref, jnp.sum(dx * y_ref[...], axis=0, keepdims=True))

    return _pcall(body, name=name,
                  out_shape=(jax.ShapeDtypeStruct((NROW, D), BF16), jax.ShapeDtypeStruct((2 * BL, 1, D), F32)),
                  grid=(NT,), in_specs=[_rows(D), _rows(D), _mods_spec()],
                  out_specs=(_rows(D), pl.BlockSpec((1, 1, D), lambda t: (_slot(t), 0, 0))),
                  sem=("arbitrary",))(dx, y, mods)


def _swap16(y, lo16):
    return jnp.where(lo16, pltpu.roll(y, LANE - 16, 1), pltpu.roll(y, 16, 1))


def _group_mean(v, g_mat):
    return jnp.dot(v, g_mat, precision=HIGHEST, preferred_element_type=F32)


def qkv_fwd(p_main, cos_t, sin_t, g_mat, gq, gk, name):
    def body(p_ref, cos_ref, sin_ref, g_ref, gq_ref, gk_ref, q_ref, k_ref, v_ref):
        cos, sin, g_mat_v = cos_ref[...], sin_ref[...], g_ref[...]
        lo16 = (lax.broadcasted_iota(jnp.int32, (TM, LANE), 1) % 32) < 16

        def block(xb, g):
            r = lax.rsqrt(_group_mean(xb * xb, g_mat_v) + EPS)
            y = xb * r * g
            return y * cos + _swap16(y, lo16) * sin

        for j in range(AW // LANE):
            q_ref[:, j * LANE:(j + 1) * LANE] = (block(p_ref[:, j * LANE:(j + 1) * LANE], gq_ref[...])
                                                 * ATTN_SCALE).astype(BF16)
        lo = lax.broadcasted_iota(jnp.int32, (TM, LANE), 1) < HD
        for src, dst_ref in ((block(p_ref[:, OFF_K:OFF_K + LANE], gk_ref[...]), k_ref), (p_ref[:, OFF_V:OFF_V + LANE], v_ref)):
            swapped = pltpu.roll(src, HD, 1)
            dst_ref[:, 0:LANE] = jnp.where(lo, src, swapped).astype(BF16)
            dst_ref[:, LANE:2 * LANE] = jnp.where(lo, swapped, src).astype(BF16)

    tab = pl.BlockSpec((TM, LANE), lambda t: (t % TPE, 0))
    small = pl.BlockSpec((1, LANE), lambda t: (0, 0))
    return _pcall(body, name=name,
                  out_shape=(jax.ShapeDtypeStruct((NROW, AW), BF16), jax.ShapeDtypeStruct((NROW, 2 * KVW), BF16),
                             jax.ShapeDtypeStruct((NROW, 2 * KVW), BF16)),
                  grid=(NT,),
                  in_specs=[_rows(QKVW), tab, tab, pl.BlockSpec((LANE, LANE), lambda t: (0, 0)), small, small],
                  out_specs=(_rows(AW), _rows(2 * KVW), _rows(2 * KVW)),
                  sem=("parallel",))(p_main, cos_t, sin_t, g_mat, gq, gk)


def qkv_bwd(p_main, cos_t, sin_t, g_mat, gq, gk, dq, dk, dv, name, comm=None):
    def body(p_ref, cos_ref, sin_ref, g_ref, gq_ref, gk_ref, dq_ref, dk_ref, dv_ref, dp_ref, dg_ref):
        t = pl.program_id(0)
        cos, sin, g_mat_v = cos_ref[...], sin_ref[...], g_ref[...]
        lo16 = (lax.broadcasted_iota(jnp.int32, (TM, LANE), 1) % 32) < 16

        def block(xb, g, dyr):
            r = lax.rsqrt(_group_mean(xb * xb, g_mat_v) + EPS)
            xn = xb * r
            dy = dyr * cos + _swap16(dyr * sin, lo16)
            dgl = jnp.sum(dy * xn, axis=0, keepdims=True)
            dxn = dy * g
            return r * (dxn - xn * _group_mean(dxn * xn, g_mat_v)), dgl

        parts = []
        for j in range(AW // LANE):
            sl = slice(j * LANE, (j + 1) * LANE)
            dxb, dgl = block(p_ref[:, sl], gq_ref[...], dq_ref[:, sl] * ATTN_SCALE)
            dp_ref[:, sl] = dxb.astype(BF16)
            parts.append(dgl)
        lo = lax.broadcasted_iota(jnp.int32, (TM, LANE), 1) < HD

        def fold(d_ref):
            d0, d1 = d_ref[:, 0:LANE], d_ref[:, LANE:2 * LANE]
            return jnp.where(lo, d0 + pltpu.roll(d0, HD, 1), d1 + pltpu.roll(d1, HD, 1))

        dxb, dgl = block(p_ref[:, OFF_K:OFF_K + LANE], gk_ref[...], fold(dk_ref))
        dp_ref[:, OFF_K:OFF_K + LANE] = dxb.astype(BF16)
        parts.append(dgl)
        parts.append(jnp.zeros((1, LANE), F32))
        dp_ref[:, OFF_V:OFF_V + LANE] = fold(dv_ref).astype(BF16)
        part = jnp.concatenate(parts, axis=1)

        @pl.when(t == 0)
        def _():
            dg_ref[...] = part

        @pl.when(t != 0)
        def _():
            dg_ref[...] += part

    tab = pl.BlockSpec((TM, LANE), lambda t: (t % TPE, 0))
    small = pl.BlockSpec((1, LANE), lambda t: (0, 0))
    return _pcall(body, name=name,
                  out_shape=(jax.ShapeDtypeStruct((NROW, QKVW), BF16), jax.ShapeDtypeStruct((1, QKVW), F32)),
                  grid=(NT,),
                  in_specs=[_rows(QKVW), tab, tab, pl.BlockSpec((LANE, LANE), lambda t: (0, 0)), small, small,
                            _rows(AW), _rows(2 * KVW), _rows(2 * KVW)],
                  out_specs=(_rows(QKVW), pl.BlockSpec((1, QKVW), lambda t: (0, 0))),
                  sem=("arbitrary",), comm=comm)(p_main, cos_t, sin_t, g_mat, gq, gk, dq, dk, dv)


def _layer_norm_parts(yc):
    mu = jnp.mean(yc, axis=-1, keepdims=True)
    xc = yc - mu
    rs = lax.rsqrt(jnp.mean(xc * xc, axis=-1, keepdims=True) + EPS)
    return xc * rs, rs


def ln_silu_fwd(yc, g, b, name):
    def body(y_ref, g_ref, b_ref, o_ref):
        nrm, _ = _layer_norm_parts(y_ref[...])
        ln = nrm * g_ref[...] + b_ref[...]
        o_ref[...] = (ln * _sigmoid(ln)).astype(BF16)

    vec = pl.BlockSpec((1, CW), lambda t: (0, 0))
    return _pcall(body, name=name, out_shape=jax.ShapeDtypeStruct((NROW, CW), BF16), grid=(NT,),
                  in_specs=[_rows(CW), vec, vec], out_specs=_rows(CW), sem=("parallel",))(yc, g, b)


def ln_silu_bwd(yc, g, b, dhs, name):
    def body(y_ref, g_ref, b_ref, dh_ref, dy_ref, dg_ref, db_ref):
        t = pl.program_id(0)
        nrm, rs = _layer_norm_parts(y_ref[...])
        ln = nrm * g_ref[...] + b_ref[...]
        sg = _sigmoid(ln)
        dln = dh_ref[...] * (sg * (1.0 + ln * (1.0 - sg)))
        dn = dln * g_ref[...]
        dy_ref[...] = rs * (dn - jnp.mean(dn, axis=-1, keepdims=True)
                            - nrm * jnp.mean(dn * nrm, axis=-1, keepdims=True))
        pg = jnp.sum(dln * nrm, axis=0, keepdims=True)
        pb = jnp.sum(dln, axis=0, keepdims=True)

        @pl.when(t == 0)
        def _():
            dg_ref[...] = pg
            db_ref[...] = pb

        @pl.when(t != 0)
        def _():
            dg_ref[...] += pg
            db_ref[...] += pb

    vec = pl.BlockSpec((1, CW), lambda t: (0, 0))
    return _pcall(body, name=name,
                  out_shape=(jax.ShapeDtypeStruct((NROW, CW), F32), jax.ShapeDtypeStruct((1, CW), F32),
                             jax.ShapeDtypeStruct((1, CW), F32)),
                  grid=(NT,), in_specs=[_rows(CW), vec, vec, _rows(CW)], out_specs=(_rows(CW), vec, vec),
                  sem=("arbitrary",))(yc, g, b, dhs)


def loss_fwd_bwd(y, target, name):
    def body(y_ref, t_ref, dy_ref, l_ref):
        t = pl.program_id(0)
        latent = (t % TPE) != 0
        err = jnp.where(latent, y_ref[...] - t_ref[...], 0.0)
        dy_ref[...] = err * (1.0 / D)
        part = jnp.sum(err * err, axis=0, keepdims=True)

        @pl.when(t == 0)
        def _():
            l_ref[...] = part

        @pl.when(t != 0)
        def _():
            l_ref[...] += part

    tgt = pl.BlockSpec((TM, D), lambda t: ((t // TPE) * (TPE - 1) + jnp.maximum(t % TPE - 1, 0), 0))
    return _pcall(body, name=name,
                  out_shape=(jax.ShapeDtypeStruct((NROW, D), F32), jax.ShapeDtypeStruct((1, D), F32)),
                  grid=(NT,), in_specs=[_rows(D), tgt], out_specs=(_rows(D), pl.BlockSpec((1, D), lambda t: (0, 0))),
                  sem=("arbitrary",))(y, target)


QB_PER_KV = AW // LANE // NKV


def _softmax_parts(qm, k):
    s = lax.dot_general(qm, k, (((1,), (1,)), ((), ())), preferred_element_type=F32)
    e = jnp.exp(s - jnp.max(s, axis=-1, keepdims=True))
    return e, 1.0 / jnp.sum(e, axis=-1, keepdims=True)


def _lane_halves():
    lo = lax.broadcasted_iota(jnp.int32, (TM, LANE), 1) < HD
    return lo, jnp.logical_not(lo)


def _stack_heads(x, halves):
    zero = jnp.zeros_like(x)
    return jnp.concatenate([jnp.where(halves[0], x, zero), jnp.where(halves[1], x, zero)], axis=0)


def attn_fwd(q, k, v, name, comm=None):
    def body(q_ref, k_ref, v_ref, o_ref):
        t = pl.program_id(2)
        halves = _lane_halves()

        def run(nk):
            kv, vv = k_ref[0:nk, :], v_ref[0:nk, :]
            for j in range(QB_PER_KV):
                lanes = slice(j * LANE, (j + 1) * LANE)
                e, rinv = _softmax_parts(_stack_heads(q_ref[:, lanes], halves), kv)
                out = jnp.dot(e.astype(BF16), vv, preferred_element_type=F32) * rinv
                o_ref[:, lanes] = jnp.where(halves[0], out[0:TM], out[TM:2 * TM]).astype(BF16)

        @pl.when(t == 0)
        def _():
            run(CTX)

        @pl.when(t != 0)
        def _():
            run(RE)

    qs = pl.BlockSpec((TM, QB_PER_KV * LANE), lambda b, h, t: (b * TPE + t, h))
    ks = pl.BlockSpec((RE, LANE), lambda b, h, t: (b, h))
    return _pcall(body, name=name, out_shape=jax.ShapeDtypeStruct((NROW, AW), BF16), grid=(BL, NKV, TPE),
                  in_specs=[qs, ks, ks], out_specs=qs, sem=("parallel",) * 3, comm=comm)(q, k, v)


def attn_bwd(q, k, v, o, do, name, comm=None):
    def body(q_ref, k_ref, v_ref, o_ref, do_ref, dq_ref, dk_ref, dv_ref):
        t = pl.program_id(2)
        halves = _lane_halves()

        @pl.when(t == 0)
        def _():
            dk_ref[...] = jnp.zeros_like(dk_ref)
            dv_ref[...] = jnp.zeros_like(dv_ref)

        def run(nk):
            kv, vv = k_ref[0:nk, :], v_ref[0:nk, :]
            dks, dvs = [], []
            for j in range(QB_PER_KV):
                lanes = slice(j * LANE, (j + 1) * LANE)
                q2, do2 = _stack_heads(q_ref[:, lanes], halves), _stack_heads(do_ref[:, lanes], halves)
                ov = o_ref[:, lanes].astype(F32)
                delta = jnp.sum(do2.astype(F32) * jnp.concatenate([ov, ov], axis=0), axis=-1, keepdims=True)
                e, rinv = _softmax_parts(q2, kv)
                p = e * rinv
                dvs.append(lax.dot_general(p.astype(BF16), do2, (((0,), (0,)), ((), ())), preferred_element_type=F32))
                dp = lax.dot_general(do2, vv, (((1,), (1,)), ((), ())), preferred_element_type=F32)
                ds = (p * (dp - delta)).astype(BF16)
                dq = jnp.dot(ds, kv, preferred_element_type=F32)
                dks.append(lax.dot_general(ds, q2, (((0,), (0,)), ((), ())), preferred_element_type=F32))
                dq_ref[:, lanes] = jnp.where(halves[0], dq[0:TM], dq[TM:2 * TM])
            dv_ref[0:nk, :] += functools.reduce(jnp.add, dvs)
            dk_ref[0:nk, :] += functools.reduce(jnp.add, dks)

        @pl.when(t == 0)
        def _():
            run(CTX)

        @pl.when(t != 0)
        def _():
            run(RE)

    qs = pl.BlockSpec((TM, QB_PER_KV * LANE), lambda b, h, t: (b * TPE + t, h))
    ks = pl.BlockSpec((RE, LANE), lambda b, h, t: (b, h))
    return _pcall(body, name=name,
                  out_shape=(jax.ShapeDtypeStruct((NROW, AW), F32), jax.ShapeDtypeStruct((NROW, 2 * KVW), F32),
                             jax.ShapeDtypeStruct((NROW, 2 * KVW), F32)),
                  grid=(BL, NKV, TPE), in_specs=[qs, ks, ks, qs, qs], out_specs=(qs, ks, ks),
                  sem=("parallel", "parallel", "arbitrary"), comm=comm)(q, k, v, o, do)


CONV_SEGS = ((0, CTX), (CTX, SEQ))


def _p_block(col0):
    return pl.BlockSpec((RE, LANE), lambda cb, b: (b, col0 // LANE + cb))


def _conv_io(width):
    return pl.BlockSpec((RE, LANE), lambda cb, b: (b, cb))


def _taps(n):
    return pl.BlockSpec((n, LANE), lambda cb, b: (0, cb))


def _fill_pad(pad_ref, length, values):
    pad_ref[0:PADR, :] = jnp.zeros((PADR, LANE), F32)
    pad_ref[PADR + length:2 * PADR + length, :] = jnp.zeros((PADR, LANE), F32)
    pad_ref[PADR:PADR + length, :] = values


def _conv_chunk(pad_ref, w_ref, ntap, c0, first_row):
    acc = jnp.zeros((CONV_CH, LANE), F32)
    for kk in range(ntap):
        r0 = c0 + first_row(kk)
        acc += w_ref[kk:kk + 1, :] * pad_ref[r0:r0 + CONV_CH, :]
    return acc


def conv_fwd(p_main, wdw, bdw, w3, name):
    def body(a_ref, g_ref, bg_ref, cg_ref, xs_ref, w_ref, b_ref, w3_ref, yc_ref, z_ref, pad_ref):
        for off, length in CONV_SEGS:
            rows = slice(off, off + length)
            _fill_pad(pad_ref, length, a_ref[rows, :] * _sigmoid(g_ref[rows, :]))
            for c0 in range(0, length, CONV_CH):
                acc = _conv_chunk(pad_ref, w_ref, CONF_K, c0, lambda kk: PADR + kk - CONF_K // 2)
                yc_ref[off + c0:off + c0 + CONV_CH, :] = acc + b_ref[...]
            pad_ref[PADR:PADR + length, :] = cg_ref[rows, :] * xs_ref[rows, :]
            for c0 in range(0, length, CONV_CH):
                acc = _conv_chunk(pad_ref, w3_ref, SC_K, c0, lambda kk: PADR + kk - SC_K // 2)
                z_ref[off + c0:off + c0 + CONV_CH, :] = (bg_ref[off + c0:off + c0 + CONV_CH, :] * acc).astype(BF16)

    return _pcall(body, name=name,
                  out_shape=(jax.ShapeDtypeStruct((NROW, CW), F32), jax.ShapeDtypeStruct((NROW, CW), BF16)),
                  grid=(CB, BL),
                  in_specs=[_p_block(OFF_CONF), _p_block(OFF_CONF + CW), _p_block(OFF_SC), _p_block(OFF_SC + CW),
                            _p_block(OFF_SC + 2 * CW), _taps(CONF_K), _taps(1), _taps(SC_K)],
                  out_specs=(_conv_io(CW), _conv_io(CW)),
                  scratch=[pltpu.VMEM((SEQ + 2 * PADR, LANE), F32)],
                  sem=("parallel", "parallel"))(p_main, p_main, p_main, p_main, p_main, wdw, bdw, w3)


def _tap_grad(pad_ref, d_ref, off, length, first_row):
    acc = jnp.zeros((8, LANE), F32)
    for c0 in range(0, length, CONV_CH):
        prod = d_ref[off + c0:off + c0 + CONV_CH, :] * pad_ref[c0 + first_row:c0 + first_row + CONV_CH, :]
        acc += jnp.sum(prod.reshape(CONV_CH // 8, 8, LANE), axis=0)
    return jnp.sum(acc, axis=0, keepdims=True)


def conv_bwd(p_main, wdw, w3, dyc, dz, name):
    def body(a_ref, g_ref, bg_ref, cg_ref, xs_ref, w_ref, w3_ref, dyc_ref, dz_ref,
             da_ref, dg_ref, dbg_ref, dcg_ref, dxs_ref, dw_ref, db_ref, dw3_ref, pad_x, pad_d, dconv_ref):
        b = pl.program_id(1)

        @pl.when(b == 0)
        def _():
            dw_ref[...] = jnp.zeros_like(dw_ref)
            db_ref[...] = jnp.zeros_like(db_ref)
            dw3_ref[...] = jnp.zeros_like(dw3_ref)

        db_ref[...] += jnp.sum(dyc_ref[...], axis=0, keepdims=True)
        for off, length in CONV_SEGS:
            rows = slice(off, off + length)
            _fill_pad(pad_x, length, a_ref[rows, :] * _sigmoid(g_ref[rows, :]))
            _fill_pad(pad_d, length, dyc_ref[rows, :])
            for kk in range(CONF_K):
                dw_ref[kk:kk + 1, :] += _tap_grad(pad_x, dyc_ref, off, length, PADR + kk - CONF_K // 2)
            for c0 in range(0, length, CONV_CH):
                dh = _conv_chunk(pad_d, w_ref, CONF_K, c0, lambda kk: PADR + CONF_K // 2 - kk)
                ch = slice(off + c0, off + c0 + CONV_CH)
                sg = _sigmoid(g_ref[ch, :])
                da_ref[ch, :] = (dh * sg).astype(BF16)
                dg_ref[ch, :] = (dh * a_ref[ch, :] * sg * (1.0 - sg)).astype(BF16)
            pad_x[PADR:PADR + length, :] = cg_ref[rows, :] * xs_ref[rows, :]
            dconv_ref[rows, :] = dz_ref[rows, :] * bg_ref[rows, :]
            pad_d[PADR:PADR + length, :] = dconv_ref[rows, :]
            for kk in range(SC_K):
                dw3_ref[kk:kk + 1, :] += _tap_grad(pad_x, dconv_ref, off, length, PADR + kk - SC_K // 2)
            for c0 in range(0, length, CONV_CH):
                ch = slice(off + c0, off + c0 + CONV_CH)
                c3 = _conv_chunk(pad_x, w3_ref, SC_K, c0, lambda kk: PADR + kk - SC_K // 2)
                dbg_ref[ch, :] = (dz_ref[ch, :] * c3).astype(BF16)
                dcx = _conv_chunk(pad_d, w3_ref, SC_K, c0, lambda kk: PADR + SC_K // 2 - kk)
                dcg_ref[ch, :] = (dcx * xs_ref[ch, :]).astype(BF16)
                dxs_ref[ch, :] = (dcx * cg_ref[ch, :]).astype(BF16)

    slab = jax.ShapeDtypeStruct((NROW, CW), BF16)
    return _pcall(body, name=name,
                  out_shape=(slab,) * 5 + (jax.ShapeDtypeStruct((CONF_K, CW), F32), jax.ShapeDtypeStruct((1, CW), F32),
                                           jax.ShapeDtypeStruct((SC_K, CW), F32)),
                  grid=(CB, BL),
                  in_specs=[_p_block(OFF_CONF), _p_block(OFF_CONF + CW), _p_block(OFF_SC), _p_block(OFF_SC + CW),
                            _p_block(OFF_SC + 2 * CW), _taps(CONF_K), _taps(SC_K), _conv_io(CW), _conv_io(CW)],
                  out_specs=(_conv_io(CW),) * 5 + (_taps(CONF_K), _taps(1), _taps(SC_K)),
                  scratch=[pltpu.VMEM((SEQ + 2 * PADR, LANE), F32), pltpu.VMEM((SEQ + 2 * PADR, LANE), F32),
                           pltpu.VMEM((RE, LANE), F32)],
                  sem=("parallel", "arbitrary"))(p_main, p_main, p_main, p_main, p_main, wdw, w3, dyc, dz)


def silu_rows(x, name):
    def body(x_ref, o_ref):
        o_ref[...] = x_ref[...] * _sigmoid(x_ref[...])

    return _pcall(body, name=name, out_shape=jax.ShapeDtypeStruct(x.shape, F32))(x)


def silu_rows_bwd(x, dcs, name):
    def body(x_ref, d_ref, o_ref):
        x = x_ref[...]
        sg = _sigmoid(x)
        tot = d_ref[0]
        for i in range(1, DEPTH):
            tot += d_ref[i]
        o_ref[...] = tot * (sg * (1.0 + x * (1.0 - sg)))

    return _pcall(body, name=name, out_shape=jax.ShapeDtypeStruct(x.shape, F32))(x, dcs)


def dmod_assemble(parts, name):
    def body(p_ref, dm_ref, db_ref):
        row = lax.broadcasted_iota(jnp.int32, (8, NMOD * D), 0)
        dm = jnp.zeros((8, NMOD * D), F32)
        db = jnp.zeros((1, NMOD * D), F32)
        for s in range(2 * BL):
            target = BL if s % 2 == 0 else s // 2
            part = p_ref[s:s + 1, :]
            dm += jnp.where(row == target, part, 0.0)
            db += part
        dm_ref[...] = dm
        db_ref[...] = db

    return _pcall(body, name=name, out_shape=(jax.ShapeDtypeStruct((8, NMOD * D), F32),
                                              jax.ShapeDtypeStruct((1, NMOD * D), F32)))(parts)


def sum_leading(x, name):
    n = x.shape[0]
    tr = _pick(x.shape[1], (256, 32, 8))

    def body(x_ref, o_ref):
        tot = x_ref[0].astype(F32)
        for i in range(1, n):
            tot += x_ref[i].astype(F32)
        o_ref[...] = tot

    return _pcall(body, name=name, out_shape=jax.ShapeDtypeStruct(x.shape[1:], F32), grid=(x.shape[1] // tr,),
                  in_specs=[pl.BlockSpec((n, tr, x.shape[2]), lambda i: (0, i, 0))],
                  out_specs=pl.BlockSpec((tr, x.shape[2]), lambda i: (i, 0)), sem=("parallel",))(x)


SLAB_ROWS = (256, 176, 128, 64, 8)


def _prefetch_call(body, name, out_shape, grid, in_specs, out_specs, sem, scalars, *args):
    spec = pltpu.PrefetchScalarGridSpec(num_scalar_prefetch=len(scalars), grid=grid, in_specs=in_specs,
                                        out_specs=out_specs)
    return pl.pallas_call(body, name=name, out_shape=out_shape, grid_spec=spec,
                          compiler_params=pltpu.CompilerParams(dimension_semantics=sem,
                                                               vmem_limit_bytes=VMEM_LIMIT))(*scalars, *args)


def cast_layers(w, chip, name):
    depth, r, c = w.shape
    tr = _pick(r, SLAB_ROWS)

    def body(s_ref, w_ref, *o_refs):
        for l in range(depth):
            o_refs[l][...] = w_ref[l].astype(BF16)

    slab = pl.BlockSpec((None, tr, c), lambda i, s: (s[0], i, 0))
    return _prefetch_call(body, name, (jax.ShapeDtypeStruct((NCHIP, r, c), BF16),) * depth, (r // tr,),
                          [pl.BlockSpec((depth, tr, c), lambda i, s: (0, i, 0))], (slab,) * depth,
                          ("parallel",), (chip,), w)


def rs_add(g, other, core, chip, name):
    _, r, c = g.shape
    rh = r // 2
    tr = _pick(rh, SLAB_ROWS)
    nblk = rh // tr

    def body(core_ref, chip_ref, g_ref, o_ref, send_ref, arr_ref):
        k = pl.program_id(1)
        tot = (g_ref[...].astype(F32) + o_ref[...].astype(F32)).astype(BF16)
        send_ref[...] = tot

        @pl.when(k == chip_ref[0])
        def _():
            arr_ref[...] = tot

    blk = (None, tr, c)
    return _prefetch_call(
        body, name, (jax.ShapeDtypeStruct(other.shape, BF16), jax.ShapeDtypeStruct(g.shape, BF16)), (nblk, NCHIP),
        [pl.BlockSpec(blk, lambda i, k, cr, ch: (k, cr[0] * nblk + i, 0)), pl.BlockSpec(blk, lambda i, k, cr, ch: (k, i, 0))],
        (pl.BlockSpec(blk, lambda i, k, cr, ch: (k, i, 0)),
         pl.BlockSpec(blk, lambda i, k, cr, ch: (ch[0], cr[0] * nblk + i, 0))),
        ("parallel", "arbitrary"), (core, chip), g, other)


def adamw_layer(w, arr, m, v, layer, prev, name):
    depth, r, c = w.shape
    tr = _pick(r, SLAB_ROWS)
    c1 = 1.0 / (1.0 - ADAM_B1 ** ADAM_STEP)
    c2 = 1.0 / (1.0 - ADAM_B2 ** ADAM_STEP)

    def body(w_ref, a_ref, m_ref, v_ref, p0, p1, p2, p3, g_ref, d_ref, mo_ref, vo_ref):
        gv = a_ref[0].astype(F32)
        for k in range(1, NCHIP):
            gv += a_ref[k].astype(F32)
        mn = ADAM_B1 * m_ref[...] + (1.0 - ADAM_B1) * gv
        vn = ADAM_B2 * v_ref[...] + (1.0 - ADAM_B2) * (gv * gv)
        g_ref[...] = gv
        d_ref[...] = -ADAM_LR * ((mn * c1) / (jnp.sqrt(vn * c2) + ADAM_EPS) + ADAM_WD * w_ref[...])
        mo_ref[...] = mn
        vo_ref[...] = vn

    spec = pl.BlockSpec((None, tr, c), lambda i: (layer, i, 0))
    sds = jax.ShapeDtypeStruct(w.shape, F32)
    return pl.pallas_call(body, name=name, out_shape=(sds,) * 4, grid=(r // tr,),
                          in_specs=[spec, pl.BlockSpec((NCHIP, tr, c), lambda i: (0, i, 0)), spec, spec] + [ANY] * 4,
                          out_specs=(spec,) * 4, input_output_aliases={4: 0, 5: 1, 6: 2, 7: 3},
                          compiler_params=pltpu.CompilerParams(dimension_semantics=("parallel",),
                                                               vmem_limit_bytes=VMEM_LIMIT))(w, arr, m, v, *prev)


def adamw(w, g, m, v, name):
    rows, cols = w.shape
    tr = _pick(rows, (256, 248, 128, 8))
    c1 = 1.0 / (1.0 - ADAM_B1 ** ADAM_STEP)
    c2 = 1.0 / (1.0 - ADAM_B2 ** ADAM_STEP)

    def body(w_ref, g_ref, m_ref, v_ref, d_ref, mo_ref, vo_ref):
        gv = g_ref[...]
        mn = ADAM_B1 * m_ref[...] + (1.0 - ADAM_B1) * gv
        vn = ADAM_B2 * v_ref[...] + (1.0 - ADAM_B2) * (gv * gv)
        d_ref[...] = -ADAM_LR * ((mn * c1) / (jnp.sqrt(vn * c2) + ADAM_EPS) + ADAM_WD * w_ref[...])
        mo_ref[...] = mn
        vo_ref[...] = vn

    spec = pl.BlockSpec((tr, cols), lambda i: (i, 0))
    sds = jax.ShapeDtypeStruct((rows, cols), F32)
    return _pcall(body, name=name, out_shape=(sds, sds, sds), grid=(rows // tr,), in_specs=[spec] * 4,
                  out_specs=(spec, spec, spec), sem=("parallel",))(w, g, m, v)


def _place():
    return lax.axis_index("x"), lax.axis_index("y"), lax.axis_index("c")


def _other_chips(x, y):
    return [(1 - x, y), (x, 1 - y), (1 - x, 1 - y)]


def _comm_call(body, name, out_shape, n_in, nsem):
    return pl.pallas_call(body, name=name, out_shape=out_shape, in_specs=[ANY] * n_in,
                          out_specs=jax.tree.map(lambda _: ANY, out_shape),
                          scratch_shapes=[pltpu.SemaphoreType.DMA((nsem,)), pltpu.SemaphoreType.DMA((nsem,)),
                                          pltpu.SemaphoreType.DMA])


def all_gather8(block, name):
    def body(x_ref, out_ref, send_sems, recv_sems, local_sem):
        x, y, c = _place()
        me, sibling = (x, y, c), (x, y, 1 - c)
        chips = _other_chips(x, y)

        def slot(px, py, pc):
            return out_ref.at[4 * px + 2 * py + pc]

        def copy(k, blk, to, src=None):
            return pltpu.make_async_remote_copy(src_ref=slot(*blk) if src is None else src, dst_ref=slot(*blk),
                                                send_sem=send_sems.at[k], recv_sem=recv_sems.at[k],
                                                device_id=to, device_id_type=MESH)

        mine = pltpu.make_async_copy(x_ref, slot(*me), local_sem)
        mine.start()
        first = [copy(0, me, sibling, src=x_ref)]
        first += [copy(1 + j, me, (*chip, c), src=x_ref) for j, chip in enumerate(chips)]
        for cp in first:
            cp.start()
        passed = [copy(4 + j, (*chip, c), sibling) for j, chip in enumerate(chips)]
        for j, chip in enumerate(chips):
            copy(1 + j, (*chip, c), me).wait_recv()
            passed[j].start()
        copy(0, sibling, me).wait_recv()
        for j, chip in enumerate(chips):
            copy(4 + j, (*chip, 1 - c), me).wait_recv()
        for cp in first + passed:
            cp.wait_send()
        mine.wait()

    return _comm_call(body, name, jax.ShapeDtypeStruct((8,) + block.shape, block.dtype), 1, 7)(block)


def _remote(src, dst, send_sems, recv_sems, k, to):
    return pltpu.make_async_remote_copy(src_ref=src, dst_ref=dst, send_sem=send_sems.at[k], recv_sem=recv_sems.at[k],
                                        device_id=to, device_id_type=MESH)


def _half(ref, slot, core):
    rh = ref.shape[1] // 2
    return ref.at[slot, pl.ds(core * rh, rh)]


def _all_slots_half(ref, core):
    rh = ref.shape[1] // 2
    return ref.at[:, pl.ds(core * rh, rh)]


def gather_ici(bufs):
    def program(ro, rw, new, ss, rs):
        x, y, c = _place()
        own = 2 * x + y
        starts, arrivals = [], []
        for w, ref in enumerate(rw):
            for j, chip in enumerate(_other_chips(x, y)):
                starts.append(_remote(_half(ref, own, c), _half(ref, own, c), ss, rs, 3 * w + j, (*chip, c)))
                arrivals.append(_remote(_half(ref, own, c), _half(ref, 2 * chip[0] + chip[1], c), ss, rs, 3 * w + j,
                                        (*chip, c)))
        return starts, arrivals

    return CommSpec((), tuple(bufs), (), 3 * len(bufs), program)


def gather_d2d(bufs):
    def program(ro, rw, new, ss, rs):
        x, y, c = _place()
        starts, arrivals = [], []
        for w, ref in enumerate(rw):
            for j, chip in enumerate(_other_chips(x, y)):
                slot = 2 * chip[0] + chip[1]
                starts.append(_remote(_half(ref, slot, c), _half(ref, slot, c), ss, rs, 3 * w + j, (x, y, 1 - c)))
                arrivals.append(_remote(_half(ref, slot, c), _half(ref, slot, 1 - c), ss, rs, 3 * w + j, (x, y, 1 - c)))
        return starts, arrivals

    return CommSpec((), tuple(bufs), (), 3 * len(bufs), program)


def rs_swap(grads):
    def program(ro, rw, new, ss, rs):
        x, y, c = _place()
        copies = [_remote(_all_slots_half(g, 1 - c), new[w], ss, rs, w, (x, y, 1 - c)) for w, g in enumerate(ro)]
        return copies, copies

    shapes = tuple(jax.ShapeDtypeStruct((NCHIP, g.shape[1] // 2, g.shape[2]), g.dtype) for g in grads)
    return CommSpec(tuple(grads), (), shapes, len(grads), program)


def rs_ici(sends, arrs):
    def program(ro, rw, new, ss, rs):
        x, y, c = _place()
        own = 2 * x + y
        starts, arrivals = [], []
        for w, (snd, arr) in enumerate(zip(ro, rw)):
            for j, chip in enumerate(_other_chips(x, y)):
                slot = 2 * chip[0] + chip[1]
                starts.append(_remote(snd.at[slot], _half(arr, own, c), ss, rs, 3 * w + j, (*chip, c)))
                arrivals.append(_remote(snd.at[slot], _half(arr, slot, c), ss, rs, 3 * w + j, (*chip, c)))
        return starts, arrivals

    return CommSpec(tuple(sends), tuple(arrs), (), 3 * len(sends), program)


def rs_d2d(arrs):
    def program(ro, rw, new, ss, rs):
        x, y, c = _place()
        starts = [_remote(_all_slots_half(a, c), _all_slots_half(a, c), ss, rs, w, (x, y, 1 - c)) for w, a in enumerate(rw)]
        arrivals = [_remote(_all_slots_half(a, c), _all_slots_half(a, 1 - c), ss, rs, w, (x, y, 1 - c))
                    for w, a in enumerate(rw)]
        return starts, arrivals

    return CommSpec((), tuple(arrs), (), len(arrs), program)


PACK_COLS = 1024
MATMUL_W = ("w_ada", "w_in", "w_attn_o", "w_conf_out", "w_sc_out", "w_mix_out", "w_ffn_in", "w_ffn_out")
ROW_SPLIT = ("w_mix_out", "w_ffn_out")
CONV_W = ("conf_dw_w", "sc_dw_w")
SMALL = ("c_ctx", "b_ada", "q_norm", "k_norm", "conf_dw_b", "conf_ln_g", "conf_ln_b", "conf_dw_w", "sc_dw_w")


def _pack_rows(arrays, row_multiple):
    flat = jnp.concatenate([a.reshape(-1) for a in arrays])
    rows = -(-flat.shape[0] // PACK_COLS)
    rows = -(-rows // row_multiple) * row_multiple
    flat = jnp.pad(flat, (0, rows * PACK_COLS - flat.shape[0]))
    return flat.reshape(rows, PACK_COLS)


def _unpack(flat2d, shapes):
    flat = flat2d.reshape(-1)
    out, pos = [], 0
    for shp in shapes:
        n = 1
        for s in shp:
            n *= s
        out.append(flat[pos:pos + n].reshape(shp))
        pos += n
    return out


def _cols_joined(stacked_layer):
    nchip, r, c = stacked_layer.shape
    return jnp.transpose(stacked_layer, (1, 0, 2)).reshape(r, nchip * c)


def _cols_split(full):
    r, cols = full.shape
    return jnp.transpose(full.reshape(r, NCHIP, cols // NCHIP), (1, 0, 2))


def _rope_tables():
    rows = SEQ // GRID_W
    r_ids = jnp.repeat(jnp.arange(rows, dtype=F32), GRID_W)
    c_ids = jnp.tile(jnp.arange(GRID_W, dtype=F32), rows)
    freqs = ROPE_THETA ** (-jnp.arange(0, HD // 2, 2, dtype=F32) / (HD // 2))
    ang_r, ang_c = r_ids[:, None] * freqs, c_ids[:, None] * freqs
    cos_h = jnp.concatenate([jnp.cos(ang_r), jnp.cos(ang_r), jnp.cos(ang_c), jnp.cos(ang_c)], axis=1)
    sin_h = jnp.concatenate([-jnp.sin(ang_r), jnp.sin(ang_r), -jnp.sin(ang_c), jnp.sin(ang_c)], axis=1)
    cos_t = jnp.concatenate([jnp.ones((CTX, HD), F32), cos_h], axis=0)
    sin_t = jnp.concatenate([jnp.zeros((CTX, HD), F32), sin_h], axis=0)
    return jnp.tile(cos_t, (1, LANE // HD)), jnp.tile(sin_t, (1, LANE // HD))


def _group_matrix():
    gid = jnp.arange(LANE) // HD
    return jnp.where(gid[:, None] == gid[None, :], 1.0 / HD, 0.0).astype(F32)


def _layer_weights(bufs, small, i):
    b = dict(zip(MATMUL_W, bufs))
    wi = _cols_joined(b["w_in"])

    def rows_joined(a):
        return a.reshape(a.shape[0] * a.shape[1], a.shape[2])

    return dict(
        w_ada=(b["w_ada"], "cols"), wi_main=(wi[:, :OFF_GATE], "mat"), wi_gate=(wi[:, OFF_GATE:], "mat"),
        w_attn_o=(_cols_joined(b["w_attn_o"]), "mat"), w_conf_out=(_cols_joined(b["w_conf_out"]), "mat"),
        w_sc_out=(_cols_joined(b["w_sc_out"]), "mat"), w_ffn_in=(b["w_ffn_in"], "cols"),
        w_mix_out=(rows_joined(b["w_mix_out"]), "mat"), w_ffn_out=(rows_joined(b["w_ffn_out"]), "mat"),
        conf_dw_w=small["conf_dw_w"][i], sc_dw_w=small["sc_dw_w"][i], conf_dw_b=small["conf_dw_b"][i][None],
        conf_ln_g=small["conf_ln_g"][i][None], conf_ln_b=small["conf_ln_b"][i][None],
        gq=jnp.tile(small["q_norm"][i], LANE // HD)[None], gk=jnp.tile(small["k_norm"][i], LANE // HD)[None])


def _layer_fwd(i, xs, h, mods, w, tabs, next_bufs, next_layer):
    cos_t, sin_t, g_mat = tabs
    n = f"l{i}_"
    sv = {"x_in": xs, "mods": mods, "h": h}
    sv["p_main"] = mm_nn(sv["h"], w["wi_main"], name=n + "p_main")
    sv["q"], sv["k"], sv["v"] = qkv_fwd(sv["p_main"], cos_t, sin_t, g_mat, w["gq"], w["gk"], n + "qkv")
    if next_bufs is None:
        sv["o"] = attn_fwd(sv["q"], sv["k"], sv["v"], n + "attn")
    else:
        sv["o"], next_bufs, _ = attn_fwd(sv["q"], sv["k"], sv["v"], n + "attn", comm=gather_ici(next_bufs))
    sv["yc"], sv["z"] = conv_fwd(sv["p_main"], w["conf_dw_w"], w["conf_dw_b"], w["sc_dw_w"], n + "conv")
    sv["hs"] = ln_silu_fwd(sv["yc"], w["conf_ln_g"], w["conf_ln_b"], n + "ln_silu")
    sv["merged"], sv["gates"], sv["ys"] = gate_mm_fwd(sv["h"], w["wi_gate"][0], sv["o"], sv["hs"], sv["z"],
                                                      w["w_attn_o"][0], w["w_conf_out"][0], w["w_sc_out"][0],
                                                      n + "gate_merge")
    sv["mixed"] = mm_nn(sv["merged"], w["w_mix_out"], name=n + "mix")
    sv["x1"], sv["h2"] = resid_norm_fwd(xs, sv["mixed"], mods, 2, mods, 3, 4, n + "resid1_norm2")
    if next_bufs is None:
        sv["f"], sv["u2"] = ffn_in_swiglu(sv["h2"], w["w_ffn_in"][0], n + "ffn_in")
    else:
        (sv["f"], sv["u2"]), next_bufs, _ = ffn_in_swiglu(sv["h2"], w["w_ffn_in"][0], n + "ffn_in",
                                                          comm=gather_d2d(next_bufs))
    sv["of"] = mm_nn(sv["f"], w["w_ffn_out"], name=n + "ffn_out")
    if next_layer is None:
        return gate_resid_fwd(sv["x1"], sv["of"], mods, 5, n + "resid2"), None, sv
    w_next, mods_next = next_layer(next_bufs)
    x2, h_next = resid_norm_fwd(sv["x1"], sv["of"], mods, 5, mods_next, 0, 1, n + "resid2_norm1")
    return x2, (h_next, w_next, mods_next), sv


def _layer_bwd(i, dx2, dof, dm5, sv, w, tabs, cs, pending, ids, below):
    cos_t, sin_t, g_mat = tabs
    n = f"l{i}b_"
    mods = sv["mods"]
    g = {}
    du = d_f_swiglu(dof, w["w_ffn_out"][0], sv["u2"], n + "d_f")
    g["w_ffn_out"] = mm_tn(sv["f"], dof, out_dtype=BF16, name=n + "dw_ffn_out").reshape(NCHIP, FH // NCHIP, D)
    if pending is None:
        dh2 = mm_nt(du, w["w_ffn_in"], name=n + "d_h2")
    else:
        dh2, _, swapped = mm_nt(du, w["w_ffn_in"], name=n + "d_h2", comm=rs_swap(pending))
        sends, arrs = zip(*[rs_add(g_, s_, ids[0], ids[1], f"{n}rs_add_{k}")
                            for k, g_, s_ in zip(MATMUL_W, pending, swapped)])
    g["w_ffn_in"] = mm_tn(sv["h2"], du, cols=True, out_dtype=BF16, name=n + "dw_ffn_in")
    dx1, dm34, dmixed, dm2 = norm_resid_bwd(sv["x1"], mods, dh2, dx2, 4, sv["mixed"], mods, 2, n + "norm2_resid1")
    dya, dyb, dys, dp_gate = d_merged_gate(dmixed, w["w_mix_out"][0], sv["gates"], sv["ys"], n + "d_merged")
    g["w_mix_out"] = mm_tn(sv["merged"], dmixed, out_dtype=BF16, name=n + "dw_mix").reshape(NCHIP, D // NCHIP, D)
    do = mm_nt(dya, w["w_attn_o"], out_dtype=BF16, name=n + "d_o")
    g["w_attn_o"] = _cols_split(mm_tn(sv["o"], dya, out_dtype=BF16, name=n + "dw_attn_o"))
    dhs = mm_nt(dyb, w["w_conf_out"], name=n + "d_hs")
    g["w_conf_out"] = _cols_split(mm_tn(sv["hs"], dyb, out_dtype=BF16, name=n + "dw_conf_out"))
    dz = mm_nt(dys, w["w_sc_out"], name=n + "d_z")
    g["w_sc_out"] = _cols_split(mm_tn(sv["z"], dys, out_dtype=BF16, name=n + "dw_sc_out"))
    dyc, g["conf_ln_g"], g["conf_ln_b"] = ln_silu_bwd(sv["yc"], w["conf_ln_g"], w["conf_ln_b"], dhs, n + "ln_silu")
    da, dg, dbg, dcg, dxs, g["conf_dw_w"], g["conf_dw_b"], g["sc_dw_w"] = conv_bwd(
        sv["p_main"], w["conf_dw_w"], w["sc_dw_w"], dyc, dz, n + "conv")
    done = None
    if pending is None:
        dq, dk, dv = attn_bwd(sv["q"], sv["k"], sv["v"], sv["o"], do, n + "attn")
        dp_qkv, dgqk = qkv_bwd(sv["p_main"], cos_t, sin_t, g_mat, w["gq"], w["gk"], dq, dk, dv, n + "qkv")
    else:
        (dq, dk, dv), arrs, _ = attn_bwd(sv["q"], sv["k"], sv["v"], sv["o"], do, n + "attn", comm=rs_ici(sends, arrs))
        (dp_qkv, dgqk), done, _ = qkv_bwd(sv["p_main"], cos_t, sin_t, g_mat, w["gq"], w["gk"], dq, dk, dv, n + "qkv",
                                          comm=rs_d2d(arrs))
    dp_main = jnp.concatenate([dp_qkv, da, dg, dbg, dcg, dxs], axis=1)
    dh = mm_nt(dp_main, w["wi_main"], name=n + "d_h_main")
    dh = mm_nt(dp_gate, w["wi_gate"], acc=dh, name=n + "d_h_gate")
    g["w_in"] = _cols_split(jnp.concatenate([mm_tn(sv["h"], dp_main, out_dtype=BF16, name=n + "dw_in_main"),
                                             mm_tn(sv["h"], dp_gate, out_dtype=BF16, name=n + "dw_in_gate")], axis=1))
    if below is None:
        dx_in, dm01 = norm_mod_bwd(sv["x_in"], mods, dh, dx1, 1, n + "norm1")
        dof_below = dm5_below = None
    else:
        dx_in, dm01, dof_below, dm5_below = norm_resid_bwd(sv["x_in"], mods, dh, dx1, 1, below["of"], below["mods"], 5,
                                                           n + "norm1_resid2")
    parts = jnp.concatenate([dm01, dm2, dm34, dm5], axis=2).reshape(2 * BL, NMOD * D)
    dmod, g["b_ada"] = dmod_assemble(parts, n + "dmod")
    g["w_ada"] = mm_tn(cs, dmod, cols=True, out_dtype=BF16, name=n + "dw_ada")
    g["dcs"] = mm_nt(dmod, w["w_ada"], name=n + "d_cs")
    g["q_norm"] = dgqk[0, :AW].reshape(NQ, HD).sum(axis=0)
    g["k_norm"] = dgqk[0, OFF_K:OFF_K + KVW].reshape(NKV, HD).sum(axis=0)
    return dx_in, dof_below, dm5_below, g, done


def kernel(x, c, ctx, c_ctx, w_ada, b_ada, w_in, q_norm, k_norm, w_attn_o, conf_dw_w, conf_dw_b, conf_ln_g, conf_ln_b, w_conf_out, sc_dw_w, w_sc_out, w_mix_out, w_ffn_in, w_ffn_out, loss_target, m_c_ctx, m_w_ada, m_b_ada, m_w_in, m_q_norm, m_k_norm, m_w_attn_o, m_conf_dw_w, m_conf_dw_b, m_conf_ln_g, m_conf_ln_b, m_w_conf_out, m_sc_dw_w, m_w_sc_out, m_w_mix_out, m_w_ffn_in, m_w_ffn_out, v_c_ctx, v_w_ada, v_b_ada, v_w_in, v_q_norm, v_k_norm, v_w_attn_o, v_conf_dw_w, v_conf_dw_b, v_conf_ln_g, v_conf_ln_b, v_w_conf_out, v_sc_dw_w, v_w_sc_out, v_w_mix_out, v_w_ffn_in, v_w_ffn_out):
    local = dict(c_ctx=c_ctx, w_ada=w_ada, b_ada=b_ada, w_in=w_in, q_norm=q_norm, k_norm=k_norm, w_attn_o=w_attn_o,
                 conf_dw_w=conf_dw_w, conf_dw_b=conf_dw_b, conf_ln_g=conf_ln_g, conf_ln_b=conf_ln_b,
                 w_conf_out=w_conf_out, sc_dw_w=sc_dw_w, w_sc_out=w_sc_out, w_mix_out=w_mix_out, w_ffn_in=w_ffn_in,
                 w_ffn_out=w_ffn_out)
    mom_m = dict(c_ctx=m_c_ctx, w_ada=m_w_ada, b_ada=m_b_ada, w_in=m_w_in, q_norm=m_q_norm, k_norm=m_k_norm,
                 w_attn_o=m_w_attn_o, conf_dw_w=m_conf_dw_w, conf_dw_b=m_conf_dw_b, conf_ln_g=m_conf_ln_g,
                 conf_ln_b=m_conf_ln_b, w_conf_out=m_w_conf_out, sc_dw_w=m_sc_dw_w, w_sc_out=m_w_sc_out,
                 w_mix_out=m_w_mix_out, w_ffn_in=m_w_ffn_in, w_ffn_out=m_w_ffn_out)
    mom_v = dict(c_ctx=v_c_ctx, w_ada=v_w_ada, b_ada=v_b_ada, w_in=v_w_in, q_norm=v_q_norm, k_norm=v_k_norm,
                 w_attn_o=v_w_attn_o, conf_dw_w=v_conf_dw_w, conf_dw_b=v_conf_dw_b, conf_ln_g=v_conf_ln_g,
                 conf_ln_b=v_conf_ln_b, w_conf_out=v_w_conf_out, sc_dw_w=v_sc_dw_w, w_sc_out=v_w_sc_out,
                 w_mix_out=v_w_mix_out, w_ffn_in=v_w_ffn_in, w_ffn_out=v_w_ffn_out)
    order = ("c_ctx", "w_ada", "b_ada", "w_in", "q_norm", "k_norm", "w_attn_o", "conf_dw_w", "conf_dw_b", "conf_ln_g",
             "conf_ln_b", "w_conf_out", "sc_dw_w", "w_sc_out", "w_mix_out", "w_ffn_in", "w_ffn_out")
    core = lax.axis_index("c").astype(jnp.int32)
    chip = (2 * lax.axis_index("x") + lax.axis_index("y")).astype(jnp.int32)

    own = [cast_layers(local[k], chip.reshape(1), "cast_" + k) for k in MATMUL_W]
    layer_bufs = [[own[w][l] for w in range(len(MATMUL_W))] for l in range(DEPTH)]
    conv_shapes = [local[k].shape for k in CONV_W]
    conv_all = all_gather8(_pack_rows([local[k] for k in CONV_W], 8), "gather_conv_taps")
    per_chip = [_unpack(conv_all[2 * s], conv_shapes) for s in range(NCHIP)]
    small = dict(b_ada=b_ada, q_norm=q_norm, k_norm=k_norm, conf_dw_b=conf_dw_b, conf_ln_g=conf_ln_g, conf_ln_b=conf_ln_b)
    for i, k in enumerate(CONV_W):
        small[k] = jnp.concatenate([per_chip[s][i] for s in range(NCHIP)], axis=2)

    loss_local, grad_x, sums, small_g = local_step(x, c, ctx, c_ctx, layer_bufs, small, loss_target,
                                                   ids=(core.reshape(1), chip.reshape(1)))
    loss = lax.psum(loss_local, ("x", "y", "c"))

    small_shapes = [small_g[k].shape for k in SMALL]
    small_sum = sum_leading(all_gather8(_pack_rows([small_g[k] for k in SMALL], 8), "gather_small_grads"), "small_sum")
    small_g = dict(zip(SMALL, _unpack(small_sum, small_shapes)))
    for k in CONV_W:
        width = local[k].shape[2]
        small_g[k] = lax.dynamic_slice_in_dim(small_g[k], chip * width, width, axis=2)

    grad, delta, new_m, new_v = {}, {}, {}, {}
    for k in order:
        if k in MATMUL_W:
            outs = [lax.empty(local[k].shape, F32) for _ in range(4)]
            for l in range(DEPTH):
                outs = adamw_layer(local[k], sums[l][MATMUL_W.index(k)], mom_m[k], mom_v[k], l, outs, f"adamw_{k}_{l}")
            grad[k], delta[k], new_m[k], new_v[k] = outs
            continue
        shp = local[k].shape
        view = (1, shp[0]) if len(shp) == 1 else (-1, shp[-1])
        d_, m_, v_ = adamw(local[k].reshape(view), small_g[k].reshape(view), mom_m[k].reshape(view),
                           mom_v[k].reshape(view), "adamw_" + k)
        grad[k], delta[k], new_m[k], new_v[k] = small_g[k], d_.reshape(shp), m_.reshape(shp), v_.reshape(shp)
    return (loss, grad_x, *[grad[k] for k in order], *[delta[k] for k in order], *[new_m[k] for k in order],
            *[new_v[k] for k in order])


def local_step(x, c, ctx, c_ctx, layer_bufs, small, loss_target, ids=None):
    tabs = _rope_tables() + (_group_matrix(),)
    distributed = ids is not None
    layer_bufs = list(layer_bufs)
    if distributed:
        layer_bufs[0], _ = comm_only("gather0_ici", gather_ici(layer_bufs[0]))
        layer_bufs[0], _ = comm_only("gather0_d2d", gather_d2d(layer_bufs[0]))

    cin = jnp.concatenate([c, c_ctx[None], jnp.zeros((8 - BL - 1, D), F32)], axis=0)
    cs = silu_rows(cin, "silu_c")
    xs = jnp.concatenate([ctx, x], axis=1).reshape(NROW, D)
    saved, layer_w = [], []

    def make_layer(i, bufs):
        w = _layer_weights(bufs, small, i)
        return w, mm_nn(cs, w["w_ada"], bias=small["b_ada"][i][None], name=f"l{i}_mod").reshape(8, 1, NMOD * D)

    w, mods = make_layer(0, layer_bufs[0])
    h = norm_mod_fwd(xs, mods, 0, 1, "l0_norm1")
    for i in range(DEPTH):
        last = i == DEPTH - 1

        def next_layer(bufs, i=i):
            if bufs is not None:
                layer_bufs[i + 1] = bufs
            return make_layer(i + 1, layer_bufs[i + 1])

        ahead = layer_bufs[i + 1] if distributed and not last else None
        xs, following, sv = _layer_fwd(i, xs, h, mods, w, tabs, ahead, None if last else next_layer)
        saved.append(sv)
        layer_w.append(w)
        if following is not None:
            h, w, mods = following
    dxs, loss_lanes = loss_fwd_bwd(xs, loss_target.reshape(BL * SEQ, D), "loss")
    loss_local = 0.5 * jnp.sum(loss_lanes) / D

    grads, sums = [None] * DEPTH, [None] * DEPTH
    pending = None
    dof, dm5 = gate_resid_bwd(dxs, saved[-1]["of"], saved[-1]["mods"], 5, "top_resid2")
    for i in reversed(range(DEPTH)):
        dxs, dof, dm5, grads[i], done = _layer_bwd(i, dxs, dof, dm5, saved[i], layer_w[i], tabs, cs, pending, ids,
                                                   saved[i - 1] if i > 0 else None)
        partial = [grads[i][k] for k in MATMUL_W]
        if distributed:
            if pending is not None:
                sums[i + 1] = done
            pending = partial
        else:
            sums[i] = partial
    if distributed:
        _, swapped = comm_only("rs0_swap", rs_swap(pending))
        sends, arrs = zip(*[rs_add(g_, s_, ids[0], ids[1], "rs0_add_" + k) for k, g_, s_ in zip(MATMUL_W, pending, swapped)])
        arrs, _ = comm_only("rs0_ici", rs_ici(sends, arrs))
        sums[0], _ = comm_only("rs0_d2d", rs_d2d(arrs))
    grad_x = dxs.reshape(BL, RE, D)[:, CTX:, :]
    dcin = silu_rows_bwd(cin, jnp.stack([grads[i]["dcs"] for i in range(DEPTH)]), "silu_c_bwd")

    def stack(key):
        return jnp.stack([grads[i][key] for i in range(DEPTH)])

    small_g = dict(c_ctx=dcin[BL], b_ada=stack("b_ada").reshape(DEPTH, NMOD * D), q_norm=stack("q_norm"),
                   k_norm=stack("k_norm"), conf_dw_b=stack("conf_dw_b").reshape(DEPTH, CW),
                   conf_ln_g=stack("conf_ln_g").reshape(DEPTH, CW), conf_ln_b=stack("conf_ln_b").reshape(DEPTH, CW),
                   conf_dw_w=stack("conf_dw_w"), sc_dw_w=stack("sc_dw_w"))
    return loss_local, grad_x, sums, small_g
```

```python
import functools
from typing import Any, Callable, NamedTuple, Sequence

import jax
import jax.numpy as jnp
from jax import lax
from jax.experimental import pallas as pl
from jax.experimental.pallas import tpu as pltpu

F32, BF16 = jnp.float32, jnp.bfloat16
HIGHEST = lax.Precision.HIGHEST

D = 1024
SEQ = 2048
CTX = 256
DEPTH = 4
BL = 4
GRID_W = 64
HD = 64
NQ = 8
NKV = 2
AW = NQ * HD
KVW = NKV * HD
CW = D // 2
CONF_K = 31
SC_K = 3
NMOD = 6
FH = -(-8 * D // (3 * 256)) * 256
EPS = 1e-6
ROPE_THETA = 10000.0
ATTN_SCALE = HD ** -0.5
OFF_K = AW
OFF_V = OFF_K + KVW
OFF_CONF = OFF_V + KVW
OFF_SC = OFF_CONF + 2 * CW
OFF_GATE = OFF_SC + 3 * CW
IN_W = OFF_GATE + 3 * D
QKVW = OFF_CONF
NCHIP = 4

ADAM_LR, ADAM_B1, ADAM_B2, ADAM_EPS, ADAM_WD, ADAM_STEP = 0.001, 0.9, 0.999, 1e-08, 0.01, 10

TM = CTX
RE = CTX + SEQ
TPE = RE // TM
NROW = BL * RE
NT = NROW // TM
LANE = 128
CB = CW // LANE
CONV_CH = 128
PADR = 16
VMEM_LIMIT = 52 * 1024 * 1024

MESH = pl.DeviceIdType.MESH
ANY = pl.BlockSpec(memory_space=pl.ANY)


class CommSpec(NamedTuple):
    ro: Sequence[Any]
    rw: Sequence[Any]
    new: Sequence[Any]
    nsem: int
    program: Callable


def _pcall(body, *, name, out_shape, grid=(), in_specs=None, out_specs=None, scratch=(), sem=None, comm=None):
    if not grid:
        return pl.pallas_call(body, name=name, out_shape=out_shape)
    if comm is None:
        params = pltpu.CompilerParams(dimension_semantics=sem, vmem_limit_bytes=VMEM_LIMIT)
        return pl.pallas_call(body, name=name, out_shape=out_shape, grid=grid, in_specs=in_specs, out_specs=out_specs,
                              scratch_shapes=list(scratch), compiler_params=params)

    single = not isinstance(out_shape, (tuple, list))
    out_shapes = (out_shape,) if single else tuple(out_shape)
    out_specs_t = (out_specs,) if single else tuple(out_specs)
    n_in, n_out, n_scr = len(in_specs), len(out_shapes), len(scratch)
    n_ro, n_rw, n_new = len(comm.ro), len(comm.rw), len(comm.new)

    def carrier(*refs):
        ins = refs[:n_in]
        ro_refs = refs[n_in:n_in + n_ro]
        o0 = n_in + n_ro + n_rw
        outs = refs[o0:o0 + n_out]
        rw_refs = refs[o0 + n_out:o0 + n_out + n_rw]
        new_refs = refs[o0 + n_out + n_rw:o0 + n_out + n_rw + n_new]
        s0 = o0 + n_out + n_rw + n_new
        scr = refs[s0:s0 + n_scr]
        send_sems, recv_sems = refs[s0 + n_scr:]
        first = functools.reduce(jnp.logical_and, [pl.program_id(a) == 0 for a in range(len(grid))])
        last = functools.reduce(jnp.logical_and, [pl.program_id(a) == grid[a] - 1 for a in range(len(grid))])
        starts, arrivals = comm.program(ro_refs, rw_refs, new_refs, send_sems, recv_sems)

        @pl.when(first)
        def _():
            for cp in starts:
                cp.start()

        body(*ins, *outs, *scr)

        @pl.when(last)
        def _():
            for cp in arrivals:
                cp.wait_recv()
            for cp in starts:
                cp.wait_send()

    def call(*args):
        rw_shapes = tuple(jax.ShapeDtypeStruct(a.shape, a.dtype) for a in comm.rw)
        res = pl.pallas_call(
            carrier, name=name, out_shape=out_shapes + rw_shapes + tuple(comm.new), grid=grid,
            in_specs=list(in_specs) + [ANY] * (n_ro + n_rw),
            out_specs=out_specs_t + (ANY,) * (n_rw + n_new),
            scratch_shapes=list(scratch) + [pltpu.SemaphoreType.DMA((comm.nsem,)), pltpu.SemaphoreType.DMA((comm.nsem,))],
            input_output_aliases={n_in + n_ro + i: n_out + i for i in range(n_rw)},
            compiler_params=pltpu.CompilerParams(dimension_semantics=("arbitrary",) * len(grid),
                                                 vmem_limit_bytes=VMEM_LIMIT))(*args, *comm.ro, *comm.rw)
        compute = res[0] if single else tuple(res[:n_out])
        return compute, list(res[n_out:n_out + n_rw]), list(res[n_out + n_rw:])

    return call


def comm_only(name, comm):
    n_ro, n_rw, n_new = len(comm.ro), len(comm.rw), len(comm.new)

    def body(*refs):
        ro_refs = refs[:n_ro]
        rw_refs = refs[n_ro + n_rw:n_ro + 2 * n_rw]
        new_refs = refs[n_ro + 2 * n_rw:n_ro + 2 * n_rw + n_new]
        send_sems, recv_sems = refs[n_ro + 2 * n_rw + n_new:]
        starts, arrivals = comm.program(ro_refs, rw_refs, new_refs, send_sems, recv_sems)
        for cp in starts:
            cp.start()
        for cp in arrivals:
            cp.wait_recv()
        for cp in starts:
            cp.wait_send()

    rw_shapes = tuple(jax.ShapeDtypeStruct(a.shape, a.dtype) for a in comm.rw)
    res = pl.pallas_call(body, name=name, out_shape=rw_shapes + tuple(comm.new), in_specs=[ANY] * (n_ro + n_rw),
                         out_specs=(ANY,) * (n_rw + n_new), input_output_aliases={n_ro + i: i for i in range(n_rw)},
                         scratch_shapes=[pltpu.SemaphoreType.DMA((comm.nsem,)), pltpu.SemaphoreType.DMA((comm.nsem,))])(
                             *comm.ro, *comm.rw)
    return list(res[:n_rw]), list(res[n_rw:])


def _pick(n, cands):
    for t in cands:
        if n % t == 0:
            return t
    return n


def _seg(t):
    return jnp.where(t % TPE == 0, BL, t // TPE)


def _slot(t):
    return 2 * (t // TPE) + jnp.where(t % TPE == 0, 0, 1)


def _sigmoid(x):
    return 1.0 / (1.0 + jnp.exp(-x))


MM_BUDGET = 40 * 1024 * 1024
N_TILE_CAP = 1664


def _tile(n, cap=N_TILE_CAP):
    if n <= cap:
        return n
    for t in range(cap - cap % LANE, 0, -LANE):
        if n % t == 0:
            return t
    return n


def _row_tile(m, bytes_of):
    for tm in (1024, 512, 256, 128):
        if m % tm == 0 and bytes_of(tm) <= MM_BUDGET:
            return tm
    return m


def _w_dims(w):
    arr, kind = w
    if kind == "cols":
        return arr.shape[1], NCHIP * arr.shape[2]
    return arr.shape


def _sz(dtype):
    return jnp.dtype(dtype).itemsize


def mm_nn(a, w, *, bias=None, out_dtype=F32, name):
    arr, kind = w
    m, k = a.shape
    _, n = _w_dims(w)
    tn = _tile(arr.shape[2]) if kind == "cols" else _tile(n)
    tm = _row_tile(m, lambda t: 2 * (t * k * _sz(a.dtype) + k * tn * 2 + t * tn * _sz(out_dtype)))
    if kind == "mat":
        b_spec = pl.BlockSpec((k, tn), lambda j, i: (0, j))
    else:
        per = arr.shape[2] // tn
        b_spec = pl.BlockSpec((None, k, tn), lambda j, i: (j // per, 0, j % per))
    has_bias = bias is not None

    def body(*refs):
        out = jnp.dot(refs[0][...].astype(BF16), refs[1][...].astype(BF16), preferred_element_type=F32)
        if has_bias:
            out = out + refs[2][...]
        refs[-1][...] = out.astype(out_dtype)

    in_specs = [pl.BlockSpec((tm, k), lambda j, i: (i, 0)), b_spec]
    args = [a, arr]
    if has_bias:
        in_specs.append(pl.BlockSpec((1, tn), lambda j, i: (0, j)))
        args.append(bias)
    return _pcall(body, name=name, out_shape=jax.ShapeDtypeStruct((m, n), out_dtype), grid=(n // tn, m // tm),
                  in_specs=in_specs, out_specs=pl.BlockSpec((tm, tn), lambda j, i: (i, j)),
                  sem=("parallel", "parallel"))(*args)


def mm_nt(a, w, *, acc=None, out_dtype=F32, name, comm=None):
    arr, kind = w
    kdim, _ = _w_dims(w)
    has_acc = acc is not None
    tk = _tile(kdim, 1408)
    if kind == "cols":
        c = arr.shape[2]
        m = a.shape[-2]
        if a.ndim == 3:
            a_spec = lambda t: pl.BlockSpec((None, t, c), lambda j, i, s: (s // 2, i, s % 2))
        else:
            a_spec = lambda t: pl.BlockSpec((t, c), lambda j, i, s: (i, s))
        tm = _row_tile(m, lambda t: 2 * (t * c * _sz(a.dtype) + tk * c * 2 + t * tk * _sz(out_dtype)) + t * tk * 4)

        def body(a_ref, b_ref, o_ref, acc_ref):
            s = pl.program_id(2)

            @pl.when(s == 0)
            def _():
                acc_ref[...] = jnp.zeros_like(acc_ref)

            acc_ref[...] += lax.dot_general(a_ref[...].astype(BF16), b_ref[...], (((1,), (1,)), ((), ())),
                                            preferred_element_type=F32)

            @pl.when(s == NCHIP - 1)
            def _():
                o_ref[...] = acc_ref[...].astype(out_dtype)

        return _pcall(body, name=name, out_shape=jax.ShapeDtypeStruct((m, kdim), out_dtype),
                      grid=(kdim // tk, m // tm, NCHIP),
                      in_specs=[a_spec(tm), pl.BlockSpec((None, tk, c), lambda j, i, s: (s, j, 0))],
                      out_specs=pl.BlockSpec((tm, tk), lambda j, i, s: (i, j)),
                      scratch=[pltpu.VMEM((tm, tk), F32)],
                      sem=("parallel", "parallel", "arbitrary"), comm=comm)(a, arr)

    m, n = a.shape
    tm = _row_tile(m, lambda t: 2 * (t * n * _sz(a.dtype) + tk * n * 2 + t * tk * (_sz(out_dtype) + 4 * has_acc))
                   + tk * n * 2)
    b_spec = pl.BlockSpec((tk, n), lambda j, i: (j, 0))

    def body(*refs):
        wt_ref = refs[-1]

        @pl.when(pl.program_id(1) == 0)
        def _():
            wt_ref[...] = refs[1][...].astype(BF16).T

        out = jnp.dot(refs[0][...].astype(BF16), wt_ref[...], preferred_element_type=F32)
        if has_acc:
            out = out + refs[2][...]
        refs[-2][...] = out.astype(out_dtype)

    in_specs = [pl.BlockSpec((tm, n), lambda j, i: (i, 0)), b_spec]
    args = [a, arr]
    if has_acc:
        in_specs.append(pl.BlockSpec((tm, tk), lambda j, i: (i, j)))
        args.append(acc)
    return _pcall(body, name=name, out_shape=jax.ShapeDtypeStruct((m, kdim), out_dtype), grid=(kdim // tk, m // tm),
                  in_specs=in_specs, out_specs=pl.BlockSpec((tm, tk), lambda j, i: (i, j)),
                  scratch=[pltpu.VMEM((n, tk), BF16)], sem=("parallel", "arbitrary"))(*args)


def mm_tn(a, b, *, cols=False, out_dtype=F32, name):
    rows, k = a.shape
    halves = b.ndim == 3
    n = 2 * b.shape[2] if halves else b.shape[1]
    odt = out_dtype
    if cols:
        c = n // NCHIP
        tn, tk = _tile(c), k
        per = c // tn
        out_spec = pl.BlockSpec((None, tk, tn), lambda i, j, r: (j // per, 0, j % per))
        out_shape = jax.ShapeDtypeStruct((NCHIP, k, c), odt)
    else:
        tn, tk = _tile(n), _tile(k, 1408)
        out_spec = pl.BlockSpec((tk, tn), lambda i, j, r: (i, j))
        out_shape = jax.ShapeDtypeStruct((k, n), odt)
    tr = _row_tile(rows, lambda t: 2 * (t * tk * _sz(a.dtype) + t * tn * _sz(b.dtype) + tk * tn * _sz(odt)) + tk * tn * 4)
    nsteps = rows // tr

    def body(*refs):
        a_ref, b_ref = refs[0], refs[1]
        o_ref, acc_ref = refs[-2], refs[-1]
        r = pl.program_id(2)

        @pl.when(r == 0)
        def _():
            acc_ref[...] = jnp.zeros_like(acc_ref)

        acc_ref[...] += lax.dot_general(a_ref[...].astype(BF16), b_ref[...].astype(BF16), (((0,), (0,)), ((), ())),
                                        preferred_element_type=F32)

        @pl.when(r == nsteps - 1)
        def _():
            o_ref[...] = acc_ref[...].astype(odt)

    if halves:
        per_half = (n // 2) // tn
        b_spec = pl.BlockSpec((None, tr, tn), lambda i, j, r: (j // per_half, r, j % per_half))
    else:
        b_spec = pl.BlockSpec((tr, tn), lambda i, j, r: (r, j))
    return _pcall(body, name=name, out_shape=out_shape, grid=(k // tk, n // tn, nsteps),
                  in_specs=[pl.BlockSpec((tr, tk), lambda i, j, r: (r, i)), b_spec], out_specs=out_spec,
                  scratch=[pltpu.VMEM((tk, tn), F32)], sem=("parallel", "parallel", "arbitrary"))(a, b)


def ffn_in_swiglu(h, w_in, name, comm=None):
    m, k = h.shape
    c = w_in.shape[2]
    tm = 512

    def body(h_ref, wa_ref, wb_ref, f_ref, u_ref):
        hv = h_ref[...]
        a = jnp.dot(hv, wa_ref[...], preferred_element_type=F32)
        b = jnp.dot(hv, wb_ref[...], preferred_element_type=F32)
        f_ref[...] = (a * _sigmoid(a) * b).astype(BF16)
        u_ref[0] = a.astype(BF16)
        u_ref[1] = b.astype(BF16)

    return _pcall(body, name=name,
                  out_shape=(jax.ShapeDtypeStruct((m, FH), BF16), jax.ShapeDtypeStruct((2, m, FH), BF16)),
                  grid=(2, m // tm),
                  in_specs=[pl.BlockSpec((tm, k), lambda j, i: (i, 0)),
                            pl.BlockSpec((None, k, c), lambda j, i: (j, 0, 0)),
                            pl.BlockSpec((None, k, c), lambda j, i: (2 + j, 0, 0))],
                  out_specs=(pl.BlockSpec((tm, c), lambda j, i: (i, j)), pl.BlockSpec((2, tm, c), lambda j, i: (0, i, j))),
                  sem=("parallel", "parallel"), comm=comm)(h, w_in, w_in)


def d_f_swiglu(dof, w_out, u2, name):
    m, k = dof.shape
    c = FH // 2
    tm = 512

    def body(d_ref, w_ref, u_ref, du_ref, wt_ref):
        @pl.when(pl.program_id(1) == 0)
        def _():
            wt_ref[...] = w_ref[...].T

        df = jnp.dot(d_ref[...], wt_ref[...], preferred_element_type=F32)
        a, b = u_ref[0].astype(F32), u_ref[1].astype(F32)
        sg = _sigmoid(a)
        du_ref[0] = (df * b * (sg * (1.0 + a * (1.0 - sg)))).astype(BF16)
        du_ref[1] = (df * a * sg).astype(BF16)

    ublk = pl.BlockSpec((2, tm, c), lambda j, i: (0, i, j))
    return _pcall(body, name=name, out_shape=jax.ShapeDtypeStruct((2, m, FH), BF16), grid=(2, m // tm),
                  in_specs=[pl.BlockSpec((tm, k), lambda j, i: (i, 0)), pl.BlockSpec((c, k), lambda j, i: (j, 0)), ublk],
                  out_specs=ublk, scratch=[pltpu.VMEM((k, c), w_out.dtype)], sem=("parallel", "arbitrary"))(dof, w_out, u2)


GATE_TN = 512


def gate_mm_fwd(h, wi_gate, o, hs, z, wo, wc, ws, name):
    m, k = h.shape
    tm, tn = 512, min(GATE_TN, D)
    nj = D // tn

    def body(h_ref, g0_ref, g1_ref, g2_ref, o_ref, hs_ref, z_ref, wo_ref, wc_ref, ws_ref, m_ref, g_ref, y_ref):
        hv = h_ref[...]
        acc = jnp.zeros((tm, tn), F32)
        for g, (gw_ref, x_ref, w_ref) in enumerate(((g0_ref, o_ref, wo_ref), (g1_ref, hs_ref, wc_ref),
                                                    (g2_ref, z_ref, ws_ref))):
            gate = _sigmoid(jnp.dot(hv, gw_ref[...], preferred_element_type=F32))
            y = jnp.dot(x_ref[...], w_ref[...], preferred_element_type=F32)
            acc += gate * y
            g_ref[g] = gate.astype(BF16)
            y_ref[g] = y.astype(BF16)
        m_ref[...] = acc.astype(BF16)

    def gate_w(g):
        return pl.BlockSpec((k, tn), lambda j, i: (0, g * nj + j))

    def branch(width):
        return pl.BlockSpec((tm, width), lambda j, i: (i, 0))

    def branch_w(width):
        return pl.BlockSpec((width, tn), lambda j, i: (0, j))

    stacked = pl.BlockSpec((3, tm, tn), lambda j, i: (0, i, j))
    sds3 = jax.ShapeDtypeStruct((3, m, D), BF16)
    return _pcall(body, name=name, out_shape=(jax.ShapeDtypeStruct((m, D), BF16), sds3, sds3), grid=(nj, m // tm),
                  in_specs=[pl.BlockSpec((tm, k), lambda j, i: (i, 0)), gate_w(0), gate_w(1), gate_w(2),
                            branch(o.shape[1]), branch(hs.shape[1]), branch(z.shape[1]),
                            branch_w(wo.shape[0]), branch_w(wc.shape[0]), branch_w(ws.shape[0])],
                  out_specs=(pl.BlockSpec((tm, tn), lambda j, i: (i, j)), stacked, stacked),
                  sem=("parallel", "parallel"))(h, wi_gate, wi_gate, wi_gate, o, hs, z, wo, wc, ws)


def d_merged_gate(dmixed, w_mix, gates, ys, name):
    m, k = dmixed.shape
    tm = 256

    def body(d_ref, w_ref, g_ref, y_ref, da_ref, db_ref, ds_ref, dp_ref, wt_ref):
        @pl.when(pl.program_id(0) == 0)
        def _():
            wt_ref[...] = w_ref[...].T

        dm = jnp.dot(d_ref[...], wt_ref[...], preferred_element_type=F32)
        for g, dy_ref in enumerate((da_ref, db_ref, ds_ref)):
            gate = g_ref[g].astype(F32)
            dy_ref[...] = (dm * gate).astype(BF16)
            dp_ref[:, g * D:(g + 1) * D] = (dm * y_ref[g].astype(F32) * gate * (1.0 - gate)).astype(BF16)

    stacked = pl.BlockSpec((3, tm, D), lambda i: (0, i, 0))
    row = pl.BlockSpec((tm, D), lambda i: (i, 0))
    sds = jax.ShapeDtypeStruct((m, D), BF16)
    return _pcall(body, name=name, out_shape=(sds, sds, sds, jax.ShapeDtypeStruct((m, 3 * D), BF16)), grid=(m // tm,),
                  in_specs=[pl.BlockSpec((tm, k), lambda i: (i, 0)), pl.BlockSpec((D, k), lambda i: (0, 0)),
                            stacked, stacked],
                  out_specs=(row, row, row, pl.BlockSpec((tm, 3 * D), lambda i: (i, 0))),
                  scratch=[pltpu.VMEM((k, D), w_mix.dtype)], sem=("arbitrary",))(dmixed, w_mix, gates, ys)


def _mods_spec():
    return pl.BlockSpec((1, 1, NMOD * D), lambda t: (_seg(t), 0, 0))


def _rows(width):
    return pl.BlockSpec((TM, width), lambda t: (t, 0))


def norm_mod_fwd(x, mods, k_sh, k_sc, name):
    def body(x_ref, m_ref, h_ref):
        x = x_ref[...]
        r = lax.rsqrt(jnp.mean(x * x, axis=-1, keepdims=True) + EPS)
        sh = m_ref[0, :, k_sh * D:(k_sh + 1) * D]
        sc = m_ref[0, :, k_sc * D:(k_sc + 1) * D]
        h_ref[...] = (x * r * (1.0 + sc) + sh).astype(BF16)

    return _pcall(body, name=name, out_shape=jax.ShapeDtypeStruct((NROW, D), BF16), grid=(NT,),
                  in_specs=[_rows(D), _mods_spec()], out_specs=_rows(D), sem=("parallel",))(x, mods)


def _accumulate_slot(t, ref, part):
    first = (t % TPE) <= 1

    @pl.when(first)
    def _():
        ref[0] = part

    @pl.when(jnp.logical_not(first))
    def _():
        ref[0] += part


def norm_mod_bwd(x, mods, dh, dres, k_sc, name):
    def body(x_ref, m_ref, dh_ref, dres_ref, dx_ref, dp_ref):
        t = pl.program_id(0)
        x = x_ref[...]
        r = lax.rsqrt(jnp.mean(x * x, axis=-1, keepdims=True) + EPS)
        xn = x * r
        sc = m_ref[0, :, k_sc * D:(k_sc + 1) * D]
        dh = dh_ref[...]
        dxn = dh * (1.0 + sc)
        dx_ref[...] = r * (dxn - xn * jnp.mean(dxn * xn, axis=-1, keepdims=True)) + dres_ref[...]
        part = jnp.concatenate([jnp.sum(dh, axis=0, keepdims=True), jnp.sum(dh * xn, axis=0, keepdims=True)], axis=1)
        _accumulate_slot(t, dp_ref, part)

    return _pcall(body, name=name,
                  out_shape=(jax.ShapeDtypeStruct((NROW, D), F32), jax.ShapeDtypeStruct((2 * BL, 1, 2 * D), F32)),
                  grid=(NT,), in_specs=[_rows(D), _mods_spec(), _rows(D), _rows(D)],
                  out_specs=(_rows(D), pl.BlockSpec((1, 1, 2 * D), lambda t: (_slot(t), 0, 0))),
                  sem=("arbitrary",))(x, mods, dh, dres)


def resid_norm_fwd(x, y, mods_g, k_g, mods_n, k_sh, k_sc, name):
    def body(x_ref, y_ref, mg_ref, mn_ref, x1_ref, h_ref):
        x1 = x_ref[...] + mg_ref[0, :, k_g * D:(k_g + 1) * D] * y_ref[...]
        x1_ref[...] = x1
        r = lax.rsqrt(jnp.mean(x1 * x1, axis=-1, keepdims=True) + EPS)
        sh = mn_ref[0, :, k_sh * D:(k_sh + 1) * D]
        sc = mn_ref[0, :, k_sc * D:(k_sc + 1) * D]
        h_ref[...] = (x1 * r * (1.0 + sc) + sh).astype(BF16)

    return _pcall(body, name=name,
                  out_shape=(jax.ShapeDtypeStruct((NROW, D), F32), jax.ShapeDtypeStruct((NROW, D), BF16)), grid=(NT,),
                  in_specs=[_rows(D), _rows(D), _mods_spec(), _mods_spec()], out_specs=(_rows(D), _rows(D)),
                  sem=("parallel",))(x, y, mods_g, mods_n)


def norm_resid_bwd(x, mods_n, dh, dres, k_sc, y, mods_g, k_g, name):
    def body(x_ref, mn_ref, dh_ref, dres_ref, y_ref, mg_ref, dx_ref, dpn_ref, dy_ref, dpg_ref):
        t = pl.program_id(0)
        x = x_ref[...]
        r = lax.rsqrt(jnp.mean(x * x, axis=-1, keepdims=True) + EPS)
        xn = x * r
        sc = mn_ref[0, :, k_sc * D:(k_sc + 1) * D]
        dh = dh_ref[...]
        dxn = dh * (1.0 + sc)
        dx = r * (dxn - xn * jnp.mean(dxn * xn, axis=-1, keepdims=True)) + dres_ref[...]
        dx_ref[...] = dx
        dy_ref[...] = (dx * mg_ref[0, :, k_g * D:(k_g + 1) * D]).astype(BF16)
        part = jnp.concatenate([jnp.sum(dh, axis=0, keepdims=True), jnp.sum(dh * xn, axis=0, keepdims=True)], axis=1)
        _accumulate_slot(t, dpn_ref, part)
        _accumulate_slot(t, dpg_ref, jnp.sum(dx * y_ref[...], axis=0, keepdims=True))

    def slot(width):
        return pl.BlockSpec((1, 1, width), lambda t: (_slot(t), 0, 0))

    return _pcall(body, name=name,
                  out_shape=(jax.ShapeDtypeStruct((NROW, D), F32), jax.ShapeDtypeStruct((2 * BL, 1, 2 * D), F32),
                             jax.ShapeDtypeStruct((NROW, D), BF16), jax.ShapeDtypeStruct((2 * BL, 1, D), F32)),
                  grid=(NT,), in_specs=[_rows(D), _mods_spec(), _rows(D), _rows(D), _rows(D), _mods_spec()],
                  out_specs=(_rows(D), slot(2 * D), _rows(D), slot(D)), sem=("arbitrary",))(x, mods_n, dh, dres, y, mods_g)


def gate_resid_fwd(x, y, mods, k_g, name):
    def body(x_ref, y_ref, m_ref, o_ref):
        o_ref[...] = x_ref[...] + m_ref[0, :, k_g * D:(k_g + 1) * D] * y_ref[...]

    return _pcall(body, name=name, out_shape=jax.ShapeDtypeStruct((NROW, D), F32), grid=(NT,),
                  in_specs=[_rows(D), _rows(D), _mods_spec()], out_specs=_rows(D), sem=("parallel",))(x, y, mods)


def gate_resid_bwd(dx, y, mods, k_g, name):
    def body(dx_ref, y_ref, m_ref, dy_ref, dp_ref):
        t = pl.program_id(0)
        dx = dx_ref[...]
        dy_ref[...] = (dx * m_ref[0, :, k_g * D:(k_g + 1) * D]).astype(BF16)
        _accumulate_slot(t, dp_ref, jnp.sum(dx * y_ref[...], axis=0, keepdims=True))

    return _pcall(body, name=name,
                  out_shape=(jax.ShapeDtypeStruct((NROW, D), BF16), jax.ShapeDtypeStruct((2 * BL, 1, D), F32)),
                  grid=(NT,), in_specs=[_rows(D), _rows(D), _mods_spec()],
                  out_specs=(_rows(D), pl.BlockSpec((1, 1, D), lambda t: (_slot(t), 0, 0))),
                  sem=("arbitrary",))(dx, y, mods)


def _swap16(y, lo16):
    return jnp.where(lo16, pltpu.roll(y, LANE - 16, 1), pltpu.roll(y, 16, 1))


def _group_mean(v, g_mat):
    return jnp.dot(v, g_mat, precision=HIGHEST, preferred_element_type=F32)


def qkv_fwd(p_main, cos_t, sin_t, g_mat, gq, gk, name):
    def body(p_ref, cos_ref, sin_ref, g_ref, gq_ref, gk_ref, q_ref, k_ref, v_ref):
        cos, sin, g_mat_v = cos_ref[...], sin_ref[...], g_ref[...]
        lo16 = (lax.broadcasted_iota(jnp.int32, (TM, LANE), 1) % 32) < 16

        def block(xb, g):
            r = lax.rsqrt(_group_mean(xb * xb, g_mat_v) + EPS)
            y = xb * r * g
            return y * cos + _swap16(y, lo16) * sin

        for j in range(AW // LANE):
            q_ref[:, j * LANE:(j + 1) * LANE] = (block(p_ref[:, j * LANE:(j + 1) * LANE], gq_ref[...])
                                                 * ATTN_SCALE).astype(BF16)
        lo = lax.broadcasted_iota(jnp.int32, (TM, LANE), 1) < HD
        for src, dst_ref in ((block(p_ref[:, OFF_K:OFF_K + LANE], gk_ref[...]), k_ref), (p_ref[:, OFF_V:OFF_V + LANE], v_ref)):
            swapped = pltpu.roll(src, HD, 1)
            dst_ref[:, 0:LANE] = jnp.where(lo, src, swapped).astype(BF16)
            dst_ref[:, LANE:2 * LANE] = jnp.where(lo, swapped, src).astype(BF16)

    tab = pl.BlockSpec((TM, LANE), lambda t: (t % TPE, 0))
    small = pl.BlockSpec((1, LANE), lambda t: (0, 0))
    return _pcall(body, name=name,
                  out_shape=(jax.ShapeDtypeStruct((NROW, AW), BF16), jax.ShapeDtypeStruct((NROW, 2 * KVW), BF16),
                             jax.ShapeDtypeStruct((NROW, 2 * KVW), BF16)),
                  grid=(NT,),
                  in_specs=[_rows(QKVW), tab, tab, pl.BlockSpec((LANE, LANE), lambda t: (0, 0)), small, small],
                  out_specs=(_rows(AW), _rows(2 * KVW), _rows(2 * KVW)),
                  sem=("parallel",))(p_main, cos_t, sin_t, g_mat, gq, gk)


def qkv_bwd(p_main, cos_t, sin_t, g_mat, gq, gk, dq, dk, dv, name, comm=None):
    def body(p_ref, cos_ref, sin_ref, g_ref, gq_ref, gk_ref, dq_ref, dk_ref, dv_ref, dp_ref, dg_ref):
        t = pl.program_id(0)
        cos, sin, g_mat_v = cos_ref[...], sin_ref[...], g_ref[...]
        lo16 = (lax.broadcasted_iota(jnp.int32, (TM, LANE), 1) % 32) < 16

        def block(xb, g, dyr):
            r = lax.rsqrt(_group_mean(xb * xb, g_mat_v) + EPS)
            xn = xb * r
            dy = dyr * cos + _swap16(dyr * sin, lo16)
            dgl = jnp.sum(dy * xn, axis=0, keepdims=True)
            dxn = dy * g
            return r * (dxn - xn * _group_mean(dxn * xn, g_mat_v)), dgl

        parts = []
        for j in range(AW // LANE):
            sl = slice(j * LANE, (j + 1) * LANE)
            dxb, dgl = block(p_ref[:, sl], gq_ref[...], dq_ref[:, sl] * ATTN_SCALE)
            dp_ref[:, sl] = dxb.astype(BF16)
            parts.append(dgl)
        lo = lax.broadcasted_iota(jnp.int32, (TM, LANE), 1) < HD

        def fold(d_ref):
            d0, d1 = d_ref[:, 0:LANE], d_ref[:, LANE:2 * LANE]
            return jnp.where(lo, d0 + pltpu.roll(d0, HD, 1), d1 + pltpu.roll(d1, HD, 1))

        dxb, dgl = block(p_ref[:, OFF_K:OFF_K + LANE], gk_ref[...], fold(dk_ref))
        dp_ref[:, OFF_K:OFF_K + LANE] = dxb.astype(BF16)
        parts.append(dgl)
        parts.append(jnp.zeros((1, LANE), F32))
        dp_ref[:, OFF_V:OFF_V + LANE] = fold(dv_ref).astype(BF16)
        part = jnp.concatenate(parts, axis=1)

        @pl.when(t == 0)
        def _():
            dg_ref[...] = part

        @pl.when(t != 0)
        def _():
            dg_ref[...] += part

    tab = pl.BlockSpec((TM, LANE), lambda t: (t % TPE, 0))
    small = pl.BlockSpec((1, LANE), lambda t: (0, 0))
    return _pcall(body, name=name,
                  out_shape=(jax.ShapeDtypeStruct((NROW, QKVW), BF16), jax.ShapeDtypeStruct((1, QKVW), F32)),
                  grid=(NT,),
                  in_specs=[_rows(QKVW), tab, tab, pl.BlockSpec((LANE, LANE), lambda t: (0, 0)), small, small,
                            _rows(AW), _rows(2 * KVW), _rows(2 * KVW)],
                  out_specs=(_rows(QKVW), pl.BlockSpec((1, QKVW), lambda t: (0, 0))),
                  sem=("arbitrary",), comm=comm)(p_main, cos_t, sin_t, g_mat, gq, gk, dq, dk, dv)


def _layer_norm_parts(yc):
    mu = jnp.mean(yc, axis=-1, keepdims=True)
    xc = yc - mu
    rs = lax.rsqrt(jnp.mean(xc * xc, axis=-1, keepdims=True) + EPS)
    return xc * rs, rs


def ln_silu_fwd(yc, g, b, name):
    def body(y_ref, g_ref, b_ref, o_ref):
        nrm, _ = _layer_norm_parts(y_ref[...])
        ln = nrm * g_ref[...] + b_ref[...]
        o_ref[...] = (ln * _sigmoid(ln)).astype(BF16)

    vec = pl.BlockSpec((1, CW), lambda t: (0, 0))
    return _pcall(body, name=name, out_shape=jax.ShapeDtypeStruct((NROW, CW), BF16), grid=(NT,),
                  in_specs=[_rows(CW), vec, vec], out_specs=_rows(CW), sem=("parallel",))(yc, g, b)


def ln_silu_bwd(yc, g, b, dhs, name):
    def body(y_ref, g_ref, b_ref, dh_ref, dy_ref, dg_ref, db_ref):
        t = pl.program_id(0)
        nrm, rs = _layer_norm_parts(y_ref[...])
        ln = nrm * g_ref[...] + b_ref[...]
        sg = _sigmoid(ln)
        dln = dh_ref[...] * (sg * (1.0 + ln * (1.0 - sg)))
        dn = dln * g_ref[...]
        dy_ref[...] = rs * (dn - jnp.mean(dn, axis=-1, keepdims=True)
                            - nrm * jnp.mean(dn * nrm, axis=-1, keepdims=True))
        pg = jnp.sum(dln * nrm, axis=0, keepdims=True)
        pb = jnp.sum(dln, axis=0, keepdims=True)

        @pl.when(t == 0)
        def _():
            dg_ref[...] = pg
            db_ref[...] = pb

        @pl.when(t != 0)
        def _():
            dg_ref[...] += pg
            db_ref[...] += pb

    vec = pl.BlockSpec((1, CW), lambda t: (0, 0))
    return _pcall(body, name=name,
                  out_shape=(jax.ShapeDtypeStruct((NROW, CW), F32), jax.ShapeDtypeStruct((1, CW), F32),
                             jax.ShapeDtypeStruct((1, CW), F32)),
                  grid=(NT,), in_specs=[_rows(CW), vec, vec, _rows(CW)], out_specs=(_rows(CW), vec, vec),
                  sem=("arbitrary",))(yc, g, b, dhs)


def loss_fwd_bwd(y, target, name):
    def body(y_ref, t_ref, dy_ref, l_ref):
        t = pl.program_id(0)
        latent = (t % TPE) != 0
        err = jnp.where(latent, y_ref[...] - t_ref[...], 0.0)
        dy_ref[...] = err * (1.0 / D)
        part = jnp.sum(err * err, axis=0, keepdims=True)

        @pl.when(t == 0)
        def _():
            l_ref[...] = part

        @pl.when(t != 0)
        def _():
            l_ref[...] += part

    tgt = pl.BlockSpec((TM, D), lambda t: ((t // TPE) * (TPE - 1) + jnp.maximum(t % TPE - 1, 0), 0))
    return _pcall(body, name=name,
                  out_shape=(jax.ShapeDtypeStruct((NROW, D), F32), jax.ShapeDtypeStruct((1, D), F32)),
                  grid=(NT,), in_specs=[_rows(D), tgt], out_specs=(_rows(D), pl.BlockSpec((1, D), lambda t: (0, 0))),
                  sem=("arbitrary",))(y, target)


QB_PER_KV = AW // LANE // NKV


def _softmax_parts(qm, k):
    s = lax.dot_general(qm, k, (((1,), (1,)), ((), ())), preferred_element_type=F32)
    e = jnp.exp(s - jnp.max(s, axis=-1, keepdims=True))
    return e, 1.0 / jnp.sum(e, axis=-1, keepdims=True)


def _lane_halves():
    lo = lax.broadcasted_iota(jnp.int32, (TM, LANE), 1) < HD
    return lo, jnp.logical_not(lo)


def _stack_heads(x, halves):
    zero = jnp.zeros_like(x)
    return jnp.concatenate([jnp.where(halves[0], x, zero), jnp.where(halves[1], x, zero)], axis=0)


def attn_fwd(q, k, v, name, comm=None):
    def body(q_ref, k_ref, v_ref, o_ref):
        t = pl.program_id(2)
        halves = _lane_halves()

        def run(nk):
            kv, vv = k_ref[0:nk, :], v_ref[0:nk, :]
            for j in range(QB_PER_KV):
                lanes = slice(j * LANE, (j + 1) * LANE)
                e, rinv = _softmax_parts(_stack_heads(q_ref[:, lanes], halves), kv)
                out = jnp.dot(e.astype(BF16), vv, preferred_element_type=F32) * rinv
                o_ref[:, lanes] = jnp.where(halves[0], out[0:TM], out[TM:2 * TM]).astype(BF16)

        @pl.when(t == 0)
        def _():
            run(CTX)

        @pl.when(t != 0)
        def _():
            run(RE)

    qs = pl.BlockSpec((TM, QB_PER_KV * LANE), lambda b, h, t: (b * TPE + t, h))
    ks = pl.BlockSpec((RE, LANE), lambda b, h, t: (b, h))
    return _pcall(body, name=name, out_shape=jax.ShapeDtypeStruct((NROW, AW), BF16), grid=(BL, NKV, TPE),
                  in_specs=[qs, ks, ks], out_specs=qs, sem=("parallel",) * 3, comm=comm)(q, k, v)


def attn_bwd(q, k, v, o, do, name, comm=None):
    def body(q_ref, k_ref, v_ref, o_ref, do_ref, dq_ref, dk_ref, dv_ref):
        t = pl.program_id(2)
        halves = _lane_halves()

        @pl.when(t == 0)
        def _():
            dk_ref[...] = jnp.zeros_like(dk_ref)
            dv_ref[...] = jnp.zeros_like(dv_ref)

        def run(nk):
            kv, vv = k_ref[0:nk, :], v_ref[0:nk, :]
            dks, dvs = [], []
            for j in range(QB_PER_KV):
                lanes = slice(j * LANE, (j + 1) * LANE)
                q2, do2 = _stack_heads(q_ref[:, lanes], halves), _stack_heads(do_ref[:, lanes], halves)
                ov = o_ref[:, lanes].astype(F32)
                delta = jnp.sum(do2.astype(F32) * jnp.concatenate([ov, ov], axis=0), axis=-1, keepdims=True)
                e, rinv = _softmax_parts(q2, kv)
                p = e * rinv
                dvs.append(lax.dot_general(p.astype(BF16), do2, (((0,), (0,)), ((), ())), preferred_element_type=F32))
                dp = lax.dot_general(do2, vv, (((1,), (1,)), ((), ())), preferred_element_type=F32)
                ds = (p * (dp - delta)).astype(BF16)
                dq = jnp.dot(ds, kv, preferred_element_type=F32)
                dks.append(lax.dot_general(ds, q2, (((0,), (0,)), ((), ())), preferred_element_type=F32))
                dq_ref[:, lanes] = jnp.where(halves[0], dq[0:TM], dq[TM:2 * TM])
            dv_ref[0:nk, :] += functools.reduce(jnp.add, dvs)
            dk_ref[0:nk, :] += functools.reduce(jnp.add, dks)

        @pl.when(t == 0)
        def _():
            run(CTX)

        @pl.when(t != 0)
        def _():
            run(RE)

    qs = pl.BlockSpec((TM, QB_PER_KV * LANE), lambda b, h, t: (b * TPE + t, h))
    ks = pl.BlockSpec((RE, LANE), lambda b, h, t: (b, h))
    return _pcall(body, name=name,
                  out_shape=(jax.ShapeDtypeStruct((NROW, AW), F32), jax.ShapeDtypeStruct((NROW, 2 * KVW), F32),
                             jax.ShapeDtypeStruct((NROW, 2 * KVW), F32)),
                  grid=(BL, NKV, TPE), in_specs=[qs, ks, ks, qs, qs], out_specs=(qs, ks, ks),
                  sem=("parallel", "parallel", "arbitrary"), comm=comm)(q, k, v, o, do)


CONV_SEGS = ((0, CTX), (CTX, SEQ))


def _p_block(col0):
    return pl.BlockSpec((RE, LANE), lambda cb, b: (b, col0 // LANE + cb))


def _conv_io(width):
    return pl.BlockSpec((RE, LANE), lambda cb, b: (b, cb))


def _taps(n):
    return pl.BlockSpec((n, LANE), lambda cb, b: (0, cb))


def _fill_pad(pad_ref, length, values):
    pad_ref[0:PADR, :] = jnp.zeros((PADR, LANE), F32)
    pad_ref[PADR + length:2 * PADR + length, :] = jnp.zeros((PADR, LANE), F32)
    pad_ref[PADR:PADR + length, :] = values


def _conv_chunk(pad_ref, w_ref, ntap, c0, first_row):
    acc = jnp.zeros((CONV_CH, LANE), F32)
    for kk in range(ntap):
        r0 = c0 + first_row(kk)
        acc += w_ref[kk:kk + 1, :] * pad_ref[r0:r0 + CONV_CH, :]
    return acc


def conv_fwd(p_main, wdw, bdw, w3, name):
    def body(a_ref, g_ref, bg_ref, cg_ref, xs_ref, w_ref, b_ref, w3_ref, yc_ref, z_ref, pad_ref):
        for off, length in CONV_SEGS:
            rows = slice(off, off + length)
            _fill_pad(pad_ref, length, a_ref[rows, :] * _sigmoid(g_ref[rows, :]))
            for c0 in range(0, length, CONV_CH):
                acc = _conv_chunk(pad_ref, w_ref, CONF_K, c0, lambda kk: PADR + kk - CONF_K // 2)
                yc_ref[off + c0:off + c0 + CONV_CH, :] = acc + b_ref[...]
            pad_ref[PADR:PADR + length, :] = cg_ref[rows, :] * xs_ref[rows, :]
            for c0 in range(0, length, CONV_CH):
                acc = _conv_chunk(pad_ref, w3_ref, SC_K, c0, lambda kk: PADR + kk - SC_K // 2)
                z_ref[off + c0:off + c0 + CONV_CH, :] = (bg_ref[off + c0:off + c0 + CONV_CH, :] * acc).astype(BF16)

    return _pcall(body, name=name,
                  out_shape=(jax.ShapeDtypeStruct((NROW, CW), F32), jax.ShapeDtypeStruct((NROW, CW), BF16)),
                  grid=(CB, BL),
                  in_specs=[_p_block(OFF_CONF), _p_block(OFF_CONF + CW), _p_block(OFF_SC), _p_block(OFF_SC + CW),
                            _p_block(OFF_SC + 2 * CW), _taps(CONF_K), _taps(1), _taps(SC_K)],
                  out_specs=(_conv_io(CW), _conv_io(CW)),
                  scratch=[pltpu.VMEM((SEQ + 2 * PADR, LANE), F32)],
                  sem=("parallel", "parallel"))(p_main, p_main, p_main, p_main, p_main, wdw, bdw, w3)


def _tap_grad(pad_ref, d_ref, off, length, first_row):
    acc = jnp.zeros((8, LANE), F32)
    for c0 in range(0, length, CONV_CH):
        prod = d_ref[off + c0:off + c0 + CONV_CH, :] * pad_ref[c0 + first_row:c0 + first_row + CONV_CH, :]
        acc += jnp.sum(prod.reshape(CONV_CH // 8, 8, LANE), axis=0)
    return jnp.sum(acc, axis=0, keepdims=True)


def conv_bwd(p_main, wdw, w3, dyc, dz, name):
    def body(a_ref, g_ref, bg_ref, cg_ref, xs_ref, w_ref, w3_ref, dyc_ref, dz_ref,
             da_ref, dg_ref, dbg_ref, dcg_ref, dxs_ref, dw_ref, db_ref, dw3_ref, pad_x, pad_d, dconv_ref):
        b = pl.program_id(1)

        @pl.when(b == 0)
        def _():
            dw_ref[...] = jnp.zeros_like(dw_ref)
            db_ref[...] = jnp.zeros_like(db_ref)
            dw3_ref[...] = jnp.zeros_like(dw3_ref)

        db_ref[...] += jnp.sum(dyc_ref[...], axis=0, keepdims=True)
        for off, length in CONV_SEGS:
            rows = slice(off, off + length)
            _fill_pad(pad_x, length, a_ref[rows, :] * _sigmoid(g_ref[rows, :]))
            _fill_pad(pad_d, length, dyc_ref[rows, :])
            for kk in range(CONF_K):
                dw_ref[kk:kk + 1, :] += _tap_grad(pad_x, dyc_ref, off, length, PADR + kk - CONF_K // 2)
            for c0 in range(0, length, CONV_CH):
                dh = _conv_chunk(pad_d, w_ref, CONF_K, c0, lambda kk: PADR + CONF_K // 2 - kk)
                ch = slice(off + c0, off + c0 + CONV_CH)
                sg = _sigmoid(g_ref[ch, :])
                da_ref[ch, :] = (dh * sg).astype(BF16)
                dg_ref[ch, :] = (dh * a_ref[ch, :] * sg * (1.0 - sg)).astype(BF16)
            pad_x[PADR:PADR + length, :] = cg_ref[rows, :] * xs_ref[rows, :]
            dconv_ref[rows, :] = dz_ref[rows, :] * bg_ref[rows, :]
            pad_d[PADR:PADR + length, :] = dconv_ref[rows, :]
            for kk in range(SC_K):
                dw3_ref[kk:kk + 1, :] += _tap_grad(pad_x, dconv_ref, off, length, PADR + kk - SC_K // 2)
            for c0 in range(0, length, CONV_CH):
                ch = slice(off + c0, off + c0 + CONV_CH)
                c3 = _conv_chunk(pad_x, w3_ref, SC_K, c0, lambda kk: PADR + kk - SC_K // 2)
                dbg_ref[ch, :] = (dz_ref[ch, :] * c3).astype(BF16)
                dcx = _conv_chunk(pad_d, w3_ref, SC_K, c0, lambda kk: PADR + SC_K // 2 - kk)
                dcg_ref[ch, :] = (dcx * xs_ref[ch, :]).astype(BF16)
                dxs_ref[ch, :] = (dcx * cg_ref[ch, :]).astype(BF16)

    slab = jax.ShapeDtypeStruct((NROW, CW), BF16)
    return _pcall(body, name=name,
                  out_shape=(slab,) * 5 + (jax.ShapeDtypeStruct((CONF_K, CW), F32), jax.ShapeDtypeStruct((1, CW), F32),
                                           jax.ShapeDtypeStruct((SC_K, CW), F32)),
                  grid=(CB, BL),
                  in_specs=[_p_block(OFF_CONF), _p_block(OFF_CONF + CW), _p_block(OFF_SC), _p_block(OFF_SC + CW),
                            _p_block(OFF_SC + 2 * CW), _taps(CONF_K), _taps(SC_K), _conv_io(CW), _conv_io(CW)],
                  out_specs=(_conv_io(CW),) * 5 + (_taps(CONF_K), _taps(1), _taps(SC_K)),
                  scratch=[pltpu.VMEM((SEQ + 2 * PADR, LANE), F32), pltpu.VMEM((SEQ + 2 * PADR, LANE), F32),
                           pltpu.VMEM((RE, LANE), F32)],
                  sem=("parallel", "arbitrary"))(p_main, p_main, p_main, p_main, p_main, wdw, w3, dyc, dz)


def silu_rows(x, name):
    def body(x_ref, o_ref):
        o_ref[...] = x_ref[...] * _sigmoid(x_ref[...])

    return _pcall(body, name=name, out_shape=jax.ShapeDtypeStruct(x.shape, F32))(x)


def silu_rows_bwd(x, dcs, name):
    def body(x_ref, d_ref, o_ref):
        x = x_ref[...]
        sg = _sigmoid(x)
        tot = d_ref[0]
        for i in range(1, DEPTH):
            tot += d_ref[i]
        o_ref[...] = tot * (sg * (1.0 + x * (1.0 - sg)))

    return _pcall(body, name=name, out_shape=jax.ShapeDtypeStruct(x.shape, F32))(x, dcs)


def dmod_assemble(parts, name):
    def body(p_ref, dm_ref, db_ref):
        row = lax.broadcasted_iota(jnp.int32, (8, NMOD * D), 0)
        dm = jnp.zeros((8, NMOD * D), F32)
        db = jnp.zeros((1, NMOD * D), F32)
        for s in range(2 * BL):
            target = BL if s % 2 == 0 else s // 2
            part = p_ref[s:s + 1, :]
            dm += jnp.where(row == target, part, 0.0)
            db += part
        dm_ref[...] = dm
        db_ref[...] = db

    return _pcall(body, name=name, out_shape=(jax.ShapeDtypeStruct((8, NMOD * D), F32),
                                              jax.ShapeDtypeStruct((1, NMOD * D), F32)))(parts)


def sum_leading(x, name):
    n = x.shape[0]
    tr = _pick(x.shape[1], (256, 32, 8))

    def body(x_ref, o_ref):
        tot = x_ref[0].astype(F32)
        for i in range(1, n):
            tot += x_ref[i].astype(F32)
        o_ref[...] = tot

    return _pcall(body, name=name, out_shape=jax.ShapeDtypeStruct(x.shape[1:], F32), grid=(x.shape[1] // tr,),
                  in_specs=[pl.BlockSpec((n, tr, x.shape[2]), lambda i: (0, i, 0))],
                  out_specs=pl.BlockSpec((tr, x.shape[2]), lambda i: (i, 0)), sem=("parallel",))(x)


SLAB_ROWS = (256, 176, 128, 64, 8)


def _prefetch_call(body, name, out_shape, grid, in_specs, out_specs, sem, scalars, *args):
    spec = pltpu.PrefetchScalarGridSpec(num_scalar_prefetch=len(scalars), grid=grid, in_specs=in_specs,
                                        out_specs=out_specs)
    return pl.pallas_call(body, name=name, out_shape=out_shape, grid_spec=spec,
                          compiler_params=pltpu.CompilerParams(dimension_semantics=sem,
                                                               vmem_limit_bytes=VMEM_LIMIT))(*scalars, *args)


def cast_layers(w, chip, name):
    depth, r, c = w.shape
    tr = _pick(r, SLAB_ROWS)

    def body(s_ref, w_ref, *o_refs):
        for l in range(depth):
            o_refs[l][...] = w_ref[l].astype(BF16)

    slab = pl.BlockSpec((None, tr, c), lambda i, s: (s[0], i, 0))
    return _prefetch_call(body, name, (jax.ShapeDtypeStruct((NCHIP, r, c), BF16),) * depth, (r // tr,),
                          [pl.BlockSpec((depth, tr, c), lambda i, s: (0, i, 0))], (slab,) * depth,
                          ("parallel",), (chip,), w)


def rs_add(g, other, core, chip, name):
    _, r, c = g.shape
    rh = r // 2
    tr = _pick(rh, SLAB_ROWS)
    nblk = rh // tr

    def body(core_ref, chip_ref, g_ref, o_ref, send_ref, arr_ref):
        k = pl.program_id(1)
        tot = (g_ref[...].astype(F32) + o_ref[...].astype(F32)).astype(BF16)
        send_ref[...] = tot

        @pl.when(k == chip_ref[0])
        def _():
            arr_ref[...] = tot

    blk = (None, tr, c)
    return _prefetch_call(
        body, name, (jax.ShapeDtypeStruct(other.shape, BF16), jax.ShapeDtypeStruct(g.shape, BF16)), (nblk, NCHIP),
        [pl.BlockSpec(blk, lambda i, k, cr, ch: (k, cr[0] * nblk + i, 0)), pl.BlockSpec(blk, lambda i, k, cr, ch: (k, i, 0))],
        (pl.BlockSpec(blk, lambda i, k, cr, ch: (k, i, 0)),
         pl.BlockSpec(blk, lambda i, k, cr, ch: (ch[0], cr[0] * nblk + i, 0))),
        ("parallel", "arbitrary"), (core, chip), g, other)


def adamw_layer(w, arr, m, v, layer, prev, name):
    depth, r, c = w.shape
    tr = _pick(r, SLAB_ROWS)
    c1 = 1.0 / (1.0 - ADAM_B1 ** ADAM_STEP)
    c2 = 1.0 / (1.0 - ADAM_B2 ** ADAM_STEP)

    def body(w_ref, a_ref, m_ref, v_ref, p0, p1, p2, p3, g_ref, d_ref, mo_ref, vo_ref):
        gv = a_ref[0].astype(F32)
        for k in range(1, NCHIP):
            gv += a_ref[k].astype(F32)
        mn = ADAM_B1 * m_ref[...] + (1.0 - ADAM_B1) * gv
        vn = ADAM_B2 * v_ref[...] + (1.0 - ADAM_B2) * (gv * gv)
        g_ref[...] = gv
        d_ref[...] = -ADAM_LR * ((mn * c1) / (jnp.sqrt(vn * c2) + ADAM_EPS) + ADAM_WD * w_ref[...])
        mo_ref[...] = mn
        vo_ref[...] = vn

    spec = pl.BlockSpec((None, tr, c), lambda i: (layer, i, 0))
    sds = jax.ShapeDtypeStruct(w.shape, F32)
    return pl.pallas_call(body, name=name, out_shape=(sds,) * 4, grid=(r // tr,),
                          in_specs=[spec, pl.BlockSpec((NCHIP, tr, c), lambda i: (0, i, 0)), spec, spec] + [ANY] * 4,
                          out_specs=(spec,) * 4, input_output_aliases={4: 0, 5: 1, 6: 2, 7: 3},
                          compiler_params=pltpu.CompilerParams(dimension_semantics=("parallel",),
                                                               vmem_limit_bytes=VMEM_LIMIT))(w, arr, m, v, *prev)


def adamw(w, g, m, v, name):
    rows, cols = w.shape
    tr = _pick(rows, (256, 248, 128, 8))
    c1 = 1.0 / (1.0 - ADAM_B1 ** ADAM_STEP)
    c2 = 1.0 / (1.0 - ADAM_B2 ** ADAM_STEP)

    def body(w_ref, g_ref, m_ref, v_ref, d_ref, mo_ref, vo_ref):
        gv = g_ref[...]
        mn = ADAM_B1 * m_ref[...] + (1.0 - ADAM_B1) * gv
        vn = ADAM_B2 * v_ref[...] + (1.0 - ADAM_B2) * (gv * gv)
        d_ref[...] = -ADAM_LR * ((mn * c1) / (jnp.sqrt(vn * c2) + ADAM_EPS) + ADAM_WD * w_ref[...])
        mo_ref[...] = mn
        vo_ref[...] = vn

    spec = pl.BlockSpec((tr, cols), lambda i: (i, 0))
    sds = jax.ShapeDtypeStruct((rows, cols), F32)
    return _pcall(body, name=name, out_shape=(sds, sds, sds), grid=(rows // tr,), in_specs=[spec] * 4,
                  out_specs=(spec, spec, spec), sem=("parallel",))(w, g, m, v)


def _place():
    return lax.axis_index("x"), lax.axis_index("y"), lax.axis_index("c")


def _other_chips(x, y):
    return [(1 - x, y), (x, 1 - y), (1 - x, 1 - y)]


def _comm_call(body, name, out_shape, n_in, nsem):
    return pl.pallas_call(body, name=name, out_shape=out_shape, in_specs=[ANY] * n_in,
                          out_specs=jax.tree.map(lambda _: ANY, out_shape),
                          scratch_shapes=[pltpu.SemaphoreType.DMA((nsem,)), pltpu.SemaphoreType.DMA((nsem,)),
                                          pltpu.SemaphoreType.DMA])


def all_gather8(block, name):
    def body(x_ref, out_ref, send_sems, recv_sems, local_sem):
        x, y, c = _place()
        me, sibling = (x, y, c), (x, y, 1 - c)
        chips = _other_chips(x, y)

        def slot(px, py, pc):
            return out_ref.at[4 * px + 2 * py + pc]

        def copy(k, blk, to, src=None):
            return pltpu.make_async_remote_copy(src_ref=slot(*blk) if src is None else src, dst_ref=slot(*blk),
                                                send_sem=send_sems.at[k], recv_sem=recv_sems.at[k],
                                                device_id=to, device_id_type=MESH)

        mine = pltpu.make_async_copy(x_ref, slot(*me), local_sem)
        mine.start()
        first = [copy(0, me, sibling, src=x_ref)]
        first += [copy(1 + j, me, (*chip, c), src=x_ref) for j, chip in enumerate(chips)]
        for cp in first:
            cp.start()
        passed = [copy(4 + j, (*chip, c), sibling) for j, chip in enumerate(chips)]
        for j, chip in enumerate(chips):
            copy(1 + j, (*chip, c), me).wait_recv()
            passed[j].start()
        copy(0, sibling, me).wait_recv()
        for j, chip in enumerate(chips):
            copy(4 + j, (*chip, 1 - c), me).wait_recv()
        for cp in first + passed:
            cp.wait_send()
        mine.wait()

    return _comm_call(body, name, jax.ShapeDtypeStruct((8,) + block.shape, block.dtype), 1, 7)(block)


def _remote(src, dst, send_sems, recv_sems, k, to):
    return pltpu.make_async_remote_copy(src_ref=src, dst_ref=dst, send_sem=send_sems.at[k], recv_sem=recv_sems.at[k],
                                        device_id=to, device_id_type=MESH)


def _half(ref, slot, core):
    rh = ref.shape[1] // 2
    return ref.at[slot, pl.ds(core * rh, rh)]


def _all_slots_half(ref, core):
    rh = ref.shape[1] // 2
    return ref.at[:, pl.ds(core * rh, rh)]


def gather_ici(bufs):
    def program(ro, rw, new, ss, rs):
        x, y, c = _place()
        own = 2 * x + y
        starts, arrivals = [], []
        for w, ref in enumerate(rw):
            for j, chip in enumerate(_other_chips(x, y)):
                starts.append(_remote(_half(ref, own, c), _half(ref, own, c), ss, rs, 3 * w + j, (*chip, c)))
                arrivals.append(_remote(_half(ref, own, c), _half(ref, 2 * chip[0] + chip[1], c), ss, rs, 3 * w + j,
                                        (*chip, c)))
        return starts, arrivals

    return CommSpec((), tuple(bufs), (), 3 * len(bufs), program)


def gather_d2d(bufs):
    def program(ro, rw, new, ss, rs):
        x, y, c = _place()
        starts, arrivals = [], []
        for w, ref in enumerate(rw):
            for j, chip in enumerate(_other_chips(x, y)):
                slot = 2 * chip[0] + chip[1]
                starts.append(_remote(_half(ref, slot, c), _half(ref, slot, c), ss, rs, 3 * w + j, (x, y, 1 - c)))
                arrivals.append(_remote(_half(ref, slot, c), _half(ref, slot, 1 - c), ss, rs, 3 * w + j, (x, y, 1 - c)))
        return starts, arrivals

    return CommSpec((), tuple(bufs), (), 3 * len(bufs), program)


def rs_swap(grads):
    def program(ro, rw, new, ss, rs):
        x, y, c = _place()
        copies = [_remote(_all_slots_half(g, 1 - c), new[w], ss, rs, w, (x, y, 1 - c)) for w, g in enumerate(ro)]
        return copies, copies

    shapes = tuple(jax.ShapeDtypeStruct((NCHIP, g.shape[1] // 2, g.shape[2]), g.dtype) for g in grads)
    return CommSpec(tuple(grads), (), shapes, len(grads), program)


def rs_ici(sends, arrs):
    def program(ro, rw, new, ss, rs):
        x, y, c = _place()
        own = 2 * x + y
        starts, arrivals = [], []
        for w, (snd, arr) in enumerate(zip(ro, rw)):
            for j, chip in enumerate(_other_chips(x, y)):
                slot = 2 * chip[0] + chip[1]
                starts.append(_remote(snd.at[slot], _half(arr, own, c), ss, rs, 3 * w + j, (*chip, c)))
                arrivals.append(_remote(snd.at[slot], _half(arr, slot, c), ss, rs, 3 * w + j, (*chip, c)))
        return starts, arrivals

    return CommSpec(tuple(sends), tuple(arrs), (), 3 * len(sends), program)


def rs_d2d(arrs):
    def program(ro, rw, new, ss, rs):
        x, y, c = _place()
        starts = [_remote(_all_slots_half(a, c), _all_slots_half(a, c), ss, rs, w, (x, y, 1 - c)) for w, a in enumerate(rw)]
        arrivals = [_remote(_all_slots_half(a, c), _all_slots_half(a, 1 - c), ss, rs, w, (x, y, 1 - c))
                    for w, a in enumerate(rw)]
        return starts, arrivals

    return CommSpec((), tuple(arrs), (), len(arrs), program)


PACK_COLS = 1024
MATMUL_W = ("w_ada", "w_in", "w_attn_o", "w_conf_out", "w_sc_out", "w_mix_out", "w_ffn_in", "w_ffn_out")
ROW_SPLIT = ("w_mix_out", "w_ffn_out")
CONV_W = ("conf_dw_w", "sc_dw_w")
SMALL = ("c_ctx", "b_ada", "q_norm", "k_norm", "conf_dw_b", "conf_ln_g", "conf_ln_b", "conf_dw_w", "sc_dw_w")


def _pack_rows(arrays, row_multiple):
    flat = jnp.concatenate([a.reshape(-1) for a in arrays])
    rows = -(-flat.shape[0] // PACK_COLS)
    rows = -(-rows // row_multiple) * row_multiple
    flat = jnp.pad(flat, (0, rows * PACK_COLS - flat.shape[0]))
    return flat.reshape(rows, PACK_COLS)


def _unpack(flat2d, shapes):
    flat = flat2d.reshape(-1)
    out, pos = [], 0
    for shp in shapes:
        n = 1
        for s in shp:
            n *= s
        out.append(flat[pos:pos + n].reshape(shp))
        pos += n
    return out


def _cols_joined(stacked_layer):
    nchip, r, c = stacked_layer.shape
    return jnp.transpose(stacked_layer, (1, 0, 2)).reshape(r, nchip * c)


def _cols_split(full):
    r, cols = full.shape
    return jnp.transpose(full.reshape(r, NCHIP, cols // NCHIP), (1, 0, 2))


def _rope_tables():
    rows = SEQ // GRID_W
    r_ids = jnp.repeat(jnp.arange(rows, dtype=F32), GRID_W)
    c_ids = jnp.tile(jnp.arange(GRID_W, dtype=F32), rows)
    freqs = ROPE_THETA ** (-jnp.arange(0, HD // 2, 2, dtype=F32) / (HD // 2))
    ang_r, ang_c = r_ids[:, None] * freqs, c_ids[:, None] * freqs
    cos_h = jnp.concatenate([jnp.cos(ang_r), jnp.cos(ang_r), jnp.cos(ang_c), jnp.cos(ang_c)], axis=1)
    sin_h = jnp.concatenate([-jnp.sin(ang_r), jnp.sin(ang_r), -jnp.sin(ang_c), jnp.sin(ang_c)], axis=1)
    cos_t = jnp.concatenate([jnp.ones((CTX, HD), F32), cos_h], axis=0)
    sin_t = jnp.concatenate([jnp.zeros((CTX, HD), F32), sin_h], axis=0)
    return jnp.tile(cos_t, (1, LANE // HD)), jnp.tile(sin_t, (1, LANE // HD))


def _group_matrix():
    gid = jnp.arange(LANE) // HD
    return jnp.where(gid[:, None] == gid[None, :], 1.0 / HD, 0.0).astype(F32)


def _layer_weights(bufs, small, i):
    b = dict(zip(MATMUL_W, bufs))
    wi = _cols_joined(b["w_in"])

    def rows_joined(a):
        return a.reshape(a.shape[0] * a.shape[1], a.shape[2])

    return dict(
        w_ada=(b["w_ada"], "cols"), wi_main=(wi[:, :OFF_GATE], "mat"), wi_gate=(wi[:, OFF_GATE:], "mat"),
        w_attn_o=(_cols_joined(b["w_attn_o"]), "mat"), w_conf_out=(_cols_joined(b["w_conf_out"]), "mat"),
        w_sc_out=(_cols_joined(b["w_sc_out"]), "mat"), w_ffn_in=(b["w_ffn_in"], "cols"),
        w_mix_out=(rows_joined(b["w_mix_out"]), "mat"), w_ffn_out=(rows_joined(b["w_ffn_out"]), "mat"),
        conf_dw_w=small["conf_dw_w"][i], sc_dw_w=small["sc_dw_w"][i], conf_dw_b=small["conf_dw_b"][i][None],
        conf_ln_g=small["conf_ln_g"][i][None], conf_ln_b=small["conf_ln_b"][i][None],
        gq=jnp.tile(small["q_norm"][i], LANE // HD)[None], gk=jnp.tile(small["k_norm"][i], LANE // HD)[None])


def _layer_fwd(i, xs, h, mods, w, tabs, next_bufs, next_layer):
    cos_t, sin_t, g_mat = tabs
    n = f"l{i}_"
    sv = {"x_in": xs, "mods": mods, "h": h}
    sv["p_main"] = mm_nn(sv["h"], w["wi_main"], name=n + "p_main")
    sv["q"], sv["k"], sv["v"] = qkv_fwd(sv["p_main"], cos_t, sin_t, g_mat, w["gq"], w["gk"], n + "qkv")
    if next_bufs is None:
        sv["o"] = attn_fwd(sv["q"], sv["k"], sv["v"], n + "attn")
    else:
        sv["o"], next_bufs, _ = attn_fwd(sv["q"], sv["k"], sv["v"], n + "attn", comm=gather_ici(next_bufs))
    sv["yc"], sv["z"] = conv_fwd(sv["p_main"], w["conf_dw_w"], w["conf_dw_b"], w["sc_dw_w"], n + "conv")
    sv["hs"] = ln_silu_fwd(sv["yc"], w["conf_ln_g"], w["conf_ln_b"], n + "ln_silu")
    sv["merged"], sv["gates"], sv["ys"] = gate_mm_fwd(sv["h"], w["wi_gate"][0], sv["o"], sv["hs"], sv["z"],
                                                      w["w_attn_o"][0], w["w_conf_out"][0], w["w_sc_out"][0],
                                                      n + "gate_merge")
    sv["mixed"] = mm_nn(sv["merged"], w["w_mix_out"], name=n + "mix")
    sv["x1"], sv["h2"] = resid_norm_fwd(xs, sv["mixed"], mods, 2, mods, 3, 4, n + "resid1_norm2")
    if next_bufs is None:
        sv["f"], sv["u2"] = ffn_in_swiglu(sv["h2"], w["w_ffn_in"][0], n + "ffn_in")
    else:
        (sv["f"], sv["u2"]), next_bufs, _ = ffn_in_swiglu(sv["h2"], w["w_ffn_in"][0], n + "ffn_in",
                                                          comm=gather_d2d(next_bufs))
    sv["of"] = mm_nn(sv["f"], w["w_ffn_out"], name=n + "ffn_out")
    if next_layer is None:
        return gate_resid_fwd(sv["x1"], sv["of"], mods, 5, n + "resid2"), None, sv
    w_next, mods_next = next_layer(next_bufs)
    x2, h_next = resid_norm_fwd(sv["x1"], sv["of"], mods, 5, mods_next, 0, 1, n + "resid2_norm1")
    return x2, (h_next, w_next, mods_next), sv


def _layer_bwd(i, dx2, dof, dm5, sv, w, tabs, cs, pending, ids, below):
    cos_t, sin_t, g_mat = tabs
    n = f"l{i}b_"
    mods = sv["mods"]
    g = {}
    du = d_f_swiglu(dof, w["w_ffn_out"][0], sv["u2"], n + "d_f")
    g["w_ffn_out"] = mm_tn(sv["f"], dof, out_dtype=BF16, name=n + "dw_ffn_out").reshape(NCHIP, FH // NCHIP, D)
    if pending is None:
        dh2 = mm_nt(du, w["w_ffn_in"], name=n + "d_h2")
    else:
        dh2, _, swapped = mm_nt(du, w["w_ffn_in"], name=n + "d_h2", comm=rs_swap(pending))
        sends, arrs = zip(*[rs_add(g_, s_, ids[0], ids[1], f"{n}rs_add_{k}")
                            for k, g_, s_ in zip(MATMUL_W, pending, swapped)])
    g["w_ffn_in"] = mm_tn(sv["h2"], du, cols=True, out_dtype=BF16, name=n + "dw_ffn_in")
    dx1, dm34, dmixed, dm2 = norm_resid_bwd(sv["x1"], mods, dh2, dx2, 4, sv["mixed"], mods, 2, n + "norm2_resid1")
    dya, dyb, dys, dp_gate = d_merged_gate(dmixed, w["w_mix_out"][0], sv["gates"], sv["ys"], n + "d_merged")
    g["w_mix_out"] = mm_tn(sv["merged"], dmixed, out_dtype=BF16, name=n + "dw_mix").reshape(NCHIP, D // NCHIP, D)
    do = mm_nt(dya, w["w_attn_o"], out_dtype=BF16, name=n + "d_o")
    g["w_attn_o"] = _cols_split(mm_tn(sv["o"], dya, out_dtype=BF16, name=n + "dw_attn_o"))
    dhs = mm_nt(dyb, w["w_conf_out"], name=n + "d_hs")
    g["w_conf_out"] = _cols_split(mm_tn(sv["hs"], dyb, out_dtype=BF16, name=n + "dw_conf_out"))
    dz = mm_nt(dys, w["w_sc_out"], name=n + "d_z")
    g["w_sc_out"] = _cols_split(mm_tn(sv["z"], dys, out_dtype=BF16, name=n + "dw_sc_out"))
    dyc, g["conf_ln_g"], g["conf_ln_b"] = ln_silu_bwd(sv["yc"], w["conf_ln_g"], w["conf_ln_b"], dhs, n + "ln_silu")
    da, dg, dbg, dcg, dxs, g["conf_dw_w"], g["conf_dw_b"], g["sc_dw_w"] = conv_bwd(
        sv["p_main"], w["conf_dw_w"], w["sc_dw_w"], dyc, dz, n + "conv")
    done = None
    if pending is None:
        dq, dk, dv = attn_bwd(sv["q"], sv["k"], sv["v"], sv["o"], do, n + "attn")
        dp_qkv, dgqk = qkv_bwd(sv["p_main"], cos_t, sin_t, g_mat, w["gq"], w["gk"], dq, dk, dv, n + "qkv")
    else:
        (dq, dk, dv), arrs, _ = attn_bwd(sv["q"], sv["k"], sv["v"], sv["o"], do, n + "attn", comm=rs_ici(sends, arrs))
        (dp_qkv, dgqk), done, _ = qkv_bwd(sv["p_main"], cos_t, sin_t, g_mat, w["gq"], w["gk"], dq, dk, dv, n + "qkv",
                                          comm=rs_d2d(arrs))
    dp_main = jnp.concatenate([dp_qkv, da, dg, dbg, dcg, dxs], axis=1)
    dh = mm_nt(dp_main, w["wi_main"], name=n + "d_h_main")
    dh = mm_nt(dp_gate, w["wi_gate"], acc=dh, name=n + "d_h_gate")
    g["w_in"] = _cols_split(jnp.concatenate([mm_tn(sv["h"], dp_main, out_dtype=BF16, name=n + "dw_in_main"),
                                             mm_tn(sv["h"], dp_gate, out_dtype=BF16, name=n + "dw_in_gate")], axis=1))
    if below is None:
        dx_in, dm01 = norm_mod_bwd(sv["x_in"], mods, dh, dx1, 1, n + "norm1")
        dof_below = dm5_below = None
    else:
        dx_in, dm01, dof_below, dm5_below = norm_resid_bwd(sv["x_in"], mods, dh, dx1, 1, below["of"], below["mods"], 5,
                                                           n + "norm1_resid2")
    parts = jnp.concatenate([dm01, dm2, dm34, dm5], axis=2).reshape(2 * BL, NMOD * D)
    dmod, g["b_ada"] = dmod_assemble(parts, n + "dmod")
    g["w_ada"] = mm_tn(cs, dmod, cols=True, out_dtype=BF16, name=n + "dw_ada")
    g["dcs"] = mm_nt(dmod, w["w_ada"], name=n + "d_cs")
    g["q_norm"] = dgqk[0, :AW].reshape(NQ, HD).sum(axis=0)
    g["k_norm"] = dgqk[0, OFF_K:OFF_K + KVW].reshape(NKV, HD).sum(axis=0)
    return dx_in, dof_below, dm5_below, g, done


def kernel(x, c, ctx, c_ctx, w_ada, b_ada, w_in, q_norm, k_norm, w_attn_o, conf_dw_w, conf_dw_b, conf_ln_g, conf_ln_b, w_conf_out, sc_dw_w, w_sc_out, w_mix_out, w_ffn_in, w_ffn_out, loss_target, m_c_ctx, m_w_ada, m_b_ada, m_w_in, m_q_norm, m_k_norm, m_w_attn_o, m_conf_dw_w, m_conf_dw_b, m_conf_ln_g, m_conf_ln_b, m_w_conf_out, m_sc_dw_w, m_w_sc_out, m_w_mix_out, m_w_ffn_in, m_w_ffn_out, v_c_ctx, v_w_ada, v_b_ada, v_w_in, v_q_norm, v_k_norm, v_w_attn_o, v_conf_dw_w, v_conf_dw_b, v_conf_ln_g, v_conf_ln_b, v_w_conf_out, v_sc_dw_w, v_w_sc_out, v_w_mix_out, v_w_ffn_in, v_w_ffn_out):
    local = dict(c_ctx=c_ctx, w_ada=w_ada, b_ada=b_ada, w_in=w_in, q_norm=q_norm, k_norm=k_norm, w_attn_o=w_attn_o,
                 conf_dw_w=conf_dw_w, conf_dw_b=conf_dw_b, conf_ln_g=conf_ln_g, conf_ln_b=conf_ln_b,
                 w_conf_out=w_conf_out, sc_dw_w=sc_dw_w, w_sc_out=w_sc_out, w_mix_out=w_mix_out, w_ffn_in=w_ffn_in,
                 w_ffn_out=w_ffn_out)
    mom_m = dict(c_ctx=m_c_ctx, w_ada=m_w_ada, b_ada=m_b_ada, w_in=m_w_in, q_norm=m_q_norm, k_norm=m_k_norm,
                 w_attn_o=m_w_attn_o, conf_dw_w=m_conf_dw_w, conf_dw_b=m_conf_dw_b, conf_ln_g=m_conf_ln_g,
                 conf_ln_b=m_conf_ln_b, w_conf_out=m_w_conf_out, sc_dw_w=m_sc_dw_w, w_sc_out=m_w_sc_out,
                 w_mix_out=m_w_mix_out, w_ffn_in=m_w_ffn_in, w_ffn_out=m_w_ffn_out)
    mom_v = dict(c_ctx=v_c_ctx, w_ada=v_w_ada, b_ada=v_b_ada, w_in=v_w_in, q_norm=v_q_norm, k_norm=v_k_norm,
                 w_attn_o=v_w_attn_o, conf_dw_w=v_conf_dw_w, conf_dw_b=v_conf_dw_b, conf_ln_g=v_conf_ln_g,
                 conf_ln_b=v_conf_ln_b, w_conf_out=v_w_conf_out, sc_dw_w=v_sc_dw_w, w_sc_out=v_w_sc_out,
                 w_mix_out=v_w_mix_out, w_ffn_in=v_w_ffn_in, w_ffn_out=v_w_ffn_out)
    order = ("c_ctx", "w_ada", "b_ada", "w_in", "q_norm", "k_norm", "w_attn_o", "conf_dw_w", "conf_dw_b", "conf_ln_g",
             "conf_ln_b", "w_conf_out", "sc_dw_w", "w_sc_out", "w_mix_out", "w_ffn_in", "w_ffn_out")
    core = lax.axis_index("c").astype(jnp.int32)
    chip = (2 * lax.axis_index("x") + lax.axis_index("y")).astype(jnp.int32)

    own = [cast_layers(local[k], chip.reshape(1), "cast_" + k) for k in MATMUL_W]
    layer_bufs = [[own[w][l] for w in range(len(MATMUL_W))] for l in range(DEPTH)]
    conv_shapes = [local[k].shape for k in CONV_W]
    conv_all = all_gather8(_pack_rows([local[k] for k in CONV_W], 8), "gather_conv_taps")
    per_chip = [_unpack(conv_all[2 * s], conv_shapes) for s in range(NCHIP)]
    small = dict(b_ada=b_ada, q_norm=q_norm, k_norm=k_norm, conf_dw_b=conf_dw_b, conf_ln_g=conf_ln_g, conf_ln_b=conf_ln_b)
    for i, k in enumerate(CONV_W):
        small[k] = jnp.concatenate([per_chip[s][i] for s in range(NCHIP)], axis=2)

    loss_local, grad_x, sums, small_g = local_step(x, c, ctx, c_ctx, layer_bufs, small, loss_target,
                                                   ids=(core.reshape(1), chip.reshape(1)))
    loss = lax.psum(loss_local, ("x", "y", "c"))

    small_shapes = [small_g[k].shape for k in SMALL]
    small_sum = sum_leading(all_gather8(_pack_rows([small_g[k] for k in SMALL], 8), "gather_small_grads"), "small_sum")
    small_g = dict(zip(SMALL, _unpack(small_sum, small_shapes)))
    for k in CONV_W:
        width = local[k].shape[2]
        small_g[k] = lax.dynamic_slice_in_dim(small_g[k], chip * width, width, axis=2)

    grad, delta, new_m, new_v = {}, {}, {}, {}
    for k in order:
        if k in MATMUL_W:
            outs = [lax.empty(local[k].shape, F32) for _ in range(4)]
            for l in range(DEPTH):
                outs = adamw_layer(local[k], sums[l][MATMUL_W.index(k)], mom_m[k], mom_v[k], l, outs, f"adamw_{k}_{l}")
            grad[k], delta[k], new_m[k], new_v[k] = outs
            continue
        shp = local[k].shape
        view = (1, shp[0]) if len(shp) == 1 else (-1, shp[-1])
        d_, m_, v_ = adamw(local[k].reshape(view), small_g[k].reshape(view), mom_m[k].reshape(view),
                           mom_v[k].reshape(view), "adamw_" + k)
        grad[k], delta[k], new_m[k], new_v[k] = small_g[k], d_.reshape(shp), m_.reshape(shp), v_.reshape(shp)
    return (loss, grad_x, *[grad[k] for k in order], *[delta[k] for k in order], *[new_m[k] for k in order],
            *[new_v[k] for k in order])


def local_step(x, c, ctx, c_ctx, layer_bufs, small, loss_target, ids=None):
    tabs = _rope_tables() + (_group_matrix(),)
    distributed = ids is not None
    layer_bufs = list(layer_bufs)
    if distributed:
        layer_bufs[0], _ = comm_only("gather0_ici", gather_ici(layer_bufs[0]))
        layer_bufs[0], _ = comm_only("gather0_d2d", gather_d2d(layer_bufs[0]))

    cin = jnp.concatenate([c, c_ctx[None], jnp.zeros((8 - BL - 1, D), F32)], axis=0)
    cs = silu_rows(cin, "silu_c")
    xs = jnp.concatenate([ctx, x], axis=1).reshape(NROW, D)
    saved, layer_w = [], []

    def make_layer(i, bufs):
        w = _layer_weights(bufs, small, i)
        return w, mm_nn(cs, w["w_ada"], bias=small["b_ada"][i][None], name=f"l{i}_mod").reshape(8, 1, NMOD * D)

    w, mods = make_layer(0, layer_bufs[0])
    h = norm_mod_fwd(xs, mods, 0, 1, "l0_norm1")
    for i in range(DEPTH):
        last = i == DEPTH - 1

        def next_layer(bufs, i=i):
            if bufs is not None:
                layer_bufs[i + 1] = bufs
            return make_layer(i + 1, layer_bufs[i + 1])

        ahead = layer_bufs[i + 1] if distributed and not last else None
        xs, following, sv = _layer_fwd(i, xs, h, mods, w, tabs, ahead, None if last else next_layer)
        saved.append(sv)
        layer_w.append(w)
        if following is not None:
            h, w, mods = following
    dxs, loss_lanes = loss_fwd_bwd(xs, loss_target.reshape(BL * SEQ, D), "loss")
    loss_local = 0.5 * jnp.sum(loss_lanes) / D

    grads, sums = [None] * DEPTH, [None] * DEPTH
    pending = None
    dof, dm5 = gate_resid_bwd(dxs, saved[-1]["of"], saved[-1]["mods"], 5, "top_resid2")
    for i in reversed(range(DEPTH)):
        dxs, dof, dm5, grads[i], done = _layer_bwd(i, dxs, dof, dm5, saved[i], layer_w[i], tabs, cs, pending, ids,
                                                   saved[i - 1] if i > 0 else None)
        partial = [grads[i][k] for k in MATMUL_W]
        if distributed:
            if pending is not None:
                sums[i + 1] = done
            pending = partial
        else:
            sums[i] = partial
    if distributed:
        _, swapped = comm_only("rs0_swap", rs_swap(pending))
        sends, arrs = zip(*[rs_add(g_, s_, ids[0], ids[1], "rs0_add_" + k) for k, g_, s_ in zip(MATMUL_W, pending, swapped)])
        arrs, _ = comm_only("rs0_ici", rs_ici(sends, arrs))
        sums[0], _ = comm_only("rs0_d2d", rs_d2d(arrs))
    grad_x = dxs.reshape(BL, RE, D)[:, CTX:, :]
    dcin = silu_rows_bwd(cin, jnp.stack([grads[i]["dcs"] for i in range(DEPTH)]), "silu_c_bwd")

    def stack(key):
        return jnp.stack([grads[i][key] for i in range(DEPTH)])

    small_g = dict(c_ctx=dcin[BL], b_ada=stack("b_ada").reshape(DEPTH, NMOD * D), q_norm=stack("q_norm"),
                   k_norm=stack("k_norm"), conf_dw_b=stack("conf_dw_b").reshape(DEPTH, CW),
                   conf_ln_g=stack("conf_ln_g").reshape(DEPTH, CW), conf_ln_b=stack("conf_ln_b").reshape(DEPTH, CW),
                   conf_dw_w=stack("conf_dw_w"), sc_dw_w=stack("sc_dw_w"))
    return loss_local, grad_x, sums, small_g
```

```python
import functools
from typing import Any, Callable, NamedTuple, Sequence

import jax
import jax.numpy as jnp
from jax import lax
from jax.experimental import pallas as pl
from jax.experimental.pallas import tpu as pltpu

F32, BF16 = jnp.float32, jnp.bfloat16
HIGHEST = lax.Precision.HIGHEST

D = 1024
SEQ = 2048
CTX = 256
DEPTH = 4
BL = 4
GRID_W = 64
HD = 64
NQ = 8
NKV = 2
AW = NQ * HD
KVW = NKV * HD
CW = D // 2
CONF_K = 31
SC_K = 3
NMOD = 6
FH = -(-8 * D // (3 * 256)) * 256
EPS = 1e-6
ROPE_THETA = 10000.0
ATTN_SCALE = HD ** -0.5
OFF_K = AW
OFF_V = OFF_K + KVW
OFF_CONF = OFF_V + KVW
OFF_SC = OFF_CONF + 2 * CW
OFF_GATE = OFF_SC + 3 * CW
IN_W = OFF_GATE + 3 * D
QKVW = OFF_CONF
NCHIP = 4

ADAM_LR, ADAM_B1, ADAM_B2, ADAM_EPS, ADAM_WD, ADAM_STEP = 0.001, 0.9, 0.999, 1e-08, 0.01, 10

TM = CTX
RE = CTX + SEQ
TPE = RE // TM
NROW = BL * RE
NT = NROW // TM
LANE = 128
CB = CW // LANE
CONV_CH = 128
PADR = 16
VMEM_LIMIT = 52 * 1024 * 1024

MESH = pl.DeviceIdType.MESH
ANY = pl.BlockSpec(memory_space=pl.ANY)


class CommSpec(NamedTuple):
    ro: Sequence[Any]
    rw: Sequence[Any]
    new: Sequence[Any]
    nsem: int
    program: Callable


def _pcall(body, *, name, out_shape, grid=(), in_specs=None, out_specs=None, scratch=(), sem=None, comm=None,
           aliases=None):
    aliases = dict(aliases or {})
    if not grid:
        return pl.pallas_call(body, name=name, out_shape=out_shape)
    if comm is None:
        params = pltpu.CompilerParams(dimension_semantics=sem, vmem_limit_bytes=VMEM_LIMIT)
        return pl.pallas_call(body, name=name, out_shape=out_shape, grid=grid, in_specs=in_specs, out_specs=out_specs,
                              scratch_shapes=list(scratch), input_output_aliases=aliases, compiler_params=params)

    single = not isinstance(out_shape, (tuple, list))
    out_shapes = (out_shape,) if single else tuple(out_shape)
    out_specs_t = (out_specs,) if single else tuple(out_specs)
    n_in, n_out, n_scr = len(in_specs), len(out_shapes), len(scratch)
    n_ro, n_rw, n_new = len(comm.ro), len(comm.rw), len(comm.new)

    def carrier(*refs):
        ins = refs[:n_in]
        ro_refs = refs[n_in:n_in + n_ro]
        o0 = n_in + n_ro + n_rw
        outs = refs[o0:o0 + n_out]
        rw_refs = refs[o0 + n_out:o0 + n_out + n_rw]
        new_refs = refs[o0 + n_out + n_rw:o0 + n_out + n_rw + n_new]
        s0 = o0 + n_out + n_rw + n_new
        scr = refs[s0:s0 + n_scr]
        send_sems, recv_sems = refs[s0 + n_scr:]
        first = functools.reduce(jnp.logical_and, [pl.program_id(a) == 0 for a in range(len(grid))])
        last = functools.reduce(jnp.logical_and, [pl.program_id(a) == grid[a] - 1 for a in range(len(grid))])
        starts, arrivals = comm.program(ro_refs, rw_refs, new_refs, send_sems, recv_sems)

        @pl.when(first)
        def _():
            for cp in starts:
                cp.start()

        body(*ins, *outs, *scr)

        @pl.when(last)
        def _():
            for cp in arrivals:
                cp.wait_recv()
            for cp in starts:
                cp.wait_send()

    def call(*args):
        rw_shapes = tuple(jax.ShapeDtypeStruct(a.shape, a.dtype) for a in comm.rw)
        res = pl.pallas_call(
            carrier, name=name, out_shape=out_shapes + rw_shapes + tuple(comm.new), grid=grid,
            in_specs=list(in_specs) + [ANY] * (n_ro + n_rw),
            out_specs=out_specs_t + (ANY,) * (n_rw + n_new),
            scratch_shapes=list(scratch) + [pltpu.SemaphoreType.DMA((comm.nsem,)), pltpu.SemaphoreType.DMA((comm.nsem,))],
            input_output_aliases={**aliases, **{n_in + n_ro + i: n_out + i for i in range(n_rw)}},
            compiler_params=pltpu.CompilerParams(dimension_semantics=("arbitrary",) * len(grid),
                                                 vmem_limit_bytes=VMEM_LIMIT))(*args, *comm.ro, *comm.rw)
        compute = res[0] if single else tuple(res[:n_out])
        return compute, list(res[n_out:n_out + n_rw]), list(res[n_out + n_rw:])

    return call


def comm_only(name, comm):
    n_ro, n_rw, n_new = len(comm.ro), len(comm.rw), len(comm.new)

    def body(*refs):
        ro_refs = refs[:n_ro]
        rw_refs = refs[n_ro + n_rw:n_ro + 2 * n_rw]
        new_refs = refs[n_ro + 2 * n_rw:n_ro + 2 * n_rw + n_new]
        send_sems, recv_sems = refs[n_ro + 2 * n_rw + n_new:]
        starts, arrivals = comm.program(ro_refs, rw_refs, new_refs, send_sems, recv_sems)
        for cp in starts:
            cp.start()
        for cp in arrivals:
            cp.wait_recv()
        for cp in starts:
            cp.wait_send()

    rw_shapes = tuple(jax.ShapeDtypeStruct(a.shape, a.dtype) for a in comm.rw)
    res = pl.pallas_call(body, name=name, out_shape=rw_shapes + tuple(comm.new), in_specs=[ANY] * (n_ro + n_rw),
                         out_specs=(ANY,) * (n_rw + n_new), input_output_aliases={n_ro + i: i for i in range(n_rw)},
                         scratch_shapes=[pltpu.SemaphoreType.DMA((comm.nsem,)), pltpu.SemaphoreType.DMA((comm.nsem,))])(
                             *comm.ro, *comm.rw)
    return list(res[:n_rw]), list(res[n_rw:])


def _pick(n, cands):
    for t in cands:
        if n % t == 0:
            return t
    return n


def _seg(t):
    return jnp.where(t % TPE == 0, BL, t // TPE)


def _slot(t):
    return 2 * (t // TPE) + jnp.where(t % TPE == 0, 0, 1)


def _sigmoid(x):
    return 1.0 / (1.0 + jnp.exp(-x))


MM_BUDGET = 40 * 1024 * 1024
N_TILE_CAP = 1664


def _tile(n, cap=N_TILE_CAP):
    if n <= cap:
        return n
    for t in range(cap - cap % LANE, 0, -LANE):
        if n % t == 0:
            return t
    return n


def _row_tile(m, bytes_of):
    for tm in (1024, 512, 256, 128):
        if m % tm == 0 and bytes_of(tm) <= MM_BUDGET:
            return tm
    return m


def _w_dims(w):
    arr, kind = w
    if kind == "cols":
        return arr.shape[1], NCHIP * arr.shape[2]
    return arr.shape


def _sz(dtype):
    return jnp.dtype(dtype).itemsize


def mm_nn(a, w, *, bias=None, out_dtype=F32, name):
    arr, kind = w
    m, k = a.shape
    _, n = _w_dims(w)
    tn = _tile(arr.shape[2]) if kind == "cols" else _tile(n)
    tm = _row_tile(m, lambda t: 2 * (t * k * _sz(a.dtype) + k * tn * 2 + t * tn * _sz(out_dtype)))
    if kind == "mat":
        b_spec = pl.BlockSpec((k, tn), lambda j, i: (0, j))
    else:
        per = arr.shape[2] // tn
        b_spec = pl.BlockSpec((None, k, tn), lambda j, i: (j // per, 0, j % per))
    has_bias = bias is not None

    def body(*refs):
        out = jnp.dot(refs[0][...].astype(BF16), refs[1][...].astype(BF16), preferred_element_type=F32)
        if has_bias:
            out = out + refs[2][...]
        refs[-1][...] = out.astype(out_dtype)

    in_specs = [pl.BlockSpec((tm, k), lambda j, i: (i, 0)), b_spec]
    args = [a, arr]
    if has_bias:
        in_specs.append(pl.BlockSpec((1, tn), lambda j, i: (0, j)))
        args.append(bias)
    return _pcall(body, name=name, out_shape=jax.ShapeDtypeStruct((m, n), out_dtype), grid=(n // tn, m // tm),
                  in_specs=in_specs, out_specs=pl.BlockSpec((tm, tn), lambda j, i: (i, j)),
                  sem=("parallel", "parallel"))(*args)


def mm_nt(a, w, *, acc=None, out_dtype=F32, name, comm=None):
    arr, kind = w
    kdim, _ = _w_dims(w)
    has_acc = acc is not None
    tk = _tile(kdim, 1408)
    if kind == "cols":
        c = arr.shape[2]
        m = a.shape[-2]
        if a.ndim == 3:
            a_spec = lambda t: pl.BlockSpec((None, t, c), lambda j, i, s: (s // 2, i, s % 2))
        else:
            a_spec = lambda t: pl.BlockSpec((t, c), lambda j, i, s: (i, s))
        tm = _row_tile(m, lambda t: 2 * (t * c * _sz(a.dtype) + tk * c * 2 + t * tk * _sz(out_dtype)) + t * tk * 4)

        def body(a_ref, b_ref, o_ref, acc_ref):
            s = pl.program_id(2)

            @pl.when(s == 0)
            def _():
                acc_ref[...] = jnp.zeros_like(acc_ref)

            acc_ref[...] += lax.dot_general(a_ref[...].astype(BF16), b_ref[...], (((1,), (1,)), ((), ())),
                                            preferred_element_type=F32)

            @pl.when(s == NCHIP - 1)
            def _():
                o_ref[...] = acc_ref[...].astype(out_dtype)

        return _pcall(body, name=name, out_shape=jax.ShapeDtypeStruct((m, kdim), out_dtype),
                      grid=(kdim // tk, m // tm, NCHIP),
                      in_specs=[a_spec(tm), pl.BlockSpec((None, tk, c), lambda j, i, s: (s, j, 0))],
                      out_specs=pl.BlockSpec((tm, tk), lambda j, i, s: (i, j)),
                      scratch=[pltpu.VMEM((tm, tk), F32)],
                      sem=("parallel", "parallel", "arbitrary"), comm=comm)(a, arr)

    m, n = a.shape
    tm = _row_tile(m, lambda t: 2 * (t * n * _sz(a.dtype) + tk * n * 2 + t * tk * (_sz(out_dtype) + 4 * has_acc))
                   + tk * n * 2)
    b_spec = pl.BlockSpec((tk, n), lambda j, i: (j, 0))

    def body(*refs):
        wt_ref = refs[-1]

        @pl.when(pl.program_id(1) == 0)
        def _():
            wt_ref[...] = refs[1][...].astype(BF16).T

        out = jnp.dot(refs[0][...].astype(BF16), wt_ref[...], preferred_element_type=F32)
        if has_acc:
            out = out + refs[2][...]
        refs[-2][...] = out.astype(out_dtype)

    in_specs = [pl.BlockSpec((tm, n), lambda j, i: (i, 0)), b_spec]
    args = [a, arr]
    if has_acc:
        in_specs.append(pl.BlockSpec((tm, tk), lambda j, i: (i, j)))
        args.append(acc)
    return _pcall(body, name=name, out_shape=jax.ShapeDtypeStruct((m, kdim), out_dtype), grid=(kdim // tk, m // tm),
                  in_specs=in_specs, out_specs=pl.BlockSpec((tm, tk), lambda j, i: (i, j)),
                  scratch=[pltpu.VMEM((n, tk), BF16)], sem=("parallel", "arbitrary"))(*args)


def mm_tn(a, b, *, cols=False, out_dtype=F32, name):
    rows, k = a.shape
    halves = b.ndim == 3
    n = 2 * b.shape[2] if halves else b.shape[1]
    odt = out_dtype
    if cols:
        c = n // NCHIP
        tn, tk = _tile(c), k
        per = c // tn
        out_spec = pl.BlockSpec((None, tk, tn), lambda i, j, r: (j // per, 0, j % per))
        out_shape = jax.ShapeDtypeStruct((NCHIP, k, c), odt)
    else:
        tn, tk = _tile(n), _tile(k, 1408)
        out_spec = pl.BlockSpec((tk, tn), lambda i, j, r: (i, j))
        out_shape = jax.ShapeDtypeStruct((k, n), odt)
    tr = _row_tile(rows, lambda t: 2 * (t * tk * _sz(a.dtype) + t * tn * _sz(b.dtype) + tk * tn * _sz(odt)) + tk * tn * 4)
    nsteps = rows // tr

    def body(*refs):
        a_ref, b_ref = refs[0], refs[1]
        o_ref, acc_ref = refs[-2], refs[-1]
        r = pl.program_id(2)

        @pl.when(r == 0)
        def _():
            acc_ref[...] = jnp.zeros_like(acc_ref)

        acc_ref[...] += lax.dot_general(a_ref[...].astype(BF16), b_ref[...].astype(BF16), (((0,), (0,)), ((), ())),
                                        preferred_element_type=F32)

        @pl.when(r == nsteps - 1)
        def _():
            o_ref[...] = acc_ref[...].astype(odt)

    if halves:
        per_half = (n // 2) // tn
        b_spec = pl.BlockSpec((None, tr, tn), lambda i, j, r: (j // per_half, r, j % per_half))
    else:
        b_spec = pl.BlockSpec((tr, tn), lambda i, j, r: (r, j))
    return _pcall(body, name=name, out_shape=out_shape, grid=(k // tk, n // tn, nsteps),
                  in_specs=[pl.BlockSpec((tr, tk), lambda i, j, r: (r, i)), b_spec], out_specs=out_spec,
                  scratch=[pltpu.VMEM((tk, tn), F32)], sem=("parallel", "parallel", "arbitrary"))(a, b)


def ffn_in_swiglu(h, w_in, name, comm=None):
    m, k = h.shape
    c = w_in.shape[2]
    tm = 512

    def body(h_ref, wa_ref, wb_ref, f_ref, u_ref):
        hv = h_ref[...]
        a = jnp.dot(hv, wa_ref[...], preferred_element_type=F32)
        b = jnp.dot(hv, wb_ref[...], preferred_element_type=F32)
        f_ref[...] = (a * _sigmoid(a) * b).astype(BF16)
        u_ref[0] = a.astype(BF16)
        u_ref[1] = b.astype(BF16)

    return _pcall(body, name=name,
                  out_shape=(jax.ShapeDtypeStruct((m, FH), BF16), jax.ShapeDtypeStruct((2, m, FH), BF16)),
                  grid=(2, m // tm),
                  in_specs=[pl.BlockSpec((tm, k), lambda j, i: (i, 0)),
                            pl.BlockSpec((None, k, c), lambda j, i: (j, 0, 0)),
                            pl.BlockSpec((None, k, c), lambda j, i: (2 + j, 0, 0))],
                  out_specs=(pl.BlockSpec((tm, c), lambda j, i: (i, j)), pl.BlockSpec((2, tm, c), lambda j, i: (0, i, j))),
                  sem=("parallel", "parallel"), comm=comm)(h, w_in, w_in)


def d_f_swiglu(dof, w_out, u2, name):
    m, k = dof.shape
    c = FH // 2
    tm = 512

    def body(d_ref, w_ref, u_ref, du_ref, wt_ref):
        @pl.when(pl.program_id(1) == 0)
        def _():
            wt_ref[...] = w_ref[...].T

        df = jnp.dot(d_ref[...], wt_ref[...], preferred_element_type=F32)
        a, b = u_ref[0].astype(F32), u_ref[1].astype(F32)
        sg = _sigmoid(a)
        du_ref[0] = (df * b * (sg * (1.0 + a * (1.0 - sg)))).astype(BF16)
        du_ref[1] = (df * a * sg).astype(BF16)

    ublk = pl.BlockSpec((2, tm, c), lambda j, i: (0, i, j))
    return _pcall(body, name=name, out_shape=jax.ShapeDtypeStruct((2, m, FH), BF16), grid=(2, m // tm),
                  in_specs=[pl.BlockSpec((tm, k), lambda j, i: (i, 0)), pl.BlockSpec((c, k), lambda j, i: (j, 0)), ublk],
                  out_specs=ublk, scratch=[pltpu.VMEM((k, c), w_out.dtype)], sem=("parallel", "arbitrary"))(dof, w_out, u2)


GATE_TN = 512


def gate_mm_fwd(h, wi_gate, o, hs, z, wo, wc, ws, name):
    m, k = h.shape
    tm, tn = 512, min(GATE_TN, D)
    nj = D // tn

    def body(h_ref, g0_ref, g1_ref, g2_ref, o_ref, hs_ref, z_ref, wo_ref, wc_ref, ws_ref, m_ref, g_ref, y_ref):
        hv = h_ref[...]
        acc = jnp.zeros((tm, tn), F32)
        for g, (gw_ref, x_ref, w_ref) in enumerate(((g0_ref, o_ref, wo_ref), (g1_ref, hs_ref, wc_ref),
                                                    (g2_ref, z_ref, ws_ref))):
            gate = _sigmoid(jnp.dot(hv, gw_ref[...], preferred_element_type=F32))
            y = jnp.dot(x_ref[...], w_ref[...], preferred_element_type=F32)
            acc += gate * y
            g_ref[g] = gate.astype(BF16)
            y_ref[g] = y.astype(BF16)
        m_ref[...] = acc.astype(BF16)

    def gate_w(g):
        return pl.BlockSpec((k, tn), lambda j, i: (0, g * nj + j))

    def branch(width):
        return pl.BlockSpec((tm, width), lambda j, i: (i, 0))

    def branch_w(width):
        return pl.BlockSpec((width, tn), lambda j, i: (0, j))

    stacked = pl.BlockSpec((3, tm, tn), lambda j, i: (0, i, j))
    sds3 = jax.ShapeDtypeStruct((3, m, D), BF16)
    return _pcall(body, name=name, out_shape=(jax.ShapeDtypeStruct((m, D), BF16), sds3, sds3), grid=(nj, m // tm),
                  in_specs=[pl.BlockSpec((tm, k), lambda j, i: (i, 0)), gate_w(0), gate_w(1), gate_w(2),
                            branch(o.shape[1]), branch(hs.shape[1]), branch(z.shape[1]),
                            branch_w(wo.shape[0]), branch_w(wc.shape[0]), branch_w(ws.shape[0])],
                  out_specs=(pl.BlockSpec((tm, tn), lambda j, i: (i, j)), stacked, stacked),
                  sem=("parallel", "parallel"))(h, wi_gate, wi_gate, wi_gate, o, hs, z, wo, wc, ws)


def d_merged_gate(dmixed, w_mix, gates, ys, name):
    m, k = dmixed.shape
    tm = 256

    def body(d_ref, w_ref, g_ref, y_ref, da_ref, db_ref, ds_ref, dp_ref, wt_ref):
        @pl.when(pl.program_id(0) == 0)
        def _():
            wt_ref[...] = w_ref[...].T

        dm = jnp.dot(d_ref[...], wt_ref[...], preferred_element_type=F32)
        for g, dy_ref in enumerate((da_ref, db_ref, ds_ref)):
            gate = g_ref[g].astype(F32)
            dy_ref[...] = (dm * gate).astype(BF16)
            dp_ref[:, g * D:(g + 1) * D] = (dm * y_ref[g].astype(F32) * gate * (1.0 - gate)).astype(BF16)

    stacked = pl.BlockSpec((3, tm, D), lambda i: (0, i, 0))
    row = pl.BlockSpec((tm, D), lambda i: (i, 0))
    sds = jax.ShapeDtypeStruct((m, D), BF16)
    return _pcall(body, name=name, out_shape=(sds, sds, sds, jax.ShapeDtypeStruct((m, 3 * D), BF16)), grid=(m // tm,),
                  in_specs=[pl.BlockSpec((tm, k), lambda i: (i, 0)), pl.BlockSpec((D, k), lambda i: (0, 0)),
                            stacked, stacked],
                  out_specs=(row, row, row, pl.BlockSpec((tm, 3 * D), lambda i: (i, 0))),
                  scratch=[pltpu.VMEM((k, D), w_mix.dtype)], sem=("arbitrary",))(dmixed, w_mix, gates, ys)


def _mods_spec():
    return pl.BlockSpec((1, 1, NMOD * D), lambda t: (_seg(t), 0, 0))


def _rows(width):
    return pl.BlockSpec((TM, width), lambda t: (t, 0))


def norm_mod_fwd(x, mods, k_sh, k_sc, name):
    def body(x_ref, m_ref, h_ref):
        x = x_ref[...]
        r = lax.rsqrt(jnp.mean(x * x, axis=-1, keepdims=True) + EPS)
        sh = m_ref[0, :, k_sh * D:(k_sh + 1) * D]
        sc = m_ref[0, :, k_sc * D:(k_sc + 1) * D]
        h_ref[...] = (x * r * (1.0 + sc) + sh).astype(BF16)

    return _pcall(body, name=name, out_shape=jax.ShapeDtypeStruct((NROW, D), BF16), grid=(NT,),
                  in_specs=[_rows(D), _mods_spec()], out_specs=_rows(D), sem=("parallel",))(x, mods)


def _accumulate_slot(t, ref, part):
    first = (t % TPE) <= 1

    @pl.when(first)
    def _():
        ref[0] = part

    @pl.when(jnp.logical_not(first))
    def _():
        ref[0] += part


def norm_mod_bwd(x, mods, dh, dres, k_sc, name):
    def body(x_ref, m_ref, dh_ref, dres_ref, dx_ref, dp_ref):
        t = pl.program_id(0)
        x = x_ref[...]
        r = lax.rsqrt(jnp.mean(x * x, axis=-1, keepdims=True) + EPS)
        xn = x * r
        sc = m_ref[0, :, k_sc * D:(k_sc + 1) * D]
        dh = dh_ref[...]
        dxn = dh * (1.0 + sc)
        dx_ref[...] = r * (dxn - xn * jnp.mean(dxn * xn, axis=-1, keepdims=True)) + dres_ref[...]
        part = jnp.concatenate([jnp.sum(dh, axis=0, keepdims=True), jnp.sum(dh * xn, axis=0, keepdims=True)], axis=1)
        _accumulate_slot(t, dp_ref, part)

    return _pcall(body, name=name,
                  out_shape=(jax.ShapeDtypeStruct((NROW, D), F32), jax.ShapeDtypeStruct((2 * BL, 1, 2 * D), F32)),
                  grid=(NT,), in_specs=[_rows(D), _mods_spec(), _rows(D), _rows(D)],
                  out_specs=(_rows(D), pl.BlockSpec((1, 1, 2 * D), lambda t: (_slot(t), 0, 0))),
                  sem=("arbitrary",))(x, mods, dh, dres)


def resid_norm_fwd(x, y, mods_g, k_g, mods_n, k_sh, k_sc, name):
    def body(x_ref, y_ref, mg_ref, mn_ref, x1_ref, h_ref):
        x1 = x_ref[...] + mg_ref[0, :, k_g * D:(k_g + 1) * D] * y_ref[...]
        x1_ref[...] = x1
        r = lax.rsqrt(jnp.mean(x1 * x1, axis=-1, keepdims=True) + EPS)
        sh = mn_ref[0, :, k_sh * D:(k_sh + 1) * D]
        sc = mn_ref[0, :, k_sc * D:(k_sc + 1) * D]
        h_ref[...] = (x1 * r * (1.0 + sc) + sh).astype(BF16)

    return _pcall(body, name=name,
                  out_shape=(jax.ShapeDtypeStruct((NROW, D), F32), jax.ShapeDtypeStruct((NROW, D), BF16)), grid=(NT,),
                  in_specs=[_rows(D), _rows(D), _mods_spec(), _mods_spec()], out_specs=(_rows(D), _rows(D)),
                  sem=("parallel",))(x, y, mods_g, mods_n)


def norm_resid_bwd(x, mods_n, dh, dres, k_sc, y, mods_g, k_g, name):
    def body(x_ref, mn_ref, dh_ref, dres_ref, y_ref, mg_ref, dx_ref, dpn_ref, dy_ref, dpg_ref):
        t = pl.program_id(0)
        x = x_ref[...]
        r = lax.rsqrt(jnp.mean(x * x, axis=-1, keepdims=True) + EPS)
        xn = x * r
        sc = mn_ref[0, :, k_sc * D:(k_sc + 1) * D]
        dh = dh_ref[...]
        dxn = dh * (1.0 + sc)
        dx = r * (dxn - xn * jnp.mean(dxn * xn, axis=-1, keepdims=True)) + dres_ref[...]
        dx_ref[...] = dx
        dy_ref[...] = (dx * mg_ref[0, :, k_g * D:(k_g + 1) * D]).astype(BF16)
        part = jnp.concatenate([jnp.sum(dh, axis=0, keepdims=True), jnp.sum(dh * xn, axis=0, keepdims=True)], axis=1)
        _accumulate_slot(t, dpn_ref, part)
        _accumulate_slot(t, dpg_ref, jnp.sum(dx * y_ref[...], axis=0, keepdims=True))

    def slot(width):
        return pl.BlockSpec((1, 1, width), lambda t: (_slot(t), 0, 0))

    return _pcall(body, name=name,
                  out_shape=(jax.ShapeDtypeStruct((NROW, D), F32), jax.ShapeDtypeStruct((2 * BL, 1, 2 * D), F32),
                             jax.ShapeDtypeStruct((NROW, D), BF16), jax.ShapeDtypeStruct((2 * BL, 1, D), F32)),
                  grid=(NT,), in_specs=[_rows(D), _mods_spec(), _rows(D), _rows(D), _rows(D), _mods_spec()],
                  out_specs=(_rows(D), slot(2 * D), _rows(D), slot(D)), sem=("arbitrary",))(x, mods_n, dh, dres, y, mods_g)


def gate_resid_fwd(x, y, mods, k_g, name):
    def body(x_ref, y_ref, m_ref, o_ref):
        o_ref[...] = x_ref[...] + m_ref[0, :, k_g * D:(k_g + 1) * D] * y_ref[...]

    return _pcall(body, name=name, out_shape=jax.ShapeDtypeStruct((NROW, D), F32), grid=(NT,),
                  in_specs=[_rows(D), _rows(D), _mods_spec()], out_specs=_rows(D), sem=("parallel",))(x, y, mods)


def gate_resid_bwd(dx, y, mods, k_g, name):
    def body(dx_ref, y_ref, m_ref, dy_ref, dp_ref):
        t = pl.program_id(0)
        dx = dx_ref[...]
        dy_ref[...] = (dx * m_ref[0, :, k_g * D:(k_g + 1) * D]).astype(BF16)
        _accumulate_slot(t, dp_ref, jnp.sum(dx * y_ref[...], axis=0, keepdims=True))

    return _pcall(body, name=name,
                  out_shape=(jax.ShapeDtypeStruct((NROW, D), BF16), jax.ShapeDtypeStruct((2 * BL, 1, D), F32)),
                  grid=(NT,), in_specs=[_rows(D), _rows(D), _mods_spec()],
                  out_specs=(_rows(D), pl.BlockSpec((1, 1, D), lambda t: (_slot(t), 0, 0))),
                  sem=("arbitrary",))(dx, y, mods)


def _swap16(y, lo16):
    return jnp.where(lo16, pltpu.roll(y, LANE - 16, 1), pltpu.roll(y, 16, 1))


def _group_mean(v, g_mat):
    return jnp.dot(v, g_mat, precision=HIGHEST, preferred_element_type=F32)


def qkv_fwd(p_main, cos_t, sin_t, g_mat, gq, gk, name):
    def body(p_ref, cos_ref, sin_ref, g_ref, gq_ref, gk_ref, q_ref, k_ref, v_ref):
        cos, sin, g_mat_v = cos_ref[...], sin_ref[...], g_ref[...]
        lo16 = (lax.broadcasted_iota(jnp.int32, (TM, LANE), 1) % 32) < 16

        def block(xb, g):
            xb = xb.astype(F32)
            r = lax.rsqrt(_group_mean(xb * xb, g_mat_v) + EPS)
            y = xb * r * g
            return y * cos + _swap16(y, lo16) * sin

        for j in range(AW // LANE):
            q_ref[:, j * LANE:(j + 1) * LANE] = (block(p_ref[:, j * LANE:(j + 1) * LANE], gq_ref[...])
                                                 * ATTN_SCALE).astype(BF16)
        lo = lax.broadcasted_iota(jnp.int32, (TM, LANE), 1) < HD
        for src, dst_ref in ((block(p_ref[:, OFF_K:OFF_K + LANE], gk_ref[...]), k_ref), (p_ref[:, OFF_V:OFF_V + LANE].astype(F32), v_ref)):
            swapped = pltpu.roll(src, HD, 1)
            dst_ref[:, 0:LANE] = jnp.where(lo, src, swapped).astype(BF16)
            dst_ref[:, LANE:2 * LANE] = jnp.where(lo, swapped, src).astype(BF16)

    tab = pl.BlockSpec((TM, LANE), lambda t: (t % TPE, 0))
    small = pl.BlockSpec((1, LANE), lambda t: (0, 0))
    return _pcall(body, name=name,
                  out_shape=(jax.ShapeDtypeStruct((NROW, AW), BF16), jax.ShapeDtypeStruct((NROW, 2 * KVW), BF16),
                             jax.ShapeDtypeStruct((NROW, 2 * KVW), BF16)),
                  grid=(NT,),
                  in_specs=[_rows(QKVW), tab, tab, pl.BlockSpec((LANE, LANE), lambda t: (0, 0)), small, small],
                  out_specs=(_rows(AW), _rows(2 * KVW), _rows(2 * KVW)),
                  sem=("parallel",))(p_main, cos_t, sin_t, g_mat, gq, gk)


def qkv_bwd(p_main, cos_t, sin_t, g_mat, gq, gk, dq, dk, dv, name, comm=None):
    def body(p_ref, cos_ref, sin_ref, g_ref, gq_ref, gk_ref, dq_ref, dk_ref, dv_ref, dp_ref, dg_ref):
        t = pl.program_id(0)
        cos, sin, g_mat_v = cos_ref[...], sin_ref[...], g_ref[...]
        lo16 = (lax.broadcasted_iota(jnp.int32, (TM, LANE), 1) % 32) < 16

        def block(xb, g, dyr):
            xb = xb.astype(F32)
            r = lax.rsqrt(_group_mean(xb * xb, g_mat_v) + EPS)
            xn = xb * r
            dy = dyr * cos + _swap16(dyr * sin, lo16)
            dgl = jnp.sum(dy * xn, axis=0, keepdims=True)
            dxn = dy * g
            return r * (dxn - xn * _group_mean(dxn * xn, g_mat_v)), dgl

        parts = []
        for j in range(AW // LANE):
            sl = slice(j * LANE, (j + 1) * LANE)
            dxb, dgl = block(p_ref[:, sl], gq_ref[...], dq_ref[:, sl] * ATTN_SCALE)
            dp_ref[:, sl] = dxb.astype(BF16)
            parts.append(dgl)
        lo = lax.broadcasted_iota(jnp.int32, (TM, LANE), 1) < HD

        def fold(d_ref):
            d0, d1 = d_ref[:, 0:LANE], d_ref[:, LANE:2 * LANE]
            return jnp.where(lo, d0 + pltpu.roll(d0, HD, 1), d1 + pltpu.roll(d1, HD, 1))

        dxb, dgl = block(p_ref[:, OFF_K:OFF_K + LANE], gk_ref[...], fold(dk_ref))
        dp_ref[:, OFF_K:OFF_K + LANE] = dxb.astype(BF16)
        parts.append(dgl)
        parts.append(jnp.zeros((1, LANE), F32))
        dp_ref[:, OFF_V:OFF_V + LANE] = fold(dv_ref).astype(BF16)
        part = jnp.concatenate(parts, axis=1)

        @pl.when(t == 0)
        def _():
            dg_ref[...] = part

        @pl.when(t != 0)
        def _():
            dg_ref[...] += part

    tab = pl.BlockSpec((TM, LANE), lambda t: (t % TPE, 0))
    small = pl.BlockSpec((1, LANE), lambda t: (0, 0))
    return _pcall(body, name=name,
                  out_shape=(jax.ShapeDtypeStruct((NROW, QKVW), BF16), jax.ShapeDtypeStruct((1, QKVW), F32)),
                  grid=(NT,),
                  in_specs=[_rows(QKVW), tab, tab, pl.BlockSpec((LANE, LANE), lambda t: (0, 0)), small, small,
                            _rows(AW), _rows(2 * KVW), _rows(2 * KVW)],
                  out_specs=(_rows(QKVW), pl.BlockSpec((1, QKVW), lambda t: (0, 0))),
                  sem=("arbitrary",), comm=comm)(p_main, cos_t, sin_t, g_mat, gq, gk, dq, dk, dv)


def _layer_norm_parts(yc):
    mu = jnp.mean(yc, axis=-1, keepdims=True)
    xc = yc - mu
    rs = lax.rsqrt(jnp.mean(xc * xc, axis=-1, keepdims=True) + EPS)
    return xc * rs, rs


def ln_silu_fwd(yc, g, b, name):
    def body(y_ref, g_ref, b_ref, o_ref):
        nrm, _ = _layer_norm_parts(y_ref[...])
        ln = nrm * g_ref[...] + b_ref[...]
        o_ref[...] = (ln * _sigmoid(ln)).astype(BF16)

    vec = pl.BlockSpec((1, CW), lambda t: (0, 0))
    return _pcall(body, name=name, out_shape=jax.ShapeDtypeStruct((NROW, CW), BF16), grid=(NT,),
                  in_specs=[_rows(CW), vec, vec], out_specs=_rows(CW), sem=("parallel",))(yc, g, b)


def ln_silu_bwd(yc, g, b, dhs, name):
    def body(y_ref, g_ref, b_ref, dh_ref, dy_ref, dg_ref, db_ref):
        t = pl.program_id(0)
        nrm, rs = _layer_norm_parts(y_ref[...])
        ln = nrm * g_ref[...] + b_ref[...]
        sg = _sigmoid(ln)
        dln = dh_ref[...] * (sg * (1.0 + ln * (1.0 - sg)))
        dn = dln * g_ref[...]
        dy_ref[...] = rs * (dn - jnp.mean(dn, axis=-1, keepdims=True)
                            - nrm * jnp.mean(dn * nrm, axis=-1, keepdims=True))
        pg = jnp.sum(dln * nrm, axis=0, keepdims=True)
        pb = jnp.sum(dln, axis=0, keepdims=True)

        @pl.when(t == 0)
        def _():
            dg_ref[...] = pg
            db_ref[...] = pb

        @pl.when(t != 0)
        def _():
            dg_ref[...] += pg
            db_ref[...] += pb

    vec = pl.BlockSpec((1, CW), lambda t: (0, 0))
    return _pcall(body, name=name,
                  out_shape=(jax.ShapeDtypeStruct((NROW, CW), F32), jax.ShapeDtypeStruct((1, CW), F32),
                             jax.ShapeDtypeStruct((1, CW), F32)),
                  grid=(NT,), in_specs=[_rows(CW), vec, vec, _rows(CW)], out_specs=(_rows(CW), vec, vec),
                  sem=("arbitrary",))(yc, g, b, dhs)


def loss_fwd_bwd(y, target, name):
    def body(y_ref, t_ref, dy_ref, l_ref):
        t = pl.program_id(0)
        latent = (t % TPE) != 0
        err = jnp.where(latent, y_ref[...] - t_ref[...], 0.0)
        dy_ref[...] = err * (1.0 / D)
        part = jnp.sum(err * err, axis=0, keepdims=True)

        @pl.when(t == 0)
        def _():
            l_ref[...] = part

        @pl.when(t != 0)
        def _():
            l_ref[...] += part

    tgt = pl.BlockSpec((TM, D), lambda t: ((t // TPE) * (TPE - 1) + jnp.maximum(t % TPE - 1, 0), 0))
    return _pcall(body, name=name,
                  out_shape=(jax.ShapeDtypeStruct((NROW, D), F32), jax.ShapeDtypeStruct((1, D), F32)),
                  grid=(NT,), in_specs=[_rows(D), tgt], out_specs=(_rows(D), pl.BlockSpec((1, D), lambda t: (0, 0))),
                  sem=("arbitrary",))(y, target)


QB_PER_KV = AW // LANE // NKV


def _softmax_parts(qm, k):
    s = lax.dot_general(qm, k, (((1,), (1,)), ((), ())), preferred_element_type=F32)
    e = jnp.exp(s - jnp.max(s, axis=-1, keepdims=True))
    return e, 1.0 / jnp.sum(e, axis=-1, keepdims=True)


def _lane_halves():
    lo = lax.broadcasted_iota(jnp.int32, (TM, LANE), 1) < HD
    return lo, jnp.logical_not(lo)


def _stack_heads(x, halves):
    zero = jnp.zeros_like(x)
    return jnp.concatenate([jnp.where(halves[0], x, zero), jnp.where(halves[1], x, zero)], axis=0)


def attn_fwd(q, k, v, name, comm=None):
    def body(q_ref, k_ref, v_ref, o_ref):
        t = pl.program_id(2)
        halves = _lane_halves()

        def run(nk):
            kv, vv = k_ref[0:nk, :], v_ref[0:nk, :]
            for j in range(QB_PER_KV):
                lanes = slice(j * LANE, (j + 1) * LANE)
                e, rinv = _softmax_parts(_stack_heads(q_ref[:, lanes], halves), kv)
                out = jnp.dot(e.astype(BF16), vv, preferred_element_type=F32) * rinv
                o_ref[:, lanes] = jnp.where(halves[0], out[0:TM], out[TM:2 * TM]).astype(BF16)

        @pl.when(t == 0)
        def _():
            run(CTX)

        @pl.when(t != 0)
        def _():
            run(RE)

    qs = pl.BlockSpec((TM, QB_PER_KV * LANE), lambda b, h, t: (b * TPE + t, h))
    ks = pl.BlockSpec((RE, LANE), lambda b, h, t: (b, h))
    return _pcall(body, name=name, out_shape=jax.ShapeDtypeStruct((NROW, AW), BF16), grid=(BL, NKV, TPE),
                  in_specs=[qs, ks, ks], out_specs=qs, sem=("parallel",) * 3, comm=comm)(q, k, v)


def attn_bwd(q, k, v, o, do, name, comm=None):
    def body(q_ref, k_ref, v_ref, o_ref, do_ref, dq_ref, dk_ref, dv_ref):
        t = pl.program_id(2)
        halves = _lane_halves()

        @pl.when(t == 0)
        def _():
            dk_ref[...] = jnp.zeros_like(dk_ref)
            dv_ref[...] = jnp.zeros_like(dv_ref)

        def run(nk):
            kv, vv = k_ref[0:nk, :], v_ref[0:nk, :]
            dks, dvs = [], []
            for j in range(QB_PER_KV):
                lanes = slice(j * LANE, (j + 1) * LANE)
                q2, do2 = _stack_heads(q_ref[:, lanes], halves), _stack_heads(do_ref[:, lanes], halves)
                ov = o_ref[:, lanes].astype(F32)
                delta = jnp.sum(do2.astype(F32) * jnp.concatenate([ov, ov], axis=0), axis=-1, keepdims=True)
                e, rinv = _softmax_parts(q2, kv)
                p = e * rinv
                dvs.append(lax.dot_general(p.astype(BF16), do2, (((0,), (0,)), ((), ())), preferred_element_type=F32))
                dp = lax.dot_general(do2, vv, (((1,), (1,)), ((), ())), preferred_element_type=F32)
                ds = (p * (dp - delta)).astype(BF16)
                dq = jnp.dot(ds, kv, preferred_element_type=F32)
                dks.append(lax.dot_general(ds, q2, (((0,), (0,)), ((), ())), preferred_element_type=F32))
                dq_ref[:, lanes] = jnp.where(halves[0], dq[0:TM], dq[TM:2 * TM])
            dv_ref[0:nk, :] += functools.reduce(jnp.add, dvs)
            dk_ref[0:nk, :] += functools.reduce(jnp.add, dks)

        @pl.when(t == 0)
        def _():
            run(CTX)

        @pl.when(t != 0)
        def _():
            run(RE)

    qs = pl.BlockSpec((TM, QB_PER_KV * LANE), lambda b, h, t: (b * TPE + t, h))
    ks = pl.BlockSpec((RE, LANE), lambda b, h, t: (b, h))
    return _pcall(body, name=name,
                  out_shape=(jax.ShapeDtypeStruct((NROW, AW), F32), jax.ShapeDtypeStruct((NROW, 2 * KVW), F32),
                             jax.ShapeDtypeStruct((NROW, 2 * KVW), F32)),
                  grid=(BL, NKV, TPE), in_specs=[qs, ks, ks, qs, qs], out_specs=(qs, ks, ks),
                  sem=("parallel", "parallel", "arbitrary"), comm=comm)(q, k, v, o, do)


CONV_SEGS = ((0, CTX), (CTX, SEQ))


def _p_block(col0):
    return pl.BlockSpec((RE, LANE), lambda cb, b: (b, col0 // LANE + cb))


def _conv_io(width):
    return pl.BlockSpec((RE, LANE), lambda cb, b: (b, cb))


def _taps(n):
    return pl.BlockSpec((n, LANE), lambda cb, b: (0, cb))


def _fill_pad(pad_ref, length, values):
    pad_ref[0:PADR, :] = jnp.zeros((PADR, LANE), F32)
    pad_ref[PADR + length:2 * PADR + length, :] = jnp.zeros((PADR, LANE), F32)
    pad_ref[PADR:PADR + length, :] = values


def _conv_chunk(pad_ref, w_ref, ntap, c0, first_row):
    acc = jnp.zeros((CONV_CH, LANE), F32)
    for kk in range(ntap):
        r0 = c0 + first_row(kk)
        acc += w_ref[kk:kk + 1, :] * pad_ref[r0:r0 + CONV_CH, :]
    return acc


def conv_fwd(p_main, wdw, bdw, w3, name):
    def body(a_ref, g_ref, bg_ref, cg_ref, xs_ref, w_ref, b_ref, w3_ref, yc_ref, z_ref, pad_ref):
        for off, length in CONV_SEGS:
            rows = slice(off, off + length)
            _fill_pad(pad_ref, length, a_ref[rows, :].astype(F32) * _sigmoid(g_ref[rows, :].astype(F32)))
            for c0 in range(0, length, CONV_CH):
                acc = _conv_chunk(pad_ref, w_ref, CONF_K, c0, lambda kk: PADR + kk - CONF_K // 2)
                yc_ref[off + c0:off + c0 + CONV_CH, :] = acc + b_ref[...]
            pad_ref[PADR:PADR + length, :] = cg_ref[rows, :].astype(F32) * xs_ref[rows, :].astype(F32)
            for c0 in range(0, length, CONV_CH):
                acc = _conv_chunk(pad_ref, w3_ref, SC_K, c0, lambda kk: PADR + kk - SC_K // 2)
                z_ref[off + c0:off + c0 + CONV_CH, :] = (bg_ref[off + c0:off + c0 + CONV_CH, :] * acc).astype(BF16)

    return _pcall(body, name=name,
                  out_shape=(jax.ShapeDtypeStruct((NROW, CW), F32), jax.ShapeDtypeStruct((NROW, CW), BF16)),
                  grid=(CB, BL),
                  in_specs=[_p_block(OFF_CONF), _p_block(OFF_CONF + CW), _p_block(OFF_SC), _p_block(OFF_SC + CW),
                            _p_block(OFF_SC + 2 * CW), _taps(CONF_K), _taps(1), _taps(SC_K)],
                  out_specs=(_conv_io(CW), _conv_io(CW)),
                  scratch=[pltpu.VMEM((SEQ + 2 * PADR, LANE), F32)],
                  sem=("parallel", "parallel"))(p_main, p_main, p_main, p_main, p_main, wdw, bdw, w3)


def _tap_grad(pad_ref, d_ref, off, length, first_row):
    acc = jnp.zeros((8, LANE), F32)
    for c0 in range(0, length, CONV_CH):
        prod = d_ref[off + c0:off + c0 + CONV_CH, :] * pad_ref[c0 + first_row:c0 + first_row + CONV_CH, :]
        acc += jnp.sum(prod.reshape(CONV_CH // 8, 8, LANE), axis=0)
    return jnp.sum(acc, axis=0, keepdims=True)


def conv_bwd(p_main, wdw, w3, dyc, dz, name):
    def body(a_ref, g_ref, bg_ref, cg_ref, xs_ref, w_ref, w3_ref, dyc_ref, dz_ref,
             da_ref, dg_ref, dbg_ref, dcg_ref, dxs_ref, dw_ref, db_ref, dw3_ref, pad_x, pad_d, dconv_ref):
        b = pl.program_id(1)

        @pl.when(b == 0)
        def _():
            dw_ref[...] = jnp.zeros_like(dw_ref)
            db_ref[...] = jnp.zeros_like(db_ref)
            dw3_ref[...] = jnp.zeros_like(dw3_ref)

        db_ref[...] += jnp.sum(dyc_ref[...], axis=0, keepdims=True)
        for off, length in CONV_SEGS:
            rows = slice(off, off + length)
            _fill_pad(pad_x, length, a_ref[rows, :].astype(F32) * _sigmoid(g_ref[rows, :].astype(F32)))
            _fill_pad(pad_d, length, dyc_ref[rows, :])
            for kk in range(CONF_K):
                dw_ref[kk:kk + 1, :] += _tap_grad(pad_x, dyc_ref, off, length, PADR + kk - CONF_K // 2)
            for c0 in range(0, length, CONV_CH):
                dh = _conv_chunk(pad_d, w_ref, CONF_K, c0, lambda kk: PADR + CONF_K // 2 - kk)
                ch = slice(off + c0, off + c0 + CONV_CH)
                sg = _sigmoid(g_ref[ch, :].astype(F32))
                da_ref[ch, :] = (dh * sg).astype(BF16)
                dg_ref[ch, :] = (dh * a_ref[ch, :] * sg * (1.0 - sg)).astype(BF16)
            pad_x[PADR:PADR + length, :] = cg_ref[rows, :].astype(F32) * xs_ref[rows, :].astype(F32)
            dconv_ref[rows, :] = dz_ref[rows, :] * bg_ref[rows, :]
            pad_d[PADR:PADR + length, :] = dconv_ref[rows, :]
            for kk in range(SC_K):
                dw3_ref[kk:kk + 1, :] += _tap_grad(pad_x, dconv_ref, off, length, PADR + kk - SC_K // 2)
            for c0 in range(0, length, CONV_CH):
                ch = slice(off + c0, off + c0 + CONV_CH)
                c3 = _conv_chunk(pad_x, w3_ref, SC_K, c0, lambda kk: PADR + kk - SC_K // 2)
                dbg_ref[ch, :] = (dz_ref[ch, :] * c3).astype(BF16)
                dcx = _conv_chunk(pad_d, w3_ref, SC_K, c0, lambda kk: PADR + SC_K // 2 - kk)
                dcg_ref[ch, :] = (dcx * xs_ref[ch, :]).astype(BF16)
                dxs_ref[ch, :] = (dcx * cg_ref[ch, :]).astype(BF16)

    slab = jax.ShapeDtypeStruct((NROW, CW), BF16)
    return _pcall(body, name=name,
                  out_shape=(slab,) * 5 + (jax.ShapeDtypeStruct((CONF_K, CW), F32), jax.ShapeDtypeStruct((1, CW), F32),
                                           jax.ShapeDtypeStruct((SC_K, CW), F32)),
                  grid=(CB, BL),
                  in_specs=[_p_block(OFF_CONF), _p_block(OFF_CONF + CW), _p_block(OFF_SC), _p_block(OFF_SC + CW),
                            _p_block(OFF_SC + 2 * CW), _taps(CONF_K), _taps(SC_K), _conv_io(CW), _conv_io(CW)],
                  out_specs=(_conv_io(CW),) * 5 + (_taps(CONF_K), _taps(1), _taps(SC_K)),
                  scratch=[pltpu.VMEM((SEQ + 2 * PADR, LANE), F32), pltpu.VMEM((SEQ + 2 * PADR, LANE), F32),
                           pltpu.VMEM((RE, LANE), F32)],
                  sem=("parallel", "arbitrary"))(p_main, p_main, p_main, p_main, p_main, wdw, w3, dyc, dz)


def silu_rows(x, name):
    def body(x_ref, o_ref):
        o_ref[...] = x_ref[...] * _sigmoid(x_ref[...])

    return _pcall(body, name=name, out_shape=jax.ShapeDtypeStruct(x.shape, F32))(x)


def silu_rows_bwd(x, dcs, name):
    def body(x_ref, d_ref, o_ref):
        x = x_ref[...]
        sg = _sigmoid(x)
        tot = d_ref[0]
        for i in range(1, DEPTH):
            tot += d_ref[i]
        o_ref[...] = tot * (sg * (1.0 + x * (1.0 - sg)))

    return _pcall(body, name=name, out_shape=jax.ShapeDtypeStruct(x.shape, F32))(x, dcs)


def dmod_assemble(parts, name):
    def body(p_ref, dm_ref, db_ref):
        row = lax.broadcasted_iota(jnp.int32, (8, NMOD * D), 0)
        dm = jnp.zeros((8, NMOD * D), F32)
        db = jnp.zeros((1, NMOD * D), F32)
        for s in range(2 * BL):
            target = BL if s % 2 == 0 else s // 2
            part = p_ref[s:s + 1, :]
            dm += jnp.where(row == target, part, 0.0)
            db += part
        dm_ref[...] = dm
        db_ref[...] = db

    return _pcall(body, name=name, out_shape=(jax.ShapeDtypeStruct((8, NMOD * D), F32),
                                              jax.ShapeDtypeStruct((1, NMOD * D), F32)))(parts)


def sum_leading(x, name):
    n = x.shape[0]
    tr = _pick(x.shape[1], (256, 32, 8))

    def body(x_ref, o_ref):
        tot = x_ref[0].astype(F32)
        for i in range(1, n):
            tot += x_ref[i].astype(F32)
        o_ref[...] = tot

    return _pcall(body, name=name, out_shape=jax.ShapeDtypeStruct(x.shape[1:], F32), grid=(x.shape[1] // tr,),
                  in_specs=[pl.BlockSpec((n, tr, x.shape[2]), lambda i: (0, i, 0))],
                  out_specs=pl.BlockSpec((tr, x.shape[2]), lambda i: (i, 0)), sem=("parallel",))(x)


SLAB_ROWS = (256, 176, 128, 64, 8)


def _prefetch_call(body, name, out_shape, grid, in_specs, out_specs, sem, scalars, *args):
    spec = pltpu.PrefetchScalarGridSpec(num_scalar_prefetch=len(scalars), grid=grid, in_specs=in_specs,
                                        out_specs=out_specs)
    return pl.pallas_call(body, name=name, out_shape=out_shape, grid_spec=spec,
                          compiler_params=pltpu.CompilerParams(dimension_semantics=sem,
                                                               vmem_limit_bytes=VMEM_LIMIT))(*scalars, *args)


def cast_layers(w, chip, name):
    depth, r, c = w.shape
    tr = _pick(r, SLAB_ROWS)

    def body(s_ref, w_ref, *o_refs):
        for l in range(depth):
            o_refs[l][...] = w_ref[l].astype(BF16)

    slab = pl.BlockSpec((None, tr, c), lambda i, s: (s[0], i, 0))
    return _prefetch_call(body, name, (jax.ShapeDtypeStruct((NCHIP, r, c), BF16),) * depth, (r // tr,),
                          [pl.BlockSpec((depth, tr, c), lambda i, s: (0, i, 0))], (slab,) * depth,
                          ("parallel",), (chip,), w)


def rs_add(g, other, core, chip, name):
    _, r, c = g.shape
    rh = r // 2
    tr = _pick(rh, SLAB_ROWS)
    nblk = rh // tr

    def body(core_ref, chip_ref, g_ref, o_ref, send_ref, arr_ref):
        k = pl.program_id(1)
        tot = (g_ref[...].astype(F32) + o_ref[...].astype(F32)).astype(BF16)
        send_ref[...] = tot

        @pl.when(k == chip_ref[0])
        def _():
            arr_ref[...] = tot

    blk = (None, tr, c)
    return _prefetch_call(
        body, name, (jax.ShapeDtypeStruct(other.shape, BF16), jax.ShapeDtypeStruct(g.shape, BF16)), (nblk, NCHIP),
        [pl.BlockSpec(blk, lambda i, k, cr, ch: (k, cr[0] * nblk + i, 0)), pl.BlockSpec(blk, lambda i, k, cr, ch: (k, i, 0))],
        (pl.BlockSpec(blk, lambda i, k, cr, ch: (k, i, 0)),
         pl.BlockSpec(blk, lambda i, k, cr, ch: (ch[0], cr[0] * nblk + i, 0))),
        ("parallel", "arbitrary"), (core, chip), g, other)


def adamw_layers(w, arrs, m, v, first, prev, name, comm=None):
    depth, r, c = w.shape
    tr = _pick(r, SLAB_ROWS)
    nblk = r // tr
    nl = len(arrs)
    c1 = 1.0 / (1.0 - ADAM_B1 ** ADAM_STEP)
    c2 = 1.0 / (1.0 - ADAM_B2 ** ADAM_STEP)

    def body(w_ref, m_ref, v_ref, *rest):
        a_refs = rest[:nl]
        g_ref, d_ref, mo_ref, vo_ref = rest[nl + 4:nl + 8]
        li = pl.program_id(0)
        gv = None
        for idx, a_ref in enumerate(a_refs):
            tot = a_ref[0].astype(F32)
            for k in range(1, NCHIP):
                tot += a_ref[k].astype(F32)
            gv = tot if gv is None else jnp.where(li == idx, tot, gv)
        mn = ADAM_B1 * m_ref[...] + (1.0 - ADAM_B1) * gv
        vn = ADAM_B2 * v_ref[...] + (1.0 - ADAM_B2) * (gv * gv)
        g_ref[...] = gv
        d_ref[...] = -ADAM_LR * ((mn * c1) / (jnp.sqrt(vn * c2) + ADAM_EPS) + ADAM_WD * w_ref[...])
        mo_ref[...] = mn
        vo_ref[...] = vn

    def arr_spec(idx):
        return pl.BlockSpec((NCHIP, tr, c),
                            lambda li, i: (0, jnp.where(li == idx, i, jnp.where(li < idx, 0, nblk - 1)), 0))

    spec = pl.BlockSpec((None, tr, c), lambda li, i: (first + li, i, 0))
    sds = jax.ShapeDtypeStruct(w.shape, F32)
    return _pcall(body, name=name, out_shape=(sds,) * 4, grid=(nl, nblk),
                  in_specs=[spec, spec, spec] + [arr_spec(idx) for idx in range(nl)] + [ANY] * 4,
                  out_specs=(spec,) * 4, aliases={3 + nl + i: i for i in range(4)},
                  sem=("arbitrary", "arbitrary"), comm=comm)(w, m, v, *arrs, *prev)


def adamw(w, g, m, v, name):
    rows, cols = w.shape
    tr = _pick(rows, (256, 248, 128, 8))
    c1 = 1.0 / (1.0 - ADAM_B1 ** ADAM_STEP)
    c2 = 1.0 / (1.0 - ADAM_B2 ** ADAM_STEP)

    def body(w_ref, g_ref, m_ref, v_ref, d_ref, mo_ref, vo_ref):
        gv = g_ref[...]
        mn = ADAM_B1 * m_ref[...] + (1.0 - ADAM_B1) * gv
        vn = ADAM_B2 * v_ref[...] + (1.0 - ADAM_B2) * (gv * gv)
        d_ref[...] = -ADAM_LR * ((mn * c1) / (jnp.sqrt(vn * c2) + ADAM_EPS) + ADAM_WD * w_ref[...])
        mo_ref[...] = mn
        vo_ref[...] = vn

    spec = pl.BlockSpec((tr, cols), lambda i: (i, 0))
    sds = jax.ShapeDtypeStruct((rows, cols), F32)
    return _pcall(body, name=name, out_shape=(sds, sds, sds), grid=(rows // tr,), in_specs=[spec] * 4,
                  out_specs=(spec, spec, spec), sem=("parallel",))(w, g, m, v)


def _place():
    return lax.axis_index("x"), lax.axis_index("y"), lax.axis_index("c")


def _other_chips(x, y):
    return [(1 - x, y), (x, 1 - y), (1 - x, 1 - y)]


def _comm_call(body, name, out_shape, n_in, nsem):
    return pl.pallas_call(body, name=name, out_shape=out_shape, in_specs=[ANY] * n_in,
                          out_specs=jax.tree.map(lambda _: ANY, out_shape),
                          scratch_shapes=[pltpu.SemaphoreType.DMA((nsem,)), pltpu.SemaphoreType.DMA((nsem,)),
                                          pltpu.SemaphoreType.DMA])


def all_gather8(block, name):
    def body(x_ref, out_ref, send_sems, recv_sems, local_sem):
        x, y, c = _place()
        me, sibling = (x, y, c), (x, y, 1 - c)
        chips = _other_chips(x, y)

        def slot(px, py, pc):
            return out_ref.at[4 * px + 2 * py + pc]

        def copy(k, blk, to, src=None):
            return pltpu.make_async_remote_copy(src_ref=slot(*blk) if src is None else src, dst_ref=slot(*blk),
                                                send_sem=send_sems.at[k], recv_sem=recv_sems.at[k],
                                                device_id=to, device_id_type=MESH)

        mine = pltpu.make_async_copy(x_ref, slot(*me), local_sem)
        mine.start()
        first = [copy(0, me, sibling, src=x_ref)]
        first += [copy(1 + j, me, (*chip, c), src=x_ref) for j, chip in enumerate(chips)]
        for cp in first:
            cp.start()
        passed = [copy(4 + j, (*chip, c), sibling) for j, chip in enumerate(chips)]
        for j, chip in enumerate(chips):
            copy(1 + j, (*chip, c), me).wait_recv()
            passed[j].start()
        copy(0, sibling, me).wait_recv()
        for j, chip in enumerate(chips):
            copy(4 + j, (*chip, 1 - c), me).wait_recv()
        for cp in first + passed:
            cp.wait_send()
        mine.wait()

    return _comm_call(body, name, jax.ShapeDtypeStruct((8,) + block.shape, block.dtype), 1, 7)(block)


def _remote(src, dst, send_sems, recv_sems, k, to):
    return pltpu.make_async_remote_copy(src_ref=src, dst_ref=dst, send_sem=send_sems.at[k], recv_sem=recv_sems.at[k],
                                        device_id=to, device_id_type=MESH)


def _half(ref, slot, core):
    rh = ref.shape[1] // 2
    return ref.at[slot, pl.ds(core * rh, rh)]


def _all_slots_half(ref, core):
    rh = ref.shape[1] // 2
    return ref.at[:, pl.ds(core * rh, rh)]


def gather_ici(bufs):
    def program(ro, rw, new, ss, rs):
        x, y, c = _place()
        own = 2 * x + y
        starts, arrivals = [], []
        for w, ref in enumerate(rw):
            for j, chip in enumerate(_other_chips(x, y)):
                starts.append(_remote(_half(ref, own, c), _half(ref, own, c), ss, rs, 3 * w + j, (*chip, c)))
                arrivals.append(_remote(_half(ref, own, c), _half(ref, 2 * chip[0] + chip[1], c), ss, rs, 3 * w + j,
                                        (*chip, c)))
        return starts, arrivals

    return CommSpec((), tuple(bufs), (), 3 * len(bufs), program)


def gather_d2d(bufs):
    def program(ro, rw, new, ss, rs):
        x, y, c = _place()
        starts, arrivals = [], []
        for w, ref in enumerate(rw):
            for j, chip in enumerate(_other_chips(x, y)):
                slot = 2 * chip[0] + chip[1]
                starts.append(_remote(_half(ref, slot, c), _half(ref, slot, c), ss, rs, 3 * w + j, (x, y, 1 - c)))
                arrivals.append(_remote(_half(ref, slot, c), _half(ref, slot, 1 - c), ss, rs, 3 * w + j, (x, y, 1 - c)))
        return starts, arrivals

    return CommSpec((), tuple(bufs), (), 3 * len(bufs), program)


def rs_swap(grads):
    def program(ro, rw, new, ss, rs):
        x, y, c = _place()
        copies = [_remote(_all_slots_half(g, 1 - c), new[w], ss, rs, w, (x, y, 1 - c)) for w, g in enumerate(ro)]
        return copies, copies

    shapes = tuple(jax.ShapeDtypeStruct((NCHIP, g.shape[1] // 2, g.shape[2]), g.dtype) for g in grads)
    return CommSpec(tuple(grads), (), shapes, len(grads), program)


def rs_ici(sends, arrs):
    def program(ro, rw, new, ss, rs):
        x, y, c = _place()
        own = 2 * x + y
        starts, arrivals = [], []
        for w, (snd, arr) in enumerate(zip(ro, rw)):
            for j, chip in enumerate(_other_chips(x, y)):
                slot = 2 * chip[0] + chip[1]
                starts.append(_remote(snd.at[slot], _half(arr, own, c), ss, rs, 3 * w + j, (*chip, c)))
                arrivals.append(_remote(snd.at[slot], _half(arr, slot, c), ss, rs, 3 * w + j, (*chip, c)))
        return starts, arrivals

    return CommSpec(tuple(sends), tuple(arrs), (), 3 * len(sends), program)


def rs_d2d(arrs):
    def program(ro, rw, new, ss, rs):
        x, y, c = _place()
        starts = [_remote(_all_slots_half(a, c), _all_slots_half(a, c), ss, rs, w, (x, y, 1 - c)) for w, a in enumerate(rw)]
        arrivals = [_remote(_all_slots_half(a, c), _all_slots_half(a, 1 - c), ss, rs, w, (x, y, 1 - c))
                    for w, a in enumerate(rw)]
        return starts, arrivals

    return CommSpec((), tuple(arrs), (), len(arrs), program)


PACK_COLS = 1024
MATMUL_W = ("w_ada", "w_in", "w_attn_o", "w_conf_out", "w_sc_out", "w_mix_out", "w_ffn_in", "w_ffn_out")
ROW_SPLIT = ("w_mix_out", "w_ffn_out")
CONV_W = ("conf_dw_w", "sc_dw_w")
SMALL = ("c_ctx", "b_ada", "q_norm", "k_norm", "conf_dw_b", "conf_ln_g", "conf_ln_b", "conf_dw_w", "sc_dw_w")


def _pack_rows(arrays, row_multiple):
    flat = jnp.concatenate([a.reshape(-1) for a in arrays])
    rows = -(-flat.shape[0] // PACK_COLS)
    rows = -(-rows // row_multiple) * row_multiple
    flat = jnp.pad(flat, (0, rows * PACK_COLS - flat.shape[0]))
    return flat.reshape(rows, PACK_COLS)


def _unpack(flat2d, shapes):
    flat = flat2d.reshape(-1)
    out, pos = [], 0
    for shp in shapes:
        n = 1
        for s in shp:
            n *= s
        out.append(flat[pos:pos + n].reshape(shp))
        pos += n
    return out


def _cols_joined(stacked_layer):
    nchip, r, c = stacked_layer.shape
    return jnp.transpose(stacked_layer, (1, 0, 2)).reshape(r, nchip * c)


def _cols_split(full):
    r, cols = full.shape
    return jnp.transpose(full.reshape(r, NCHIP, cols // NCHIP), (1, 0, 2))


def _rope_tables():
    rows = SEQ // GRID_W
    r_ids = jnp.repeat(jnp.arange(rows, dtype=F32), GRID_W)
    c_ids = jnp.tile(jnp.arange(GRID_W, dtype=F32), rows)
    freqs = ROPE_THETA ** (-jnp.arange(0, HD // 2, 2, dtype=F32) / (HD // 2))
    ang_r, ang_c = r_ids[:, None] * freqs, c_ids[:, None] * freqs
    cos_h = jnp.concatenate([jnp.cos(ang_r), jnp.cos(ang_r), jnp.cos(ang_c), jnp.cos(ang_c)], axis=1)
    sin_h = jnp.concatenate([-jnp.sin(ang_r), jnp.sin(ang_r), -jnp.sin(ang_c), jnp.sin(ang_c)], axis=1)
    cos_t = jnp.concatenate([jnp.ones((CTX, HD), F32), cos_h], axis=0)
    sin_t = jnp.concatenate([jnp.zeros((CTX, HD), F32), sin_h], axis=0)
    return jnp.tile(cos_t, (1, LANE // HD)), jnp.tile(sin_t, (1, LANE // HD))


def _group_matrix():
    gid = jnp.arange(LANE) // HD
    return jnp.where(gid[:, None] == gid[None, :], 1.0 / HD, 0.0).astype(F32)


def _layer_weights(bufs, small, i):
    b = dict(zip(MATMUL_W, bufs))
    wi = _cols_joined(b["w_in"])

    def rows_joined(a):
        return a.reshape(a.shape[0] * a.shape[1], a.shape[2])

    return dict(
        w_ada=(b["w_ada"], "cols"), wi_main=(wi[:, :OFF_GATE], "mat"), wi_gate=(wi[:, OFF_GATE:], "mat"),
        w_attn_o=(_cols_joined(b["w_attn_o"]), "mat"), w_conf_out=(_cols_joined(b["w_conf_out"]), "mat"),
        w_sc_out=(_cols_joined(b["w_sc_out"]), "mat"), w_ffn_in=(b["w_ffn_in"], "cols"),
        w_mix_out=(rows_joined(b["w_mix_out"]), "mat"), w_ffn_out=(rows_joined(b["w_ffn_out"]), "mat"),
        conf_dw_w=small["conf_dw_w"][i], sc_dw_w=small["sc_dw_w"][i], conf_dw_b=small["conf_dw_b"][i][None],
        conf_ln_g=small["conf_ln_g"][i][None], conf_ln_b=small["conf_ln_b"][i][None],
        gq=jnp.tile(small["q_norm"][i], LANE // HD)[None], gk=jnp.tile(small["k_norm"][i], LANE // HD)[None])


def _layer_fwd(i, xs, h, mods, w, tabs, next_bufs, next_layer):
    cos_t, sin_t, g_mat = tabs
    n = f"l{i}_"
    sv = {"x_in": xs, "mods": mods, "h": h}
    sv["p_main"] = mm_nn(sv["h"], w["wi_main"], out_dtype=BF16, name=n + "p_main")
    sv["q"], sv["k"], sv["v"] = qkv_fwd(sv["p_main"], cos_t, sin_t, g_mat, w["gq"], w["gk"], n + "qkv")
    if next_bufs is None:
        sv["o"] = attn_fwd(sv["q"], sv["k"], sv["v"], n + "attn")
    else:
        sv["o"], next_bufs, _ = attn_fwd(sv["q"], sv["k"], sv["v"], n + "attn", comm=gather_ici(next_bufs))
    sv["yc"], sv["z"] = conv_fwd(sv["p_main"], w["conf_dw_w"], w["conf_dw_b"], w["sc_dw_w"], n + "conv")
    sv["hs"] = ln_silu_fwd(sv["yc"], w["conf_ln_g"], w["conf_ln_b"], n + "ln_silu")
    sv["merged"], sv["gates"], sv["ys"] = gate_mm_fwd(sv["h"], w["wi_gate"][0], sv["o"], sv["hs"], sv["z"],
                                                      w["w_attn_o"][0], w["w_conf_out"][0], w["w_sc_out"][0],
                                                      n + "gate_merge")
    sv["mixed"] = mm_nn(sv["merged"], w["w_mix_out"], name=n + "mix")
    sv["x1"], sv["h2"] = resid_norm_fwd(xs, sv["mixed"], mods, 2, mods, 3, 4, n + "resid1_norm2")
    if next_bufs is None:
        sv["f"], sv["u2"] = ffn_in_swiglu(sv["h2"], w["w_ffn_in"][0], n + "ffn_in")
    else:
        (sv["f"], sv["u2"]), next_bufs, _ = ffn_in_swiglu(sv["h2"], w["w_ffn_in"][0], n + "ffn_in",
                                                          comm=gather_d2d(next_bufs))
    sv["of"] = mm_nn(sv["f"], w["w_ffn_out"], name=n + "ffn_out")
    if next_layer is None:
        return gate_resid_fwd(sv["x1"], sv["of"], mods, 5, n + "resid2"), None, sv
    w_next, mods_next = next_layer(next_bufs)
    x2, h_next = resid_norm_fwd(sv["x1"], sv["of"], mods, 5, mods_next, 0, 1, n + "resid2_norm1")
    return x2, (h_next, w_next, mods_next), sv


def _layer_bwd(i, dx2, dof, dm5, sv, w, tabs, cs, pending, ids, below):
    cos_t, sin_t, g_mat = tabs
    n = f"l{i}b_"
    mods = sv["mods"]
    g = {}
    du = d_f_swiglu(dof, w["w_ffn_out"][0], sv["u2"], n + "d_f")
    g["w_ffn_out"] = mm_tn(sv["f"], dof, out_dtype=BF16, name=n + "dw_ffn_out").reshape(NCHIP, FH // NCHIP, D)
    if pending is None:
        dh2 = mm_nt(du, w["w_ffn_in"], name=n + "d_h2")
    else:
        dh2, _, swapped = mm_nt(du, w["w_ffn_in"], name=n + "d_h2", comm=rs_swap(pending))
        sends, arrs = zip(*[rs_add(g_, s_, ids[0], ids[1], f"{n}rs_add_{k}")
                            for k, g_, s_ in zip(MATMUL_W, pending, swapped)])
    g["w_ffn_in"] = mm_tn(sv["h2"], du, cols=True, out_dtype=BF16, name=n + "dw_ffn_in")
    dx1, dm34, dmixed, dm2 = norm_resid_bwd(sv["x1"], mods, dh2, dx2, 4, sv["mixed"], mods, 2, n + "norm2_resid1")
    dya, dyb, dys, dp_gate = d_merged_gate(dmixed, w["w_mix_out"][0], sv["gates"], sv["ys"], n + "d_merged")
    g["w_mix_out"] = mm_tn(sv["merged"], dmixed, out_dtype=BF16, name=n + "dw_mix").reshape(NCHIP, D // NCHIP, D)
    do = mm_nt(dya, w["w_attn_o"], out_dtype=BF16, name=n + "d_o")
    g["w_attn_o"] = _cols_split(mm_tn(sv["o"], dya, out_dtype=BF16, name=n + "dw_attn_o"))
    dhs = mm_nt(dyb, w["w_conf_out"], name=n + "d_hs")
    g["w_conf_out"] = _cols_split(mm_tn(sv["hs"], dyb, out_dtype=BF16, name=n + "dw_conf_out"))
    dz = mm_nt(dys, w["w_sc_out"], name=n + "d_z")
    g["w_sc_out"] = _cols_split(mm_tn(sv["z"], dys, out_dtype=BF16, name=n + "dw_sc_out"))
    dyc, g["conf_ln_g"], g["conf_ln_b"] = ln_silu_bwd(sv["yc"], w["conf_ln_g"], w["conf_ln_b"], dhs, n + "ln_silu")
    da, dg, dbg, dcg, dxs, g["conf_dw_w"], g["conf_dw_b"], g["sc_dw_w"] = conv_bwd(
        sv["p_main"], w["conf_dw_w"], w["sc_dw_w"], dyc, dz, n + "conv")
    done = None
    if pending is None:
        dq, dk, dv = attn_bwd(sv["q"], sv["k"], sv["v"], sv["o"], do, n + "attn")
        dp_qkv, dgqk = qkv_bwd(sv["p_main"], cos_t, sin_t, g_mat, w["gq"], w["gk"], dq, dk, dv, n + "qkv")
    else:
        (dq, dk, dv), arrs, _ = attn_bwd(sv["q"], sv["k"], sv["v"], sv["o"], do, n + "attn", comm=rs_ici(sends, arrs))
        (dp_qkv, dgqk), done, _ = qkv_bwd(sv["p_main"], cos_t, sin_t, g_mat, w["gq"], w["gk"], dq, dk, dv, n + "qkv",
                                          comm=rs_d2d(arrs))
    dp_main = jnp.concatenate([dp_qkv, da, dg, dbg, dcg, dxs], axis=1)
    dh = mm_nt(dp_main, w["wi_main"], name=n + "d_h_main")
    dh = mm_nt(dp_gate, w["wi_gate"], acc=dh, name=n + "d_h_gate")
    g["w_in"] = _cols_split(jnp.concatenate([mm_tn(sv["h"], dp_main, out_dtype=BF16, name=n + "dw_in_main"),
                                             mm_tn(sv["h"], dp_gate, out_dtype=BF16, name=n + "dw_in_gate")], axis=1))
    if below is None:
        dx_in, dm01 = norm_mod_bwd(sv["x_in"], mods, dh, dx1, 1, n + "norm1")
        dof_below = dm5_below = None
    else:
        dx_in, dm01, dof_below, dm5_below = norm_resid_bwd(sv["x_in"], mods, dh, dx1, 1, below["of"], below["mods"], 5,
                                                           n + "norm1_resid2")
    parts = jnp.concatenate([dm01, dm2, dm34, dm5], axis=2).reshape(2 * BL, NMOD * D)
    dmod, g["b_ada"] = dmod_assemble(parts, n + "dmod")
    g["w_ada"] = mm_tn(cs, dmod, cols=True, out_dtype=BF16, name=n + "dw_ada")
    g["dcs"] = mm_nt(dmod, w["w_ada"], name=n + "d_cs")
    g["q_norm"] = dgqk[0, :AW].reshape(NQ, HD).sum(axis=0)
    g["k_norm"] = dgqk[0, OFF_K:OFF_K + KVW].reshape(NKV, HD).sum(axis=0)
    return dx_in, dof_below, dm5_below, g, done


def kernel(x, c, ctx, c_ctx, w_ada, b_ada, w_in, q_norm, k_norm, w_attn_o, conf_dw_w, conf_dw_b, conf_ln_g, conf_ln_b, w_conf_out, sc_dw_w, w_sc_out, w_mix_out, w_ffn_in, w_ffn_out, loss_target, m_c_ctx, m_w_ada, m_b_ada, m_w_in, m_q_norm, m_k_norm, m_w_attn_o, m_conf_dw_w, m_conf_dw_b, m_conf_ln_g, m_conf_ln_b, m_w_conf_out, m_sc_dw_w, m_w_sc_out, m_w_mix_out, m_w_ffn_in, m_w_ffn_out, v_c_ctx, v_w_ada, v_b_ada, v_w_in, v_q_norm, v_k_norm, v_w_attn_o, v_conf_dw_w, v_conf_dw_b, v_conf_ln_g, v_conf_ln_b, v_w_conf_out, v_sc_dw_w, v_w_sc_out, v_w_mix_out, v_w_ffn_in, v_w_ffn_out):
    local = dict(c_ctx=c_ctx, w_ada=w_ada, b_ada=b_ada, w_in=w_in, q_norm=q_norm, k_norm=k_norm, w_attn_o=w_attn_o,
                 conf_dw_w=conf_dw_w, conf_dw_b=conf_dw_b, conf_ln_g=conf_ln_g, conf_ln_b=conf_ln_b,
                 w_conf_out=w_conf_out, sc_dw_w=sc_dw_w, w_sc_out=w_sc_out, w_mix_out=w_mix_out, w_ffn_in=w_ffn_in,
                 w_ffn_out=w_ffn_out)
    mom_m = dict(c_ctx=m_c_ctx, w_ada=m_w_ada, b_ada=m_b_ada, w_in=m_w_in, q_norm=m_q_norm, k_norm=m_k_norm,
                 w_attn_o=m_w_attn_o, conf_dw_w=m_conf_dw_w, conf_dw_b=m_conf_dw_b, conf_ln_g=m_conf_ln_g,
                 conf_ln_b=m_conf_ln_b, w_conf_out=m_w_conf_out, sc_dw_w=m_sc_dw_w, w_sc_out=m_w_sc_out,
                 w_mix_out=m_w_mix_out, w_ffn_in=m_w_ffn_in, w_ffn_out=m_w_ffn_out)
    mom_v = dict(c_ctx=v_c_ctx, w_ada=v_w_ada, b_ada=v_b_ada, w_in=v_w_in, q_norm=v_q_norm, k_norm=v_k_norm,
                 w_attn_o=v_w_attn_o, conf_dw_w=v_conf_dw_w, conf_dw_b=v_conf_dw_b, conf_ln_g=v_conf_ln_g,
                 conf_ln_b=v_conf_ln_b, w_conf_out=v_w_conf_out, sc_dw_w=v_sc_dw_w, w_sc_out=v_w_sc_out,
                 w_mix_out=v_w_mix_out, w_ffn_in=v_w_ffn_in, w_ffn_out=v_w_ffn_out)
    order = ("c_ctx", "w_ada", "b_ada", "w_in", "q_norm", "k_norm", "w_attn_o", "conf_dw_w", "conf_dw_b", "conf_ln_g",
             "conf_ln_b", "w_conf_out", "sc_dw_w", "w_sc_out", "w_mix_out", "w_ffn_in", "w_ffn_out")
    core = lax.axis_index("c").astype(jnp.int32)
    chip = (2 * lax.axis_index("x") + lax.axis_index("y")).astype(jnp.int32)

    own = [cast_layers(local[k], chip.reshape(1), "cast_" + k) for k in MATMUL_W]
    layer_bufs = [[own[w][l] for w in range(len(MATMUL_W))] for l in range(DEPTH)]
    conv_shapes = [local[k].shape for k in CONV_W]
    conv_all = all_gather8(_pack_rows([local[k] for k in CONV_W], 8), "gather_conv_taps")
    per_chip = [_unpack(conv_all[2 * s], conv_shapes) for s in range(NCHIP)]
    small = dict(b_ada=b_ada, q_norm=q_norm, k_norm=k_norm, conf_dw_b=conf_dw_b, conf_ln_g=conf_ln_g, conf_ln_b=conf_ln_b)
    for i, k in enumerate(CONV_W):
        small[k] = jnp.concatenate([per_chip[s][i] for s in range(NCHIP)], axis=2)

    loss_local, grad_x, sums, small_g = local_step(x, c, ctx, c_ctx, layer_bufs, small, loss_target,
                                                   ids=(core.reshape(1), chip.reshape(1)))
    loss = lax.psum(loss_local, ("x", "y", "c"))

    small_shapes = [small_g[k].shape for k in SMALL]
    small_sum = sum_leading(all_gather8(_pack_rows([small_g[k] for k in SMALL], 8), "gather_small_grads"), "small_sum")
    small_g = dict(zip(SMALL, _unpack(small_sum, small_shapes)))
    for k in CONV_W:
        width = local[k].shape[2]
        small_g[k] = lax.dynamic_slice_in_dim(small_g[k], chip * width, width, axis=2)

    grad, delta, new_m, new_v = {}, {}, {}, {}
    sends0, arrs0 = sums[0]
    upper = {}
    for wi, k in enumerate(MATMUL_W):
        outs = [lax.empty(local[k].shape, F32) for _ in range(4)]
        upper[k], (arrs0[wi],), _ = adamw_layers(local[k], [sums[l][wi] for l in range(1, DEPTH)], mom_m[k], mom_v[k], 1,
                                                 outs, f"adamw_{k}_upper", comm=rs_ici([sends0[wi]], [arrs0[wi]]))
    arrs0, _ = comm_only("rs0_d2d", rs_d2d(arrs0))
    for wi, k in enumerate(MATMUL_W):
        grad[k], delta[k], new_m[k], new_v[k] = adamw_layers(local[k], [arrs0[wi]], mom_m[k], mom_v[k], 0, upper[k],
                                                             f"adamw_{k}_0")
    for k in order:
        if k in MATMUL_W:
            continue
        shp = local[k].shape
        view = (1, shp[0]) if len(shp) == 1 else (-1, shp[-1])
        d_, m_, v_ = adamw(local[k].reshape(view), small_g[k].reshape(view), mom_m[k].reshape(view),
                           mom_v[k].reshape(view), "adamw_" + k)
        grad[k], delta[k], new_m[k], new_v[k] = small_g[k], d_.reshape(shp), m_.reshape(shp), v_.reshape(shp)
    return (loss, grad_x, *[grad[k] for k in order], *[delta[k] for k in order], *[new_m[k] for k in order],
            *[new_v[k] for k in order])


def local_step(x, c, ctx, c_ctx, layer_bufs, small, loss_target, ids=None):
    tabs = _rope_tables() + (_group_matrix(),)
    distributed = ids is not None
    layer_bufs = list(layer_bufs)
    if distributed:
        layer_bufs[0], _ = comm_only("gather0_ici", gather_ici(layer_bufs[0]))
        layer_bufs[0], _ = comm_only("gather0_d2d", gather_d2d(layer_bufs[0]))

    cin = jnp.concatenate([c, c_ctx[None], jnp.zeros((8 - BL - 1, D), F32)], axis=0)
    cs = silu_rows(cin, "silu_c")
    xs = jnp.concatenate([ctx, x], axis=1).reshape(NROW, D)
    saved, layer_w = [], []

    def make_layer(i, bufs):
        w = _layer_weights(bufs, small, i)
        return w, mm_nn(cs, w["w_ada"], bias=small["b_ada"][i][None], name=f"l{i}_mod").reshape(8, 1, NMOD * D)

    w, mods = make_layer(0, layer_bufs[0])
    h = norm_mod_fwd(xs, mods, 0, 1, "l0_norm1")
    for i in range(DEPTH):
        last = i == DEPTH - 1

        def next_layer(bufs, i=i):
            if bufs is not None:
                layer_bufs[i + 1] = bufs
            return make_layer(i + 1, layer_bufs[i + 1])

        ahead = layer_bufs[i + 1] if distributed and not last else None
        xs, following, sv = _layer_fwd(i, xs, h, mods, w, tabs, ahead, None if last else next_layer)
        saved.append(sv)
        layer_w.append(w)
        if following is not None:
            h, w, mods = following
    dxs, loss_lanes = loss_fwd_bwd(xs, loss_target.reshape(BL * SEQ, D), "loss")
    loss_local = 0.5 * jnp.sum(loss_lanes) / D

    grads, sums = [None] * DEPTH, [None] * DEPTH
    pending = None
    dof, dm5 = gate_resid_bwd(dxs, saved[-1]["of"], saved[-1]["mods"], 5, "top_resid2")
    for i in reversed(range(DEPTH)):
        dxs, dof, dm5, grads[i], done = _layer_bwd(i, dxs, dof, dm5, saved[i], layer_w[i], tabs, cs, pending, ids,
                                                   saved[i - 1] if i > 0 else None)
        partial = [grads[i][k] for k in MATMUL_W]
        if distributed:
            if pending is not None:
                sums[i + 1] = done
            pending = partial
        else:
            sums[i] = partial
    if distributed:
        _, swapped = comm_only("rs0_swap", rs_swap(pending))
        sends, arrs = zip(*[rs_add(g_, s_, ids[0], ids[1], "rs0_add_" + k) for k, g_, s_ in zip(MATMUL_W, pending, swapped)])
        sums[0] = (list(sends), list(arrs))
    grad_x = dxs.reshape(BL, RE, D)[:, CTX:, :]
    dcin = silu_rows_bwd(cin, jnp.stack([grads[i]["dcs"] for i in range(DEPTH)]), "silu_c_bwd")

    def stack(key):
        return jnp.stack([grads[i][key] for i in range(DEPTH)])

    small_g = dict(c_ctx=dcin[BL], b_ada=stack("b_ada").reshape(DEPTH, NMOD * D), q_norm=stack("q_norm"),
                   k_norm=stack("k_norm"), conf_dw_b=stack("conf_dw_b").reshape(DEPTH, CW),
                   conf_ln_g=stack("conf_ln_g").reshape(DEPTH, CW), conf_ln_b=stack("conf_ln_b").reshape(DEPTH, CW),
                   conf_dw_w=stack("conf_dw_w"), sc_dw_w=stack("sc_dw_w"))
    return loss_local, grad_x, sums, small_g
```

```python
import functools
from typing import Any, Callable, NamedTuple, Sequence

import jax
import jax.numpy as jnp
from jax import lax
from jax.experimental import pallas as pl
from jax.experimental.pallas import tpu as pltpu

F32, BF16 = jnp.float32, jnp.bfloat16
HIGHEST = lax.Precision.HIGHEST

D = 1024
SEQ = 2048
CTX = 256
DEPTH = 4
BL = 4
GRID_W = 64
HD = 64
NQ = 8
NKV = 2
AW = NQ * HD
KVW = NKV * HD
CW = D // 2
CONF_K = 31
SC_K = 3
NMOD = 6
FH = -(-8 * D // (3 * 256)) * 256
EPS = 1e-6
ROPE_THETA = 10000.0
ATTN_SCALE = HD ** -0.5
OFF_K = AW
OFF_V = OFF_K + KVW
OFF_CONF = OFF_V + KVW
OFF_SC = OFF_CONF + 2 * CW
OFF_GATE = OFF_SC + 3 * CW
IN_W = OFF_GATE + 3 * D
QKVW = OFF_CONF
NCHIP = 4

ADAM_LR, ADAM_B1, ADAM_B2, ADAM_EPS, ADAM_WD, ADAM_STEP = 0.001, 0.9, 0.999, 1e-08, 0.01, 10

TM = CTX
RE = CTX + SEQ
TPE = RE // TM
NROW = BL * RE
NT = NROW // TM
LANE = 128
CB = CW // LANE
CONV_CH = 128
PADR = 16
VMEM_LIMIT = 52 * 1024 * 1024

MESH = pl.DeviceIdType.MESH
ANY = pl.BlockSpec(memory_space=pl.ANY)


class CommSpec(NamedTuple):
    ro: Sequence[Any]
    rw: Sequence[Any]
    new: Sequence[Any]
    nsem: int
    program: Callable


def _pcall(body, *, name, out_shape, grid=(), in_specs=None, out_specs=None, scratch=(), sem=None, comm=None,
           aliases=None):
    aliases = dict(aliases or {})
    if not grid:
        return pl.pallas_call(body, name=name, out_shape=out_shape)
    if comm is None:
        params = pltpu.CompilerParams(dimension_semantics=sem, vmem_limit_bytes=VMEM_LIMIT)
        return pl.pallas_call(body, name=name, out_shape=out_shape, grid=grid, in_specs=in_specs, out_specs=out_specs,
                              scratch_shapes=list(scratch), input_output_aliases=aliases, compiler_params=params)

    single = not isinstance(out_shape, (tuple, list))
    out_shapes = (out_shape,) if single else tuple(out_shape)
    out_specs_t = (out_specs,) if single else tuple(out_specs)
    n_in, n_out, n_scr = len(in_specs), len(out_shapes), len(scratch)
    n_ro, n_rw, n_new = len(comm.ro), len(comm.rw), len(comm.new)

    def carrier(*refs):
        ins = refs[:n_in]
        ro_refs = refs[n_in:n_in + n_ro]
        o0 = n_in + n_ro + n_rw
        outs = refs[o0:o0 + n_out]
        rw_refs = refs[o0 + n_out:o0 + n_out + n_rw]
        new_refs = refs[o0 + n_out + n_rw:o0 + n_out + n_rw + n_new]
        s0 = o0 + n_out + n_rw + n_new
        scr = refs[s0:s0 + n_scr]
        send_sems, recv_sems = refs[s0 + n_scr:]
        first = functools.reduce(jnp.logical_and, [pl.program_id(a) == 0 for a in range(len(grid))])
        last = functools.reduce(jnp.logical_and, [pl.program_id(a) == grid[a] - 1 for a in range(len(grid))])
        starts, arrivals = comm.program(ro_refs, rw_refs, new_refs, send_sems, recv_sems)

        @pl.when(first)
        def _():
            for cp in starts:
                cp.start()

        body(*ins, *outs, *scr)

        @pl.when(last)
        def _():
            for cp in arrivals:
                cp.wait_recv()
            for cp in starts:
                cp.wait_send()

    def call(*args):
        rw_shapes = tuple(jax.ShapeDtypeStruct(a.shape, a.dtype) for a in comm.rw)
        res = pl.pallas_call(
            carrier, name=name, out_shape=out_shapes + rw_shapes + tuple(comm.new), grid=grid,
            in_specs=list(in_specs) + [ANY] * (n_ro + n_rw),
            out_specs=out_specs_t + (ANY,) * (n_rw + n_new),
            scratch_shapes=list(scratch) + [pltpu.SemaphoreType.DMA((comm.nsem,)), pltpu.SemaphoreType.DMA((comm.nsem,))],
            input_output_aliases={**aliases, **{n_in + n_ro + i: n_out + i for i in range(n_rw)}},
            compiler_params=pltpu.CompilerParams(dimension_semantics=("arbitrary",) * len(grid),
                                                 vmem_limit_bytes=VMEM_LIMIT))(*args, *comm.ro, *comm.rw)
        compute = res[0] if single else tuple(res[:n_out])
        return compute, list(res[n_out:n_out + n_rw]), list(res[n_out + n_rw:])

    return call


def comm_only(name, comm):
    n_ro, n_rw, n_new = len(comm.ro), len(comm.rw), len(comm.new)

    def body(*refs):
        ro_refs = refs[:n_ro]
        rw_refs = refs[n_ro + n_rw:n_ro + 2 * n_rw]
        new_refs = refs[n_ro + 2 * n_rw:n_ro + 2 * n_rw + n_new]
        send_sems, recv_sems = refs[n_ro + 2 * n_rw + n_new:]
        starts, arrivals = comm.program(ro_refs, rw_refs, new_refs, send_sems, recv_sems)
        for cp in starts:
            cp.start()
        for cp in arrivals:
            cp.wait_recv()
        for cp in starts:
            cp.wait_send()

    rw_shapes = tuple(jax.ShapeDtypeStruct(a.shape, a.dtype) for a in comm.rw)
    res = pl.pallas_call(body, name=name, out_shape=rw_shapes + tuple(comm.new), in_specs=[ANY] * (n_ro + n_rw),
                         out_specs=(ANY,) * (n_rw + n_new), input_output_aliases={n_ro + i: i for i in range(n_rw)},
                         scratch_shapes=[pltpu.SemaphoreType.DMA((comm.nsem,)), pltpu.SemaphoreType.DMA((comm.nsem,))])(
                             *comm.ro, *comm.rw)
    return list(res[:n_rw]), list(res[n_rw:])


def _pick(n, cands):
    for t in cands:
        if n % t == 0:
            return t
    return n


def _seg(t):
    return jnp.where(t % TPE == 0, BL, t // TPE)


def _slot(t):
    return 2 * (t // TPE) + jnp.where(t % TPE == 0, 0, 1)


def _sigmoid(x):
    return 1.0 / (1.0 + jnp.exp(-x))


MM_BUDGET = 40 * 1024 * 1024
N_TILE_CAP = 1664


def _tile(n, cap=N_TILE_CAP):
    if n <= cap:
        return n
    for t in range(cap - cap % LANE, 0, -LANE):
        if n % t == 0:
            return t
    return n


def _row_tile(m, bytes_of):
    for tm in (1024, 512, 256, 128):
        if m % tm == 0 and bytes_of(tm) <= MM_BUDGET:
            return tm
    return m


def _w_dims(w):
    arr, kind = w
    if kind == "cols":
        return arr.shape[1], NCHIP * arr.shape[2]
    return arr.shape


def _sz(dtype):
    return jnp.dtype(dtype).itemsize


def mm_nn(a, w, *, bias=None, out_dtype=F32, name):
    arr, kind = w
    m, k = a.shape
    _, n = _w_dims(w)
    tn = _tile(arr.shape[2]) if kind == "cols" else _tile(n)
    tm = _row_tile(m, lambda t: 2 * (t * k * _sz(a.dtype) + k * tn * 2 + t * tn * _sz(out_dtype)))
    if kind == "mat":
        b_spec = pl.BlockSpec((k, tn), lambda j, i: (0, j))
    else:
        per = arr.shape[2] // tn
        b_spec = pl.BlockSpec((None, k, tn), lambda j, i: (j // per, 0, j % per))
    has_bias = bias is not None

    def body(*refs):
        out = jnp.dot(refs[0][...].astype(BF16), refs[1][...].astype(BF16), preferred_element_type=F32)
        if has_bias:
            out = out + refs[2][...]
        refs[-1][...] = out.astype(out_dtype)

    in_specs = [pl.BlockSpec((tm, k), lambda j, i: (i, 0)), b_spec]
    args = [a, arr]
    if has_bias:
        in_specs.append(pl.BlockSpec((1, tn), lambda j, i: (0, j)))
        args.append(bias)
    return _pcall(body, name=name, out_shape=jax.ShapeDtypeStruct((m, n), out_dtype), grid=(n // tn, m // tm),
                  in_specs=in_specs, out_specs=pl.BlockSpec((tm, tn), lambda j, i: (i, j)),
                  sem=("parallel", "parallel"))(*args)


def mm_nt(a, w, *, acc=None, out_dtype=F32, name, comm=None):
    arr, kind = w
    kdim, _ = _w_dims(w)
    has_acc = acc is not None
    tk = _tile(kdim, 1408)
    if kind == "cols":
        c = arr.shape[2]
        m = a.shape[-2]
        if a.ndim == 3:
            a_spec = lambda t: pl.BlockSpec((None, t, c), lambda j, i, s: (s // 2, i, s % 2))
        else:
            a_spec = lambda t: pl.BlockSpec((t, c), lambda j, i, s: (i, s))
        tm = _row_tile(m, lambda t: 2 * (t * c * _sz(a.dtype) + tk * c * 2 + t * tk * _sz(out_dtype)) + t * tk * 4)

        def body(a_ref, b_ref, o_ref, acc_ref):
            s = pl.program_id(2)

            @pl.when(s == 0)
            def _():
                acc_ref[...] = jnp.zeros_like(acc_ref)

            acc_ref[...] += lax.dot_general(a_ref[...].astype(BF16), b_ref[...], (((1,), (1,)), ((), ())),
                                            preferred_element_type=F32)

            @pl.when(s == NCHIP - 1)
            def _():
                o_ref[...] = acc_ref[...].astype(out_dtype)

        return _pcall(body, name=name, out_shape=jax.ShapeDtypeStruct((m, kdim), out_dtype),
                      grid=(kdim // tk, m // tm, NCHIP),
                      in_specs=[a_spec(tm), pl.BlockSpec((None, tk, c), lambda j, i, s: (s, j, 0))],
                      out_specs=pl.BlockSpec((tm, tk), lambda j, i, s: (i, j)),
                      scratch=[pltpu.VMEM((tm, tk), F32)],
                      sem=("parallel", "parallel", "arbitrary"), comm=comm)(a, arr)

    m, n = a.shape
    tm = _row_tile(m, lambda t: 2 * (t * n * _sz(a.dtype) + tk * n * 2 + t * tk * (_sz(out_dtype) + 4 * has_acc))
                   + tk * n * 2)
    b_spec = pl.BlockSpec((tk, n), lambda j, i: (j, 0))

    def body(*refs):
        wt_ref = refs[-1]

        @pl.when(pl.program_id(1) == 0)
        def _():
            wt_ref[...] = refs[1][...].astype(BF16).T

        out = jnp.dot(refs[0][...].astype(BF16), wt_ref[...], preferred_element_type=F32)
        if has_acc:
            out = out + refs[2][...]
        refs[-2][...] = out.astype(out_dtype)

    in_specs = [pl.BlockSpec((tm, n), lambda j, i: (i, 0)), b_spec]
    args = [a, arr]
    if has_acc:
        in_specs.append(pl.BlockSpec((tm, tk), lambda j, i: (i, j)))
        args.append(acc)
    return _pcall(body, name=name, out_shape=jax.ShapeDtypeStruct((m, kdim), out_dtype), grid=(kdim // tk, m // tm),
                  in_specs=in_specs, out_specs=pl.BlockSpec((tm, tk), lambda j, i: (i, j)),
                  scratch=[pltpu.VMEM((n, tk), BF16)], sem=("parallel", "arbitrary"))(*args)


def mm_tn(a, b, *, cols=False, out_dtype=F32, name):
    rows, k = a.shape
    halves = b.ndim == 3
    n = 2 * b.shape[2] if halves else b.shape[1]
    odt = out_dtype
    if cols:
        c = n // NCHIP
        tn, tk = _tile(c), k
        per = c // tn
        out_spec = pl.BlockSpec((None, tk, tn), lambda i, j, r: (j // per, 0, j % per))
        out_shape = jax.ShapeDtypeStruct((NCHIP, k, c), odt)
    else:
        tn, tk = _tile(n), _tile(k, 1408)
        out_spec = pl.BlockSpec((tk, tn), lambda i, j, r: (i, j))
        out_shape = jax.ShapeDtypeStruct((k, n), odt)
    tr = _row_tile(rows, lambda t: 2 * (t * tk * _sz(a.dtype) + t * tn * _sz(b.dtype) + tk * tn * _sz(odt)) + tk * tn * 4)
    nsteps = rows // tr

    def body(*refs):
        a_ref, b_ref = refs[0], refs[1]
        o_ref, acc_ref = refs[-2], refs[-1]
        r = pl.program_id(2)

        @pl.when(r == 0)
        def _():
            acc_ref[...] = jnp.zeros_like(acc_ref)

        acc_ref[...] += lax.dot_general(a_ref[...].astype(BF16), b_ref[...].astype(BF16), (((0,), (0,)), ((), ())),
                                        preferred_element_type=F32)

        @pl.when(r == nsteps - 1)
        def _():
            o_ref[...] = acc_ref[...].astype(odt)

    if halves:
        per_half = (n // 2) // tn
        b_spec = pl.BlockSpec((None, tr, tn), lambda i, j, r: (j // per_half, r, j % per_half))
    else:
        b_spec = pl.BlockSpec((tr, tn), lambda i, j, r: (r, j))
    return _pcall(body, name=name, out_shape=out_shape, grid=(k // tk, n // tn, nsteps),
                  in_specs=[pl.BlockSpec((tr, tk), lambda i, j, r: (r, i)), b_spec], out_specs=out_spec,
                  scratch=[pltpu.VMEM((tk, tn), F32)], sem=("parallel", "parallel", "arbitrary"))(a, b)


def ffn_in_swiglu(h, w_in, name, comm=None):
    m, k = h.shape
    c = w_in.shape[2]
    tm = 512

    def body(h_ref, wa_ref, wb_ref, f_ref, u_ref):
        hv = h_ref[...]
        a = jnp.dot(hv, wa_ref[...], preferred_element_type=F32)
        b = jnp.dot(hv, wb_ref[...], preferred_element_type=F32)
        f_ref[...] = (a * _sigmoid(a) * b).astype(BF16)
        u_ref[0] = a.astype(BF16)
        u_ref[1] = b.astype(BF16)

    return _pcall(body, name=name,
                  out_shape=(jax.ShapeDtypeStruct((m, FH), BF16), jax.ShapeDtypeStruct((2, m, FH), BF16)),
                  grid=(2, m // tm),
                  in_specs=[pl.BlockSpec((tm, k), lambda j, i: (i, 0)),
                            pl.BlockSpec((None, k, c), lambda j, i: (j, 0, 0)),
                            pl.BlockSpec((None, k, c), lambda j, i: (2 + j, 0, 0))],
                  out_specs=(pl.BlockSpec((tm, c), lambda j, i: (i, j)), pl.BlockSpec((2, tm, c), lambda j, i: (0, i, j))),
                  sem=("parallel", "parallel"), comm=comm)(h, w_in, w_in)


def d_f_swiglu(dof, w_out, u2, name):
    m, k = dof.shape
    c = FH // 2
    tm = 512

    def body(d_ref, w_ref, u_ref, du_ref, wt_ref):
        @pl.when(pl.program_id(1) == 0)
        def _():
            wt_ref[...] = w_ref[...].T

        df = jnp.dot(d_ref[...], wt_ref[...], preferred_element_type=F32)
        a, b = u_ref[0].astype(F32), u_ref[1].astype(F32)
        sg = _sigmoid(a)
        du_ref[0] = (df * b * (sg * (1.0 + a * (1.0 - sg)))).astype(BF16)
        du_ref[1] = (df * a * sg).astype(BF16)

    ublk = pl.BlockSpec((2, tm, c), lambda j, i: (0, i, j))
    return _pcall(body, name=name, out_shape=jax.ShapeDtypeStruct((2, m, FH), BF16), grid=(2, m // tm),
                  in_specs=[pl.BlockSpec((tm, k), lambda j, i: (i, 0)), pl.BlockSpec((c, k), lambda j, i: (j, 0)), ublk],
                  out_specs=ublk, scratch=[pltpu.VMEM((k, c), w_out.dtype)], sem=("parallel", "arbitrary"))(dof, w_out, u2)


GATE_TN = 512


def gate_mm_fwd(h, wi_gate, o, hs, z, wo, wc, ws, name):
    m, k = h.shape
    tm, tn = 512, min(GATE_TN, D)
    nj = D // tn

    def body(h_ref, g0_ref, g1_ref, g2_ref, o_ref, hs_ref, z_ref, wo_ref, wc_ref, ws_ref, m_ref, g_ref, y_ref):
        hv = h_ref[...]
        acc = jnp.zeros((tm, tn), F32)
        for g, (gw_ref, x_ref, w_ref) in enumerate(((g0_ref, o_ref, wo_ref), (g1_ref, hs_ref, wc_ref),
                                                    (g2_ref, z_ref, ws_ref))):
            gate = _sigmoid(jnp.dot(hv, gw_ref[...], preferred_element_type=F32))
            y = jnp.dot(x_ref[...], w_ref[...], preferred_element_type=F32)
            acc += gate * y
            g_ref[g] = gate.astype(BF16)
            y_ref[g] = y.astype(BF16)
        m_ref[...] = acc.astype(BF16)

    def gate_w(g):
        return pl.BlockSpec((k, tn), lambda j, i: (0, g * nj + j))

    def branch(width):
        return pl.BlockSpec((tm, width), lambda j, i: (i, 0))

    def branch_w(width):
        return pl.BlockSpec((width, tn), lambda j, i: (0, j))

    stacked = pl.BlockSpec((3, tm, tn), lambda j, i: (0, i, j))
    sds3 = jax.ShapeDtypeStruct((3, m, D), BF16)
    return _pcall(body, name=name, out_shape=(jax.ShapeDtypeStruct((m, D), BF16), sds3, sds3), grid=(nj, m // tm),
                  in_specs=[pl.BlockSpec((tm, k), lambda j, i: (i, 0)), gate_w(0), gate_w(1), gate_w(2),
                            branch(o.shape[1]), branch(hs.shape[1]), branch(z.shape[1]),
                            branch_w(wo.shape[0]), branch_w(wc.shape[0]), branch_w(ws.shape[0])],
                  out_specs=(pl.BlockSpec((tm, tn), lambda j, i: (i, j)), stacked, stacked),
                  sem=("parallel", "parallel"))(h, wi_gate, wi_gate, wi_gate, o, hs, z, wo, wc, ws)


def d_merged_gate(dmixed, w_mix, gates, ys, name):
    m, k = dmixed.shape
    tm = 256

    def body(d_ref, w_ref, g_ref, y_ref, da_ref, db_ref, ds_ref, dp_ref, wt_ref):
        @pl.when(pl.program_id(0) == 0)
        def _():
            wt_ref[...] = w_ref[...].T

        dm = jnp.dot(d_ref[...], wt_ref[...], preferred_element_type=F32)
        for g, dy_ref in enumerate((da_ref, db_ref, ds_ref)):
            gate = g_ref[g].astype(F32)
            dy_ref[...] = (dm * gate).astype(BF16)
            dp_ref[:, g * D:(g + 1) * D] = (dm * y_ref[g].astype(F32) * gate * (1.0 - gate)).astype(BF16)

    stacked = pl.BlockSpec((3, tm, D), lambda i: (0, i, 0))
    row = pl.BlockSpec((tm, D), lambda i: (i, 0))
    sds = jax.ShapeDtypeStruct((m, D), BF16)
    return _pcall(body, name=name, out_shape=(sds, sds, sds, jax.ShapeDtypeStruct((m, 3 * D), BF16)), grid=(m // tm,),
                  in_specs=[pl.BlockSpec((tm, k), lambda i: (i, 0)), pl.BlockSpec((D, k), lambda i: (0, 0)),
                            stacked, stacked],
                  out_specs=(row, row, row, pl.BlockSpec((tm, 3 * D), lambda i: (i, 0))),
                  scratch=[pltpu.VMEM((k, D), w_mix.dtype)], sem=("arbitrary",))(dmixed, w_mix, gates, ys)


def _mods_spec():
    return pl.BlockSpec((1, 1, NMOD * D), lambda t: (_seg(t), 0, 0))


def _rows(width):
    return pl.BlockSpec((TM, width), lambda t: (t, 0))


def norm_mod_fwd(x, mods, k_sh, k_sc, name):
    def body(x_ref, m_ref, h_ref):
        x = x_ref[...]
        r = lax.rsqrt(jnp.mean(x * x, axis=-1, keepdims=True) + EPS)
        sh = m_ref[0, :, k_sh * D:(k_sh + 1) * D]
        sc = m_ref[0, :, k_sc * D:(k_sc + 1) * D]
        h_ref[...] = (x * r * (1.0 + sc) + sh).astype(BF16)

    return _pcall(body, name=name, out_shape=jax.ShapeDtypeStruct((NROW, D), BF16), grid=(NT,),
                  in_specs=[_rows(D), _mods_spec()], out_specs=_rows(D), sem=("parallel",))(x, mods)


def _accumulate_slot(t, ref, part):
    first = (t % TPE) <= 1

    @pl.when(first)
    def _():
        ref[0] = part

    @pl.when(jnp.logical_not(first))
    def _():
        ref[0] += part


def norm_mod_bwd(x, mods, dh, dres, k_sc, name):
    def body(x_ref, m_ref, dh_ref, dres_ref, dx_ref, dp_ref):
        t = pl.program_id(0)
        x = x_ref[...]
        r = lax.rsqrt(jnp.mean(x * x, axis=-1, keepdims=True) + EPS)
        xn = x * r
        sc = m_ref[0, :, k_sc * D:(k_sc + 1) * D]
        dh = dh_ref[...]
        dxn = dh * (1.0 + sc)
        dx_ref[...] = r * (dxn - xn * jnp.mean(dxn * xn, axis=-1, keepdims=True)) + dres_ref[...]
        part = jnp.concatenate([jnp.sum(dh, axis=0, keepdims=True), jnp.sum(dh * xn, axis=0, keepdims=True)], axis=1)
        _accumulate_slot(t, dp_ref, part)

    return _pcall(body, name=name,
                  out_shape=(jax.ShapeDtypeStruct((NROW, D), F32), jax.ShapeDtypeStruct((2 * BL, 1, 2 * D), F32)),
                  grid=(NT,), in_specs=[_rows(D), _mods_spec(), _rows(D), _rows(D)],
                  out_specs=(_rows(D), pl.BlockSpec((1, 1, 2 * D), lambda t: (_slot(t), 0, 0))),
                  sem=("arbitrary",))(x, mods, dh, dres)


def resid_norm_fwd(x, y, mods_g, k_g, mods_n, k_sh, k_sc, name):
    def body(x_ref, y_ref, mg_ref, mn_ref, x1_ref, h_ref):
        x1 = x_ref[...] + mg_ref[0, :, k_g * D:(k_g + 1) * D] * y_ref[...]
        x1_ref[...] = x1
        r = lax.rsqrt(jnp.mean(x1 * x1, axis=-1, keepdims=True) + EPS)
        sh = mn_ref[0, :, k_sh * D:(k_sh + 1) * D]
        sc = mn_ref[0, :, k_sc * D:(k_sc + 1) * D]
        h_ref[...] = (x1 * r * (1.0 + sc) + sh).astype(BF16)

    return _pcall(body, name=name,
                  out_shape=(jax.ShapeDtypeStruct((NROW, D), F32), jax.ShapeDtypeStruct((NROW, D), BF16)), grid=(NT,),
                  in_specs=[_rows(D), _rows(D), _mods_spec(), _mods_spec()], out_specs=(_rows(D), _rows(D)),
                  sem=("parallel",))(x, y, mods_g, mods_n)


def norm_resid_bwd(x, mods_n, dh, dres, k_sc, y, mods_g, k_g, name):
    def body(x_ref, mn_ref, dh_ref, dres_ref, y_ref, mg_ref, dx_ref, dpn_ref, dy_ref, dpg_ref):
        t = pl.program_id(0)
        x = x_ref[...]
        r = lax.rsqrt(jnp.mean(x * x, axis=-1, keepdims=True) + EPS)
        xn = x * r
        sc = mn_ref[0, :, k_sc * D:(k_sc + 1) * D]
        dh = dh_ref[...]
        dxn = dh * (1.0 + sc)
        dx = r * (dxn - xn * jnp.mean(dxn * xn, axis=-1, keepdims=True)) + dres_ref[...]
        dx_ref[...] = dx
        dy_ref[...] = (dx * mg_ref[0, :, k_g * D:(k_g + 1) * D]).astype(BF16)
        part = jnp.concatenate([jnp.sum(dh, axis=0, keepdims=True), jnp.sum(dh * xn, axis=0, keepdims=True)], axis=1)
        _accumulate_slot(t, dpn_ref, part)
        _accumulate_slot(t, dpg_ref, jnp.sum(dx * y_ref[...], axis=0, keepdims=True))

    def slot(width):
        return pl.BlockSpec((1, 1, width), lambda t: (_slot(t), 0, 0))

    return _pcall(body, name=name,
                  out_shape=(jax.ShapeDtypeStruct((NROW, D), F32), jax.ShapeDtypeStruct((2 * BL, 1, 2 * D), F32),
                             jax.ShapeDtypeStruct((NROW, D), BF16), jax.ShapeDtypeStruct((2 * BL, 1, D), F32)),
                  grid=(NT,), in_specs=[_rows(D), _mods_spec(), _rows(D), _rows(D), _rows(D), _mods_spec()],
                  out_specs=(_rows(D), slot(2 * D), _rows(D), slot(D)), sem=("arbitrary",))(x, mods_n, dh, dres, y, mods_g)


def gate_resid_fwd(x, y, mods, k_g, name):
    def body(x_ref, y_ref, m_ref, o_ref):
        o_ref[...] = x_ref[...] + m_ref[0, :, k_g * D:(k_g + 1) * D] * y_ref[...]

    return _pcall(body, name=name, out_shape=jax.ShapeDtypeStruct((NROW, D), F32), grid=(NT,),
                  in_specs=[_rows(D), _rows(D), _mods_spec()], out_specs=_rows(D), sem=("parallel",))(x, y, mods)


def gate_resid_bwd(dx, y, mods, k_g, name):
    def body(dx_ref, y_ref, m_ref, dy_ref, dp_ref):
        t = pl.program_id(0)
        dx = dx_ref[...]
        dy_ref[...] = (dx * m_ref[0, :, k_g * D:(k_g + 1) * D]).astype(BF16)
        _accumulate_slot(t, dp_ref, jnp.sum(dx * y_ref[...], axis=0, keepdims=True))

    return _pcall(body, name=name,
                  out_shape=(jax.ShapeDtypeStruct((NROW, D), BF16), jax.ShapeDtypeStruct((2 * BL, 1, D), F32)),
                  grid=(NT,), in_specs=[_rows(D), _rows(D), _mods_spec()],
                  out_specs=(_rows(D), pl.BlockSpec((1, 1, D), lambda t: (_slot(t), 0, 0))),
                  sem=("arbitrary",))(dx, y, mods)


def _swap16(y, lo16):
    return jnp.where(lo16, pltpu.roll(y, LANE - 16, 1), pltpu.roll(y, 16, 1))


def _group_mean(v, g_mat):
    return jnp.dot(v, g_mat, precision=HIGHEST, preferred_element_type=F32)


def qkv_fwd(p_main, cos_t, sin_t, g_mat, gq, gk, name):
    def body(p_ref, cos_ref, sin_ref, g_ref, gq_ref, gk_ref, q_ref, k_ref, v_ref):
        cos, sin, g_mat_v = cos_ref[...], sin_ref[...], g_ref[...]
        lo16 = (lax.broadcasted_iota(jnp.int32, (TM, LANE), 1) % 32) < 16

        def block(xb, g):
            xb = xb.astype(F32)
            r = lax.rsqrt(_group_mean(xb * xb, g_mat_v) + EPS)
            y = xb * r * g
            return y * cos + _swap16(y, lo16) * sin

        for j in range(AW // LANE):
            q_ref[:, j * LANE:(j + 1) * LANE] = (block(p_ref[:, j * LANE:(j + 1) * LANE], gq_ref[...])
                                                 * ATTN_SCALE).astype(BF16)
        lo = lax.broadcasted_iota(jnp.int32, (TM, LANE), 1) < HD
        for src, dst_ref in ((block(p_ref[:, OFF_K:OFF_K + LANE], gk_ref[...]), k_ref), (p_ref[:, OFF_V:OFF_V + LANE].astype(F32), v_ref)):
            swapped = pltpu.roll(src, HD, 1)
            dst_ref[:, 0:LANE] = jnp.where(lo, src, swapped).astype(BF16)
            dst_ref[:, LANE:2 * LANE] = jnp.where(lo, swapped, src).astype(BF16)

    tab = pl.BlockSpec((TM, LANE), lambda t: (t % TPE, 0))
    small = pl.BlockSpec((1, LANE), lambda t: (0, 0))
    return _pcall(body, name=name,
                  out_shape=(jax.ShapeDtypeStruct((NROW, AW), BF16), jax.ShapeDtypeStruct((NROW, 2 * KVW), BF16),
                             jax.ShapeDtypeStruct((NROW, 2 * KVW), BF16)),
                  grid=(NT,),
                  in_specs=[_rows(QKVW), tab, tab, pl.BlockSpec((LANE, LANE), lambda t: (0, 0)), small, small],
                  out_specs=(_rows(AW), _rows(2 * KVW), _rows(2 * KVW)),
                  sem=("parallel",))(p_main, cos_t, sin_t, g_mat, gq, gk)


def qkv_bwd(p_main, cos_t, sin_t, g_mat, gq, gk, dq, dk, dv, name, comm=None):
    def body(p_ref, cos_ref, sin_ref, g_ref, gq_ref, gk_ref, dq_ref, dk_ref, dv_ref, dp_ref, dg_ref):
        t = pl.program_id(0)
        cos, sin, g_mat_v = cos_ref[...], sin_ref[...], g_ref[...]
        lo16 = (lax.broadcasted_iota(jnp.int32, (TM, LANE), 1) % 32) < 16

        def block(xb, g, dyr):
            xb = xb.astype(F32)
            r = lax.rsqrt(_group_mean(xb * xb, g_mat_v) + EPS)
            xn = xb * r
            dy = dyr * cos + _swap16(dyr * sin, lo16)
            dgl = jnp.sum(dy * xn, axis=0, keepdims=True)
            dxn = dy * g
            return r * (dxn - xn * _group_mean(dxn * xn, g_mat_v)), dgl

        parts = []
        for j in range(AW // LANE):
            sl = slice(j * LANE, (j + 1) * LANE)
            dxb, dgl = block(p_ref[:, sl], gq_ref[...], dq_ref[:, sl] * ATTN_SCALE)
            dp_ref[:, sl] = dxb.astype(BF16)
            parts.append(dgl)
        lo = lax.broadcasted_iota(jnp.int32, (TM, LANE), 1) < HD

        def fold(d_ref):
            d0, d1 = d_ref[:, 0:LANE], d_ref[:, LANE:2 * LANE]
            return jnp.where(lo, d0 + pltpu.roll(d0, HD, 1), d1 + pltpu.roll(d1, HD, 1))

        dxb, dgl = block(p_ref[:, OFF_K:OFF_K + LANE], gk_ref[...], fold(dk_ref))
        dp_ref[:, OFF_K:OFF_K + LANE] = dxb.astype(BF16)
        parts.append(dgl)
        parts.append(jnp.zeros((1, LANE), F32))
        dp_ref[:, OFF_V:OFF_V + LANE] = fold(dv_ref).astype(BF16)
        part = jnp.concatenate(parts, axis=1)

        @pl.when(t == 0)
        def _():
            dg_ref[...] = part

        @pl.when(t != 0)
        def _():
            dg_ref[...] += part

    tab = pl.BlockSpec((TM, LANE), lambda t: (t % TPE, 0))
    small = pl.BlockSpec((1, LANE), lambda t: (0, 0))
    return _pcall(body, name=name,
                  out_shape=(jax.ShapeDtypeStruct((NROW, QKVW), BF16), jax.ShapeDtypeStruct((1, QKVW), F32)),
                  grid=(NT,),
                  in_specs=[_rows(QKVW), tab, tab, pl.BlockSpec((LANE, LANE), lambda t: (0, 0)), small, small,
                            _rows(AW), _rows(2 * KVW), _rows(2 * KVW)],
                  out_specs=(_rows(QKVW), pl.BlockSpec((1, QKVW), lambda t: (0, 0))),
                  sem=("arbitrary",), comm=comm)(p_main, cos_t, sin_t, g_mat, gq, gk, dq, dk, dv)


def _layer_norm_parts(yc):
    mu = jnp.mean(yc, axis=-1, keepdims=True)
    xc = yc - mu
    rs = lax.rsqrt(jnp.mean(xc * xc, axis=-1, keepdims=True) + EPS)
    return xc * rs, rs


def ln_silu_fwd(yc, g, b, name):
    def body(y_ref, g_ref, b_ref, o_ref):
        nrm, _ = _layer_norm_parts(y_ref[...])
        ln = nrm * g_ref[...] + b_ref[...]
        o_ref[...] = (ln * _sigmoid(ln)).astype(BF16)

    vec = pl.BlockSpec((1, CW), lambda t: (0, 0))
    return _pcall(body, name=name, out_shape=jax.ShapeDtypeStruct((NROW, CW), BF16), grid=(NT,),
                  in_specs=[_rows(CW), vec, vec], out_specs=_rows(CW), sem=("parallel",))(yc, g, b)


def ln_silu_bwd(yc, g, b, dhs, name, comm=None):
    def body(y_ref, g_ref, b_ref, dh_ref, dy_ref, dg_ref, db_ref):
        t = pl.program_id(0)
        nrm, rs = _layer_norm_parts(y_ref[...])
        ln = nrm * g_ref[...] + b_ref[...]
        sg = _sigmoid(ln)
        dln = dh_ref[...] * (sg * (1.0 + ln * (1.0 - sg)))
        dn = dln * g_ref[...]
        dy_ref[...] = rs * (dn - jnp.mean(dn, axis=-1, keepdims=True)
                            - nrm * jnp.mean(dn * nrm, axis=-1, keepdims=True))
        pg = jnp.sum(dln * nrm, axis=0, keepdims=True)
        pb = jnp.sum(dln, axis=0, keepdims=True)

        @pl.when(t == 0)
        def _():
            dg_ref[...] = pg
            db_ref[...] = pb

        @pl.when(t != 0)
        def _():
            dg_ref[...] += pg
            db_ref[...] += pb

    vec = pl.BlockSpec((1, CW), lambda t: (0, 0))
    return _pcall(body, name=name,
                  out_shape=(jax.ShapeDtypeStruct((NROW, CW), F32), jax.ShapeDtypeStruct((1, CW), F32),
                             jax.ShapeDtypeStruct((1, CW), F32)),
                  grid=(NT,), in_specs=[_rows(CW), vec, vec, _rows(CW)], out_specs=(_rows(CW), vec, vec),
                  sem=("arbitrary",), comm=comm)(yc, g, b, dhs)


def loss_fwd_bwd(y, target, name):
    def body(y_ref, t_ref, dy_ref, l_ref):
        t = pl.program_id(0)
        latent = (t % TPE) != 0
        err = jnp.where(latent, y_ref[...] - t_ref[...], 0.0)
        dy_ref[...] = err * (1.0 / D)
        part = jnp.sum(err * err, axis=0, keepdims=True)

        @pl.when(t == 0)
        def _():
            l_ref[...] = part

        @pl.when(t != 0)
        def _():
            l_ref[...] += part

    tgt = pl.BlockSpec((TM, D), lambda t: ((t // TPE) * (TPE - 1) + jnp.maximum(t % TPE - 1, 0), 0))
    return _pcall(body, name=name,
                  out_shape=(jax.ShapeDtypeStruct((NROW, D), F32), jax.ShapeDtypeStruct((1, D), F32)),
                  grid=(NT,), in_specs=[_rows(D), tgt], out_specs=(_rows(D), pl.BlockSpec((1, D), lambda t: (0, 0))),
                  sem=("arbitrary",))(y, target)


QB_PER_KV = AW // LANE // NKV


def _softmax_parts(qm, k):
    s = lax.dot_general(qm, k, (((1,), (1,)), ((), ())), preferred_element_type=F32)
    e = jnp.exp(s - jnp.max(s, axis=-1, keepdims=True))
    return e, 1.0 / jnp.sum(e, axis=-1, keepdims=True)


def _lane_halves():
    lo = lax.broadcasted_iota(jnp.int32, (TM, LANE), 1) < HD
    return lo, jnp.logical_not(lo)


def _stack_heads(x, halves):
    zero = jnp.zeros_like(x)
    return jnp.concatenate([jnp.where(halves[0], x, zero), jnp.where(halves[1], x, zero)], axis=0)


def attn_fwd(q, k, v, name, comm=None):
    def body(q_ref, k_ref, v_ref, o_ref):
        t = pl.program_id(2)
        halves = _lane_halves()

        def run(nk):
            kv, vv = k_ref[0:nk, :], v_ref[0:nk, :]
            for j in range(QB_PER_KV):
                lanes = slice(j * LANE, (j + 1) * LANE)
                e, rinv = _softmax_parts(_stack_heads(q_ref[:, lanes], halves), kv)
                out = jnp.dot(e.astype(BF16), vv, preferred_element_type=F32) * rinv
                o_ref[:, lanes] = jnp.where(halves[0], out[0:TM], out[TM:2 * TM]).astype(BF16)

        @pl.when(t == 0)
        def _():
            run(CTX)

        @pl.when(t != 0)
        def _():
            run(RE)

    qs = pl.BlockSpec((TM, QB_PER_KV * LANE), lambda b, h, t: (b * TPE + t, h))
    ks = pl.BlockSpec((RE, LANE), lambda b, h, t: (b, h))
    return _pcall(body, name=name, out_shape=jax.ShapeDtypeStruct((NROW, AW), BF16), grid=(BL, NKV, TPE),
                  in_specs=[qs, ks, ks], out_specs=qs, sem=("parallel",) * 3, comm=comm)(q, k, v)


def attn_bwd(q, k, v, o, do, name, comm=None):
    def body(q_ref, k_ref, v_ref, o_ref, do_ref, dq_ref, dk_ref, dv_ref):
        t = pl.program_id(2)
        halves = _lane_halves()

        @pl.when(t == 0)
        def _():
            dk_ref[...] = jnp.zeros_like(dk_ref)
            dv_ref[...] = jnp.zeros_like(dv_ref)

        def run(nk):
            kv, vv = k_ref[0:nk, :], v_ref[0:nk, :]
            dks, dvs = [], []
            for j in range(QB_PER_KV):
                lanes = slice(j * LANE, (j + 1) * LANE)
                q2, do2 = _stack_heads(q_ref[:, lanes], halves), _stack_heads(do_ref[:, lanes], halves)
                ov = o_ref[:, lanes].astype(F32)
                delta = jnp.sum(do2.astype(F32) * jnp.concatenate([ov, ov], axis=0), axis=-1, keepdims=True)
                e, rinv = _softmax_parts(q2, kv)
                p = e * rinv
                dvs.append(lax.dot_general(p.astype(BF16), do2, (((0,), (0,)), ((), ())), preferred_element_type=F32))
                dp = lax.dot_general(do2, vv, (((1,), (1,)), ((), ())), preferred_element_type=F32)
                ds = (p * (dp - delta)).astype(BF16)
                dq = jnp.dot(ds, kv, preferred_element_type=F32)
                dks.append(lax.dot_general(ds, q2, (((0,), (0,)), ((), ())), preferred_element_type=F32))
                dq_ref[:, lanes] = jnp.where(halves[0], dq[0:TM], dq[TM:2 * TM])
            dv_ref[0:nk, :] += functools.reduce(jnp.add, dvs)
            dk_ref[0:nk, :] += functools.reduce(jnp.add, dks)

        @pl.when(t == 0)
        def _():
            run(CTX)

        @pl.when(t != 0)
        def _():
            run(RE)

    qs = pl.BlockSpec((TM, QB_PER_KV * LANE), lambda b, h, t: (b * TPE + t, h))
    ks = pl.BlockSpec((RE, LANE), lambda b, h, t: (b, h))
    return _pcall(body, name=name,
                  out_shape=(jax.ShapeDtypeStruct((NROW, AW), F32), jax.ShapeDtypeStruct((NROW, 2 * KVW), F32),
                             jax.ShapeDtypeStruct((NROW, 2 * KVW), F32)),
                  grid=(BL, NKV, TPE), in_specs=[qs, ks, ks, qs, qs], out_specs=(qs, ks, ks),
                  sem=("parallel", "parallel", "arbitrary"), comm=comm)(q, k, v, o, do)


CONV_SEGS = ((0, CTX), (CTX, SEQ))


def _p_block(col0):
    return pl.BlockSpec((RE, LANE), lambda cb, b: (b, col0 // LANE + cb))


def _conv_io(width):
    return pl.BlockSpec((RE, LANE), lambda cb, b: (b, cb))


def _taps(n):
    return pl.BlockSpec((n, LANE), lambda cb, b: (0, cb))


def _fill_pad(pad_ref, length, values):
    pad_ref[0:PADR, :] = jnp.zeros((PADR, LANE), F32)
    pad_ref[PADR + length:2 * PADR + length, :] = jnp.zeros((PADR, LANE), F32)
    pad_ref[PADR:PADR + length, :] = values


def _conv_chunk(pad_ref, w_ref, ntap, c0, first_row):
    acc = jnp.zeros((CONV_CH, LANE), F32)
    for kk in range(ntap):
        r0 = c0 + first_row(kk)
        acc += w_ref[kk:kk + 1, :] * pad_ref[r0:r0 + CONV_CH, :]
    return acc


def conv_fwd(p_main, wdw, bdw, w3, name, comm=None):
    def body(a_ref, g_ref, bg_ref, cg_ref, xs_ref, w_ref, b_ref, w3_ref, yc_ref, z_ref, pad_ref):
        for off, length in CONV_SEGS:
            rows = slice(off, off + length)
            _fill_pad(pad_ref, length, a_ref[rows, :].astype(F32) * _sigmoid(g_ref[rows, :].astype(F32)))
            for c0 in range(0, length, CONV_CH):
                acc = _conv_chunk(pad_ref, w_ref, CONF_K, c0, lambda kk: PADR + kk - CONF_K // 2)
                yc_ref[off + c0:off + c0 + CONV_CH, :] = acc + b_ref[...]
            pad_ref[PADR:PADR + length, :] = cg_ref[rows, :].astype(F32) * xs_ref[rows, :].astype(F32)
            for c0 in range(0, length, CONV_CH):
                acc = _conv_chunk(pad_ref, w3_ref, SC_K, c0, lambda kk: PADR + kk - SC_K // 2)
                z_ref[off + c0:off + c0 + CONV_CH, :] = (bg_ref[off + c0:off + c0 + CONV_CH, :] * acc).astype(BF16)

    return _pcall(body, name=name,
                  out_shape=(jax.ShapeDtypeStruct((NROW, CW), F32), jax.ShapeDtypeStruct((NROW, CW), BF16)),
                  grid=(CB, BL),
                  in_specs=[_p_block(OFF_CONF), _p_block(OFF_CONF + CW), _p_block(OFF_SC), _p_block(OFF_SC + CW),
                            _p_block(OFF_SC + 2 * CW), _taps(CONF_K), _taps(1), _taps(SC_K)],
                  out_specs=(_conv_io(CW), _conv_io(CW)),
                  scratch=[pltpu.VMEM((SEQ + 2 * PADR, LANE), F32)],
                  sem=("parallel", "parallel"), comm=comm)(p_main, p_main, p_main, p_main, p_main, wdw, bdw, w3)


def _tap_grad(pad_ref, d_ref, off, length, first_row):
    acc = jnp.zeros((8, LANE), F32)
    for c0 in range(0, length, CONV_CH):
        prod = d_ref[off + c0:off + c0 + CONV_CH, :] * pad_ref[c0 + first_row:c0 + first_row + CONV_CH, :]
        acc += jnp.sum(prod.reshape(CONV_CH // 8, 8, LANE), axis=0)
    return jnp.sum(acc, axis=0, keepdims=True)


def conv_bwd(p_main, wdw, w3, dyc, dz, name):
    def body(a_ref, g_ref, bg_ref, cg_ref, xs_ref, w_ref, w3_ref, dyc_ref, dz_ref,
             da_ref, dg_ref, dbg_ref, dcg_ref, dxs_ref, dw_ref, db_ref, dw3_ref, pad_x, pad_d, dconv_ref):
        b = pl.program_id(1)

        @pl.when(b == 0)
        def _():
            dw_ref[...] = jnp.zeros_like(dw_ref)
            db_ref[...] = jnp.zeros_like(db_ref)
            dw3_ref[...] = jnp.zeros_like(dw3_ref)

        db_ref[...] += jnp.sum(dyc_ref[...], axis=0, keepdims=True)
        for off, length in CONV_SEGS:
            rows = slice(off, off + length)
            _fill_pad(pad_x, length, a_ref[rows, :].astype(F32) * _sigmoid(g_ref[rows, :].astype(F32)))
            _fill_pad(pad_d, length, dyc_ref[rows, :])
            for kk in range(CONF_K):
                dw_ref[kk:kk + 1, :] += _tap_grad(pad_x, dyc_ref, off, length, PADR + kk - CONF_K // 2)
            for c0 in range(0, length, CONV_CH):
                dh = _conv_chunk(pad_d, w_ref, CONF_K, c0, lambda kk: PADR + CONF_K // 2 - kk)
                ch = slice(off + c0, off + c0 + CONV_CH)
                sg = _sigmoid(g_ref[ch, :].astype(F32))
                da_ref[ch, :] = (dh * sg).astype(BF16)
                dg_ref[ch, :] = (dh * a_ref[ch, :] * sg * (1.0 - sg)).astype(BF16)
            pad_x[PADR:PADR + length, :] = cg_ref[rows, :].astype(F32) * xs_ref[rows, :].astype(F32)
            dconv_ref[rows, :] = dz_ref[rows, :] * bg_ref[rows, :]
            pad_d[PADR:PADR + length, :] = dconv_ref[rows, :]
            for kk in range(SC_K):
                dw3_ref[kk:kk + 1, :] += _tap_grad(pad_x, dconv_ref, off, length, PADR + kk - SC_K // 2)
            for c0 in range(0, length, CONV_CH):
                ch = slice(off + c0, off + c0 + CONV_CH)
                c3 = _conv_chunk(pad_x, w3_ref, SC_K, c0, lambda kk: PADR + kk - SC_K // 2)
                dbg_ref[ch, :] = (dz_ref[ch, :] * c3).astype(BF16)
                dcx = _conv_chunk(pad_d, w3_ref, SC_K, c0, lambda kk: PADR + SC_K // 2 - kk)
                dcg_ref[ch, :] = (dcx * xs_ref[ch, :]).astype(BF16)
                dxs_ref[ch, :] = (dcx * cg_ref[ch, :]).astype(BF16)

    slab = jax.ShapeDtypeStruct((NROW, CW), BF16)
    return _pcall(body, name=name,
                  out_shape=(slab,) * 5 + (jax.ShapeDtypeStruct((CONF_K, CW), F32), jax.ShapeDtypeStruct((1, CW), F32),
                                           jax.ShapeDtypeStruct((SC_K, CW), F32)),
                  grid=(CB, BL),
                  in_specs=[_p_block(OFF_CONF), _p_block(OFF_CONF + CW), _p_block(OFF_SC), _p_block(OFF_SC + CW),
                            _p_block(OFF_SC + 2 * CW), _taps(CONF_K), _taps(SC_K), _conv_io(CW), _conv_io(CW)],
                  out_specs=(_conv_io(CW),) * 5 + (_taps(CONF_K), _taps(1), _taps(SC_K)),
                  scratch=[pltpu.VMEM((SEQ + 2 * PADR, LANE), F32), pltpu.VMEM((SEQ + 2 * PADR, LANE), F32),
                           pltpu.VMEM((RE, LANE), F32)],
                  sem=("parallel", "arbitrary"))(p_main, p_main, p_main, p_main, p_main, wdw, w3, dyc, dz)


def silu_rows(x, name):
    def body(x_ref, o_ref):
        o_ref[...] = x_ref[...] * _sigmoid(x_ref[...])

    return _pcall(body, name=name, out_shape=jax.ShapeDtypeStruct(x.shape, F32))(x)


def silu_rows_bwd(x, dcs, name):
    def body(x_ref, d_ref, o_ref):
        x = x_ref[...]
        sg = _sigmoid(x)
        tot = d_ref[0]
        for i in range(1, DEPTH):
            tot += d_ref[i]
        o_ref[...] = tot * (sg * (1.0 + x * (1.0 - sg)))

    return _pcall(body, name=name, out_shape=jax.ShapeDtypeStruct(x.shape, F32))(x, dcs)


def dmod_assemble(parts, name):
    def body(p_ref, dm_ref, db_ref):
        row = lax.broadcasted_iota(jnp.int32, (8, NMOD * D), 0)
        dm = jnp.zeros((8, NMOD * D), F32)
        db = jnp.zeros((1, NMOD * D), F32)
        for s in range(2 * BL):
            target = BL if s % 2 == 0 else s // 2
            part = p_ref[s:s + 1, :]
            dm += jnp.where(row == target, part, 0.0)
            db += part
        dm_ref[...] = dm
        db_ref[...] = db

    return _pcall(body, name=name, out_shape=(jax.ShapeDtypeStruct((8, NMOD * D), F32),
                                              jax.ShapeDtypeStruct((1, NMOD * D), F32)))(parts)


def sum_leading(x, name):
    n = x.shape[0]
    tr = _pick(x.shape[1], (256, 32, 8))

    def body(x_ref, o_ref):
        tot = x_ref[0].astype(F32)
        for i in range(1, n):
            tot += x_ref[i].astype(F32)
        o_ref[...] = tot

    return _pcall(body, name=name, out_shape=jax.ShapeDtypeStruct(x.shape[1:], F32), grid=(x.shape[1] // tr,),
                  in_specs=[pl.BlockSpec((n, tr, x.shape[2]), lambda i: (0, i, 0))],
                  out_specs=pl.BlockSpec((tr, x.shape[2]), lambda i: (i, 0)), sem=("parallel",))(x)


SLAB_ROWS = (256, 176, 128, 64, 8)


def _prefetch_call(body, name, out_shape, grid, in_specs, out_specs, sem, scalars, *args):
    spec = pltpu.PrefetchScalarGridSpec(num_scalar_prefetch=len(scalars), grid=grid, in_specs=in_specs,
                                        out_specs=out_specs)
    return pl.pallas_call(body, name=name, out_shape=out_shape, grid_spec=spec,
                          compiler_params=pltpu.CompilerParams(dimension_semantics=sem,
                                                               vmem_limit_bytes=VMEM_LIMIT))(*scalars, *args)


def cast_layers(w, chip, name):
    depth, r, c = w.shape
    tr = _pick(r, SLAB_ROWS)

    def body(s_ref, w_ref, *o_refs):
        for l in range(depth):
            o_refs[l][...] = w_ref[l].astype(BF16)

    slab = pl.BlockSpec((None, tr, c), lambda i, s: (s[0], i, 0))
    return _prefetch_call(body, name, (jax.ShapeDtypeStruct((NCHIP, r, c), BF16),) * depth, (r // tr,),
                          [pl.BlockSpec((depth, tr, c), lambda i, s: (0, i, 0))], (slab,) * depth,
                          ("parallel",), (chip,), w)


def rs_add(g, other, core, chip, name):
    _, r, c = g.shape
    rh = r // 2
    tr = _pick(rh, SLAB_ROWS)
    nblk = rh // tr

    def body(core_ref, chip_ref, g_ref, o_ref, send_ref, arr_ref):
        k = pl.program_id(1)
        tot = (g_ref[...].astype(F32) + o_ref[...].astype(F32)).astype(BF16)
        send_ref[...] = tot

        @pl.when(k == chip_ref[0])
        def _():
            arr_ref[...] = tot

    blk = (None, tr, c)
    return _prefetch_call(
        body, name, (jax.ShapeDtypeStruct(other.shape, BF16), jax.ShapeDtypeStruct(g.shape, BF16)), (nblk, NCHIP),
        [pl.BlockSpec(blk, lambda i, k, cr, ch: (k, cr[0] * nblk + i, 0)), pl.BlockSpec(blk, lambda i, k, cr, ch: (k, i, 0))],
        (pl.BlockSpec(blk, lambda i, k, cr, ch: (k, i, 0)),
         pl.BlockSpec(blk, lambda i, k, cr, ch: (ch[0], cr[0] * nblk + i, 0))),
        ("parallel", "arbitrary"), (core, chip), g, other)


def adamw_layers(w, arrs, m, v, first, prev, name, comm=None):
    depth, r, c = w.shape
    tr = _pick(r, (128, 176, 64, 8))
    nblk = r // tr
    nl = len(arrs)
    c1 = 1.0 / (1.0 - ADAM_B1 ** ADAM_STEP)
    c2 = 1.0 / (1.0 - ADAM_B2 ** ADAM_STEP)

    def body(w_ref, m_ref, v_ref, *rest):
        a_refs = rest[:nl]
        g_ref, d_ref, mo_ref, vo_ref = rest[nl + 4:nl + 8]
        li = pl.program_id(0)
        gv = None
        for idx, a_ref in enumerate(a_refs):
            tot = a_ref[0].astype(F32)
            for k in range(1, NCHIP):
                tot += a_ref[k].astype(F32)
            gv = tot if gv is None else jnp.where(li == idx, tot, gv)
        mn = ADAM_B1 * m_ref[...] + (1.0 - ADAM_B1) * gv
        vn = ADAM_B2 * v_ref[...] + (1.0 - ADAM_B2) * (gv * gv)
        g_ref[...] = gv
        d_ref[...] = -ADAM_LR * ((mn * c1) / (jnp.sqrt(vn * c2) + ADAM_EPS) + ADAM_WD * w_ref[...])
        mo_ref[...] = mn
        vo_ref[...] = vn

    def arr_spec(idx):
        return pl.BlockSpec((NCHIP, tr, c),
                            lambda li, i: (0, jnp.where(li == idx, i, jnp.where(li < idx, 0, nblk - 1)), 0))

    spec = pl.BlockSpec((None, tr, c), lambda li, i: (first + li, i, 0))
    sds = jax.ShapeDtypeStruct(w.shape, F32)
    return _pcall(body, name=name, out_shape=(sds,) * 4, grid=(nl, nblk),
                  in_specs=[spec, spec, spec] + [arr_spec(idx) for idx in range(nl)] + [ANY] * 4,
                  out_specs=(spec,) * 4, aliases={3 + nl + i: i for i in range(4)},
                  sem=("arbitrary", "arbitrary"), comm=comm)(w, m, v, *arrs, *prev)


def adamw(w, g, m, v, name):
    rows, cols = w.shape
    tr = _pick(rows, (256, 248, 128, 8))
    c1 = 1.0 / (1.0 - ADAM_B1 ** ADAM_STEP)
    c2 = 1.0 / (1.0 - ADAM_B2 ** ADAM_STEP)

    def body(w_ref, g_ref, m_ref, v_ref, d_ref, mo_ref, vo_ref):
        gv = g_ref[...]
        mn = ADAM_B1 * m_ref[...] + (1.0 - ADAM_B1) * gv
        vn = ADAM_B2 * v_ref[...] + (1.0 - ADAM_B2) * (gv * gv)
        d_ref[...] = -ADAM_LR * ((mn * c1) / (jnp.sqrt(vn * c2) + ADAM_EPS) + ADAM_WD * w_ref[...])
        mo_ref[...] = mn
        vo_ref[...] = vn

    spec = pl.BlockSpec((tr, cols), lambda i: (i, 0))
    sds = jax.ShapeDtypeStruct((rows, cols), F32)
    return _pcall(body, name=name, out_shape=(sds, sds, sds), grid=(rows // tr,), in_specs=[spec] * 4,
                  out_specs=(spec, spec, spec), sem=("parallel",))(w, g, m, v)


def _place():
    return lax.axis_index("x"), lax.axis_index("y"), lax.axis_index("c")


def _other_chips(x, y):
    return [(1 - x, y), (x, 1 - y), (1 - x, 1 - y)]


def _comm_call(body, name, out_shape, n_in, nsem):
    return pl.pallas_call(body, name=name, out_shape=out_shape, in_specs=[ANY] * n_in,
                          out_specs=jax.tree.map(lambda _: ANY, out_shape),
                          scratch_shapes=[pltpu.SemaphoreType.DMA((nsem,)), pltpu.SemaphoreType.DMA((nsem,)),
                                          pltpu.SemaphoreType.DMA])


def all_gather8(block, name):
    def body(x_ref, out_ref, send_sems, recv_sems, local_sem):
        x, y, c = _place()
        me, sibling = (x, y, c), (x, y, 1 - c)
        chips = _other_chips(x, y)

        def slot(px, py, pc):
            return out_ref.at[4 * px + 2 * py + pc]

        def copy(k, blk, to, src=None):
            return pltpu.make_async_remote_copy(src_ref=slot(*blk) if src is None else src, dst_ref=slot(*blk),
                                                send_sem=send_sems.at[k], recv_sem=recv_sems.at[k],
                                                device_id=to, device_id_type=MESH)

        mine = pltpu.make_async_copy(x_ref, slot(*me), local_sem)
        mine.start()
        first = [copy(0, me, sibling, src=x_ref)]
        first += [copy(1 + j, me, (*chip, c), src=x_ref) for j, chip in enumerate(chips)]
        for cp in first:
            cp.start()
        passed = [copy(4 + j, (*chip, c), sibling) for j, chip in enumerate(chips)]
        for j, chip in enumerate(chips):
            copy(1 + j, (*chip, c), me).wait_recv()
            passed[j].start()
        copy(0, sibling, me).wait_recv()
        for j, chip in enumerate(chips):
            copy(4 + j, (*chip, 1 - c), me).wait_recv()
        for cp in first + passed:
            cp.wait_send()
        mine.wait()

    return _comm_call(body, name, jax.ShapeDtypeStruct((8,) + block.shape, block.dtype), 1, 7)(block)


def _remote(src, dst, send_sems, recv_sems, k, to):
    return pltpu.make_async_remote_copy(src_ref=src, dst_ref=dst, send_sem=send_sems.at[k], recv_sem=recv_sems.at[k],
                                        device_id=to, device_id_type=MESH)


def _half(ref, slot, core):
    rh = ref.shape[1] // 2
    return ref.at[slot, pl.ds(core * rh, rh)]


def _all_slots_half(ref, core):
    rh = ref.shape[1] // 2
    return ref.at[:, pl.ds(core * rh, rh)]


def gather_ici(bufs):
    def program(ro, rw, new, ss, rs):
        x, y, c = _place()
        own = 2 * x + y
        starts, arrivals = [], []
        for w, ref in enumerate(rw):
            for j, chip in enumerate(_other_chips(x, y)):
                starts.append(_remote(_half(ref, own, c), _half(ref, own, c), ss, rs, 3 * w + j, (*chip, c)))
                arrivals.append(_remote(_half(ref, own, c), _half(ref, 2 * chip[0] + chip[1], c), ss, rs, 3 * w + j,
                                        (*chip, c)))
        return starts, arrivals

    return CommSpec((), tuple(bufs), (), 3 * len(bufs), program)


def gather_d2d(bufs):
    def program(ro, rw, new, ss, rs):
        x, y, c = _place()
        starts, arrivals = [], []
        for w, ref in enumerate(rw):
            for j, chip in enumerate(_other_chips(x, y)):
                slot = 2 * chip[0] + chip[1]
                starts.append(_remote(_half(ref, slot, c), _half(ref, slot, c), ss, rs, 3 * w + j, (x, y, 1 - c)))
                arrivals.append(_remote(_half(ref, slot, c), _half(ref, slot, 1 - c), ss, rs, 3 * w + j, (x, y, 1 - c)))
        return starts, arrivals

    return CommSpec((), tuple(bufs), (), 3 * len(bufs), program)


def rs_swap(grads):
    def program(ro, rw, new, ss, rs):
        x, y, c = _place()
        copies = [_remote(_all_slots_half(g, 1 - c), new[w], ss, rs, w, (x, y, 1 - c)) for w, g in enumerate(ro)]
        return copies, copies

    shapes = tuple(jax.ShapeDtypeStruct((NCHIP, g.shape[1] // 2, g.shape[2]), g.dtype) for g in grads)
    return CommSpec(tuple(grads), (), shapes, len(grads), program)


def rs_ici(sends, arrs):
    def program(ro, rw, new, ss, rs):
        x, y, c = _place()
        own = 2 * x + y
        starts, arrivals = [], []
        for w, (snd, arr) in enumerate(zip(ro, rw)):
            for j, chip in enumerate(_other_chips(x, y)):
                slot = 2 * chip[0] + chip[1]
                starts.append(_remote(snd.at[slot], _half(arr, own, c), ss, rs, 3 * w + j, (*chip, c)))
                arrivals.append(_remote(snd.at[slot], _half(arr, slot, c), ss, rs, 3 * w + j, (*chip, c)))
        return starts, arrivals

    return CommSpec(tuple(sends), tuple(arrs), (), 3 * len(sends), program)


def rs_d2d(arrs):
    def program(ro, rw, new, ss, rs):
        x, y, c = _place()
        starts = [_remote(_all_slots_half(a, c), _all_slots_half(a, c), ss, rs, w, (x, y, 1 - c)) for w, a in enumerate(rw)]
        arrivals = [_remote(_all_slots_half(a, c), _all_slots_half(a, 1 - c), ss, rs, w, (x, y, 1 - c))
                    for w, a in enumerate(rw)]
        return starts, arrivals

    return CommSpec((), tuple(arrs), (), len(arrs), program)


PACK_COLS = 1024
MATMUL_W = ("w_ada", "w_in", "w_attn_o", "w_conf_out", "w_sc_out", "w_mix_out", "w_ffn_in", "w_ffn_out")
ROW_SPLIT = ("w_mix_out", "w_ffn_out")
CONV_W = ("conf_dw_w", "sc_dw_w")
SMALL = ("c_ctx", "b_ada", "q_norm", "k_norm", "conf_dw_b", "conf_ln_g", "conf_ln_b", "conf_dw_w", "sc_dw_w")


def _pack_rows(arrays, row_multiple):
    flat = jnp.concatenate([a.reshape(-1) for a in arrays])
    rows = -(-flat.shape[0] // PACK_COLS)
    rows = -(-rows // row_multiple) * row_multiple
    flat = jnp.pad(flat, (0, rows * PACK_COLS - flat.shape[0]))
    return flat.reshape(rows, PACK_COLS)


def _unpack(flat2d, shapes):
    flat = flat2d.reshape(-1)
    out, pos = [], 0
    for shp in shapes:
        n = 1
        for s in shp:
            n *= s
        out.append(flat[pos:pos + n].reshape(shp))
        pos += n
    return out


def _cols_joined(stacked_layer):
    nchip, r, c = stacked_layer.shape
    return jnp.transpose(stacked_layer, (1, 0, 2)).reshape(r, nchip * c)


def _cols_split(full):
    r, cols = full.shape
    return jnp.transpose(full.reshape(r, NCHIP, cols // NCHIP), (1, 0, 2))


def _rope_tables():
    rows = SEQ // GRID_W
    r_ids = jnp.repeat(jnp.arange(rows, dtype=F32), GRID_W)
    c_ids = jnp.tile(jnp.arange(GRID_W, dtype=F32), rows)
    freqs = ROPE_THETA ** (-jnp.arange(0, HD // 2, 2, dtype=F32) / (HD // 2))
    ang_r, ang_c = r_ids[:, None] * freqs, c_ids[:, None] * freqs
    cos_h = jnp.concatenate([jnp.cos(ang_r), jnp.cos(ang_r), jnp.cos(ang_c), jnp.cos(ang_c)], axis=1)
    sin_h = jnp.concatenate([-jnp.sin(ang_r), jnp.sin(ang_r), -jnp.sin(ang_c), jnp.sin(ang_c)], axis=1)
    cos_t = jnp.concatenate([jnp.ones((CTX, HD), F32), cos_h], axis=0)
    sin_t = jnp.concatenate([jnp.zeros((CTX, HD), F32), sin_h], axis=0)
    return jnp.tile(cos_t, (1, LANE // HD)), jnp.tile(sin_t, (1, LANE // HD))


def _group_matrix():
    gid = jnp.arange(LANE) // HD
    return jnp.where(gid[:, None] == gid[None, :], 1.0 / HD, 0.0).astype(F32)


N_FIRST = 2
assert MATMUL_W[:N_FIRST] == ("w_ada", "w_in")


def _first_weights(bufs, small, i):
    wi = _cols_joined(bufs[1])
    return dict(
        w_ada=(bufs[0], "cols"), wi_main=(wi[:, :OFF_GATE], "mat"), wi_gate=(wi[:, OFF_GATE:], "mat"),
        conf_dw_w=small["conf_dw_w"][i], sc_dw_w=small["sc_dw_w"][i], conf_dw_b=small["conf_dw_b"][i][None],
        conf_ln_g=small["conf_ln_g"][i][None], conf_ln_b=small["conf_ln_b"][i][None],
        gq=jnp.tile(small["q_norm"][i], LANE // HD)[None], gk=jnp.tile(small["k_norm"][i], LANE // HD)[None])


def _second_weights(bufs):
    b = dict(zip(MATMUL_W[N_FIRST:], bufs))

    def rows_joined(a):
        return a.reshape(a.shape[0] * a.shape[1], a.shape[2])

    return dict(
        w_attn_o=(_cols_joined(b["w_attn_o"]), "mat"), w_conf_out=(_cols_joined(b["w_conf_out"]), "mat"),
        w_sc_out=(_cols_joined(b["w_sc_out"]), "mat"), w_ffn_in=(b["w_ffn_in"], "cols"),
        w_mix_out=(rows_joined(b["w_mix_out"]), "mat"), w_ffn_out=(rows_joined(b["w_ffn_out"]), "mat"))


def _layer_fwd(i, xs, h, mods, w, tabs, second_bufs, next_first, next_layer, distributed):
    cos_t, sin_t, g_mat = tabs
    n = f"l{i}_"
    w = dict(w)
    sv = {"x_in": xs, "mods": mods, "h": h, "w": w}
    n_second = len(second_bufs)
    sv["p_main"] = mm_nn(sv["h"], w["wi_main"], name=n + "p_main")
    sv["q"], sv["k"], sv["v"] = qkv_fwd(sv["p_main"], cos_t, sin_t, g_mat, w["gq"], w["gk"], n + "qkv")
    if distributed:
        riding = list(second_bufs) + list(next_first or [])
        sv["o"], riding, _ = attn_fwd(sv["q"], sv["k"], sv["v"], n + "attn", comm=gather_ici(riding))
        second_bufs, next_first = riding[:n_second], (riding[n_second:] or None)
        (sv["yc"], sv["z"]), second_bufs, _ = conv_fwd(sv["p_main"], w["conf_dw_w"], w["conf_dw_b"], w["sc_dw_w"],
                                                       n + "conv", comm=gather_d2d(second_bufs))
    else:
        sv["o"] = attn_fwd(sv["q"], sv["k"], sv["v"], n + "attn")
        sv["yc"], sv["z"] = conv_fwd(sv["p_main"], w["conf_dw_w"], w["conf_dw_b"], w["sc_dw_w"], n + "conv")
    w.update(_second_weights(second_bufs))
    next_bufs = next_first if distributed else None
    sv["hs"] = ln_silu_fwd(sv["yc"], w["conf_ln_g"], w["conf_ln_b"], n + "ln_silu")
    sv["merged"], sv["gates"], sv["ys"] = gate_mm_fwd(sv["h"], w["wi_gate"][0], sv["o"], sv["hs"], sv["z"],
                                                      w["w_attn_o"][0], w["w_conf_out"][0], w["w_sc_out"][0],
                                                      n + "gate_merge")
    sv["mixed"] = mm_nn(sv["merged"], w["w_mix_out"], name=n + "mix")
    sv["x1"], sv["h2"] = resid_norm_fwd(xs, sv["mixed"], mods, 2, mods, 3, 4, n + "resid1_norm2")
    if next_bufs is None:
        sv["f"], sv["u2"] = ffn_in_swiglu(sv["h2"], w["w_ffn_in"][0], n + "ffn_in")
    else:
        (sv["f"], sv["u2"]), next_bufs, _ = ffn_in_swiglu(sv["h2"], w["w_ffn_in"][0], n + "ffn_in",
                                                          comm=gather_d2d(next_bufs))
    sv["of"] = mm_nn(sv["f"], w["w_ffn_out"], name=n + "ffn_out")
    if next_layer is None:
        return gate_resid_fwd(sv["x1"], sv["of"], mods, 5, n + "resid2"), None, sv
    w_next, mods_next = next_layer(next_bufs)
    x2, h_next = resid_norm_fwd(sv["x1"], sv["of"], mods, 5, mods_next, 0, 1, n + "resid2_norm1")
    return x2, (h_next, w_next, mods_next), sv


def _layer_bwd(i, dx2, dof, dm5, sv, tabs, cs, pending, ids, below):
    cos_t, sin_t, g_mat = tabs
    n = f"l{i}b_"
    mods, w = sv["mods"], sv["w"]
    g = {}
    sends, arrs = [], []
    du = d_f_swiglu(dof, w["w_ffn_out"][0], sv["u2"], n + "d_f")
    g["w_ffn_out"] = mm_tn(sv["f"], dof, out_dtype=BF16, name=n + "dw_ffn_out").reshape(NCHIP, FH // NCHIP, D)
    if pending is None:
        dh2 = mm_nt(du, w["w_ffn_in"], name=n + "d_h2")
    else:
        dh2, _, swapped = mm_nt(du, w["w_ffn_in"], name=n + "d_h2", comm=rs_swap(pending))
        for k, g_, s_ in zip(MATMUL_W[:N_FIRST], pending, swapped):
            send, arr = rs_add(g_, s_, ids[0], ids[1], f"{n}rs_add_above_{k}")
            sends.append(send)
            arrs.append(arr)
    g["w_ffn_in"] = mm_tn(sv["h2"], du, cols=True, out_dtype=BF16, name=n + "dw_ffn_in")
    dx1, dm34, dmixed, dm2 = norm_resid_bwd(sv["x1"], mods, dh2, dx2, 4, sv["mixed"], mods, 2, n + "norm2_resid1")
    dya, dyb, dys, dp_gate = d_merged_gate(dmixed, w["w_mix_out"][0], sv["gates"], sv["ys"], n + "d_merged")
    g["w_mix_out"] = mm_tn(sv["merged"], dmixed, out_dtype=BF16, name=n + "dw_mix").reshape(NCHIP, D // NCHIP, D)
    do = mm_nt(dya, w["w_attn_o"], out_dtype=BF16, name=n + "d_o")
    g["w_attn_o"] = _cols_split(mm_tn(sv["o"], dya, out_dtype=BF16, name=n + "dw_attn_o"))
    dhs = mm_nt(dyb, w["w_conf_out"], name=n + "d_hs")
    g["w_conf_out"] = _cols_split(mm_tn(sv["hs"], dyb, out_dtype=BF16, name=n + "dw_conf_out"))
    dz = mm_nt(dys, w["w_sc_out"], name=n + "d_z")
    g["w_sc_out"] = _cols_split(mm_tn(sv["z"], dys, out_dtype=BF16, name=n + "dw_sc_out"))
    done = None
    if ids is None:
        dyc, g["conf_ln_g"], g["conf_ln_b"] = ln_silu_bwd(sv["yc"], w["conf_ln_g"], w["conf_ln_b"], dhs, n + "ln_silu")
    else:
        own = [g[k] for k in MATMUL_W[N_FIRST:]]
        (dyc, g["conf_ln_g"], g["conf_ln_b"]), _, swapped = ln_silu_bwd(sv["yc"], w["conf_ln_g"], w["conf_ln_b"], dhs,
                                                                        n + "ln_silu", comm=rs_swap(own))
        for k, g_, s_ in zip(MATMUL_W[N_FIRST:], own, swapped):
            send, arr = rs_add(g_, s_, ids[0], ids[1], f"{n}rs_add_{k}")
            sends.append(send)
            arrs.append(arr)
    da, dg, dbg, dcg, dxs, g["conf_dw_w"], g["conf_dw_b"], g["sc_dw_w"] = conv_bwd(
        sv["p_main"], w["conf_dw_w"], w["sc_dw_w"], dyc, dz, n + "conv")
    if ids is None:
        dq, dk, dv = attn_bwd(sv["q"], sv["k"], sv["v"], sv["o"], do, n + "attn")
        dp_qkv, dgqk = qkv_bwd(sv["p_main"], cos_t, sin_t, g_mat, w["gq"], w["gk"], dq, dk, dv, n + "qkv")
    else:
        (dq, dk, dv), arrs, _ = attn_bwd(sv["q"], sv["k"], sv["v"], sv["o"], do, n + "attn", comm=rs_ici(sends, arrs))
        (dp_qkv, dgqk), done, _ = qkv_bwd(sv["p_main"], cos_t, sin_t, g_mat, w["gq"], w["gk"], dq, dk, dv, n + "qkv",
                                          comm=rs_d2d(arrs))
    dp_main = jnp.concatenate([dp_qkv, da, dg, dbg, dcg, dxs], axis=1)
    dh = mm_nt(dp_main, w["wi_main"], name=n + "d_h_main")
    dh = mm_nt(dp_gate, w["wi_gate"], acc=dh, name=n + "d_h_gate")
    g["w_in"] = _cols_split(jnp.concatenate([mm_tn(sv["h"], dp_main, out_dtype=BF16, name=n + "dw_in_main"),
                                             mm_tn(sv["h"], dp_gate, out_dtype=BF16, name=n + "dw_in_gate")], axis=1))
    if below is None:
        dx_in, dm01 = norm_mod_bwd(sv["x_in"], mods, dh, dx1, 1, n + "norm1")
        dof_below = dm5_below = None
    else:
        dx_in, dm01, dof_below, dm5_below = norm_resid_bwd(sv["x_in"], mods, dh, dx1, 1, below["of"], below["mods"], 5,
                                                           n + "norm1_resid2")
    parts = jnp.concatenate([dm01, dm2, dm34, dm5], axis=2).reshape(2 * BL, NMOD * D)
    dmod, g["b_ada"] = dmod_assemble(parts, n + "dmod")
    g["w_ada"] = mm_tn(cs, dmod, cols=True, out_dtype=BF16, name=n + "dw_ada")
    g["dcs"] = mm_nt(dmod, w["w_ada"], name=n + "d_cs")
    g["q_norm"] = dgqk[0, :AW].reshape(NQ, HD).sum(axis=0)
    g["k_norm"] = dgqk[0, OFF_K:OFF_K + KVW].reshape(NKV, HD).sum(axis=0)
    return dx_in, dof_below, dm5_below, g, done


def kernel(x, c, ctx, c_ctx, w_ada, b_ada, w_in, q_norm, k_norm, w_attn_o, conf_dw_w, conf_dw_b, conf_ln_g, conf_ln_b, w_conf_out, sc_dw_w, w_sc_out, w_mix_out, w_ffn_in, w_ffn_out, loss_target, m_c_ctx, m_w_ada, m_b_ada, m_w_in, m_q_norm, m_k_norm, m_w_attn_o, m_conf_dw_w, m_conf_dw_b, m_conf_ln_g, m_conf_ln_b, m_w_conf_out, m_sc_dw_w, m_w_sc_out, m_w_mix_out, m_w_ffn_in, m_w_ffn_out, v_c_ctx, v_w_ada, v_b_ada, v_w_in, v_q_norm, v_k_norm, v_w_attn_o, v_conf_dw_w, v_conf_dw_b, v_conf_ln_g, v_conf_ln_b, v_w_conf_out, v_sc_dw_w, v_w_sc_out, v_w_mix_out, v_w_ffn_in, v_w_ffn_out):
    local = dict(c_ctx=c_ctx, w_ada=w_ada, b_ada=b_ada, w_in=w_in, q_norm=q_norm, k_norm=k_norm, w_attn_o=w_attn_o,
                 conf_dw_w=conf_dw_w, conf_dw_b=conf_dw_b, conf_ln_g=conf_ln_g, conf_ln_b=conf_ln_b,
                 w_conf_out=w_conf_out, sc_dw_w=sc_dw_w, w_sc_out=w_sc_out, w_mix_out=w_mix_out, w_ffn_in=w_ffn_in,
                 w_ffn_out=w_ffn_out)
    mom_m = dict(c_ctx=m_c_ctx, w_ada=m_w_ada, b_ada=m_b_ada, w_in=m_w_in, q_norm=m_q_norm, k_norm=m_k_norm,
                 w_attn_o=m_w_attn_o, conf_dw_w=m_conf_dw_w, conf_dw_b=m_conf_dw_b, conf_ln_g=m_conf_ln_g,
                 conf_ln_b=m_conf_ln_b, w_conf_out=m_w_conf_out, sc_dw_w=m_sc_dw_w, w_sc_out=m_w_sc_out,
                 w_mix_out=m_w_mix_out, w_ffn_in=m_w_ffn_in, w_ffn_out=m_w_ffn_out)
    mom_v = dict(c_ctx=v_c_ctx, w_ada=v_w_ada, b_ada=v_b_ada, w_in=v_w_in, q_norm=v_q_norm, k_norm=v_k_norm,
                 w_attn_o=v_w_attn_o, conf_dw_w=v_conf_dw_w, conf_dw_b=v_conf_dw_b, conf_ln_g=v_conf_ln_g,
                 conf_ln_b=v_conf_ln_b, w_conf_out=v_w_conf_out, sc_dw_w=v_sc_dw_w, w_sc_out=v_w_sc_out,
                 w_mix_out=v_w_mix_out, w_ffn_in=v_w_ffn_in, w_ffn_out=v_w_ffn_out)
    order = ("c_ctx", "w_ada", "b_ada", "w_in", "q_norm", "k_norm", "w_attn_o", "conf_dw_w", "conf_dw_b", "conf_ln_g",
             "conf_ln_b", "w_conf_out", "sc_dw_w", "w_sc_out", "w_mix_out", "w_ffn_in", "w_ffn_out")
    core = lax.axis_index("c").astype(jnp.int32)
    chip = (2 * lax.axis_index("x") + lax.axis_index("y")).astype(jnp.int32)

    own = [cast_layers(local[k], chip.reshape(1), "cast_" + k) for k in MATMUL_W]
    layer_bufs = [[own[w][l] for w in range(len(MATMUL_W))] for l in range(DEPTH)]
    conv_shapes = [local[k].shape for k in CONV_W]
    conv_all = all_gather8(_pack_rows([local[k] for k in CONV_W], 8), "gather_conv_taps")
    per_chip = [_unpack(conv_all[2 * s], conv_shapes) for s in range(NCHIP)]
    small = dict(b_ada=b_ada, q_norm=q_norm, k_norm=k_norm, conf_dw_b=conf_dw_b, conf_ln_g=conf_ln_g, conf_ln_b=conf_ln_b)
    for i, k in enumerate(CONV_W):
        small[k] = jnp.concatenate([per_chip[s][i] for s in range(NCHIP)], axis=2)

    loss_local, grad_x, sums, small_g = local_step(x, c, ctx, c_ctx, layer_bufs, small, loss_target,
                                                   ids=(core.reshape(1), chip.reshape(1)))
    loss = lax.psum(loss_local, ("x", "y", "c"))

    small_shapes = [small_g[k].shape for k in SMALL]
    small_sum = sum_leading(all_gather8(_pack_rows([small_g[k] for k in SMALL], 8), "gather_small_grads"), "small_sum")
    small_g = dict(zip(SMALL, _unpack(small_sum, small_shapes)))
    for k in CONV_W:
        width = local[k].shape[2]
        small_g[k] = lax.dynamic_slice_in_dim(small_g[k], chip * width, width, axis=2)

    grad, delta, new_m, new_v = {}, {}, {}, {}
    for wi, k in enumerate(MATMUL_W):
        outs = [lax.empty(local[k].shape, F32) for _ in range(4)]
        grad[k], delta[k], new_m[k], new_v[k] = adamw_layers(local[k], [sums[l][wi] for l in range(DEPTH)], mom_m[k],
                                                             mom_v[k], 0, outs, "adamw_" + k)
    for k in order:
        if k in MATMUL_W:
            continue
        shp = local[k].shape
        view = (1, shp[0]) if len(shp) == 1 else (-1, shp[-1])
        d_, m_, v_ = adamw(local[k].reshape(view), small_g[k].reshape(view), mom_m[k].reshape(view),
                           mom_v[k].reshape(view), "adamw_" + k)
        grad[k], delta[k], new_m[k], new_v[k] = small_g[k], d_.reshape(shp), m_.reshape(shp), v_.reshape(shp)
    return (loss, grad_x, *[grad[k] for k in order], *[delta[k] for k in order], *[new_m[k] for k in order],
            *[new_v[k] for k in order])


def local_step(x, c, ctx, c_ctx, layer_bufs, small, loss_target, ids=None):
    tabs = _rope_tables() + (_group_matrix(),)
    distributed = ids is not None
    first_bufs = [list(b[:N_FIRST]) for b in layer_bufs]
    second_bufs = [list(b[N_FIRST:]) for b in layer_bufs]
    if distributed:
        first_bufs[0], _ = comm_only("gather0_ici", gather_ici(first_bufs[0]))
        first_bufs[0], _ = comm_only("gather0_d2d", gather_d2d(first_bufs[0]))

    cin = jnp.concatenate([c, c_ctx[None], jnp.zeros((8 - BL - 1, D), F32)], axis=0)
    cs = silu_rows(cin, "silu_c")
    xs = jnp.concatenate([ctx, x], axis=1).reshape(NROW, D)
    saved = []

    def make_layer(i, bufs):
        w = _first_weights(bufs, small, i)
        return w, mm_nn(cs, w["w_ada"], bias=small["b_ada"][i][None], name=f"l{i}_mod").reshape(8, 1, NMOD * D)

    w, mods = make_layer(0, first_bufs[0])
    h = norm_mod_fwd(xs, mods, 0, 1, "l0_norm1")
    for i in range(DEPTH):
        last = i == DEPTH - 1

        def next_layer(bufs, i=i):
            return make_layer(i + 1, first_bufs[i + 1] if bufs is None else bufs)

        xs, following, sv = _layer_fwd(i, xs, h, mods, w, tabs, second_bufs[i], None if last else first_bufs[i + 1],
                                       None if last else next_layer, distributed)
        saved.append(sv)
        if following is not None:
            h, w, mods = following
    dxs, loss_lanes = loss_fwd_bwd(xs, loss_target.reshape(BL * SEQ, D), "loss")
    loss_local = 0.5 * jnp.sum(loss_lanes) / D

    grads = [None] * DEPTH
    sums = [[None] * len(MATMUL_W) for _ in range(DEPTH)]
    pending = None
    dof, dm5 = gate_resid_bwd(dxs, saved[-1]["of"], saved[-1]["mods"], 5, "top_resid2")
    for i in reversed(range(DEPTH)):
        dxs, dof, dm5, grads[i], done = _layer_bwd(i, dxs, dof, dm5, saved[i], tabs, cs, pending, ids,
                                                   saved[i - 1] if i > 0 else None)
        partial = [grads[i][k] for k in MATMUL_W]
        if distributed:
            if pending is not None:
                sums[i + 1][:N_FIRST] = done[:N_FIRST]
                done = done[N_FIRST:]
            sums[i][N_FIRST:] = done
            pending = partial[:N_FIRST]
        else:
            sums[i] = partial
    if distributed:
        names = MATMUL_W[:N_FIRST]
        _, swapped = comm_only("rs0_swap", rs_swap(pending))
        sends, arrs = zip(*[rs_add(g_, s_, ids[0], ids[1], "rs0_add_" + k) for k, g_, s_ in zip(names, pending, swapped)])
        arrs, _ = comm_only("rs0_ici", rs_ici(sends, arrs))
        sums[0][:N_FIRST], _ = comm_only("rs0_d2d", rs_d2d(arrs))
    grad_x = dxs.reshape(BL, RE, D)[:, CTX:, :]
    dcin = silu_rows_bwd(cin, jnp.stack([grads[i]["dcs"] for i in range(DEPTH)]), "silu_c_bwd")

    def stack(key):
        return jnp.stack([grads[i][key] for i in range(DEPTH)])

    small_g = dict(c_ctx=dcin[BL], b_ada=stack("b_ada").reshape(DEPTH, NMOD * D), q_norm=stack("q_norm"),
                   k_norm=stack("k_norm"), conf_dw_b=stack("conf_dw_b").reshape(DEPTH, CW),
                   conf_ln_g=stack("conf_ln_g").reshape(DEPTH, CW), conf_ln_b=stack("conf_ln_b").reshape(DEPTH, CW),
                   conf_dw_w=stack("conf_dw_w"), sc_dw_w=stack("sc_dw_w"))
    return loss_local, grad_x, sums, small_g
```

```python
import functools
from typing import Any, Callable, NamedTuple, Sequence

import jax
import jax.numpy as jnp
from jax import lax
from jax.experimental import pallas as pl
from jax.experimental.pallas import tpu as pltpu

F32, BF16 = jnp.float32, jnp.bfloat16
HIGHEST = lax.Precision.HIGHEST

D = 1024
SEQ = 2048
CTX = 256
DEPTH = 4
BL = 4
GRID_W = 64
HD = 64
NQ = 8
NKV = 2
AW = NQ * HD
KVW = NKV * HD
CW = D // 2
CONF_K = 31
SC_K = 3
NMOD = 6
FH = -(-8 * D // (3 * 256)) * 256
EPS = 1e-6
ROPE_THETA = 10000.0
ATTN_SCALE = HD ** -0.5
OFF_K = AW
OFF_V = OFF_K + KVW
OFF_CONF = OFF_V + KVW
OFF_SC = OFF_CONF + 2 * CW
OFF_GATE = OFF_SC + 3 * CW
IN_W = OFF_GATE + 3 * D
QKVW = OFF_CONF
NCHIP = 4

ADAM_LR, ADAM_B1, ADAM_B2, ADAM_EPS, ADAM_WD, ADAM_STEP = 0.001, 0.9, 0.999, 1e-08, 0.01, 10

TM = CTX
RE = CTX + SEQ
TPE = RE // TM
NROW = BL * RE
NT = NROW // TM
LANE = 128
CB = CW // LANE
CONV_CH = 128
PADR = 16
VMEM_LIMIT = 52 * 1024 * 1024

MESH = pl.DeviceIdType.MESH
ANY = pl.BlockSpec(memory_space=pl.ANY)


class CommSpec(NamedTuple):
    ro: Sequence[Any]
    rw: Sequence[Any]
    new: Sequence[Any]
    nsem: int
    program: Callable


def _pcall(body, *, name, out_shape, grid=(), in_specs=None, out_specs=None, scratch=(), sem=None, comm=None,
           aliases=None):
    aliases = dict(aliases or {})
    if not grid:
        return pl.pallas_call(body, name=name, out_shape=out_shape)
    if comm is None:
        params = pltpu.CompilerParams(dimension_semantics=sem, vmem_limit_bytes=VMEM_LIMIT)
        return pl.pallas_call(body, name=name, out_shape=out_shape, grid=grid, in_specs=in_specs, out_specs=out_specs,
                              scratch_shapes=list(scratch), input_output_aliases=aliases, compiler_params=params)

    single = not isinstance(out_shape, (tuple, list))
    out_shapes = (out_shape,) if single else tuple(out_shape)
    out_specs_t = (out_specs,) if single else tuple(out_specs)
    n_in, n_out, n_scr = len(in_specs), len(out_shapes), len(scratch)
    n_ro, n_rw, n_new = len(comm.ro), len(comm.rw), len(comm.new)

    def carrier(*refs):
        ins = refs[:n_in]
        ro_refs = refs[n_in:n_in + n_ro]
        o0 = n_in + n_ro + n_rw
        outs = refs[o0:o0 + n_out]
        rw_refs = refs[o0 + n_out:o0 + n_out + n_rw]
        new_refs = refs[o0 + n_out + n_rw:o0 + n_out + n_rw + n_new]
        s0 = o0 + n_out + n_rw + n_new
        scr = refs[s0:s0 + n_scr]
        send_sems, recv_sems = refs[s0 + n_scr:]
        first = functools.reduce(jnp.logical_and, [pl.program_id(a) == 0 for a in range(len(grid))])
        last = functools.reduce(jnp.logical_and, [pl.program_id(a) == grid[a] - 1 for a in range(len(grid))])
        starts, arrivals = comm.program(ro_refs, rw_refs, new_refs, send_sems, recv_sems)

        @pl.when(first)
        def _():
            for cp in starts:
                cp.start()

        body(*ins, *outs, *scr)

        @pl.when(last)
        def _():
            for cp in arrivals:
                cp.wait_recv()
            for cp in starts:
                cp.wait_send()

    def call(*args):
        rw_shapes = tuple(jax.ShapeDtypeStruct(a.shape, a.dtype) for a in comm.rw)
        res = pl.pallas_call(
            carrier, name=name, out_shape=out_shapes + rw_shapes + tuple(comm.new), grid=grid,
            in_specs=list(in_specs) + [ANY] * (n_ro + n_rw),
            out_specs=out_specs_t + (ANY,) * (n_rw + n_new),
            scratch_shapes=list(scratch) + [pltpu.SemaphoreType.DMA((comm.nsem,)), pltpu.SemaphoreType.DMA((comm.nsem,))],
            input_output_aliases={**aliases, **{n_in + n_ro + i: n_out + i for i in range(n_rw)}},
            compiler_params=pltpu.CompilerParams(dimension_semantics=("arbitrary",) * len(grid),
                                                 vmem_limit_bytes=VMEM_LIMIT))(*args, *comm.ro, *comm.rw)
        compute = res[0] if single else tuple(res[:n_out])
        return compute, list(res[n_out:n_out + n_rw]), list(res[n_out + n_rw:])

    return call


def comm_only(name, comm):
    n_ro, n_rw, n_new = len(comm.ro), len(comm.rw), len(comm.new)

    def body(*refs):
        ro_refs = refs[:n_ro]
        rw_refs = refs[n_ro + n_rw:n_ro + 2 * n_rw]
        new_refs = refs[n_ro + 2 * n_rw:n_ro + 2 * n_rw + n_new]
        send_sems, recv_sems = refs[n_ro + 2 * n_rw + n_new:]
        starts, arrivals = comm.program(ro_refs, rw_refs, new_refs, send_sems, recv_sems)
        for cp in starts:
            cp.start()
        for cp in arrivals:
            cp.wait_recv()
        for cp in starts:
            cp.wait_send()

    rw_shapes = tuple(jax.ShapeDtypeStruct(a.shape, a.dtype) for a in comm.rw)
    res = pl.pallas_call(body, name=name, out_shape=rw_shapes + tuple(comm.new), in_specs=[ANY] * (n_ro + n_rw),
                         out_specs=(ANY,) * (n_rw + n_new), input_output_aliases={n_ro + i: i for i in range(n_rw)},
                         scratch_shapes=[pltpu.SemaphoreType.DMA((comm.nsem,)), pltpu.SemaphoreType.DMA((comm.nsem,))])(
                             *comm.ro, *comm.rw)
    return list(res[:n_rw]), list(res[n_rw:])


def _pick(n, cands):
    for t in cands:
        if n % t == 0:
            return t
    return n


def _seg(t):
    return jnp.where(t % TPE == 0, BL, t // TPE)


def _slot(t):
    return 2 * (t // TPE) + jnp.where(t % TPE == 0, 0, 1)


def _sigmoid(x):
    return 1.0 / (1.0 + jnp.exp(-x))


MM_BUDGET = 40 * 1024 * 1024
N_TILE_CAP = 1664


def _tile(n, cap=N_TILE_CAP):
    if n <= cap:
        return n
    for t in range(cap - cap % LANE, 0, -LANE):
        if n % t == 0:
            return t
    return n


def _row_tile(m, bytes_of):
    for tm in (1024, 512, 256, 128):
        if m % tm == 0 and bytes_of(tm) <= MM_BUDGET:
            return tm
    return m


def _w_dims(w):
    arr, kind = w
    if kind == "cols":
        return arr.shape[1], NCHIP * arr.shape[2]
    return arr.shape


def _sz(dtype):
    return jnp.dtype(dtype).itemsize


def mm_nn(a, w, *, bias=None, out_dtype=F32, name):
    arr, kind = w
    m, k = a.shape
    _, n = _w_dims(w)
    tn = _tile(arr.shape[2]) if kind == "cols" else _tile(n)
    tm = _row_tile(m, lambda t: 2 * (t * k * _sz(a.dtype) + k * tn * 2 + t * tn * _sz(out_dtype)))
    if kind == "mat":
        b_spec = pl.BlockSpec((k, tn), lambda j, i: (0, j))
    else:
        per = arr.shape[2] // tn
        b_spec = pl.BlockSpec((None, k, tn), lambda j, i: (j // per, 0, j % per))
    has_bias = bias is not None

    def body(*refs):
        out = jnp.dot(refs[0][...].astype(BF16), refs[1][...].astype(BF16), preferred_element_type=F32)
        if has_bias:
            out = out + refs[2][...]
        refs[-1][...] = out.astype(out_dtype)

    in_specs = [pl.BlockSpec((tm, k), lambda j, i: (i, 0)), b_spec]
    args = [a, arr]
    if has_bias:
        in_specs.append(pl.BlockSpec((1, tn), lambda j, i: (0, j)))
        args.append(bias)
    return _pcall(body, name=name, out_shape=jax.ShapeDtypeStruct((m, n), out_dtype), grid=(n // tn, m // tm),
                  in_specs=in_specs, out_specs=pl.BlockSpec((tm, tn), lambda j, i: (i, j)),
                  sem=("parallel", "parallel"))(*args)


def mm_nt(a, w, *, acc=None, out_dtype=F32, name, comm=None):
    arr, kind = w
    kdim, _ = _w_dims(w)
    has_acc = acc is not None
    tk = _tile(kdim, 1408)
    if kind == "cols":
        c = arr.shape[2]
        m = a.shape[-2]
        if a.ndim == 3:
            a_spec = lambda t: pl.BlockSpec((None, t, c), lambda j, i, s: (s // 2, i, s % 2))
        else:
            a_spec = lambda t: pl.BlockSpec((t, c), lambda j, i, s: (i, s))
        tm = _row_tile(m, lambda t: 2 * (t * c * _sz(a.dtype) + tk * c * 2 + t * tk * _sz(out_dtype)) + t * tk * 4)

        def body(a_ref, b_ref, o_ref, acc_ref):
            s = pl.program_id(2)

            @pl.when(s == 0)
            def _():
                acc_ref[...] = jnp.zeros_like(acc_ref)

            acc_ref[...] += lax.dot_general(a_ref[...].astype(BF16), b_ref[...], (((1,), (1,)), ((), ())),
                                            preferred_element_type=F32)

            @pl.when(s == NCHIP - 1)
            def _():
                o_ref[...] = acc_ref[...].astype(out_dtype)

        return _pcall(body, name=name, out_shape=jax.ShapeDtypeStruct((m, kdim), out_dtype),
                      grid=(kdim // tk, m // tm, NCHIP),
                      in_specs=[a_spec(tm), pl.BlockSpec((None, tk, c), lambda j, i, s: (s, j, 0))],
                      out_specs=pl.BlockSpec((tm, tk), lambda j, i, s: (i, j)),
                      scratch=[pltpu.VMEM((tm, tk), F32)],
                      sem=("parallel", "parallel", "arbitrary"), comm=comm)(a, arr)

    m, n = a.shape
    tm = _row_tile(m, lambda t: 2 * (t * n * _sz(a.dtype) + tk * n * 2 + t * tk * (_sz(out_dtype) + 4 * has_acc))
                   + tk * n * 2)
    b_spec = pl.BlockSpec((tk, n), lambda j, i: (j, 0))

    def body(*refs):
        wt_ref = refs[-1]

        @pl.when(pl.program_id(1) == 0)
        def _():
            wt_ref[...] = refs[1][...].astype(BF16).T

        out = jnp.dot(refs[0][...].astype(BF16), wt_ref[...], preferred_element_type=F32)
        if has_acc:
            out = out + refs[2][...]
        refs[-2][...] = out.astype(out_dtype)

    in_specs = [pl.BlockSpec((tm, n), lambda j, i: (i, 0)), b_spec]
    args = [a, arr]
    if has_acc:
        in_specs.append(pl.BlockSpec((tm, tk), lambda j, i: (i, j)))
        args.append(acc)
    return _pcall(body, name=name, out_shape=jax.ShapeDtypeStruct((m, kdim), out_dtype), grid=(kdim // tk, m // tm),
                  in_specs=in_specs, out_specs=pl.BlockSpec((tm, tk), lambda j, i: (i, j)),
                  scratch=[pltpu.VMEM((n, tk), BF16)], sem=("parallel", "arbitrary"))(*args)


def mm_tn(a, b, *, cols=False, out_dtype=F32, name):
    rows, k = a.shape
    halves = b.ndim == 3
    n = 2 * b.shape[2] if halves else b.shape[1]
    odt = out_dtype
    if cols:
        c = n // NCHIP
        tn, tk = _tile(c), k
        per = c // tn
        out_spec = pl.BlockSpec((None, tk, tn), lambda i, j, r: (j // per, 0, j % per))
        out_shape = jax.ShapeDtypeStruct((NCHIP, k, c), odt)
    else:
        tn, tk = _tile(n), _tile(k, 1408)
        out_spec = pl.BlockSpec((tk, tn), lambda i, j, r: (i, j))
        out_shape = jax.ShapeDtypeStruct((k, n), odt)
    tr = _row_tile(rows, lambda t: 2 * (t * tk * _sz(a.dtype) + t * tn * _sz(b.dtype) + tk * tn * _sz(odt)) + tk * tn * 4)
    nsteps = rows // tr

    def body(*refs):
        a_ref, b_ref = refs[0], refs[1]
        o_ref, acc_ref = refs[-2], refs[-1]
        r = pl.program_id(2)

        @pl.when(r == 0)
        def _():
            acc_ref[...] = jnp.zeros_like(acc_ref)

        acc_ref[...] += lax.dot_general(a_ref[...].astype(BF16), b_ref[...].astype(BF16), (((0,), (0,)), ((), ())),
                                        preferred_element_type=F32)

        @pl.when(r == nsteps - 1)
        def _():
            o_ref[...] = acc_ref[...].astype(odt)

    if halves:
        per_half = (n // 2) // tn
        b_spec = pl.BlockSpec((None, tr, tn), lambda i, j, r: (j // per_half, r, j % per_half))
    else:
        b_spec = pl.BlockSpec((tr, tn), lambda i, j, r: (r, j))
    return _pcall(body, name=name, out_shape=out_shape, grid=(k // tk, n // tn, nsteps),
                  in_specs=[pl.BlockSpec((tr, tk), lambda i, j, r: (r, i)), b_spec], out_specs=out_spec,
                  scratch=[pltpu.VMEM((tk, tn), F32)], sem=("parallel", "parallel", "arbitrary"))(a, b)


def ffn_in_swiglu(h, w_in, name, comm=None):
    m, k = h.shape
    c = w_in.shape[2]
    tm = 512

    def body(h_ref, wa_ref, wb_ref, f_ref, u_ref):
        for r0 in range(0, tm, tm // 2):
            rows = slice(r0, r0 + tm // 2)
            hv = h_ref[rows, :]
            a = jnp.dot(hv, wa_ref[...], preferred_element_type=F32)
            b = jnp.dot(hv, wb_ref[...], preferred_element_type=F32)
            f_ref[rows, :] = (a * _sigmoid(a) * b).astype(BF16)
            u_ref[0, rows, :] = a.astype(BF16)
            u_ref[1, rows, :] = b.astype(BF16)

    return _pcall(body, name=name,
                  out_shape=(jax.ShapeDtypeStruct((m, FH), BF16), jax.ShapeDtypeStruct((2, m, FH), BF16)),
                  grid=(2, m // tm),
                  in_specs=[pl.BlockSpec((tm, k), lambda j, i: (i, 0)),
                            pl.BlockSpec((None, k, c), lambda j, i: (j, 0, 0)),
                            pl.BlockSpec((None, k, c), lambda j, i: (2 + j, 0, 0))],
                  out_specs=(pl.BlockSpec((tm, c), lambda j, i: (i, j)), pl.BlockSpec((2, tm, c), lambda j, i: (0, i, j))),
                  sem=("parallel", "parallel"), comm=comm)(h, w_in, w_in)


def d_f_swiglu(dof, w_out, u2, name):
    m, k = dof.shape
    c = FH // 2
    tm = 512

    def body(d_ref, w_ref, u_ref, du_ref, wt_ref):
        @pl.when(pl.program_id(1) == 0)
        def _():
            wt_ref[...] = w_ref[...].T

        for r0 in range(0, tm, tm // 2):
            rows = slice(r0, r0 + tm // 2)
            df = jnp.dot(d_ref[rows, :], wt_ref[...], preferred_element_type=F32)
            a, b = u_ref[0, rows, :].astype(F32), u_ref[1, rows, :].astype(F32)
            sg = _sigmoid(a)
            du_ref[0, rows, :] = (df * b * (sg * (1.0 + a * (1.0 - sg)))).astype(BF16)
            du_ref[1, rows, :] = (df * a * sg).astype(BF16)

    ublk = pl.BlockSpec((2, tm, c), lambda j, i: (0, i, j))
    return _pcall(body, name=name, out_shape=jax.ShapeDtypeStruct((2, m, FH), BF16), grid=(2, m // tm),
                  in_specs=[pl.BlockSpec((tm, k), lambda j, i: (i, 0)), pl.BlockSpec((c, k), lambda j, i: (j, 0)), ublk],
                  out_specs=ublk, scratch=[pltpu.VMEM((k, c), w_out.dtype)], sem=("parallel", "arbitrary"))(dof, w_out, u2)


GATE_TN = 512


def gate_mm_fwd(h, wi_gate, o, hs, z, wo, wc, ws, name):
    m, k = h.shape
    tm, tn = 512, min(GATE_TN, D)
    nj = D // tn

    def body(h_ref, g0_ref, g1_ref, g2_ref, o_ref, hs_ref, z_ref, wo_ref, wc_ref, ws_ref, m_ref, g_ref, y_ref):
        hv = h_ref[...]
        acc = jnp.zeros((tm, tn), F32)
        for g, (gw_ref, x_ref, w_ref) in enumerate(((g0_ref, o_ref, wo_ref), (g1_ref, hs_ref, wc_ref),
                                                    (g2_ref, z_ref, ws_ref))):
            gate = _sigmoid(jnp.dot(hv, gw_ref[...], preferred_element_type=F32))
            y = jnp.dot(x_ref[...], w_ref[...], preferred_element_type=F32)
            acc += gate * y
            g_ref[g] = gate.astype(BF16)
            y_ref[g] = y.astype(BF16)
        m_ref[...] = acc.astype(BF16)

    def gate_w(g):
        return pl.BlockSpec((k, tn), lambda j, i: (0, g * nj + j))

    def branch(width):
        return pl.BlockSpec((tm, width), lambda j, i: (i, 0))

    def branch_w(width):
        return pl.BlockSpec((width, tn), lambda j, i: (0, j))

    stacked = pl.BlockSpec((3, tm, tn), lambda j, i: (0, i, j))
    sds3 = jax.ShapeDtypeStruct((3, m, D), BF16)
    return _pcall(body, name=name, out_shape=(jax.ShapeDtypeStruct((m, D), BF16), sds3, sds3), grid=(nj, m // tm),
                  in_specs=[pl.BlockSpec((tm, k), lambda j, i: (i, 0)), gate_w(0), gate_w(1), gate_w(2),
                            branch(o.shape[1]), branch(hs.shape[1]), branch(z.shape[1]),
                            branch_w(wo.shape[0]), branch_w(wc.shape[0]), branch_w(ws.shape[0])],
                  out_specs=(pl.BlockSpec((tm, tn), lambda j, i: (i, j)), stacked, stacked),
                  sem=("parallel", "parallel"))(h, wi_gate, wi_gate, wi_gate, o, hs, z, wo, wc, ws)


def d_merged_gate(dmixed, w_mix, gates, ys, name):
    m, k = dmixed.shape
    tm = 256

    def body(d_ref, w_ref, g_ref, y_ref, da_ref, db_ref, ds_ref, dp_ref, wt_ref):
        @pl.when(pl.program_id(0) == 0)
        def _():
            wt_ref[...] = w_ref[...].T

        dm = jnp.dot(d_ref[...], wt_ref[...], preferred_element_type=F32)
        for g, dy_ref in enumerate((da_ref, db_ref, ds_ref)):
            gate = g_ref[g].astype(F32)
            dy_ref[...] = (dm * gate).astype(BF16)
            dp_ref[:, g * D:(g + 1) * D] = (dm * y_ref[g].astype(F32) * gate * (1.0 - gate)).astype(BF16)

    stacked = pl.BlockSpec((3, tm, D), lambda i: (0, i, 0))
    row = pl.BlockSpec((tm, D), lambda i: (i, 0))
    sds = jax.ShapeDtypeStruct((m, D), BF16)
    return _pcall(body, name=name, out_shape=(sds, sds, sds, jax.ShapeDtypeStruct((m, 3 * D), BF16)), grid=(m // tm,),
                  in_specs=[pl.BlockSpec((tm, k), lambda i: (i, 0)), pl.BlockSpec((D, k), lambda i: (0, 0)),
                            stacked, stacked],
                  out_specs=(row, row, row, pl.BlockSpec((tm, 3 * D), lambda i: (i, 0))),
                  scratch=[pltpu.VMEM((k, D), w_mix.dtype)], sem=("arbitrary",))(dmixed, w_mix, gates, ys)


def _mods_spec():
    return pl.BlockSpec((1, 1, NMOD * D), lambda t: (_seg(t), 0, 0))


def _rows(width):
    return pl.BlockSpec((TM, width), lambda t: (t, 0))


def norm_mod_fwd(x, mods, k_sh, k_sc, name):
    def body(x_ref, m_ref, h_ref):
        x = x_ref[...]
        r = lax.rsqrt(jnp.mean(x * x, axis=-1, keepdims=True) + EPS)
        sh = m_ref[0, :, k_sh * D:(k_sh + 1) * D]
        sc = m_ref[0, :, k_sc * D:(k_sc + 1) * D]
        h_ref[...] = (x * r * (1.0 + sc) + sh).astype(BF16)

    return _pcall(body, name=name, out_shape=jax.ShapeDtypeStruct((NROW, D), BF16), grid=(NT,),
                  in_specs=[_rows(D), _mods_spec()], out_specs=_rows(D), sem=("parallel",))(x, mods)


def _accumulate_slot(t, ref, part):
    first = (t % TPE) <= 1

    @pl.when(first)
    def _():
        ref[0] = part

    @pl.when(jnp.logical_not(first))
    def _():
        ref[0] += part


def norm_mod_bwd(x, mods, dh, dres, k_sc, name):
    def body(x_ref, m_ref, dh_ref, dres_ref, dx_ref, dp_ref):
        t = pl.program_id(0)
        x = x_ref[...]
        r = lax.rsqrt(jnp.mean(x * x, axis=-1, keepdims=True) + EPS)
        xn = x * r
        sc = m_ref[0, :, k_sc * D:(k_sc + 1) * D]
        dh = dh_ref[...]
        dxn = dh * (1.0 + sc)
        dx_ref[...] = r * (dxn - xn * jnp.mean(dxn * xn, axis=-1, keepdims=True)) + dres_ref[...]
        part = jnp.concatenate([jnp.sum(dh, axis=0, keepdims=True), jnp.sum(dh * xn, axis=0, keepdims=True)], axis=1)
        _accumulate_slot(t, dp_ref, part)

    return _pcall(body, name=name,
                  out_shape=(jax.ShapeDtypeStruct((NROW, D), F32), jax.ShapeDtypeStruct((2 * BL, 1, 2 * D), F32)),
                  grid=(NT,), in_specs=[_rows(D), _mods_spec(), _rows(D), _rows(D)],
                  out_specs=(_rows(D), pl.BlockSpec((1, 1, 2 * D), lambda t: (_slot(t), 0, 0))),
                  sem=("arbitrary",))(x, mods, dh, dres)


def resid_norm_fwd(x, y, mods_g, k_g, mods_n, k_sh, k_sc, name):
    def body(x_ref, y_ref, mg_ref, mn_ref, x1_ref, h_ref):
        x1 = x_ref[...] + mg_ref[0, :, k_g * D:(k_g + 1) * D] * y_ref[...]
        x1_ref[...] = x1
        r = lax.rsqrt(jnp.mean(x1 * x1, axis=-1, keepdims=True) + EPS)
        sh = mn_ref[0, :, k_sh * D:(k_sh + 1) * D]
        sc = mn_ref[0, :, k_sc * D:(k_sc + 1) * D]
        h_ref[...] = (x1 * r * (1.0 + sc) + sh).astype(BF16)

    return _pcall(body, name=name,
                  out_shape=(jax.ShapeDtypeStruct((NROW, D), F32), jax.ShapeDtypeStruct((NROW, D), BF16)), grid=(NT,),
                  in_specs=[_rows(D), _rows(D), _mods_spec(), _mods_spec()], out_specs=(_rows(D), _rows(D)),
                  sem=("parallel",))(x, y, mods_g, mods_n)


def norm_resid_bwd(x, mods_n, dh, dres, k_sc, y, mods_g, k_g, name):
    def body(x_ref, mn_ref, dh_ref, dres_ref, y_ref, mg_ref, dx_ref, dpn_ref, dy_ref, dpg_ref):
        t = pl.program_id(0)
        x = x_ref[...]
        r = lax.rsqrt(jnp.mean(x * x, axis=-1, keepdims=True) + EPS)
        xn = x * r
        sc = mn_ref[0, :, k_sc * D:(k_sc + 1) * D]
        dh = dh_ref[...]
        dxn = dh * (1.0 + sc)
        dx = r * (dxn - xn * jnp.mean(dxn * xn, axis=-1, keepdims=True)) + dres_ref[...]
        dx_ref[...] = dx
        dy_ref[...] = (dx * mg_ref[0, :, k_g * D:(k_g + 1) * D]).astype(BF16)
        part = jnp.concatenate([jnp.sum(dh, axis=0, keepdims=True), jnp.sum(dh * xn, axis=0, keepdims=True)], axis=1)
        _accumulate_slot(t, dpn_ref, part)
        _accumulate_slot(t, dpg_ref, jnp.sum(dx * y_ref[...], axis=0, keepdims=True))

    def slot(width):
        return pl.BlockSpec((1, 1, width), lambda t: (_slot(t), 0, 0))

    return _pcall(body, name=name,
                  out_shape=(jax.ShapeDtypeStruct((NROW, D), F32), jax.ShapeDtypeStruct((2 * BL, 1, 2 * D), F32),
                             jax.ShapeDtypeStruct((NROW, D), BF16), jax.ShapeDtypeStruct((2 * BL, 1, D), F32)),
                  grid=(NT,), in_specs=[_rows(D), _mods_spec(), _rows(D), _rows(D), _rows(D), _mods_spec()],
                  out_specs=(_rows(D), slot(2 * D), _rows(D), slot(D)), sem=("arbitrary",))(x, mods_n, dh, dres, y, mods_g)


def gate_resid_fwd(x, y, mods, k_g, name):
    def body(x_ref, y_ref, m_ref, o_ref):
        o_ref[...] = x_ref[...] + m_ref[0, :, k_g * D:(k_g + 1) * D] * y_ref[...]

    return _pcall(body, name=name, out_shape=jax.ShapeDtypeStruct((NROW, D), F32), grid=(NT,),
                  in_specs=[_rows(D), _rows(D), _mods_spec()], out_specs=_rows(D), sem=("parallel",))(x, y, mods)


def gate_resid_bwd(dx, y, mods, k_g, name):
    def body(dx_ref, y_ref, m_ref, dy_ref, dp_ref):
        t = pl.program_id(0)
        dx = dx_ref[...]
        dy_ref[...] = (dx * m_ref[0, :, k_g * D:(k_g + 1) * D]).astype(BF16)
        _accumulate_slot(t, dp_ref, jnp.sum(dx * y_ref[...], axis=0, keepdims=True))

    return _pcall(body, name=name,
                  out_shape=(jax.ShapeDtypeStruct((NROW, D), BF16), jax.ShapeDtypeStruct((2 * BL, 1, D), F32)),
                  grid=(NT,), in_specs=[_rows(D), _rows(D), _mods_spec()],
                  out_specs=(_rows(D), pl.BlockSpec((1, 1, D), lambda t: (_slot(t), 0, 0))),
                  sem=("arbitrary",))(dx, y, mods)


def _swap16(y, lo16):
    return jnp.where(lo16, pltpu.roll(y, LANE - 16, 1), pltpu.roll(y, 16, 1))


def _group_mean(v, g_mat):
    return jnp.dot(v, g_mat, precision=HIGHEST, preferred_element_type=F32)


def qkv_fwd(p_main, cos_t, sin_t, g_mat, gq, gk, name):
    def body(p_ref, cos_ref, sin_ref, g_ref, gq_ref, gk_ref, q_ref, k_ref, v_ref):
        cos, sin, g_mat_v = cos_ref[...], sin_ref[...], g_ref[...]
        lo16 = (lax.broadcasted_iota(jnp.int32, (TM, LANE), 1) % 32) < 16

        def block(xb, g):
            xb = xb.astype(F32)
            r = lax.rsqrt(_group_mean(xb * xb, g_mat_v) + EPS)
            y = xb * r * g
            return y * cos + _swap16(y, lo16) * sin

        for j in range(AW // LANE):
            q_ref[:, j * LANE:(j + 1) * LANE] = (block(p_ref[:, j * LANE:(j + 1) * LANE], gq_ref[...])
                                                 * ATTN_SCALE).astype(BF16)
        lo = lax.broadcasted_iota(jnp.int32, (TM, LANE), 1) < HD
        for src, dst_ref in ((block(p_ref[:, OFF_K:OFF_K + LANE], gk_ref[...]), k_ref), (p_ref[:, OFF_V:OFF_V + LANE].astype(F32), v_ref)):
            swapped = pltpu.roll(src, HD, 1)
            dst_ref[:, 0:LANE] = jnp.where(lo, src, swapped).astype(BF16)
            dst_ref[:, LANE:2 * LANE] = jnp.where(lo, swapped, src).astype(BF16)

    tab = pl.BlockSpec((TM, LANE), lambda t: (t % TPE, 0))
    small = pl.BlockSpec((1, LANE), lambda t: (0, 0))
    return _pcall(body, name=name,
                  out_shape=(jax.ShapeDtypeStruct((NROW, AW), BF16), jax.ShapeDtypeStruct((NROW, 2 * KVW), BF16),
                             jax.ShapeDtypeStruct((NROW, 2 * KVW), BF16)),
                  grid=(NT,),
                  in_specs=[_rows(QKVW), tab, tab, pl.BlockSpec((LANE, LANE), lambda t: (0, 0)), small, small],
                  out_specs=(_rows(AW), _rows(2 * KVW), _rows(2 * KVW)),
                  sem=("parallel",))(p_main, cos_t, sin_t, g_mat, gq, gk)


def qkv_bwd(p_main, cos_t, sin_t, g_mat, gq, gk, dq, dk, dv, name, comm=None):
    def body(p_ref, cos_ref, sin_ref, g_ref, gq_ref, gk_ref, dq_ref, dk_ref, dv_ref, dp_ref, dg_ref):
        t = pl.program_id(0)
        cos, sin, g_mat_v = cos_ref[...], sin_ref[...], g_ref[...]
        lo16 = (lax.broadcasted_iota(jnp.int32, (TM, LANE), 1) % 32) < 16

        def block(xb, g, dyr):
            xb = xb.astype(F32)
            r = lax.rsqrt(_group_mean(xb * xb, g_mat_v) + EPS)
            xn = xb * r
            dy = dyr * cos + _swap16(dyr * sin, lo16)
            dgl = jnp.sum(dy * xn, axis=0, keepdims=True)
            dxn = dy * g
            return r * (dxn - xn * _group_mean(dxn * xn, g_mat_v)), dgl

        parts = []
        for j in range(AW // LANE):
            sl = slice(j * LANE, (j + 1) * LANE)
            dxb, dgl = block(p_ref[:, sl], gq_ref[...], dq_ref[:, sl] * ATTN_SCALE)
            dp_ref[:, sl] = dxb.astype(BF16)
            parts.append(dgl)
        lo = lax.broadcasted_iota(jnp.int32, (TM, LANE), 1) < HD

        def fold(d_ref):
            d0, d1 = d_ref[:, 0:LANE], d_ref[:, LANE:2 * LANE]
            return jnp.where(lo, d0 + pltpu.roll(d0, HD, 1), d1 + pltpu.roll(d1, HD, 1))

        dxb, dgl = block(p_ref[:, OFF_K:OFF_K + LANE], gk_ref[...], fold(dk_ref))
        dp_ref[:, OFF_K:OFF_K + LANE] = dxb.astype(BF16)
        parts.append(dgl)
        parts.append(jnp.zeros((1, LANE), F32))
        dp_ref[:, OFF_V:OFF_V + LANE] = fold(dv_ref).astype(BF16)
        part = jnp.concatenate(parts, axis=1)

        @pl.when(t == 0)
        def _():
            dg_ref[...] = part

        @pl.when(t != 0)
        def _():
            dg_ref[...] += part

    tab = pl.BlockSpec((TM, LANE), lambda t: (t % TPE, 0))
    small = pl.BlockSpec((1, LANE), lambda t: (0, 0))
    return _pcall(body, name=name,
                  out_shape=(jax.ShapeDtypeStruct((NROW, QKVW), BF16), jax.ShapeDtypeStruct((1, QKVW), F32)),
                  grid=(NT,),
                  in_specs=[_rows(QKVW), tab, tab, pl.BlockSpec((LANE, LANE), lambda t: (0, 0)), small, small,
                            _rows(AW), _rows(2 * KVW), _rows(2 * KVW)],
                  out_specs=(_rows(QKVW), pl.BlockSpec((1, QKVW), lambda t: (0, 0))),
                  sem=("arbitrary",), comm=comm)(p_main, cos_t, sin_t, g_mat, gq, gk, dq, dk, dv)


def _layer_norm_parts(yc):
    mu = jnp.mean(yc, axis=-1, keepdims=True)
    xc = yc - mu
    rs = lax.rsqrt(jnp.mean(xc * xc, axis=-1, keepdims=True) + EPS)
    return xc * rs, rs


def ln_silu_fwd(yc, g, b, name):
    def body(y_ref, g_ref, b_ref, o_ref):
        nrm, _ = _layer_norm_parts(y_ref[...])
        ln = nrm * g_ref[...] + b_ref[...]
        o_ref[...] = (ln * _sigmoid(ln)).astype(BF16)

    vec = pl.BlockSpec((1, CW), lambda t: (0, 0))
    return _pcall(body, name=name, out_shape=jax.ShapeDtypeStruct((NROW, CW), BF16), grid=(NT,),
                  in_specs=[_rows(CW), vec, vec], out_specs=_rows(CW), sem=("parallel",))(yc, g, b)


def ln_silu_bwd(yc, g, b, dhs, name, comm=None):
    def body(y_ref, g_ref, b_ref, dh_ref, dy_ref, dg_ref, db_ref):
        t = pl.program_id(0)
        nrm, rs = _layer_norm_parts(y_ref[...])
        ln = nrm * g_ref[...] + b_ref[...]
        sg = _sigmoid(ln)
        dln = dh_ref[...] * (sg * (1.0 + ln * (1.0 - sg)))
        dn = dln * g_ref[...]
        dy_ref[...] = rs * (dn - jnp.mean(dn, axis=-1, keepdims=True)
                            - nrm * jnp.mean(dn * nrm, axis=-1, keepdims=True))
        pg = jnp.sum(dln * nrm, axis=0, keepdims=True)
        pb = jnp.sum(dln, axis=0, keepdims=True)

        @pl.when(t == 0)
        def _():
            dg_ref[...] = pg
            db_ref[...] = pb

        @pl.when(t != 0)
        def _():
            dg_ref[...] += pg
            db_ref[...] += pb

    vec = pl.BlockSpec((1, CW), lambda t: (0, 0))
    return _pcall(body, name=name,
                  out_shape=(jax.ShapeDtypeStruct((NROW, CW), F32), jax.ShapeDtypeStruct((1, CW), F32),
                             jax.ShapeDtypeStruct((1, CW), F32)),
                  grid=(NT,), in_specs=[_rows(CW), vec, vec, _rows(CW)], out_specs=(_rows(CW), vec, vec),
                  sem=("arbitrary",), comm=comm)(yc, g, b, dhs)


def loss_fwd_bwd(y, target, name):
    def body(y_ref, t_ref, dy_ref, l_ref):
        t = pl.program_id(0)
        latent = (t % TPE) != 0
        err = jnp.where(latent, y_ref[...] - t_ref[...], 0.0)
        dy_ref[...] = err * (1.0 / D)
        part = jnp.sum(err * err, axis=0, keepdims=True)

        @pl.when(t == 0)
        def _():
            l_ref[...] = part

        @pl.when(t != 0)
        def _():
            l_ref[...] += part

    tgt = pl.BlockSpec((TM, D), lambda t: ((t // TPE) * (TPE - 1) + jnp.maximum(t % TPE - 1, 0), 0))
    return _pcall(body, name=name,
                  out_shape=(jax.ShapeDtypeStruct((NROW, D), F32), jax.ShapeDtypeStruct((1, D), F32)),
                  grid=(NT,), in_specs=[_rows(D), tgt], out_specs=(_rows(D), pl.BlockSpec((1, D), lambda t: (0, 0))),
                  sem=("arbitrary",))(y, target)


QB_PER_KV = AW // LANE // NKV


def _softmax_parts(qm, k):
    s = lax.dot_general(qm, k, (((1,), (1,)), ((), ())), preferred_element_type=F32)
    e = jnp.exp(s - jnp.max(s, axis=-1, keepdims=True))
    return e, 1.0 / jnp.sum(e, axis=-1, keepdims=True)


def _lane_halves():
    lo = lax.broadcasted_iota(jnp.int32, (TM, LANE), 1) < HD
    return lo, jnp.logical_not(lo)


def _stack_heads(x, halves):
    zero = jnp.zeros_like(x)
    return jnp.concatenate([jnp.where(halves[0], x, zero), jnp.where(halves[1], x, zero)], axis=0)


def attn_fwd(q, k, v, name, comm=None):
    def body(q_ref, k_ref, v_ref, o_ref):
        t = pl.program_id(2)
        halves = _lane_halves()

        def run(nk):
            kv, vv = k_ref[0:nk, :], v_ref[0:nk, :]
            for j in range(QB_PER_KV):
                lanes = slice(j * LANE, (j + 1) * LANE)
                e, rinv = _softmax_parts(_stack_heads(q_ref[:, lanes], halves), kv)
                out = jnp.dot(e.astype(BF16), vv, preferred_element_type=F32) * rinv
                o_ref[:, lanes] = jnp.where(halves[0], out[0:TM], out[TM:2 * TM]).astype(BF16)

        @pl.when(t == 0)
        def _():
            run(CTX)

        @pl.when(t != 0)
        def _():
            run(RE)

    qs = pl.BlockSpec((TM, QB_PER_KV * LANE), lambda b, h, t: (b * TPE + t, h))
    ks = pl.BlockSpec((RE, LANE), lambda b, h, t: (b, h))
    return _pcall(body, name=name, out_shape=jax.ShapeDtypeStruct((NROW, AW), BF16), grid=(BL, NKV, TPE),
                  in_specs=[qs, ks, ks], out_specs=qs, sem=("parallel",) * 3, comm=comm)(q, k, v)


def attn_bwd(q, k, v, o, do, name, comm=None):
    def body(q_ref, k_ref, v_ref, o_ref, do_ref, dq_ref, dk_ref, dv_ref):
        t = pl.program_id(2)
        halves = _lane_halves()

        @pl.when(t == 0)
        def _():
            dk_ref[...] = jnp.zeros_like(dk_ref)
            dv_ref[...] = jnp.zeros_like(dv_ref)

        def run(nk):
            kv, vv = k_ref[0:nk, :], v_ref[0:nk, :]
            dks, dvs = [], []
            for j in range(QB_PER_KV):
                lanes = slice(j * LANE, (j + 1) * LANE)
                q2, do2 = _stack_heads(q_ref[:, lanes], halves), _stack_heads(do_ref[:, lanes], halves)
                ov = o_ref[:, lanes].astype(F32)
                delta = jnp.sum(do2.astype(F32) * jnp.concatenate([ov, ov], axis=0), axis=-1, keepdims=True)
                e, rinv = _softmax_parts(q2, kv)
                p = e * rinv
                dvs.append(lax.dot_general(p.astype(BF16), do2, (((0,), (0,)), ((), ())), preferred_element_type=F32))
                dp = lax.dot_general(do2, vv, (((1,), (1,)), ((), ())), preferred_element_type=F32)
                ds = (p * (dp - delta)).astype(BF16)
                dq = jnp.dot(ds, kv, preferred_element_type=F32)
                dks.append(lax.dot_general(ds, q2, (((0,), (0,)), ((), ())), preferred_element_type=F32))
                dq_ref[:, lanes] = jnp.where(halves[0], dq[0:TM], dq[TM:2 * TM])
            dv_ref[0:nk, :] += functools.reduce(jnp.add, dvs)
            dk_ref[0:nk, :] += functools.reduce(jnp.add, dks)

        @pl.when(t == 0)
        def _():
            run(CTX)

        @pl.when(t != 0)
        def _():
            run(RE)

    qs = pl.BlockSpec((TM, QB_PER_KV * LANE), lambda b, h, t: (b * TPE + t, h))
    ks = pl.BlockSpec((RE, LANE), lambda b, h, t: (b, h))
    return _pcall(body, name=name,
                  out_shape=(jax.ShapeDtypeStruct((NROW, AW), F32), jax.ShapeDtypeStruct((NROW, 2 * KVW), F32),
                             jax.ShapeDtypeStruct((NROW, 2 * KVW), F32)),
                  grid=(BL, NKV, TPE), in_specs=[qs, ks, ks, qs, qs], out_specs=(qs, ks, ks),
                  sem=("parallel", "parallel", "arbitrary"), comm=comm)(q, k, v, o, do)


CONV_SEGS = ((0, CTX), (CTX, SEQ))


def _p_block(col0):
    return pl.BlockSpec((RE, LANE), lambda cb, b: (b, col0 // LANE + cb))


def _conv_io(width):
    return pl.BlockSpec((RE, LANE), lambda cb, b: (b, cb))


def _taps(n):
    return pl.BlockSpec((n, LANE), lambda cb, b: (0, cb))


def _fill_pad(pad_ref, length, values):
    pad_ref[0:PADR, :] = jnp.zeros((PADR, LANE), F32)
    pad_ref[PADR + length:2 * PADR + length, :] = jnp.zeros((PADR, LANE), F32)
    pad_ref[PADR:PADR + length, :] = values


def _conv_chunk(pad_ref, w_ref, ntap, c0, first_row):
    acc = jnp.zeros((CONV_CH, LANE), F32)
    for kk in range(ntap):
        r0 = c0 + first_row(kk)
        acc += w_ref[kk:kk + 1, :] * pad_ref[r0:r0 + CONV_CH, :]
    return acc


def conv_fwd(p_main, wdw, bdw, w3, name, comm=None):
    def body(a_ref, g_ref, bg_ref, cg_ref, xs_ref, w_ref, b_ref, w3_ref, yc_ref, z_ref, pad_ref):
        for off, length in CONV_SEGS:
            rows = slice(off, off + length)
            _fill_pad(pad_ref, length, a_ref[rows, :].astype(F32) * _sigmoid(g_ref[rows, :].astype(F32)))
            for c0 in range(0, length, CONV_CH):
                acc = _conv_chunk(pad_ref, w_ref, CONF_K, c0, lambda kk: PADR + kk - CONF_K // 2)
                yc_ref[off + c0:off + c0 + CONV_CH, :] = acc + b_ref[...]
            pad_ref[PADR:PADR + length, :] = cg_ref[rows, :].astype(F32) * xs_ref[rows, :].astype(F32)
            for c0 in range(0, length, CONV_CH):
                acc = _conv_chunk(pad_ref, w3_ref, SC_K, c0, lambda kk: PADR + kk - SC_K // 2)
                z_ref[off + c0:off + c0 + CONV_CH, :] = (bg_ref[off + c0:off + c0 + CONV_CH, :] * acc).astype(BF16)

    return _pcall(body, name=name,
                  out_shape=(jax.ShapeDtypeStruct((NROW, CW), F32), jax.ShapeDtypeStruct((NROW, CW), BF16)),
                  grid=(CB, BL),
                  in_specs=[_p_block(OFF_CONF), _p_block(OFF_CONF + CW), _p_block(OFF_SC), _p_block(OFF_SC + CW),
                            _p_block(OFF_SC + 2 * CW), _taps(CONF_K), _taps(1), _taps(SC_K)],
                  out_specs=(_conv_io(CW), _conv_io(CW)),
                  scratch=[pltpu.VMEM((SEQ + 2 * PADR, LANE), F32)],
                  sem=("parallel", "parallel"), comm=comm)(p_main, p_main, p_main, p_main, p_main, wdw, bdw, w3)


def _tap_grad(pad_ref, d_ref, off, length, first_row):
    acc = jnp.zeros((8, LANE), F32)
    for c0 in range(0, length, CONV_CH):
        prod = d_ref[off + c0:off + c0 + CONV_CH, :] * pad_ref[c0 + first_row:c0 + first_row + CONV_CH, :]
        acc += jnp.sum(prod.reshape(CONV_CH // 8, 8, LANE), axis=0)
    return jnp.sum(acc, axis=0, keepdims=True)


def conv_bwd(p_main, wdw, w3, dyc, dz, name):
    def body(a_ref, g_ref, bg_ref, cg_ref, xs_ref, w_ref, w3_ref, dyc_ref, dz_ref,
             da_ref, dg_ref, dbg_ref, dcg_ref, dxs_ref, dw_ref, db_ref, dw3_ref, pad_x, pad_d, dconv_ref):
        b = pl.program_id(1)

        @pl.when(b == 0)
        def _():
            dw_ref[...] = jnp.zeros_like(dw_ref)
            db_ref[...] = jnp.zeros_like(db_ref)
            dw3_ref[...] = jnp.zeros_like(dw3_ref)

        db_ref[...] += jnp.sum(dyc_ref[...], axis=0, keepdims=True)
        for off, length in CONV_SEGS:
            rows = slice(off, off + length)
            _fill_pad(pad_x, length, a_ref[rows, :].astype(F32) * _sigmoid(g_ref[rows, :].astype(F32)))
            _fill_pad(pad_d, length, dyc_ref[rows, :])
            for kk in range(CONF_K):
                dw_ref[kk:kk + 1, :] += _tap_grad(pad_x, dyc_ref, off, length, PADR + kk - CONF_K // 2)
            for c0 in range(0, length, CONV_CH):
                dh = _conv_chunk(pad_d, w_ref, CONF_K, c0, lambda kk: PADR + CONF_K // 2 - kk)
                ch = slice(off + c0, off + c0 + CONV_CH)
                sg = _sigmoid(g_ref[ch, :].astype(F32))
                da_ref[ch, :] = (dh * sg).astype(BF16)
                dg_ref[ch, :] = (dh * a_ref[ch, :] * sg * (1.0 - sg)).astype(BF16)
            pad_x[PADR:PADR + length, :] = cg_ref[rows, :].astype(F32) * xs_ref[rows, :].astype(F32)
            dconv_ref[rows, :] = dz_ref[rows, :] * bg_ref[rows, :]
            pad_d[PADR:PADR + length, :] = dconv_ref[rows, :]
            for kk in range(SC_K):
                dw3_ref[kk:kk + 1, :] += _tap_grad(pad_x, dconv_ref, off, length, PADR + kk - SC_K // 2)
            for c0 in range(0, length, CONV_CH):
                ch = slice(off + c0, off + c0 + CONV_CH)
                c3 = _conv_chunk(pad_x, w3_ref, SC_K, c0, lambda kk: PADR + kk - SC_K // 2)
                dbg_ref[ch, :] = (dz_ref[ch, :] * c3).astype(BF16)
                dcx = _conv_chunk(pad_d, w3_ref, SC_K, c0, lambda kk: PADR + SC_K // 2 - kk)
                dcg_ref[ch, :] = (dcx * xs_ref[ch, :]).astype(BF16)
                dxs_ref[ch, :] = (dcx * cg_ref[ch, :]).astype(BF16)

    slab = jax.ShapeDtypeStruct((NROW, CW), BF16)
    return _pcall(body, name=name,
                  out_shape=(slab,) * 5 + (jax.ShapeDtypeStruct((CONF_K, CW), F32), jax.ShapeDtypeStruct((1, CW), F32),
                                           jax.ShapeDtypeStruct((SC_K, CW), F32)),
                  grid=(CB, BL),
                  in_specs=[_p_block(OFF_CONF), _p_block(OFF_CONF + CW), _p_block(OFF_SC), _p_block(OFF_SC + CW),
                            _p_block(OFF_SC + 2 * CW), _taps(CONF_K), _taps(SC_K), _conv_io(CW), _conv_io(CW)],
                  out_specs=(_conv_io(CW),) * 5 + (_taps(CONF_K), _taps(1), _taps(SC_K)),
                  scratch=[pltpu.VMEM((SEQ + 2 * PADR, LANE), F32), pltpu.VMEM((SEQ + 2 * PADR, LANE), F32),
                           pltpu.VMEM((RE, LANE), F32)],
                  sem=("parallel", "arbitrary"))(p_main, p_main, p_main, p_main, p_main, wdw, w3, dyc, dz)


def silu_rows(x, name):
    def body(x_ref, o_ref):
        o_ref[...] = x_ref[...] * _sigmoid(x_ref[...])

    return _pcall(body, name=name, out_shape=jax.ShapeDtypeStruct(x.shape, F32))(x)


def silu_rows_bwd(x, dcs, name):
    def body(x_ref, d_ref, o_ref):
        x = x_ref[...]
        sg = _sigmoid(x)
        tot = d_ref[0]
        for i in range(1, DEPTH):
            tot += d_ref[i]
        o_ref[...] = tot * (sg * (1.0 + x * (1.0 - sg)))

    return _pcall(body, name=name, out_shape=jax.ShapeDtypeStruct(x.shape, F32))(x, dcs)


def dmod_assemble(parts, name):
    def body(p_ref, dm_ref, db_ref):
        row = lax.broadcasted_iota(jnp.int32, (8, NMOD * D), 0)
        dm = jnp.zeros((8, NMOD * D), F32)
        db = jnp.zeros((1, NMOD * D), F32)
        for s in range(2 * BL):
            target = BL if s % 2 == 0 else s // 2
            part = p_ref[s:s + 1, :]
            dm += jnp.where(row == target, part, 0.0)
            db += part
        dm_ref[...] = dm
        db_ref[...] = db

    return _pcall(body, name=name, out_shape=(jax.ShapeDtypeStruct((8, NMOD * D), F32),
                                              jax.ShapeDtypeStruct((1, NMOD * D), F32)))(parts)


def sum_leading(x, name):
    n = x.shape[0]
    tr = _pick(x.shape[1], (256, 32, 8))

    def body(x_ref, o_ref):
        tot = x_ref[0].astype(F32)
        for i in range(1, n):
            tot += x_ref[i].astype(F32)
        o_ref[...] = tot

    return _pcall(body, name=name, out_shape=jax.ShapeDtypeStruct(x.shape[1:], F32), grid=(x.shape[1] // tr,),
                  in_specs=[pl.BlockSpec((n, tr, x.shape[2]), lambda i: (0, i, 0))],
                  out_specs=pl.BlockSpec((tr, x.shape[2]), lambda i: (i, 0)), sem=("parallel",))(x)


SLAB_ROWS = (256, 176, 128, 64, 8)


def _prefetch_call(body, name, out_shape, grid, in_specs, out_specs, sem, scalars, *args):
    spec = pltpu.PrefetchScalarGridSpec(num_scalar_prefetch=len(scalars), grid=grid, in_specs=in_specs,
                                        out_specs=out_specs)
    return pl.pallas_call(body, name=name, out_shape=out_shape, grid_spec=spec,
                          compiler_params=pltpu.CompilerParams(dimension_semantics=sem,
                                                               vmem_limit_bytes=VMEM_LIMIT))(*scalars, *args)


def cast_layers(w, chip, name):
    depth, r, c = w.shape
    tr = _pick(r, SLAB_ROWS)

    def body(s_ref, w_ref, *o_refs):
        for l in range(depth):
            o_refs[l][...] = w_ref[l].astype(BF16)

    slab = pl.BlockSpec((None, tr, c), lambda i, s: (s[0], i, 0))
    return _prefetch_call(body, name, (jax.ShapeDtypeStruct((NCHIP, r, c), BF16),) * depth, (r // tr,),
                          [pl.BlockSpec((depth, tr, c), lambda i, s: (0, i, 0))], (slab,) * depth,
                          ("parallel",), (chip,), w)


def rs_add(g, other, core, chip, name):
    _, r, c = g.shape
    rh = r // 2
    tr = _pick(rh, SLAB_ROWS)
    nblk = rh // tr

    def body(core_ref, chip_ref, g_ref, o_ref, send_ref, arr_ref):
        k = pl.program_id(1)
        tot = (g_ref[...].astype(F32) + o_ref[...].astype(F32)).astype(BF16)
        send_ref[...] = tot

        @pl.when(k == chip_ref[0])
        def _():
            arr_ref[...] = tot

    blk = (None, tr, c)
    return _prefetch_call(
        body, name, (jax.ShapeDtypeStruct(other.shape, BF16), jax.ShapeDtypeStruct(g.shape, BF16)), (nblk, NCHIP),
        [pl.BlockSpec(blk, lambda i, k, cr, ch: (k, cr[0] * nblk + i, 0)), pl.BlockSpec(blk, lambda i, k, cr, ch: (k, i, 0))],
        (pl.BlockSpec(blk, lambda i, k, cr, ch: (k, i, 0)),
         pl.BlockSpec(blk, lambda i, k, cr, ch: (ch[0], cr[0] * nblk + i, 0))),
        ("parallel", "arbitrary"), (core, chip), g, other)


def adamw_layers(w, arrs, m, v, first, prev, name, comm=None):
    depth, r, c = w.shape
    tr = _pick(r, (128, 176, 64, 8))
    nblk = r // tr
    nl = len(arrs)
    c1 = 1.0 / (1.0 - ADAM_B1 ** ADAM_STEP)
    c2 = 1.0 / (1.0 - ADAM_B2 ** ADAM_STEP)

    def body(w_ref, m_ref, v_ref, *rest):
        a_refs = rest[:nl]
        g_ref, d_ref, mo_ref, vo_ref = rest[nl + 4:nl + 8]
        li = pl.program_id(0)
        gv = None
        for idx, a_ref in enumerate(a_refs):
            tot = a_ref[0].astype(F32)
            for k in range(1, NCHIP):
                tot += a_ref[k].astype(F32)
            gv = tot if gv is None else jnp.where(li == idx, tot, gv)
        mn = ADAM_B1 * m_ref[...] + (1.0 - ADAM_B1) * gv
        vn = ADAM_B2 * v_ref[...] + (1.0 - ADAM_B2) * (gv * gv)
        g_ref[...] = gv
        d_ref[...] = -ADAM_LR * ((mn * c1) / (jnp.sqrt(vn * c2) + ADAM_EPS) + ADAM_WD * w_ref[...])
        mo_ref[...] = mn
        vo_ref[...] = vn

    def arr_spec(idx):
        return pl.BlockSpec((NCHIP, tr, c),
                            lambda li, i: (0, jnp.where(li == idx, i, jnp.where(li < idx, 0, nblk - 1)), 0))

    spec = pl.BlockSpec((None, tr, c), lambda li, i: (first + li, i, 0))
    sds = jax.ShapeDtypeStruct(w.shape, F32)
    return _pcall(body, name=name, out_shape=(sds,) * 4, grid=(nl, nblk),
                  in_specs=[spec, spec, spec] + [arr_spec(idx) for idx in range(nl)] + [ANY] * 4,
                  out_specs=(spec,) * 4, aliases={3 + nl + i: i for i in range(4)},
                  sem=("arbitrary", "arbitrary"), comm=comm)(w, m, v, *arrs, *prev)


def adamw(w, g, m, v, name):
    rows, cols = w.shape
    tr = _pick(rows, (256, 248, 128, 8))
    c1 = 1.0 / (1.0 - ADAM_B1 ** ADAM_STEP)
    c2 = 1.0 / (1.0 - ADAM_B2 ** ADAM_STEP)

    def body(w_ref, g_ref, m_ref, v_ref, d_ref, mo_ref, vo_ref):
        gv = g_ref[...]
        mn = ADAM_B1 * m_ref[...] + (1.0 - ADAM_B1) * gv
        vn = ADAM_B2 * v_ref[...] + (1.0 - ADAM_B2) * (gv * gv)
        d_ref[...] = -ADAM_LR * ((mn * c1) / (jnp.sqrt(vn * c2) + ADAM_EPS) + ADAM_WD * w_ref[...])
        mo_ref[...] = mn
        vo_ref[...] = vn

    spec = pl.BlockSpec((tr, cols), lambda i: (i, 0))
    sds = jax.ShapeDtypeStruct((rows, cols), F32)
    return _pcall(body, name=name, out_shape=(sds, sds, sds), grid=(rows // tr,), in_specs=[spec] * 4,
                  out_specs=(spec, spec, spec), sem=("parallel",))(w, g, m, v)


def _place():
    return lax.axis_index("x"), lax.axis_index("y"), lax.axis_index("c")


def _other_chips(x, y):
    return [(1 - x, y), (x, 1 - y), (1 - x, 1 - y)]


def _comm_call(body, name, out_shape, n_in, nsem):
    return pl.pallas_call(body, name=name, out_shape=out_shape, in_specs=[ANY] * n_in,
                          out_specs=jax.tree.map(lambda _: ANY, out_shape),
                          scratch_shapes=[pltpu.SemaphoreType.DMA((nsem,)), pltpu.SemaphoreType.DMA((nsem,)),
                                          pltpu.SemaphoreType.DMA])


def all_gather8(block, name):
    def body(x_ref, out_ref, send_sems, recv_sems, local_sem):
        x, y, c = _place()
        me, sibling = (x, y, c), (x, y, 1 - c)
        chips = _other_chips(x, y)

        def slot(px, py, pc):
            return out_ref.at[4 * px + 2 * py + pc]

        def copy(k, blk, to, src=None):
            return pltpu.make_async_remote_copy(src_ref=slot(*blk) if src is None else src, dst_ref=slot(*blk),
                                                send_sem=send_sems.at[k], recv_sem=recv_sems.at[k],
                                                device_id=to, device_id_type=MESH)

        mine = pltpu.make_async_copy(x_ref, slot(*me), local_sem)
        mine.start()
        first = [copy(0, me, sibling, src=x_ref)]
        first += [copy(1 + j, me, (*chip, c), src=x_ref) for j, chip in enumerate(chips)]
        for cp in first:
            cp.start()
        passed = [copy(4 + j, (*chip, c), sibling) for j, chip in enumerate(chips)]
        for j, chip in enumerate(chips):
            copy(1 + j, (*chip, c), me).wait_recv()
            passed[j].start()
        copy(0, sibling, me).wait_recv()
        for j, chip in enumerate(chips):
            copy(4 + j, (*chip, 1 - c), me).wait_recv()
        for cp in first + passed:
            cp.wait_send()
        mine.wait()

    return _comm_call(body, name, jax.ShapeDtypeStruct((8,) + block.shape, block.dtype), 1, 7)(block)


def _remote(src, dst, send_sems, recv_sems, k, to):
    return pltpu.make_async_remote_copy(src_ref=src, dst_ref=dst, send_sem=send_sems.at[k], recv_sem=recv_sems.at[k],
                                        device_id=to, device_id_type=MESH)


def _half(ref, slot, core):
    rh = ref.shape[1] // 2
    return ref.at[slot, pl.ds(core * rh, rh)]


def _all_slots_half(ref, core):
    rh = ref.shape[1] // 2
    return ref.at[:, pl.ds(core * rh, rh)]


def gather_ici(bufs):
    def program(ro, rw, new, ss, rs):
        x, y, c = _place()
        own = 2 * x + y
        starts, arrivals = [], []
        for w, ref in enumerate(rw):
            for j, chip in enumerate(_other_chips(x, y)):
                starts.append(_remote(_half(ref, own, c), _half(ref, own, c), ss, rs, 3 * w + j, (*chip, c)))
                arrivals.append(_remote(_half(ref, own, c), _half(ref, 2 * chip[0] + chip[1], c), ss, rs, 3 * w + j,
                                        (*chip, c)))
        return starts, arrivals

    return CommSpec((), tuple(bufs), (), 3 * len(bufs), program)


def gather_d2d(bufs):
    def program(ro, rw, new, ss, rs):
        x, y, c = _place()
        starts, arrivals = [], []
        for w, ref in enumerate(rw):
            for j, chip in enumerate(_other_chips(x, y)):
                slot = 2 * chip[0] + chip[1]
                starts.append(_remote(_half(ref, slot, c), _half(ref, slot, c), ss, rs, 3 * w + j, (x, y, 1 - c)))
                arrivals.append(_remote(_half(ref, slot, c), _half(ref, slot, 1 - c), ss, rs, 3 * w + j, (x, y, 1 - c)))
        return starts, arrivals

    return CommSpec((), tuple(bufs), (), 3 * len(bufs), program)


def rs_swap(grads):
    def program(ro, rw, new, ss, rs):
        x, y, c = _place()
        copies = [_remote(_all_slots_half(g, 1 - c), new[w], ss, rs, w, (x, y, 1 - c)) for w, g in enumerate(ro)]
        return copies, copies

    shapes = tuple(jax.ShapeDtypeStruct((NCHIP, g.shape[1] // 2, g.shape[2]), g.dtype) for g in grads)
    return CommSpec(tuple(grads), (), shapes, len(grads), program)


def rs_ici(sends, arrs):
    def program(ro, rw, new, ss, rs):
        x, y, c = _place()
        own = 2 * x + y
        starts, arrivals = [], []
        for w, (snd, arr) in enumerate(zip(ro, rw)):
            for j, chip in enumerate(_other_chips(x, y)):
                slot = 2 * chip[0] + chip[1]
                starts.append(_remote(snd.at[slot], _half(arr, own, c), ss, rs, 3 * w + j, (*chip, c)))
                arrivals.append(_remote(snd.at[slot], _half(arr, slot, c), ss, rs, 3 * w + j, (*chip, c)))
        return starts, arrivals

    return CommSpec(tuple(sends), tuple(arrs), (), 3 * len(sends), program)


def rs_d2d(arrs):
    def program(ro, rw, new, ss, rs):
        x, y, c = _place()
        starts = [_remote(_all_slots_half(a, c), _all_slots_half(a, c), ss, rs, w, (x, y, 1 - c)) for w, a in enumerate(rw)]
        arrivals = [_remote(_all_slots_half(a, c), _all_slots_half(a, 1 - c), ss, rs, w, (x, y, 1 - c))
                    for w, a in enumerate(rw)]
        return starts, arrivals

    return CommSpec((), tuple(arrs), (), len(arrs), program)


PACK_COLS = 1024
MATMUL_W = ("w_ada", "w_in", "w_attn_o", "w_conf_out", "w_sc_out", "w_mix_out", "w_ffn_in", "w_ffn_out")
ROW_SPLIT = ("w_mix_out", "w_ffn_out")
CONV_W = ("conf_dw_w", "sc_dw_w")
SMALL = ("c_ctx", "b_ada", "q_norm", "k_norm", "conf_dw_b", "conf_ln_g", "conf_ln_b", "conf_dw_w", "sc_dw_w")


def _pack_rows(arrays, row_multiple):
    flat = jnp.concatenate([a.reshape(-1) for a in arrays])
    rows = -(-flat.shape[0] // PACK_COLS)
    rows = -(-rows // row_multiple) * row_multiple
    flat = jnp.pad(flat, (0, rows * PACK_COLS - flat.shape[0]))
    return flat.reshape(rows, PACK_COLS)


def _unpack(flat2d, shapes):
    flat = flat2d.reshape(-1)
    out, pos = [], 0
    for shp in shapes:
        n = 1
        for s in shp:
            n *= s
        out.append(flat[pos:pos + n].reshape(shp))
        pos += n
    return out


def _cols_joined(stacked_layer):
    nchip, r, c = stacked_layer.shape
    return jnp.transpose(stacked_layer, (1, 0, 2)).reshape(r, nchip * c)


def _cols_split(full):
    r, cols = full.shape
    return jnp.transpose(full.reshape(r, NCHIP, cols // NCHIP), (1, 0, 2))


def _rope_tables():
    rows = SEQ // GRID_W
    r_ids = jnp.repeat(jnp.arange(rows, dtype=F32), GRID_W)
    c_ids = jnp.tile(jnp.arange(GRID_W, dtype=F32), rows)
    freqs = ROPE_THETA ** (-jnp.arange(0, HD // 2, 2, dtype=F32) / (HD // 2))
    ang_r, ang_c = r_ids[:, None] * freqs, c_ids[:, None] * freqs
    cos_h = jnp.concatenate([jnp.cos(ang_r), jnp.cos(ang_r), jnp.cos(ang_c), jnp.cos(ang_c)], axis=1)
    sin_h = jnp.concatenate([-jnp.sin(ang_r), jnp.sin(ang_r), -jnp.sin(ang_c), jnp.sin(ang_c)], axis=1)
    cos_t = jnp.concatenate([jnp.ones((CTX, HD), F32), cos_h], axis=0)
    sin_t = jnp.concatenate([jnp.zeros((CTX, HD), F32), sin_h], axis=0)
    return jnp.tile(cos_t, (1, LANE // HD)), jnp.tile(sin_t, (1, LANE // HD))


def _group_matrix():
    gid = jnp.arange(LANE) // HD
    return jnp.where(gid[:, None] == gid[None, :], 1.0 / HD, 0.0).astype(F32)


N_FIRST = 2
assert MATMUL_W[:N_FIRST] == ("w_ada", "w_in")


def _first_weights(bufs, small, i):
    wi = _cols_joined(bufs[1])
    return dict(
        w_ada=(bufs[0], "cols"), wi_main=(wi[:, :OFF_GATE], "mat"), wi_gate=(wi[:, OFF_GATE:], "mat"),
        conf_dw_w=small["conf_dw_w"][i], sc_dw_w=small["sc_dw_w"][i], conf_dw_b=small["conf_dw_b"][i][None],
        conf_ln_g=small["conf_ln_g"][i][None], conf_ln_b=small["conf_ln_b"][i][None],
        gq=jnp.tile(small["q_norm"][i], LANE // HD)[None], gk=jnp.tile(small["k_norm"][i], LANE // HD)[None])


def _second_weights(bufs):
    b = dict(zip(MATMUL_W[N_FIRST:], bufs))

    def rows_joined(a):
        return a.reshape(a.shape[0] * a.shape[1], a.shape[2])

    return dict(
        w_attn_o=(_cols_joined(b["w_attn_o"]), "mat"), w_conf_out=(_cols_joined(b["w_conf_out"]), "mat"),
        w_sc_out=(_cols_joined(b["w_sc_out"]), "mat"), w_ffn_in=(b["w_ffn_in"], "cols"),
        w_mix_out=(rows_joined(b["w_mix_out"]), "mat"), w_ffn_out=(rows_joined(b["w_ffn_out"]), "mat"))


def _layer_fwd(i, xs, h, mods, w, tabs, second_bufs, next_first, next_layer, distributed):
    cos_t, sin_t, g_mat = tabs
    n = f"l{i}_"
    w = dict(w)
    sv = {"x_in": xs, "mods": mods, "h": h, "w": w}
    n_second = len(second_bufs)
    sv["p_main"] = mm_nn(sv["h"], w["wi_main"], name=n + "p_main")
    sv["q"], sv["k"], sv["v"] = qkv_fwd(sv["p_main"], cos_t, sin_t, g_mat, w["gq"], w["gk"], n + "qkv")
    if distributed:
        riding = list(second_bufs) + list(next_first or [])
        sv["o"], riding, _ = attn_fwd(sv["q"], sv["k"], sv["v"], n + "attn", comm=gather_ici(riding))
        second_bufs, next_first = riding[:n_second], (riding[n_second:] or None)
        (sv["yc"], sv["z"]), second_bufs, _ = conv_fwd(sv["p_main"], w["conf_dw_w"], w["conf_dw_b"], w["sc_dw_w"],
                                                       n + "conv", comm=gather_d2d(second_bufs))
    else:
        sv["o"] = attn_fwd(sv["q"], sv["k"], sv["v"], n + "attn")
        sv["yc"], sv["z"] = conv_fwd(sv["p_main"], w["conf_dw_w"], w["conf_dw_b"], w["sc_dw_w"], n + "conv")
    w.update(_second_weights(second_bufs))
    next_bufs = next_first if distributed else None
    sv["hs"] = ln_silu_fwd(sv["yc"], w["conf_ln_g"], w["conf_ln_b"], n + "ln_silu")
    sv["merged"], sv["gates"], sv["ys"] = gate_mm_fwd(sv["h"], w["wi_gate"][0], sv["o"], sv["hs"], sv["z"],
                                                      w["w_attn_o"][0], w["w_conf_out"][0], w["w_sc_out"][0],
                                                      n + "gate_merge")
    sv["mixed"] = mm_nn(sv["merged"], w["w_mix_out"], name=n + "mix")
    sv["x1"], sv["h2"] = resid_norm_fwd(xs, sv["mixed"], mods, 2, mods, 3, 4, n + "resid1_norm2")
    if next_bufs is None:
        sv["f"], sv["u2"] = ffn_in_swiglu(sv["h2"], w["w_ffn_in"][0], n + "ffn_in")
    else:
        (sv["f"], sv["u2"]), next_bufs, _ = ffn_in_swiglu(sv["h2"], w["w_ffn_in"][0], n + "ffn_in",
                                                          comm=gather_d2d(next_bufs))
    sv["of"] = mm_nn(sv["f"], w["w_ffn_out"], name=n + "ffn_out")
    if next_layer is None:
        return gate_resid_fwd(sv["x1"], sv["of"], mods, 5, n + "resid2"), None, sv
    w_next, mods_next = next_layer(next_bufs)
    x2, h_next = resid_norm_fwd(sv["x1"], sv["of"], mods, 5, mods_next, 0, 1, n + "resid2_norm1")
    return x2, (h_next, w_next, mods_next), sv


def _layer_bwd(i, dx2, dof, dm5, sv, tabs, cs, pending, ids, below):
    cos_t, sin_t, g_mat = tabs
    n = f"l{i}b_"
    mods, w = sv["mods"], sv["w"]
    g = {}
    sends, arrs = [], []
    du = d_f_swiglu(dof, w["w_ffn_out"][0], sv["u2"], n + "d_f")
    g["w_ffn_out"] = mm_tn(sv["f"], dof, out_dtype=BF16, name=n + "dw_ffn_out").reshape(NCHIP, FH // NCHIP, D)
    if pending is None:
        dh2 = mm_nt(du, w["w_ffn_in"], name=n + "d_h2")
    else:
        dh2, _, swapped = mm_nt(du, w["w_ffn_in"], name=n + "d_h2", comm=rs_swap(pending))
        for k, g_, s_ in zip(MATMUL_W[:N_FIRST], pending, swapped):
            send, arr = rs_add(g_, s_, ids[0], ids[1], f"{n}rs_add_above_{k}")
            sends.append(send)
            arrs.append(arr)
    g["w_ffn_in"] = mm_tn(sv["h2"], du, cols=True, out_dtype=BF16, name=n + "dw_ffn_in")
    dx1, dm34, dmixed, dm2 = norm_resid_bwd(sv["x1"], mods, dh2, dx2, 4, sv["mixed"], mods, 2, n + "norm2_resid1")
    dya, dyb, dys, dp_gate = d_merged_gate(dmixed, w["w_mix_out"][0], sv["gates"], sv["ys"], n + "d_merged")
    g["w_mix_out"] = mm_tn(sv["merged"], dmixed, out_dtype=BF16, name=n + "dw_mix").reshape(NCHIP, D // NCHIP, D)
    do = mm_nt(dya, w["w_attn_o"], out_dtype=BF16, name=n + "d_o")
    g["w_attn_o"] = _cols_split(mm_tn(sv["o"], dya, out_dtype=BF16, name=n + "dw_attn_o"))
    dhs = mm_nt(dyb, w["w_conf_out"], name=n + "d_hs")
    g["w_conf_out"] = _cols_split(mm_tn(sv["hs"], dyb, out_dtype=BF16, name=n + "dw_conf_out"))
    dz = mm_nt(dys, w["w_sc_out"], name=n + "d_z")
    g["w_sc_out"] = _cols_split(mm_tn(sv["z"], dys, out_dtype=BF16, name=n + "dw_sc_out"))
    done = None
    if ids is None:
        dyc, g["conf_ln_g"], g["conf_ln_b"] = ln_silu_bwd(sv["yc"], w["conf_ln_g"], w["conf_ln_b"], dhs, n + "ln_silu")
    else:
        own = [g[k] for k in MATMUL_W[N_FIRST:]]
        (dyc, g["conf_ln_g"], g["conf_ln_b"]), _, swapped = ln_silu_bwd(sv["yc"], w["conf_ln_g"], w["conf_ln_b"], dhs,
                                                                        n + "ln_silu", comm=rs_swap(own))
        for k, g_, s_ in zip(MATMUL_W[N_FIRST:], own, swapped):
            send, arr = rs_add(g_, s_, ids[0], ids[1], f"{n}rs_add_{k}")
            sends.append(send)
            arrs.append(arr)
    da, dg, dbg, dcg, dxs, g["conf_dw_w"], g["conf_dw_b"], g["sc_dw_w"] = conv_bwd(
        sv["p_main"], w["conf_dw_w"], w["sc_dw_w"], dyc, dz, n + "conv")
    if ids is None:
        dq, dk, dv = attn_bwd(sv["q"], sv["k"], sv["v"], sv["o"], do, n + "attn")
        dp_qkv, dgqk = qkv_bwd(sv["p_main"], cos_t, sin_t, g_mat, w["gq"], w["gk"], dq, dk, dv, n + "qkv")
    else:
        (dq, dk, dv), arrs, _ = attn_bwd(sv["q"], sv["k"], sv["v"], sv["o"], do, n + "attn", comm=rs_ici(sends, arrs))
        (dp_qkv, dgqk), done, _ = qkv_bwd(sv["p_main"], cos_t, sin_t, g_mat, w["gq"], w["gk"], dq, dk, dv, n + "qkv",
                                          comm=rs_d2d(arrs))
    dp_main = jnp.concatenate([dp_qkv, da, dg, dbg, dcg, dxs], axis=1)
    dh = mm_nt(dp_main, w["wi_main"], name=n + "d_h_main")
    dh = mm_nt(dp_gate, w["wi_gate"], acc=dh, name=n + "d_h_gate")
    g["w_in"] = _cols_split(jnp.concatenate([mm_tn(sv["h"], dp_main, out_dtype=BF16, name=n + "dw_in_main"),
                                             mm_tn(sv["h"], dp_gate, out_dtype=BF16, name=n + "dw_in_gate")], axis=1))
    if below is None:
        dx_in, dm01 = norm_mod_bwd(sv["x_in"], mods, dh, dx1, 1, n + "norm1")
        dof_below = dm5_below = None
    else:
        dx_in, dm01, dof_below, dm5_below = norm_resid_bwd(sv["x_in"], mods, dh, dx1, 1, below["of"], below["mods"], 5,
                                                           n + "norm1_resid2")
    parts = jnp.concatenate([dm01, dm2, dm34, dm5], axis=2).reshape(2 * BL, NMOD * D)
    dmod, g["b_ada"] = dmod_assemble(parts, n + "dmod")
    g["w_ada"] = mm_tn(cs, dmod, cols=True, out_dtype=BF16, name=n + "dw_ada")
    g["dcs"] = mm_nt(dmod, w["w_ada"], name=n + "d_cs")
    g["q_norm"] = dgqk[0, :AW].reshape(NQ, HD).sum(axis=0)
    g["k_norm"] = dgqk[0, OFF_K:OFF_K + KVW].reshape(NKV, HD).sum(axis=0)
    return dx_in, dof_below, dm5_below, g, done


def kernel(x, c, ctx, c_ctx, w_ada, b_ada, w_in, q_norm, k_norm, w_attn_o, conf_dw_w, conf_dw_b, conf_ln_g, conf_ln_b, w_conf_out, sc_dw_w, w_sc_out, w_mix_out, w_ffn_in, w_ffn_out, loss_target, m_c_ctx, m_w_ada, m_b_ada, m_w_in, m_q_norm, m_k_norm, m_w_attn_o, m_conf_dw_w, m_conf_dw_b, m_conf_ln_g, m_conf_ln_b, m_w_conf_out, m_sc_dw_w, m_w_sc_out, m_w_mix_out, m_w_ffn_in, m_w_ffn_out, v_c_ctx, v_w_ada, v_b_ada, v_w_in, v_q_norm, v_k_norm, v_w_attn_o, v_conf_dw_w, v_conf_dw_b, v_conf_ln_g, v_conf_ln_b, v_w_conf_out, v_sc_dw_w, v_w_sc_out, v_w_mix_out, v_w_ffn_in, v_w_ffn_out):
    local = dict(c_ctx=c_ctx, w_ada=w_ada, b_ada=b_ada, w_in=w_in, q_norm=q_norm, k_norm=k_norm, w_attn_o=w_attn_o,
                 conf_dw_w=conf_dw_w, conf_dw_b=conf_dw_b, conf_ln_g=conf_ln_g, conf_ln_b=conf_ln_b,
                 w_conf_out=w_conf_out, sc_dw_w=sc_dw_w, w_sc_out=w_sc_out, w_mix_out=w_mix_out, w_ffn_in=w_ffn_in,
                 w_ffn_out=w_ffn_out)
    mom_m = dict(c_ctx=m_c_ctx, w_ada=m_w_ada, b_ada=m_b_ada, w_in=m_w_in, q_norm=m_q_norm, k_norm=m_k_norm,
                 w_attn_o=m_w_attn_o, conf_dw_w=m_conf_dw_w, conf_dw_b=m_conf_dw_b, conf_ln_g=m_conf_ln_g,
                 conf_ln_b=m_conf_ln_b, w_conf_out=m_w_conf_out, sc_dw_w=m_sc_dw_w, w_sc_out=m_w_sc_out,
                 w_mix_out=m_w_mix_out, w_ffn_in=m_w_ffn_in, w_ffn_out=m_w_ffn_out)
    mom_v = dict(c_ctx=v_c_ctx, w_ada=v_w_ada, b_ada=v_b_ada, w_in=v_w_in, q_norm=v_q_norm, k_norm=v_k_norm,
                 w_attn_o=v_w_attn_o, conf_dw_w=v_conf_dw_w, conf_dw_b=v_conf_dw_b, conf_ln_g=v_conf_ln_g,
                 conf_ln_b=v_conf_ln_b, w_conf_out=v_w_conf_out, sc_dw_w=v_sc_dw_w, w_sc_out=v_w_sc_out,
                 w_mix_out=v_w_mix_out, w_ffn_in=v_w_ffn_in, w_ffn_out=v_w_ffn_out)
    order = ("c_ctx", "w_ada", "b_ada", "w_in", "q_norm", "k_norm", "w_attn_o", "conf_dw_w", "conf_dw_b", "conf_ln_g",
             "conf_ln_b", "w_conf_out", "sc_dw_w", "w_sc_out", "w_mix_out", "w_ffn_in", "w_ffn_out")
    core = lax.axis_index("c").astype(jnp.int32)
    chip = (2 * lax.axis_index("x") + lax.axis_index("y")).astype(jnp.int32)

    own = [cast_layers(local[k], chip.reshape(1), "cast_" + k) for k in MATMUL_W]
    layer_bufs = [[own[w][l] for w in range(len(MATMUL_W))] for l in range(DEPTH)]
    conv_shapes = [local[k].shape for k in CONV_W]
    conv_all = all_gather8(_pack_rows([local[k] for k in CONV_W], 8), "gather_conv_taps")
    per_chip = [_unpack(conv_all[2 * s], conv_shapes) for s in range(NCHIP)]
    small = dict(b_ada=b_ada, q_norm=q_norm, k_norm=k_norm, conf_dw_b=conf_dw_b, conf_ln_g=conf_ln_g, conf_ln_b=conf_ln_b)
    for i, k in enumerate(CONV_W):
        small[k] = jnp.concatenate([per_chip[s][i] for s in range(NCHIP)], axis=2)

    loss_local, grad_x, sums, small_g = local_step(x, c, ctx, c_ctx, layer_bufs, small, loss_target,
                                                   ids=(core.reshape(1), chip.reshape(1)))
    loss = lax.psum(loss_local, ("x", "y", "c"))

    small_shapes = [small_g[k].shape for k in SMALL]
    small_sum = sum_leading(all_gather8(_pack_rows([small_g[k] for k in SMALL], 8), "gather_small_grads"), "small_sum")
    small_g = dict(zip(SMALL, _unpack(small_sum, small_shapes)))
    for k in CONV_W:
        width = local[k].shape[2]
        small_g[k] = lax.dynamic_slice_in_dim(small_g[k], chip * width, width, axis=2)

    grad, delta, new_m, new_v = {}, {}, {}, {}
    for wi, k in enumerate(MATMUL_W):
        outs = [lax.empty(local[k].shape, F32) for _ in range(4)]
        grad[k], delta[k], new_m[k], new_v[k] = adamw_layers(local[k], [sums[l][wi] for l in range(DEPTH)], mom_m[k],
                                                             mom_v[k], 0, outs, "adamw_" + k)
    for k in order:
        if k in MATMUL_W:
            continue
        shp = local[k].shape
        view = (1, shp[0]) if len(shp) == 1 else (-1, shp[-1])
        d_, m_, v_ = adamw(local[k].reshape(view), small_g[k].reshape(view), mom_m[k].reshape(view),
                           mom_v[k].reshape(view), "adamw_" + k)
        grad[k], delta[k], new_m[k], new_v[k] = small_g[k], d_.reshape(shp), m_.reshape(shp), v_.reshape(shp)
    return (loss, grad_x, *[grad[k] for k in order], *[delta[k] for k in order], *[new_m[k] for k in order],
            *[new_v[k] for k in order])


def local_step(x, c, ctx, c_ctx, layer_bufs, small, loss_target, ids=None):
    tabs = _rope_tables() + (_group_matrix(),)
    distributed = ids is not None
    first_bufs = [list(b[:N_FIRST]) for b in layer_bufs]
    second_bufs = [list(b[N_FIRST:]) for b in layer_bufs]
    if distributed:
        first_bufs[0], _ = comm_only("gather0_ici", gather_ici(first_bufs[0]))
        first_bufs[0], _ = comm_only("gather0_d2d", gather_d2d(first_bufs[0]))

    cin = jnp.concatenate([c, c_ctx[None], jnp.zeros((8 - BL - 1, D), F32)], axis=0)
    cs = silu_rows(cin, "silu_c")
    xs = jnp.concatenate([ctx, x], axis=1).reshape(NROW, D)
    saved = []

    def make_layer(i, bufs):
        w = _first_weights(bufs, small, i)
        return w, mm_nn(cs, w["w_ada"], bias=small["b_ada"][i][None], name=f"l{i}_mod").reshape(8, 1, NMOD * D)

    w, mods = make_layer(0, first_bufs[0])
    h = norm_mod_fwd(xs, mods, 0, 1, "l0_norm1")
    for i in range(DEPTH):
        last = i == DEPTH - 1

        def next_layer(bufs, i=i):
            return make_layer(i + 1, first_bufs[i + 1] if bufs is None else bufs)

        xs, following, sv = _layer_fwd(i, xs, h, mods, w, tabs, second_bufs[i], None if last else first_bufs[i + 1],
                                       None if last else next_layer, distributed)
        saved.append(sv)
        if following is not None:
            h, w, mods = following
    dxs, loss_lanes = loss_fwd_bwd(xs, loss_target.reshape(BL * SEQ, D), "loss")
    loss_local = 0.5 * jnp.sum(loss_lanes) / D

    grads = [None] * DEPTH
    sums = [[None] * len(MATMUL_W) for _ in range(DEPTH)]
    pending = None
    dof, dm5 = gate_resid_bwd(dxs, saved[-1]["of"], saved[-1]["mods"], 5, "top_resid2")
    for i in reversed(range(DEPTH)):
        dxs, dof, dm5, grads[i], done = _layer_bwd(i, dxs, dof, dm5, saved[i], tabs, cs, pending, ids,
                                                   saved[i - 1] if i > 0 else None)
        partial = [grads[i][k] for k in MATMUL_W]
        if distributed:
            if pending is not None:
                sums[i + 1][:N_FIRST] = done[:N_FIRST]
                done = done[N_FIRST:]
            sums[i][N_FIRST:] = done
            pending = partial[:N_FIRST]
        else:
            sums[i] = partial
    if distributed:
        names = MATMUL_W[:N_FIRST]
        _, swapped = comm_only("rs0_swap", rs_swap(pending))
        sends, arrs = zip(*[rs_add(g_, s_, ids[0], ids[1], "rs0_add_" + k) for k, g_, s_ in zip(names, pending, swapped)])
        arrs, _ = comm_only("rs0_ici", rs_ici(sends, arrs))
        sums[0][:N_FIRST], _ = comm_only("rs0_d2d", rs_d2d(arrs))
    grad_x = dxs.reshape(BL, RE, D)[:, CTX:, :]
    dcin = silu_rows_bwd(cin, jnp.stack([grads[i]["dcs"] for i in range(DEPTH)]), "silu_c_bwd")

    def stack(key):
        return jnp.stack([grads[i][key] for i in range(DEPTH)])

    small_g = dict(c_ctx=dcin[BL], b_ada=stack("b_ada").reshape(DEPTH, NMOD * D), q_norm=stack("q_norm"),
                   k_norm=stack("k_norm"), conf_dw_b=stack("conf_dw_b").reshape(DEPTH, CW),
                   conf_ln_g=stack("conf_ln_g").reshape(DEPTH, CW), conf_ln_b=stack("conf_ln_b").reshape(DEPTH, CW),
                   conf_dw_w=stack("conf_dw_w"), sc_dw_w=stack("sc_dw_w"))
    return loss_local, grad_x, sums, small_g
```

```python
import functools
from typing import Any, Callable, NamedTuple, Sequence

import jax
import jax.numpy as jnp
from jax import lax
from jax.experimental import pallas as pl
from jax.experimental.pallas import tpu as pltpu

F32, BF16 = jnp.float32, jnp.bfloat16
HIGHEST = lax.Precision.HIGHEST

D = 1024
SEQ = 2048
CTX = 256
DEPTH = 4
BL = 4
GRID_W = 64
HD = 64
NQ = 8
NKV = 2
AW = NQ * HD
KVW = NKV * HD
CW = D // 2
CONF_K = 31
SC_K = 3
NMOD = 6
FH = -(-8 * D // (3 * 256)) * 256
EPS = 1e-6
ROPE_THETA = 10000.0
ATTN_SCALE = HD ** -0.5
OFF_K = AW
OFF_V = OFF_K + KVW
OFF_CONF = OFF_V + KVW
OFF_SC = OFF_CONF + 2 * CW
OFF_GATE = OFF_SC + 3 * CW
IN_W = OFF_GATE + 3 * D
QKVW = OFF_CONF
NCHIP = 4

ADAM_LR, ADAM_B1, ADAM_B2, ADAM_EPS, ADAM_WD, ADAM_STEP = 0.001, 0.9, 0.999, 1e-08, 0.01, 10

TM = CTX
RE = CTX + SEQ
TPE = RE // TM
NROW = BL * RE
NT = NROW // TM
LANE = 128
CB = CW // LANE
CONV_CH = 128
PADR = 16
VMEM_LIMIT = 52 * 1024 * 1024

MESH = pl.DeviceIdType.MESH
ANY = pl.BlockSpec(memory_space=pl.ANY)


class CommSpec(NamedTuple):
    ro: Sequence[Any]
    rw: Sequence[Any]
    new: Sequence[Any]
    nsem: int
    program: Callable


def _pcall(body, *, name, out_shape, grid=(), in_specs=None, out_specs=None, scratch=(), sem=None, comm=None,
           aliases=None):
    aliases = dict(aliases or {})
    if not grid:
        return pl.pallas_call(body, name=name, out_shape=out_shape)
    if comm is None:
        params = pltpu.CompilerParams(dimension_semantics=sem, vmem_limit_bytes=VMEM_LIMIT)
        return pl.pallas_call(body, name=name, out_shape=out_shape, grid=grid, in_specs=in_specs, out_specs=out_specs,
                              scratch_shapes=list(scratch), input_output_aliases=aliases, compiler_params=params)

    single = not isinstance(out_shape, (tuple, list))
    out_shapes = (out_shape,) if single else tuple(out_shape)
    out_specs_t = (out_specs,) if single else tuple(out_specs)
    n_in, n_out, n_scr = len(in_specs), len(out_shapes), len(scratch)
    n_ro, n_rw, n_new = len(comm.ro), len(comm.rw), len(comm.new)

    def carrier(*refs):
        ins = refs[:n_in]
        ro_refs = refs[n_in:n_in + n_ro]
        o0 = n_in + n_ro + n_rw
        outs = refs[o0:o0 + n_out]
        rw_refs = refs[o0 + n_out:o0 + n_out + n_rw]
        new_refs = refs[o0 + n_out + n_rw:o0 + n_out + n_rw + n_new]
        s0 = o0 + n_out + n_rw + n_new
        scr = refs[s0:s0 + n_scr]
        send_sems, recv_sems = refs[s0 + n_scr:]
        first = functools.reduce(jnp.logical_and, [pl.program_id(a) == 0 for a in range(len(grid))])
        last = functools.reduce(jnp.logical_and, [pl.program_id(a) == grid[a] - 1 for a in range(len(grid))])
        starts, arrivals = comm.program(ro_refs, rw_refs, new_refs, send_sems, recv_sems)

        @pl.when(first)
        def _():
            for cp in starts:
                cp.start()

        body(*ins, *outs, *scr)

        @pl.when(last)
        def _():
            for cp in arrivals:
                cp.wait_recv()
            for cp in starts:
                cp.wait_send()

    def call(*args):
        rw_shapes = tuple(jax.ShapeDtypeStruct(a.shape, a.dtype) for a in comm.rw)
        res = pl.pallas_call(
            carrier, name=name, out_shape=out_shapes + rw_shapes + tuple(comm.new), grid=grid,
            in_specs=list(in_specs) + [ANY] * (n_ro + n_rw),
            out_specs=out_specs_t + (ANY,) * (n_rw + n_new),
            scratch_shapes=list(scratch) + [pltpu.SemaphoreType.DMA((comm.nsem,)), pltpu.SemaphoreType.DMA((comm.nsem,))],
            input_output_aliases={**aliases, **{n_in + n_ro + i: n_out + i for i in range(n_rw)}},
            compiler_params=pltpu.CompilerParams(dimension_semantics=("arbitrary",) * len(grid),
                                                 vmem_limit_bytes=VMEM_LIMIT))(*args, *comm.ro, *comm.rw)
        compute = res[0] if single else tuple(res[:n_out])
        return compute, list(res[n_out:n_out + n_rw]), list(res[n_out + n_rw:])

    return call


def comm_only(name, comm):
    n_ro, n_rw, n_new = len(comm.ro), len(comm.rw), len(comm.new)

    def body(*refs):
        ro_refs = refs[:n_ro]
        rw_refs = refs[n_ro + n_rw:n_ro + 2 * n_rw]
        new_refs = refs[n_ro + 2 * n_rw:n_ro + 2 * n_rw + n_new]
        send_sems, recv_sems = refs[n_ro + 2 * n_rw + n_new:]
        starts, arrivals = comm.program(ro_refs, rw_refs, new_refs, send_sems, recv_sems)
        for cp in starts:
            cp.start()
        for cp in arrivals:
            cp.wait_recv()
        for cp in starts:
            cp.wait_send()

    rw_shapes = tuple(jax.ShapeDtypeStruct(a.shape, a.dtype) for a in comm.rw)
    res = pl.pallas_call(body, name=name, out_shape=rw_shapes + tuple(comm.new), in_specs=[ANY] * (n_ro + n_rw),
                         out_specs=(ANY,) * (n_rw + n_new), input_output_aliases={n_ro + i: i for i in range(n_rw)},
                         scratch_shapes=[pltpu.SemaphoreType.DMA((comm.nsem,)), pltpu.SemaphoreType.DMA((comm.nsem,))])(
                             *comm.ro, *comm.rw)
    return list(res[:n_rw]), list(res[n_rw:])


def _pick(n, cands):
    for t in cands:
        if n % t == 0:
            return t
    return n


def _seg(t):
    return jnp.where(t % TPE == 0, BL, t // TPE)


def _slot(t):
    return 2 * (t // TPE) + jnp.where(t % TPE == 0, 0, 1)


def _sigmoid(x):
    return 1.0 / (1.0 + jnp.exp(-x))


MM_BUDGET = 40 * 1024 * 1024
N_TILE_CAP = 1664


def _tile(n, cap=N_TILE_CAP):
    if n <= cap:
        return n
    for t in range(cap - cap % LANE, 0, -LANE):
        if n % t == 0:
            return t
    return n


def _row_tile(m, bytes_of):
    for tm in (1024, 512, 256, 128):
        if m % tm == 0 and bytes_of(tm) <= MM_BUDGET:
            return tm
    return m


def _w_dims(w):
    arr, kind = w
    if kind == "cols":
        return arr.shape[1], NCHIP * arr.shape[2]
    return arr.shape


def _sz(dtype):
    return jnp.dtype(dtype).itemsize


def mm_nn(a, w, *, bias=None, out_dtype=F32, name):
    arr, kind = w
    m, k = a.shape
    _, n = _w_dims(w)
    tn = _tile(arr.shape[2]) if kind == "cols" else _tile(n)
    tm = _row_tile(m, lambda t: 2 * (t * k * _sz(a.dtype) + k * tn * 2 + t * tn * _sz(out_dtype)))
    if kind == "mat":
        b_spec = pl.BlockSpec((k, tn), lambda j, i: (0, j))
    else:
        per = arr.shape[2] // tn
        b_spec = pl.BlockSpec((None, k, tn), lambda j, i: (j // per, 0, j % per))
    has_bias = bias is not None

    def body(*refs):
        out = jnp.dot(refs[0][...].astype(BF16), refs[1][...].astype(BF16), preferred_element_type=F32)
        if has_bias:
            out = out + refs[2][...]
        refs[-1][...] = out.astype(out_dtype)

    in_specs = [pl.BlockSpec((tm, k), lambda j, i: (i, 0)), b_spec]
    args = [a, arr]
    if has_bias:
        in_specs.append(pl.BlockSpec((1, tn), lambda j, i: (0, j)))
        args.append(bias)
    return _pcall(body, name=name, out_shape=jax.ShapeDtypeStruct((m, n), out_dtype), grid=(n // tn, m // tm),
                  in_specs=in_specs, out_specs=pl.BlockSpec((tm, tn), lambda j, i: (i, j)),
                  sem=("parallel", "parallel"))(*args)


def mm_nt(a, w, *, acc=None, out_dtype=F32, name, comm=None):
    arr, kind = w
    kdim, _ = _w_dims(w)
    has_acc = acc is not None
    tk = _tile(kdim, 1408)
    if kind == "cols":
        c = arr.shape[2]
        m = a.shape[-2]
        if a.ndim == 3:
            a_spec = lambda t: pl.BlockSpec((None, t, c), lambda j, i, s: (s // 2, i, s % 2))
        else:
            a_spec = lambda t: pl.BlockSpec((t, c), lambda j, i, s: (i, s))
        tm = _row_tile(m, lambda t: 2 * (t * c * _sz(a.dtype) + tk * c * 2 + t * tk * _sz(out_dtype)) + t * tk * 4)

        def body(a_ref, b_ref, o_ref, acc_ref):
            s = pl.program_id(2)

            @pl.when(s == 0)
            def _():
                acc_ref[...] = jnp.zeros_like(acc_ref)

            acc_ref[...] += lax.dot_general(a_ref[...].astype(BF16), b_ref[...], (((1,), (1,)), ((), ())),
                                            preferred_element_type=F32)

            @pl.when(s == NCHIP - 1)
            def _():
                o_ref[...] = acc_ref[...].astype(out_dtype)

        return _pcall(body, name=name, out_shape=jax.ShapeDtypeStruct((m, kdim), out_dtype),
                      grid=(kdim // tk, m // tm, NCHIP),
                      in_specs=[a_spec(tm), pl.BlockSpec((None, tk, c), lambda j, i, s: (s, j, 0))],
                      out_specs=pl.BlockSpec((tm, tk), lambda j, i, s: (i, j)),
                      scratch=[pltpu.VMEM((tm, tk), F32)],
                      sem=("parallel", "parallel", "arbitrary"), comm=comm)(a, arr)

    m, n = a.shape
    tm = _row_tile(m, lambda t: 2 * (t * n * _sz(a.dtype) + tk * n * 2 + t * tk * (_sz(out_dtype) + 4 * has_acc))
                   + tk * n * 2)
    b_spec = pl.BlockSpec((tk, n), lambda j, i: (j, 0))

    def body(*refs):
        wt_ref = refs[-1]

        @pl.when(pl.program_id(1) == 0)
        def _():
            wt_ref[...] = refs[1][...].astype(BF16).T

        out = jnp.dot(refs[0][...].astype(BF16), wt_ref[...], preferred_element_type=F32)
        if has_acc:
            out = out + refs[2][...]
        refs[-2][...] = out.astype(out_dtype)

    in_specs = [pl.BlockSpec((tm, n), lambda j, i: (i, 0)), b_spec]
    args = [a, arr]
    if has_acc:
        in_specs.append(pl.BlockSpec((tm, tk), lambda j, i: (i, j)))
        args.append(acc)
    return _pcall(body, name=name, out_shape=jax.ShapeDtypeStruct((m, kdim), out_dtype), grid=(kdim // tk, m // tm),
                  in_specs=in_specs, out_specs=pl.BlockSpec((tm, tk), lambda j, i: (i, j)),
                  scratch=[pltpu.VMEM((n, tk), BF16)], sem=("parallel", "arbitrary"))(*args)


def mm_tn(a, b, *, cols=False, out_dtype=F32, name):
    rows, k = a.shape
    halves = b.ndim == 3
    n = 2 * b.shape[2] if halves else b.shape[1]
    odt = out_dtype
    if cols:
        c = n // NCHIP
        tn, tk = _tile(c), k
        per = c // tn
        out_spec = pl.BlockSpec((None, tk, tn), lambda i, j, r: (j // per, 0, j % per))
        out_shape = jax.ShapeDtypeStruct((NCHIP, k, c), odt)
    else:
        tn, tk = _tile(n), _tile(k, 1408)
        out_spec = pl.BlockSpec((tk, tn), lambda i, j, r: (i, j))
        out_shape = jax.ShapeDtypeStruct((k, n), odt)
    tr = _row_tile(rows, lambda t: 2 * (t * tk * _sz(a.dtype) + t * tn * _sz(b.dtype) + tk * tn * _sz(odt)) + tk * tn * 4)
    nsteps = rows // tr

    def body(*refs):
        a_ref, b_ref = refs[0], refs[1]
        o_ref, acc_ref = refs[-2], refs[-1]
        r = pl.program_id(2)

        @pl.when(r == 0)
        def _():
            acc_ref[...] = jnp.zeros_like(acc_ref)

        acc_ref[...] += lax.dot_general(a_ref[...].astype(BF16), b_ref[...].astype(BF16), (((0,), (0,)), ((), ())),
                                        preferred_element_type=F32)

        @pl.when(r == nsteps - 1)
        def _():
            o_ref[...] = acc_ref[...].astype(odt)

    if halves:
        per_half = (n // 2) // tn
        b_spec = pl.BlockSpec((None, tr, tn), lambda i, j, r: (j // per_half, r, j % per_half))
    else:
        b_spec = pl.BlockSpec((tr, tn), lambda i, j, r: (r, j))
    return _pcall(body, name=name, out_shape=out_shape, grid=(k // tk, n // tn, nsteps),
                  in_specs=[pl.BlockSpec((tr, tk), lambda i, j, r: (r, i)), b_spec], out_specs=out_spec,
                  scratch=[pltpu.VMEM((tk, tn), F32)], sem=("parallel", "parallel", "arbitrary"))(a, b)


def ffn_in_swiglu(h, w_in, name, comm=None):
    m, k = h.shape
    c = w_in.shape[2]
    tm = 512

    def body(h_ref, wa_ref, wb_ref, f_ref, u_ref):
        for r0 in range(0, tm, tm // 2):
            rows = slice(r0, r0 + tm // 2)
            hv = h_ref[rows, :]
            a = jnp.dot(hv, wa_ref[...], preferred_element_type=F32)
            b = jnp.dot(hv, wb_ref[...], preferred_element_type=F32)
            f_ref[rows, :] = (a * _sigmoid(a) * b).astype(BF16)
            u_ref[0, rows, :] = a.astype(BF16)
            u_ref[1, rows, :] = b.astype(BF16)

    return _pcall(body, name=name,
                  out_shape=(jax.ShapeDtypeStruct((m, FH), BF16), jax.ShapeDtypeStruct((2, m, FH), BF16)),
                  grid=(2, m // tm),
                  in_specs=[pl.BlockSpec((tm, k), lambda j, i: (i, 0)),
                            pl.BlockSpec((None, k, c), lambda j, i: (j, 0, 0)),
                            pl.BlockSpec((None, k, c), lambda j, i: (2 + j, 0, 0))],
                  out_specs=(pl.BlockSpec((tm, c), lambda j, i: (i, j)), pl.BlockSpec((2, tm, c), lambda j, i: (0, i, j))),
                  sem=("parallel", "parallel"), comm=comm)(h, w_in, w_in)


def d_f_swiglu(dof, w_out, u2, name):
    m, k = dof.shape
    c = FH // 2
    tm = 512

    n_i = m // tm
    total = 2 * n_i
    ring = 3

    def body(d_ref, w_ref, u_hbm, du_ref, wt_ref, ubuf, usem):
        j, i = pl.program_id(0), pl.program_id(1)
        s = j * n_i + i

        def fetch(step, half):
            row0 = pl.multiple_of((step % n_i) * tm, tm)
            return pltpu.make_async_copy(u_hbm.at[:, pl.ds(row0, tm), pl.ds(half * c, c)], ubuf.at[step % ring],
                                         usem.at[step % ring])

        def start(step):
            for half in range(2):
                @pl.when(step // n_i == half)
                def _():
                    fetch(step, half).start()

        @pl.when(s == 0)
        def _():
            start(s)
            start(s + 1)

        @pl.when(s + 2 < total)
        def _():
            start(s + 2)

        @pl.when(i == 0)
        def _():
            wt_ref[...] = w_ref[...].T

        fetch(s, 0).wait()
        slot = s % ring
        for r0 in range(0, tm, tm // 2):
            rows = slice(r0, r0 + tm // 2)
            df = jnp.dot(d_ref[rows, :], wt_ref[...], preferred_element_type=F32)
            a, b = ubuf[slot, 0, rows, :].astype(F32), ubuf[slot, 1, rows, :].astype(F32)
            sg = _sigmoid(a)
            du_ref[0, rows, :] = (df * b * (sg * (1.0 + a * (1.0 - sg)))).astype(BF16)
            du_ref[1, rows, :] = (df * a * sg).astype(BF16)

    ublk = pl.BlockSpec((2, tm, c), lambda j, i: (0, i, j))
    return _pcall(body, name=name, out_shape=jax.ShapeDtypeStruct((2, m, FH), BF16), grid=(2, n_i),
                  in_specs=[pl.BlockSpec((tm, k), lambda j, i: (i, 0)), pl.BlockSpec((c, k), lambda j, i: (j, 0)), ANY],
                  out_specs=ublk,
                  scratch=[pltpu.VMEM((k, c), w_out.dtype), pltpu.VMEM((ring, 2, tm, c), u2.dtype),
                           pltpu.SemaphoreType.DMA((ring,))],
                  sem=("arbitrary", "arbitrary"))(dof, w_out, u2)


GATE_TN = 512


def gate_mm_fwd(h, wi_gate, o, hs, z, wo, wc, ws, name):
    m, k = h.shape
    tm, tn = 512, min(GATE_TN, D)
    nj = D // tn

    def body(h_ref, g0_ref, g1_ref, g2_ref, o_ref, hs_ref, z_ref, wo_ref, wc_ref, ws_ref, m_ref, g_ref, y_ref):
        hv = h_ref[...]
        acc = jnp.zeros((tm, tn), F32)
        for g, (gw_ref, x_ref, w_ref) in enumerate(((g0_ref, o_ref, wo_ref), (g1_ref, hs_ref, wc_ref),
                                                    (g2_ref, z_ref, ws_ref))):
            gate = _sigmoid(jnp.dot(hv, gw_ref[...], preferred_element_type=F32))
            y = jnp.dot(x_ref[...], w_ref[...], preferred_element_type=F32)
            acc += gate * y
            g_ref[g] = gate.astype(BF16)
            y_ref[g] = y.astype(BF16)
        m_ref[...] = acc.astype(BF16)

    def gate_w(g):
        return pl.BlockSpec((k, tn), lambda j, i: (0, g * nj + j))

    def branch(width):
        return pl.BlockSpec((tm, width), lambda j, i: (i, 0))

    def branch_w(width):
        return pl.BlockSpec((width, tn), lambda j, i: (0, j))

    stacked = pl.BlockSpec((3, tm, tn), lambda j, i: (0, i, j))
    sds3 = jax.ShapeDtypeStruct((3, m, D), BF16)
    return _pcall(body, name=name, out_shape=(jax.ShapeDtypeStruct((m, D), BF16), sds3, sds3), grid=(nj, m // tm),
                  in_specs=[pl.BlockSpec((tm, k), lambda j, i: (i, 0)), gate_w(0), gate_w(1), gate_w(2),
                            branch(o.shape[1]), branch(hs.shape[1]), branch(z.shape[1]),
                            branch_w(wo.shape[0]), branch_w(wc.shape[0]), branch_w(ws.shape[0])],
                  out_specs=(pl.BlockSpec((tm, tn), lambda j, i: (i, j)), stacked, stacked),
                  sem=("parallel", "parallel"))(h, wi_gate, wi_gate, wi_gate, o, hs, z, wo, wc, ws)


def d_merged_gate(dmixed, w_mix, gates, ys, name):
    m, k = dmixed.shape
    tm = 256

    def body(d_ref, w_ref, g_ref, y_ref, da_ref, db_ref, ds_ref, dp_ref, wt_ref):
        @pl.when(pl.program_id(0) == 0)
        def _():
            wt_ref[...] = w_ref[...].T

        dm = jnp.dot(d_ref[...], wt_ref[...], preferred_element_type=F32)
        for g, dy_ref in enumerate((da_ref, db_ref, ds_ref)):
            gate = g_ref[g].astype(F32)
            dy_ref[...] = (dm * gate).astype(BF16)
            dp_ref[:, g * D:(g + 1) * D] = (dm * y_ref[g].astype(F32) * gate * (1.0 - gate)).astype(BF16)

    stacked = pl.BlockSpec((3, tm, D), lambda i: (0, i, 0))
    row = pl.BlockSpec((tm, D), lambda i: (i, 0))
    sds = jax.ShapeDtypeStruct((m, D), BF16)
    return _pcall(body, name=name, out_shape=(sds, sds, sds, jax.ShapeDtypeStruct((m, 3 * D), BF16)), grid=(m // tm,),
                  in_specs=[pl.BlockSpec((tm, k), lambda i: (i, 0)), pl.BlockSpec((D, k), lambda i: (0, 0)),
                            stacked, stacked],
                  out_specs=(row, row, row, pl.BlockSpec((tm, 3 * D), lambda i: (i, 0))),
                  scratch=[pltpu.VMEM((k, D), w_mix.dtype)], sem=("arbitrary",))(dmixed, w_mix, gates, ys)


def _mods_spec():
    return pl.BlockSpec((1, 1, NMOD * D), lambda t: (_seg(t), 0, 0))


def _rows(width):
    return pl.BlockSpec((TM, width), lambda t: (t, 0))


def norm_mod_fwd(x, mods, k_sh, k_sc, name):
    def body(x_ref, m_ref, h_ref):
        x = x_ref[...]
        r = lax.rsqrt(jnp.mean(x * x, axis=-1, keepdims=True) + EPS)
        sh = m_ref[0, :, k_sh * D:(k_sh + 1) * D]
        sc = m_ref[0, :, k_sc * D:(k_sc + 1) * D]
        h_ref[...] = (x * r * (1.0 + sc) + sh).astype(BF16)

    return _pcall(body, name=name, out_shape=jax.ShapeDtypeStruct((NROW, D), BF16), grid=(NT,),
                  in_specs=[_rows(D), _mods_spec()], out_specs=_rows(D), sem=("parallel",))(x, mods)


def _accumulate_slot(t, ref, part):
    first = (t % TPE) <= 1

    @pl.when(first)
    def _():
        ref[0] = part

    @pl.when(jnp.logical_not(first))
    def _():
        ref[0] += part


def norm_mod_bwd(x, mods, dh, dres, k_sc, name):
    def body(x_ref, m_ref, dh_ref, dres_ref, dx_ref, dp_ref):
        t = pl.program_id(0)
        x = x_ref[...]
        r = lax.rsqrt(jnp.mean(x * x, axis=-1, keepdims=True) + EPS)
        xn = x * r
        sc = m_ref[0, :, k_sc * D:(k_sc + 1) * D]
        dh = dh_ref[...]
        dxn = dh * (1.0 + sc)
        dx_ref[...] = r * (dxn - xn * jnp.mean(dxn * xn, axis=-1, keepdims=True)) + dres_ref[...]
        part = jnp.concatenate([jnp.sum(dh, axis=0, keepdims=True), jnp.sum(dh * xn, axis=0, keepdims=True)], axis=1)
        _accumulate_slot(t, dp_ref, part)

    return _pcall(body, name=name,
                  out_shape=(jax.ShapeDtypeStruct((NROW, D), F32), jax.ShapeDtypeStruct((2 * BL, 1, 2 * D), F32)),
                  grid=(NT,), in_specs=[_rows(D), _mods_spec(), _rows(D), _rows(D)],
                  out_specs=(_rows(D), pl.BlockSpec((1, 1, 2 * D), lambda t: (_slot(t), 0, 0))),
                  sem=("arbitrary",))(x, mods, dh, dres)


def resid_norm_fwd(x, y, mods_g, k_g, mods_n, k_sh, k_sc, name):
    def body(x_ref, y_ref, mg_ref, mn_ref, x1_ref, h_ref):
        x1 = x_ref[...] + mg_ref[0, :, k_g * D:(k_g + 1) * D] * y_ref[...]
        x1_ref[...] = x1
        r = lax.rsqrt(jnp.mean(x1 * x1, axis=-1, keepdims=True) + EPS)
        sh = mn_ref[0, :, k_sh * D:(k_sh + 1) * D]
        sc = mn_ref[0, :, k_sc * D:(k_sc + 1) * D]
        h_ref[...] = (x1 * r * (1.0 + sc) + sh).astype(BF16)

    return _pcall(body, name=name,
                  out_shape=(jax.ShapeDtypeStruct((NROW, D), F32), jax.ShapeDtypeStruct((NROW, D), BF16)), grid=(NT,),
                  in_specs=[_rows(D), _rows(D), _mods_spec(), _mods_spec()], out_specs=(_rows(D), _rows(D)),
                  sem=("parallel",))(x, y, mods_g, mods_n)


def norm_resid_bwd(x, mods_n, dh, dres, k_sc, y, mods_g, k_g, name):
    def body(x_ref, mn_ref, dh_ref, dres_ref, y_ref, mg_ref, dx_ref, dpn_ref, dy_ref, dpg_ref):
        t = pl.program_id(0)
        x = x_ref[...]
        r = lax.rsqrt(jnp.mean(x * x, axis=-1, keepdims=True) + EPS)
        xn = x * r
        sc = mn_ref[0, :, k_sc * D:(k_sc + 1) * D]
        dh = dh_ref[...]
        dxn = dh * (1.0 + sc)
        dx = r * (dxn - xn * jnp.mean(dxn * xn, axis=-1, keepdims=True)) + dres_ref[...]
        dx_ref[...] = dx
        dy_ref[...] = (dx * mg_ref[0, :, k_g * D:(k_g + 1) * D]).astype(BF16)
        part = jnp.concatenate([jnp.sum(dh, axis=0, keepdims=True), jnp.sum(dh * xn, axis=0, keepdims=True)], axis=1)
        _accumulate_slot(t, dpn_ref, part)
        _accumulate_slot(t, dpg_ref, jnp.sum(dx * y_ref[...], axis=0, keepdims=True))

    def slot(width):
        return pl.BlockSpec((1, 1, width), lambda t: (_slot(t), 0, 0))

    return _pcall(body, name=name,
                  out_shape=(jax.ShapeDtypeStruct((NROW, D), F32), jax.ShapeDtypeStruct((2 * BL, 1, 2 * D), F32),
                             jax.ShapeDtypeStruct((NROW, D), BF16), jax.ShapeDtypeStruct((2 * BL, 1, D), F32)),
                  grid=(NT,), in_specs=[_rows(D), _mods_spec(), _rows(D), _rows(D), _rows(D), _mods_spec()],
                  out_specs=(_rows(D), slot(2 * D), _rows(D), slot(D)), sem=("arbitrary",))(x, mods_n, dh, dres, y, mods_g)


def gate_resid_fwd(x, y, mods, k_g, name):
    def body(x_ref, y_ref, m_ref, o_ref):
        o_ref[...] = x_ref[...] + m_ref[0, :, k_g * D:(k_g + 1) * D] * y_ref[...]

    return _pcall(body, name=name, out_shape=jax.ShapeDtypeStruct((NROW, D), F32), grid=(NT,),
                  in_specs=[_rows(D), _rows(D), _mods_spec()], out_specs=_rows(D), sem=("parallel",))(x, y, mods)


def gate_resid_bwd(dx, y, mods, k_g, name):
    def body(dx_ref, y_ref, m_ref, dy_ref, dp_ref):
        t = pl.program_id(0)
        dx = dx_ref[...]
        dy_ref[...] = (dx * m_ref[0, :, k_g * D:(k_g + 1) * D]).astype(BF16)
        _accumulate_slot(t, dp_ref, jnp.sum(dx * y_ref[...], axis=0, keepdims=True))

    return _pcall(body, name=name,
                  out_shape=(jax.ShapeDtypeStruct((NROW, D), BF16), jax.ShapeDtypeStruct((2 * BL, 1, D), F32)),
                  grid=(NT,), in_specs=[_rows(D), _rows(D), _mods_spec()],
                  out_specs=(_rows(D), pl.BlockSpec((1, 1, D), lambda t: (_slot(t), 0, 0))),
                  sem=("arbitrary",))(dx, y, mods)


def _swap16(y, lo16):
    return jnp.where(lo16, pltpu.roll(y, LANE - 16, 1), pltpu.roll(y, 16, 1))


def _group_mean(v, g_mat):
    return jnp.dot(v, g_mat, precision=HIGHEST, preferred_element_type=F32)


def qkv_fwd(p_main, cos_t, sin_t, g_mat, gq, gk, name):
    def body(p_ref, cos_ref, sin_ref, g_ref, gq_ref, gk_ref, q_ref, k_ref, v_ref):
        cos, sin, g_mat_v = cos_ref[...], sin_ref[...], g_ref[...]
        lo16 = (lax.broadcasted_iota(jnp.int32, (TM, LANE), 1) % 32) < 16

        def block(xb, g):
            xb = xb.astype(F32)
            r = lax.rsqrt(_group_mean(xb * xb, g_mat_v) + EPS)
            y = xb * r * g
            return y * cos + _swap16(y, lo16) * sin

        for j in range(AW // LANE):
            q_ref[:, j * LANE:(j + 1) * LANE] = (block(p_ref[:, j * LANE:(j + 1) * LANE], gq_ref[...])
                                                 * ATTN_SCALE).astype(BF16)
        lo = lax.broadcasted_iota(jnp.int32, (TM, LANE), 1) < HD
        for src, dst_ref in ((block(p_ref[:, OFF_K:OFF_K + LANE], gk_ref[...]), k_ref), (p_ref[:, OFF_V:OFF_V + LANE].astype(F32), v_ref)):
            swapped = pltpu.roll(src, HD, 1)
            dst_ref[:, 0:LANE] = jnp.where(lo, src, swapped).astype(BF16)
            dst_ref[:, LANE:2 * LANE] = jnp.where(lo, swapped, src).astype(BF16)

    tab = pl.BlockSpec((TM, LANE), lambda t: (t % TPE, 0))
    small = pl.BlockSpec((1, LANE), lambda t: (0, 0))
    return _pcall(body, name=name,
                  out_shape=(jax.ShapeDtypeStruct((NROW, AW), BF16), jax.ShapeDtypeStruct((NROW, 2 * KVW), BF16),
                             jax.ShapeDtypeStruct((NROW, 2 * KVW), BF16)),
                  grid=(NT,),
                  in_specs=[_rows(QKVW), tab, tab, pl.BlockSpec((LANE, LANE), lambda t: (0, 0)), small, small],
                  out_specs=(_rows(AW), _rows(2 * KVW), _rows(2 * KVW)),
                  sem=("parallel",))(p_main, cos_t, sin_t, g_mat, gq, gk)


def qkv_bwd(p_main, cos_t, sin_t, g_mat, gq, gk, dq, dk, dv, name, comm=None):
    def body(p_ref, cos_ref, sin_ref, g_ref, gq_ref, gk_ref, dq_ref, dk_ref, dv_ref, dp_ref, dg_ref):
        t = pl.program_id(0)
        cos, sin, g_mat_v = cos_ref[...], sin_ref[...], g_ref[...]
        lo16 = (lax.broadcasted_iota(jnp.int32, (TM, LANE), 1) % 32) < 16

        def block(xb, g, dyr):
            xb = xb.astype(F32)
            r = lax.rsqrt(_group_mean(xb * xb, g_mat_v) + EPS)
            xn = xb * r
            dy = dyr * cos + _swap16(dyr * sin, lo16)
            dgl = jnp.sum(dy * xn, axis=0, keepdims=True)
            dxn = dy * g
            return r * (dxn - xn * _group_mean(dxn * xn, g_mat_v)), dgl

        parts = []
        for j in range(AW // LANE):
            sl = slice(j * LANE, (j + 1) * LANE)
            dxb, dgl = block(p_ref[:, sl], gq_ref[...], dq_ref[:, sl] * ATTN_SCALE)
            dp_ref[:, sl] = dxb.astype(BF16)
            parts.append(dgl)
        lo = lax.broadcasted_iota(jnp.int32, (TM, LANE), 1) < HD

        def fold(d_ref):
            d0, d1 = d_ref[:, 0:LANE], d_ref[:, LANE:2 * LANE]
            return jnp.where(lo, d0 + pltpu.roll(d0, HD, 1), d1 + pltpu.roll(d1, HD, 1))

        dxb, dgl = block(p_ref[:, OFF_K:OFF_K + LANE], gk_ref[...], fold(dk_ref))
        dp_ref[:, OFF_K:OFF_K + LANE] = dxb.astype(BF16)
        parts.append(dgl)
        parts.append(jnp.zeros((1, LANE), F32))
        dp_ref[:, OFF_V:OFF_V + LANE] = fold(dv_ref).astype(BF16)
        part = jnp.concatenate(parts, axis=1)

        @pl.when(t == 0)
        def _():
            dg_ref[...] = part

        @pl.when(t != 0)
        def _():
            dg_ref[...] += part

    tab = pl.BlockSpec((TM, LANE), lambda t: (t % TPE, 0))
    small = pl.BlockSpec((1, LANE), lambda t: (0, 0))
    return _pcall(body, name=name,
                  out_shape=(jax.ShapeDtypeStruct((NROW, QKVW), BF16), jax.ShapeDtypeStruct((1, QKVW), F32)),
                  grid=(NT,),
                  in_specs=[_rows(QKVW), tab, tab, pl.BlockSpec((LANE, LANE), lambda t: (0, 0)), small, small,
                            _rows(AW), _rows(2 * KVW), _rows(2 * KVW)],
                  out_specs=(_rows(QKVW), pl.BlockSpec((1, QKVW), lambda t: (0, 0))),
                  sem=("arbitrary",), comm=comm)(p_main, cos_t, sin_t, g_mat, gq, gk, dq, dk, dv)


def _layer_norm_parts(yc):
    mu = jnp.mean(yc, axis=-1, keepdims=True)
    xc = yc - mu
    rs = lax.rsqrt(jnp.mean(xc * xc, axis=-1, keepdims=True) + EPS)
    return xc * rs, rs


def ln_silu_fwd(yc, g, b, name):
    def body(y_ref, g_ref, b_ref, o_ref):
        nrm, _ = _layer_norm_parts(y_ref[...])
        ln = nrm * g_ref[...] + b_ref[...]
        o_ref[...] = (ln * _sigmoid(ln)).astype(BF16)

    vec = pl.BlockSpec((1, CW), lambda t: (0, 0))
    return _pcall(body, name=name, out_shape=jax.ShapeDtypeStruct((NROW, CW), BF16), grid=(NT,),
                  in_specs=[_rows(CW), vec, vec], out_specs=_rows(CW), sem=("parallel",))(yc, g, b)


def ln_silu_bwd(yc, g, b, dhs, name, comm=None):
    def body(y_ref, g_ref, b_ref, dh_ref, dy_ref, dg_ref, db_ref):
        t = pl.program_id(0)
        nrm, rs = _layer_norm_parts(y_ref[...])
        ln = nrm * g_ref[...] + b_ref[...]
        sg = _sigmoid(ln)
        dln = dh_ref[...] * (sg * (1.0 + ln * (1.0 - sg)))
        dn = dln * g_ref[...]
        dy_ref[...] = rs * (dn - jnp.mean(dn, axis=-1, keepdims=True)
                            - nrm * jnp.mean(dn * nrm, axis=-1, keepdims=True))
        pg = jnp.sum(dln * nrm, axis=0, keepdims=True)
        pb = jnp.sum(dln, axis=0, keepdims=True)

        @pl.when(t == 0)
        def _():
            dg_ref[...] = pg
            db_ref[...] = pb

        @pl.when(t != 0)
        def _():
            dg_ref[...] += pg
            db_ref[...] += pb

    vec = pl.BlockSpec((1, CW), lambda t: (0, 0))
    return _pcall(body, name=name,
                  out_shape=(jax.ShapeDtypeStruct((NROW, CW), F32), jax.ShapeDtypeStruct((1, CW), F32),
                             jax.ShapeDtypeStruct((1, CW), F32)),
                  grid=(NT,), in_specs=[_rows(CW), vec, vec, _rows(CW)], out_specs=(_rows(CW), vec, vec),
                  sem=("arbitrary",), comm=comm)(yc, g, b, dhs)


def loss_fwd_bwd(y, target, name):
    def body(y_ref, t_ref, dy_ref, l_ref):
        t = pl.program_id(0)
        latent = (t % TPE) != 0
        err = jnp.where(latent, y_ref[...] - t_ref[...], 0.0)
        dy_ref[...] = err * (1.0 / D)
        part = jnp.sum(err * err, axis=0, keepdims=True)

        @pl.when(t == 0)
        def _():
            l_ref[...] = part

        @pl.when(t != 0)
        def _():
            l_ref[...] += part

    tgt = pl.BlockSpec((TM, D), lambda t: ((t // TPE) * (TPE - 1) + jnp.maximum(t % TPE - 1, 0), 0))
    return _pcall(body, name=name,
                  out_shape=(jax.ShapeDtypeStruct((NROW, D), F32), jax.ShapeDtypeStruct((1, D), F32)),
                  grid=(NT,), in_specs=[_rows(D), tgt], out_specs=(_rows(D), pl.BlockSpec((1, D), lambda t: (0, 0))),
                  sem=("arbitrary",))(y, target)


QB_PER_KV = AW // LANE // NKV


def _softmax_parts(qm, k):
    s = lax.dot_general(qm, k, (((1,), (1,)), ((), ())), preferred_element_type=F32)
    e = jnp.exp(s - jnp.max(s, axis=-1, keepdims=True))
    return e, 1.0 / jnp.sum(e, axis=-1, keepdims=True)


def _lane_halves():
    lo = lax.broadcasted_iota(jnp.int32, (TM, LANE), 1) < HD
    return lo, jnp.logical_not(lo)


def _stack_heads(x, halves):
    zero = jnp.zeros_like(x)
    return jnp.concatenate([jnp.where(halves[0], x, zero), jnp.where(halves[1], x, zero)], axis=0)


def attn_fwd(q, k, v, name, comm=None):
    def body(q_ref, k_ref, v_ref, o_ref):
        t = pl.program_id(2)
        halves = _lane_halves()

        def run(nk):
            kv, vv = k_ref[0:nk, :], v_ref[0:nk, :]
            for j in range(QB_PER_KV):
                lanes = slice(j * LANE, (j + 1) * LANE)
                e, rinv = _softmax_parts(_stack_heads(q_ref[:, lanes], halves), kv)
                out = jnp.dot(e.astype(BF16), vv, preferred_element_type=F32) * rinv
                o_ref[:, lanes] = jnp.where(halves[0], out[0:TM], out[TM:2 * TM]).astype(BF16)

        @pl.when(t == 0)
        def _():
            run(CTX)

        @pl.when(t != 0)
        def _():
            run(RE)

    qs = pl.BlockSpec((TM, QB_PER_KV * LANE), lambda b, h, t: (b * TPE + t, h))
    ks = pl.BlockSpec((RE, LANE), lambda b, h, t: (b, h))
    return _pcall(body, name=name, out_shape=jax.ShapeDtypeStruct((NROW, AW), BF16), grid=(BL, NKV, TPE),
                  in_specs=[qs, ks, ks], out_specs=qs, sem=("parallel",) * 3, comm=comm)(q, k, v)


def attn_bwd(q, k, v, o, do, name, comm=None):
    def body(q_ref, k_ref, v_ref, o_ref, do_ref, dq_ref, dk_ref, dv_ref):
        t = pl.program_id(2)
        halves = _lane_halves()

        @pl.when(t == 0)
        def _():
            dk_ref[...] = jnp.zeros_like(dk_ref)
            dv_ref[...] = jnp.zeros_like(dv_ref)

        def run(nk):
            kv, vv = k_ref[0:nk, :], v_ref[0:nk, :]
            dks, dvs = [], []
            for j in range(QB_PER_KV):
                lanes = slice(j * LANE, (j + 1) * LANE)
                q2, do2 = _stack_heads(q_ref[:, lanes], halves), _stack_heads(do_ref[:, lanes], halves)
                ov = o_ref[:, lanes].astype(F32)
                delta = jnp.sum(do2.astype(F32) * jnp.concatenate([ov, ov], axis=0), axis=-1, keepdims=True)
                e, rinv = _softmax_parts(q2, kv)
                p = e * rinv
                dvs.append(lax.dot_general(p.astype(BF16), do2, (((0,), (0,)), ((), ())), preferred_element_type=F32))
                dp = lax.dot_general(do2, vv, (((1,), (1,)), ((), ())), preferred_element_type=F32)
                ds = (p * (dp - delta)).astype(BF16)
                dq = jnp.dot(ds, kv, preferred_element_type=F32)
                dks.append(lax.dot_general(ds, q2, (((0,), (0,)), ((), ())), preferred_element_type=F32))
                dq_ref[:, lanes] = jnp.where(halves[0], dq[0:TM], dq[TM:2 * TM])
            dv_ref[0:nk, :] += functools.reduce(jnp.add, dvs)
            dk_ref[0:nk, :] += functools.reduce(jnp.add, dks)

        @pl.when(t == 0)
        def _():
            run(CTX)

        @pl.when(t != 0)
        def _():
            run(RE)

    qs = pl.BlockSpec((TM, QB_PER_KV * LANE), lambda b, h, t: (b * TPE + t, h))
    ks = pl.BlockSpec((RE, LANE), lambda b, h, t: (b, h))
    return _pcall(body, name=name,
                  out_shape=(jax.ShapeDtypeStruct((NROW, AW), F32), jax.ShapeDtypeStruct((NROW, 2 * KVW), F32),
                             jax.ShapeDtypeStruct((NROW, 2 * KVW), F32)),
                  grid=(BL, NKV, TPE), in_specs=[qs, ks, ks, qs, qs], out_specs=(qs, ks, ks),
                  sem=("parallel", "parallel", "arbitrary"), comm=comm)(q, k, v, o, do)


CONV_SEGS = ((0, CTX), (CTX, SEQ))


def _p_block(col0):
    return pl.BlockSpec((RE, LANE), lambda cb, b: (b, col0 // LANE + cb))


def _conv_io(width):
    return pl.BlockSpec((RE, LANE), lambda cb, b: (b, cb))


def _taps(n):
    return pl.BlockSpec((n, LANE), lambda cb, b: (0, cb))


def _fill_pad(pad_ref, length, values):
    pad_ref[0:PADR, :] = jnp.zeros((PADR, LANE), F32)
    pad_ref[PADR + length:2 * PADR + length, :] = jnp.zeros((PADR, LANE), F32)
    pad_ref[PADR:PADR + length, :] = values


def _conv_chunk(pad_ref, w_ref, ntap, c0, first_row):
    acc = jnp.zeros((CONV_CH, LANE), F32)
    for kk in range(ntap):
        r0 = c0 + first_row(kk)
        acc += w_ref[kk:kk + 1, :] * pad_ref[r0:r0 + CONV_CH, :]
    return acc


def conv_fwd(p_main, wdw, bdw, w3, name, comm=None):
    def body(a_ref, g_ref, bg_ref, cg_ref, xs_ref, w_ref, b_ref, w3_ref, yc_ref, z_ref, pad_ref):
        for off, length in CONV_SEGS:
            rows = slice(off, off + length)
            _fill_pad(pad_ref, length, a_ref[rows, :].astype(F32) * _sigmoid(g_ref[rows, :].astype(F32)))
            for c0 in range(0, length, CONV_CH):
                acc = _conv_chunk(pad_ref, w_ref, CONF_K, c0, lambda kk: PADR + kk - CONF_K // 2)
                yc_ref[off + c0:off + c0 + CONV_CH, :] = acc + b_ref[...]
            pad_ref[PADR:PADR + length, :] = cg_ref[rows, :].astype(F32) * xs_ref[rows, :].astype(F32)
            for c0 in range(0, length, CONV_CH):
                acc = _conv_chunk(pad_ref, w3_ref, SC_K, c0, lambda kk: PADR + kk - SC_K // 2)
                z_ref[off + c0:off + c0 + CONV_CH, :] = (bg_ref[off + c0:off + c0 + CONV_CH, :] * acc).astype(BF16)

    return _pcall(body, name=name,
                  out_shape=(jax.ShapeDtypeStruct((NROW, CW), F32), jax.ShapeDtypeStruct((NROW, CW), BF16)),
                  grid=(CB, BL),
                  in_specs=[_p_block(OFF_CONF), _p_block(OFF_CONF + CW), _p_block(OFF_SC), _p_block(OFF_SC + CW),
                            _p_block(OFF_SC + 2 * CW), _taps(CONF_K), _taps(1), _taps(SC_K)],
                  out_specs=(_conv_io(CW), _conv_io(CW)),
                  scratch=[pltpu.VMEM((SEQ + 2 * PADR, LANE), F32)],
                  sem=("parallel", "parallel"), comm=comm)(p_main, p_main, p_main, p_main, p_main, wdw, bdw, w3)


def _tap_grad(pad_ref, d_ref, off, length, first_row):
    acc = jnp.zeros((8, LANE), F32)
    for c0 in range(0, length, CONV_CH):
        prod = d_ref[off + c0:off + c0 + CONV_CH, :] * pad_ref[c0 + first_row:c0 + first_row + CONV_CH, :]
        acc += jnp.sum(prod.reshape(CONV_CH // 8, 8, LANE), axis=0)
    return jnp.sum(acc, axis=0, keepdims=True)


def conv_bwd(p_main, wdw, w3, dyc, dz, name):
    def body(a_ref, g_ref, bg_ref, cg_ref, xs_ref, w_ref, w3_ref, dyc_ref, dz_ref,
             da_ref, dg_ref, dbg_ref, dcg_ref, dxs_ref, dw_ref, db_ref, dw3_ref, pad_x, pad_d, dconv_ref):
        b = pl.program_id(1)

        @pl.when(b == 0)
        def _():
            dw_ref[...] = jnp.zeros_like(dw_ref)
            db_ref[...] = jnp.zeros_like(db_ref)
            dw3_ref[...] = jnp.zeros_like(dw3_ref)

        db_ref[...] += jnp.sum(dyc_ref[...], axis=0, keepdims=True)
        for off, length in CONV_SEGS:
            rows = slice(off, off + length)
            _fill_pad(pad_x, length, a_ref[rows, :].astype(F32) * _sigmoid(g_ref[rows, :].astype(F32)))
            _fill_pad(pad_d, length, dyc_ref[rows, :])
            for kk in range(CONF_K):
                dw_ref[kk:kk + 1, :] += _tap_grad(pad_x, dyc_ref, off, length, PADR + kk - CONF_K // 2)
            for c0 in range(0, length, CONV_CH):
                dh = _conv_chunk(pad_d, w_ref, CONF_K, c0, lambda kk: PADR + CONF_K // 2 - kk)
                ch = slice(off + c0, off + c0 + CONV_CH)
                sg = _sigmoid(g_ref[ch, :].astype(F32))
                da_ref[ch, :] = (dh * sg).astype(BF16)
                dg_ref[ch, :] = (dh * a_ref[ch, :] * sg * (1.0 - sg)).astype(BF16)
            pad_x[PADR:PADR + length, :] = cg_ref[rows, :].astype(F32) * xs_ref[rows, :].astype(F32)
            dconv_ref[rows, :] = dz_ref[rows, :] * bg_ref[rows, :]
            pad_d[PADR:PADR + length, :] = dconv_ref[rows, :]
            for kk in range(SC_K):
                dw3_ref[kk:kk + 1, :] += _tap_grad(pad_x, dconv_ref, off, length, PADR + kk - SC_K // 2)
            for c0 in range(0, length, CONV_CH):
                ch = slice(off + c0, off + c0 + CONV_CH)
                c3 = _conv_chunk(pad_x, w3_ref, SC_K, c0, lambda kk: PADR + kk - SC_K // 2)
                dbg_ref[ch, :] = (dz_ref[ch, :] * c3).astype(BF16)
                dcx = _conv_chunk(pad_d, w3_ref, SC_K, c0, lambda kk: PADR + SC_K // 2 - kk)
                dcg_ref[ch, :] = (dcx * xs_ref[ch, :]).astype(BF16)
                dxs_ref[ch, :] = (dcx * cg_ref[ch, :]).astype(BF16)

    slab = jax.ShapeDtypeStruct((NROW, CW), BF16)
    return _pcall(body, name=name,
                  out_shape=(slab,) * 5 + (jax.ShapeDtypeStruct((CONF_K, CW), F32), jax.ShapeDtypeStruct((1, CW), F32),
                                           jax.ShapeDtypeStruct((SC_K, CW), F32)),
                  grid=(CB, BL),
                  in_specs=[_p_block(OFF_CONF), _p_block(OFF_CONF + CW), _p_block(OFF_SC), _p_block(OFF_SC + CW),
                            _p_block(OFF_SC + 2 * CW), _taps(CONF_K), _taps(SC_K), _conv_io(CW), _conv_io(CW)],
                  out_specs=(_conv_io(CW),) * 5 + (_taps(CONF_K), _taps(1), _taps(SC_K)),
                  scratch=[pltpu.VMEM((SEQ + 2 * PADR, LANE), F32), pltpu.VMEM((SEQ + 2 * PADR, LANE), F32),
                           pltpu.VMEM((RE, LANE), F32)],
                  sem=("parallel", "arbitrary"))(p_main, p_main, p_main, p_main, p_main, wdw, w3, dyc, dz)


def silu_rows(x, name):
    def body(x_ref, o_ref):
        o_ref[...] = x_ref[...] * _sigmoid(x_ref[...])

    return _pcall(body, name=name, out_shape=jax.ShapeDtypeStruct(x.shape, F32))(x)


def silu_rows_bwd(x, dcs, name):
    def body(x_ref, d_ref, o_ref):
        x = x_ref[...]
        sg = _sigmoid(x)
        tot = d_ref[0]
        for i in range(1, DEPTH):
            tot += d_ref[i]
        o_ref[...] = tot * (sg * (1.0 + x * (1.0 - sg)))

    return _pcall(body, name=name, out_shape=jax.ShapeDtypeStruct(x.shape, F32))(x, dcs)


def dmod_assemble(parts, name):
    def body(p_ref, dm_ref, db_ref):
        row = lax.broadcasted_iota(jnp.int32, (8, NMOD * D), 0)
        dm = jnp.zeros((8, NMOD * D), F32)
        db = jnp.zeros((1, NMOD * D), F32)
        for s in range(2 * BL):
            target = BL if s % 2 == 0 else s // 2
            part = p_ref[s:s + 1, :]
            dm += jnp.where(row == target, part, 0.0)
            db += part
        dm_ref[...] = dm
        db_ref[...] = db

    return _pcall(body, name=name, out_shape=(jax.ShapeDtypeStruct((8, NMOD * D), F32),
                                              jax.ShapeDtypeStruct((1, NMOD * D), F32)))(parts)


def sum_leading(x, name):
    n = x.shape[0]
    tr = _pick(x.shape[1], (256, 32, 8))

    def body(x_ref, o_ref):
        tot = x_ref[0].astype(F32)
        for i in range(1, n):
            tot += x_ref[i].astype(F32)
        o_ref[...] = tot

    return _pcall(body, name=name, out_shape=jax.ShapeDtypeStruct(x.shape[1:], F32), grid=(x.shape[1] // tr,),
                  in_specs=[pl.BlockSpec((n, tr, x.shape[2]), lambda i: (0, i, 0))],
                  out_specs=pl.BlockSpec((tr, x.shape[2]), lambda i: (i, 0)), sem=("parallel",))(x)


SLAB_ROWS = (256, 176, 128, 64, 8)


def _prefetch_call(body, name, out_shape, grid, in_specs, out_specs, sem, scalars, *args):
    spec = pltpu.PrefetchScalarGridSpec(num_scalar_prefetch=len(scalars), grid=grid, in_specs=in_specs,
                                        out_specs=out_specs)
    return pl.pallas_call(body, name=name, out_shape=out_shape, grid_spec=spec,
                          compiler_params=pltpu.CompilerParams(dimension_semantics=sem,
                                                               vmem_limit_bytes=VMEM_LIMIT))(*scalars, *args)


def cast_layers(w, chip, name):
    depth, r, c = w.shape
    tr = _pick(r, SLAB_ROWS)

    def body(s_ref, w_ref, *o_refs):
        for l in range(depth):
            o_refs[l][...] = w_ref[l].astype(BF16)

    slab = pl.BlockSpec((None, tr, c), lambda i, s: (s[0], i, 0))
    return _prefetch_call(body, name, (jax.ShapeDtypeStruct((NCHIP, r, c), BF16),) * depth, (r // tr,),
                          [pl.BlockSpec((depth, tr, c), lambda i, s: (0, i, 0))], (slab,) * depth,
                          ("parallel",), (chip,), w)


def rs_add(g, other, core, chip, name):
    _, r, c = g.shape
    rh = r // 2
    tr = _pick(rh, SLAB_ROWS)
    nblk = rh // tr

    def body(core_ref, chip_ref, g_ref, o_ref, send_ref, arr_ref):
        k = pl.program_id(1)
        tot = (g_ref[...].astype(F32) + o_ref[...].astype(F32)).astype(BF16)
        send_ref[...] = tot

        @pl.when(k == chip_ref[0])
        def _():
            arr_ref[...] = tot

    blk = (None, tr, c)
    return _prefetch_call(
        body, name, (jax.ShapeDtypeStruct(other.shape, BF16), jax.ShapeDtypeStruct(g.shape, BF16)), (nblk, NCHIP),
        [pl.BlockSpec(blk, lambda i, k, cr, ch: (k, cr[0] * nblk + i, 0)), pl.BlockSpec(blk, lambda i, k, cr, ch: (k, i, 0))],
        (pl.BlockSpec(blk, lambda i, k, cr, ch: (k, i, 0)),
         pl.BlockSpec(blk, lambda i, k, cr, ch: (ch[0], cr[0] * nblk + i, 0))),
        ("parallel", "arbitrary"), (core, chip), g, other)


def adamw_layers(w, arrs, m, v, first, prev, name, comm=None):
    depth, r, c = w.shape
    tr = _pick(r, (128, 176, 64, 8))
    nblk = r // tr
    nl = len(arrs)
    c1 = 1.0 / (1.0 - ADAM_B1 ** ADAM_STEP)
    c2 = 1.0 / (1.0 - ADAM_B2 ** ADAM_STEP)

    def body(w_ref, m_ref, v_ref, *rest):
        a_refs = rest[:nl]
        g_ref, d_ref, mo_ref, vo_ref = rest[nl + 4:nl + 8]
        li = pl.program_id(0)
        gv = None
        for idx, a_ref in enumerate(a_refs):
            tot = a_ref[0].astype(F32)
            for k in range(1, NCHIP):
                tot += a_ref[k].astype(F32)
            gv = tot if gv is None else jnp.where(li == idx, tot, gv)
        mn = ADAM_B1 * m_ref[...] + (1.0 - ADAM_B1) * gv
        vn = ADAM_B2 * v_ref[...] + (1.0 - ADAM_B2) * (gv * gv)
        g_ref[...] = gv
        d_ref[...] = -ADAM_LR * ((mn * c1) / (jnp.sqrt(vn * c2) + ADAM_EPS) + ADAM_WD * w_ref[...])
        mo_ref[...] = mn
        vo_ref[...] = vn

    def arr_spec(idx):
        return pl.BlockSpec((NCHIP, tr, c),
                            lambda li, i: (0, jnp.where(li == idx, i, jnp.where(li < idx, 0, nblk - 1)), 0))

    spec = pl.BlockSpec((None, tr, c), lambda li, i: (first + li, i, 0))
    sds = jax.ShapeDtypeStruct(w.shape, F32)
    return _pcall(body, name=name, out_shape=(sds,) * 4, grid=(nl, nblk),
                  in_specs=[spec, spec, spec] + [arr_spec(idx) for idx in range(nl)] + [ANY] * 4,
                  out_specs=(spec,) * 4, aliases={3 + nl + i: i for i in range(4)},
                  sem=("arbitrary", "arbitrary"), comm=comm)(w, m, v, *arrs, *prev)


def adamw(w, g, m, v, name):
    rows, cols = w.shape
    tr = _pick(rows, (256, 248, 128, 8))
    c1 = 1.0 / (1.0 - ADAM_B1 ** ADAM_STEP)
    c2 = 1.0 / (1.0 - ADAM_B2 ** ADAM_STEP)

    def body(w_ref, g_ref, m_ref, v_ref, d_ref, mo_ref, vo_ref):
        gv = g_ref[...]
        mn = ADAM_B1 * m_ref[...] + (1.0 - ADAM_B1) * gv
        vn = ADAM_B2 * v_ref[...] + (1.0 - ADAM_B2) * (gv * gv)
        d_ref[...] = -ADAM_LR * ((mn * c1) / (jnp.sqrt(vn * c2) + ADAM_EPS) + ADAM_WD * w_ref[...])
        mo_ref[...] = mn
        vo_ref[...] = vn

    spec = pl.BlockSpec((tr, cols), lambda i: (i, 0))
    sds = jax.ShapeDtypeStruct((rows, cols), F32)
    return _pcall(body, name=name, out_shape=(sds, sds, sds), grid=(rows // tr,), in_specs=[spec] * 4,
                  out_specs=(spec, spec, spec), sem=("parallel",))(w, g, m, v)


def _place():
    return lax.axis_index("x"), lax.axis_index("y"), lax.axis_index("c")


def _other_chips(x, y):
    return [(1 - x, y), (x, 1 - y), (1 - x, 1 - y)]


def _comm_call(body, name, out_shape, n_in, nsem):
    return pl.pallas_call(body, name=name, out_shape=out_shape, in_specs=[ANY] * n_in,
                          out_specs=jax.tree.map(lambda _: ANY, out_shape),
                          scratch_shapes=[pltpu.SemaphoreType.DMA((nsem,)), pltpu.SemaphoreType.DMA((nsem,)),
                                          pltpu.SemaphoreType.DMA])


def all_gather8(block, name):
    def body(x_ref, out_ref, send_sems, recv_sems, local_sem):
        x, y, c = _place()
        me, sibling = (x, y, c), (x, y, 1 - c)
        chips = _other_chips(x, y)

        def slot(px, py, pc):
            return out_ref.at[4 * px + 2 * py + pc]

        def copy(k, blk, to, src=None):
            return pltpu.make_async_remote_copy(src_ref=slot(*blk) if src is None else src, dst_ref=slot(*blk),
                                                send_sem=send_sems.at[k], recv_sem=recv_sems.at[k],
                                                device_id=to, device_id_type=MESH)

        mine = pltpu.make_async_copy(x_ref, slot(*me), local_sem)
        mine.start()
        first = [copy(0, me, sibling, src=x_ref)]
        first += [copy(1 + j, me, (*chip, c), src=x_ref) for j, chip in enumerate(chips)]
        for cp in first:
            cp.start()
        passed = [copy(4 + j, (*chip, c), sibling) for j, chip in enumerate(chips)]
        for j, chip in enumerate(chips):
            copy(1 + j, (*chip, c), me).wait_recv()
            passed[j].start()
        copy(0, sibling, me).wait_recv()
        for j, chip in enumerate(chips):
            copy(4 + j, (*chip, 1 - c), me).wait_recv()
        for cp in first + passed:
            cp.wait_send()
        mine.wait()

    return _comm_call(body, name, jax.ShapeDtypeStruct((8,) + block.shape, block.dtype), 1, 7)(block)


def _remote(src, dst, send_sems, recv_sems, k, to):
    return pltpu.make_async_remote_copy(src_ref=src, dst_ref=dst, send_sem=send_sems.at[k], recv_sem=recv_sems.at[k],
                                        device_id=to, device_id_type=MESH)


def _half(ref, slot, core):
    rh = ref.shape[1] // 2
    return ref.at[slot, pl.ds(core * rh, rh)]


def _all_slots_half(ref, core):
    rh = ref.shape[1] // 2
    return ref.at[:, pl.ds(core * rh, rh)]


def gather_ici(bufs):
    def program(ro, rw, new, ss, rs):
        x, y, c = _place()
        own = 2 * x + y
        starts, arrivals = [], []
        for w, ref in enumerate(rw):
            for j, chip in enumerate(_other_chips(x, y)):
                starts.append(_remote(_half(ref, own, c), _half(ref, own, c), ss, rs, 3 * w + j, (*chip, c)))
                arrivals.append(_remote(_half(ref, own, c), _half(ref, 2 * chip[0] + chip[1], c), ss, rs, 3 * w + j,
                                        (*chip, c)))
        return starts, arrivals

    return CommSpec((), tuple(bufs), (), 3 * len(bufs), program)


def gather_d2d(bufs):
    def program(ro, rw, new, ss, rs):
        x, y, c = _place()
        starts, arrivals = [], []
        for w, ref in enumerate(rw):
            for j, chip in enumerate(_other_chips(x, y)):
                slot = 2 * chip[0] + chip[1]
                starts.append(_remote(_half(ref, slot, c), _half(ref, slot, c), ss, rs, 3 * w + j, (x, y, 1 - c)))
                arrivals.append(_remote(_half(ref, slot, c), _half(ref, slot, 1 - c), ss, rs, 3 * w + j, (x, y, 1 - c)))
        return starts, arrivals

    return CommSpec((), tuple(bufs), (), 3 * len(bufs), program)


def rs_swap(grads):
    def program(ro, rw, new, ss, rs):
        x, y, c = _place()
        copies = [_remote(_all_slots_half(g, 1 - c), new[w], ss, rs, w, (x, y, 1 - c)) for w, g in enumerate(ro)]
        return copies, copies

    shapes = tuple(jax.ShapeDtypeStruct((NCHIP, g.shape[1] // 2, g.shape[2]), g.dtype) for g in grads)
    return CommSpec(tuple(grads), (), shapes, len(grads), program)


def rs_ici(sends, arrs):
    def program(ro, rw, new, ss, rs):
        x, y, c = _place()
        own = 2 * x + y
        starts, arrivals = [], []
        for w, (snd, arr) in enumerate(zip(ro, rw)):
            for j, chip in enumerate(_other_chips(x, y)):
                slot = 2 * chip[0] + chip[1]
                starts.append(_remote(snd.at[slot], _half(arr, own, c), ss, rs, 3 * w + j, (*chip, c)))
                arrivals.append(_remote(snd.at[slot], _half(arr, slot, c), ss, rs, 3 * w + j, (*chip, c)))
        return starts, arrivals

    return CommSpec(tuple(sends), tuple(arrs), (), 3 * len(sends), program)


def rs_d2d(arrs):
    def program(ro, rw, new, ss, rs):
        x, y, c = _place()
        starts = [_remote(_all_slots_half(a, c), _all_slots_half(a, c), ss, rs, w, (x, y, 1 - c)) for w, a in enumerate(rw)]
        arrivals = [_remote(_all_slots_half(a, c), _all_slots_half(a, 1 - c), ss, rs, w, (x, y, 1 - c))
                    for w, a in enumerate(rw)]
        return starts, arrivals

    return CommSpec((), tuple(arrs), (), len(arrs), program)


PACK_COLS = 1024
MATMUL_W = ("w_ada", "w_in", "w_attn_o", "w_conf_out", "w_sc_out", "w_mix_out", "w_ffn_in", "w_ffn_out")
ROW_SPLIT = ("w_mix_out", "w_ffn_out")
CONV_W = ("conf_dw_w", "sc_dw_w")
SMALL = ("c_ctx", "b_ada", "q_norm", "k_norm", "conf_dw_b", "conf_ln_g", "conf_ln_b", "conf_dw_w", "sc_dw_w")


def _pack_rows(arrays, row_multiple):
    flat = jnp.concatenate([a.reshape(-1) for a in arrays])
    rows = -(-flat.shape[0] // PACK_COLS)
    rows = -(-rows // row_multiple) * row_multiple
    flat = jnp.pad(flat, (0, rows * PACK_COLS - flat.shape[0]))
    return flat.reshape(rows, PACK_COLS)


def _unpack(flat2d, shapes):
    flat = flat2d.reshape(-1)
    out, pos = [], 0
    for shp in shapes:
        n = 1
        for s in shp:
            n *= s
        out.append(flat[pos:pos + n].reshape(shp))
        pos += n
    return out


def _cols_joined(stacked_layer):
    nchip, r, c = stacked_layer.shape
    return jnp.transpose(stacked_layer, (1, 0, 2)).reshape(r, nchip * c)


def _cols_split(full):
    r, cols = full.shape
    return jnp.transpose(full.reshape(r, NCHIP, cols // NCHIP), (1, 0, 2))


def _rope_tables():
    rows = SEQ // GRID_W
    r_ids = jnp.repeat(jnp.arange(rows, dtype=F32), GRID_W)
    c_ids = jnp.tile(jnp.arange(GRID_W, dtype=F32), rows)
    freqs = ROPE_THETA ** (-jnp.arange(0, HD // 2, 2, dtype=F32) / (HD // 2))
    ang_r, ang_c = r_ids[:, None] * freqs, c_ids[:, None] * freqs
    cos_h = jnp.concatenate([jnp.cos(ang_r), jnp.cos(ang_r), jnp.cos(ang_c), jnp.cos(ang_c)], axis=1)
    sin_h = jnp.concatenate([-jnp.sin(ang_r), jnp.sin(ang_r), -jnp.sin(ang_c), jnp.sin(ang_c)], axis=1)
    cos_t = jnp.concatenate([jnp.ones((CTX, HD), F32), cos_h], axis=0)
    sin_t = jnp.concatenate([jnp.zeros((CTX, HD), F32), sin_h], axis=0)
    return jnp.tile(cos_t, (1, LANE // HD)), jnp.tile(sin_t, (1, LANE // HD))


def _group_matrix():
    gid = jnp.arange(LANE) // HD
    return jnp.where(gid[:, None] == gid[None, :], 1.0 / HD, 0.0).astype(F32)


N_FIRST = 2
assert MATMUL_W[:N_FIRST] == ("w_ada", "w_in")


def _first_weights(bufs, small, i):
    wi = _cols_joined(bufs[1])
    return dict(
        w_ada=(bufs[0], "cols"), wi_main=(wi[:, :OFF_GATE], "mat"), wi_gate=(wi[:, OFF_GATE:], "mat"),
        conf_dw_w=small["conf_dw_w"][i], sc_dw_w=small["sc_dw_w"][i], conf_dw_b=small["conf_dw_b"][i][None],
        conf_ln_g=small["conf_ln_g"][i][None], conf_ln_b=small["conf_ln_b"][i][None],
        gq=jnp.tile(small["q_norm"][i], LANE // HD)[None], gk=jnp.tile(small["k_norm"][i], LANE // HD)[None])


def _second_weights(bufs):
    b = dict(zip(MATMUL_W[N_FIRST:], bufs))

    def rows_joined(a):
        return a.reshape(a.shape[0] * a.shape[1], a.shape[2])

    return dict(
        w_attn_o=(_cols_joined(b["w_attn_o"]), "mat"), w_conf_out=(_cols_joined(b["w_conf_out"]), "mat"),
        w_sc_out=(_cols_joined(b["w_sc_out"]), "mat"), w_ffn_in=(b["w_ffn_in"], "cols"),
        w_mix_out=(rows_joined(b["w_mix_out"]), "mat"), w_ffn_out=(rows_joined(b["w_ffn_out"]), "mat"))


def _layer_fwd(i, xs, h, mods, w, tabs, second_bufs, next_first, next_layer, distributed):
    cos_t, sin_t, g_mat = tabs
    n = f"l{i}_"
    w = dict(w)
    sv = {"x_in": xs, "mods": mods, "h": h, "w": w}
    n_second = len(second_bufs)
    sv["p_main"] = mm_nn(sv["h"], w["wi_main"], name=n + "p_main")
    sv["q"], sv["k"], sv["v"] = qkv_fwd(sv["p_main"], cos_t, sin_t, g_mat, w["gq"], w["gk"], n + "qkv")
    if distributed:
        riding = list(second_bufs) + list(next_first or [])
        sv["o"], riding, _ = attn_fwd(sv["q"], sv["k"], sv["v"], n + "attn", comm=gather_ici(riding))
        second_bufs, next_first = riding[:n_second], (riding[n_second:] or None)
        (sv["yc"], sv["z"]), second_bufs, _ = conv_fwd(sv["p_main"], w["conf_dw_w"], w["conf_dw_b"], w["sc_dw_w"],
                                                       n + "conv", comm=gather_d2d(second_bufs))
    else:
        sv["o"] = attn_fwd(sv["q"], sv["k"], sv["v"], n + "attn")
        sv["yc"], sv["z"] = conv_fwd(sv["p_main"], w["conf_dw_w"], w["conf_dw_b"], w["sc_dw_w"], n + "conv")
    w.update(_second_weights(second_bufs))
    next_bufs = next_first if distributed else None
    sv["hs"] = ln_silu_fwd(sv["yc"], w["conf_ln_g"], w["conf_ln_b"], n + "ln_silu")
    sv["merged"], sv["gates"], sv["ys"] = gate_mm_fwd(sv["h"], w["wi_gate"][0], sv["o"], sv["hs"], sv["z"],
                                                      w["w_attn_o"][0], w["w_conf_out"][0], w["w_sc_out"][0],
                                                      n + "gate_merge")
    sv["mixed"] = mm_nn(sv["merged"], w["w_mix_out"], name=n + "mix")
    sv["x1"], sv["h2"] = resid_norm_fwd(xs, sv["mixed"], mods, 2, mods, 3, 4, n + "resid1_norm2")
    if next_bufs is None:
        sv["f"], sv["u2"] = ffn_in_swiglu(sv["h2"], w["w_ffn_in"][0], n + "ffn_in")
    else:
        (sv["f"], sv["u2"]), next_bufs, _ = ffn_in_swiglu(sv["h2"], w["w_ffn_in"][0], n + "ffn_in",
                                                          comm=gather_d2d(next_bufs))
    sv["of"] = mm_nn(sv["f"], w["w_ffn_out"], name=n + "ffn_out")
    if next_layer is None:
        return gate_resid_fwd(sv["x1"], sv["of"], mods, 5, n + "resid2"), None, sv
    w_next, mods_next = next_layer(next_bufs)
    x2, h_next = resid_norm_fwd(sv["x1"], sv["of"], mods, 5, mods_next, 0, 1, n + "resid2_norm1")
    return x2, (h_next, w_next, mods_next), sv


def _layer_bwd(i, dx2, dof, dm5, sv, tabs, cs, pending, ids, below):
    cos_t, sin_t, g_mat = tabs
    n = f"l{i}b_"
    mods, w = sv["mods"], sv["w"]
    g = {}
    sends, arrs = [], []
    du = d_f_swiglu(dof, w["w_ffn_out"][0], sv["u2"], n + "d_f")
    g["w_ffn_out"] = mm_tn(sv["f"], dof, out_dtype=BF16, name=n + "dw_ffn_out").reshape(NCHIP, FH // NCHIP, D)
    if pending is None:
        dh2 = mm_nt(du, w["w_ffn_in"], name=n + "d_h2")
    else:
        dh2, _, swapped = mm_nt(du, w["w_ffn_in"], name=n + "d_h2", comm=rs_swap(pending))
        for k, g_, s_ in zip(MATMUL_W[:N_FIRST], pending, swapped):
            send, arr = rs_add(g_, s_, ids[0], ids[1], f"{n}rs_add_above_{k}")
            sends.append(send)
            arrs.append(arr)
    g["w_ffn_in"] = mm_tn(sv["h2"], du, cols=True, out_dtype=BF16, name=n + "dw_ffn_in")
    dx1, dm34, dmixed, dm2 = norm_resid_bwd(sv["x1"], mods, dh2, dx2, 4, sv["mixed"], mods, 2, n + "norm2_resid1")
    dya, dyb, dys, dp_gate = d_merged_gate(dmixed, w["w_mix_out"][0], sv["gates"], sv["ys"], n + "d_merged")
    g["w_mix_out"] = mm_tn(sv["merged"], dmixed, out_dtype=BF16, name=n + "dw_mix").reshape(NCHIP, D // NCHIP, D)
    do = mm_nt(dya, w["w_attn_o"], out_dtype=BF16, name=n + "d_o")
    g["w_attn_o"] = _cols_split(mm_tn(sv["o"], dya, out_dtype=BF16, name=n + "dw_attn_o"))
    dhs = mm_nt(dyb, w["w_conf_out"], name=n + "d_hs")
    g["w_conf_out"] = _cols_split(mm_tn(sv["hs"], dyb, out_dtype=BF16, name=n + "dw_conf_out"))
    dz = mm_nt(dys, w["w_sc_out"], name=n + "d_z")
    g["w_sc_out"] = _cols_split(mm_tn(sv["z"], dys, out_dtype=BF16, name=n + "dw_sc_out"))
    done = None
    if ids is None:
        dyc, g["conf_ln_g"], g["conf_ln_b"] = ln_silu_bwd(sv["yc"], w["conf_ln_g"], w["conf_ln_b"], dhs, n + "ln_silu")
    else:
        own = [g[k] for k in MATMUL_W[N_FIRST:]]
        (dyc, g["conf_ln_g"], g["conf_ln_b"]), _, swapped = ln_silu_bwd(sv["yc"], w["conf_ln_g"], w["conf_ln_b"], dhs,
                                                                        n + "ln_silu", comm=rs_swap(own))
        for k, g_, s_ in zip(MATMUL_W[N_FIRST:], own, swapped):
            send, arr = rs_add(g_, s_, ids[0], ids[1], f"{n}rs_add_{k}")
            sends.append(send)
            arrs.append(arr)
    da, dg, dbg, dcg, dxs, g["conf_dw_w"], g["conf_dw_b"], g["sc_dw_w"] = conv_bwd(
        sv["p_main"], w["conf_dw_w"], w["sc_dw_w"], dyc, dz, n + "conv")
    if ids is None:
        dq, dk, dv = attn_bwd(sv["q"], sv["k"], sv["v"], sv["o"], do, n + "attn")
        dp_qkv, dgqk = qkv_bwd(sv["p_main"], cos_t, sin_t, g_mat, w["gq"], w["gk"], dq, dk, dv, n + "qkv")
    else:
        (dq, dk, dv), arrs, _ = attn_bwd(sv["q"], sv["k"], sv["v"], sv["o"], do, n + "attn", comm=rs_ici(sends, arrs))
        (dp_qkv, dgqk), done, _ = qkv_bwd(sv["p_main"], cos_t, sin_t, g_mat, w["gq"], w["gk"], dq, dk, dv, n + "qkv",
                                          comm=rs_d2d(arrs))
    dp_main = jnp.concatenate([dp_qkv, da, dg, dbg, dcg, dxs], axis=1)
    dh = mm_nt(dp_main, w["wi_main"], name=n + "d_h_main")
    dh = mm_nt(dp_gate, w["wi_gate"], acc=dh, name=n + "d_h_gate")
    g["w_in"] = _cols_split(jnp.concatenate([mm_tn(sv["h"], dp_main, out_dtype=BF16, name=n + "dw_in_main"),
                                             mm_tn(sv["h"], dp_gate, out_dtype=BF16, name=n + "dw_in_gate")], axis=1))
    if below is None:
        dx_in, dm01 = norm_mod_bwd(sv["x_in"], mods, dh, dx1, 1, n + "norm1")
        dof_below = dm5_below = None
    else:
        dx_in, dm01, dof_below, dm5_below = norm_resid_bwd(sv["x_in"], mods, dh, dx1, 1, below["of"], below["mods"], 5,
                                                           n + "norm1_resid2")
    parts = jnp.concatenate([dm01, dm2, dm34, dm5], axis=2).reshape(2 * BL, NMOD * D)
    dmod, g["b_ada"] = dmod_assemble(parts, n + "dmod")
    g["w_ada"] = mm_tn(cs, dmod, cols=True, out_dtype=BF16, name=n + "dw_ada")
    g["dcs"] = mm_nt(dmod, w["w_ada"], name=n + "d_cs")
    g["q_norm"] = dgqk[0, :AW].reshape(NQ, HD).sum(axis=0)
    g["k_norm"] = dgqk[0, OFF_K:OFF_K + KVW].reshape(NKV, HD).sum(axis=0)
    return dx_in, dof_below, dm5_below, g, done


def kernel(x, c, ctx, c_ctx, w_ada, b_ada, w_in, q_norm, k_norm, w_attn_o, conf_dw_w, conf_dw_b, conf_ln_g, conf_ln_b, w_conf_out, sc_dw_w, w_sc_out, w_mix_out, w_ffn_in, w_ffn_out, loss_target, m_c_ctx, m_w_ada, m_b_ada, m_w_in, m_q_norm, m_k_norm, m_w_attn_o, m_conf_dw_w, m_conf_dw_b, m_conf_ln_g, m_conf_ln_b, m_w_conf_out, m_sc_dw_w, m_w_sc_out, m_w_mix_out, m_w_ffn_in, m_w_ffn_out, v_c_ctx, v_w_ada, v_b_ada, v_w_in, v_q_norm, v_k_norm, v_w_attn_o, v_conf_dw_w, v_conf_dw_b, v_conf_ln_g, v_conf_ln_b, v_w_conf_out, v_sc_dw_w, v_w_sc_out, v_w_mix_out, v_w_ffn_in, v_w_ffn_out):
    local = dict(c_ctx=c_ctx, w_ada=w_ada, b_ada=b_ada, w_in=w_in, q_norm=q_norm, k_norm=k_norm, w_attn_o=w_attn_o,
                 conf_dw_w=conf_dw_w, conf_dw_b=conf_dw_b, conf_ln_g=conf_ln_g, conf_ln_b=conf_ln_b,
                 w_conf_out=w_conf_out, sc_dw_w=sc_dw_w, w_sc_out=w_sc_out, w_mix_out=w_mix_out, w_ffn_in=w_ffn_in,
                 w_ffn_out=w_ffn_out)
    mom_m = dict(c_ctx=m_c_ctx, w_ada=m_w_ada, b_ada=m_b_ada, w_in=m_w_in, q_norm=m_q_norm, k_norm=m_k_norm,
                 w_attn_o=m_w_attn_o, conf_dw_w=m_conf_dw_w, conf_dw_b=m_conf_dw_b, conf_ln_g=m_conf_ln_g,
                 conf_ln_b=m_conf_ln_b, w_conf_out=m_w_conf_out, sc_dw_w=m_sc_dw_w, w_sc_out=m_w_sc_out,
                 w_mix_out=m_w_mix_out, w_ffn_in=m_w_ffn_in, w_ffn_out=m_w_ffn_out)
    mom_v = dict(c_ctx=v_c_ctx, w_ada=v_w_ada, b_ada=v_b_ada, w_in=v_w_in, q_norm=v_q_norm, k_norm=v_k_norm,
                 w_attn_o=v_w_attn_o, conf_dw_w=v_conf_dw_w, conf_dw_b=v_conf_dw_b, conf_ln_g=v_conf_ln_g,
                 conf_ln_b=v_conf_ln_b, w_conf_out=v_w_conf_out, sc_dw_w=v_sc_dw_w, w_sc_out=v_w_sc_out,
                 w_mix_out=v_w_mix_out, w_ffn_in=v_w_ffn_in, w_ffn_out=v_w_ffn_out)
    order = ("c_ctx", "w_ada", "b_ada", "w_in", "q_norm", "k_norm", "w_attn_o", "conf_dw_w", "conf_dw_b", "conf_ln_g",
             "conf_ln_b", "w_conf_out", "sc_dw_w", "w_sc_out", "w_mix_out", "w_ffn_in", "w_ffn_out")
    core = lax.axis_index("c").astype(jnp.int32)
    chip = (2 * lax.axis_index("x") + lax.axis_index("y")).astype(jnp.int32)

    own = [cast_layers(local[k], chip.reshape(1), "cast_" + k) for k in MATMUL_W]
    layer_bufs = [[own[w][l] for w in range(len(MATMUL_W))] for l in range(DEPTH)]
    conv_shapes = [local[k].shape for k in CONV_W]
    conv_all = all_gather8(_pack_rows([local[k] for k in CONV_W], 8), "gather_conv_taps")
    per_chip = [_unpack(conv_all[2 * s], conv_shapes) for s in range(NCHIP)]
    small = dict(b_ada=b_ada, q_norm=q_norm, k_norm=k_norm, conf_dw_b=conf_dw_b, conf_ln_g=conf_ln_g, conf_ln_b=conf_ln_b)
    for i, k in enumerate(CONV_W):
        small[k] = jnp.concatenate([per_chip[s][i] for s in range(NCHIP)], axis=2)

    loss_local, grad_x, sums, small_g = local_step(x, c, ctx, c_ctx, layer_bufs, small, loss_target,
                                                   ids=(core.reshape(1), chip.reshape(1)))
    loss = lax.psum(loss_local, ("x", "y", "c"))

    small_shapes = [small_g[k].shape for k in SMALL]
    small_sum = sum_leading(all_gather8(_pack_rows([small_g[k] for k in SMALL], 8), "gather_small_grads"), "small_sum")
    small_g = dict(zip(SMALL, _unpack(small_sum, small_shapes)))
    for k in CONV_W:
        width = local[k].shape[2]
        small_g[k] = lax.dynamic_slice_in_dim(small_g[k], chip * width, width, axis=2)

    grad, delta, new_m, new_v = {}, {}, {}, {}
    for wi, k in enumerate(MATMUL_W):
        outs = [lax.empty(local[k].shape, F32) for _ in range(4)]
        grad[k], delta[k], new_m[k], new_v[k] = adamw_layers(local[k], [sums[l][wi] for l in range(DEPTH)], mom_m[k],
                                                             mom_v[k], 0, outs, "adamw_" + k)
    for k in order:
        if k in MATMUL_W:
            continue
        shp = local[k].shape
        view = (1, shp[0]) if len(shp) == 1 else (-1, shp[-1])
        d_, m_, v_ = adamw(local[k].reshape(view), small_g[k].reshape(view), mom_m[k].reshape(view),
                           mom_v[k].reshape(view), "adamw_" + k)
        grad[k], delta[k], new_m[k], new_v[k] = small_g[k], d_.reshape(shp), m_.reshape(shp), v_.reshape(shp)
    return (loss, grad_x, *[grad[k] for k in order], *[delta[k] for k in order], *[new_m[k] for k in order],
            *[new_v[k] for k in order])


def local_step(x, c, ctx, c_ctx, layer_bufs, small, loss_target, ids=None):
    tabs = _rope_tables() + (_group_matrix(),)
    distributed = ids is not None
    first_bufs = [list(b[:N_FIRST]) for b in layer_bufs]
    second_bufs = [list(b[N_FIRST:]) for b in layer_bufs]
    if distributed:
        first_bufs[0], _ = comm_only("gather0_ici", gather_ici(first_bufs[0]))
        first_bufs[0], _ = comm_only("gather0_d2d", gather_d2d(first_bufs[0]))

    cin = jnp.concatenate([c, c_ctx[None], jnp.zeros((8 - BL - 1, D), F32)], axis=0)
    cs = silu_rows(cin, "silu_c")
    xs = jnp.concatenate([ctx, x], axis=1).reshape(NROW, D)
    saved = []

    def make_layer(i, bufs):
        w = _first_weights(bufs, small, i)
        return w, mm_nn(cs, w["w_ada"], bias=small["b_ada"][i][None], name=f"l{i}_mod").reshape(8, 1, NMOD * D)

    w, mods = make_layer(0, first_bufs[0])
    h = norm_mod_fwd(xs, mods, 0, 1, "l0_norm1")
    for i in range(DEPTH):
        last = i == DEPTH - 1

        def next_layer(bufs, i=i):
            return make_layer(i + 1, first_bufs[i + 1] if bufs is None else bufs)

        xs, following, sv = _layer_fwd(i, xs, h, mods, w, tabs, second_bufs[i], None if last else first_bufs[i + 1],
                                       None if last else next_layer, distributed)
        saved.append(sv)
        if following is not None:
            h, w, mods = following
    dxs, loss_lanes = loss_fwd_bwd(xs, loss_target.reshape(BL * SEQ, D), "loss")
    loss_local = 0.5 * jnp.sum(loss_lanes) / D

    grads = [None] * DEPTH
    sums = [[None] * len(MATMUL_W) for _ in range(DEPTH)]
    pending = None
    dof, dm5 = gate_resid_bwd(dxs, saved[-1]["of"], saved[-1]["mods"], 5, "top_resid2")
    for i in reversed(range(DEPTH)):
        dxs, dof, dm5, grads[i], done = _layer_bwd(i, dxs, dof, dm5, saved[i], tabs, cs, pending, ids,
                                                   saved[i - 1] if i > 0 else None)
        partial = [grads[i][k] for k in MATMUL_W]
        if distributed:
            if pending is not None:
                sums[i + 1][:N_FIRST] = done[:N_FIRST]
                done = done[N_FIRST:]
            sums[i][N_FIRST:] = done
            pending = partial[:N_FIRST]
        else:
            sums[i] = partial
    if distributed:
        names = MATMUL_W[:N_FIRST]
        _, swapped = comm_only("rs0_swap", rs_swap(pending))
        sends, arrs = zip(*[rs_add(g_, s_, ids[0], ids[1], "rs0_add_" + k) for k, g_, s_ in zip(names, pending, swapped)])
        arrs, _ = comm_only("rs0_ici", rs_ici(sends, arrs))
        sums[0][:N_FIRST], _ = comm_only("rs0_d2d", rs_d2d(arrs))
    grad_x = dxs.reshape(BL, RE, D)[:, CTX:, :]
    dcin = silu_rows_bwd(cin, jnp.stack([grads[i]["dcs"] for i in range(DEPTH)]), "silu_c_bwd")

    def stack(key):
        return jnp.stack([grads[i][key] for i in range(DEPTH)])

    small_g = dict(c_ctx=dcin[BL], b_ada=stack("b_ada").reshape(DEPTH, NMOD * D), q_norm=stack("q_norm"),
                   k_norm=stack("k_norm"), conf_dw_b=stack("conf_dw_b").reshape(DEPTH, CW),
                   conf_ln_g=stack("conf_ln_g").reshape(DEPTH, CW), conf_ln_b=stack("conf_ln_b").reshape(DEPTH, CW),
                   conf_dw_w=stack("conf_dw_w"), sc_dw_w=stack("sc_dw_w"))
    return loss_local, grad_x, sums, small_g
```

```python
import functools
from typing import Any, Callable, NamedTuple, Sequence

import jax
import jax.numpy as jnp
from jax import lax
from jax.experimental import pallas as pl
from jax.experimental.pallas import tpu as pltpu

F32, BF16 = jnp.float32, jnp.bfloat16
HIGHEST = lax.Precision.HIGHEST

D = 1024
SEQ = 2048
CTX = 256
DEPTH = 4
BL = 4
GRID_W = 64
HD = 64
NQ = 8
NKV = 2
AW = NQ * HD
KVW = NKV * HD
CW = D // 2
CONF_K = 31
SC_K = 3
NMOD = 6
FH = -(-8 * D // (3 * 256)) * 256
EPS = 1e-6
ROPE_THETA = 10000.0
ATTN_SCALE = HD ** -0.5
OFF_K = AW
OFF_V = OFF_K + KVW
OFF_CONF = OFF_V + KVW
OFF_SC = OFF_CONF + 2 * CW
OFF_GATE = OFF_SC + 3 * CW
IN_W = OFF_GATE + 3 * D
QKVW = OFF_CONF
NCHIP = 4

ADAM_LR, ADAM_B1, ADAM_B2, ADAM_EPS, ADAM_WD, ADAM_STEP = 0.001, 0.9, 0.999, 1e-08, 0.01, 10

TM = CTX
RE = CTX + SEQ
TPE = RE // TM
NROW = BL * RE
NT = NROW // TM
LANE = 128
CB = CW // LANE
CONV_CH = 128
PADR = 16
VMEM_LIMIT = 52 * 1024 * 1024

MESH = pl.DeviceIdType.MESH
ANY = pl.BlockSpec(memory_space=pl.ANY)


class CommSpec(NamedTuple):
    ro: Sequence[Any]
    rw: Sequence[Any]
    new: Sequence[Any]
    nsem: int
    program: Callable


def _pcall(body, *, name, out_shape, grid=(), in_specs=None, out_specs=None, scratch=(), sem=None, comm=None,
           aliases=None):
    aliases = dict(aliases or {})
    if not grid:
        return pl.pallas_call(body, name=name, out_shape=out_shape)
    if comm is None:
        params = pltpu.CompilerParams(dimension_semantics=sem, vmem_limit_bytes=VMEM_LIMIT)
        return pl.pallas_call(body, name=name, out_shape=out_shape, grid=grid, in_specs=in_specs, out_specs=out_specs,
                              scratch_shapes=list(scratch), input_output_aliases=aliases, compiler_params=params)

    single = not isinstance(out_shape, (tuple, list))
    out_shapes = (out_shape,) if single else tuple(out_shape)
    out_specs_t = (out_specs,) if single else tuple(out_specs)
    n_in, n_out, n_scr = len(in_specs), len(out_shapes), len(scratch)
    n_ro, n_rw, n_new = len(comm.ro), len(comm.rw), len(comm.new)

    def carrier(*refs):
        ins = refs[:n_in]
        ro_refs = refs[n_in:n_in + n_ro]
        o0 = n_in + n_ro + n_rw
        outs = refs[o0:o0 + n_out]
        rw_refs = refs[o0 + n_out:o0 + n_out + n_rw]
        new_refs = refs[o0 + n_out + n_rw:o0 + n_out + n_rw + n_new]
        s0 = o0 + n_out + n_rw + n_new
        scr = refs[s0:s0 + n_scr]
        send_sems, recv_sems = refs[s0 + n_scr:]
        first = functools.reduce(jnp.logical_and, [pl.program_id(a) == 0 for a in range(len(grid))])
        last = functools.reduce(jnp.logical_and, [pl.program_id(a) == grid[a] - 1 for a in range(len(grid))])
        starts, arrivals = comm.program(ro_refs, rw_refs, new_refs, send_sems, recv_sems)

        @pl.when(first)
        def _():
            for cp in starts:
                cp.start()

        body(*ins, *outs, *scr)

        @pl.when(last)
        def _():
            for cp in arrivals:
                cp.wait_recv()
            for cp in starts:
                cp.wait_send()

    def call(*args):
        rw_shapes = tuple(jax.ShapeDtypeStruct(a.shape, a.dtype) for a in comm.rw)
        res = pl.pallas_call(
            carrier, name=name, out_shape=out_shapes + rw_shapes + tuple(comm.new), grid=grid,
            in_specs=list(in_specs) + [ANY] * (n_ro + n_rw),
            out_specs=out_specs_t + (ANY,) * (n_rw + n_new),
            scratch_shapes=list(scratch) + [pltpu.SemaphoreType.DMA((comm.nsem,)), pltpu.SemaphoreType.DMA((comm.nsem,))],
            input_output_aliases={**aliases, **{n_in + n_ro + i: n_out + i for i in range(n_rw)}},
            compiler_params=pltpu.CompilerParams(dimension_semantics=("arbitrary",) * len(grid),
                                                 vmem_limit_bytes=VMEM_LIMIT))(*args, *comm.ro, *comm.rw)
        compute = res[0] if single else tuple(res[:n_out])
        return compute, list(res[n_out:n_out + n_rw]), list(res[n_out + n_rw:])

    return call


def comm_only(name, comm):
    n_ro, n_rw, n_new = len(comm.ro), len(comm.rw), len(comm.new)

    def body(*refs):
        ro_refs = refs[:n_ro]
        rw_refs = refs[n_ro + n_rw:n_ro + 2 * n_rw]
        new_refs = refs[n_ro + 2 * n_rw:n_ro + 2 * n_rw + n_new]
        send_sems, recv_sems = refs[n_ro + 2 * n_rw + n_new:]
        starts, arrivals = comm.program(ro_refs, rw_refs, new_refs, send_sems, recv_sems)
        for cp in starts:
            cp.start()
        for cp in arrivals:
            cp.wait_recv()
        for cp in starts:
            cp.wait_send()

    rw_shapes = tuple(jax.ShapeDtypeStruct(a.shape, a.dtype) for a in comm.rw)
    res = pl.pallas_call(body, name=name, out_shape=rw_shapes + tuple(comm.new), in_specs=[ANY] * (n_ro + n_rw),
                         out_specs=(ANY,) * (n_rw + n_new), input_output_aliases={n_ro + i: i for i in range(n_rw)},
                         scratch_shapes=[pltpu.SemaphoreType.DMA((comm.nsem,)), pltpu.SemaphoreType.DMA((comm.nsem,))])(
                             *comm.ro, *comm.rw)
    return list(res[:n_rw]), list(res[n_rw:])


def _pick(n, cands):
    for t in cands:
        if n % t == 0:
            return t
    return n


def _seg(t):
    return jnp.where(t % TPE == 0, BL, t // TPE)


def _slot(t):
    return 2 * (t // TPE) + jnp.where(t % TPE == 0, 0, 1)


def _sigmoid(x):
    return 1.0 / (1.0 + jnp.exp(-x))


MM_BUDGET = 40 * 1024 * 1024
N_TILE_CAP = 1664


def _tile(n, cap=N_TILE_CAP):
    if n <= cap:
        return n
    for t in range(cap - cap % LANE, 0, -LANE):
        if n % t == 0:
            return t
    return n


def _row_tile(m, bytes_of):
    for tm in (1024, 512, 256, 128):
        if m % tm == 0 and bytes_of(tm) <= MM_BUDGET:
            return tm
    return m


def _w_dims(w):
    arr, kind = w
    if kind == "cols":
        return arr.shape[1], NCHIP * arr.shape[2]
    return arr.shape


def _sz(dtype):
    return jnp.dtype(dtype).itemsize


def mm_nn(a, w, *, bias=None, out_dtype=F32, name):
    arr, kind = w
    m, k = a.shape
    _, n = _w_dims(w)
    tn = _tile(arr.shape[2]) if kind == "cols" else _tile(n)
    tm = _row_tile(m, lambda t: 2 * (t * k * _sz(a.dtype) + k * tn * 2 + t * tn * _sz(out_dtype)))
    if kind == "mat":
        b_spec = pl.BlockSpec((k, tn), lambda j, i: (0, j))
    else:
        per = arr.shape[2] // tn
        b_spec = pl.BlockSpec((None, k, tn), lambda j, i: (j // per, 0, j % per))
    has_bias = bias is not None

    def body(*refs):
        out = jnp.dot(refs[0][...].astype(BF16), refs[1][...].astype(BF16), preferred_element_type=F32)
        if has_bias:
            out = out + refs[2][...]
        refs[-1][...] = out.astype(out_dtype)

    in_specs = [pl.BlockSpec((tm, k), lambda j, i: (i, 0)), b_spec]
    args = [a, arr]
    if has_bias:
        in_specs.append(pl.BlockSpec((1, tn), lambda j, i: (0, j)))
        args.append(bias)
    return _pcall(body, name=name, out_shape=jax.ShapeDtypeStruct((m, n), out_dtype), grid=(n // tn, m // tm),
                  in_specs=in_specs, out_specs=pl.BlockSpec((tm, tn), lambda j, i: (i, j)),
                  sem=("parallel", "parallel"))(*args)


def mm_nt(a, w, *, acc=None, out_dtype=F32, name, comm=None):
    arr, kind = w
    kdim, _ = _w_dims(w)
    has_acc = acc is not None
    tk = _tile(kdim, 1408)
    if kind == "cols":
        c = arr.shape[2]
        m = a.shape[-2]
        if a.ndim == 3:
            a_spec = lambda t: pl.BlockSpec((None, t, c), lambda j, i, s: (s // 2, i, s % 2))
        else:
            a_spec = lambda t: pl.BlockSpec((t, c), lambda j, i, s: (i, s))
        tm = _row_tile(m, lambda t: 2 * (t * c * _sz(a.dtype) + tk * c * 2 + t * tk * _sz(out_dtype)) + t * tk * 4)

        def body(a_ref, b_ref, o_ref, acc_ref):
            s = pl.program_id(2)

            @pl.when(s == 0)
            def _():
                acc_ref[...] = jnp.zeros_like(acc_ref)

            acc_ref[...] += lax.dot_general(a_ref[...].astype(BF16), b_ref[...], (((1,), (1,)), ((), ())),
                                            preferred_element_type=F32)

            @pl.when(s == NCHIP - 1)
            def _():
                o_ref[...] = acc_ref[...].astype(out_dtype)

        return _pcall(body, name=name, out_shape=jax.ShapeDtypeStruct((m, kdim), out_dtype),
                      grid=(kdim // tk, m // tm, NCHIP),
                      in_specs=[a_spec(tm), pl.BlockSpec((None, tk, c), lambda j, i, s: (s, j, 0))],
                      out_specs=pl.BlockSpec((tm, tk), lambda j, i, s: (i, j)),
                      scratch=[pltpu.VMEM((tm, tk), F32)],
                      sem=("parallel", "parallel", "arbitrary"), comm=comm)(a, arr)

    m, n = a.shape
    tm = _row_tile(m, lambda t: 2 * (t * n * _sz(a.dtype) + tk * n * 2 + t * tk * (_sz(out_dtype) + 4 * has_acc))
                   + tk * n * 2)
    b_spec = pl.BlockSpec((tk, n), lambda j, i: (j, 0))

    def body(*refs):
        wt_ref = refs[-1]

        @pl.when(pl.program_id(1) == 0)
        def _():
            wt_ref[...] = refs[1][...].astype(BF16).T

        out = jnp.dot(refs[0][...].astype(BF16), wt_ref[...], preferred_element_type=F32)
        if has_acc:
            out = out + refs[2][...]
        refs[-2][...] = out.astype(out_dtype)

    in_specs = [pl.BlockSpec((tm, n), lambda j, i: (i, 0)), b_spec]
    args = [a, arr]
    if has_acc:
        in_specs.append(pl.BlockSpec((tm, tk), lambda j, i: (i, j)))
        args.append(acc)
    return _pcall(body, name=name, out_shape=jax.ShapeDtypeStruct((m, kdim), out_dtype), grid=(kdim // tk, m // tm),
                  in_specs=in_specs, out_specs=pl.BlockSpec((tm, tk), lambda j, i: (i, j)),
                  scratch=[pltpu.VMEM((n, tk), BF16)], sem=("parallel", "arbitrary"))(*args)


def mm_tn(a, b, *, cols=False, out_dtype=F32, name):
    rows, k = a.shape
    halves = b.ndim == 3
    n = 2 * b.shape[2] if halves else b.shape[1]
    odt = out_dtype
    if cols:
        c = n // NCHIP
        tn, tk = _tile(c), k
        per = c // tn
        out_spec = pl.BlockSpec((None, tk, tn), lambda i, j, r: (j // per, 0, j % per))
        out_shape = jax.ShapeDtypeStruct((NCHIP, k, c), odt)
    else:
        tn, tk = _tile(n), _tile(k, 1408)
        out_spec = pl.BlockSpec((tk, tn), lambda i, j, r: (i, j))
        out_shape = jax.ShapeDtypeStruct((k, n), odt)
    tr = _row_tile(rows, lambda t: 2 * (t * tk * _sz(a.dtype) + t * tn * _sz(b.dtype) + tk * tn * _sz(odt)) + tk * tn * 4)
    nsteps = rows // tr

    def body(*refs):
        a_ref, b_ref = refs[0], refs[1]
        o_ref, acc_ref = refs[-2], refs[-1]
        r = pl.program_id(2)

        @pl.when(r == 0)
        def _():
            acc_ref[...] = jnp.zeros_like(acc_ref)

        acc_ref[...] += lax.dot_general(a_ref[...].astype(BF16), b_ref[...].astype(BF16), (((0,), (0,)), ((), ())),
                                        preferred_element_type=F32)

        @pl.when(r == nsteps - 1)
        def _():
            o_ref[...] = acc_ref[...].astype(odt)

    if halves:
        per_half = (n // 2) // tn
        b_spec = pl.BlockSpec((None, tr, tn), lambda i, j, r: (j // per_half, r, j % per_half))
    else:
        b_spec = pl.BlockSpec((tr, tn), lambda i, j, r: (r, j))
    return _pcall(body, name=name, out_shape=out_shape, grid=(k // tk, n // tn, nsteps),
                  in_specs=[pl.BlockSpec((tr, tk), lambda i, j, r: (r, i)), b_spec], out_specs=out_spec,
                  scratch=[pltpu.VMEM((tk, tn), F32)], sem=("parallel", "parallel", "arbitrary"))(a, b)


def ffn_in_swiglu(h, w_in, name, comm=None):
    m, k = h.shape
    c = w_in.shape[2]
    tm = 512

    def body(h_ref, wa_ref, wb_ref, f_ref, u_ref):
        for r0 in range(0, tm, tm // 2):
            rows = slice(r0, r0 + tm // 2)
            hv = h_ref[rows, :]
            a = jnp.dot(hv, wa_ref[...], preferred_element_type=F32)
            b = jnp.dot(hv, wb_ref[...], preferred_element_type=F32)
            f_ref[rows, :] = (a * _sigmoid(a) * b).astype(BF16)
            u_ref[0, rows, :] = a.astype(BF16)
            u_ref[1, rows, :] = b.astype(BF16)

    return _pcall(body, name=name,
                  out_shape=(jax.ShapeDtypeStruct((m, FH), BF16), jax.ShapeDtypeStruct((2, m, FH), BF16)),
                  grid=(2, m // tm),
                  in_specs=[pl.BlockSpec((tm, k), lambda j, i: (i, 0)),
                            pl.BlockSpec((None, k, c), lambda j, i: (j, 0, 0)),
                            pl.BlockSpec((None, k, c), lambda j, i: (2 + j, 0, 0))],
                  out_specs=(pl.BlockSpec((tm, c), lambda j, i: (i, j)), pl.BlockSpec((2, tm, c), lambda j, i: (0, i, j))),
                  sem=("parallel", "parallel"), comm=comm)(h, w_in, w_in)


def d_f_swiglu(dof, w_out, u2, name):
    m, k = dof.shape
    c = FH // 2
    tm = 512

    n_i = m // tm
    total = 2 * n_i
    ring = 3

    def body(d_ref, w_ref, u_hbm, du_ref, wt_ref, ubuf, usem):
        j, i = pl.program_id(0), pl.program_id(1)
        s = j * n_i + i

        def fetch(step, half):
            row0 = pl.multiple_of((step % n_i) * tm, tm)
            return pltpu.make_async_copy(u_hbm.at[:, pl.ds(row0, tm), pl.ds(half * c, c)], ubuf.at[step % ring],
                                         usem.at[step % ring])

        def start(step):
            for half in range(2):
                @pl.when(step // n_i == half)
                def _():
                    fetch(step, half).start()

        @pl.when(s == 0)
        def _():
            start(s)
            start(s + 1)

        @pl.when(s + 2 < total)
        def _():
            start(s + 2)

        @pl.when(i == 0)
        def _():
            wt_ref[...] = w_ref[...].T

        fetch(s, 0).wait()
        slot = s % ring
        for r0 in range(0, tm, tm // 2):
            rows = slice(r0, r0 + tm // 2)
            df = jnp.dot(d_ref[rows, :], wt_ref[...], preferred_element_type=F32)
            a, b = ubuf[slot, 0, rows, :].astype(F32), ubuf[slot, 1, rows, :].astype(F32)
            sg = _sigmoid(a)
            du_ref[0, rows, :] = (df * b * (sg * (1.0 + a * (1.0 - sg)))).astype(BF16)
            du_ref[1, rows, :] = (df * a * sg).astype(BF16)

    ublk = pl.BlockSpec((2, tm, c), lambda j, i: (0, i, j))
    return _pcall(body, name=name, out_shape=jax.ShapeDtypeStruct((2, m, FH), BF16), grid=(2, n_i),
                  in_specs=[pl.BlockSpec((tm, k), lambda j, i: (i, 0)), pl.BlockSpec((c, k), lambda j, i: (j, 0)), ANY],
                  out_specs=ublk,
                  scratch=[pltpu.VMEM((k, c), w_out.dtype), pltpu.VMEM((ring, 2, tm, c), u2.dtype),
                           pltpu.SemaphoreType.DMA((ring,))],
                  sem=("arbitrary", "arbitrary"))(dof, w_out, u2)


GATE_TN = 512


def gate_mm_fwd(h, wi_gate, o, hs, z, wo, wc, ws, name):
    m, k = h.shape
    tm, tn = 512, min(GATE_TN, D)
    nj = D // tn

    def body(h_ref, g0_ref, g1_ref, g2_ref, o_ref, hs_ref, z_ref, wo_ref, wc_ref, ws_ref, m_ref, g_ref, y_ref):
        hv = h_ref[...]
        acc = jnp.zeros((tm, tn), F32)
        for g, (gw_ref, x_ref, w_ref) in enumerate(((g0_ref, o_ref, wo_ref), (g1_ref, hs_ref, wc_ref),
                                                    (g2_ref, z_ref, ws_ref))):
            gate = _sigmoid(jnp.dot(hv, gw_ref[...], preferred_element_type=F32))
            y = jnp.dot(x_ref[...], w_ref[...], preferred_element_type=F32)
            acc += gate * y
            g_ref[g] = gate.astype(BF16)
            y_ref[g] = y.astype(BF16)
        m_ref[...] = acc.astype(BF16)

    def gate_w(g):
        return pl.BlockSpec((k, tn), lambda j, i: (0, g * nj + j))

    def branch(width):
        return pl.BlockSpec((tm, width), lambda j, i: (i, 0))

    def branch_w(width):
        return pl.BlockSpec((width, tn), lambda j, i: (0, j))

    stacked = pl.BlockSpec((3, tm, tn), lambda j, i: (0, i, j))
    sds3 = jax.ShapeDtypeStruct((3, m, D), BF16)
    return _pcall(body, name=name, out_shape=(jax.ShapeDtypeStruct((m, D), BF16), sds3, sds3), grid=(nj, m // tm),
                  in_specs=[pl.BlockSpec((tm, k), lambda j, i: (i, 0)), gate_w(0), gate_w(1), gate_w(2),
                            branch(o.shape[1]), branch(hs.shape[1]), branch(z.shape[1]),
                            branch_w(wo.shape[0]), branch_w(wc.shape[0]), branch_w(ws.shape[0])],
                  out_specs=(pl.BlockSpec((tm, tn), lambda j, i: (i, j)), stacked, stacked),
                  sem=("parallel", "parallel"))(h, wi_gate, wi_gate, wi_gate, o, hs, z, wo, wc, ws)


def d_merged_gate(dmixed, w_mix, gates, ys, name):
    m, k = dmixed.shape
    tm = 256

    def body(d_ref, w_ref, g_ref, y_ref, da_ref, db_ref, ds_ref, dp_ref, wt_ref):
        @pl.when(pl.program_id(0) == 0)
        def _():
            wt_ref[...] = w_ref[...].T

        for r0 in range(0, tm, tm // 2):
            rows = slice(r0, r0 + tm // 2)
            dm = jnp.dot(d_ref[rows, :], wt_ref[...], preferred_element_type=F32)
            for g, dy_ref in enumerate((da_ref, db_ref, ds_ref)):
                gate = g_ref[g, rows, :].astype(F32)
                dy_ref[rows, :] = (dm * gate).astype(BF16)
                dp_ref[rows, g * D:(g + 1) * D] = (dm * y_ref[g, rows, :].astype(F32) * gate * (1.0 - gate)).astype(BF16)

    stacked = pl.BlockSpec((3, tm, D), lambda i: (0, i, 0))
    row = pl.BlockSpec((tm, D), lambda i: (i, 0))
    sds = jax.ShapeDtypeStruct((m, D), BF16)
    return _pcall(body, name=name, out_shape=(sds, sds, sds, jax.ShapeDtypeStruct((m, 3 * D), BF16)), grid=(m // tm,),
                  in_specs=[pl.BlockSpec((tm, k), lambda i: (i, 0)), pl.BlockSpec((D, k), lambda i: (0, 0)),
                            stacked, stacked],
                  out_specs=(row, row, row, pl.BlockSpec((tm, 3 * D), lambda i: (i, 0))),
                  scratch=[pltpu.VMEM((k, D), w_mix.dtype)], sem=("arbitrary",))(dmixed, w_mix, gates, ys)


def _mods_spec():
    return pl.BlockSpec((1, 1, NMOD * D), lambda t: (_seg(t), 0, 0))


def _rows(width):
    return pl.BlockSpec((TM, width), lambda t: (t, 0))


def norm_mod_fwd(x, mods, k_sh, k_sc, name):
    def body(x_ref, m_ref, h_ref):
        x = x_ref[...]
        r = lax.rsqrt(jnp.mean(x * x, axis=-1, keepdims=True) + EPS)
        sh = m_ref[0, :, k_sh * D:(k_sh + 1) * D]
        sc = m_ref[0, :, k_sc * D:(k_sc + 1) * D]
        h_ref[...] = (x * r * (1.0 + sc) + sh).astype(BF16)

    return _pcall(body, name=name, out_shape=jax.ShapeDtypeStruct((NROW, D), BF16), grid=(NT,),
                  in_specs=[_rows(D), _mods_spec()], out_specs=_rows(D), sem=("parallel",))(x, mods)


def _accumulate_slot(t, ref, part):
    first = (t % TPE) <= 1

    @pl.when(first)
    def _():
        ref[0] = part

    @pl.when(jnp.logical_not(first))
    def _():
        ref[0] += part


def norm_mod_bwd(x, mods, dh, dres, k_sc, name):
    def body(x_ref, m_ref, dh_ref, dres_ref, dx_ref, dp_ref):
        t = pl.program_id(0)
        x = x_ref[...]
        r = lax.rsqrt(jnp.mean(x * x, axis=-1, keepdims=True) + EPS)
        xn = x * r
        sc = m_ref[0, :, k_sc * D:(k_sc + 1) * D]
        dh = dh_ref[...]
        dxn = dh * (1.0 + sc)
        dx_ref[...] = r * (dxn - xn * jnp.mean(dxn * xn, axis=-1, keepdims=True)) + dres_ref[...]
        part = jnp.concatenate([jnp.sum(dh, axis=0, keepdims=True), jnp.sum(dh * xn, axis=0, keepdims=True)], axis=1)
        _accumulate_slot(t, dp_ref, part)

    return _pcall(body, name=name,
                  out_shape=(jax.ShapeDtypeStruct((NROW, D), F32), jax.ShapeDtypeStruct((2 * BL, 1, 2 * D), F32)),
                  grid=(NT,), in_specs=[_rows(D), _mods_spec(), _rows(D), _rows(D)],
                  out_specs=(_rows(D), pl.BlockSpec((1, 1, 2 * D), lambda t: (_slot(t), 0, 0))),
                  sem=("arbitrary",))(x, mods, dh, dres)


def resid_norm_fwd(x, y, mods_g, k_g, mods_n, k_sh, k_sc, name):
    def body(x_ref, y_ref, mg_ref, mn_ref, x1_ref, h_ref):
        x1 = x_ref[...] + mg_ref[0, :, k_g * D:(k_g + 1) * D] * y_ref[...]
        x1_ref[...] = x1
        r = lax.rsqrt(jnp.mean(x1 * x1, axis=-1, keepdims=True) + EPS)
        sh = mn_ref[0, :, k_sh * D:(k_sh + 1) * D]
        sc = mn_ref[0, :, k_sc * D:(k_sc + 1) * D]
        h_ref[...] = (x1 * r * (1.0 + sc) + sh).astype(BF16)

    return _pcall(body, name=name,
                  out_shape=(jax.ShapeDtypeStruct((NROW, D), F32), jax.ShapeDtypeStruct((NROW, D), BF16)), grid=(NT,),
                  in_specs=[_rows(D), _rows(D), _mods_spec(), _mods_spec()], out_specs=(_rows(D), _rows(D)),
                  sem=("parallel",))(x, y, mods_g, mods_n)


def norm_resid_bwd(x, mods_n, dh, dres, k_sc, y, mods_g, k_g, name):
    def body(x_ref, mn_ref, dh_ref, dres_ref, y_ref, mg_ref, dx_ref, dpn_ref, dy_ref, dpg_ref):
        t = pl.program_id(0)
        x = x_ref[...]
        r = lax.rsqrt(jnp.mean(x * x, axis=-1, keepdims=True) + EPS)
        xn = x * r
        sc = mn_ref[0, :, k_sc * D:(k_sc + 1) * D]
        dh = dh_ref[...]
        dxn = dh * (1.0 + sc)
        dx = r * (dxn - xn * jnp.mean(dxn * xn, axis=-1, keepdims=True)) + dres_ref[...]
        dx_ref[...] = dx
        dy_ref[...] = (dx * mg_ref[0, :, k_g * D:(k_g + 1) * D]).astype(BF16)
        part = jnp.concatenate([jnp.sum(dh, axis=0, keepdims=True), jnp.sum(dh * xn, axis=0, keepdims=True)], axis=1)
        _accumulate_slot(t, dpn_ref, part)
        _accumulate_slot(t, dpg_ref, jnp.sum(dx * y_ref[...], axis=0, keepdims=True))

    def slot(width):
        return pl.BlockSpec((1, 1, width), lambda t: (_slot(t), 0, 0))

    return _pcall(body, name=name,
                  out_shape=(jax.ShapeDtypeStruct((NROW, D), F32), jax.ShapeDtypeStruct((2 * BL, 1, 2 * D), F32),
                             jax.ShapeDtypeStruct((NROW, D), BF16), jax.ShapeDtypeStruct((2 * BL, 1, D), F32)),
                  grid=(NT,), in_specs=[_rows(D), _mods_spec(), _rows(D), _rows(D), _rows(D), _mods_spec()],
                  out_specs=(_rows(D), slot(2 * D), _rows(D), slot(D)), sem=("arbitrary",))(x, mods_n, dh, dres, y, mods_g)


def gate_resid_fwd(x, y, mods, k_g, name):
    def body(x_ref, y_ref, m_ref, o_ref):
        o_ref[...] = x_ref[...] + m_ref[0, :, k_g * D:(k_g + 1) * D] * y_ref[...]

    return _pcall(body, name=name, out_shape=jax.ShapeDtypeStruct((NROW, D), F32), grid=(NT,),
                  in_specs=[_rows(D), _rows(D), _mods_spec()], out_specs=_rows(D), sem=("parallel",))(x, y, mods)


def gate_resid_bwd(dx, y, mods, k_g, name):
    def body(dx_ref, y_ref, m_ref, dy_ref, dp_ref):
        t = pl.program_id(0)
        dx = dx_ref[...]
        dy_ref[...] = (dx * m_ref[0, :, k_g * D:(k_g + 1) * D]).astype(BF16)
        _accumulate_slot(t, dp_ref, jnp.sum(dx * y_ref[...], axis=0, keepdims=True))

    return _pcall(body, name=name,
                  out_shape=(jax.ShapeDtypeStruct((NROW, D), BF16), jax.ShapeDtypeStruct((2 * BL, 1, D), F32)),
                  grid=(NT,), in_specs=[_rows(D), _rows(D), _mods_spec()],
                  out_specs=(_rows(D), pl.BlockSpec((1, 1, D), lambda t: (_slot(t), 0, 0))),
                  sem=("arbitrary",))(dx, y, mods)


def _swap16(y, lo16):
    return jnp.where(lo16, pltpu.roll(y, LANE - 16, 1), pltpu.roll(y, 16, 1))


def _group_mean(v, g_mat):
    return jnp.dot(v, g_mat, precision=HIGHEST, preferred_element_type=F32)


def qkv_fwd(p_main, cos_t, sin_t, g_mat, gq, gk, name):
    def body(p_ref, cos_ref, sin_ref, g_ref, gq_ref, gk_ref, q_ref, k_ref, v_ref):
        cos, sin, g_mat_v = cos_ref[...], sin_ref[...], g_ref[...]
        lo16 = (lax.broadcasted_iota(jnp.int32, (TM, LANE), 1) % 32) < 16

        def block(xb, g):
            xb = xb.astype(F32)
            r = lax.rsqrt(_group_mean(xb * xb, g_mat_v) + EPS)
            y = xb * r * g
            return y * cos + _swap16(y, lo16) * sin

        for j in range(AW // LANE):
            q_ref[:, j * LANE:(j + 1) * LANE] = (block(p_ref[:, j * LANE:(j + 1) * LANE], gq_ref[...])
                                                 * ATTN_SCALE).astype(BF16)
        lo = lax.broadcasted_iota(jnp.int32, (TM, LANE), 1) < HD
        for src, dst_ref in ((block(p_ref[:, OFF_K:OFF_K + LANE], gk_ref[...]), k_ref), (p_ref[:, OFF_V:OFF_V + LANE].astype(F32), v_ref)):
            swapped = pltpu.roll(src, HD, 1)
            dst_ref[:, 0:LANE] = jnp.where(lo, src, swapped).astype(BF16)
            dst_ref[:, LANE:2 * LANE] = jnp.where(lo, swapped, src).astype(BF16)

    tab = pl.BlockSpec((TM, LANE), lambda t: (t % TPE, 0))
    small = pl.BlockSpec((1, LANE), lambda t: (0, 0))
    return _pcall(body, name=name,
                  out_shape=(jax.ShapeDtypeStruct((NROW, AW), BF16), jax.ShapeDtypeStruct((NROW, 2 * KVW), BF16),
                             jax.ShapeDtypeStruct((NROW, 2 * KVW), BF16)),
                  grid=(NT,),
                  in_specs=[_rows(QKVW), tab, tab, pl.BlockSpec((LANE, LANE), lambda t: (0, 0)), small, small],
                  out_specs=(_rows(AW), _rows(2 * KVW), _rows(2 * KVW)),
                  sem=("parallel",))(p_main, cos_t, sin_t, g_mat, gq, gk)


def qkv_bwd(p_main, cos_t, sin_t, g_mat, gq, gk, dq, dk, dv, name, comm=None):
    def body(p_ref, cos_ref, sin_ref, g_ref, gq_ref, gk_ref, dq_ref, dk_ref, dv_ref, dp_ref, dg_ref):
        t = pl.program_id(0)
        cos, sin, g_mat_v = cos_ref[...], sin_ref[...], g_ref[...]
        lo16 = (lax.broadcasted_iota(jnp.int32, (TM, LANE), 1) % 32) < 16

        def block(xb, g, dyr):
            xb = xb.astype(F32)
            r = lax.rsqrt(_group_mean(xb * xb, g_mat_v) + EPS)
            xn = xb * r
            dy = dyr * cos + _swap16(dyr * sin, lo16)
            dgl = jnp.sum(dy * xn, axis=0, keepdims=True)
            dxn = dy * g
            return r * (dxn - xn * _group_mean(dxn * xn, g_mat_v)), dgl

        parts = []
        for j in range(AW // LANE):
            sl = slice(j * LANE, (j + 1) * LANE)
            dxb, dgl = block(p_ref[:, sl], gq_ref[...], dq_ref[:, sl] * ATTN_SCALE)
            dp_ref[:, sl] = dxb.astype(BF16)
            parts.append(dgl)
        lo = lax.broadcasted_iota(jnp.int32, (TM, LANE), 1) < HD

        def fold(d_ref):
            d0, d1 = d_ref[:, 0:LANE], d_ref[:, LANE:2 * LANE]
            return jnp.where(lo, d0 + pltpu.roll(d0, HD, 1), d1 + pltpu.roll(d1, HD, 1))

        dxb, dgl = block(p_ref[:, OFF_K:OFF_K + LANE], gk_ref[...], fold(dk_ref))
        dp_ref[:, OFF_K:OFF_K + LANE] = dxb.astype(BF16)
        parts.append(dgl)
        parts.append(jnp.zeros((1, LANE), F32))
        dp_ref[:, OFF_V:OFF_V + LANE] = fold(dv_ref).astype(BF16)
        part = jnp.concatenate(parts, axis=1)

        @pl.when(t == 0)
        def _():
            dg_ref[...] = part

        @pl.when(t != 0)
        def _():
            dg_ref[...] += part

    tab = pl.BlockSpec((TM, LANE), lambda t: (t % TPE, 0))
    small = pl.BlockSpec((1, LANE), lambda t: (0, 0))
    return _pcall(body, name=name,
                  out_shape=(jax.ShapeDtypeStruct((NROW, QKVW), BF16), jax.ShapeDtypeStruct((1, QKVW), F32)),
                  grid=(NT,),
                  in_specs=[_rows(QKVW), tab, tab, pl.BlockSpec((LANE, LANE), lambda t: (0, 0)), small, small,
                            _rows(AW), _rows(2 * KVW), _rows(2 * KVW)],
                  out_specs=(_rows(QKVW), pl.BlockSpec((1, QKVW), lambda t: (0, 0))),
                  sem=("arbitrary",), comm=comm)(p_main, cos_t, sin_t, g_mat, gq, gk, dq, dk, dv)


def _layer_norm_parts(yc):
    mu = jnp.mean(yc, axis=-1, keepdims=True)
    xc = yc - mu
    rs = lax.rsqrt(jnp.mean(xc * xc, axis=-1, keepdims=True) + EPS)
    return xc * rs, rs


def ln_silu_fwd(yc, g, b, name):
    def body(y_ref, g_ref, b_ref, o_ref):
        nrm, _ = _layer_norm_parts(y_ref[...])
        ln = nrm * g_ref[...] + b_ref[...]
        o_ref[...] = (ln * _sigmoid(ln)).astype(BF16)

    vec = pl.BlockSpec((1, CW), lambda t: (0, 0))
    return _pcall(body, name=name, out_shape=jax.ShapeDtypeStruct((NROW, CW), BF16), grid=(NT,),
                  in_specs=[_rows(CW), vec, vec], out_specs=_rows(CW), sem=("parallel",))(yc, g, b)


def ln_silu_bwd(yc, g, b, dhs, name, comm=None):
    def body(y_ref, g_ref, b_ref, dh_ref, dy_ref, dg_ref, db_ref):
        t = pl.program_id(0)
        nrm, rs = _layer_norm_parts(y_ref[...])
        ln = nrm * g_ref[...] + b_ref[...]
        sg = _sigmoid(ln)
        dln = dh_ref[...] * (sg * (1.0 + ln * (1.0 - sg)))
        dn = dln * g_ref[...]
        dy_ref[...] = rs * (dn - jnp.mean(dn, axis=-1, keepdims=True)
                            - nrm * jnp.mean(dn * nrm, axis=-1, keepdims=True))
        pg = jnp.sum(dln * nrm, axis=0, keepdims=True)
        pb = jnp.sum(dln, axis=0, keepdims=True)

        @pl.when(t == 0)
        def _():
            dg_ref[...] = pg
            db_ref[...] = pb

        @pl.when(t != 0)
        def _():
            dg_ref[...] += pg
            db_ref[...] += pb

    vec = pl.BlockSpec((1, CW), lambda t: (0, 0))
    return _pcall(body, name=name,
                  out_shape=(jax.ShapeDtypeStruct((NROW, CW), F32), jax.ShapeDtypeStruct((1, CW), F32),
                             jax.ShapeDtypeStruct((1, CW), F32)),
                  grid=(NT,), in_specs=[_rows(CW), vec, vec, _rows(CW)], out_specs=(_rows(CW), vec, vec),
                  sem=("arbitrary",), comm=comm)(yc, g, b, dhs)


def loss_fwd_bwd(y, target, name):
    def body(y_ref, t_ref, dy_ref, l_ref):
        t = pl.program_id(0)
        latent = (t % TPE) != 0
        err = jnp.where(latent, y_ref[...] - t_ref[...], 0.0)
        dy_ref[...] = err * (1.0 / D)
        part = jnp.sum(err * err, axis=0, keepdims=True)

        @pl.when(t == 0)
        def _():
            l_ref[...] = part

        @pl.when(t != 0)
        def _():
            l_ref[...] += part

    tgt = pl.BlockSpec((TM, D), lambda t: ((t // TPE) * (TPE - 1) + jnp.maximum(t % TPE - 1, 0), 0))
    return _pcall(body, name=name,
                  out_shape=(jax.ShapeDtypeStruct((NROW, D), F32), jax.ShapeDtypeStruct((1, D), F32)),
                  grid=(NT,), in_specs=[_rows(D), tgt], out_specs=(_rows(D), pl.BlockSpec((1, D), lambda t: (0, 0))),
                  sem=("arbitrary",))(y, target)


QB_PER_KV = AW // LANE // NKV


def _softmax_parts(qm, k):
    s = lax.dot_general(qm, k, (((1,), (1,)), ((), ())), preferred_element_type=F32)
    e = jnp.exp(s - jnp.max(s, axis=-1, keepdims=True))
    return e, 1.0 / jnp.sum(e, axis=-1, keepdims=True)


def _lane_halves():
    lo = lax.broadcasted_iota(jnp.int32, (TM, LANE), 1) < HD
    return lo, jnp.logical_not(lo)


def _stack_heads(x, halves):
    zero = jnp.zeros_like(x)
    return jnp.concatenate([jnp.where(halves[0], x, zero), jnp.where(halves[1], x, zero)], axis=0)


def attn_fwd(q, k, v, name, comm=None):
    def body(q_ref, k_ref, v_ref, o_ref):
        t = pl.program_id(2)
        halves = _lane_halves()

        def run(nk):
            kv, vv = k_ref[0:nk, :], v_ref[0:nk, :]
            for j in range(QB_PER_KV):
                lanes = slice(j * LANE, (j + 1) * LANE)
                e, rinv = _softmax_parts(_stack_heads(q_ref[:, lanes], halves), kv)
                out = jnp.dot(e.astype(BF16), vv, preferred_element_type=F32) * rinv
                o_ref[:, lanes] = jnp.where(halves[0], out[0:TM], out[TM:2 * TM]).astype(BF16)

        @pl.when(t == 0)
        def _():
            run(CTX)

        @pl.when(t != 0)
        def _():
            run(RE)

    qs = pl.BlockSpec((TM, QB_PER_KV * LANE), lambda b, h, t: (b * TPE + t, h))
    ks = pl.BlockSpec((RE, LANE), lambda b, h, t: (b, h))
    return _pcall(body, name=name, out_shape=jax.ShapeDtypeStruct((NROW, AW), BF16), grid=(BL, NKV, TPE),
                  in_specs=[qs, ks, ks], out_specs=qs, sem=("parallel",) * 3, comm=comm)(q, k, v)


def attn_bwd(q, k, v, o, do, name, comm=None):
    def body(q_ref, k_ref, v_ref, o_ref, do_ref, dq_ref, dk_ref, dv_ref):
        t = pl.program_id(2)
        halves = _lane_halves()

        @pl.when(t == 0)
        def _():
            dk_ref[...] = jnp.zeros_like(dk_ref)
            dv_ref[...] = jnp.zeros_like(dv_ref)

        def run(nk):
            kv, vv = k_ref[0:nk, :], v_ref[0:nk, :]
            dks, dvs = [], []
            for j in range(QB_PER_KV):
                lanes = slice(j * LANE, (j + 1) * LANE)
                q2, do2 = _stack_heads(q_ref[:, lanes], halves), _stack_heads(do_ref[:, lanes], halves)
                ov = o_ref[:, lanes].astype(F32)
                delta = jnp.sum(do2.astype(F32) * jnp.concatenate([ov, ov], axis=0), axis=-1, keepdims=True)
                e, rinv = _softmax_parts(q2, kv)
                p = e * rinv
                dvs.append(lax.dot_general(p.astype(BF16), do2, (((0,), (0,)), ((), ())), preferred_element_type=F32))
                dp = lax.dot_general(do2, vv, (((1,), (1,)), ((), ())), preferred_element_type=F32)
                ds = (p * (dp - delta)).astype(BF16)
                dq = jnp.dot(ds, kv, preferred_element_type=F32)
                dks.append(lax.dot_general(ds, q2, (((0,), (0,)), ((), ())), preferred_element_type=F32))
                dq_ref[:, lanes] = jnp.where(halves[0], dq[0:TM], dq[TM:2 * TM])
            dv_ref[0:nk, :] += functools.reduce(jnp.add, dvs)
            dk_ref[0:nk, :] += functools.reduce(jnp.add, dks)

        @pl.when(t == 0)
        def _():
            run(CTX)

        @pl.when(t != 0)
        def _():
            run(RE)

    qs = pl.BlockSpec((TM, QB_PER_KV * LANE), lambda b, h, t: (b * TPE + t, h))
    ks = pl.BlockSpec((RE, LANE), lambda b, h, t: (b, h))
    return _pcall(body, name=name,
                  out_shape=(jax.ShapeDtypeStruct((NROW, AW), F32), jax.ShapeDtypeStruct((NROW, 2 * KVW), F32),
                             jax.ShapeDtypeStruct((NROW, 2 * KVW), F32)),
                  grid=(BL, NKV, TPE), in_specs=[qs, ks, ks, qs, qs], out_specs=(qs, ks, ks),
                  sem=("parallel", "parallel", "arbitrary"), comm=comm)(q, k, v, o, do)


CONV_SEGS = ((0, CTX), (CTX, SEQ))


def _p_block(col0):
    return pl.BlockSpec((RE, LANE), lambda cb, b: (b, col0 // LANE + cb))


def _conv_io(width):
    return pl.BlockSpec((RE, LANE), lambda cb, b: (b, cb))


def _taps(n):
    return pl.BlockSpec((n, LANE), lambda cb, b: (0, cb))


def _fill_pad(pad_ref, length, values):
    pad_ref[0:PADR, :] = jnp.zeros((PADR, LANE), F32)
    pad_ref[PADR + length:2 * PADR + length, :] = jnp.zeros((PADR, LANE), F32)
    pad_ref[PADR:PADR + length, :] = values


def _conv_chunk(pad_ref, w_ref, ntap, c0, first_row):
    acc = jnp.zeros((CONV_CH, LANE), F32)
    for kk in range(ntap):
        r0 = c0 + first_row(kk)
        acc += w_ref[kk:kk + 1, :] * pad_ref[r0:r0 + CONV_CH, :]
    return acc


def conv_fwd(p_main, wdw, bdw, w3, name, comm=None):
    def body(a_ref, g_ref, bg_ref, cg_ref, xs_ref, w_ref, b_ref, w3_ref, yc_ref, z_ref, pad_ref):
        for off, length in CONV_SEGS:
            rows = slice(off, off + length)
            _fill_pad(pad_ref, length, a_ref[rows, :].astype(F32) * _sigmoid(g_ref[rows, :].astype(F32)))
            for c0 in range(0, length, CONV_CH):
                acc = _conv_chunk(pad_ref, w_ref, CONF_K, c0, lambda kk: PADR + kk - CONF_K // 2)
                yc_ref[off + c0:off + c0 + CONV_CH, :] = acc + b_ref[...]
            pad_ref[PADR:PADR + length, :] = cg_ref[rows, :].astype(F32) * xs_ref[rows, :].astype(F32)
            for c0 in range(0, length, CONV_CH):
                acc = _conv_chunk(pad_ref, w3_ref, SC_K, c0, lambda kk: PADR + kk - SC_K // 2)
                z_ref[off + c0:off + c0 + CONV_CH, :] = (bg_ref[off + c0:off + c0 + CONV_CH, :] * acc).astype(BF16)

    return _pcall(body, name=name,
                  out_shape=(jax.ShapeDtypeStruct((NROW, CW), F32), jax.ShapeDtypeStruct((NROW, CW), BF16)),
                  grid=(CB, BL),
                  in_specs=[_p_block(OFF_CONF), _p_block(OFF_CONF + CW), _p_block(OFF_SC), _p_block(OFF_SC + CW),
                            _p_block(OFF_SC + 2 * CW), _taps(CONF_K), _taps(1), _taps(SC_K)],
                  out_specs=(_conv_io(CW), _conv_io(CW)),
                  scratch=[pltpu.VMEM((SEQ + 2 * PADR, LANE), F32)],
                  sem=("parallel", "parallel"), comm=comm)(p_main, p_main, p_main, p_main, p_main, wdw, bdw, w3)


def _tap_grad(pad_ref, d_ref, off, length, first_row):
    acc = jnp.zeros((8, LANE), F32)
    for c0 in range(0, length, CONV_CH):
        prod = d_ref[off + c0:off + c0 + CONV_CH, :] * pad_ref[c0 + first_row:c0 + first_row + CONV_CH, :]
        acc += jnp.sum(prod.reshape(CONV_CH // 8, 8, LANE), axis=0)
    return jnp.sum(acc, axis=0, keepdims=True)


def conv_bwd(p_main, wdw, w3, dyc, dz, name):
    def body(a_ref, g_ref, bg_ref, cg_ref, xs_ref, w_ref, w3_ref, dyc_ref, dz_ref,
             da_ref, dg_ref, dbg_ref, dcg_ref, dxs_ref, dw_ref, db_ref, dw3_ref, pad_x, pad_d, dconv_ref):
        b = pl.program_id(1)

        @pl.when(b == 0)
        def _():
            dw_ref[...] = jnp.zeros_like(dw_ref)
            db_ref[...] = jnp.zeros_like(db_ref)
            dw3_ref[...] = jnp.zeros_like(dw3_ref)

        db_ref[...] += jnp.sum(dyc_ref[...], axis=0, keepdims=True)
        for off, length in CONV_SEGS:
            rows = slice(off, off + length)
            _fill_pad(pad_x, length, a_ref[rows, :].astype(F32) * _sigmoid(g_ref[rows, :].astype(F32)))
            _fill_pad(pad_d, length, dyc_ref[rows, :])
            for kk in range(CONF_K):
                dw_ref[kk:kk + 1, :] += _tap_grad(pad_x, dyc_ref, off, length, PADR + kk - CONF_K // 2)
            for c0 in range(0, length, CONV_CH):
                dh = _conv_chunk(pad_d, w_ref, CONF_K, c0, lambda kk: PADR + CONF_K // 2 - kk)
                ch = slice(off + c0, off + c0 + CONV_CH)
                sg = _sigmoid(g_ref[ch, :].astype(F32))
                da_ref[ch, :] = (dh * sg).astype(BF16)
                dg_ref[ch, :] = (dh * a_ref[ch, :] * sg * (1.0 - sg)).astype(BF16)
            pad_x[PADR:PADR + length, :] = cg_ref[rows, :].astype(F32) * xs_ref[rows, :].astype(F32)
            dconv_ref[rows, :] = dz_ref[rows, :] * bg_ref[rows, :]
            pad_d[PADR:PADR + length, :] = dconv_ref[rows, :]
            for kk in range(SC_K):
                dw3_ref[kk:kk + 1, :] += _tap_grad(pad_x, dconv_ref, off, length, PADR + kk - SC_K // 2)
            for c0 in range(0, length, CONV_CH):
                ch = slice(off + c0, off + c0 + CONV_CH)
                c3 = _conv_chunk(pad_x, w3_ref, SC_K, c0, lambda kk: PADR + kk - SC_K // 2)
                dbg_ref[ch, :] = (dz_ref[ch, :] * c3).astype(BF16)
                dcx = _conv_chunk(pad_d, w3_ref, SC_K, c0, lambda kk: PADR + SC_K // 2 - kk)
                dcg_ref[ch, :] = (dcx * xs_ref[ch, :]).astype(BF16)
                dxs_ref[ch, :] = (dcx * cg_ref[ch, :]).astype(BF16)

    slab = jax.ShapeDtypeStruct((NROW, CW), BF16)
    return _pcall(body, name=name,
                  out_shape=(slab,) * 5 + (jax.ShapeDtypeStruct((CONF_K, CW), F32), jax.ShapeDtypeStruct((1, CW), F32),
                                           jax.ShapeDtypeStruct((SC_K, CW), F32)),
                  grid=(CB, BL),
                  in_specs=[_p_block(OFF_CONF), _p_block(OFF_CONF + CW), _p_block(OFF_SC), _p_block(OFF_SC + CW),
                            _p_block(OFF_SC + 2 * CW), _taps(CONF_K), _taps(SC_K), _conv_io(CW), _conv_io(CW)],
                  out_specs=(_conv_io(CW),) * 5 + (_taps(CONF_K), _taps(1), _taps(SC_K)),
                  scratch=[pltpu.VMEM((SEQ + 2 * PADR, LANE), F32), pltpu.VMEM((SEQ + 2 * PADR, LANE), F32),
                           pltpu.VMEM((RE, LANE), F32)],
                  sem=("parallel", "arbitrary"))(p_main, p_main, p_main, p_main, p_main, wdw, w3, dyc, dz)


def silu_rows(x, name):
    def body(x_ref, o_ref):
        o_ref[...] = x_ref[...] * _sigmoid(x_ref[...])

    return _pcall(body, name=name, out_shape=jax.ShapeDtypeStruct(x.shape, F32))(x)


def silu_rows_bwd(x, dcs, name):
    def body(x_ref, d_ref, o_ref):
        x = x_ref[...]
        sg = _sigmoid(x)
        tot = d_ref[0]
        for i in range(1, DEPTH):
            tot += d_ref[i]
        o_ref[...] = tot * (sg * (1.0 + x * (1.0 - sg)))

    return _pcall(body, name=name, out_shape=jax.ShapeDtypeStruct(x.shape, F32))(x, dcs)


def dmod_assemble(parts, name):
    def body(p_ref, dm_ref, db_ref):
        row = lax.broadcasted_iota(jnp.int32, (8, NMOD * D), 0)
        dm = jnp.zeros((8, NMOD * D), F32)
        db = jnp.zeros((1, NMOD * D), F32)
        for s in range(2 * BL):
            target = BL if s % 2 == 0 else s // 2
            part = p_ref[s:s + 1, :]
            dm += jnp.where(row == target, part, 0.0)
            db += part
        dm_ref[...] = dm
        db_ref[...] = db

    return _pcall(body, name=name, out_shape=(jax.ShapeDtypeStruct((8, NMOD * D), F32),
                                              jax.ShapeDtypeStruct((1, NMOD * D), F32)))(parts)


def sum_leading(x, name):
    n = x.shape[0]
    tr = _pick(x.shape[1], (256, 32, 8))

    def body(x_ref, o_ref):
        tot = x_ref[0].astype(F32)
        for i in range(1, n):
            tot += x_ref[i].astype(F32)
        o_ref[...] = tot

    return _pcall(body, name=name, out_shape=jax.ShapeDtypeStruct(x.shape[1:], F32), grid=(x.shape[1] // tr,),
                  in_specs=[pl.BlockSpec((n, tr, x.shape[2]), lambda i: (0, i, 0))],
                  out_specs=pl.BlockSpec((tr, x.shape[2]), lambda i: (i, 0)), sem=("parallel",))(x)


SLAB_ROWS = (256, 176, 128, 64, 8)


def _prefetch_call(body, name, out_shape, grid, in_specs, out_specs, sem, scalars, *args):
    spec = pltpu.PrefetchScalarGridSpec(num_scalar_prefetch=len(scalars), grid=grid, in_specs=in_specs,
                                        out_specs=out_specs)
    return pl.pallas_call(body, name=name, out_shape=out_shape, grid_spec=spec,
                          compiler_params=pltpu.CompilerParams(dimension_semantics=sem,
                                                               vmem_limit_bytes=VMEM_LIMIT))(*scalars, *args)


def cast_layers(w, chip, name):
    depth, r, c = w.shape
    tr = _pick(r, SLAB_ROWS)

    def body(s_ref, w_ref, *o_refs):
        for l in range(depth):
            o_refs[l][...] = w_ref[l].astype(BF16)

    slab = pl.BlockSpec((None, tr, c), lambda i, s: (s[0], i, 0))
    return _prefetch_call(body, name, (jax.ShapeDtypeStruct((NCHIP, r, c), BF16),) * depth, (r // tr,),
                          [pl.BlockSpec((depth, tr, c), lambda i, s: (0, i, 0))], (slab,) * depth,
                          ("parallel",), (chip,), w)


def rs_add(g, other, core, chip, name):
    _, r, c = g.shape
    rh = r // 2
    tr = _pick(rh, SLAB_ROWS)
    nblk = rh // tr

    def body(core_ref, chip_ref, g_ref, o_ref, send_ref, arr_ref):
        k = pl.program_id(1)
        tot = (g_ref[...].astype(F32) + o_ref[...].astype(F32)).astype(BF16)
        send_ref[...] = tot

        @pl.when(k == chip_ref[0])
        def _():
            arr_ref[...] = tot

    blk = (None, tr, c)
    return _prefetch_call(
        body, name, (jax.ShapeDtypeStruct(other.shape, BF16), jax.ShapeDtypeStruct(g.shape, BF16)), (nblk, NCHIP),
        [pl.BlockSpec(blk, lambda i, k, cr, ch: (k, cr[0] * nblk + i, 0)), pl.BlockSpec(blk, lambda i, k, cr, ch: (k, i, 0))],
        (pl.BlockSpec(blk, lambda i, k, cr, ch: (k, i, 0)),
         pl.BlockSpec(blk, lambda i, k, cr, ch: (ch[0], cr[0] * nblk + i, 0))),
        ("parallel", "arbitrary"), (core, chip), g, other)


def adamw_layers(w, arrs, m, v, first, prev, name, comm=None):
    depth, r, c = w.shape
    tr = _pick(r, (128, 176, 64, 8))
    nblk = r // tr
    nl = len(arrs)
    c1 = 1.0 / (1.0 - ADAM_B1 ** ADAM_STEP)
    c2 = 1.0 / (1.0 - ADAM_B2 ** ADAM_STEP)

    def body(w_ref, m_ref, v_ref, *rest):
        a_refs = rest[:nl]
        g_ref, d_ref, mo_ref, vo_ref = rest[nl + 4:nl + 8]
        li = pl.program_id(0)
        gv = None
        for idx, a_ref in enumerate(a_refs):
            tot = a_ref[0].astype(F32)
            for k in range(1, NCHIP):
                tot += a_ref[k].astype(F32)
            gv = tot if gv is None else jnp.where(li == idx, tot, gv)
        mn = ADAM_B1 * m_ref[...] + (1.0 - ADAM_B1) * gv
        vn = ADAM_B2 * v_ref[...] + (1.0 - ADAM_B2) * (gv * gv)
        g_ref[...] = gv
        d_ref[...] = -ADAM_LR * ((mn * c1) / (jnp.sqrt(vn * c2) + ADAM_EPS) + ADAM_WD * w_ref[...])
        mo_ref[...] = mn
        vo_ref[...] = vn

    def arr_spec(idx):
        return pl.BlockSpec((NCHIP, tr, c),
                            lambda li, i: (0, jnp.where(li == idx, i, jnp.where(li < idx, 0, nblk - 1)), 0))

    spec = pl.BlockSpec((None, tr, c), lambda li, i: (first + li, i, 0))
    sds = jax.ShapeDtypeStruct(w.shape, F32)
    return _pcall(body, name=name, out_shape=(sds,) * 4, grid=(nl, nblk),
                  in_specs=[spec, spec, spec] + [arr_spec(idx) for idx in range(nl)] + [ANY] * 4,
                  out_specs=(spec,) * 4, aliases={3 + nl + i: i for i in range(4)},
                  sem=("arbitrary", "arbitrary"), comm=comm)(w, m, v, *arrs, *prev)


def adamw(w, g, m, v, name):
    rows, cols = w.shape
    tr = _pick(rows, (256, 248, 128, 8))
    c1 = 1.0 / (1.0 - ADAM_B1 ** ADAM_STEP)
    c2 = 1.0 / (1.0 - ADAM_B2 ** ADAM_STEP)

    def body(w_ref, g_ref, m_ref, v_ref, d_ref, mo_ref, vo_ref):
        gv = g_ref[...]
        mn = ADAM_B1 * m_ref[...] + (1.0 - ADAM_B1) * gv
        vn = ADAM_B2 * v_ref[...] + (1.0 - ADAM_B2) * (gv * gv)
        d_ref[...] = -ADAM_LR * ((mn * c1) / (jnp.sqrt(vn * c2) + ADAM_EPS) + ADAM_WD * w_ref[...])
        mo_ref[...] = mn
        vo_ref[...] = vn

    spec = pl.BlockSpec((tr, cols), lambda i: (i, 0))
    sds = jax.ShapeDtypeStruct((rows, cols), F32)
    return _pcall(body, name=name, out_shape=(sds, sds, sds), grid=(rows // tr,), in_specs=[spec] * 4,
                  out_specs=(spec, spec, spec), sem=("parallel",))(w, g, m, v)


def _place():
    return lax.axis_index("x"), lax.axis_index("y"), lax.axis_index("c")


def _other_chips(x, y):
    return [(1 - x, y), (x, 1 - y), (1 - x, 1 - y)]


def _comm_call(body, name, out_shape, n_in, nsem):
    return pl.pallas_call(body, name=name, out_shape=out_shape, in_specs=[ANY] * n_in,
                          out_specs=jax.tree.map(lambda _: ANY, out_shape),
                          scratch_shapes=[pltpu.SemaphoreType.DMA((nsem,)), pltpu.SemaphoreType.DMA((nsem,)),
                                          pltpu.SemaphoreType.DMA])


def all_gather8(block, name):
    def body(x_ref, out_ref, send_sems, recv_sems, local_sem):
        x, y, c = _place()
        me, sibling = (x, y, c), (x, y, 1 - c)
        chips = _other_chips(x, y)

        def slot(px, py, pc):
            return out_ref.at[4 * px + 2 * py + pc]

        def copy(k, blk, to, src=None):
            return pltpu.make_async_remote_copy(src_ref=slot(*blk) if src is None else src, dst_ref=slot(*blk),
                                                send_sem=send_sems.at[k], recv_sem=recv_sems.at[k],
                                                device_id=to, device_id_type=MESH)

        mine = pltpu.make_async_copy(x_ref, slot(*me), local_sem)
        mine.start()
        first = [copy(0, me, sibling, src=x_ref)]
        first += [copy(1 + j, me, (*chip, c), src=x_ref) for j, chip in enumerate(chips)]
        for cp in first:
            cp.start()
        passed = [copy(4 + j, (*chip, c), sibling) for j, chip in enumerate(chips)]
        for j, chip in enumerate(chips):
            copy(1 + j, (*chip, c), me).wait_recv()
            passed[j].start()
        copy(0, sibling, me).wait_recv()
        for j, chip in enumerate(chips):
            copy(4 + j, (*chip, 1 - c), me).wait_recv()
        for cp in first + passed:
            cp.wait_send()
        mine.wait()

    return _comm_call(body, name, jax.ShapeDtypeStruct((8,) + block.shape, block.dtype), 1, 7)(block)


def _remote(src, dst, send_sems, recv_sems, k, to):
    return pltpu.make_async_remote_copy(src_ref=src, dst_ref=dst, send_sem=send_sems.at[k], recv_sem=recv_sems.at[k],
                                        device_id=to, device_id_type=MESH)


def _half(ref, slot, core):
    rh = ref.shape[1] // 2
    return ref.at[slot, pl.ds(core * rh, rh)]


def _all_slots_half(ref, core):
    rh = ref.shape[1] // 2
    return ref.at[:, pl.ds(core * rh, rh)]


def gather_ici(bufs):
    def program(ro, rw, new, ss, rs):
        x, y, c = _place()
        own = 2 * x + y
        starts, arrivals = [], []
        for w, ref in enumerate(rw):
            for j, chip in enumerate(_other_chips(x, y)):
                starts.append(_remote(_half(ref, own, c), _half(ref, own, c), ss, rs, 3 * w + j, (*chip, c)))
                arrivals.append(_remote(_half(ref, own, c), _half(ref, 2 * chip[0] + chip[1], c), ss, rs, 3 * w + j,
                                        (*chip, c)))
        return starts, arrivals

    return CommSpec((), tuple(bufs), (), 3 * len(bufs), program)


def gather_d2d(bufs):
    def program(ro, rw, new, ss, rs):
        x, y, c = _place()
        starts, arrivals = [], []
        for w, ref in enumerate(rw):
            for j, chip in enumerate(_other_chips(x, y)):
                slot = 2 * chip[0] + chip[1]
                starts.append(_remote(_half(ref, slot, c), _half(ref, slot, c), ss, rs, 3 * w + j, (x, y, 1 - c)))
                arrivals.append(_remote(_half(ref, slot, c), _half(ref, slot, 1 - c), ss, rs, 3 * w + j, (x, y, 1 - c)))
        return starts, arrivals

    return CommSpec((), tuple(bufs), (), 3 * len(bufs), program)


def rs_swap(grads):
    def program(ro, rw, new, ss, rs):
        x, y, c = _place()
        copies = [_remote(_all_slots_half(g, 1 - c), new[w], ss, rs, w, (x, y, 1 - c)) for w, g in enumerate(ro)]
        return copies, copies

    shapes = tuple(jax.ShapeDtypeStruct((NCHIP, g.shape[1] // 2, g.shape[2]), g.dtype) for g in grads)
    return CommSpec(tuple(grads), (), shapes, len(grads), program)


def rs_ici(sends, arrs):
    def program(ro, rw, new, ss, rs):
        x, y, c = _place()
        own = 2 * x + y
        starts, arrivals = [], []
        for w, (snd, arr) in enumerate(zip(ro, rw)):
            for j, chip in enumerate(_other_chips(x, y)):
                slot = 2 * chip[0] + chip[1]
                starts.append(_remote(snd.at[slot], _half(arr, own, c), ss, rs, 3 * w + j, (*chip, c)))
                arrivals.append(_remote(snd.at[slot], _half(arr, slot, c), ss, rs, 3 * w + j, (*chip, c)))
        return starts, arrivals

    return CommSpec(tuple(sends), tuple(arrs), (), 3 * len(sends), program)


def rs_d2d(arrs):
    def program(ro, rw, new, ss, rs):
        x, y, c = _place()
        starts = [_remote(_all_slots_half(a, c), _all_slots_half(a, c), ss, rs, w, (x, y, 1 - c)) for w, a in enumerate(rw)]
        arrivals = [_remote(_all_slots_half(a, c), _all_slots_half(a, 1 - c), ss, rs, w, (x, y, 1 - c))
                    for w, a in enumerate(rw)]
        return starts, arrivals

    return CommSpec((), tuple(arrs), (), len(arrs), program)


PACK_COLS = 1024
MATMUL_W = ("w_ada", "w_in", "w_attn_o", "w_conf_out", "w_sc_out", "w_mix_out", "w_ffn_in", "w_ffn_out")
ROW_SPLIT = ("w_mix_out", "w_ffn_out")
CONV_W = ("conf_dw_w", "sc_dw_w")
SMALL = ("c_ctx", "b_ada", "q_norm", "k_norm", "conf_dw_b", "conf_ln_g", "conf_ln_b", "conf_dw_w", "sc_dw_w")


def _pack_rows(arrays, row_multiple):
    flat = jnp.concatenate([a.reshape(-1) for a in arrays])
    rows = -(-flat.shape[0] // PACK_COLS)
    rows = -(-rows // row_multiple) * row_multiple
    flat = jnp.pad(flat, (0, rows * PACK_COLS - flat.shape[0]))
    return flat.reshape(rows, PACK_COLS)


def _unpack(flat2d, shapes):
    flat = flat2d.reshape(-1)
    out, pos = [], 0
    for shp in shapes:
        n = 1
        for s in shp:
            n *= s
        out.append(flat[pos:pos + n].reshape(shp))
        pos += n
    return out


def _cols_joined(stacked_layer):
    nchip, r, c = stacked_layer.shape
    return jnp.transpose(stacked_layer, (1, 0, 2)).reshape(r, nchip * c)


def _cols_split(full):
    r, cols = full.shape
    return jnp.transpose(full.reshape(r, NCHIP, cols // NCHIP), (1, 0, 2))


def _rope_tables():
    rows = SEQ // GRID_W
    r_ids = jnp.repeat(jnp.arange(rows, dtype=F32), GRID_W)
    c_ids = jnp.tile(jnp.arange(GRID_W, dtype=F32), rows)
    freqs = ROPE_THETA ** (-jnp.arange(0, HD // 2, 2, dtype=F32) / (HD // 2))
    ang_r, ang_c = r_ids[:, None] * freqs, c_ids[:, None] * freqs
    cos_h = jnp.concatenate([jnp.cos(ang_r), jnp.cos(ang_r), jnp.cos(ang_c), jnp.cos(ang_c)], axis=1)
    sin_h = jnp.concatenate([-jnp.sin(ang_r), jnp.sin(ang_r), -jnp.sin(ang_c), jnp.sin(ang_c)], axis=1)
    cos_t = jnp.concatenate([jnp.ones((CTX, HD), F32), cos_h], axis=0)
    sin_t = jnp.concatenate([jnp.zeros((CTX, HD), F32), sin_h], axis=0)
    return jnp.tile(cos_t, (1, LANE // HD)), jnp.tile(sin_t, (1, LANE // HD))


def _group_matrix():
    gid = jnp.arange(LANE) // HD
    return jnp.where(gid[:, None] == gid[None, :], 1.0 / HD, 0.0).astype(F32)


N_FIRST = 2
assert MATMUL_W[:N_FIRST] == ("w_ada", "w_in")


def _first_weights(bufs, small, i):
    wi = _cols_joined(bufs[1])
    return dict(
        w_ada=(bufs[0], "cols"), wi_main=(wi[:, :OFF_GATE], "mat"), wi_gate=(wi[:, OFF_GATE:], "mat"),
        conf_dw_w=small["conf_dw_w"][i], sc_dw_w=small["sc_dw_w"][i], conf_dw_b=small["conf_dw_b"][i][None],
        conf_ln_g=small["conf_ln_g"][i][None], conf_ln_b=small["conf_ln_b"][i][None],
        gq=jnp.tile(small["q_norm"][i], LANE // HD)[None], gk=jnp.tile(small["k_norm"][i], LANE // HD)[None])


def _second_weights(bufs):
    b = dict(zip(MATMUL_W[N_FIRST:], bufs))

    def rows_joined(a):
        return a.reshape(a.shape[0] * a.shape[1], a.shape[2])

    return dict(
        w_attn_o=(_cols_joined(b["w_attn_o"]), "mat"), w_conf_out=(_cols_joined(b["w_conf_out"]), "mat"),
        w_sc_out=(_cols_joined(b["w_sc_out"]), "mat"), w_ffn_in=(b["w_ffn_in"], "cols"),
        w_mix_out=(rows_joined(b["w_mix_out"]), "mat"), w_ffn_out=(rows_joined(b["w_ffn_out"]), "mat"))


def _layer_fwd(i, xs, h, mods, w, tabs, second_bufs, next_first, next_layer, distributed):
    cos_t, sin_t, g_mat = tabs
    n = f"l{i}_"
    w = dict(w)
    sv = {"x_in": xs, "mods": mods, "h": h, "w": w}
    n_second = len(second_bufs)
    sv["p_main"] = mm_nn(sv["h"], w["wi_main"], name=n + "p_main")
    sv["q"], sv["k"], sv["v"] = qkv_fwd(sv["p_main"], cos_t, sin_t, g_mat, w["gq"], w["gk"], n + "qkv")
    if distributed:
        riding = list(second_bufs) + list(next_first or [])
        sv["o"], riding, _ = attn_fwd(sv["q"], sv["k"], sv["v"], n + "attn", comm=gather_ici(riding))
        second_bufs, next_first = riding[:n_second], (riding[n_second:] or None)
        (sv["yc"], sv["z"]), second_bufs, _ = conv_fwd(sv["p_main"], w["conf_dw_w"], w["conf_dw_b"], w["sc_dw_w"],
                                                       n + "conv", comm=gather_d2d(second_bufs))
    else:
        sv["o"] = attn_fwd(sv["q"], sv["k"], sv["v"], n + "attn")
        sv["yc"], sv["z"] = conv_fwd(sv["p_main"], w["conf_dw_w"], w["conf_dw_b"], w["sc_dw_w"], n + "conv")
    w.update(_second_weights(second_bufs))
    next_bufs = next_first if distributed else None
    sv["hs"] = ln_silu_fwd(sv["yc"], w["conf_ln_g"], w["conf_ln_b"], n + "ln_silu")
    sv["merged"], sv["gates"], sv["ys"] = gate_mm_fwd(sv["h"], w["wi_gate"][0], sv["o"], sv["hs"], sv["z"],
                                                      w["w_attn_o"][0], w["w_conf_out"][0], w["w_sc_out"][0],
                                                      n + "gate_merge")
    sv["mixed"] = mm_nn(sv["merged"], w["w_mix_out"], name=n + "mix")
    sv["x1"], sv["h2"] = resid_norm_fwd(xs, sv["mixed"], mods, 2, mods, 3, 4, n + "resid1_norm2")
    if next_bufs is None:
        sv["f"], sv["u2"] = ffn_in_swiglu(sv["h2"], w["w_ffn_in"][0], n + "ffn_in")
    else:
        (sv["f"], sv["u2"]), next_bufs, _ = ffn_in_swiglu(sv["h2"], w["w_ffn_in"][0], n + "ffn_in",
                                                          comm=gather_d2d(next_bufs))
    sv["of"] = mm_nn(sv["f"], w["w_ffn_out"], name=n + "ffn_out")
    if next_layer is None:
        return gate_resid_fwd(sv["x1"], sv["of"], mods, 5, n + "resid2"), None, sv
    w_next, mods_next = next_layer(next_bufs)
    x2, h_next = resid_norm_fwd(sv["x1"], sv["of"], mods, 5, mods_next, 0, 1, n + "resid2_norm1")
    return x2, (h_next, w_next, mods_next), sv


def _layer_bwd(i, dx2, dof, dm5, sv, tabs, cs, pending, ids, below):
    cos_t, sin_t, g_mat = tabs
    n = f"l{i}b_"
    mods, w = sv["mods"], sv["w"]
    g = {}
    sends, arrs = [], []
    du = d_f_swiglu(dof, w["w_ffn_out"][0], sv["u2"], n + "d_f")
    g["w_ffn_out"] = mm_tn(sv["f"], dof, out_dtype=BF16, name=n + "dw_ffn_out").reshape(NCHIP, FH // NCHIP, D)
    if pending is None:
        dh2 = mm_nt(du, w["w_ffn_in"], name=n + "d_h2")
    else:
        dh2, _, swapped = mm_nt(du, w["w_ffn_in"], name=n + "d_h2", comm=rs_swap(pending))
        for k, g_, s_ in zip(MATMUL_W[:N_FIRST], pending, swapped):
            send, arr = rs_add(g_, s_, ids[0], ids[1], f"{n}rs_add_above_{k}")
            sends.append(send)
            arrs.append(arr)
    g["w_ffn_in"] = mm_tn(sv["h2"], du, cols=True, out_dtype=BF16, name=n + "dw_ffn_in")
    dx1, dm34, dmixed, dm2 = norm_resid_bwd(sv["x1"], mods, dh2, dx2, 4, sv["mixed"], mods, 2, n + "norm2_resid1")
    dya, dyb, dys, dp_gate = d_merged_gate(dmixed, w["w_mix_out"][0], sv["gates"], sv["ys"], n + "d_merged")
    g["w_mix_out"] = mm_tn(sv["merged"], dmixed, out_dtype=BF16, name=n + "dw_mix").reshape(NCHIP, D // NCHIP, D)
    do = mm_nt(dya, w["w_attn_o"], out_dtype=BF16, name=n + "d_o")
    g["w_attn_o"] = _cols_split(mm_tn(sv["o"], dya, out_dtype=BF16, name=n + "dw_attn_o"))
    dhs = mm_nt(dyb, w["w_conf_out"], name=n + "d_hs")
    g["w_conf_out"] = _cols_split(mm_tn(sv["hs"], dyb, out_dtype=BF16, name=n + "dw_conf_out"))
    dz = mm_nt(dys, w["w_sc_out"], name=n + "d_z")
    g["w_sc_out"] = _cols_split(mm_tn(sv["z"], dys, out_dtype=BF16, name=n + "dw_sc_out"))
    done = None
    if ids is None:
        dyc, g["conf_ln_g"], g["conf_ln_b"] = ln_silu_bwd(sv["yc"], w["conf_ln_g"], w["conf_ln_b"], dhs, n + "ln_silu")
    else:
        own = [g[k] for k in MATMUL_W[N_FIRST:]]
        (dyc, g["conf_ln_g"], g["conf_ln_b"]), _, swapped = ln_silu_bwd(sv["yc"], w["conf_ln_g"], w["conf_ln_b"], dhs,
                                                                        n + "ln_silu", comm=rs_swap(own))
        for k, g_, s_ in zip(MATMUL_W[N_FIRST:], own, swapped):
            send, arr = rs_add(g_, s_, ids[0], ids[1], f"{n}rs_add_{k}")
            sends.append(send)
            arrs.append(arr)
    da, dg, dbg, dcg, dxs, g["conf_dw_w"], g["conf_dw_b"], g["sc_dw_w"] = conv_bwd(
        sv["p_main"], w["conf_dw_w"], w["sc_dw_w"], dyc, dz, n + "conv")
    if ids is None:
        dq, dk, dv = attn_bwd(sv["q"], sv["k"], sv["v"], sv["o"], do, n + "attn")
        dp_qkv, dgqk = qkv_bwd(sv["p_main"], cos_t, sin_t, g_mat, w["gq"], w["gk"], dq, dk, dv, n + "qkv")
    else:
        (dq, dk, dv), arrs, _ = attn_bwd(sv["q"], sv["k"], sv["v"], sv["o"], do, n + "attn", comm=rs_ici(sends, arrs))
        (dp_qkv, dgqk), done, _ = qkv_bwd(sv["p_main"], cos_t, sin_t, g_mat, w["gq"], w["gk"], dq, dk, dv, n + "qkv",
                                          comm=rs_d2d(arrs))
    dp_main = jnp.concatenate([dp_qkv, da, dg, dbg, dcg, dxs], axis=1)
    dh = mm_nt(dp_main, w["wi_main"], name=n + "d_h_main")
    dh = mm_nt(dp_gate, w["wi_gate"], acc=dh, name=n + "d_h_gate")
    g["w_in"] = _cols_split(jnp.concatenate([mm_tn(sv["h"], dp_main, out_dtype=BF16, name=n + "dw_in_main"),
                                             mm_tn(sv["h"], dp_gate, out_dtype=BF16, name=n + "dw_in_gate")], axis=1))
    if below is None:
        dx_in, dm01 = norm_mod_bwd(sv["x_in"], mods, dh, dx1, 1, n + "norm1")
        dof_below = dm5_below = None
    else:
        dx_in, dm01, dof_below, dm5_below = norm_resid_bwd(sv["x_in"], mods, dh, dx1, 1, below["of"], below["mods"], 5,
                                                           n + "norm1_resid2")
    parts = jnp.concatenate([dm01, dm2, dm34, dm5], axis=2).reshape(2 * BL, NMOD * D)
    dmod, g["b_ada"] = dmod_assemble(parts, n + "dmod")
    g["w_ada"] = mm_tn(cs, dmod, cols=True, out_dtype=BF16, name=n + "dw_ada")
    g["dcs"] = mm_nt(dmod, w["w_ada"], name=n + "d_cs")
    g["q_norm"] = dgqk[0, :AW].reshape(NQ, HD).sum(axis=0)
    g["k_norm"] = dgqk[0, OFF_K:OFF_K + KVW].reshape(NKV, HD).sum(axis=0)
    return dx_in, dof_below, dm5_below, g, done


def kernel(x, c, ctx, c_ctx, w_ada, b_ada, w_in, q_norm, k_norm, w_attn_o, conf_dw_w, conf_dw_b, conf_ln_g, conf_ln_b, w_conf_out, sc_dw_w, w_sc_out, w_mix_out, w_ffn_in, w_ffn_out, loss_target, m_c_ctx, m_w_ada, m_b_ada, m_w_in, m_q_norm, m_k_norm, m_w_attn_o, m_conf_dw_w, m_conf_dw_b, m_conf_ln_g, m_conf_ln_b, m_w_conf_out, m_sc_dw_w, m_w_sc_out, m_w_mix_out, m_w_ffn_in, m_w_ffn_out, v_c_ctx, v_w_ada, v_b_ada, v_w_in, v_q_norm, v_k_norm, v_w_attn_o, v_conf_dw_w, v_conf_dw_b, v_conf_ln_g, v_conf_ln_b, v_w_conf_out, v_sc_dw_w, v_w_sc_out, v_w_mix_out, v_w_ffn_in, v_w_ffn_out):
    local = dict(c_ctx=c_ctx, w_ada=w_ada, b_ada=b_ada, w_in=w_in, q_norm=q_norm, k_norm=k_norm, w_attn_o=w_attn_o,
                 conf_dw_w=conf_dw_w, conf_dw_b=conf_dw_b, conf_ln_g=conf_ln_g, conf_ln_b=conf_ln_b,
                 w_conf_out=w_conf_out, sc_dw_w=sc_dw_w, w_sc_out=w_sc_out, w_mix_out=w_mix_out, w_ffn_in=w_ffn_in,
                 w_ffn_out=w_ffn_out)
    mom_m = dict(c_ctx=m_c_ctx, w_ada=m_w_ada, b_ada=m_b_ada, w_in=m_w_in, q_norm=m_q_norm, k_norm=m_k_norm,
                 w_attn_o=m_w_attn_o, conf_dw_w=m_conf_dw_w, conf_dw_b=m_conf_dw_b, conf_ln_g=m_conf_ln_g,
                 conf_ln_b=m_conf_ln_b, w_conf_out=m_w_conf_out, sc_dw_w=m_sc_dw_w, w_sc_out=m_w_sc_out,
                 w_mix_out=m_w_mix_out, w_ffn_in=m_w_ffn_in, w_ffn_out=m_w_ffn_out)
    mom_v = dict(c_ctx=v_c_ctx, w_ada=v_w_ada, b_ada=v_b_ada, w_in=v_w_in, q_norm=v_q_norm, k_norm=v_k_norm,
                 w_attn_o=v_w_attn_o, conf_dw_w=v_conf_dw_w, conf_dw_b=v_conf_dw_b, conf_ln_g=v_conf_ln_g,
                 conf_ln_b=v_conf_ln_b, w_conf_out=v_w_conf_out, sc_dw_w=v_sc_dw_w, w_sc_out=v_w_sc_out,
                 w_mix_out=v_w_mix_out, w_ffn_in=v_w_ffn_in, w_ffn_out=v_w_ffn_out)
    order = ("c_ctx", "w_ada", "b_ada", "w_in", "q_norm", "k_norm", "w_attn_o", "conf_dw_w", "conf_dw_b", "conf_ln_g",
             "conf_ln_b", "w_conf_out", "sc_dw_w", "w_sc_out", "w_mix_out", "w_ffn_in", "w_ffn_out")
    core = lax.axis_index("c").astype(jnp.int32)
    chip = (2 * lax.axis_index("x") + lax.axis_index("y")).astype(jnp.int32)

    own = [cast_layers(local[k], chip.reshape(1), "cast_" + k) for k in MATMUL_W]
    layer_bufs = [[own[w][l] for w in range(len(MATMUL_W))] for l in range(DEPTH)]
    conv_shapes = [local[k].shape for k in CONV_W]
    conv_all = all_gather8(_pack_rows([local[k] for k in CONV_W], 8), "gather_conv_taps")
    per_chip = [_unpack(conv_all[2 * s], conv_shapes) for s in range(NCHIP)]
    small = dict(b_ada=b_ada, q_norm=q_norm, k_norm=k_norm, conf_dw_b=conf_dw_b, conf_ln_g=conf_ln_g, conf_ln_b=conf_ln_b)
    for i, k in enumerate(CONV_W):
        small[k] = jnp.concatenate([per_chip[s][i] for s in range(NCHIP)], axis=2)

    loss_local, grad_x, sums, small_g = local_step(x, c, ctx, c_ctx, layer_bufs, small, loss_target,
                                                   ids=(core.reshape(1), chip.reshape(1)))
    loss = lax.psum(loss_local, ("x", "y", "c"))

    small_shapes = [small_g[k].shape for k in SMALL]
    small_sum = sum_leading(all_gather8(_pack_rows([small_g[k] for k in SMALL], 8), "gather_small_grads"), "small_sum")
    small_g = dict(zip(SMALL, _unpack(small_sum, small_shapes)))
    for k in CONV_W:
        width = local[k].shape[2]
        small_g[k] = lax.dynamic_slice_in_dim(small_g[k], chip * width, width, axis=2)

    grad, delta, new_m, new_v = {}, {}, {}, {}
    for wi, k in enumerate(MATMUL_W):
        outs = [lax.empty(local[k].shape, F32) for _ in range(4)]
        grad[k], delta[k], new_m[k], new_v[k] = adamw_layers(local[k], [sums[l][wi] for l in range(DEPTH)], mom_m[k],
                                                             mom_v[k], 0, outs, "adamw_" + k)
    for k in order:
        if k in MATMUL_W:
            continue
        shp = local[k].shape
        view = (1, shp[0]) if len(shp) == 1 else (-1, shp[-1])
        d_, m_, v_ = adamw(local[k].reshape(view), small_g[k].reshape(view), mom_m[k].reshape(view),
                           mom_v[k].reshape(view), "adamw_" + k)
        grad[k], delta[k], new_m[k], new_v[k] = small_g[k], d_.reshape(shp), m_.reshape(shp), v_.reshape(shp)
    return (loss, grad_x, *[grad[k] for k in order], *[delta[k] for k in order], *[new_m[k] for k in order],
            *[new_v[k] for k in order])


def local_step(x, c, ctx, c_ctx, layer_bufs, small, loss_target, ids=None):
    tabs = _rope_tables() + (_group_matrix(),)
    distributed = ids is not None
    first_bufs = [list(b[:N_FIRST]) for b in layer_bufs]
    second_bufs = [list(b[N_FIRST:]) for b in layer_bufs]
    if distributed:
        first_bufs[0], _ = comm_only("gather0_ici", gather_ici(first_bufs[0]))
        first_bufs[0], _ = comm_only("gather0_d2d", gather_d2d(first_bufs[0]))

    cin = jnp.concatenate([c, c_ctx[None], jnp.zeros((8 - BL - 1, D), F32)], axis=0)
    cs = silu_rows(cin, "silu_c")
    xs = jnp.concatenate([ctx, x], axis=1).reshape(NROW, D)
    saved = []

    def make_layer(i, bufs):
        w = _first_weights(bufs, small, i)
        return w, mm_nn(cs, w["w_ada"], bias=small["b_ada"][i][None], name=f"l{i}_mod").reshape(8, 1, NMOD * D)

    w, mods = make_layer(0, first_bufs[0])
    h = norm_mod_fwd(xs, mods, 0, 1, "l0_norm1")
    for i in range(DEPTH):
        last = i == DEPTH - 1

        def next_layer(bufs, i=i):
            return make_layer(i + 1, first_bufs[i + 1] if bufs is None else bufs)

        xs, following, sv = _layer_fwd(i, xs, h, mods, w, tabs, second_bufs[i], None if last else first_bufs[i + 1],
                                       None if last else next_layer, distributed)
        saved.append(sv)
        if following is not None:
            h, w, mods = following
    dxs, loss_lanes = loss_fwd_bwd(xs, loss_target.reshape(BL * SEQ, D), "loss")
    loss_local = 0.5 * jnp.sum(loss_lanes) / D

    grads = [None] * DEPTH
    sums = [[None] * len(MATMUL_W) for _ in range(DEPTH)]
    pending = None
    dof, dm5 = gate_resid_bwd(dxs, saved[-1]["of"], saved[-1]["mods"], 5, "top_resid2")
    for i in reversed(range(DEPTH)):
        dxs, dof, dm5, grads[i], done = _layer_bwd(i, dxs, dof, dm5, saved[i], tabs, cs, pending, ids,
                                                   saved[i - 1] if i > 0 else None)
        partial = [grads[i][k] for k in MATMUL_W]
        if distributed:
            if pending is not None:
                sums[i + 1][:N_FIRST] = done[:N_FIRST]
                done = done[N_FIRST:]
            sums[i][N_FIRST:] = done
            pending = partial[:N_FIRST]
        else:
            sums[i] = partial
    if distributed:
        names = MATMUL_W[:N_FIRST]
        _, swapped = comm_only("rs0_swap", rs_swap(pending))
        sends, arrs = zip(*[rs_add(g_, s_, ids[0], ids[1], "rs0_add_" + k) for k, g_, s_ in zip(names, pending, swapped)])
        arrs, _ = comm_only("rs0_ici", rs_ici(sends, arrs))
        sums[0][:N_FIRST], _ = comm_only("rs0_d2d", rs_d2d(arrs))
    grad_x = dxs.reshape(BL, RE, D)[:, CTX:, :]
    dcin = silu_rows_bwd(cin, jnp.stack([grads[i]["dcs"] for i in range(DEPTH)]), "silu_c_bwd")

    def stack(key):
        return jnp.stack([grads[i][key] for i in range(DEPTH)])

    small_g = dict(c_ctx=dcin[BL], b_ada=stack("b_ada").reshape(DEPTH, NMOD * D), q_norm=stack("q_norm"),
                   k_norm=stack("k_norm"), conf_dw_b=stack("conf_dw_b").reshape(DEPTH, CW),
                   conf_ln_g=stack("conf_ln_g").reshape(DEPTH, CW), conf_ln_b=stack("conf_ln_b").reshape(DEPTH, CW),
                   conf_dw_w=stack("conf_dw_w"), sc_dw_w=stack("sc_dw_w"))
    return loss_local, grad_x, sums, small_g
```
